```python
import jax, jax.numpy as jnp
from jax import lax
import numpy as np

D_MODEL = 1024
BATCH = 8
SEQ = 8192
DEPTH = 4

N_A_LAYERS = DEPTH // 2
N_B_LAYERS = DEPTH - N_A_LAYERS
D_FF = 2816
CONV_WIDTH = 3
HEAD_DIM = 64
HEADS_PER_GROUP = 8
DILATED_GROUPS = ((128, 1), (512, 4), (2048, 16))
N_GROUPS = len(DILATED_GROUPS)
N_Q_HEADS = N_GROUPS * HEADS_PER_GROUP
Q_WIDTH = N_Q_HEADS * HEAD_DIM
OUT_WIDTH = HEADS_PER_GROUP * HEAD_DIM
ROPE_DIM = HEAD_DIM // 4
ROPE_THETA = 500000.0
NORM_EPS = 1e-5
FFN_RES_WEIGHT = 0.5
N_MOD = 9

kernel_name = "hybrid_shortconv_dilated_yoco_trunk"


def rms_norm(x, g):
    xf = x.astype(jnp.float32)
    y = xf * lax.rsqrt(jnp.mean(xf * xf, axis=-1, keepdims=True) + NORM_EPS)
    return (y * g.astype(jnp.float32)).astype(x.dtype)


def modulate(h, shift, scale):
    return h * (1 + scale[:, None, :]) + shift[:, None, :]


def swiglu(h, w_in, w_out):
    a, b = jnp.split(h @ w_in, 2, axis=-1)
    return (jax.nn.silu(a) * b) @ w_out


def short_conv_mixer(h, w_in, conv_w, w_out):
    b_gate, c_gate, u = jnp.split(h @ w_in, 3, axis=-1)
    v = c_gate * u
    conv = lax.conv_general_dilated(
        v, conv_w[:, None, :], window_strides=(1,), padding=[(CONV_WIDTH - 1, 0)],
        dimension_numbers=('NWC', 'WIO', 'NWC'), feature_group_count=D_MODEL)
    return (b_gate * conv) @ w_out


def rope_tables(positions):
    inv = ROPE_THETA ** (-jnp.arange(0, ROPE_DIM, 2, dtype=jnp.float32) / ROPE_DIM)
    ang = positions.astype(jnp.float32)[..., None] * inv
    return jnp.cos(ang)[:, :, None, :], jnp.sin(ang)[:, :, None, :]


def apply_partial_rope(t, cos, sin):
    tf = t.astype(jnp.float32)
    r1 = tf[..., :ROPE_DIM // 2]
    r2 = tf[..., ROPE_DIM // 2:ROPE_DIM]
    out = jnp.concatenate([r1 * cos - r2 * sin, r2 * cos + r1 * sin, tf[..., ROPE_DIM:]], axis=-1)
    return out.astype(t.dtype)


def dilated_window_attention(q, k, v, window, dilation):
    bsz, seq, nh, hd = q.shape
    n = window // dilation
    span = n * dilation
    seq_p = -(-seq // span) * span
    pad = seq_p - seq
    m_len = seq_p // dilation
    nb = m_len // n

    def to_blocks(t):
        t = jnp.pad(t, ((0, 0), (0, pad), (0, 0), (0, 0)))
        t = t.reshape(bsz, m_len, dilation, nh, hd).transpose(0, 2, 3, 1, 4)
        return t.reshape(bsz, dilation, nh, nb, n, hd)

    def with_prev(t):
        prev = jnp.pad(t[:, :, :, :-1], ((0, 0), (0, 0), (0, 0), (1, 0), (0, 0), (0, 0)))
        return jnp.concatenate([prev, t], axis=-2)

    qb = to_blocks(q)
    kk = with_prev(to_blocks(k))
    vv = with_prev(to_blocks(v))
    s = jnp.einsum('brhiqe,brhike->brhiqk', qb, kk,
                   preferred_element_type=jnp.float32) * (hd ** -0.5)
    blk = jnp.arange(nb)[:, None, None]
    qi = jnp.arange(n)[None, :, None]
    kj = jnp.arange(2 * n)[None, None, :]
    dist = n + qi - kj
    valid = (dist >= 0) & (dist <= n) & ((blk - 1) * n + kj >= 0)
    s = jnp.where(valid, s, -jnp.inf)
    m = jnp.max(s, axis=-1, keepdims=True)
    p = jnp.exp(s - m)
    den = jnp.sum(p, axis=-1, keepdims=True)
    o = jnp.einsum('brhiqk,brhike->brhiqe', (p / den).astype(v.dtype), vv)
    lse = (m + jnp.log(den))[..., 0]
    o = o.reshape(bsz, dilation, nh, m_len, hd).transpose(0, 3, 1, 2, 4).reshape(bsz, seq_p, nh, hd)
    lse = lse.reshape(bsz, dilation, nh, m_len).transpose(0, 3, 1, 2).reshape(bsz, seq_p, nh)
    return o[:, :seq], lse[:, :seq]


def dilated_attention_mixer(h, w_q, w_o, k_sh, v_sh, cos, sin):
    bsz, seq, _ = h.shape
    q = apply_partial_rope((h @ w_q).reshape(bsz, seq, N_Q_HEADS, HEAD_DIM), cos, sin)
    outs, lses = [], []
    for g, (win, dil) in enumerate(DILATED_GROUPS):
        sl = slice(g * HEADS_PER_GROUP, (g + 1) * HEADS_PER_GROUP)
        o, l = dilated_window_attention(q[:, :, sl], k_sh[:, :, sl], v_sh[:, :, sl], win, dil)
        outs.append(o)
        lses.append(l)
    o = jnp.stack(outs, axis=0).astype(jnp.float32)
    w = jax.nn.softmax(jnp.stack(lses, axis=0), axis=0)
    mixed = jnp.sum(w[..., None] * o, axis=0).astype(h.dtype)
    return mixed.reshape(bsz, seq, OUT_WIDTH) @ w_o


def shared_kv(x, g, shift, scale, w_kv, cos, sin):
    bsz, seq, _ = x.shape
    h = modulate(rms_norm(x, g), shift, scale)
    k, v = jnp.split(h @ w_kv, 2, axis=-1)
    k = apply_partial_rope(k.reshape(bsz, seq, N_Q_HEADS, HEAD_DIM), cos, sin)
    v = v.reshape(bsz, seq, N_Q_HEADS, HEAD_DIM)
    return k, v


def _fwd_setup_inputs(seed: int = 0) -> dict:
    key = jax.random.key(seed)
    ks = jax.random.split(key, 24)
    f32 = jnp.float32
    D, F = D_MODEL, D_FF

    def nrm(k, shape, fan_in, mult=1.0):
        return jax.random.normal(k, shape, f32) * (mult * fan_in ** -0.5)

    x = jax.random.normal(ks[0], (BATCH, SEQ, D), f32)
    c = jax.random.normal(ks[1], (BATCH, D), f32)
    offset = jax.random.randint(ks[2], (BATCH, 1), 0, 1024, dtype=jnp.int32)
    positions = offset + jnp.arange(SEQ, dtype=jnp.int32)[None, :]
    return {
        "x": x,
        "c": c,
        "positions": positions,
        "norm_g": 1.0 + 0.02 * jax.random.normal(ks[3], (DEPTH, 3, D), f32),
        "ada_w": nrm(ks[4], (DEPTH, D, N_MOD * D), D, 0.1),
        "ada_b": 0.01 * jax.random.normal(ks[5], (DEPTH, N_MOD * D), f32),
        "ffn1_w_in": nrm(ks[6], (DEPTH, D, 2 * F), D),
        "ffn1_w_out": nrm(ks[7], (DEPTH, F, D), F),
        "ffn2_w_in": nrm(ks[8], (DEPTH, D, 2 * F), D),
        "ffn2_w_out": nrm(ks[9], (DEPTH, F, D), F),
        "conv_w_in": nrm(ks[10], (N_A_LAYERS, D, 3 * D), D),
        "conv_w": nrm(ks[11], (N_A_LAYERS, CONV_WIDTH, D), CONV_WIDTH),
        "conv_w_out": nrm(ks[12], (N_A_LAYERS, D, D), D),
        "kv_norm_g": 1.0 + 0.02 * jax.random.normal(ks[13], (D,), f32),
        "kv_ada_w": nrm(ks[14], (D, 2 * D), D, 0.1),
        "kv_ada_b": 0.01 * jax.random.normal(ks[15], (2 * D,), f32),
        "w_kv": nrm(ks[16], (D, 2 * Q_WIDTH), D),
        "attn_w_q": nrm(ks[17], (N_B_LAYERS, D, Q_WIDTH), D),
        "attn_w_o": nrm(ks[18], (N_B_LAYERS, OUT_WIDTH, D), OUT_WIDTH),
        "final_norm_g": 1.0 + 0.02 * jax.random.normal(ks[19], (D,), f32),
    }


def _fwd_reference(x, c, positions, norm_g, ada_w, ada_b, ffn1_w_in, ffn1_w_out, ffn2_w_in, ffn2_w_out,
              conv_w_in, conv_w, conv_w_out, kv_norm_g, kv_ada_w, kv_ada_b, w_kv,
              attn_w_q, attn_w_o, final_norm_g):
    cond = jax.nn.silu(c)
    cos, sin = rope_tables(positions)
    k_sh = v_sh = None
    for layer in range(DEPTH):
        if layer == N_A_LAYERS:
            kv_shift, kv_scale = jnp.split(cond @ kv_ada_w + kv_ada_b, 2, axis=-1)
            k_sh, v_sh = shared_kv(x, kv_norm_g, kv_shift, kv_scale, w_kv, cos, sin)
        mods = cond @ ada_w[layer] + ada_b[layer]
        sh1, sc1, g1, sh2, sc2, g2, sh3, sc3, g3 = jnp.split(mods, N_MOD, axis=-1)
        h = modulate(rms_norm(x, norm_g[layer, 0]), sh1, sc1)
        x = x + FFN_RES_WEIGHT * (1 + g1)[:, None, :] * swiglu(h, ffn1_w_in[layer], ffn1_w_out[layer])
        h = modulate(rms_norm(x, norm_g[layer, 1]), sh2, sc2)
        if layer < N_A_LAYERS:
            mix = short_conv_mixer(h, conv_w_in[layer], conv_w[layer], conv_w_out[layer])
        else:
            j = layer - N_A_LAYERS
            mix = dilated_attention_mixer(h, attn_w_q[j], attn_w_o[j], k_sh, v_sh, cos, sin)
        x = x + (1 + g2)[:, None, :] * mix
        h = modulate(rms_norm(x, norm_g[layer, 2]), sh3, sc3)
        x = x + FFN_RES_WEIGHT * (1 + g3)[:, None, :] * swiglu(h, ffn2_w_in[layer], ffn2_w_out[layer])
    return rms_norm(x, final_norm_g)


import jax as _jax
import jax.numpy as _jnp

TWIN_FORMAT = 'train_step'
FWD_PARAMS = ['x', 'c', 'positions', 'norm_g', 'ada_w', 'ada_b', 'ffn1_w_in', 'ffn1_w_out', 'ffn2_w_in', 'ffn2_w_out', 'conv_w_in', 'conv_w', 'conv_w_out', 'kv_norm_g', 'kv_ada_w', 'kv_ada_b', 'w_kv', 'attn_w_q', 'attn_w_o', 'final_norm_g']
TWIN_WEIGHTS = ['norm_g', 'ada_w', 'ada_b', 'ffn1_w_in', 'ffn1_w_out', 'ffn2_w_in', 'ffn2_w_out', 'conv_w_in', 'conv_w', 'conv_w_out', 'kv_norm_g', 'kv_ada_w', 'kv_ada_b', 'w_kv', 'attn_w_q', 'attn_w_o', 'final_norm_g']
TWIN_DIFF_INPUT = 'x'
TWIN_INPUTS = ['x', 'c', 'positions', 'norm_g', 'ada_w', 'ada_b', 'ffn1_w_in', 'ffn1_w_out', 'ffn2_w_in', 'ffn2_w_out', 'conv_w_in', 'conv_w', 'conv_w_out', 'kv_norm_g', 'kv_ada_w', 'kv_ada_b', 'w_kv', 'attn_w_q', 'attn_w_o', 'final_norm_g', 'loss_target', 'm_norm_g', 'm_ada_w', 'm_ada_b', 'm_ffn1_w_in', 'm_ffn1_w_out', 'm_ffn2_w_in', 'm_ffn2_w_out', 'm_conv_w_in', 'm_conv_w', 'm_conv_w_out', 'm_kv_norm_g', 'm_kv_ada_w', 'm_kv_ada_b', 'm_w_kv', 'm_attn_w_q', 'm_attn_w_o', 'm_final_norm_g', 'v_norm_g', 'v_ada_w', 'v_ada_b', 'v_ffn1_w_in', 'v_ffn1_w_out', 'v_ffn2_w_in', 'v_ffn2_w_out', 'v_conv_w_in', 'v_conv_w', 'v_conv_w_out', 'v_kv_norm_g', 'v_kv_ada_w', 'v_kv_ada_b', 'v_w_kv', 'v_attn_w_q', 'v_attn_w_o', 'v_final_norm_g']
TWIN_OUTPUTS = ['loss', 'grad_x', 'grad_norm_g', 'grad_ada_w', 'grad_ada_b', 'grad_ffn1_w_in', 'grad_ffn1_w_out', 'grad_ffn2_w_in', 'grad_ffn2_w_out', 'grad_conv_w_in', 'grad_conv_w', 'grad_conv_w_out', 'grad_kv_norm_g', 'grad_kv_ada_w', 'grad_kv_ada_b', 'grad_w_kv', 'grad_attn_w_q', 'grad_attn_w_o', 'grad_final_norm_g', 'delta_norm_g', 'delta_ada_w', 'delta_ada_b', 'delta_ffn1_w_in', 'delta_ffn1_w_out', 'delta_ffn2_w_in', 'delta_ffn2_w_out', 'delta_conv_w_in', 'delta_conv_w', 'delta_conv_w_out', 'delta_kv_norm_g', 'delta_kv_ada_w', 'delta_kv_ada_b', 'delta_w_kv', 'delta_attn_w_q', 'delta_attn_w_o', 'delta_final_norm_g', 'new_m_norm_g', 'new_m_ada_w', 'new_m_ada_b', 'new_m_ffn1_w_in', 'new_m_ffn1_w_out', 'new_m_ffn2_w_in', 'new_m_ffn2_w_out', 'new_m_conv_w_in', 'new_m_conv_w', 'new_m_conv_w_out', 'new_m_kv_norm_g', 'new_m_kv_ada_w', 'new_m_kv_ada_b', 'new_m_w_kv', 'new_m_attn_w_q', 'new_m_attn_w_o', 'new_m_final_norm_g', 'new_v_norm_g', 'new_v_ada_w', 'new_v_ada_b', 'new_v_ffn1_w_in', 'new_v_ffn1_w_out', 'new_v_ffn2_w_in', 'new_v_ffn2_w_out', 'new_v_conv_w_in', 'new_v_conv_w', 'new_v_conv_w_out', 'new_v_kv_norm_g', 'new_v_kv_ada_w', 'new_v_kv_ada_b', 'new_v_w_kv', 'new_v_attn_w_q', 'new_v_attn_w_o', 'new_v_final_norm_g']
TWIN_LEAF_KINDS = {'loss': 'loss', 'grad_x': 'grad_x', 'grad_norm_g': 'grad_w', 'grad_ada_w': 'grad_w', 'grad_ada_b': 'grad_w', 'grad_ffn1_w_in': 'grad_w', 'grad_ffn1_w_out': 'grad_w', 'grad_ffn2_w_in': 'grad_w', 'grad_ffn2_w_out': 'grad_w', 'grad_conv_w_in': 'grad_w', 'grad_conv_w': 'grad_w', 'grad_conv_w_out': 'grad_w', 'grad_kv_norm_g': 'grad_w', 'grad_kv_ada_w': 'grad_w', 'grad_kv_ada_b': 'grad_w', 'grad_w_kv': 'grad_w', 'grad_attn_w_q': 'grad_w', 'grad_attn_w_o': 'grad_w', 'grad_final_norm_g': 'grad_w', 'delta_norm_g': 'delta_w', 'delta_ada_w': 'delta_w', 'delta_ada_b': 'delta_w', 'delta_ffn1_w_in': 'delta_w', 'delta_ffn1_w_out': 'delta_w', 'delta_ffn2_w_in': 'delta_w', 'delta_ffn2_w_out': 'delta_w', 'delta_conv_w_in': 'delta_w', 'delta_conv_w': 'delta_w', 'delta_conv_w_out': 'delta_w', 'delta_kv_norm_g': 'delta_w', 'delta_kv_ada_w': 'delta_w', 'delta_kv_ada_b': 'delta_w', 'delta_w_kv': 'delta_w', 'delta_attn_w_q': 'delta_w', 'delta_attn_w_o': 'delta_w', 'delta_final_norm_g': 'delta_w', 'new_m_norm_g': 'new_m', 'new_m_ada_w': 'new_m', 'new_m_ada_b': 'new_m', 'new_m_ffn1_w_in': 'new_m', 'new_m_ffn1_w_out': 'new_m', 'new_m_ffn2_w_in': 'new_m', 'new_m_ffn2_w_out': 'new_m', 'new_m_conv_w_in': 'new_m', 'new_m_conv_w': 'new_m', 'new_m_conv_w_out': 'new_m', 'new_m_kv_norm_g': 'new_m', 'new_m_kv_ada_w': 'new_m', 'new_m_kv_ada_b': 'new_m', 'new_m_w_kv': 'new_m', 'new_m_attn_w_q': 'new_m', 'new_m_attn_w_o': 'new_m', 'new_m_final_norm_g': 'new_m', 'new_v_norm_g': 'new_v', 'new_v_ada_w': 'new_v', 'new_v_ada_b': 'new_v', 'new_v_ffn1_w_in': 'new_v', 'new_v_ffn1_w_out': 'new_v', 'new_v_ffn2_w_in': 'new_v', 'new_v_ffn2_w_out': 'new_v', 'new_v_conv_w_in': 'new_v', 'new_v_conv_w': 'new_v', 'new_v_conv_w_out': 'new_v', 'new_v_kv_norm_g': 'new_v', 'new_v_kv_ada_w': 'new_v', 'new_v_kv_ada_b': 'new_v', 'new_v_w_kv': 'new_v', 'new_v_attn_w_q': 'new_v', 'new_v_attn_w_o': 'new_v', 'new_v_final_norm_g': 'new_v'}


def _forward(args):
    return _fwd_reference(*[args[k] for k in FWD_PARAMS])


def _output_shape():
    def fwd():
        inp = _fwd_setup_inputs(0)
        return _fwd_reference(*[inp[k] for k in FWD_PARAMS])
    out = _jax.eval_shape(fwd)
    return out.shape, out.dtype

N_MICROBATCH = 1
ADAM_LR = 0.001
ADAM_B1 = 0.9
ADAM_B2 = 0.999
ADAM_EPS = 1e-08
ADAM_WD = 0.01
ADAM_STEP = 10
PER_EXAMPLE_BATCH_AXIS = {'x': 0, 'c': 0, 'positions': 0, 'loss_target': 0}
SHARED_INPUTS = []
_WEIGHT_DTYPES = {'norm_g': _jnp.float32, 'ada_w': _jnp.float32, 'ada_b': _jnp.float32, 'ffn1_w_in': _jnp.float32, 'ffn1_w_out': _jnp.float32, 'ffn2_w_in': _jnp.float32, 'ffn2_w_out': _jnp.float32, 'conv_w_in': _jnp.float32, 'conv_w': _jnp.float32, 'conv_w_out': _jnp.float32, 'kv_norm_g': _jnp.float32, 'kv_ada_w': _jnp.float32, 'kv_ada_b': _jnp.float32, 'w_kv': _jnp.float32, 'attn_w_q': _jnp.float32, 'attn_w_o': _jnp.float32, 'final_norm_g': _jnp.float32}
MOMENT_SCALE = {'norm_g': 1.680958e-01, 'ada_w': 9.101076e-02, 'ada_b': 1.705617e-01, 'ffn1_w_in': 5.236302e-02, 'ffn1_w_out': 8.532466e-02, 'ffn2_w_in': 3.110642e-02, 'ffn2_w_out': 5.071612e-02, 'conv_w_in': 2.061549e-01, 'conv_w': 2.114478e-01, 'conv_w_out': 2.055920e-01, 'kv_norm_g': 4.947325e-02, 'kv_ada_w': 6.668051e-02, 'kv_ada_b': 1.089595e-01, 'w_kv': 2.844164e-02, 'attn_w_q': 1.855838e-02, 'attn_w_o': 2.636580e-02, 'final_norm_g': 6.401589e+01}


def _to_microbatches(a, axis):
    t = _jnp.moveaxis(a, axis, 0)
    t = t.reshape((N_MICROBATCH, t.shape[0] // N_MICROBATCH) + t.shape[1:])
    return _jnp.moveaxis(t, 1, axis + 1)


def setup_inputs(seed: int = 0) -> dict:
    inp = _fwd_setup_inputs(seed)
    key = _jax.random.fold_in(_jax.random.key(seed), 7919)
    shape, _ = _output_shape()
    out = dict(inp)
    out["loss_target"] = _jax.random.normal(_jax.random.fold_in(key, 0), shape, _jnp.float32)
    for i, name in enumerate(TWIN_WEIGHTS):
        w = inp[name].astype(_jnp.float32)
        if MOMENT_SCALE is None:
            s = _jnp.sqrt(_jnp.mean(_jnp.square(w)) + 1e-30)
        else:
            s = MOMENT_SCALE[name]
        km, kv = _jax.random.split(_jax.random.fold_in(key, i + 1))
        out[name] = w
        out["m_" + name] = s * _jax.random.normal(km, w.shape, _jnp.float32)
        out["v_" + name] = (s * s) * _jax.random.uniform(kv, w.shape, _jnp.float32, 0.5, 1.5)
    if N_MICROBATCH > 1:
        for name, axis in PER_EXAMPLE_BATCH_AXIS.items():
            out[name] = _to_microbatches(out[name], axis)
    return {'x': out['x'], 'c': out['c'], 'positions': out['positions'], 'norm_g': out['norm_g'], 'ada_w': out['ada_w'], 'ada_b': out['ada_b'], 'ffn1_w_in': out['ffn1_w_in'], 'ffn1_w_out': out['ffn1_w_out'], 'ffn2_w_in': out['ffn2_w_in'], 'ffn2_w_out': out['ffn2_w_out'], 'conv_w_in': out['conv_w_in'], 'conv_w': out['conv_w'], 'conv_w_out': out['conv_w_out'], 'kv_norm_g': out['kv_norm_g'], 'kv_ada_w': out['kv_ada_w'], 'kv_ada_b': out['kv_ada_b'], 'w_kv': out['w_kv'], 'attn_w_q': out['attn_w_q'], 'attn_w_o': out['attn_w_o'], 'final_norm_g': out['final_norm_g'], 'loss_target': out['loss_target'], 'm_norm_g': out['m_norm_g'], 'm_ada_w': out['m_ada_w'], 'm_ada_b': out['m_ada_b'], 'm_ffn1_w_in': out['m_ffn1_w_in'], 'm_ffn1_w_out': out['m_ffn1_w_out'], 'm_ffn2_w_in': out['m_ffn2_w_in'], 'm_ffn2_w_out': out['m_ffn2_w_out'], 'm_conv_w_in': out['m_conv_w_in'], 'm_conv_w': out['m_conv_w'], 'm_conv_w_out': out['m_conv_w_out'], 'm_kv_norm_g': out['m_kv_norm_g'], 'm_kv_ada_w': out['m_kv_ada_w'], 'm_kv_ada_b': out['m_kv_ada_b'], 'm_w_kv': out['m_w_kv'], 'm_attn_w_q': out['m_attn_w_q'], 'm_attn_w_o': out['m_attn_w_o'], 'm_final_norm_g': out['m_final_norm_g'], 'v_norm_g': out['v_norm_g'], 'v_ada_w': out['v_ada_w'], 'v_ada_b': out['v_ada_b'], 'v_ffn1_w_in': out['v_ffn1_w_in'], 'v_ffn1_w_out': out['v_ffn1_w_out'], 'v_ffn2_w_in': out['v_ffn2_w_in'], 'v_ffn2_w_out': out['v_ffn2_w_out'], 'v_conv_w_in': out['v_conv_w_in'], 'v_conv_w': out['v_conv_w'], 'v_conv_w_out': out['v_conv_w_out'], 'v_kv_norm_g': out['v_kv_norm_g'], 'v_kv_ada_w': out['v_kv_ada_w'], 'v_kv_ada_b': out['v_kv_ada_b'], 'v_w_kv': out['v_w_kv'], 'v_attn_w_q': out['v_attn_w_q'], 'v_attn_w_o': out['v_attn_w_o'], 'v_final_norm_g': out['v_final_norm_g']}


def _loss(weights, diff, rest, loss_target):
    with _jax.named_scope("forward"):
        args = {**rest, TWIN_DIFF_INPUT: diff, **{k: w.astype(_WEIGHT_DTYPES[k]) for k, w in weights.items()}}
        y = _forward(args)
    with _jax.named_scope("loss_head"):
        err = _jnp.square(y.astype(_jnp.float32) - loss_target)
        return 0.5 * _jnp.sum(_jnp.mean(err, axis=-1)) if err.ndim else 0.5 * err


def _adamw(w, g, m, v):
    m = ADAM_B1 * m + (1.0 - ADAM_B1) * g
    v = ADAM_B2 * v + (1.0 - ADAM_B2) * _jnp.square(g)
    m_hat = m / (1.0 - ADAM_B1 ** ADAM_STEP)
    v_hat = v / (1.0 - ADAM_B2 ** ADAM_STEP)
    delta = -ADAM_LR * (m_hat / (_jnp.sqrt(v_hat) + ADAM_EPS) + ADAM_WD * w)
    return delta, m, v


def reference(x, c, positions, norm_g, ada_w, ada_b, ffn1_w_in, ffn1_w_out, ffn2_w_in, ffn2_w_out, conv_w_in, conv_w, conv_w_out, kv_norm_g, kv_ada_w, kv_ada_b, w_kv, attn_w_q, attn_w_o, final_norm_g, loss_target, m_norm_g, m_ada_w, m_ada_b, m_ffn1_w_in, m_ffn1_w_out, m_ffn2_w_in, m_ffn2_w_out, m_conv_w_in, m_conv_w, m_conv_w_out, m_kv_norm_g, m_kv_ada_w, m_kv_ada_b, m_w_kv, m_attn_w_q, m_attn_w_o, m_final_norm_g, v_norm_g, v_ada_w, v_ada_b, v_ffn1_w_in, v_ffn1_w_out, v_ffn2_w_in, v_ffn2_w_out, v_conv_w_in, v_conv_w, v_conv_w_out, v_kv_norm_g, v_kv_ada_w, v_kv_ada_b, v_w_kv, v_attn_w_q, v_attn_w_o, v_final_norm_g):
    given = dict(x=x, c=c, positions=positions, norm_g=norm_g, ada_w=ada_w, ada_b=ada_b, ffn1_w_in=ffn1_w_in, ffn1_w_out=ffn1_w_out, ffn2_w_in=ffn2_w_in, ffn2_w_out=ffn2_w_out, conv_w_in=conv_w_in, conv_w=conv_w, conv_w_out=conv_w_out, kv_norm_g=kv_norm_g, kv_ada_w=kv_ada_w, kv_ada_b=kv_ada_b, w_kv=w_kv, attn_w_q=attn_w_q, attn_w_o=attn_w_o, final_norm_g=final_norm_g, loss_target=loss_target, m_norm_g=m_norm_g, m_ada_w=m_ada_w, m_ada_b=m_ada_b, m_ffn1_w_in=m_ffn1_w_in, m_ffn1_w_out=m_ffn1_w_out, m_ffn2_w_in=m_ffn2_w_in, m_ffn2_w_out=m_ffn2_w_out, m_conv_w_in=m_conv_w_in, m_conv_w=m_conv_w, m_conv_w_out=m_conv_w_out, m_kv_norm_g=m_kv_norm_g, m_kv_ada_w=m_kv_ada_w, m_kv_ada_b=m_kv_ada_b, m_w_kv=m_w_kv, m_attn_w_q=m_attn_w_q, m_attn_w_o=m_attn_w_o, m_final_norm_g=m_final_norm_g, v_norm_g=v_norm_g, v_ada_w=v_ada_w, v_ada_b=v_ada_b, v_ffn1_w_in=v_ffn1_w_in, v_ffn1_w_out=v_ffn1_w_out, v_ffn2_w_in=v_ffn2_w_in, v_ffn2_w_out=v_ffn2_w_out, v_conv_w_in=v_conv_w_in, v_conv_w=v_conv_w, v_conv_w_out=v_conv_w_out, v_kv_norm_g=v_kv_norm_g, v_kv_ada_w=v_kv_ada_w, v_kv_ada_b=v_kv_ada_b, v_w_kv=v_w_kv, v_attn_w_q=v_attn_w_q, v_attn_w_o=v_attn_w_o, v_final_norm_g=v_final_norm_g)
    weights = {n: given[n] for n in TWIN_WEIGHTS}
    shared = {n: given[n] for n in SHARED_INPUTS}
    per_example = {n: given[n] for n in ['x', 'c', 'positions']}
    grad_fn = _jax.value_and_grad(_loss, argnums=(0, 1))

    def one_microbatch(ex, loss_target):
        ex = dict(ex)
        diff = ex.pop(TWIN_DIFF_INPUT)
        return grad_fn(weights, diff, {**shared, **ex}, loss_target)

    if N_MICROBATCH == 1:
        loss, (grad_w, grad_x) = one_microbatch(per_example, given["loss_target"])
    else:
        def body(carry, xs):
            loss_sum, grad_sum = carry
            l_k, (gw_k, gx_k) = one_microbatch(xs[0], xs[1])
            with _jax.named_scope("update"):
                return (loss_sum + l_k, _jax.tree.map(_jnp.add, grad_sum, gw_k)), gx_k

        init = (_jnp.zeros((), _jnp.float32), _jax.tree.map(_jnp.zeros_like, weights))
        (loss, grad_w), grad_x = _jax.lax.scan(body, init, (per_example, given["loss_target"]))
    with _jax.named_scope("update"):
        delta_w, new_m, new_v = {}, {}, {}
        for n in TWIN_WEIGHTS:
            delta_w[n], new_m[n], new_v[n] = _adamw(weights[n], grad_w[n], given["m_" + n], given["v_" + n])
    return (loss, grad_x, *[grad_w[n] for n in TWIN_WEIGHTS], *[delta_w[n] for n in TWIN_WEIGHTS],
            *[new_m[n] for n in TWIN_WEIGHTS], *[new_v[n] for n in TWIN_WEIGHTS])
```

```python
import functools
import math

import jax
import jax.numpy as jnp
from jax import lax
from jax.experimental import pallas as pl
from jax.experimental.pallas import tpu as pltpu

F32 = jnp.float32
BF = jnp.bfloat16
MESH = pl.DeviceIdType.MESH
AXES = ("x", "y", "c")

NORM_EPS = 1e-5
HEAD_DIM = 64
HEADS_PER_GROUP = 8
GROUP_W = HEADS_PER_GROUP * HEAD_DIM
DILATIONS = (1, 4, 16)
BAND = 128
ROPE_DIM = 16
ROPE_THETA = 500000.0
SM_SCALE = HEAD_DIM ** -0.5
NEG = -1e30
ADAM_LR, ADAM_B1, ADAM_B2, ADAM_EPS, ADAM_WD, ADAM_STEP = 0.001, 0.9, 0.999, 1e-08, 0.01, 10

V7X_VMEM_BYTES = 64 * 1024 * 1024
VMEM_LIMIT = V7X_VMEM_BYTES - 6 * 1024 * 1024
LANES = 128
TM = 512
TK = 512


def _cparams(ngrid):
    return pltpu.CompilerParams(dimension_semantics=("arbitrary",) * ngrid, vmem_limit_bytes=VMEM_LIMIT)


def _dot(a, b):
    return jnp.dot(a, b, preferred_element_type=F32)


def _dot_nt(a, b):
    return lax.dot_general(a, b, (((1,), (1,)), ((), ())), preferred_element_type=F32)


def _dot_tn(a, b):
    return lax.dot_general(a, b, (((0,), (0,)), ((), ())), preferred_element_type=F32)


def _tile(n, pref):
    t = min(n, pref)
    while n % t:
        t //= 2
    return t


def _row_tile(n, cap=256, mult=16):
    best = n
    for t in range(mult, min(n, cap) + 1, mult):
        if n % t == 0:
            best = t
    return best


def _rstd(x):
    return lax.rsqrt(jnp.mean(x * x, axis=-1, keepdims=True) + NORM_EPS)


def _norm_mod(x, v):
    return (x * _rstd(x) * v[0:1]) * (1.0 + v[1:2]) + v[2:3]


def _silu_parts(a):
    sg = jax.nn.sigmoid(a)
    return sg, a * sg


def _row(i):
    return lambda *_: (i, 0)


def _ffn_fwd(x, vec, w_in4, w_out):
    S, D = x.shape
    cs = w_in4.shape[2]
    F = 2 * cs
    tm = _tile(S, TM)

    def body(x_ref, vec_ref, wa_ref, wb_ref, wo_ref, xo_ref, h_ref, ab_ref, y_ref):
        j = pl.program_id(1)

        @pl.when(j == 0)
        def _():
            h_ref[...] = _norm_mod(x_ref[...], vec_ref[...]).astype(BF)
            y_ref[...] = jnp.zeros_like(y_ref)

        h = h_ref[...]
        a = _dot(h, wa_ref[...])
        b = _dot(h, wb_ref[...])
        ab_ref[0] = a.astype(BF)
        ab_ref[1] = b.astype(BF)
        _, s = _silu_parts(a)
        y_ref[...] += _dot((s * b).astype(BF), wo_ref[...])

        @pl.when(j == 1)
        def _():
            xo_ref[...] = x_ref[...] + (0.5 * (1.0 + vec_ref[3:4, :])) * y_ref[...]

    tok = pl.BlockSpec((tm, D), lambda i, j: (i, 0))
    return pl.pallas_call(
        body, name="ffn_fwd", grid=(S // tm, 2),
        in_specs=[tok, pl.BlockSpec((8, D), lambda i, j: (0, 0)),
                  pl.BlockSpec((None, D, cs), lambda i, j: (j, 0, 0)),
                  pl.BlockSpec((None, D, cs), lambda i, j: (j + 2, 0, 0)),
                  pl.BlockSpec((cs, D), lambda i, j: (j, 0))],
        out_specs=[tok, tok, pl.BlockSpec((2, tm, cs), lambda i, j: (0, i, j)), tok],
        out_shape=[jax.ShapeDtypeStruct((S, D), F32), jax.ShapeDtypeStruct((S, D), BF),
                   jax.ShapeDtypeStruct((2, S, F), BF), jax.ShapeDtypeStruct((S, D), F32)],
        compiler_params=_cparams(2),
    )(x, vec, w_in4, w_in4, w_out)


def _ffn_dgrad(dxo, vec, ab, w_in4, w_out):
    S, D = dxo.shape
    cs = w_in4.shape[2]
    F = 2 * cs
    tm = _tile(S, TM)

    def body(dxo_ref, vec_ref, ab_ref, wa_ref, wb_ref, wo_ref, dy_ref, u_ref, dab_ref, dh_ref):
        j = pl.program_id(1)

        @pl.when(j == 0)
        def _():
            dy_ref[...] = ((0.5 * (1.0 + vec_ref[3:4, :])) * dxo_ref[...]).astype(BF)
            dh_ref[...] = jnp.zeros_like(dh_ref)

        du = _dot_nt(dy_ref[...], wo_ref[...])
        a = ab_ref[0].astype(F32)
        b = ab_ref[1].astype(F32)
        sg, s = _silu_parts(a)
        u_ref[...] = (s * b).astype(BF)
        da = (du * b * (sg * (1.0 + a * (1.0 - sg)))).astype(BF)
        db = (du * s).astype(BF)
        dab_ref[0] = da
        dab_ref[1] = db
        dh_ref[...] += _dot_nt(da, wa_ref[...]) + _dot_nt(db, wb_ref[...])

    tok = pl.BlockSpec((tm, D), lambda i, j: (i, 0))
    return pl.pallas_call(
        body, name="ffn_dgrad", grid=(S // tm, 2),
        in_specs=[tok, pl.BlockSpec((8, D), lambda i, j: (0, 0)),
                  pl.BlockSpec((2, tm, cs), lambda i, j: (0, i, j)),
                  pl.BlockSpec((None, D, cs), lambda i, j: (j, 0, 0)),
                  pl.BlockSpec((None, D, cs), lambda i, j: (j + 2, 0, 0)),
                  pl.BlockSpec((cs, D), lambda i, j: (j, 0))],
        out_specs=[tok, pl.BlockSpec((tm, cs), lambda i, j: (i, j)),
                   pl.BlockSpec((2, tm, cs), lambda i, j: (0, i, j)), tok],
        out_shape=[jax.ShapeDtypeStruct((S, D), BF), jax.ShapeDtypeStruct((S, F), BF),
                   jax.ShapeDtypeStruct((2, S, F), BF), jax.ShapeDtypeStruct((S, D), F32)],
        compiler_params=_cparams(2),
    )(dxo, vec, ab, w_in4, w_in4, w_out)


def _mm_tn(a, b3, cs):
    K, M = a.shape
    G, _, Nb = b3.shape
    N = G * Nb
    tn = math.gcd(cs, Nb)
    tmo = M if M <= 1024 else M // 2
    tk = _tile(K, TK)

    def body(a_ref, b_ref, o_ref):
        @pl.when(pl.program_id(2) == 0)
        def _():
            o_ref[...] = jnp.zeros_like(o_ref)

        o_ref[...] += _dot_tn(a_ref[...].astype(BF), b_ref[...].astype(BF))

    return pl.pallas_call(
        body, name="wgrad_tn", grid=(M // tmo, N // tn, K // tk),
        in_specs=[pl.BlockSpec((tk, tmo), lambda m, n, k: (k, m)),
                  pl.BlockSpec((None, tk, tn), lambda m, n, k: ((n * tn) // Nb, k, ((n * tn) % Nb) // tn))],
        out_specs=pl.BlockSpec((None, tmo, tn), lambda m, n, k: ((n * tn) // cs, m, ((n * tn) % cs) // tn)),
        out_shape=jax.ShapeDtypeStruct((N // cs, M, cs), F32),
        compiler_params=_cparams(3),
    )(a, b3)


def _norm_bwd(dxo, dhs, x, y, vec, coef):
    S, D = x.shape
    tm = _tile(S, TM)
    nh = len(dhs)
    has_y = y is not None

    def body(*refs):
        dxo_ref = refs[0]
        dh_refs = refs[1:1 + nh]
        x_ref = refs[1 + nh]
        y_ref = refs[2 + nh] if has_y else None
        vec_ref, dx_ref, red_ref = refs[-3:]

        @pl.when(pl.program_id(0) == 0)
        def _():
            red_ref[...] = jnp.zeros_like(red_ref)

        xv = x_ref[...]
        dxo = dxo_ref[...]
        dh = dh_refs[0][...]
        for r in dh_refs[1:]:
            dh = dh + r[...]
        g = vec_ref[0:1, :]
        sc = vec_ref[1:2, :]
        r = _rstd(xv)
        xh = xv * r
        dhn = dh * (1.0 + sc)
        dxh = dhn * g
        dx_ref[...] = dxo + r * (dxh - xh * jnp.mean(dxh * xh, axis=-1, keepdims=True))
        red_ref[0:1, :] += jnp.sum(dhn * xh, axis=0, keepdims=True)
        red_ref[1:2, :] += jnp.sum(dh * (xh * g), axis=0, keepdims=True)
        red_ref[2:3, :] += jnp.sum(dh, axis=0, keepdims=True)
        if has_y:
            red_ref[3:4, :] += coef * jnp.sum(dxo * y_ref[...], axis=0, keepdims=True)

    tok = pl.BlockSpec((tm, D), lambda i: (i, 0))
    small = pl.BlockSpec((8, D), lambda i: (0, 0))
    ops = [dxo, *dhs, x] + ([y] if has_y else []) + [vec]
    return pl.pallas_call(
        body, name="norm_bwd", grid=(S // tm,),
        in_specs=[tok] * (len(ops) - 1) + [small],
        out_specs=[tok, small],
        out_shape=[jax.ShapeDtypeStruct((S, D), F32), jax.ShapeDtypeStruct((8, D), F32)],
        compiler_params=_cparams(1),
    )(*ops)


def _conv_fwd(x, vec, cw, w_in4, w_out):
    S, D = x.shape
    cs = w_in4.shape[2]
    tm = _tile(S, TM)

    def body(x_ref, vec_ref, cw_ref, wi_ref, wo_ref, xo_ref, h_ref, s4_ref, z_ref, y_ref, vs_ref):
        @pl.when(pl.program_id(0) == 0)
        def _():
            vs_ref[0:8, :] = jnp.zeros((8, D), F32)

        xv = x_ref[...]
        h = _norm_mod(xv, vec_ref[...]).astype(BF)
        h_ref[...] = h
        bcu = jnp.concatenate([_dot(h, wi_ref[q]) for q in range(4)], axis=1)
        bg, cg, u = bcu[:, :D], bcu[:, D:2 * D], bcu[:, 2 * D:]
        v = cg * u
        vs_ref[8:8 + tm, :] = v
        conv = cw_ref[0:1, :] * vs_ref[pl.ds(6, tm), :] + cw_ref[1:2, :] * vs_ref[pl.ds(7, tm), :] + cw_ref[2:3, :] * v
        vs_ref[0:8, :] = vs_ref[tm:tm + 8, :]
        z = (bg * conv).astype(BF)
        s4_ref[0] = bg.astype(BF)
        s4_ref[1] = cg.astype(BF)
        s4_ref[2] = u.astype(BF)
        s4_ref[3] = conv.astype(BF)
        z_ref[...] = z
        y = _dot(z, wo_ref[...])
        y_ref[...] = y
        xo_ref[...] = xv + (1.0 + vec_ref[3:4, :]) * y

    tok = pl.BlockSpec((tm, D), lambda i: (i, 0))
    small = pl.BlockSpec((8, D), lambda i: (0, 0))
    return pl.pallas_call(
        body, name="conv_fwd", grid=(S // tm,),
        in_specs=[tok, small, small, pl.BlockSpec((4, D, cs), lambda i: (0, 0, 0)),
                  pl.BlockSpec((D, D), lambda i: (0, 0))],
        out_specs=[tok, tok, pl.BlockSpec((4, tm, D), lambda i: (0, i, 0)), tok, tok],
        out_shape=[jax.ShapeDtypeStruct((S, D), F32), jax.ShapeDtypeStruct((S, D), BF),
                   jax.ShapeDtypeStruct((4, S, D), BF), jax.ShapeDtypeStruct((S, D), BF),
                   jax.ShapeDtypeStruct((S, D), F32)],
        scratch_shapes=[pltpu.VMEM((tm + 8, D), F32)],
        compiler_params=_cparams(1),
    )(x, vec, cw, w_in4, w_out)


def _conv_dgrad(dxo, vec, cw, s4, w_in4, w_out):
    S, D = dxo.shape
    cs = w_in4.shape[2]
    tm = _tile(S, TM)
    nt = S // tm

    def body(dxo_ref, vec_ref, cw_ref, s4_ref, wi_ref, wo_ref, dy_ref, d3_ref, dh_ref, dcw_ref, ds_ref):
        @pl.when(pl.program_id(0) == 0)
        def _():
            ds_ref[tm:tm + 8, :] = jnp.zeros((8, D), F32)
            dcw_ref[...] = jnp.zeros_like(dcw_ref)

        dy = ((1.0 + vec_ref[3:4, :]) * dxo_ref[...]).astype(BF)
        dy_ref[...] = dy
        dz = _dot_nt(dy, wo_ref[...])
        bg = s4_ref[0].astype(F32)
        cg = s4_ref[1].astype(F32)
        u = s4_ref[2].astype(F32)
        conv = s4_ref[3].astype(F32)
        dbg = dz * conv
        dconv = dz * bg
        ds_ref[0:tm, :] = dconv
        d1 = ds_ref[pl.ds(1, tm), :]
        d2 = ds_ref[pl.ds(2, tm), :]
        ds_ref[tm:tm + 8, :] = ds_ref[0:8, :]
        dv = cw_ref[2:3, :] * dconv + cw_ref[1:2, :] * d1 + cw_ref[0:1, :] * d2
        v = cg * u
        dcw_ref[0:1, :] += jnp.sum(d2 * v, axis=0, keepdims=True)
        dcw_ref[1:2, :] += jnp.sum(d1 * v, axis=0, keepdims=True)
        dcw_ref[2:3, :] += jnp.sum(dconv * v, axis=0, keepdims=True)
        dbcu = jnp.concatenate([dbg, dv * u, dv * cg], axis=1).astype(BF)
        d3_ref[0] = dbcu[:, :D]
        d3_ref[1] = dbcu[:, D:2 * D]
        d3_ref[2] = dbcu[:, 2 * D:]
        dh = _dot_nt(dbcu[:, 0:cs], wi_ref[0])
        for q in range(1, 4):
            dh = dh + _dot_nt(dbcu[:, q * cs:(q + 1) * cs], wi_ref[q])
        dh_ref[...] = dh

    tok = pl.BlockSpec((tm, D), lambda i: (nt - 1 - i, 0))
    small = pl.BlockSpec((8, D), lambda i: (0, 0))
    return pl.pallas_call(
        body, name="conv_dgrad", grid=(nt,),
        in_specs=[tok, small, small, pl.BlockSpec((4, tm, D), lambda i: (0, nt - 1 - i, 0)),
                  pl.BlockSpec((4, D, cs), lambda i: (0, 0, 0)), pl.BlockSpec((D, D), lambda i: (0, 0))],
        out_specs=[tok, pl.BlockSpec((3, tm, D), lambda i: (0, nt - 1 - i, 0)), tok, small],
        out_shape=[jax.ShapeDtypeStruct((S, D), BF), jax.ShapeDtypeStruct((3, S, D), BF),
                   jax.ShapeDtypeStruct((S, D), F32), jax.ShapeDtypeStruct((8, D), F32)],
        scratch_shapes=[pltpu.VMEM((tm + 8, D), F32)],
        compiler_params=_cparams(1),
    )(dxo, vec, cw, s4, w_in4, w_out)


def _rope_tables(positions):
    S = positions.shape[-1]
    inv = ROPE_THETA ** (-jnp.arange(0, ROPE_DIM, 2, dtype=F32) / ROPE_DIM)
    ang = positions.reshape(S, 1).astype(F32) * inv
    cos = jnp.tile(jnp.cos(ang), (1, LANES // 8))
    sin = jnp.tile(jnp.sin(ang), (1, LANES // 8))
    l64 = jnp.arange(LANES) % HEAD_DIM
    return jnp.stack([jnp.where(l64 < ROPE_DIM, cos, 1.0),
                      jnp.where(l64 < ROPE_DIM // 2, -sin, 0.0),
                      jnp.where((l64 >= ROPE_DIM // 2) & (l64 < ROPE_DIM), sin, 0.0)])


def _rope(t, tab_ref):
    return t * tab_ref[0] + pltpu.roll(t, LANES - 8, 1) * tab_ref[1] + pltpu.roll(t, 8, 1) * tab_ref[2]


def _rope_t(d, tab_ref):
    return d * tab_ref[0] + pltpu.roll(d * tab_ref[1], 8, 1) + pltpu.roll(d * tab_ref[2], LANES - 8, 1)


def _proj_fwd(x, vec, w4, tabs, n_rope):
    S, D = x.shape
    cs = w4.shape[2]
    N = 4 * cs
    tm = _tile(S, TM)

    def body(x_ref, vec_ref, w_ref, tab_ref, h_ref, o_ref):
        h = _norm_mod(x_ref[...], vec_ref[...]).astype(BF)
        h_ref[...] = h
        per = cs // LANES
        for q in range(4):
            acc = _dot(h, w_ref[q])
            cols = [acc[:, ch * LANES:(ch + 1) * LANES] for ch in range(per)]
            cols = [_rope(t, tab_ref) if q * per + ch < n_rope else t for ch, t in enumerate(cols)]
            o_ref[:, q * cs:(q + 1) * cs] = jnp.concatenate(cols, axis=1).astype(BF)

    return pl.pallas_call(
        body, name="proj_fwd", grid=(S // tm,),
        in_specs=[pl.BlockSpec((tm, D), lambda i: (i, 0)), pl.BlockSpec((8, D), lambda i: (0, 0)),
                  pl.BlockSpec((4, D, cs), lambda i: (0, 0, 0)), pl.BlockSpec((3, tm, LANES), lambda i: (0, i, 0))],
        out_specs=[pl.BlockSpec((tm, D), lambda i: (i, 0)), pl.BlockSpec((tm, N), lambda i: (i, 0))],
        out_shape=[jax.ShapeDtypeStruct((S, D), BF), jax.ShapeDtypeStruct((S, N), BF)],
        compiler_params=_cparams(1),
    )(x, vec, w4, tabs)


def _proj_dgrad(parts, tabs, w4, n_rope_groups):
    S = parts[0][0].shape[0]
    cs, D = w4.shape[2], w4.shape[1]
    N = 4 * cs
    tm = _tile(S, TM)
    flat = [p for grp in parts for p in grp]
    counts = [len(grp) for grp in parts]

    def body(*refs):
        prefs = refs[:len(flat)]
        tab_ref, w_ref, dz_ref, dh_ref = refs[len(flat):]
        k = 0
        for j, cnt in enumerate(counts):
            z = prefs[k][...]
            for r in prefs[k + 1:k + cnt]:
                z = z + r[...]
            k += cnt
            if j < n_rope_groups:
                z = jnp.concatenate([_rope_t(z[:, ch * LANES:(ch + 1) * LANES], tab_ref)
                                     for ch in range(GROUP_W // LANES)], axis=1)
            dz_ref[:, j * GROUP_W:(j + 1) * GROUP_W] = z.astype(BF)
        dh = _dot_nt(dz_ref[:, 0:cs], w_ref[0])
        for q in range(1, 4):
            dh = dh + _dot_nt(dz_ref[:, q * cs:(q + 1) * cs], w_ref[q])
        dh_ref[...] = dh

    grp = pl.BlockSpec((tm, GROUP_W), lambda i: (i, 0))
    return pl.pallas_call(
        body, name="proj_dgrad", grid=(S // tm,),
        in_specs=[grp] * len(flat) + [pl.BlockSpec((3, tm, LANES), lambda i: (0, i, 0)),
                                      pl.BlockSpec((4, D, cs), lambda i: (0, 0, 0))],
        out_specs=[pl.BlockSpec((tm, N), lambda i: (i, 0)), pl.BlockSpec((tm, D), lambda i: (i, 0))],
        out_shape=[jax.ShapeDtypeStruct((S, N), BF), jax.ShapeDtypeStruct((S, D), F32)],
        compiler_params=_cparams(1),
    )(*flat, tabs, w4)


def _band_masks():
    qi = lax.broadcasted_iota(jnp.int32, (BAND, BAND), 0)
    kj = lax.broadcasted_iota(jnp.int32, (BAND, BAND), 1)
    return kj <= qi, kj >= qi


def _head(ref, h):
    return ref[:, h * HEAD_DIM:(h + 1) * HEAD_DIM]


def _attn_fwd(q, kv, g):
    d = DILATIONS[g]
    S = q.shape[0]
    sd = S // d
    nb = sd // BAND
    qw, kw = q.shape[1] // GROUP_W, kv.shape[1] // GROUP_W

    def body(q_ref, kp_ref, kc_ref, vp_ref, vc_ref, o_ref, l_ref):
        same, prev = _band_masks()
        prev = jnp.logical_and(prev, pl.program_id(1) > 0)
        for h in range(HEADS_PER_GROUP):
            qh = _head(q_ref, h)
            sc = jnp.where(same, _dot_nt(qh, _head(kc_ref, h)) * SM_SCALE, NEG)
            sp = jnp.where(prev, _dot_nt(qh, _head(kp_ref, h)) * SM_SCALE, NEG)
            m = jnp.maximum(jnp.max(sc, axis=-1, keepdims=True), jnp.max(sp, axis=-1, keepdims=True))
            pc = jnp.exp(sc - m)
            pp = jnp.exp(sp - m)
            den = jnp.sum(pc, axis=-1, keepdims=True) + jnp.sum(pp, axis=-1, keepdims=True)
            o = (_dot(pc.astype(BF), _head(vc_ref, h)) + _dot(pp.astype(BF), _head(vp_ref, h))) / den
            o_ref[:, h * HEAD_DIM:(h + 1) * HEAD_DIM] = o
            l_ref[:, h * HEAD_DIM:(h + 1) * HEAD_DIM] = jnp.broadcast_to(m + jnp.log(den), (BAND, HEAD_DIM))

    blk = (BAND, GROUP_W)
    out = pl.BlockSpec(blk, lambda r, i: (i, r))
    o, lse = pl.pallas_call(
        body, name=f"attn_fwd_g{g}", grid=(d, nb),
        in_specs=[pl.BlockSpec(blk, lambda r, i: (i, r * qw + g)),
                  pl.BlockSpec(blk, lambda r, i: (jnp.maximum(i - 1, 0), r * kw + g)),
                  pl.BlockSpec(blk, lambda r, i: (i, r * kw + g)),
                  pl.BlockSpec(blk, lambda r, i: (jnp.maximum(i - 1, 0), r * kw + qw + g)),
                  pl.BlockSpec(blk, lambda r, i: (i, r * kw + qw + g))],
        out_specs=[out, out],
        out_shape=[jax.ShapeDtypeStruct((sd, d * GROUP_W), F32)] * 2,
        compiler_params=_cparams(2),
    )(q.reshape(sd, d * q.shape[1]), *([kv.reshape(sd, d * kv.shape[1])] * 4))
    return o.reshape(S, GROUP_W), lse.reshape(S, GROUP_W)


def _attn_out(os, ls, x, vec, wo4):
    S, D = x.shape
    cs = wo4.shape[2]
    tm = _tile(S, TM)

    def body(o0, l0, o1, l1, o2, l2, x_ref, vec_ref, wo_ref, mix_ref, lj_ref, y_ref, xo_ref):
        la, lb, lc = l0[...], l1[...], l2[...]
        mx = jnp.maximum(jnp.maximum(la, lb), lc)
        ea, eb, ec = jnp.exp(la - mx), jnp.exp(lb - mx), jnp.exp(lc - mx)
        den = ea + eb + ec
        mix = (ea * o0[...] + eb * o1[...] + ec * o2[...]) / den
        mix_ref[...] = mix
        lj_ref[...] = mx + jnp.log(den)
        mb = mix.astype(BF)
        y = jnp.concatenate([_dot(mb, wo_ref[q]) for q in range(4)], axis=1)
        y_ref[...] = y
        xo_ref[...] = x_ref[...] + (1.0 + vec_ref[3:4, :]) * y

    grp = pl.BlockSpec((tm, GROUP_W), lambda i: (i, 0))
    tok = pl.BlockSpec((tm, D), lambda i: (i, 0))
    return pl.pallas_call(
        body, name="attn_out", grid=(S // tm,),
        in_specs=[grp] * 6 + [tok, pl.BlockSpec((8, D), lambda i: (0, 0)),
                              pl.BlockSpec((4, GROUP_W, cs), lambda i: (0, 0, 0))],
        out_specs=[grp, grp, tok, tok],
        out_shape=[jax.ShapeDtypeStruct((S, GROUP_W), F32)] * 2 + [jax.ShapeDtypeStruct((S, D), F32)] * 2,
        compiler_params=_cparams(1),
    )(os[0], ls[0], os[1], ls[1], os[2], ls[2], x, vec, wo4)


def _attn_out_dgrad(dxo, vec, wo4):
    S, D = dxo.shape
    cs = wo4.shape[2]
    tm = _tile(S, TM)

    def body(dxo_ref, vec_ref, wo_ref, dy_ref, dm_ref):
        dy = ((1.0 + vec_ref[3:4, :]) * dxo_ref[...]).astype(BF)
        dy_ref[...] = dy
        dm = _dot_nt(dy[:, 0:cs], wo_ref[0])
        for q in range(1, 4):
            dm = dm + _dot_nt(dy[:, q * cs:(q + 1) * cs], wo_ref[q])
        dm_ref[...] = dm

    tok = pl.BlockSpec((tm, D), lambda i: (i, 0))
    return pl.pallas_call(
        body, name="attn_out_dgrad", grid=(S // tm,),
        in_specs=[tok, pl.BlockSpec((8, D), lambda i: (0, 0)), pl.BlockSpec((4, GROUP_W, cs), lambda i: (0, 0, 0))],
        out_specs=[tok, pl.BlockSpec((tm, GROUP_W), lambda i: (i, 0))],
        out_shape=[jax.ShapeDtypeStruct((S, D), BF), jax.ShapeDtypeStruct((S, GROUP_W), F32)],
        compiler_params=_cparams(1),
    )(dxo, vec, wo4)


def _prob_and_ds(qh, kh, vh, doh, oh, lrow, mask):
    s = jnp.where(mask, _dot_nt(qh, kh) * SM_SCALE, NEG)
    p = jnp.exp(s - lrow)
    delta = jnp.sum(doh * oh, axis=-1, keepdims=True)
    dp = _dot_nt(doh.astype(BF), vh)
    return p, p * (dp - delta) * SM_SCALE


def _attn_dq(q, kv, dmix, mix, lj, g):
    d = DILATIONS[g]
    S = q.shape[0]
    sd = S // d
    nb = sd // BAND
    qw, kw = q.shape[1] // GROUP_W, kv.shape[1] // GROUP_W

    def body(q_ref, kp_ref, kc_ref, vp_ref, vc_ref, do_ref, o_ref, l_ref, dq_ref):
        same, prev = _band_masks()
        prev = jnp.logical_and(prev, pl.program_id(1) > 0)
        for h in range(HEADS_PER_GROUP):
            qh, doh, oh = _head(q_ref, h), _head(do_ref, h), _head(o_ref, h)
            lrow = l_ref[:, h * HEAD_DIM:h * HEAD_DIM + 1]
            kc, kp = _head(kc_ref, h), _head(kp_ref, h)
            _, dsc = _prob_and_ds(qh, kc, _head(vc_ref, h), doh, oh, lrow, same)
            _, dsp = _prob_and_ds(qh, kp, _head(vp_ref, h), doh, oh, lrow, prev)
            dq_ref[:, h * HEAD_DIM:(h + 1) * HEAD_DIM] = _dot(dsc.astype(BF), kc) + _dot(dsp.astype(BF), kp)

    blk = (BAND, GROUP_W)
    res = pl.BlockSpec(blk, lambda r, i: (i, r))
    kvd = kv.reshape(sd, d * kv.shape[1])
    dq = pl.pallas_call(
        body, name=f"attn_dq_g{g}", grid=(d, nb),
        in_specs=[pl.BlockSpec(blk, lambda r, i: (i, r * qw + g)),
                  pl.BlockSpec(blk, lambda r, i: (jnp.maximum(i - 1, 0), r * kw + g)),
                  pl.BlockSpec(blk, lambda r, i: (i, r * kw + g)),
                  pl.BlockSpec(blk, lambda r, i: (jnp.maximum(i - 1, 0), r * kw + qw + g)),
                  pl.BlockSpec(blk, lambda r, i: (i, r * kw + qw + g)),
                  res, res, res],
        out_specs=res,
        out_shape=jax.ShapeDtypeStruct((sd, d * GROUP_W), F32),
        compiler_params=_cparams(2),
    )(q.reshape(sd, d * q.shape[1]), kvd, kvd, kvd, kvd, dmix.reshape(sd, d * GROUP_W),
      mix.reshape(sd, d * GROUP_W), lj.reshape(sd, d * GROUP_W))
    return dq.reshape(S, GROUP_W)


def _attn_dkv(q, kv, dmix, mix, lj, g):
    d = DILATIONS[g]
    S = q.shape[0]
    sd = S // d
    nb = sd // BAND
    qw, kw = q.shape[1] // GROUP_W, kv.shape[1] // GROUP_W

    def body(k_ref, v_ref, qs_ref, qn_ref, dos_ref, don_ref, os_ref, on_ref, ls_ref, ln_ref, dk_ref, dv_ref):
        same, nxt = _band_masks()
        nxt = jnp.logical_and(nxt, pl.program_id(1) < nb - 1)
        for h in range(HEADS_PER_GROUP):
            kh, vh = _head(k_ref, h), _head(v_ref, h)
            dk = jnp.zeros((BAND, HEAD_DIM), F32)
            dv = jnp.zeros((BAND, HEAD_DIM), F32)
            for q_ref, do_ref, o_ref, l_ref, mask in ((qs_ref, dos_ref, os_ref, ls_ref, same),
                                                       (qn_ref, don_ref, on_ref, ln_ref, nxt)):
                qh, doh = _head(q_ref, h), _head(do_ref, h)
                p, ds = _prob_and_ds(qh, kh, vh, doh, _head(o_ref, h), l_ref[:, h * HEAD_DIM:h * HEAD_DIM + 1], mask)
                dv = dv + _dot_tn(p.astype(BF), doh.astype(BF))
                dk = dk + _dot_tn(ds.astype(BF), qh)
            dk_ref[:, h * HEAD_DIM:(h + 1) * HEAD_DIM] = dk
            dv_ref[:, h * HEAD_DIM:(h + 1) * HEAD_DIM] = dv

    blk = (BAND, GROUP_W)
    cur = pl.BlockSpec(blk, lambda r, i: (i, r))
    nxt_spec = pl.BlockSpec(blk, lambda r, i: (jnp.minimum(i + 1, nb - 1), r))
    qd = q.reshape(sd, d * q.shape[1])
    kvd = kv.reshape(sd, d * kv.shape[1])
    dod, od, ld = (t.reshape(sd, d * GROUP_W) for t in (dmix, mix, lj))
    dk, dv = pl.pallas_call(
        body, name=f"attn_dkv_g{g}", grid=(d, nb),
        in_specs=[pl.BlockSpec(blk, lambda r, i: (i, r * kw + g)),
                  pl.BlockSpec(blk, lambda r, i: (i, r * kw + qw + g)),
                  pl.BlockSpec(blk, lambda r, i: (i, r * qw + g)),
                  pl.BlockSpec(blk, lambda r, i: (jnp.minimum(i + 1, nb - 1), r * qw + g)),
                  cur, nxt_spec, cur, nxt_spec, cur, nxt_spec],
        out_specs=[cur, cur],
        out_shape=[jax.ShapeDtypeStruct((sd, d * GROUP_W), F32)] * 2,
        compiler_params=_cparams(2),
    )(kvd, kvd, qd, qd, dod, dod, od, od, ld, ld)
    return dk.reshape(S, GROUP_W), dv.reshape(S, GROUP_W)


def _final_loss(x, gvec, tgt):
    S, D = x.shape
    tm = _tile(S, TM)

    def body(x_ref, g_ref, t_ref, dx_ref, red_ref):
        @pl.when(pl.program_id(0) == 0)
        def _():
            red_ref[...] = jnp.zeros_like(red_ref)

        xv = x_ref[...]
        g = g_ref[0:1, :]
        r = _rstd(xv)
        xh = xv * r
        err = xh * g - t_ref[...]
        dy = err * (1.0 / D)
        dxh = dy * g
        dx_ref[...] = r * (dxh - xh * jnp.mean(dxh * xh, axis=-1, keepdims=True))
        red_ref[0:1, :] += jnp.sum(dy * xh, axis=0, keepdims=True)
        red_ref[1:2, :] += jnp.sum(err * err, axis=0, keepdims=True)

    tok = pl.BlockSpec((tm, D), lambda i: (i, 0))
    small = pl.BlockSpec((8, D), lambda i: (0, 0))
    return pl.pallas_call(
        body, name="final_loss", grid=(S // tm,),
        in_specs=[tok, small, tok], out_specs=[tok, small],
        out_shape=[jax.ShapeDtypeStruct((S, D), F32), jax.ShapeDtypeStruct((8, D), F32)],
        compiler_params=_cparams(1),
    )(x, gvec, tgt)


def _adamw(w, g, m, v):
    shape = w.shape
    C = shape[-1]
    R = w.size // C
    tr = 256 if R % 256 == 0 else R

    def body(w_ref, g_ref, m_ref, v_ref, d_ref, nm_ref, nv_ref):
        gv = g_ref[...]
        nm = ADAM_B1 * m_ref[...] + (1.0 - ADAM_B1) * gv
        nv = ADAM_B2 * v_ref[...] + (1.0 - ADAM_B2) * (gv * gv)
        m_hat = nm / (1.0 - ADAM_B1 ** ADAM_STEP)
        v_hat = nv / (1.0 - ADAM_B2 ** ADAM_STEP)
        d_ref[...] = -ADAM_LR * (m_hat / (jnp.sqrt(v_hat) + ADAM_EPS) + ADAM_WD * w_ref[...])
        nm_ref[...] = nm
        nv_ref[...] = nv

    spec = pl.BlockSpec((tr, C), lambda i: (i, 0))
    outs = pl.pallas_call(
        body, name="adamw", grid=(R // tr,),
        in_specs=[spec] * 4, out_specs=[spec] * 3,
        out_shape=[jax.ShapeDtypeStruct((R, C), F32)] * 3,
        compiler_params=_cparams(1),
    )(*(t.reshape(R, C) for t in (w, g, m, v)))
    return tuple(o.reshape(shape) for o in outs)


def _ada_fwd(c_all, w, b):
    L, D, N = w.shape
    tn = 768 if N % 768 == 0 else _tile(N, 512)

    def body(c_ref, w_ref, b_ref, o_ref):
        cv = c_ref[...]
        cond = (cv * jax.nn.sigmoid(cv)).astype(BF)
        o_ref[...] = _dot(cond, w_ref[...].astype(BF)) + b_ref[...]

    return pl.pallas_call(
        body, name="ada_fwd", grid=(L, N // tn),
        in_specs=[pl.BlockSpec((8, D), lambda l, n: (0, 0)), pl.BlockSpec((None, D, tn), lambda l, n: (l, 0, n)),
                  pl.BlockSpec((None, 1, tn), lambda l, n: (l, 0, n))],
        out_specs=pl.BlockSpec((None, 8, tn), lambda l, n: (l, 0, n)),
        out_shape=jax.ShapeDtypeStruct((L, 8, N), F32),
        compiler_params=_cparams(2),
    )(c_all, w, b)


def _ada_wgrad(c_all_t, dm):
    D = c_all_t.shape[0]
    L, _, N = dm.shape
    tn = 256

    def body(c_ref, dm_ref, o_ref):
        cv = c_ref[...]
        cond = cv * jax.nn.sigmoid(cv)
        acc = cond[:, 0:1] * dm_ref[0:1, :]
        for b in range(1, 8):
            acc = acc + cond[:, b:b + 1] * dm_ref[b:b + 1, :]
        o_ref[...] = acc

    return pl.pallas_call(
        body, name="ada_wgrad", grid=(L, N // tn),
        in_specs=[pl.BlockSpec((D, 8), lambda l, n: (0, 0)), pl.BlockSpec((None, 8, tn), lambda l, n: (l, 0, n))],
        out_specs=pl.BlockSpec((None, D, tn), lambda l, n: (l, 0, n)),
        out_shape=jax.ShapeDtypeStruct((L, D, N), F32),
        compiler_params=_cparams(2),
    )(c_all_t, dm)


def _sum8(g):
    _, R, C = g.shape

    def body(g_ref, o_ref):
        acc = g_ref[0]
        for b in range(1, 8):
            acc = acc + g_ref[b]
        o_ref[...] = acc

    return pl.pallas_call(body, name="sum8", out_shape=jax.ShapeDtypeStruct((R, C), F32),
                          in_specs=[pl.BlockSpec(memory_space=pltpu.VMEM)],
                          out_specs=pl.BlockSpec(memory_space=pltpu.VMEM))(g)


def _pair_add(g4, recv, cidx):
    _, _, ha, B = g4.shape
    tr = _row_tile(ha)

    def body(c_ref, g_ref, r_ref, o_ref):
        o_ref[...] = (g_ref[...] + r_ref[...]).astype(BF)

    return pl.pallas_call(
        body, name="pair_add",
        grid_spec=pltpu.PrefetchScalarGridSpec(
            num_scalar_prefetch=1, grid=(4, ha // tr),
            in_specs=[pl.BlockSpec((None, None, tr, B), lambda s, i, c: (s, c[0], i, 0)),
                      pl.BlockSpec((None, tr, B), lambda s, i, c: (s, i, 0))],
            out_specs=pl.BlockSpec((None, tr, B), lambda s, i, c: (s, i, 0))),
        out_shape=jax.ShapeDtypeStruct((4, ha, B), BF),
        compiler_params=_cparams(2),
    )(cidx, g4, recv)


def _sum4(parts):
    _, ha, B = parts.shape
    tr = _row_tile(ha)

    def body(p_ref, o_ref):
        o_ref[...] = ((p_ref[0].astype(F32) + p_ref[1].astype(F32)) + p_ref[2].astype(F32)) + p_ref[3].astype(F32)

    return pl.pallas_call(
        body, name="sum4", grid=(ha // tr,),
        in_specs=[pl.BlockSpec((4, tr, B), lambda i: (0, i, 0))],
        out_specs=pl.BlockSpec((tr, B), lambda i: (i, 0)),
        out_shape=jax.ShapeDtypeStruct((ha, B), F32),
        compiler_params=_cparams(1),
    )(parts)


ANY = pl.BlockSpec(memory_space=pl.ANY)


def _place():
    x, y, c = (lax.axis_index(a) for a in AXES)
    chips = [(1 - x, y), (x, 1 - y), (1 - x, 1 - y)]
    return x, y, c, 2 * x + y, chips, [2 * cx + cy for cx, cy in chips]


def _allgather_weights(shards):
    layers = [(a, l) for a, s in enumerate(shards) for l in range(s.shape[0])]
    n_in, n_out = len(shards), len(layers)

    def body(*refs):
        ins, outs = refs[:n_in], refs[n_in:n_in + n_out]
        ssem, rsem, lsem = refs[n_in + n_out:]
        x, y, c, p, chips, qs = _place()
        sib = (x, y, 1 - c)

        def rcopy(src, dst, k, j, dev):
            return pltpu.make_async_remote_copy(src_ref=src, dst_ref=dst, send_sem=ssem.at[k, j], recv_sem=rsem.at[k, j],
                                                device_id=dev, device_id_type=MESH)

        local, sent = [], []
        for k, (a, l) in enumerate(layers):
            cp = pltpu.make_async_copy(ins[a].at[l], outs[k].at[p], lsem.at[k])
            cp.start()
            local.append(cp)
            for j in range(3):
                cp = rcopy(ins[a].at[l, c], outs[k].at[p, c], k, j, (*chips[j], c))
                cp.start()
                sent.append(cp)
        for k in range(n_out):
            for j in range(3):
                here = outs[k].at[qs[j], c]
                rcopy(here, here, k, j, sib).wait_recv()
                cp = rcopy(here, here, k, 3 + j, sib)
                cp.start()
                sent.append(cp)
        for k in range(n_out):
            for j in range(3):
                there = outs[k].at[qs[j], 1 - c]
                rcopy(there, there, k, 3 + j, sib).wait_recv()
        for cp in sent:
            cp.wait_send()
        for cp in local:
            cp.wait()

    out_shape = [jax.ShapeDtypeStruct((4,) + shards[a].shape[1:], BF) for a, _ in layers]
    outs = pl.pallas_call(
        body, name="allgather_weights", in_specs=[ANY] * n_in, out_specs=[ANY] * n_out, out_shape=out_shape,
        scratch_shapes=[pltpu.SemaphoreType.DMA((n_out, 6)), pltpu.SemaphoreType.DMA((n_out, 6)),
                        pltpu.SemaphoreType.DMA((n_out,))],
    )(*shards)
    res, k = [], 0
    for s in shards:
        res.append(list(outs[k:k + s.shape[0]]))
        k += s.shape[0]
    return res


def _exchange_halves(grads):
    n = len(grads)

    def body(*refs):
        ins, outs = refs[:n], refs[n:2 * n]
        ssem, rsem = refs[2 * n:]
        x, y, c, *_ = _place()
        sib = (x, y, 1 - c)
        cps = []
        for k in range(n):
            cp = pltpu.make_async_remote_copy(src_ref=ins[k].at[:, 1 - c], dst_ref=outs[k], send_sem=ssem.at[k],
                                              recv_sem=rsem.at[k], device_id=sib, device_id_type=MESH)
            cp.start()
            cps.append(cp)
        for cp in cps:
            cp.wait()

    return pl.pallas_call(
        body, name="grad_exchange_halves", in_specs=[ANY] * n, out_specs=[ANY] * n,
        out_shape=[jax.ShapeDtypeStruct((4,) + g.shape[2:], F32) for g in grads],
        scratch_shapes=[pltpu.SemaphoreType.DMA((n,)), pltpu.SemaphoreType.DMA((n,))],
    )(*grads)


def _scatter_partials(parts):
    n = len(parts)

    def body(*refs):
        ins, outs = refs[:n], refs[n:2 * n]
        ssem, rsem, lsem = refs[2 * n:]
        x, y, c, p, chips, qs = _place()
        loc, sent = [], []
        for k in range(n):
            cp = pltpu.make_async_copy(ins[k].at[p], outs[k].at[p], lsem.at[k])
            cp.start()
            loc.append(cp)
            for j in range(3):
                cp = pltpu.make_async_remote_copy(src_ref=ins[k].at[qs[j]], dst_ref=outs[k].at[p], send_sem=ssem.at[k, j],
                                                  recv_sem=rsem.at[k, j], device_id=(*chips[j], c), device_id_type=MESH)
                cp.start()
                sent.append(cp)
        for k in range(n):
            for j in range(3):
                there = outs[k].at[qs[j]]
                pltpu.make_async_remote_copy(src_ref=there, dst_ref=there, send_sem=ssem.at[k, j], recv_sem=rsem.at[k, j],
                                             device_id=(*chips[j], c), device_id_type=MESH).wait_recv()
        for cp in sent:
            cp.wait_send()
        for cp in loc:
            cp.wait()

    return pl.pallas_call(
        body, name="grad_scatter_partials", in_specs=[ANY] * n, out_specs=[ANY] * n,
        out_shape=[jax.ShapeDtypeStruct(t.shape, BF) for t in parts],
        scratch_shapes=[pltpu.SemaphoreType.DMA((n, 3)), pltpu.SemaphoreType.DMA((n, 3)), pltpu.SemaphoreType.DMA((n,))],
    )(*parts)


def _share_halves(halves, groups):
    n = len(halves)

    def body(*refs):
        ins, outs = refs[:n], refs[n:n + len(groups)]
        ssem, rsem, lsem = refs[n + len(groups):]
        x, y, c, *_ = _place()
        sib = (x, y, 1 - c)
        loc, rem = [], []
        for w, ks in enumerate(groups):
            for l, k in enumerate(ks):
                cp = pltpu.make_async_copy(ins[k], outs[w].at[l, c], lsem.at[k])
                cp.start()
                loc.append(cp)
                cp = pltpu.make_async_remote_copy(src_ref=ins[k], dst_ref=outs[w].at[l, c], send_sem=ssem.at[k],
                                                  recv_sem=rsem.at[k], device_id=sib, device_id_type=MESH)
                cp.start()
                rem.append((cp, w, l, k))
        for cp, w, l, k in rem:
            there = outs[w].at[l, 1 - c]
            pltpu.make_async_remote_copy(src_ref=there, dst_ref=there, send_sem=ssem.at[k], recv_sem=rsem.at[k],
                                         device_id=sib, device_id_type=MESH).wait_recv()
        for cp, *_ in rem:
            cp.wait_send()
        for cp in loc:
            cp.wait()

    out_shape = [jax.ShapeDtypeStruct((len(ks), 2) + halves[ks[0]].shape, F32) for ks in groups]
    return pl.pallas_call(
        body, name="grad_share_halves", in_specs=[ANY] * n, out_specs=[ANY] * len(groups), out_shape=out_shape,
        scratch_shapes=[pltpu.SemaphoreType.DMA((n,)), pltpu.SemaphoreType.DMA((n,)), pltpu.SemaphoreType.DMA((n,))],
    )(*halves)


def _gather8(v):
    R, C = v.shape

    def body(v_ref, o_ref, ssem, rsem):
        x, y, c, *_ = _place()
        me = 4 * x + 2 * y + c
        o_ref[me] = v_ref[...]
        cps = []
        for k in range(1, 8):
            fx, fy, fc = (k >> 2) & 1, (k >> 1) & 1, k & 1
            peer = (x ^ fx, y ^ fy, c ^ fc)
            cp = pltpu.make_async_remote_copy(src_ref=v_ref, dst_ref=o_ref.at[me], send_sem=ssem.at[k - 1],
                                              recv_sem=rsem.at[k - 1], device_id=peer, device_id_type=MESH)
            cp.start()
            cps.append((cp, 4 * peer[0] + 2 * peer[1] + peer[2]))
        for k, (cp, slot) in enumerate(cps):
            there = o_ref.at[slot]
            pltpu.make_async_remote_copy(src_ref=there, dst_ref=there, send_sem=ssem.at[k], recv_sem=rsem.at[k],
                                         device_id=(x, y, c), device_id_type=MESH).wait_recv()
        for cp, _ in cps:
            cp.wait_send()

    vm = pl.BlockSpec(memory_space=pltpu.VMEM)
    return pl.pallas_call(
        body, name="gather8", in_specs=[vm], out_specs=vm, out_shape=jax.ShapeDtypeStruct((8, R, C), F32),
        scratch_shapes=[pltpu.SemaphoreType.DMA((7,)), pltpu.SemaphoreType.DMA((7,))],
    )(v)


def _mods_to_owner(ms):
    _, R, C = ms.shape

    def body(m_ref, o_ref, ssem, rsem):
        x, y, c, p, chips, qs = _place()
        o_ref[p] = m_ref[2 * p + c]
        cps = []
        for j in range(3):
            cp = pltpu.make_async_remote_copy(src_ref=m_ref.at[2 * qs[j] + c], dst_ref=o_ref.at[p], send_sem=ssem.at[j],
                                              recv_sem=rsem.at[j], device_id=(*chips[j], c), device_id_type=MESH)
            cp.start()
            cps.append(cp)
        for j in range(3):
            there = o_ref.at[qs[j]]
            pltpu.make_async_remote_copy(src_ref=there, dst_ref=there, send_sem=ssem.at[j], recv_sem=rsem.at[j],
                                         device_id=(x, y, c), device_id_type=MESH).wait_recv()
        for cp in cps:
            cp.wait_send()

    vm = pl.BlockSpec(memory_space=pltpu.VMEM)
    return pl.pallas_call(
        body, name="mods_to_owner", in_specs=[vm], out_specs=vm, out_shape=jax.ShapeDtypeStruct((4, R, C), F32),
        scratch_shapes=[pltpu.SemaphoreType.DMA((3,)), pltpu.SemaphoreType.DMA((3,))],
    )(ms)


def _vec(*rows):
    D = rows[0].shape[-1]
    rows = [r.reshape(1, D) for r in rows]
    return jnp.concatenate(rows + [jnp.zeros((8 - len(rows), D), F32)], axis=0)


def _local_step(x, tgt, tabs, norm_g, conv_w, kv_norm_g, final_norm_g, mods, kvmods, W):
    S, D = x.shape
    n_layers, n_conv = norm_g.shape[0], conv_w.shape[0]
    md = mods.reshape(n_layers, 9, D)

    def vec_of(l, k):
        return _vec(norm_g[l, k], md[l, 3 * k + 1], md[l, 3 * k], md[l, 3 * k + 2])

    vec_kv = _vec(kv_norm_g, kvmods[D:], kvmods[:D])
    saved = {}
    xc = x
    kv = None
    for l in range(n_layers):
        if l == n_conv:
            h_kv, kv = _proj_fwd(xc, vec_kv, W["w_kv"][0], tabs, GROUP_W * 3 // LANES)
            saved["kv"] = (xc, h_kv)
        v0 = vec_of(l, 0)
        xo, h, ab, y = _ffn_fwd(xc, v0, W["ffn1_w_in"][l], W["ffn1_w_out"][l])
        saved[l, 0] = (xc, v0, h, ab, y)
        xc = xo
        v1 = vec_of(l, 1)
        if l < n_conv:
            cw = _vec(conv_w[l, 0], conv_w[l, 1], conv_w[l, 2])
            xo, h, s4, z, y = _conv_fwd(xc, v1, cw, W["conv_w_in"][l], W["conv_w_out"][l])
            saved[l, 1] = (xc, v1, h, y, cw, s4, z)
        else:
            j = l - n_conv
            h, q = _proj_fwd(xc, v1, W["attn_w_q"][j], tabs, GROUP_W * 3 // LANES)
            og = [_attn_fwd(q, kv, g) for g in range(3)]
            mix, lj, y, xo = _attn_out([o for o, _ in og], [s for _, s in og], xc, v1, W["attn_w_o"][j])
            saved[l, 1] = (xc, v1, h, y, q, mix, lj)
        xc = xo
        v2 = vec_of(l, 2)
        xo, h, ab, y = _ffn_fwd(xc, v2, W["ffn2_w_in"][l], W["ffn2_w_out"][l])
        saved[l, 2] = (xc, v2, h, ab, y)
        xc = xo

    dx, red_f = _final_loss(xc, _vec(final_norm_g), tgt)
    loss_part = 0.5 * jnp.sum(red_f[1]) / D
    G = {k: [None] * len(v) for k, v in W.items()}
    dmods = [[None] * 9 for _ in range(n_layers)]
    dnorm = [[None] * 3 for _ in range(n_layers)]
    dconvw = [None] * n_conv
    dkv_parts = []

    def ffn_bwd(dx, l, k, name):
        xin, v, h, ab, y = saved[l, k]
        w_in, w_out = W[name + "_w_in"][l], W[name + "_w_out"][l]
        dy, u, dab, dh = _ffn_dgrad(dx, v, ab, w_in, w_out)
        G[name + "_w_in"][l] = _mm_tn(h, dab, w_in.shape[2])
        G[name + "_w_out"][l] = _mm_tn(u, dy[None], D).reshape(4, -1, D)
        return _norm_bwd(dx, [dh], xin, y, v, 0.5)

    def note(l, k, red):
        dnorm[l][k] = red[0]
        dmods[l][3 * k], dmods[l][3 * k + 1], dmods[l][3 * k + 2] = red[2], red[1], red[3]

    for l in reversed(range(n_layers)):
        dx, red = ffn_bwd(dx, l, 2, "ffn2")
        note(l, 2, red)
        if l < n_conv:
            xin, v, h, y, cw, s4, z = saved[l, 1]
            w_in, w_out = W["conv_w_in"][l], W["conv_w_out"][l]
            dy, d3, dh, dcw = _conv_dgrad(dx, v, cw, s4, w_in, w_out)
            G["conv_w_in"][l] = _mm_tn(h, d3, w_in.shape[2])
            G["conv_w_out"][l] = _mm_tn(z, dy[None], D).reshape(4, -1, D)
            dconvw[l] = dcw[0:3]
        else:
            j = l - n_conv
            xin, v, h, y, q, mix, lj = saved[l, 1]
            wq, wo = W["attn_w_q"][j], W["attn_w_o"][j]
            dy, dmix = _attn_out_dgrad(dx, v, wo)
            G["attn_w_o"][j] = _mm_tn(mix, dy[None], wo.shape[2])
            dqs = [_attn_dq(q, kv, dmix, mix, lj, g) for g in range(3)]
            dkv_parts.append([_attn_dkv(q, kv, dmix, mix, lj, g) for g in range(3)])
            dq, dh = _proj_dgrad([[t] for t in dqs], tabs, wq, 3)
            G["attn_w_q"][j] = _mm_tn(h, dq[None], wq.shape[2])
        dx, red = _norm_bwd(dx, [dh], xin, y, v, 1.0)
        note(l, 1, red)
        dx, red = ffn_bwd(dx, l, 0, "ffn1")
        note(l, 0, red)
        if l == n_conv:
            xin, h_kv = saved["kv"]
            parts = [[lay[g][0] for lay in dkv_parts] for g in range(3)] + [[lay[g][1] for lay in dkv_parts] for g in range(3)]
            dkv, dh = _proj_dgrad(parts, tabs, W["w_kv"][0], 3)
            G["w_kv"][0] = _mm_tn(h_kv, dkv[None], W["w_kv"][0].shape[2])
            dx, red_kv = _norm_bwd(dx, [dh], xin, None, vec_kv, 1.0)
    small = jnp.concatenate(
        [jnp.stack([jnp.stack(r) for r in dmods]).reshape(-1), red_kv[2], red_kv[1],
         jnp.stack([jnp.stack(r) for r in dnorm]).reshape(-1), jnp.stack(dconvw).reshape(-1), red_kv[0], red_f[0]])
    return loss_part, dx, G, small


BIG = ("ffn1_w_in", "ffn1_w_out", "ffn2_w_in", "ffn2_w_out", "conv_w_in", "conv_w_out", "w_kv", "attn_w_q", "attn_w_o")


def _halved(t):
    if t.ndim == 2:
        t = t[None]
    L, A, B = t.shape
    return t.reshape(L, 2, A // 2, B)


def kernel(x, c, positions, norm_g, ada_w, ada_b, ffn1_w_in, ffn1_w_out, ffn2_w_in, ffn2_w_out, conv_w_in, conv_w, conv_w_out, kv_norm_g, kv_ada_w, kv_ada_b, w_kv, attn_w_q, attn_w_o, final_norm_g, loss_target, m_norm_g, m_ada_w, m_ada_b, m_ffn1_w_in, m_ffn1_w_out, m_ffn2_w_in, m_ffn2_w_out, m_conv_w_in, m_conv_w, m_conv_w_out, m_kv_norm_g, m_kv_ada_w, m_kv_ada_b, m_w_kv, m_attn_w_q, m_attn_w_o, m_final_norm_g, v_norm_g, v_ada_w, v_ada_b, v_ffn1_w_in, v_ffn1_w_out, v_ffn2_w_in, v_ffn2_w_out, v_conv_w_in, v_conv_w, v_conv_w_out, v_kv_norm_g, v_kv_ada_w, v_kv_ada_b, v_w_kv, v_attn_w_q, v_attn_w_o, v_final_norm_g):
    wts = dict(norm_g=norm_g, ada_w=ada_w, ada_b=ada_b, ffn1_w_in=ffn1_w_in, ffn1_w_out=ffn1_w_out, ffn2_w_in=ffn2_w_in,
               ffn2_w_out=ffn2_w_out, conv_w_in=conv_w_in, conv_w=conv_w, conv_w_out=conv_w_out, kv_norm_g=kv_norm_g,
               kv_ada_w=kv_ada_w, kv_ada_b=kv_ada_b, w_kv=w_kv, attn_w_q=attn_w_q, attn_w_o=attn_w_o,
               final_norm_g=final_norm_g)
    ms = dict(norm_g=m_norm_g, ada_w=m_ada_w, ada_b=m_ada_b, ffn1_w_in=m_ffn1_w_in, ffn1_w_out=m_ffn1_w_out,
              ffn2_w_in=m_ffn2_w_in, ffn2_w_out=m_ffn2_w_out, conv_w_in=m_conv_w_in, conv_w=m_conv_w,
              conv_w_out=m_conv_w_out, kv_norm_g=m_kv_norm_g, kv_ada_w=m_kv_ada_w, kv_ada_b=m_kv_ada_b, w_kv=m_w_kv,
              attn_w_q=m_attn_w_q, attn_w_o=m_attn_w_o, final_norm_g=m_final_norm_g)
    vs = dict(norm_g=v_norm_g, ada_w=v_ada_w, ada_b=v_ada_b, ffn1_w_in=v_ffn1_w_in, ffn1_w_out=v_ffn1_w_out,
              ffn2_w_in=v_ffn2_w_in, ffn2_w_out=v_ffn2_w_out, conv_w_in=v_conv_w_in, conv_w=v_conv_w,
              conv_w_out=v_conv_w_out, kv_norm_g=v_kv_norm_g, kv_ada_w=v_kv_ada_w, kv_ada_b=v_kv_ada_b, w_kv=v_w_kv,
              attn_w_q=v_attn_w_q, attn_w_o=v_attn_w_o, final_norm_g=v_final_norm_g)
    order = list(wts)
    S, D = x.shape[1], x.shape[2]
    n_layers, n_conv = norm_g.shape[0], conv_w.shape[0]
    nm, sw = ada_w.shape[2], norm_g.shape[2]
    ix, iy, ic = (lax.axis_index(a) for a in AXES)
    chip = 2 * ix + iy

    pack = jnp.concatenate([norm_g.reshape(-1), conv_w.reshape(-1), c.reshape(-1)])
    npad = (-pack.size) % (8 * LANES)
    allp = _gather8(jnp.pad(pack, (0, npad)).reshape(-1, LANES)).reshape(8, -1)
    n1, n2 = norm_g.size, norm_g.size + conv_w.size
    by_chip = allp[0::2]
    norm_full = jnp.moveaxis(by_chip[:, :n1].reshape(4, n_layers, 3, sw), 0, 2).reshape(n_layers, 3, 4 * sw)
    conv_full = jnp.moveaxis(by_chip[:, n1:n2].reshape(4, n_conv, 3, sw), 0, 2).reshape(n_conv, 3, 4 * sw)
    c_all = allp[:, n2:n2 + D]

    b_l = lax.dynamic_slice_in_dim(ada_b, chip * nm, nm, axis=1)
    mod_sh = _ada_fwd(c_all, ada_w, b_l[:, None, :])
    nkv = kv_ada_w.shape[1]
    bkv = lax.dynamic_slice_in_dim(kv_ada_b, chip * nkv, nkv, axis=0)
    kv_sh = _ada_fwd(c_all, kv_ada_w[None], bkv[None, None, :])
    rows = jnp.concatenate([jnp.moveaxis(mod_sh, 0, 1), jnp.pad(jnp.moveaxis(kv_sh, 0, 1), ((0, 0), (0, 0), (0, nm - nkv)))], axis=1)
    rpad = (-rows.shape[1]) % 8
    mine = _mods_to_owner(jnp.pad(rows, ((0, 0), (0, rpad), (0, 0))))
    mods = jnp.moveaxis(mine[:, :n_layers], 0, 1).reshape(n_layers, 4 * nm)
    kvmods = mine[:, n_layers, :nkv].reshape(4 * nkv)

    gathered = _allgather_weights([_halved(wts[k]).astype(BF) for k in BIG])
    W = {}
    for k, per_layer in zip(BIG, gathered):
        A, B = 2 * per_layer[0].shape[2], per_layer[0].shape[3]
        if k.endswith("w_out"):
            W[k] = [t.reshape(4 * A, B) for t in per_layer]
        else:
            W[k] = [t.reshape(4, A, B) for t in per_layer]

    loss_part, dx, G, small = _local_step(x[0], loss_target[0], _rope_tables(positions), norm_full, conv_full,
                                          kv_norm_g, final_norm_g, mods, kvmods, W)
    loss = lax.psum(loss_part, AXES)

    flat, groups = [], []
    for k in BIG:
        groups.append(list(range(len(flat), len(flat) + len(G[k]))))
        flat += [g.reshape(4, 2, g.shape[1] // 2, g.shape[2]) for g in G[k]]
    recv = _exchange_halves(flat)
    cidx = ic.astype(jnp.int32).reshape(1)
    parts = _scatter_partials([_pair_add(g, r, cidx) for g, r in zip(flat, recv)])
    shared = _share_halves([_sum4(p) for p in parts], groups)
    grads = {k: s.reshape(wts[k].shape) for k, s in zip(BIG, shared)}

    spad = (-small.size) % (8 * LANES)
    gath = _gather8(jnp.pad(small, (0, spad)).reshape(-1, LANES))
    tot = _sum8(gath).reshape(-1)
    gath = gath.reshape(8, -1)
    o = 0

    def take(n):
        nonlocal o
        o += n
        return tot[o - n:o]

    g_mods = take(n_layers * 9 * D).reshape(n_layers, 9 * D)
    g_kvmods = take(2 * D)
    g_norm = take(n_layers * 3 * D).reshape(n_layers, 3, D)
    g_convw = take(n_conv * 3 * D).reshape(n_conv, 3, D)
    g_kvn = take(D)
    g_fin = take(D)
    grads["ada_b"] = g_mods
    grads["kv_ada_b"] = g_kvmods
    grads["norm_g"] = lax.dynamic_slice_in_dim(g_norm, chip * sw, sw, axis=2)
    grads["conv_w"] = lax.dynamic_slice_in_dim(g_convw, chip * sw, sw, axis=2)
    grads["kv_norm_g"] = g_kvn
    grads["final_norm_g"] = g_fin

    dm_all = gath[:, :n_layers * 9 * D].reshape(8, n_layers, 9 * D)
    dm_mine = jnp.moveaxis(lax.dynamic_slice_in_dim(dm_all, chip * nm, nm, axis=2), 0, 1)
    dkv_all = gath[:, n_layers * 9 * D:n_layers * 9 * D + 2 * D]
    dkv_mine = lax.dynamic_slice_in_dim(dkv_all, chip * nkv, nkv, axis=1)[None]
    c_t = c_all.T
    grads["ada_w"] = _ada_wgrad(c_t, dm_mine)
    grads["kv_ada_w"] = _ada_wgrad(c_t, dkv_mine)[0]

    deltas, new_m, new_v = {}, {}, {}
    for k in order:
        shp = wts[k].shape
        two_d = (lambda t: t.reshape(1, -1)) if len(shp) == 1 else (lambda t: t)
        dlt, nmk, nvk = _adamw(two_d(wts[k]), two_d(grads[k]), two_d(ms[k]), two_d(vs[k]))
        deltas[k], new_m[k], new_v[k] = dlt.reshape(shp), nmk.reshape(shp), nvk.reshape(shp)
    return (loss, dx[None], *[grads[k] for k in order], *[deltas[k] for k in order], *[new_m[k] for k in order],
            *[new_v[k] for k in order])
```

```python
import functools
import math

import jax
import jax.numpy as jnp
from jax import lax
from jax.experimental import pallas as pl
from jax.experimental.pallas import tpu as pltpu

F32 = jnp.float32
BF = jnp.bfloat16
MESH = pl.DeviceIdType.MESH
AXES = ("x", "y", "c")

NORM_EPS = 1e-5
HEAD_DIM = 64
HEADS_PER_GROUP = 8
GROUP_W = HEADS_PER_GROUP * HEAD_DIM
DILATIONS = (1, 4, 16)
BAND = 128
ROPE_DIM = 16
ROPE_THETA = 500000.0
SM_SCALE = HEAD_DIM ** -0.5
NEG = -1e30
ADAM_LR, ADAM_B1, ADAM_B2, ADAM_EPS, ADAM_WD, ADAM_STEP = 0.001, 0.9, 0.999, 1e-08, 0.01, 10

V7X_VMEM_BYTES = 64 * 1024 * 1024
VMEM_LIMIT = V7X_VMEM_BYTES - 6 * 1024 * 1024
LANES = 128
TM = 512
TK = 512


def _cparams(ngrid):
    return pltpu.CompilerParams(dimension_semantics=("arbitrary",) * ngrid, vmem_limit_bytes=VMEM_LIMIT)


def _dot(a, b):
    return jnp.dot(a, b, preferred_element_type=F32)


def _dot_nt(a, b):
    return lax.dot_general(a, b, (((1,), (1,)), ((), ())), preferred_element_type=F32)


def _dot_tn(a, b):
    return lax.dot_general(a, b, (((0,), (0,)), ((), ())), preferred_element_type=F32)


def _tile(n, pref):
    t = min(n, pref)
    while n % t:
        t //= 2
    return t


def _row_tile(n, cap=256, mult=16):
    best = n
    for t in range(mult, min(n, cap) + 1, mult):
        if n % t == 0:
            best = t
    return best


def _rstd(x):
    return lax.rsqrt(jnp.mean(x * x, axis=-1, keepdims=True) + NORM_EPS)


def _norm_mod(x, v):
    return (x * _rstd(x) * v[0:1]) * (1.0 + v[1:2]) + v[2:3]


def _silu_parts(a):
    sg = jax.nn.sigmoid(a)
    return sg, a * sg


def _row(i):
    return lambda *_: (i, 0)


def _ffn_fwd(x, vec, w_in4, w_out):
    S, D = x.shape
    cs = w_in4.shape[2]
    F = 2 * cs
    tm = _tile(S, TM)

    def body(x_ref, vec_ref, wa_ref, wb_ref, wo_ref, xo_ref, h_ref, ab_ref, y_ref):
        j = pl.program_id(1)

        @pl.when(j == 0)
        def _():
            h_ref[...] = _norm_mod(x_ref[...], vec_ref[...]).astype(BF)
            y_ref[...] = jnp.zeros_like(y_ref)

        h = h_ref[...]
        a = _dot(h, wa_ref[...])
        b = _dot(h, wb_ref[...])
        ab_ref[0] = a.astype(BF)
        ab_ref[1] = b.astype(BF)
        _, s = _silu_parts(a)
        y_ref[...] += _dot((s * b).astype(BF), wo_ref[...])

        @pl.when(j == 1)
        def _():
            xo_ref[...] = x_ref[...] + (0.5 * (1.0 + vec_ref[3:4, :])) * y_ref[...]

    tok = pl.BlockSpec((tm, D), lambda i, j: (i, 0))
    return pl.pallas_call(
        body, name="ffn_fwd", grid=(S // tm, 2),
        in_specs=[tok, pl.BlockSpec((8, D), lambda i, j: (0, 0)),
                  pl.BlockSpec((None, D, cs), lambda i, j: (j, 0, 0)),
                  pl.BlockSpec((None, D, cs), lambda i, j: (j + 2, 0, 0)),
                  pl.BlockSpec((cs, D), lambda i, j: (j, 0))],
        out_specs=[tok, tok, pl.BlockSpec((2, tm, cs), lambda i, j: (0, i, j)), tok],
        out_shape=[jax.ShapeDtypeStruct((S, D), F32), jax.ShapeDtypeStruct((S, D), BF),
                   jax.ShapeDtypeStruct((2, S, F), BF), jax.ShapeDtypeStruct((S, D), F32)],
        compiler_params=_cparams(2),
    )(x, vec, w_in4, w_in4, w_out)


def _ffn_dgrad(dxo, vec, ab, w_in4, w_out):
    S, D = dxo.shape
    cs = w_in4.shape[2]
    F = 2 * cs
    tm = _tile(S, TM)

    def body(dxo_ref, vec_ref, ab_ref, wa_ref, wb_ref, wo_ref, dy_ref, u_ref, dab_ref, dh_ref):
        j = pl.program_id(1)

        @pl.when(j == 0)
        def _():
            dy_ref[...] = ((0.5 * (1.0 + vec_ref[3:4, :])) * dxo_ref[...]).astype(BF)
            dh_ref[...] = jnp.zeros_like(dh_ref)

        du = _dot_nt(dy_ref[...], wo_ref[...])
        a = ab_ref[0].astype(F32)
        b = ab_ref[1].astype(F32)
        sg, s = _silu_parts(a)
        u_ref[...] = (s * b).astype(BF)
        da = (du * b * (sg * (1.0 + a * (1.0 - sg)))).astype(BF)
        db = (du * s).astype(BF)
        dab_ref[0] = da
        dab_ref[1] = db
        dh_ref[...] += _dot_nt(da, wa_ref[...]) + _dot_nt(db, wb_ref[...])

    tok = pl.BlockSpec((tm, D), lambda i, j: (i, 0))
    return pl.pallas_call(
        body, name="ffn_dgrad", grid=(S // tm, 2),
        in_specs=[tok, pl.BlockSpec((8, D), lambda i, j: (0, 0)),
                  pl.BlockSpec((2, tm, cs), lambda i, j: (0, i, j)),
                  pl.BlockSpec((None, D, cs), lambda i, j: (j, 0, 0)),
                  pl.BlockSpec((None, D, cs), lambda i, j: (j + 2, 0, 0)),
                  pl.BlockSpec((cs, D), lambda i, j: (j, 0))],
        out_specs=[tok, pl.BlockSpec((tm, cs), lambda i, j: (i, j)),
                   pl.BlockSpec((2, tm, cs), lambda i, j: (0, i, j)), tok],
        out_shape=[jax.ShapeDtypeStruct((S, D), BF), jax.ShapeDtypeStruct((S, F), BF),
                   jax.ShapeDtypeStruct((2, S, F), BF), jax.ShapeDtypeStruct((S, D), F32)],
        compiler_params=_cparams(2),
    )(dxo, vec, ab, w_in4, w_in4, w_out)


def _mm_tn(a, b3, cs):
    K, M = a.shape
    G, _, Nb = b3.shape
    N = G * Nb
    tn = math.gcd(cs, Nb)
    tmo = M if M <= 1024 else M // 2
    tk = _tile(K, TK)

    def body(a_ref, b_ref, o_ref):
        @pl.when(pl.program_id(2) == 0)
        def _():
            o_ref[...] = jnp.zeros_like(o_ref)

        o_ref[...] += _dot_tn(a_ref[...].astype(BF), b_ref[...].astype(BF))

    return pl.pallas_call(
        body, name="wgrad_tn", grid=(M // tmo, N // tn, K // tk),
        in_specs=[pl.BlockSpec((tk, tmo), lambda m, n, k: (k, m)),
                  pl.BlockSpec((None, tk, tn), lambda m, n, k: ((n * tn) // Nb, k, ((n * tn) % Nb) // tn))],
        out_specs=pl.BlockSpec((None, tmo, tn), lambda m, n, k: ((n * tn) // cs, m, ((n * tn) % cs) // tn)),
        out_shape=jax.ShapeDtypeStruct((N // cs, M, cs), F32),
        compiler_params=_cparams(3),
    )(a, b3)


def _norm_bwd(dxo, dhs, x, y, vec, coef):
    S, D = x.shape
    tm = _tile(S, TM)
    nh = len(dhs)
    has_y = y is not None

    def body(*refs):
        dxo_ref = refs[0]
        dh_refs = refs[1:1 + nh]
        x_ref = refs[1 + nh]
        y_ref = refs[2 + nh] if has_y else None
        vec_ref, dx_ref, red_ref = refs[-3:]

        @pl.when(pl.program_id(0) == 0)
        def _():
            red_ref[...] = jnp.zeros_like(red_ref)

        xv = x_ref[...]
        dxo = dxo_ref[...]
        dh = dh_refs[0][...]
        for r in dh_refs[1:]:
            dh = dh + r[...]
        g = vec_ref[0:1, :]
        sc = vec_ref[1:2, :]
        r = _rstd(xv)
        xh = xv * r
        dhn = dh * (1.0 + sc)
        dxh = dhn * g
        dx_ref[...] = dxo + r * (dxh - xh * jnp.mean(dxh * xh, axis=-1, keepdims=True))
        red_ref[0:1, :] += jnp.sum(dhn * xh, axis=0, keepdims=True)
        red_ref[1:2, :] += jnp.sum(dh * (xh * g), axis=0, keepdims=True)
        red_ref[2:3, :] += jnp.sum(dh, axis=0, keepdims=True)
        if has_y:
            red_ref[3:4, :] += coef * jnp.sum(dxo * y_ref[...], axis=0, keepdims=True)

    tok = pl.BlockSpec((tm, D), lambda i: (i, 0))
    small = pl.BlockSpec((8, D), lambda i: (0, 0))
    ops = [dxo, *dhs, x] + ([y] if has_y else []) + [vec]
    return pl.pallas_call(
        body, name="norm_bwd", grid=(S // tm,),
        in_specs=[tok] * (len(ops) - 1) + [small],
        out_specs=[tok, small],
        out_shape=[jax.ShapeDtypeStruct((S, D), F32), jax.ShapeDtypeStruct((8, D), F32)],
        compiler_params=_cparams(1),
    )(*ops)


def _conv_fwd(x, vec, cw, w_in4, w_out):
    S, D = x.shape
    cs = w_in4.shape[2]
    tm = _tile(S, TM)

    def body(x_ref, vec_ref, cw_ref, wi_ref, wo_ref, xo_ref, h_ref, s4_ref, z_ref, y_ref, vs_ref):
        @pl.when(pl.program_id(0) == 0)
        def _():
            vs_ref[0:8, :] = jnp.zeros((8, D), F32)

        xv = x_ref[...]
        h = _norm_mod(xv, vec_ref[...]).astype(BF)
        h_ref[...] = h
        bcu = jnp.concatenate([_dot(h, wi_ref[q]) for q in range(4)], axis=1)
        bg, cg, u = bcu[:, :D], bcu[:, D:2 * D], bcu[:, 2 * D:]
        v = cg * u
        vs_ref[8:8 + tm, :] = v
        conv = cw_ref[0:1, :] * vs_ref[pl.ds(6, tm), :] + cw_ref[1:2, :] * vs_ref[pl.ds(7, tm), :] + cw_ref[2:3, :] * v
        vs_ref[0:8, :] = vs_ref[tm:tm + 8, :]
        z = (bg * conv).astype(BF)
        s4_ref[0] = bg.astype(BF)
        s4_ref[1] = cg.astype(BF)
        s4_ref[2] = u.astype(BF)
        s4_ref[3] = conv.astype(BF)
        z_ref[...] = z
        y = _dot(z, wo_ref[...])
        y_ref[...] = y
        xo_ref[...] = xv + (1.0 + vec_ref[3:4, :]) * y

    tok = pl.BlockSpec((tm, D), lambda i: (i, 0))
    small = pl.BlockSpec((8, D), lambda i: (0, 0))
    return pl.pallas_call(
        body, name="conv_fwd", grid=(S // tm,),
        in_specs=[tok, small, small, pl.BlockSpec((4, D, cs), lambda i: (0, 0, 0)),
                  pl.BlockSpec((D, D), lambda i: (0, 0))],
        out_specs=[tok, tok, pl.BlockSpec((4, tm, D), lambda i: (0, i, 0)), tok, tok],
        out_shape=[jax.ShapeDtypeStruct((S, D), F32), jax.ShapeDtypeStruct((S, D), BF),
                   jax.ShapeDtypeStruct((4, S, D), BF), jax.ShapeDtypeStruct((S, D), BF),
                   jax.ShapeDtypeStruct((S, D), F32)],
        scratch_shapes=[pltpu.VMEM((tm + 8, D), F32)],
        compiler_params=_cparams(1),
    )(x, vec, cw, w_in4, w_out)


def _conv_dgrad(dxo, vec, cw, s4, w_in4, w_out):
    S, D = dxo.shape
    cs = w_in4.shape[2]
    tm = _tile(S, TM)
    nt = S // tm

    def body(dxo_ref, vec_ref, cw_ref, s4_ref, wi_ref, wo_ref, dy_ref, d3_ref, dh_ref, dcw_ref, ds_ref):
        @pl.when(pl.program_id(0) == 0)
        def _():
            ds_ref[tm:tm + 8, :] = jnp.zeros((8, D), F32)
            dcw_ref[...] = jnp.zeros_like(dcw_ref)

        dy = ((1.0 + vec_ref[3:4, :]) * dxo_ref[...]).astype(BF)
        dy_ref[...] = dy
        dz = _dot_nt(dy, wo_ref[...])
        bg = s4_ref[0].astype(F32)
        cg = s4_ref[1].astype(F32)
        u = s4_ref[2].astype(F32)
        conv = s4_ref[3].astype(F32)
        dbg = dz * conv
        dconv = dz * bg
        ds_ref[0:tm, :] = dconv
        d1 = ds_ref[pl.ds(1, tm), :]
        d2 = ds_ref[pl.ds(2, tm), :]
        ds_ref[tm:tm + 8, :] = ds_ref[0:8, :]
        dv = cw_ref[2:3, :] * dconv + cw_ref[1:2, :] * d1 + cw_ref[0:1, :] * d2
        v = cg * u
        dcw_ref[0:1, :] += jnp.sum(d2 * v, axis=0, keepdims=True)
        dcw_ref[1:2, :] += jnp.sum(d1 * v, axis=0, keepdims=True)
        dcw_ref[2:3, :] += jnp.sum(dconv * v, axis=0, keepdims=True)
        dbcu = jnp.concatenate([dbg, dv * u, dv * cg], axis=1).astype(BF)
        d3_ref[0] = dbcu[:, :D]
        d3_ref[1] = dbcu[:, D:2 * D]
        d3_ref[2] = dbcu[:, 2 * D:]
        dh = _dot_nt(dbcu[:, 0:cs], wi_ref[0])
        for q in range(1, 4):
            dh = dh + _dot_nt(dbcu[:, q * cs:(q + 1) * cs], wi_ref[q])
        dh_ref[...] = dh

    tok = pl.BlockSpec((tm, D), lambda i: (nt - 1 - i, 0))
    small = pl.BlockSpec((8, D), lambda i: (0, 0))
    return pl.pallas_call(
        body, name="conv_dgrad", grid=(nt,),
        in_specs=[tok, small, small, pl.BlockSpec((4, tm, D), lambda i: (0, nt - 1 - i, 0)),
                  pl.BlockSpec((4, D, cs), lambda i: (0, 0, 0)), pl.BlockSpec((D, D), lambda i: (0, 0))],
        out_specs=[tok, pl.BlockSpec((3, tm, D), lambda i: (0, nt - 1 - i, 0)), tok, small],
        out_shape=[jax.ShapeDtypeStruct((S, D), BF), jax.ShapeDtypeStruct((3, S, D), BF),
                   jax.ShapeDtypeStruct((S, D), F32), jax.ShapeDtypeStruct((8, D), F32)],
        scratch_shapes=[pltpu.VMEM((tm + 8, D), F32)],
        compiler_params=_cparams(1),
    )(dxo, vec, cw, s4, w_in4, w_out)


def _rope_tables(positions):
    S = positions.shape[-1]
    inv = ROPE_THETA ** (-jnp.arange(0, ROPE_DIM, 2, dtype=F32) / ROPE_DIM)
    ang = positions.reshape(S, 1).astype(F32) * inv
    cos = jnp.tile(jnp.cos(ang), (1, LANES // 8))
    sin = jnp.tile(jnp.sin(ang), (1, LANES // 8))
    l64 = jnp.arange(LANES) % HEAD_DIM
    return jnp.stack([jnp.where(l64 < ROPE_DIM, cos, 1.0),
                      jnp.where(l64 < ROPE_DIM // 2, -sin, 0.0),
                      jnp.where((l64 >= ROPE_DIM // 2) & (l64 < ROPE_DIM), sin, 0.0)])


def _rope(t, tab_ref):
    return t * tab_ref[0] + pltpu.roll(t, LANES - 8, 1) * tab_ref[1] + pltpu.roll(t, 8, 1) * tab_ref[2]


def _rope_t(d, tab_ref):
    return d * tab_ref[0] + pltpu.roll(d * tab_ref[1], 8, 1) + pltpu.roll(d * tab_ref[2], LANES - 8, 1)


def _proj_fwd(x, vec, w4, tabs, n_rope):
    S, D = x.shape
    cs = w4.shape[2]
    N = 4 * cs
    tm = _tile(S, TM)

    def body(x_ref, vec_ref, w_ref, tab_ref, h_ref, o_ref):
        h = _norm_mod(x_ref[...], vec_ref[...]).astype(BF)
        h_ref[...] = h
        per = cs // LANES
        for q in range(4):
            acc = _dot(h, w_ref[q])
            cols = [acc[:, ch * LANES:(ch + 1) * LANES] for ch in range(per)]
            cols = [_rope(t, tab_ref) if q * per + ch < n_rope else t for ch, t in enumerate(cols)]
            o_ref[:, q * cs:(q + 1) * cs] = jnp.concatenate(cols, axis=1).astype(BF)

    return pl.pallas_call(
        body, name="proj_fwd", grid=(S // tm,),
        in_specs=[pl.BlockSpec((tm, D), lambda i: (i, 0)), pl.BlockSpec((8, D), lambda i: (0, 0)),
                  pl.BlockSpec((4, D, cs), lambda i: (0, 0, 0)), pl.BlockSpec((3, tm, LANES), lambda i: (0, i, 0))],
        out_specs=[pl.BlockSpec((tm, D), lambda i: (i, 0)), pl.BlockSpec((tm, N), lambda i: (i, 0))],
        out_shape=[jax.ShapeDtypeStruct((S, D), BF), jax.ShapeDtypeStruct((S, N), BF)],
        compiler_params=_cparams(1),
    )(x, vec, w4, tabs)


def _proj_dgrad(parts, tabs, w4, n_rope_groups):
    S = parts[0][0].shape[0]
    cs, D = w4.shape[2], w4.shape[1]
    N = 4 * cs
    tm = _tile(S, TM)
    flat = [p for grp in parts for p in grp]
    counts = [len(grp) for grp in parts]

    def body(*refs):
        prefs = refs[:len(flat)]
        tab_ref, w_ref, dz_ref, dh_ref = refs[len(flat):]
        k = 0
        for j, cnt in enumerate(counts):
            z = prefs[k][...]
            for r in prefs[k + 1:k + cnt]:
                z = z + r[...]
            k += cnt
            if j < n_rope_groups:
                z = jnp.concatenate([_rope_t(z[:, ch * LANES:(ch + 1) * LANES], tab_ref)
                                     for ch in range(GROUP_W // LANES)], axis=1)
            dz_ref[:, j * GROUP_W:(j + 1) * GROUP_W] = z.astype(BF)
        dh = _dot_nt(dz_ref[:, 0:cs], w_ref[0])
        for q in range(1, 4):
            dh = dh + _dot_nt(dz_ref[:, q * cs:(q + 1) * cs], w_ref[q])
        dh_ref[...] = dh

    grp = pl.BlockSpec((tm, GROUP_W), lambda i: (i, 0))
    return pl.pallas_call(
        body, name="proj_dgrad", grid=(S // tm,),
        in_specs=[grp] * len(flat) + [pl.BlockSpec((3, tm, LANES), lambda i: (0, i, 0)),
                                      pl.BlockSpec((4, D, cs), lambda i: (0, 0, 0))],
        out_specs=[pl.BlockSpec((tm, N), lambda i: (i, 0)), pl.BlockSpec((tm, D), lambda i: (i, 0))],
        out_shape=[jax.ShapeDtypeStruct((S, N), BF), jax.ShapeDtypeStruct((S, D), F32)],
        compiler_params=_cparams(1),
    )(*flat, tabs, w4)


def _band_masks():
    qi = lax.broadcasted_iota(jnp.int32, (BAND, BAND), 0)
    kj = lax.broadcasted_iota(jnp.int32, (BAND, BAND), 1)
    return kj <= qi, kj >= qi


def _head(ref, h):
    return ref[:, h * HEAD_DIM:(h + 1) * HEAD_DIM]


def _attn_fwd(q, kv, g):
    d = DILATIONS[g]
    S = q.shape[0]
    sd = S // d
    nb = sd // BAND
    qw, kw = q.shape[1] // GROUP_W, kv.shape[1] // GROUP_W

    def body(q_ref, kp_ref, kc_ref, vp_ref, vc_ref, o_ref, l_ref):
        same, prev = _band_masks()
        prev = jnp.logical_and(prev, pl.program_id(1) > 0)
        for h in range(HEADS_PER_GROUP):
            qh = _head(q_ref, h)
            sc = jnp.where(same, _dot_nt(qh, _head(kc_ref, h)) * SM_SCALE, NEG)
            sp = jnp.where(prev, _dot_nt(qh, _head(kp_ref, h)) * SM_SCALE, NEG)
            m = jnp.maximum(jnp.max(sc, axis=-1, keepdims=True), jnp.max(sp, axis=-1, keepdims=True))
            pc = jnp.exp(sc - m)
            pp = jnp.exp(sp - m)
            den = jnp.sum(pc, axis=-1, keepdims=True) + jnp.sum(pp, axis=-1, keepdims=True)
            o = (_dot(pc.astype(BF), _head(vc_ref, h)) + _dot(pp.astype(BF), _head(vp_ref, h))) / den
            o_ref[:, h * HEAD_DIM:(h + 1) * HEAD_DIM] = o
            l_ref[:, h * HEAD_DIM:(h + 1) * HEAD_DIM] = jnp.broadcast_to(m + jnp.log(den), (BAND, HEAD_DIM))

    blk = (BAND, GROUP_W)
    out = pl.BlockSpec(blk, lambda r, i: (i, r))
    o, lse = pl.pallas_call(
        body, name=f"attn_fwd_g{g}", grid=(d, nb),
        in_specs=[pl.BlockSpec(blk, lambda r, i: (i, r * qw + g)),
                  pl.BlockSpec(blk, lambda r, i: (jnp.maximum(i - 1, 0), r * kw + g)),
                  pl.BlockSpec(blk, lambda r, i: (i, r * kw + g)),
                  pl.BlockSpec(blk, lambda r, i: (jnp.maximum(i - 1, 0), r * kw + qw + g)),
                  pl.BlockSpec(blk, lambda r, i: (i, r * kw + qw + g))],
        out_specs=[out, out],
        out_shape=[jax.ShapeDtypeStruct((sd, d * GROUP_W), F32)] * 2,
        compiler_params=_cparams(2),
    )(q.reshape(sd, d * q.shape[1]), *([kv.reshape(sd, d * kv.shape[1])] * 4))
    return o.reshape(S, GROUP_W), lse.reshape(S, GROUP_W)


def _attn_out(os, ls, x, vec, wo4):
    S, D = x.shape
    cs = wo4.shape[2]
    tm = _tile(S, TM)

    def body(o0, l0, o1, l1, o2, l2, x_ref, vec_ref, wo_ref, mix_ref, lj_ref, y_ref, xo_ref):
        la, lb, lc = l0[...], l1[...], l2[...]
        mx = jnp.maximum(jnp.maximum(la, lb), lc)
        ea, eb, ec = jnp.exp(la - mx), jnp.exp(lb - mx), jnp.exp(lc - mx)
        den = ea + eb + ec
        mix = (ea * o0[...] + eb * o1[...] + ec * o2[...]) / den
        mix_ref[...] = mix
        lj_ref[...] = mx + jnp.log(den)
        mb = mix.astype(BF)
        y = jnp.concatenate([_dot(mb, wo_ref[q]) for q in range(4)], axis=1)
        y_ref[...] = y
        xo_ref[...] = x_ref[...] + (1.0 + vec_ref[3:4, :]) * y

    grp = pl.BlockSpec((tm, GROUP_W), lambda i: (i, 0))
    tok = pl.BlockSpec((tm, D), lambda i: (i, 0))
    return pl.pallas_call(
        body, name="attn_out", grid=(S // tm,),
        in_specs=[grp] * 6 + [tok, pl.BlockSpec((8, D), lambda i: (0, 0)),
                              pl.BlockSpec((4, GROUP_W, cs), lambda i: (0, 0, 0))],
        out_specs=[grp, grp, tok, tok],
        out_shape=[jax.ShapeDtypeStruct((S, GROUP_W), F32)] * 2 + [jax.ShapeDtypeStruct((S, D), F32)] * 2,
        compiler_params=_cparams(1),
    )(os[0], ls[0], os[1], ls[1], os[2], ls[2], x, vec, wo4)


def _attn_out_dgrad(dxo, vec, wo4):
    S, D = dxo.shape
    cs = wo4.shape[2]
    tm = _tile(S, TM)

    def body(dxo_ref, vec_ref, wo_ref, dy_ref, dm_ref):
        dy = ((1.0 + vec_ref[3:4, :]) * dxo_ref[...]).astype(BF)
        dy_ref[...] = dy
        dm = _dot_nt(dy[:, 0:cs], wo_ref[0])
        for q in range(1, 4):
            dm = dm + _dot_nt(dy[:, q * cs:(q + 1) * cs], wo_ref[q])
        dm_ref[...] = dm

    tok = pl.BlockSpec((tm, D), lambda i: (i, 0))
    return pl.pallas_call(
        body, name="attn_out_dgrad", grid=(S // tm,),
        in_specs=[tok, pl.BlockSpec((8, D), lambda i: (0, 0)), pl.BlockSpec((4, GROUP_W, cs), lambda i: (0, 0, 0))],
        out_specs=[tok, pl.BlockSpec((tm, GROUP_W), lambda i: (i, 0))],
        out_shape=[jax.ShapeDtypeStruct((S, D), BF), jax.ShapeDtypeStruct((S, GROUP_W), F32)],
        compiler_params=_cparams(1),
    )(dxo, vec, wo4)


def _prob_and_ds(qh, kh, vh, doh, oh, lrow, mask):
    s = jnp.where(mask, _dot_nt(qh, kh) * SM_SCALE, NEG)
    p = jnp.exp(s - lrow)
    delta = jnp.sum(doh * oh, axis=-1, keepdims=True)
    dp = _dot_nt(doh.astype(BF), vh)
    return p, p * (dp - delta) * SM_SCALE


def _attn_dq(q, kv, dmix, mix, lj, g):
    d = DILATIONS[g]
    S = q.shape[0]
    sd = S // d
    nb = sd // BAND
    qw, kw = q.shape[1] // GROUP_W, kv.shape[1] // GROUP_W

    def body(q_ref, kp_ref, kc_ref, vp_ref, vc_ref, do_ref, o_ref, l_ref, dq_ref):
        same, prev = _band_masks()
        prev = jnp.logical_and(prev, pl.program_id(1) > 0)
        for h in range(HEADS_PER_GROUP):
            qh, doh, oh = _head(q_ref, h), _head(do_ref, h), _head(o_ref, h)
            lrow = l_ref[:, h * HEAD_DIM:h * HEAD_DIM + 1]
            kc, kp = _head(kc_ref, h), _head(kp_ref, h)
            _, dsc = _prob_and_ds(qh, kc, _head(vc_ref, h), doh, oh, lrow, same)
            _, dsp = _prob_and_ds(qh, kp, _head(vp_ref, h), doh, oh, lrow, prev)
            dq_ref[:, h * HEAD_DIM:(h + 1) * HEAD_DIM] = _dot(dsc.astype(BF), kc) + _dot(dsp.astype(BF), kp)

    blk = (BAND, GROUP_W)
    res = pl.BlockSpec(blk, lambda r, i: (i, r))
    kvd = kv.reshape(sd, d * kv.shape[1])
    dq = pl.pallas_call(
        body, name=f"attn_dq_g{g}", grid=(d, nb),
        in_specs=[pl.BlockSpec(blk, lambda r, i: (i, r * qw + g)),
                  pl.BlockSpec(blk, lambda r, i: (jnp.maximum(i - 1, 0), r * kw + g)),
                  pl.BlockSpec(blk, lambda r, i: (i, r * kw + g)),
                  pl.BlockSpec(blk, lambda r, i: (jnp.maximum(i - 1, 0), r * kw + qw + g)),
                  pl.BlockSpec(blk, lambda r, i: (i, r * kw + qw + g)),
                  res, res, res],
        out_specs=res,
        out_shape=jax.ShapeDtypeStruct((sd, d * GROUP_W), F32),
        compiler_params=_cparams(2),
    )(q.reshape(sd, d * q.shape[1]), kvd, kvd, kvd, kvd, dmix.reshape(sd, d * GROUP_W),
      mix.reshape(sd, d * GROUP_W), lj.reshape(sd, d * GROUP_W))
    return dq.reshape(S, GROUP_W)


def _attn_dkv(q, kv, dmix, mix, lj, g):
    d = DILATIONS[g]
    S = q.shape[0]
    sd = S // d
    nb = sd // BAND
    qw, kw = q.shape[1] // GROUP_W, kv.shape[1] // GROUP_W

    def body(k_ref, v_ref, qs_ref, qn_ref, dos_ref, don_ref, os_ref, on_ref, ls_ref, ln_ref, dk_ref, dv_ref):
        same, nxt = _band_masks()
        nxt = jnp.logical_and(nxt, pl.program_id(1) < nb - 1)
        for h in range(HEADS_PER_GROUP):
            kh, vh = _head(k_ref, h), _head(v_ref, h)
            dk = jnp.zeros((BAND, HEAD_DIM), F32)
            dv = jnp.zeros((BAND, HEAD_DIM), F32)
            for q_ref, do_ref, o_ref, l_ref, mask in ((qs_ref, dos_ref, os_ref, ls_ref, same),
                                                       (qn_ref, don_ref, on_ref, ln_ref, nxt)):
                qh, doh = _head(q_ref, h), _head(do_ref, h)
                p, ds = _prob_and_ds(qh, kh, vh, doh, _head(o_ref, h), l_ref[:, h * HEAD_DIM:h * HEAD_DIM + 1], mask)
                dv = dv + _dot_tn(p.astype(BF), doh.astype(BF))
                dk = dk + _dot_tn(ds.astype(BF), qh)
            dk_ref[:, h * HEAD_DIM:(h + 1) * HEAD_DIM] = dk
            dv_ref[:, h * HEAD_DIM:(h + 1) * HEAD_DIM] = dv

    blk = (BAND, GROUP_W)
    cur = pl.BlockSpec(blk, lambda r, i: (i, r))
    nxt_spec = pl.BlockSpec(blk, lambda r, i: (jnp.minimum(i + 1, nb - 1), r))
    qd = q.reshape(sd, d * q.shape[1])
    kvd = kv.reshape(sd, d * kv.shape[1])
    dod, od, ld = (t.reshape(sd, d * GROUP_W) for t in (dmix, mix, lj))
    dk, dv = pl.pallas_call(
        body, name=f"attn_dkv_g{g}", grid=(d, nb),
        in_specs=[pl.BlockSpec(blk, lambda r, i: (i, r * kw + g)),
                  pl.BlockSpec(blk, lambda r, i: (i, r * kw + qw + g)),
                  pl.BlockSpec(blk, lambda r, i: (i, r * qw + g)),
                  pl.BlockSpec(blk, lambda r, i: (jnp.minimum(i + 1, nb - 1), r * qw + g)),
                  cur, nxt_spec, cur, nxt_spec, cur, nxt_spec],
        out_specs=[cur, cur],
        out_shape=[jax.ShapeDtypeStruct((sd, d * GROUP_W), F32)] * 2,
        compiler_params=_cparams(2),
    )(kvd, kvd, qd, qd, dod, dod, od, od, ld, ld)
    return dk.reshape(S, GROUP_W), dv.reshape(S, GROUP_W)


def _final_loss(x, gvec, tgt):
    S, D = x.shape
    tm = _tile(S, TM)

    def body(x_ref, g_ref, t_ref, dx_ref, red_ref):
        @pl.when(pl.program_id(0) == 0)
        def _():
            red_ref[...] = jnp.zeros_like(red_ref)

        xv = x_ref[...]
        g = g_ref[0:1, :]
        r = _rstd(xv)
        xh = xv * r
        err = xh * g - t_ref[...]
        dy = err * (1.0 / D)
        dxh = dy * g
        dx_ref[...] = r * (dxh - xh * jnp.mean(dxh * xh, axis=-1, keepdims=True))
        red_ref[0:1, :] += jnp.sum(dy * xh, axis=0, keepdims=True)
        red_ref[1:2, :] += jnp.sum(err * err, axis=0, keepdims=True)

    tok = pl.BlockSpec((tm, D), lambda i: (i, 0))
    small = pl.BlockSpec((8, D), lambda i: (0, 0))
    return pl.pallas_call(
        body, name="final_loss", grid=(S // tm,),
        in_specs=[tok, small, tok], out_specs=[tok, small],
        out_shape=[jax.ShapeDtypeStruct((S, D), F32), jax.ShapeDtypeStruct((8, D), F32)],
        compiler_params=_cparams(1),
    )(x, gvec, tgt)


def _adamw(w, g, m, v):
    shape = w.shape
    C = shape[-1]
    R = w.size // C
    tr = 256 if R % 256 == 0 else R

    def body(w_ref, g_ref, m_ref, v_ref, d_ref, nm_ref, nv_ref):
        gv = g_ref[...]
        nm = ADAM_B1 * m_ref[...] + (1.0 - ADAM_B1) * gv
        nv = ADAM_B2 * v_ref[...] + (1.0 - ADAM_B2) * (gv * gv)
        m_hat = nm / (1.0 - ADAM_B1 ** ADAM_STEP)
        v_hat = nv / (1.0 - ADAM_B2 ** ADAM_STEP)
        d_ref[...] = -ADAM_LR * (m_hat / (jnp.sqrt(v_hat) + ADAM_EPS) + ADAM_WD * w_ref[...])
        nm_ref[...] = nm
        nv_ref[...] = nv

    spec = pl.BlockSpec((tr, C), lambda i: (i, 0))
    outs = pl.pallas_call(
        body, name="adamw", grid=(R // tr,),
        in_specs=[spec] * 4, out_specs=[spec] * 3,
        out_shape=[jax.ShapeDtypeStruct((R, C), F32)] * 3,
        compiler_params=_cparams(1),
    )(*(t.reshape(R, C) for t in (w, g, m, v)))
    return tuple(o.reshape(shape) for o in outs)


def _ada_fwd(c_all, w, b):
    L, D, N = w.shape
    tn = 768 if N % 768 == 0 else _tile(N, 512)

    def body(c_ref, w_ref, b_ref, o_ref):
        cv = c_ref[...]
        cond = (cv * jax.nn.sigmoid(cv)).astype(BF)
        o_ref[...] = _dot(cond, w_ref[...].astype(BF)) + b_ref[...]

    return pl.pallas_call(
        body, name="ada_fwd", grid=(L, N // tn),
        in_specs=[pl.BlockSpec((8, D), lambda l, n: (0, 0)), pl.BlockSpec((None, D, tn), lambda l, n: (l, 0, n)),
                  pl.BlockSpec((None, 1, tn), lambda l, n: (l, 0, n))],
        out_specs=pl.BlockSpec((None, 8, tn), lambda l, n: (l, 0, n)),
        out_shape=jax.ShapeDtypeStruct((L, 8, N), F32),
        compiler_params=_cparams(2),
    )(c_all, w, b)


def _ada_wgrad(c_all_t, dm):
    D = c_all_t.shape[0]
    L, _, N = dm.shape
    tn = 256

    def body(c_ref, dm_ref, o_ref):
        cv = c_ref[...]
        cond = cv * jax.nn.sigmoid(cv)
        acc = cond[:, 0:1] * dm_ref[0:1, :]
        for b in range(1, 8):
            acc = acc + cond[:, b:b + 1] * dm_ref[b:b + 1, :]
        o_ref[...] = acc

    return pl.pallas_call(
        body, name="ada_wgrad", grid=(L, N // tn),
        in_specs=[pl.BlockSpec((D, 8), lambda l, n: (0, 0)), pl.BlockSpec((None, 8, tn), lambda l, n: (l, 0, n))],
        out_specs=pl.BlockSpec((None, D, tn), lambda l, n: (l, 0, n)),
        out_shape=jax.ShapeDtypeStruct((L, D, N), F32),
        compiler_params=_cparams(2),
    )(c_all_t, dm)


def _sum8(g):
    _, R, C = g.shape

    def body(g_ref, o_ref):
        acc = g_ref[0]
        for b in range(1, 8):
            acc = acc + g_ref[b]
        o_ref[...] = acc

    return pl.pallas_call(body, name="sum8", out_shape=jax.ShapeDtypeStruct((R, C), F32),
                          in_specs=[pl.BlockSpec(memory_space=pltpu.VMEM)],
                          out_specs=pl.BlockSpec(memory_space=pltpu.VMEM))(g)


def _pair_add(g4, recv, cidx):
    _, _, ha, B = g4.shape
    tr = _row_tile(ha)

    def body(c_ref, g_ref, r_ref, o_ref):
        o_ref[...] = (g_ref[...] + r_ref[...]).astype(BF)

    return pl.pallas_call(
        body, name="pair_add",
        grid_spec=pltpu.PrefetchScalarGridSpec(
            num_scalar_prefetch=1, grid=(4, ha // tr),
            in_specs=[pl.BlockSpec((None, None, tr, B), lambda s, i, c: (s, c[0], i, 0)),
                      pl.BlockSpec((None, tr, B), lambda s, i, c: (s, i, 0))],
            out_specs=pl.BlockSpec((None, tr, B), lambda s, i, c: (s, i, 0))),
        out_shape=jax.ShapeDtypeStruct((4, ha, B), BF),
        compiler_params=_cparams(2),
    )(cidx, g4, recv)


def _sum_partials(part, recv, gbuf, n_layers, l, place):
    _, ha, B = part.shape
    tr = _row_tile(ha)

    def body(pc_ref, p_ref, r_ref, *rest):
        o_ref = rest[-1]
        o_ref[...] = ((p_ref[...].astype(F32) + r_ref[0].astype(F32)) + r_ref[1].astype(F32)) + r_ref[2].astype(F32)

    in_specs = [pl.BlockSpec((None, tr, B), lambda i, pc: (pc[0], i, 0)), pl.BlockSpec((3, tr, B), lambda i, pc: (0, i, 0))]
    ops = [part, recv]
    if gbuf is not None:
        in_specs.append(ANY)
        ops.append(gbuf)
    return pl.pallas_call(
        body, name="sum_partials",
        grid_spec=pltpu.PrefetchScalarGridSpec(
            num_scalar_prefetch=1, grid=(ha // tr,), in_specs=in_specs,
            out_specs=pl.BlockSpec((None, None, tr, B), lambda i, pc: (l, pc[1], i, 0))),
        out_shape=jax.ShapeDtypeStruct((n_layers, 2, ha, B), F32),
        input_output_aliases={} if gbuf is None else {3: 0},
        compiler_params=_cparams(1),
    )(place, *ops)


def _cast_shard(shard, l, place):
    _, _, ha, B = shard.shape
    tr = _row_tile(ha)

    def body(pc_ref, s_ref, o_ref):
        o_ref[...] = s_ref[...].astype(BF)

    return pl.pallas_call(
        body, name="cast_shard",
        grid_spec=pltpu.PrefetchScalarGridSpec(
            num_scalar_prefetch=1, grid=(2, ha // tr),
            in_specs=[pl.BlockSpec((None, None, tr, B), lambda h, i, pc: (l, h, i, 0))],
            out_specs=pl.BlockSpec((None, None, tr, B), lambda h, i, pc: (pc[0], h, i, 0))),
        out_shape=jax.ShapeDtypeStruct((4, 2, ha, B), BF),
        compiler_params=_cparams(2),
    )(place, shard)


ANY = pl.BlockSpec(memory_space=pl.ANY)


def _place():
    x, y, c = (lax.axis_index(a) for a in AXES)
    chips = [(1 - x, y), (x, 1 - y), (1 - x, 1 - y)]
    return x, y, c, 2 * x + y, chips, [2 * cx + cy for cx, cy in chips]


def _allgather_weights(bufs):
    n = len(bufs)

    def body(*refs):
        outs = refs[n:2 * n]
        ssem, rsem = refs[2 * n:]
        x, y, c, p, chips, qs = _place()
        sib = (x, y, 1 - c)

        def rcopy(blk, k, j, dev):
            return pltpu.make_async_remote_copy(src_ref=blk, dst_ref=blk, send_sem=ssem.at[k, j], recv_sem=rsem.at[k, j],
                                                device_id=dev, device_id_type=MESH)

        sent = []
        for k in range(n):
            for j in range(3):
                cp = rcopy(outs[k].at[p, c], k, j, (*chips[j], c))
                cp.start()
                sent.append(cp)
        for k in range(n):
            for j in range(3):
                here = outs[k].at[qs[j], c]
                rcopy(here, k, j, sib).wait_recv()
                cp = rcopy(here, k, 3 + j, sib)
                cp.start()
                sent.append(cp)
        for k in range(n):
            for j in range(3):
                rcopy(outs[k].at[qs[j], 1 - c], k, 3 + j, sib).wait_recv()
        for cp in sent:
            cp.wait_send()

    return pl.pallas_call(
        body, name="allgather_weights", in_specs=[ANY] * n, out_specs=[ANY] * n,
        out_shape=[jax.ShapeDtypeStruct(b.shape, b.dtype) for b in bufs],
        input_output_aliases={k: k for k in range(n)},
        scratch_shapes=[pltpu.SemaphoreType.DMA((n, 6)), pltpu.SemaphoreType.DMA((n, 6))],
    )(*bufs)


def _exchange_halves(grads):
    n = len(grads)

    def body(*refs):
        ins, outs = refs[:n], refs[n:2 * n]
        ssem, rsem = refs[2 * n:]
        x, y, c, *_ = _place()
        sib = (x, y, 1 - c)
        cps = []
        for k in range(n):
            cp = pltpu.make_async_remote_copy(src_ref=ins[k].at[:, 1 - c], dst_ref=outs[k], send_sem=ssem.at[k],
                                              recv_sem=rsem.at[k], device_id=sib, device_id_type=MESH)
            cp.start()
            cps.append(cp)
        for cp in cps:
            cp.wait()

    return pl.pallas_call(
        body, name="grad_exchange_halves", in_specs=[ANY] * n, out_specs=[ANY] * n,
        out_shape=[jax.ShapeDtypeStruct((4,) + g.shape[2:], F32) for g in grads],
        scratch_shapes=[pltpu.SemaphoreType.DMA((n,)), pltpu.SemaphoreType.DMA((n,))],
    )(*grads)


def _scatter_partials(parts):
    n = len(parts)

    def body(*refs):
        ins, outs = refs[:n], refs[n:2 * n]
        ssem, rsem = refs[2 * n:]
        x, y, c, p, chips, qs = _place()

        def rcopy(k, j):
            return pltpu.make_async_remote_copy(src_ref=ins[k].at[qs[j]], dst_ref=outs[k].at[j], send_sem=ssem.at[k, j],
                                                recv_sem=rsem.at[k, j], device_id=(*chips[j], c), device_id_type=MESH)

        for k in range(n):
            for j in range(3):
                rcopy(k, j).start()
        for k in range(n):
            for j in range(3):
                rcopy(k, j).wait()

    return pl.pallas_call(
        body, name="grad_scatter_partials", in_specs=[ANY] * n, out_specs=[ANY] * n,
        out_shape=[jax.ShapeDtypeStruct((3,) + t.shape[1:], BF) for t in parts],
        scratch_shapes=[pltpu.SemaphoreType.DMA((n, 3)), pltpu.SemaphoreType.DMA((n, 3))],
    )(*parts)


def _share_halves(gbufs):
    n = len(gbufs)

    def body(*refs):
        outs = refs[n:2 * n]
        ssem, rsem = refs[2 * n:]
        x, y, c, *_ = _place()
        sib = (x, y, 1 - c)
        sent = []
        for w in range(n):
            cp = pltpu.make_async_remote_copy(src_ref=outs[w].at[:, c], dst_ref=outs[w].at[:, c], send_sem=ssem.at[w],
                                              recv_sem=rsem.at[w], device_id=sib, device_id_type=MESH)
            cp.start()
            sent.append(cp)
        for w in range(n):
            there = outs[w].at[:, 1 - c]
            pltpu.make_async_remote_copy(src_ref=there, dst_ref=there, send_sem=ssem.at[w], recv_sem=rsem.at[w],
                                         device_id=sib, device_id_type=MESH).wait_recv()
        for cp in sent:
            cp.wait_send()

    return pl.pallas_call(
        body, name="grad_share_halves", in_specs=[ANY] * n, out_specs=[ANY] * n,
        out_shape=[jax.ShapeDtypeStruct(g.shape, g.dtype) for g in gbufs],
        input_output_aliases={k: k for k in range(n)},
        scratch_shapes=[pltpu.SemaphoreType.DMA((n,)), pltpu.SemaphoreType.DMA((n,))],
    )(*gbufs)


def _gather8(v):
    R, C = v.shape

    def body(v_ref, o_ref, ssem, rsem):
        x, y, c, *_ = _place()
        me = 4 * x + 2 * y + c
        o_ref[me] = v_ref[...]
        cps = []
        for k in range(1, 8):
            fx, fy, fc = (k >> 2) & 1, (k >> 1) & 1, k & 1
            peer = (x ^ fx, y ^ fy, c ^ fc)
            cp = pltpu.make_async_remote_copy(src_ref=v_ref, dst_ref=o_ref.at[me], send_sem=ssem.at[k - 1],
                                              recv_sem=rsem.at[k - 1], device_id=peer, device_id_type=MESH)
            cp.start()
            cps.append((cp, 4 * peer[0] + 2 * peer[1] + peer[2]))
        for k, (cp, slot) in enumerate(cps):
            there = o_ref.at[slot]
            pltpu.make_async_remote_copy(src_ref=there, dst_ref=there, send_sem=ssem.at[k], recv_sem=rsem.at[k],
                                         device_id=(x, y, c), device_id_type=MESH).wait_recv()
        for cp, _ in cps:
            cp.wait_send()

    vm = pl.BlockSpec(memory_space=pltpu.VMEM)
    return pl.pallas_call(
        body, name="gather8", in_specs=[vm], out_specs=vm, out_shape=jax.ShapeDtypeStruct((8, R, C), F32),
        scratch_shapes=[pltpu.SemaphoreType.DMA((7,)), pltpu.SemaphoreType.DMA((7,))],
    )(v)


def _mods_to_owner(ms):
    _, R, C = ms.shape

    def body(m_ref, o_ref, ssem, rsem):
        x, y, c, p, chips, qs = _place()
        o_ref[p] = m_ref[2 * p + c]
        cps = []
        for j in range(3):
            cp = pltpu.make_async_remote_copy(src_ref=m_ref.at[2 * qs[j] + c], dst_ref=o_ref.at[p], send_sem=ssem.at[j],
                                              recv_sem=rsem.at[j], device_id=(*chips[j], c), device_id_type=MESH)
            cp.start()
            cps.append(cp)
        for j in range(3):
            there = o_ref.at[qs[j]]
            pltpu.make_async_remote_copy(src_ref=there, dst_ref=there, send_sem=ssem.at[j], recv_sem=rsem.at[j],
                                         device_id=(x, y, c), device_id_type=MESH).wait_recv()
        for cp in cps:
            cp.wait_send()

    vm = pl.BlockSpec(memory_space=pltpu.VMEM)
    return pl.pallas_call(
        body, name="mods_to_owner", in_specs=[vm], out_specs=vm, out_shape=jax.ShapeDtypeStruct((4, R, C), F32),
        scratch_shapes=[pltpu.SemaphoreType.DMA((3,)), pltpu.SemaphoreType.DMA((3,))],
    )(ms)


def _vec(*rows):
    D = rows[0].shape[-1]
    rows = [r.reshape(1, D) for r in rows]
    return jnp.concatenate(rows + [jnp.zeros((8 - len(rows), D), F32)], axis=0)


def _local_step(x, tgt, tabs, norm_g, conv_w, kv_norm_g, final_norm_g, mods, kvmods, W):
    S, D = x.shape
    n_layers, n_conv = norm_g.shape[0], conv_w.shape[0]
    md = mods.reshape(n_layers, 9, D)

    def vec_of(l, k):
        return _vec(norm_g[l, k], md[l, 3 * k + 1], md[l, 3 * k], md[l, 3 * k + 2])

    vec_kv = _vec(kv_norm_g, kvmods[D:], kvmods[:D])
    saved = {}
    xc = x
    kv = None
    for l in range(n_layers):
        if l == n_conv:
            h_kv, kv = _proj_fwd(xc, vec_kv, W["w_kv"][0], tabs, GROUP_W * 3 // LANES)
            saved["kv"] = (xc, h_kv)
        v0 = vec_of(l, 0)
        xo, h, ab, y = _ffn_fwd(xc, v0, W["ffn1_w_in"][l], W["ffn1_w_out"][l])
        saved[l, 0] = (xc, v0, h, ab, y)
        xc = xo
        v1 = vec_of(l, 1)
        if l < n_conv:
            cw = _vec(conv_w[l, 0], conv_w[l, 1], conv_w[l, 2])
            xo, h, s4, z, y = _conv_fwd(xc, v1, cw, W["conv_w_in"][l], W["conv_w_out"][l])
            saved[l, 1] = (xc, v1, h, y, cw, s4, z)
        else:
            j = l - n_conv
            h, q = _proj_fwd(xc, v1, W["attn_w_q"][j], tabs, GROUP_W * 3 // LANES)
            og = [_attn_fwd(q, kv, g) for g in range(3)]
            mix, lj, y, xo = _attn_out([o for o, _ in og], [s for _, s in og], xc, v1, W["attn_w_o"][j])
            saved[l, 1] = (xc, v1, h, y, q, mix, lj)
        xc = xo
        v2 = vec_of(l, 2)
        xo, h, ab, y = _ffn_fwd(xc, v2, W["ffn2_w_in"][l], W["ffn2_w_out"][l])
        saved[l, 2] = (xc, v2, h, ab, y)
        xc = xo

    dx, red_f = _final_loss(xc, _vec(final_norm_g), tgt)
    loss_part = 0.5 * jnp.sum(red_f[1]) / D
    G = {k: [None] * len(v) for k, v in W.items()}
    dmods = [[None] * 9 for _ in range(n_layers)]
    dnorm = [[None] * 3 for _ in range(n_layers)]
    dconvw = [None] * n_conv
    dkv_parts = []

    def ffn_bwd(dx, l, k, name):
        xin, v, h, ab, y = saved[l, k]
        w_in, w_out = W[name + "_w_in"][l], W[name + "_w_out"][l]
        dy, u, dab, dh = _ffn_dgrad(dx, v, ab, w_in, w_out)
        G[name + "_w_in"][l] = _mm_tn(h, dab, w_in.shape[2])
        G[name + "_w_out"][l] = _mm_tn(u, dy[None], D).reshape(4, -1, D)
        return _norm_bwd(dx, [dh], xin, y, v, 0.5)

    def note(l, k, red):
        dnorm[l][k] = red[0]
        dmods[l][3 * k], dmods[l][3 * k + 1], dmods[l][3 * k + 2] = red[2], red[1], red[3]

    for l in reversed(range(n_layers)):
        dx, red = ffn_bwd(dx, l, 2, "ffn2")
        note(l, 2, red)
        if l < n_conv:
            xin, v, h, y, cw, s4, z = saved[l, 1]
            w_in, w_out = W["conv_w_in"][l], W["conv_w_out"][l]
            dy, d3, dh, dcw = _conv_dgrad(dx, v, cw, s4, w_in, w_out)
            G["conv_w_in"][l] = _mm_tn(h, d3, w_in.shape[2])
            G["conv_w_out"][l] = _mm_tn(z, dy[None], D).reshape(4, -1, D)
            dconvw[l] = dcw[0:3]
        else:
            j = l - n_conv
            xin, v, h, y, q, mix, lj = saved[l, 1]
            wq, wo = W["attn_w_q"][j], W["attn_w_o"][j]
            dy, dmix = _attn_out_dgrad(dx, v, wo)
            G["attn_w_o"][j] = _mm_tn(mix, dy[None], wo.shape[2])
            dqs = [_attn_dq(q, kv, dmix, mix, lj, g) for g in range(3)]
            dkv_parts.append([_attn_dkv(q, kv, dmix, mix, lj, g) for g in range(3)])
            dq, dh = _proj_dgrad([[t] for t in dqs], tabs, wq, 3)
            G["attn_w_q"][j] = _mm_tn(h, dq[None], wq.shape[2])
        dx, red = _norm_bwd(dx, [dh], xin, y, v, 1.0)
        note(l, 1, red)
        dx, red = ffn_bwd(dx, l, 0, "ffn1")
        note(l, 0, red)
        if l == n_conv:
            xin, h_kv = saved["kv"]
            parts = [[lay[g][0] for lay in dkv_parts] for g in range(3)] + [[lay[g][1] for lay in dkv_parts] for g in range(3)]
            dkv, dh = _proj_dgrad(parts, tabs, W["w_kv"][0], 3)
            G["w_kv"][0] = _mm_tn(h_kv, dkv[None], W["w_kv"][0].shape[2])
            dx, red_kv = _norm_bwd(dx, [dh], xin, None, vec_kv, 1.0)
    small = jnp.concatenate(
        [jnp.stack([jnp.stack(r) for r in dmods]).reshape(-1), red_kv[2], red_kv[1],
         jnp.stack([jnp.stack(r) for r in dnorm]).reshape(-1), jnp.stack(dconvw).reshape(-1), red_kv[0], red_f[0]])
    return loss_part, dx, G, small


BIG = ("ffn1_w_in", "ffn1_w_out", "ffn2_w_in", "ffn2_w_out", "conv_w_in", "conv_w_out", "w_kv", "attn_w_q", "attn_w_o")


def _halved(t):
    if t.ndim == 2:
        t = t[None]
    L, A, B = t.shape
    return t.reshape(L, 2, A // 2, B)


def kernel(x, c, positions, norm_g, ada_w, ada_b, ffn1_w_in, ffn1_w_out, ffn2_w_in, ffn2_w_out, conv_w_in, conv_w, conv_w_out, kv_norm_g, kv_ada_w, kv_ada_b, w_kv, attn_w_q, attn_w_o, final_norm_g, loss_target, m_norm_g, m_ada_w, m_ada_b, m_ffn1_w_in, m_ffn1_w_out, m_ffn2_w_in, m_ffn2_w_out, m_conv_w_in, m_conv_w, m_conv_w_out, m_kv_norm_g, m_kv_ada_w, m_kv_ada_b, m_w_kv, m_attn_w_q, m_attn_w_o, m_final_norm_g, v_norm_g, v_ada_w, v_ada_b, v_ffn1_w_in, v_ffn1_w_out, v_ffn2_w_in, v_ffn2_w_out, v_conv_w_in, v_conv_w, v_conv_w_out, v_kv_norm_g, v_kv_ada_w, v_kv_ada_b, v_w_kv, v_attn_w_q, v_attn_w_o, v_final_norm_g):
    wts = dict(norm_g=norm_g, ada_w=ada_w, ada_b=ada_b, ffn1_w_in=ffn1_w_in, ffn1_w_out=ffn1_w_out, ffn2_w_in=ffn2_w_in,
               ffn2_w_out=ffn2_w_out, conv_w_in=conv_w_in, conv_w=conv_w, conv_w_out=conv_w_out, kv_norm_g=kv_norm_g,
               kv_ada_w=kv_ada_w, kv_ada_b=kv_ada_b, w_kv=w_kv, attn_w_q=attn_w_q, attn_w_o=attn_w_o,
               final_norm_g=final_norm_g)
    ms = dict(norm_g=m_norm_g, ada_w=m_ada_w, ada_b=m_ada_b, ffn1_w_in=m_ffn1_w_in, ffn1_w_out=m_ffn1_w_out,
              ffn2_w_in=m_ffn2_w_in, ffn2_w_out=m_ffn2_w_out, conv_w_in=m_conv_w_in, conv_w=m_conv_w,
              conv_w_out=m_conv_w_out, kv_norm_g=m_kv_norm_g, kv_ada_w=m_kv_ada_w, kv_ada_b=m_kv_ada_b, w_kv=m_w_kv,
              attn_w_q=m_attn_w_q, attn_w_o=m_attn_w_o, final_norm_g=m_final_norm_g)
    vs = dict(norm_g=v_norm_g, ada_w=v_ada_w, ada_b=v_ada_b, ffn1_w_in=v_ffn1_w_in, ffn1_w_out=v_ffn1_w_out,
              ffn2_w_in=v_ffn2_w_in, ffn2_w_out=v_ffn2_w_out, conv_w_in=v_conv_w_in, conv_w=v_conv_w,
              conv_w_out=v_conv_w_out, kv_norm_g=v_kv_norm_g, kv_ada_w=v_kv_ada_w, kv_ada_b=v_kv_ada_b, w_kv=v_w_kv,
              attn_w_q=v_attn_w_q, attn_w_o=v_attn_w_o, final_norm_g=v_final_norm_g)
    order = list(wts)
    S, D = x.shape[1], x.shape[2]
    n_layers, n_conv = norm_g.shape[0], conv_w.shape[0]
    nm, sw = ada_w.shape[2], norm_g.shape[2]
    ix, iy, ic = (lax.axis_index(a) for a in AXES)
    chip = 2 * ix + iy

    pack = jnp.concatenate([norm_g.reshape(-1), conv_w.reshape(-1), c.reshape(-1)])
    npad = (-pack.size) % (8 * LANES)
    allp = _gather8(jnp.pad(pack, (0, npad)).reshape(-1, LANES)).reshape(8, -1)
    n1, n2 = norm_g.size, norm_g.size + conv_w.size
    by_chip = allp[0::2]
    norm_full = jnp.moveaxis(by_chip[:, :n1].reshape(4, n_layers, 3, sw), 0, 2).reshape(n_layers, 3, 4 * sw)
    conv_full = jnp.moveaxis(by_chip[:, n1:n2].reshape(4, n_conv, 3, sw), 0, 2).reshape(n_conv, 3, 4 * sw)
    c_all = allp[:, n2:n2 + D]

    b_l = lax.dynamic_slice_in_dim(ada_b, chip * nm, nm, axis=1)
    mod_sh = _ada_fwd(c_all, ada_w, b_l[:, None, :])
    nkv = kv_ada_w.shape[1]
    bkv = lax.dynamic_slice_in_dim(kv_ada_b, chip * nkv, nkv, axis=0)
    kv_sh = _ada_fwd(c_all, kv_ada_w[None], bkv[None, None, :])
    rows = jnp.concatenate([jnp.moveaxis(mod_sh, 0, 1), jnp.pad(jnp.moveaxis(kv_sh, 0, 1), ((0, 0), (0, 0), (0, nm - nkv)))], axis=1)
    rpad = (-rows.shape[1]) % 8
    mine = _mods_to_owner(jnp.pad(rows, ((0, 0), (0, rpad), (0, 0))))
    mods = jnp.moveaxis(mine[:, :n_layers], 0, 1).reshape(n_layers, 4 * nm)
    kvmods = mine[:, n_layers, :nkv].reshape(4 * nkv)

    place = jnp.stack([chip, ic]).astype(jnp.int32)
    halved = {k: _halved(wts[k]) for k in BIG}
    gathered = _allgather_weights([_cast_shard(halved[k], l, place) for k in BIG for l in range(halved[k].shape[0])])
    W, at = {}, 0
    for k in BIG:
        n_l, _, ha, B = halved[k].shape
        full = (4 * 2 * ha, B) if k.endswith("w_out") else (4, 2 * ha, B)
        W[k] = [t.reshape(full) for t in gathered[at:at + n_l]]
        at += n_l

    loss_part, dx, G, small = _local_step(x[0], loss_target[0], _rope_tables(positions), norm_full, conv_full,
                                          kv_norm_g, final_norm_g, mods, kvmods, W)
    loss = lax.psum(loss_part, AXES)

    flat = [g.reshape(4, 2, g.shape[1] // 2, g.shape[2]) for k in BIG for g in G[k]]
    recv = _exchange_halves(flat)
    parts = [_pair_add(g, r, place[1:]) for g, r in zip(flat, recv)]
    others = _scatter_partials(parts)
    gbufs, at = [], 0
    for k in BIG:
        gbuf = None
        for l in range(len(G[k])):
            gbuf = _sum_partials(parts[at], others[at], gbuf, len(G[k]), l, place)
            at += 1
        gbufs.append(gbuf)
    grads = {k: s.reshape(wts[k].shape) for k, s in zip(BIG, _share_halves(gbufs))}

    spad = (-small.size) % (8 * LANES)
    gath = _gather8(jnp.pad(small, (0, spad)).reshape(-1, LANES))
    tot = _sum8(gath).reshape(-1)
    gath = gath.reshape(8, -1)
    o = 0

    def take(n):
        nonlocal o
        o += n
        return tot[o - n:o]

    g_mods = take(n_layers * 9 * D).reshape(n_layers, 9 * D)
    g_kvmods = take(2 * D)
    g_norm = take(n_layers * 3 * D).reshape(n_layers, 3, D)
    g_convw = take(n_conv * 3 * D).reshape(n_conv, 3, D)
    g_kvn = take(D)
    g_fin = take(D)
    grads["ada_b"] = g_mods
    grads["kv_ada_b"] = g_kvmods
    grads["norm_g"] = lax.dynamic_slice_in_dim(g_norm, chip * sw, sw, axis=2)
    grads["conv_w"] = lax.dynamic_slice_in_dim(g_convw, chip * sw, sw, axis=2)
    grads["kv_norm_g"] = g_kvn
    grads["final_norm_g"] = g_fin

    dm_all = gath[:, :n_layers * 9 * D].reshape(8, n_layers, 9 * D)
    dm_mine = jnp.moveaxis(lax.dynamic_slice_in_dim(dm_all, chip * nm, nm, axis=2), 0, 1)
    dkv_all = gath[:, n_layers * 9 * D:n_layers * 9 * D + 2 * D]
    dkv_mine = lax.dynamic_slice_in_dim(dkv_all, chip * nkv, nkv, axis=1)[None]
    c_t = c_all.T
    grads["ada_w"] = _ada_wgrad(c_t, dm_mine)
    grads["kv_ada_w"] = _ada_wgrad(c_t, dkv_mine)[0]

    deltas, new_m, new_v = {}, {}, {}
    for k in order:
        shp = wts[k].shape
        two_d = (lambda t: t.reshape(1, -1)) if len(shp) == 1 else (lambda t: t)
        dlt, nmk, nvk = _adamw(two_d(wts[k]), two_d(grads[k]), two_d(ms[k]), two_d(vs[k]))
        deltas[k], new_m[k], new_v[k] = dlt.reshape(shp), nmk.reshape(shp), nvk.reshape(shp)
    return (loss, dx[None], *[grads[k] for k in order], *[deltas[k] for k in order], *[new_m[k] for k in order],
            *[new_v[k] for k in order])
```

```python
import functools
import math

import jax
import jax.numpy as jnp
from jax import lax
from jax.experimental import pallas as pl
from jax.experimental.pallas import tpu as pltpu

F32 = jnp.float32
BF = jnp.bfloat16
MESH = pl.DeviceIdType.MESH
AXES = ("x", "y", "c")

NORM_EPS = 1e-5
HEAD_DIM = 64
HEADS_PER_GROUP = 8
GROUP_W = HEADS_PER_GROUP * HEAD_DIM
DILATIONS = (1, 4, 16)
BAND = 128
ROPE_DIM = 16
ROPE_THETA = 500000.0
SM_SCALE = HEAD_DIM ** -0.5
NEG = -1e30
ADAM_LR, ADAM_B1, ADAM_B2, ADAM_EPS, ADAM_WD, ADAM_STEP = 0.001, 0.9, 0.999, 1e-08, 0.01, 10

V7X_VMEM_BYTES = 64 * 1024 * 1024
VMEM_LIMIT = V7X_VMEM_BYTES - 6 * 1024 * 1024
LANES = 128
TM = 512
TK = 512


def _cparams(ngrid):
    return pltpu.CompilerParams(dimension_semantics=("arbitrary",) * ngrid, vmem_limit_bytes=VMEM_LIMIT)


def _dot(a, b):
    return jnp.dot(a, b, preferred_element_type=F32)


def _dot_nt(a, b):
    return lax.dot_general(a, b, (((1,), (1,)), ((), ())), preferred_element_type=F32)


def _dot_tn(a, b):
    return lax.dot_general(a, b, (((0,), (0,)), ((), ())), preferred_element_type=F32)


def _tile(n, pref):
    t = min(n, pref)
    while n % t:
        t //= 2
    return t


def _row_tile(n, cap=256, mult=16):
    best = n
    for t in range(mult, min(n, cap) + 1, mult):
        if n % t == 0:
            best = t
    return best


def _rstd(x):
    return lax.rsqrt(jnp.mean(x * x, axis=-1, keepdims=True) + NORM_EPS)


def _norm_mod(x, v):
    return (x * _rstd(x) * v[0:1]) * (1.0 + v[1:2]) + v[2:3]


def _silu_parts(a):
    sg = jax.nn.sigmoid(a)
    return sg, a * sg


def _row(i):
    return lambda *_: (i, 0)


ANY = pl.BlockSpec(memory_space=pl.ANY)
IN_PLACE = ("gather_ici", "gather_d2d", "share")


def _place():
    x, y, c = (lax.axis_index(a) for a in AXES)
    chips = [(1 - x, y), (x, 1 - y), (1 - x, 1 - y)]
    return x, y, c, 2 * x + y, chips, [2 * cx + cy for cx, cy in chips]


def _transfers(kind, ins, outs):
    x, y, c, p, chips, qs = _place()
    sib = (x, y, 1 - c)
    rows = []
    for k, o in enumerate(outs):
        if kind == "gather_ici":
            rows.append([(o.at[p, c], o.at[p, c], (*chips[j], c), o.at[qs[j], c]) for j in range(3)])
        elif kind == "gather_d2d":
            rows.append([(o.at[qs[j], c], o.at[qs[j], c], sib, o.at[qs[j], 1 - c]) for j in range(3)])
        elif kind == "exchange":
            rows.append([(ins[k].at[:, 1 - c], o, sib, o)])
        elif kind == "scatter":
            rows.append([(ins[k].at[qs[j]], o.at[j], (*chips[j], c), o.at[j]) for j in range(3)])
        elif kind == "share":
            rows.append([(o.at[:, c], o.at[:, c], sib, o.at[:, 1 - c])])
    return rows


def _comm_out_shapes(kind, arrays):
    if kind in IN_PLACE:
        return [jax.ShapeDtypeStruct(a.shape, a.dtype) for a in arrays]
    if kind == "exchange":
        return [jax.ShapeDtypeStruct((4,) + a.shape[2:], a.dtype) for a in arrays]
    return [jax.ShapeDtypeStruct((3,) + a.shape[1:], a.dtype) for a in arrays]


def _comm_sems(n):
    return [pltpu.SemaphoreType.DMA((n, 3)), pltpu.SemaphoreType.DMA((n, 3))]


def _comm_start(rows, ssem, rsem):
    for k, row in enumerate(rows):
        for j, (src, dst, dev, _) in enumerate(row):
            pltpu.make_async_remote_copy(src_ref=src, dst_ref=dst, send_sem=ssem.at[k, j], recv_sem=rsem.at[k, j],
                                         device_id=dev, device_id_type=MESH).start()


def _comm_wait(rows, ssem, rsem):
    for k, row in enumerate(rows):
        for j, (src, dst, dev, land) in enumerate(row):
            pltpu.make_async_remote_copy(src_ref=src, dst_ref=dst, send_sem=ssem.at[k, j], recv_sem=rsem.at[k, j],
                                         device_id=dev, device_id_type=MESH).wait_send()
            pltpu.make_async_remote_copy(src_ref=land, dst_ref=land, send_sem=ssem.at[k, j], recv_sem=rsem.at[k, j],
                                         device_id=dev, device_id_type=MESH).wait_recv()


def _comm_call(kind, arrays):
    n = len(arrays)

    def body(*refs):
        rows = _transfers(kind, refs[:n], refs[n:2 * n])
        _comm_start(rows, *refs[2 * n:])
        _comm_wait(rows, *refs[2 * n:])

    return pl.pallas_call(
        body, name="comm_" + kind, in_specs=[ANY] * n, out_specs=[ANY] * n, out_shape=_comm_out_shapes(kind, arrays),
        input_output_aliases={k: k for k in range(n)} if kind in IN_PLACE else {},
        scratch_shapes=_comm_sems(n),
    )(*arrays)


def _ride(comm, n_in, n_out):
    if comm is None:
        return dict(arrays=[], in_specs=[], out_specs=[], out_shape=[], aliases={}, scratch=[], n=0)
    kind, arrays = comm
    n = len(arrays)
    aliases = {n_in + k: n_out + k for k in range(n)} if kind in IN_PLACE else {}
    return dict(arrays=list(arrays), in_specs=[ANY] * n, out_specs=[ANY] * n, out_shape=_comm_out_shapes(kind, arrays),
                aliases=aliases, scratch=_comm_sems(n), n=n)


def _ride_when(comm, ins, outs, sems, cond, action):
    if comm is None:
        return

    @pl.when(cond)
    def _():
        action(_transfers(comm[0], ins, outs), *sems)


def _ffn_fwd(x, vec, w_in4, w_out, comm=None):
    S, D = x.shape
    cs = w_in4.shape[2]
    F = 2 * cs
    tm = _tile(S, TM)
    ni = S // tm
    ride = _ride(comm, 5, 4)
    nc = ride["n"]

    def body(*refs):
        x_ref, vec_ref, wa_ref, wb_ref, wo_ref = refs[:5]
        xo_ref, h_ref, ab_ref, y_ref = refs[5 + nc:9 + nc]
        i, j = pl.program_id(0), pl.program_id(1)
        riding = (comm, refs[5:5 + nc], refs[9 + nc:9 + 2 * nc], refs[9 + 2 * nc:])
        _ride_when(*riding, jnp.logical_and(i == 0, j == 0), _comm_start)

        @pl.when(j == 0)
        def _():
            h_ref[...] = _norm_mod(x_ref[...], vec_ref[...]).astype(BF)
            y_ref[...] = jnp.zeros_like(y_ref)

        h = h_ref[...]
        a = _dot(h, wa_ref[...])
        b = _dot(h, wb_ref[...])
        ab_ref[0] = a.astype(BF)
        ab_ref[1] = b.astype(BF)
        _, s = _silu_parts(a)
        y_ref[...] += _dot((s * b).astype(BF), wo_ref[...])

        @pl.when(j == 1)
        def _():
            xo_ref[...] = x_ref[...] + (0.5 * (1.0 + vec_ref[3:4, :])) * y_ref[...]

        _ride_when(*riding, jnp.logical_and(i == ni - 1, j == 1), _comm_wait)

    tok = pl.BlockSpec((tm, D), lambda i, j: (i, 0))
    return pl.pallas_call(
        body, name="ffn_fwd", grid=(ni, 2),
        in_specs=[tok, pl.BlockSpec((8, D), lambda i, j: (0, 0)),
                  pl.BlockSpec((None, D, cs), lambda i, j: (j, 0, 0)),
                  pl.BlockSpec((None, D, cs), lambda i, j: (j + 2, 0, 0)),
                  pl.BlockSpec((cs, D), lambda i, j: (j, 0))] + ride["in_specs"],
        out_specs=[tok, tok, pl.BlockSpec((2, tm, cs), lambda i, j: (0, i, j)), tok] + ride["out_specs"],
        out_shape=[jax.ShapeDtypeStruct((S, D), F32), jax.ShapeDtypeStruct((S, D), BF),
                   jax.ShapeDtypeStruct((2, S, F), BF), jax.ShapeDtypeStruct((S, D), F32)] + ride["out_shape"],
        input_output_aliases=ride["aliases"], scratch_shapes=ride["scratch"],
        compiler_params=_cparams(2),
    )(x, vec, w_in4, w_in4, w_out, *ride["arrays"])


def _ffn_dgrad(dxo, vec, ab, w_in4, w_out, comm=None):
    S, D = dxo.shape
    cs = w_in4.shape[2]
    F = 2 * cs
    tm = _tile(S, TM)
    ni = S // tm
    ride = _ride(comm, 6, 4)
    nc = ride["n"]

    def body(*refs):
        dxo_ref, vec_ref, ab_ref, wa_ref, wb_ref, wo_ref = refs[:6]
        dy_ref, u_ref, dab_ref, dh_ref = refs[6 + nc:10 + nc]
        i, j = pl.program_id(0), pl.program_id(1)
        riding = (comm, refs[6:6 + nc], refs[10 + nc:10 + 2 * nc], refs[10 + 2 * nc:])
        _ride_when(*riding, jnp.logical_and(i == 0, j == 0), _comm_start)

        @pl.when(j == 0)
        def _():
            dy_ref[...] = ((0.5 * (1.0 + vec_ref[3:4, :])) * dxo_ref[...]).astype(BF)
            dh_ref[...] = jnp.zeros_like(dh_ref)

        du = _dot_nt(dy_ref[...], wo_ref[...])
        a = ab_ref[0].astype(F32)
        b = ab_ref[1].astype(F32)
        sg, s = _silu_parts(a)
        u_ref[...] = (s * b).astype(BF)
        da = (du * b * (sg * (1.0 + a * (1.0 - sg)))).astype(BF)
        db = (du * s).astype(BF)
        dab_ref[0] = da
        dab_ref[1] = db
        dh_ref[...] += _dot_nt(da, wa_ref[...]) + _dot_nt(db, wb_ref[...])
        _ride_when(*riding, jnp.logical_and(i == ni - 1, j == 1), _comm_wait)

    tok = pl.BlockSpec((tm, D), lambda i, j: (i, 0))
    return pl.pallas_call(
        body, name="ffn_dgrad", grid=(ni, 2),
        in_specs=[tok, pl.BlockSpec((8, D), lambda i, j: (0, 0)),
                  pl.BlockSpec((2, tm, cs), lambda i, j: (0, i, j)),
                  pl.BlockSpec((None, D, cs), lambda i, j: (j, 0, 0)),
                  pl.BlockSpec((None, D, cs), lambda i, j: (j + 2, 0, 0)),
                  pl.BlockSpec((cs, D), lambda i, j: (j, 0))] + ride["in_specs"],
        out_specs=[tok, pl.BlockSpec((tm, cs), lambda i, j: (i, j)),
                   pl.BlockSpec((2, tm, cs), lambda i, j: (0, i, j)), tok] + ride["out_specs"],
        out_shape=[jax.ShapeDtypeStruct((S, D), BF), jax.ShapeDtypeStruct((S, F), BF),
                   jax.ShapeDtypeStruct((2, S, F), BF), jax.ShapeDtypeStruct((S, D), F32)] + ride["out_shape"],
        input_output_aliases=ride["aliases"], scratch_shapes=ride["scratch"],
        compiler_params=_cparams(2),
    )(dxo, vec, ab, w_in4, w_in4, w_out, *ride["arrays"])


def _mm_tn(a, b3, cs):
    K, M = a.shape
    G, _, Nb = b3.shape
    N = G * Nb
    tn = math.gcd(cs, Nb)
    tmo = M if M <= 1024 else M // 2
    tk = _tile(K, TK)

    def body(a_ref, b_ref, o_ref):
        @pl.when(pl.program_id(2) == 0)
        def _():
            o_ref[...] = jnp.zeros_like(o_ref)

        o_ref[...] += _dot_tn(a_ref[...].astype(BF), b_ref[...].astype(BF))

    return pl.pallas_call(
        body, name="wgrad_tn", grid=(M // tmo, N // tn, K // tk),
        in_specs=[pl.BlockSpec((tk, tmo), lambda m, n, k: (k, m)),
                  pl.BlockSpec((None, tk, tn), lambda m, n, k: ((n * tn) // Nb, k, ((n * tn) % Nb) // tn))],
        out_specs=pl.BlockSpec((None, tmo, tn), lambda m, n, k: ((n * tn) // cs, m, ((n * tn) % cs) // tn)),
        out_shape=jax.ShapeDtypeStruct((N // cs, M, cs), F32),
        compiler_params=_cparams(3),
    )(a, b3)


def _norm_bwd(dxo, dhs, x, y, vec, coef):
    S, D = x.shape
    tm = _tile(S, TM)
    nh = len(dhs)
    has_y = y is not None

    def body(*refs):
        dxo_ref = refs[0]
        dh_refs = refs[1:1 + nh]
        x_ref = refs[1 + nh]
        y_ref = refs[2 + nh] if has_y else None
        vec_ref, dx_ref, red_ref = refs[-3:]

        @pl.when(pl.program_id(0) == 0)
        def _():
            red_ref[...] = jnp.zeros_like(red_ref)

        xv = x_ref[...]
        dxo = dxo_ref[...]
        dh = dh_refs[0][...]
        for r in dh_refs[1:]:
            dh = dh + r[...]
        g = vec_ref[0:1, :]
        sc = vec_ref[1:2, :]
        r = _rstd(xv)
        xh = xv * r
        dhn = dh * (1.0 + sc)
        dxh = dhn * g
        dx_ref[...] = dxo + r * (dxh - xh * jnp.mean(dxh * xh, axis=-1, keepdims=True))
        red_ref[0:1, :] += jnp.sum(dhn * xh, axis=0, keepdims=True)
        red_ref[1:2, :] += jnp.sum(dh * (xh * g), axis=0, keepdims=True)
        red_ref[2:3, :] += jnp.sum(dh, axis=0, keepdims=True)
        if has_y:
            red_ref[3:4, :] += coef * jnp.sum(dxo * y_ref[...], axis=0, keepdims=True)

    tok = pl.BlockSpec((tm, D), lambda i: (i, 0))
    small = pl.BlockSpec((8, D), lambda i: (0, 0))
    ops = [dxo, *dhs, x] + ([y] if has_y else []) + [vec]
    return pl.pallas_call(
        body, name="norm_bwd", grid=(S // tm,),
        in_specs=[tok] * (len(ops) - 1) + [small],
        out_specs=[tok, small],
        out_shape=[jax.ShapeDtypeStruct((S, D), F32), jax.ShapeDtypeStruct((8, D), F32)],
        compiler_params=_cparams(1),
    )(*ops)


def _conv_fwd(x, vec, cw, w_in4, w_out):
    S, D = x.shape
    cs = w_in4.shape[2]
    tm = _tile(S, TM)

    def body(x_ref, vec_ref, cw_ref, wi_ref, wo_ref, xo_ref, h_ref, s4_ref, z_ref, y_ref, vs_ref):
        @pl.when(pl.program_id(0) == 0)
        def _():
            vs_ref[0:8, :] = jnp.zeros((8, D), F32)

        xv = x_ref[...]
        h = _norm_mod(xv, vec_ref[...]).astype(BF)
        h_ref[...] = h
        bcu = jnp.concatenate([_dot(h, wi_ref[q]) for q in range(4)], axis=1)
        bg, cg, u = bcu[:, :D], bcu[:, D:2 * D], bcu[:, 2 * D:]
        v = cg * u
        vs_ref[8:8 + tm, :] = v
        conv = cw_ref[0:1, :] * vs_ref[pl.ds(6, tm), :] + cw_ref[1:2, :] * vs_ref[pl.ds(7, tm), :] + cw_ref[2:3, :] * v
        vs_ref[0:8, :] = vs_ref[tm:tm + 8, :]
        z = (bg * conv).astype(BF)
        s4_ref[0] = bg.astype(BF)
        s4_ref[1] = cg.astype(BF)
        s4_ref[2] = u.astype(BF)
        s4_ref[3] = conv.astype(BF)
        z_ref[...] = z
        y = _dot(z, wo_ref[...])
        y_ref[...] = y
        xo_ref[...] = xv + (1.0 + vec_ref[3:4, :]) * y

    tok = pl.BlockSpec((tm, D), lambda i: (i, 0))
    small = pl.BlockSpec((8, D), lambda i: (0, 0))
    return pl.pallas_call(
        body, name="conv_fwd", grid=(S // tm,),
        in_specs=[tok, small, small, pl.BlockSpec((4, D, cs), lambda i: (0, 0, 0)),
                  pl.BlockSpec((D, D), lambda i: (0, 0))],
        out_specs=[tok, tok, pl.BlockSpec((4, tm, D), lambda i: (0, i, 0)), tok, tok],
        out_shape=[jax.ShapeDtypeStruct((S, D), F32), jax.ShapeDtypeStruct((S, D), BF),
                   jax.ShapeDtypeStruct((4, S, D), BF), jax.ShapeDtypeStruct((S, D), BF),
                   jax.ShapeDtypeStruct((S, D), F32)],
        scratch_shapes=[pltpu.VMEM((tm + 8, D), F32)],
        compiler_params=_cparams(1),
    )(x, vec, cw, w_in4, w_out)


def _conv_dgrad(dxo, vec, cw, s4, w_in4, w_out):
    S, D = dxo.shape
    cs = w_in4.shape[2]
    tm = _tile(S, TM)
    nt = S // tm

    def body(dxo_ref, vec_ref, cw_ref, s4_ref, wi_ref, wo_ref, dy_ref, d3_ref, dh_ref, dcw_ref, ds_ref):
        @pl.when(pl.program_id(0) == 0)
        def _():
            ds_ref[tm:tm + 8, :] = jnp.zeros((8, D), F32)
            dcw_ref[...] = jnp.zeros_like(dcw_ref)

        dy = ((1.0 + vec_ref[3:4, :]) * dxo_ref[...]).astype(BF)
        dy_ref[...] = dy
        dz = _dot_nt(dy, wo_ref[...])
        bg = s4_ref[0].astype(F32)
        cg = s4_ref[1].astype(F32)
        u = s4_ref[2].astype(F32)
        conv = s4_ref[3].astype(F32)
        dbg = dz * conv
        dconv = dz * bg
        ds_ref[0:tm, :] = dconv
        d1 = ds_ref[pl.ds(1, tm), :]
        d2 = ds_ref[pl.ds(2, tm), :]
        ds_ref[tm:tm + 8, :] = ds_ref[0:8, :]
        dv = cw_ref[2:3, :] * dconv + cw_ref[1:2, :] * d1 + cw_ref[0:1, :] * d2
        v = cg * u
        dcw_ref[0:1, :] += jnp.sum(d2 * v, axis=0, keepdims=True)
        dcw_ref[1:2, :] += jnp.sum(d1 * v, axis=0, keepdims=True)
        dcw_ref[2:3, :] += jnp.sum(dconv * v, axis=0, keepdims=True)
        dbcu = jnp.concatenate([dbg, dv * u, dv * cg], axis=1).astype(BF)
        d3_ref[...] = dbcu
        dh = _dot_nt(dbcu[:, 0:cs], wi_ref[0])
        for q in range(1, 4):
            dh = dh + _dot_nt(dbcu[:, q * cs:(q + 1) * cs], wi_ref[q])
        dh_ref[...] = dh

    tok = pl.BlockSpec((tm, D), lambda i: (nt - 1 - i, 0))
    small = pl.BlockSpec((8, D), lambda i: (0, 0))
    return pl.pallas_call(
        body, name="conv_dgrad", grid=(nt,),
        in_specs=[tok, small, small, pl.BlockSpec((4, tm, D), lambda i: (0, nt - 1 - i, 0)),
                  pl.BlockSpec((4, D, cs), lambda i: (0, 0, 0)), pl.BlockSpec((D, D), lambda i: (0, 0))],
        out_specs=[tok, pl.BlockSpec((tm, 3 * D), lambda i: (nt - 1 - i, 0)), tok, small],
        out_shape=[jax.ShapeDtypeStruct((S, D), BF), jax.ShapeDtypeStruct((S, 3 * D), BF),
                   jax.ShapeDtypeStruct((S, D), F32), jax.ShapeDtypeStruct((8, D), F32)],
        scratch_shapes=[pltpu.VMEM((tm + 8, D), F32)],
        compiler_params=_cparams(1),
    )(dxo, vec, cw, s4, w_in4, w_out)


def _rope_tables(positions):
    S = positions.shape[-1]
    inv = ROPE_THETA ** (-jnp.arange(0, ROPE_DIM, 2, dtype=F32) / ROPE_DIM)
    ang = positions.reshape(S, 1).astype(F32) * inv
    cos = jnp.tile(jnp.cos(ang), (1, LANES // 8))
    sin = jnp.tile(jnp.sin(ang), (1, LANES // 8))
    l64 = jnp.arange(LANES) % HEAD_DIM
    return jnp.stack([jnp.where(l64 < ROPE_DIM, cos, 1.0),
                      jnp.where(l64 < ROPE_DIM // 2, -sin, 0.0),
                      jnp.where((l64 >= ROPE_DIM // 2) & (l64 < ROPE_DIM), sin, 0.0)])


def _rope(t, tab_ref):
    return t * tab_ref[0] + pltpu.roll(t, LANES - 8, 1) * tab_ref[1] + pltpu.roll(t, 8, 1) * tab_ref[2]


def _rope_t(d, tab_ref):
    return d * tab_ref[0] + pltpu.roll(d * tab_ref[1], 8, 1) + pltpu.roll(d * tab_ref[2], LANES - 8, 1)


def _proj_fwd(x, vec, w4, tabs, n_rope):
    S, D = x.shape
    cs = w4.shape[2]
    N = 4 * cs
    tm = _tile(S, TM)

    def body(x_ref, vec_ref, w_ref, tab_ref, h_ref, o_ref):
        h = _norm_mod(x_ref[...], vec_ref[...]).astype(BF)
        h_ref[...] = h
        per = cs // LANES
        for q in range(4):
            acc = _dot(h, w_ref[q])
            cols = [acc[:, ch * LANES:(ch + 1) * LANES] for ch in range(per)]
            cols = [_rope(t, tab_ref) if q * per + ch < n_rope else t for ch, t in enumerate(cols)]
            o_ref[:, q * cs:(q + 1) * cs] = jnp.concatenate(cols, axis=1).astype(BF)

    return pl.pallas_call(
        body, name="proj_fwd", grid=(S // tm,),
        in_specs=[pl.BlockSpec((tm, D), lambda i: (i, 0)), pl.BlockSpec((8, D), lambda i: (0, 0)),
                  pl.BlockSpec((4, D, cs), lambda i: (0, 0, 0)), pl.BlockSpec((3, tm, LANES), lambda i: (0, i, 0))],
        out_specs=[pl.BlockSpec((tm, D), lambda i: (i, 0)), pl.BlockSpec((tm, N), lambda i: (i, 0))],
        out_shape=[jax.ShapeDtypeStruct((S, D), BF), jax.ShapeDtypeStruct((S, N), BF)],
        compiler_params=_cparams(1),
    )(x, vec, w4, tabs)


def _proj_dgrad(parts, tabs, w4, n_rope_groups):
    S = parts[0][0].shape[0]
    cs, D = w4.shape[2], w4.shape[1]
    N = 4 * cs
    tm = _tile(S, TM)
    flat = [p for grp in parts for p in grp]
    counts = [len(grp) for grp in parts]

    def body(*refs):
        prefs = refs[:len(flat)]
        tab_ref, w_ref, dz_ref, dh_ref = refs[len(flat):]
        k = 0
        for j, cnt in enumerate(counts):
            z = prefs[k][...]
            for r in prefs[k + 1:k + cnt]:
                z = z + r[...]
            k += cnt
            if j < n_rope_groups:
                z = jnp.concatenate([_rope_t(z[:, ch * LANES:(ch + 1) * LANES], tab_ref)
                                     for ch in range(GROUP_W // LANES)], axis=1)
            dz_ref[:, j * GROUP_W:(j + 1) * GROUP_W] = z.astype(BF)
        dh = _dot_nt(dz_ref[:, 0:cs], w_ref[0])
        for q in range(1, 4):
            dh = dh + _dot_nt(dz_ref[:, q * cs:(q + 1) * cs], w_ref[q])
        dh_ref[...] = dh

    grp = pl.BlockSpec((tm, GROUP_W), lambda i: (i, 0))
    return pl.pallas_call(
        body, name="proj_dgrad", grid=(S // tm,),
        in_specs=[grp] * len(flat) + [pl.BlockSpec((3, tm, LANES), lambda i: (0, i, 0)),
                                      pl.BlockSpec((4, D, cs), lambda i: (0, 0, 0))],
        out_specs=[pl.BlockSpec((tm, N), lambda i: (i, 0)), pl.BlockSpec((tm, D), lambda i: (i, 0))],
        out_shape=[jax.ShapeDtypeStruct((S, N), BF), jax.ShapeDtypeStruct((S, D), F32)],
        compiler_params=_cparams(1),
    )(*flat, tabs, w4)


def _band_masks():
    qi = lax.broadcasted_iota(jnp.int32, (BAND, BAND), 0)
    kj = lax.broadcasted_iota(jnp.int32, (BAND, BAND), 1)
    return kj <= qi, kj >= qi


def _head(ref, h):
    return ref[:, h * HEAD_DIM:(h + 1) * HEAD_DIM]


def _attn_fwd(q, kv, g):
    d = DILATIONS[g]
    S = q.shape[0]
    sd = S // d
    nb = sd // BAND
    qw, kw = q.shape[1] // GROUP_W, kv.shape[1] // GROUP_W

    def body(q_ref, kp_ref, kc_ref, vp_ref, vc_ref, o_ref, l_ref):
        same, prev = _band_masks()
        prev = jnp.logical_and(prev, pl.program_id(1) > 0)
        for h in range(HEADS_PER_GROUP):
            qh = _head(q_ref, h)
            sc = jnp.where(same, _dot_nt(qh, _head(kc_ref, h)) * SM_SCALE, NEG)
            sp = jnp.where(prev, _dot_nt(qh, _head(kp_ref, h)) * SM_SCALE, NEG)
            m = jnp.maximum(jnp.max(sc, axis=-1, keepdims=True), jnp.max(sp, axis=-1, keepdims=True))
            pc = jnp.exp(sc - m)
            pp = jnp.exp(sp - m)
            den = jnp.sum(pc, axis=-1, keepdims=True) + jnp.sum(pp, axis=-1, keepdims=True)
            o = (_dot(pc.astype(BF), _head(vc_ref, h)) + _dot(pp.astype(BF), _head(vp_ref, h))) / den
            o_ref[:, h * HEAD_DIM:(h + 1) * HEAD_DIM] = o
            l_ref[:, h * HEAD_DIM:(h + 1) * HEAD_DIM] = jnp.broadcast_to(m + jnp.log(den), (BAND, HEAD_DIM))

    blk = (BAND, GROUP_W)
    out = pl.BlockSpec(blk, lambda r, i: (i, r))
    o, lse = pl.pallas_call(
        body, name=f"attn_fwd_g{g}", grid=(d, nb),
        in_specs=[pl.BlockSpec(blk, lambda r, i: (i, r * qw + g)),
                  pl.BlockSpec(blk, lambda r, i: (jnp.maximum(i - 1, 0), r * kw + g)),
                  pl.BlockSpec(blk, lambda r, i: (i, r * kw + g)),
                  pl.BlockSpec(blk, lambda r, i: (jnp.maximum(i - 1, 0), r * kw + qw + g)),
                  pl.BlockSpec(blk, lambda r, i: (i, r * kw + qw + g))],
        out_specs=[out, out],
        out_shape=[jax.ShapeDtypeStruct((sd, d * GROUP_W), F32)] * 2,
        compiler_params=_cparams(2),
    )(q.reshape(sd, d * q.shape[1]), *([kv.reshape(sd, d * kv.shape[1])] * 4))
    return o.reshape(S, GROUP_W), lse.reshape(S, GROUP_W)


def _attn_out(os, ls, x, vec, wo4):
    S, D = x.shape
    cs = wo4.shape[2]
    tm = _tile(S, TM)

    def body(o0, l0, o1, l1, o2, l2, x_ref, vec_ref, wo_ref, mix_ref, lj_ref, y_ref, xo_ref):
        la, lb, lc = l0[...], l1[...], l2[...]
        mx = jnp.maximum(jnp.maximum(la, lb), lc)
        ea, eb, ec = jnp.exp(la - mx), jnp.exp(lb - mx), jnp.exp(lc - mx)
        den = ea + eb + ec
        mix = (ea * o0[...] + eb * o1[...] + ec * o2[...]) / den
        mix_ref[...] = mix
        lj_ref[...] = mx + jnp.log(den)
        mb = mix.astype(BF)
        y = jnp.concatenate([_dot(mb, wo_ref[q]) for q in range(4)], axis=1)
        y_ref[...] = y
        xo_ref[...] = x_ref[...] + (1.0 + vec_ref[3:4, :]) * y

    grp = pl.BlockSpec((tm, GROUP_W), lambda i: (i, 0))
    tok = pl.BlockSpec((tm, D), lambda i: (i, 0))
    return pl.pallas_call(
        body, name="attn_out", grid=(S // tm,),
        in_specs=[grp] * 6 + [tok, pl.BlockSpec((8, D), lambda i: (0, 0)),
                              pl.BlockSpec((4, GROUP_W, cs), lambda i: (0, 0, 0))],
        out_specs=[grp, grp, tok, tok],
        out_shape=[jax.ShapeDtypeStruct((S, GROUP_W), F32)] * 2 + [jax.ShapeDtypeStruct((S, D), F32)] * 2,
        compiler_params=_cparams(1),
    )(os[0], ls[0], os[1], ls[1], os[2], ls[2], x, vec, wo4)


def _attn_out_dgrad(dxo, vec, wo4):
    S, D = dxo.shape
    cs = wo4.shape[2]
    tm = _tile(S, TM)

    def body(dxo_ref, vec_ref, wo_ref, dy_ref, dm_ref):
        dy = ((1.0 + vec_ref[3:4, :]) * dxo_ref[...]).astype(BF)
        dy_ref[...] = dy
        dm = _dot_nt(dy[:, 0:cs], wo_ref[0])
        for q in range(1, 4):
            dm = dm + _dot_nt(dy[:, q * cs:(q + 1) * cs], wo_ref[q])
        dm_ref[...] = dm

    tok = pl.BlockSpec((tm, D), lambda i: (i, 0))
    return pl.pallas_call(
        body, name="attn_out_dgrad", grid=(S // tm,),
        in_specs=[tok, pl.BlockSpec((8, D), lambda i: (0, 0)), pl.BlockSpec((4, GROUP_W, cs), lambda i: (0, 0, 0))],
        out_specs=[tok, pl.BlockSpec((tm, GROUP_W), lambda i: (i, 0))],
        out_shape=[jax.ShapeDtypeStruct((S, D), BF), jax.ShapeDtypeStruct((S, GROUP_W), F32)],
        compiler_params=_cparams(1),
    )(dxo, vec, wo4)


def _prob_and_ds(qh, kh, vh, doh, oh, lrow, mask):
    s = jnp.where(mask, _dot_nt(qh, kh) * SM_SCALE, NEG)
    p = jnp.exp(s - lrow)
    delta = jnp.sum(doh * oh, axis=-1, keepdims=True)
    dp = _dot_nt(doh.astype(BF), vh)
    return p, p * (dp - delta) * SM_SCALE


def _attn_dq(q, kv, dmix, mix, lj, g):
    d = DILATIONS[g]
    S = q.shape[0]
    sd = S // d
    nb = sd // BAND
    qw, kw = q.shape[1] // GROUP_W, kv.shape[1] // GROUP_W

    def body(q_ref, kp_ref, kc_ref, vp_ref, vc_ref, do_ref, o_ref, l_ref, dq_ref):
        same, prev = _band_masks()
        prev = jnp.logical_and(prev, pl.program_id(1) > 0)
        for h in range(HEADS_PER_GROUP):
            qh, doh, oh = _head(q_ref, h), _head(do_ref, h), _head(o_ref, h)
            lrow = l_ref[:, h * HEAD_DIM:h * HEAD_DIM + 1]
            kc, kp = _head(kc_ref, h), _head(kp_ref, h)
            _, dsc = _prob_and_ds(qh, kc, _head(vc_ref, h), doh, oh, lrow, same)
            _, dsp = _prob_and_ds(qh, kp, _head(vp_ref, h), doh, oh, lrow, prev)
            dq_ref[:, h * HEAD_DIM:(h + 1) * HEAD_DIM] = _dot(dsc.astype(BF), kc) + _dot(dsp.astype(BF), kp)

    blk = (BAND, GROUP_W)
    res = pl.BlockSpec(blk, lambda r, i: (i, r))
    kvd = kv.reshape(sd, d * kv.shape[1])
    dq = pl.pallas_call(
        body, name=f"attn_dq_g{g}", grid=(d, nb),
        in_specs=[pl.BlockSpec(blk, lambda r, i: (i, r * qw + g)),
                  pl.BlockSpec(blk, lambda r, i: (jnp.maximum(i - 1, 0), r * kw + g)),
                  pl.BlockSpec(blk, lambda r, i: (i, r * kw + g)),
                  pl.BlockSpec(blk, lambda r, i: (jnp.maximum(i - 1, 0), r * kw + qw + g)),
                  pl.BlockSpec(blk, lambda r, i: (i, r * kw + qw + g)),
                  res, res, res],
        out_specs=res,
        out_shape=jax.ShapeDtypeStruct((sd, d * GROUP_W), F32),
        compiler_params=_cparams(2),
    )(q.reshape(sd, d * q.shape[1]), kvd, kvd, kvd, kvd, dmix.reshape(sd, d * GROUP_W),
      mix.reshape(sd, d * GROUP_W), lj.reshape(sd, d * GROUP_W))
    return dq.reshape(S, GROUP_W)


def _attn_dkv(q, kv, dmix, mix, lj, g):
    d = DILATIONS[g]
    S = q.shape[0]
    sd = S // d
    nb = sd // BAND
    qw, kw = q.shape[1] // GROUP_W, kv.shape[1] // GROUP_W

    def body(k_ref, v_ref, qs_ref, qn_ref, dos_ref, don_ref, os_ref, on_ref, ls_ref, ln_ref, dk_ref, dv_ref):
        same, nxt = _band_masks()
        nxt = jnp.logical_and(nxt, pl.program_id(1) < nb - 1)
        for h in range(HEADS_PER_GROUP):
            kh, vh = _head(k_ref, h), _head(v_ref, h)
            dk = jnp.zeros((BAND, HEAD_DIM), F32)
            dv = jnp.zeros((BAND, HEAD_DIM), F32)
            for q_ref, do_ref, o_ref, l_ref, mask in ((qs_ref, dos_ref, os_ref, ls_ref, same),
                                                       (qn_ref, don_ref, on_ref, ln_ref, nxt)):
                qh, doh = _head(q_ref, h), _head(do_ref, h)
                p, ds = _prob_and_ds(qh, kh, vh, doh, _head(o_ref, h), l_ref[:, h * HEAD_DIM:h * HEAD_DIM + 1], mask)
                dv = dv + _dot_tn(p.astype(BF), doh.astype(BF))
                dk = dk + _dot_tn(ds.astype(BF), qh)
            dk_ref[:, h * HEAD_DIM:(h + 1) * HEAD_DIM] = dk
            dv_ref[:, h * HEAD_DIM:(h + 1) * HEAD_DIM] = dv

    blk = (BAND, GROUP_W)
    cur = pl.BlockSpec(blk, lambda r, i: (i, r))
    nxt_spec = pl.BlockSpec(blk, lambda r, i: (jnp.minimum(i + 1, nb - 1), r))
    qd = q.reshape(sd, d * q.shape[1])
    kvd = kv.reshape(sd, d * kv.shape[1])
    dod, od, ld = (t.reshape(sd, d * GROUP_W) for t in (dmix, mix, lj))
    dk, dv = pl.pallas_call(
        body, name=f"attn_dkv_g{g}", grid=(d, nb),
        in_specs=[pl.BlockSpec(blk, lambda r, i: (i, r * kw + g)),
                  pl.BlockSpec(blk, lambda r, i: (i, r * kw + qw + g)),
                  pl.BlockSpec(blk, lambda r, i: (i, r * qw + g)),
                  pl.BlockSpec(blk, lambda r, i: (jnp.minimum(i + 1, nb - 1), r * qw + g)),
                  cur, nxt_spec, cur, nxt_spec, cur, nxt_spec],
        out_specs=[cur, cur],
        out_shape=[jax.ShapeDtypeStruct((sd, d * GROUP_W), F32)] * 2,
        compiler_params=_cparams(2),
    )(kvd, kvd, qd, qd, dod, dod, od, od, ld, ld)
    return dk.reshape(S, GROUP_W), dv.reshape(S, GROUP_W)


def _final_loss(x, gvec, tgt):
    S, D = x.shape
    tm = _tile(S, TM)

    def body(x_ref, g_ref, t_ref, dx_ref, red_ref):
        @pl.when(pl.program_id(0) == 0)
        def _():
            red_ref[...] = jnp.zeros_like(red_ref)

        xv = x_ref[...]
        g = g_ref[0:1, :]
        r = _rstd(xv)
        xh = xv * r
        err = xh * g - t_ref[...]
        dy = err * (1.0 / D)
        dxh = dy * g
        dx_ref[...] = r * (dxh - xh * jnp.mean(dxh * xh, axis=-1, keepdims=True))
        red_ref[0:1, :] += jnp.sum(dy * xh, axis=0, keepdims=True)
        red_ref[1:2, :] += jnp.sum(err * err, axis=0, keepdims=True)

    tok = pl.BlockSpec((tm, D), lambda i: (i, 0))
    small = pl.BlockSpec((8, D), lambda i: (0, 0))
    return pl.pallas_call(
        body, name="final_loss", grid=(S // tm,),
        in_specs=[tok, small, tok], out_specs=[tok, small],
        out_shape=[jax.ShapeDtypeStruct((S, D), F32), jax.ShapeDtypeStruct((8, D), F32)],
        compiler_params=_cparams(1),
    )(x, gvec, tgt)


def _adamw(w, g, m, v):
    shape = w.shape
    C = shape[-1]
    R = w.size // C
    tr = 256 if R % 256 == 0 else R

    def body(w_ref, g_ref, m_ref, v_ref, d_ref, nm_ref, nv_ref):
        gv = g_ref[...]
        nm = ADAM_B1 * m_ref[...] + (1.0 - ADAM_B1) * gv
        nv = ADAM_B2 * v_ref[...] + (1.0 - ADAM_B2) * (gv * gv)
        m_hat = nm / (1.0 - ADAM_B1 ** ADAM_STEP)
        v_hat = nv / (1.0 - ADAM_B2 ** ADAM_STEP)
        d_ref[...] = -ADAM_LR * (m_hat / (jnp.sqrt(v_hat) + ADAM_EPS) + ADAM_WD * w_ref[...])
        nm_ref[...] = nm
        nv_ref[...] = nv

    spec = pl.BlockSpec((tr, C), lambda i: (i, 0))
    outs = pl.pallas_call(
        body, name="adamw", grid=(R // tr,),
        in_specs=[spec] * 4, out_specs=[spec] * 3,
        out_shape=[jax.ShapeDtypeStruct((R, C), F32)] * 3,
        compiler_params=_cparams(1),
    )(*(t.reshape(R, C) for t in (w, g, m, v)))
    return tuple(o.reshape(shape) for o in outs)


def _ada_fwd(c_all, w, b):
    L, D, N = w.shape
    tn = 768 if N % 768 == 0 else _tile(N, 512)

    def body(c_ref, w_ref, b_ref, o_ref):
        cv = c_ref[...]
        cond = (cv * jax.nn.sigmoid(cv)).astype(BF)
        o_ref[...] = _dot(cond, w_ref[...].astype(BF)) + b_ref[...]

    return pl.pallas_call(
        body, name="ada_fwd", grid=(L, N // tn),
        in_specs=[pl.BlockSpec((8, D), lambda l, n: (0, 0)), pl.BlockSpec((None, D, tn), lambda l, n: (l, 0, n)),
                  pl.BlockSpec((None, 1, tn), lambda l, n: (l, 0, n))],
        out_specs=pl.BlockSpec((None, 8, tn), lambda l, n: (l, 0, n)),
        out_shape=jax.ShapeDtypeStruct((L, 8, N), F32),
        compiler_params=_cparams(2),
    )(c_all, w, b)


def _ada_wgrad(c_all_t, dm):
    D = c_all_t.shape[0]
    L, _, N = dm.shape
    tn = 256

    def body(c_ref, dm_ref, o_ref):
        cv = c_ref[...]
        cond = cv * jax.nn.sigmoid(cv)
        acc = cond[:, 0:1] * dm_ref[0:1, :]
        for b in range(1, 8):
            acc = acc + cond[:, b:b + 1] * dm_ref[b:b + 1, :]
        o_ref[...] = acc

    return pl.pallas_call(
        body, name="ada_wgrad", grid=(L, N // tn),
        in_specs=[pl.BlockSpec((D, 8), lambda l, n: (0, 0)), pl.BlockSpec((None, 8, tn), lambda l, n: (l, 0, n))],
        out_specs=pl.BlockSpec((None, D, tn), lambda l, n: (l, 0, n)),
        out_shape=jax.ShapeDtypeStruct((L, D, N), F32),
        compiler_params=_cparams(2),
    )(c_all_t, dm)


def _sum8(g):
    _, R, C = g.shape

    def body(g_ref, o_ref):
        acc = g_ref[0]
        for b in range(1, 8):
            acc = acc + g_ref[b]
        o_ref[...] = acc

    return pl.pallas_call(body, name="sum8", out_shape=jax.ShapeDtypeStruct((R, C), F32),
                          in_specs=[pl.BlockSpec(memory_space=pltpu.VMEM)],
                          out_specs=pl.BlockSpec(memory_space=pltpu.VMEM))(g)


def _pair_add(g4, recv, cidx):
    _, _, ha, B = g4.shape
    tr = _row_tile(ha)

    def body(c_ref, g_ref, r_ref, o_ref):
        o_ref[...] = (g_ref[...] + r_ref[...]).astype(BF)

    return pl.pallas_call(
        body, name="pair_add",
        grid_spec=pltpu.PrefetchScalarGridSpec(
            num_scalar_prefetch=1, grid=(4, ha // tr),
            in_specs=[pl.BlockSpec((None, None, tr, B), lambda s, i, c: (s, c[0], i, 0)),
                      pl.BlockSpec((None, tr, B), lambda s, i, c: (s, i, 0))],
            out_specs=pl.BlockSpec((None, tr, B), lambda s, i, c: (s, i, 0))),
        out_shape=jax.ShapeDtypeStruct((4, ha, B), BF),
        compiler_params=_cparams(2),
    )(cidx, g4, recv)


def _sum_partials(part, recv, gbuf, n_layers, l, place):
    _, ha, B = part.shape
    tr = _row_tile(ha)

    def body(pc_ref, p_ref, r_ref, *rest):
        o_ref = rest[-1]
        o_ref[...] = ((p_ref[...].astype(F32) + r_ref[0].astype(F32)) + r_ref[1].astype(F32)) + r_ref[2].astype(F32)

    in_specs = [pl.BlockSpec((None, tr, B), lambda i, pc: (pc[0], i, 0)), pl.BlockSpec((3, tr, B), lambda i, pc: (0, i, 0))]
    ops = [part, recv]
    if gbuf is not None:
        in_specs.append(ANY)
        ops.append(gbuf)
    return pl.pallas_call(
        body, name="sum_partials",
        grid_spec=pltpu.PrefetchScalarGridSpec(
            num_scalar_prefetch=1, grid=(ha // tr,), in_specs=in_specs,
            out_specs=pl.BlockSpec((None, None, tr, B), lambda i, pc: (l, pc[1], i, 0))),
        out_shape=jax.ShapeDtypeStruct((n_layers, 2, ha, B), F32),
        input_output_aliases={} if gbuf is None else {3: 0},
        compiler_params=_cparams(1),
    )(place, *ops)


def _cast_shard(shard, l, place):
    _, _, ha, B = shard.shape
    tr = _row_tile(ha)

    def body(pc_ref, s_ref, o_ref):
        o_ref[...] = s_ref[...].astype(BF)

    return pl.pallas_call(
        body, name="cast_shard",
        grid_spec=pltpu.PrefetchScalarGridSpec(
            num_scalar_prefetch=1, grid=(2, ha // tr),
            in_specs=[pl.BlockSpec((None, None, tr, B), lambda h, i, pc: (l, h, i, 0))],
            out_specs=pl.BlockSpec((None, None, tr, B), lambda h, i, pc: (pc[0], h, i, 0))),
        out_shape=jax.ShapeDtypeStruct((4, 2, ha, B), BF),
        compiler_params=_cparams(2),
    )(place, shard)


def _gather8(v):
    R, C = v.shape

    def body(v_ref, o_ref, ssem, rsem):
        x, y, c, *_ = _place()
        me = 4 * x + 2 * y + c
        o_ref[me] = v_ref[...]
        cps = []
        for k in range(1, 8):
            fx, fy, fc = (k >> 2) & 1, (k >> 1) & 1, k & 1
            peer = (x ^ fx, y ^ fy, c ^ fc)
            cp = pltpu.make_async_remote_copy(src_ref=v_ref, dst_ref=o_ref.at[me], send_sem=ssem.at[k - 1],
                                              recv_sem=rsem.at[k - 1], device_id=peer, device_id_type=MESH)
            cp.start()
            cps.append((cp, 4 * peer[0] + 2 * peer[1] + peer[2]))
        for k, (cp, slot) in enumerate(cps):
            there = o_ref.at[slot]
            pltpu.make_async_remote_copy(src_ref=there, dst_ref=there, send_sem=ssem.at[k], recv_sem=rsem.at[k],
                                         device_id=(x, y, c), device_id_type=MESH).wait_recv()
        for cp, _ in cps:
            cp.wait_send()

    vm = pl.BlockSpec(memory_space=pltpu.VMEM)
    return pl.pallas_call(
        body, name="gather8", in_specs=[vm], out_specs=vm, out_shape=jax.ShapeDtypeStruct((8, R, C), F32),
        scratch_shapes=[pltpu.SemaphoreType.DMA((7,)), pltpu.SemaphoreType.DMA((7,))],
    )(v)


def _mods_to_owner(ms):
    _, R, C = ms.shape

    def body(m_ref, o_ref, ssem, rsem):
        x, y, c, p, chips, qs = _place()
        o_ref[p] = m_ref[2 * p + c]
        cps = []
        for j in range(3):
            cp = pltpu.make_async_remote_copy(src_ref=m_ref.at[2 * qs[j] + c], dst_ref=o_ref.at[p], send_sem=ssem.at[j],
                                              recv_sem=rsem.at[j], device_id=(*chips[j], c), device_id_type=MESH)
            cp.start()
            cps.append(cp)
        for j in range(3):
            there = o_ref.at[qs[j]]
            pltpu.make_async_remote_copy(src_ref=there, dst_ref=there, send_sem=ssem.at[j], recv_sem=rsem.at[j],
                                         device_id=(x, y, c), device_id_type=MESH).wait_recv()
        for cp in cps:
            cp.wait_send()

    vm = pl.BlockSpec(memory_space=pltpu.VMEM)
    return pl.pallas_call(
        body, name="mods_to_owner", in_specs=[vm], out_specs=vm, out_shape=jax.ShapeDtypeStruct((4, R, C), F32),
        scratch_shapes=[pltpu.SemaphoreType.DMA((3,)), pltpu.SemaphoreType.DMA((3,))],
    )(ms)


def _vec(*rows):
    D = rows[0].shape[-1]
    rows = [r.reshape(1, D) for r in rows]
    return jnp.concatenate(rows + [jnp.zeros((8 - len(rows), D), F32)], axis=0)


def _stage(l, n_conv):
    items = [("ffn1_w_in", l), ("ffn1_w_out", l), ("ffn2_w_in", l), ("ffn2_w_out", l)]
    if l < n_conv:
        items += [("conv_w_in", l), ("conv_w_out", l)]
    else:
        items += [("attn_w_q", l - n_conv), ("attn_w_o", l - n_conv)]
    if l == n_conv:
        items.append(("w_kv", 0))
    return items


def _as_weight(name, buf):
    _, _, ha, B = buf.shape
    return buf.reshape(8 * ha, B) if name.endswith("w_out") else buf.reshape(4, 2 * ha, B)


def _local_step(x, tgt, tabs, norm_g, conv_w, kv_norm_g, final_norm_g, mods, kvmods, cast, place):
    S, D = x.shape
    n_layers, n_conv = norm_g.shape[0], conv_w.shape[0]
    md = mods.reshape(n_layers, 9, D)

    def vec_of(l, k):
        return _vec(norm_g[l, k], md[l, 3 * k + 1], md[l, 3 * k], md[l, 3 * k + 2])

    W = {}

    def install(items, bufs):
        for (name, idx), b in zip(items, bufs):
            W[name, idx] = _as_weight(name, b)

    first = _stage(0, n_conv)
    install(first, _comm_call("gather_d2d", _comm_call("gather_ici", [cast[it] for it in first])))

    vec_kv = _vec(kv_norm_g, kvmods[D:], kvmods[:D])
    saved = {}
    xc = x
    kv = None
    for l in range(n_layers):
        nxt = _stage(l + 1, n_conv) if l + 1 < n_layers else None
        if l == n_conv:
            h_kv, kv = _proj_fwd(xc, vec_kv, W["w_kv", 0], tabs, GROUP_W * 3 // LANES)
            saved["kv"] = (xc, h_kv)
        v0 = vec_of(l, 0)
        xo, h, ab, y, *arrived = _ffn_fwd(xc, v0, W["ffn1_w_in", l], W["ffn1_w_out", l],
                                          ("gather_ici", [cast[it] for it in nxt]) if nxt else None)
        saved[l, 0] = (xc, v0, h, ab, y)
        xc = xo
        v1 = vec_of(l, 1)
        if l < n_conv:
            cw = _vec(conv_w[l, 0], conv_w[l, 1], conv_w[l, 2])
            xo, h, s4, z, y = _conv_fwd(xc, v1, cw, W["conv_w_in", l], W["conv_w_out", l])
            saved[l, 1] = (xc, v1, h, y, cw, s4, z)
        else:
            j = l - n_conv
            h, q = _proj_fwd(xc, v1, W["attn_w_q", j], tabs, GROUP_W * 3 // LANES)
            og = [_attn_fwd(q, kv, g) for g in range(3)]
            mix, lj, y, xo = _attn_out([o for o, _ in og], [s for _, s in og], xc, v1, W["attn_w_o", j])
            saved[l, 1] = (xc, v1, h, y, q, mix, lj)
        xc = xo
        v2 = vec_of(l, 2)
        xo, h, ab, y, *full = _ffn_fwd(xc, v2, W["ffn2_w_in", l], W["ffn2_w_out", l],
                                       ("gather_d2d", arrived) if nxt else None)
        if nxt:
            install(nxt, full)
        saved[l, 2] = (xc, v2, h, ab, y)
        xc = xo

    dx, red_f = _final_loss(xc, _vec(final_norm_g), tgt)
    loss_part = 0.5 * jnp.sum(red_f[1]) / D
    G, parts, others = {}, {}, {}
    dmods = [[None] * 9 for _ in range(n_layers)]
    dnorm = [[None] * 3 for _ in range(n_layers)]
    dconvw = [None] * n_conv
    dkv_parts = []

    def halves(items):
        return [G[it].reshape(4, 2, G[it].shape[1] // 2, G[it].shape[2]) for it in items]

    def pair_sums(items, recv):
        for it, g, r in zip(items, halves(items), recv):
            parts[it] = _pair_add(g, r, place[1:])
        return [parts[it] for it in items]

    def ffn_bwd(dx, l, k, name, comm):
        xin, v, h, ab, y = saved[l, k]
        w_in, w_out = W[name + "_w_in", l], W[name + "_w_out", l]
        dy, u, dab, dh, *moved = _ffn_dgrad(dx, v, ab, w_in, w_out, comm)
        G[name + "_w_in", l] = _mm_tn(h, dab, w_in.shape[2])
        G[name + "_w_out", l] = _mm_tn(u, dy[None], D).reshape(4, -1, D)
        return (*_norm_bwd(dx, [dh], xin, y, v, 0.5), moved)

    def note(l, k, red):
        dnorm[l][k] = red[0]
        dmods[l][3 * k], dmods[l][3 * k + 1], dmods[l][3 * k + 2] = red[2], red[1], red[3]

    for l in reversed(range(n_layers)):
        done = _stage(l + 1, n_conv) if l + 1 < n_layers else None
        dx, red, recv = ffn_bwd(dx, l, 2, "ffn2", ("exchange", halves(done)) if done else None)
        note(l, 2, red)
        if l < n_conv:
            xin, v, h, y, cw, s4, z = saved[l, 1]
            w_in, w_out = W["conv_w_in", l], W["conv_w_out", l]
            dy, d3, dh, dcw = _conv_dgrad(dx, v, cw, s4, w_in, w_out)
            G["conv_w_in", l] = _mm_tn(h, d3[None], w_in.shape[2])
            G["conv_w_out", l] = _mm_tn(z, dy[None], D).reshape(4, -1, D)
            dconvw[l] = dcw[0:3]
        else:
            j = l - n_conv
            xin, v, h, y, q, mix, lj = saved[l, 1]
            wq, wo = W["attn_w_q", j], W["attn_w_o", j]
            dy, dmix = _attn_out_dgrad(dx, v, wo)
            G["attn_w_o", j] = _mm_tn(mix, dy[None], wo.shape[2])
            dqs = [_attn_dq(q, kv, dmix, mix, lj, g) for g in range(3)]
            dkv_parts.append([_attn_dkv(q, kv, dmix, mix, lj, g) for g in range(3)])
            dq, dh = _proj_dgrad([[t] for t in dqs], tabs, wq, 3)
            G["attn_w_q", j] = _mm_tn(h, dq[None], wq.shape[2])
        dx, red = _norm_bwd(dx, [dh], xin, y, v, 1.0)
        note(l, 1, red)
        dx, red, got = ffn_bwd(dx, l, 0, "ffn1", ("scatter", pair_sums(done, recv)) if done else None)
        note(l, 0, red)
        if done:
            others.update(zip(done, got))
        if l == n_conv:
            xin, h_kv = saved["kv"]
            kv_parts = [[lay[g][0] for lay in dkv_parts] for g in range(3)] + [[lay[g][1] for lay in dkv_parts] for g in range(3)]
            dkv, dh = _proj_dgrad(kv_parts, tabs, W["w_kv", 0], 3)
            G["w_kv", 0] = _mm_tn(h_kv, dkv[None], W["w_kv", 0].shape[2])
            dx, red_kv = _norm_bwd(dx, [dh], xin, None, vec_kv, 1.0)
    others.update(zip(first, _comm_call("scatter", pair_sums(first, _comm_call("exchange", halves(first))))))
    small = jnp.concatenate(
        [jnp.stack([jnp.stack(r) for r in dmods]).reshape(-1), red_kv[2], red_kv[1],
         jnp.stack([jnp.stack(r) for r in dnorm]).reshape(-1), jnp.stack(dconvw).reshape(-1), red_kv[0], red_f[0]])
    return loss_part, dx, parts, others, small


BIG = ("ffn1_w_in", "ffn1_w_out", "ffn2_w_in", "ffn2_w_out", "conv_w_in", "conv_w_out", "w_kv", "attn_w_q", "attn_w_o")


def _halved(t):
    if t.ndim == 2:
        t = t[None]
    L, A, B = t.shape
    return t.reshape(L, 2, A // 2, B)


def kernel(x, c, positions, norm_g, ada_w, ada_b, ffn1_w_in, ffn1_w_out, ffn2_w_in, ffn2_w_out, conv_w_in, conv_w, conv_w_out, kv_norm_g, kv_ada_w, kv_ada_b, w_kv, attn_w_q, attn_w_o, final_norm_g, loss_target, m_norm_g, m_ada_w, m_ada_b, m_ffn1_w_in, m_ffn1_w_out, m_ffn2_w_in, m_ffn2_w_out, m_conv_w_in, m_conv_w, m_conv_w_out, m_kv_norm_g, m_kv_ada_w, m_kv_ada_b, m_w_kv, m_attn_w_q, m_attn_w_o, m_final_norm_g, v_norm_g, v_ada_w, v_ada_b, v_ffn1_w_in, v_ffn1_w_out, v_ffn2_w_in, v_ffn2_w_out, v_conv_w_in, v_conv_w, v_conv_w_out, v_kv_norm_g, v_kv_ada_w, v_kv_ada_b, v_w_kv, v_attn_w_q, v_attn_w_o, v_final_norm_g):
    wts = dict(norm_g=norm_g, ada_w=ada_w, ada_b=ada_b, ffn1_w_in=ffn1_w_in, ffn1_w_out=ffn1_w_out, ffn2_w_in=ffn2_w_in,
               ffn2_w_out=ffn2_w_out, conv_w_in=conv_w_in, conv_w=conv_w, conv_w_out=conv_w_out, kv_norm_g=kv_norm_g,
               kv_ada_w=kv_ada_w, kv_ada_b=kv_ada_b, w_kv=w_kv, attn_w_q=attn_w_q, attn_w_o=attn_w_o,
               final_norm_g=final_norm_g)
    ms = dict(norm_g=m_norm_g, ada_w=m_ada_w, ada_b=m_ada_b, ffn1_w_in=m_ffn1_w_in, ffn1_w_out=m_ffn1_w_out,
              ffn2_w_in=m_ffn2_w_in, ffn2_w_out=m_ffn2_w_out, conv_w_in=m_conv_w_in, conv_w=m_conv_w,
              conv_w_out=m_conv_w_out, kv_norm_g=m_kv_norm_g, kv_ada_w=m_kv_ada_w, kv_ada_b=m_kv_ada_b, w_kv=m_w_kv,
              attn_w_q=m_attn_w_q, attn_w_o=m_attn_w_o, final_norm_g=m_final_norm_g)
    vs = dict(norm_g=v_norm_g, ada_w=v_ada_w, ada_b=v_ada_b, ffn1_w_in=v_ffn1_w_in, ffn1_w_out=v_ffn1_w_out,
              ffn2_w_in=v_ffn2_w_in, ffn2_w_out=v_ffn2_w_out, conv_w_in=v_conv_w_in, conv_w=v_conv_w,
              conv_w_out=v_conv_w_out, kv_norm_g=v_kv_norm_g, kv_ada_w=v_kv_ada_w, kv_ada_b=v_kv_ada_b, w_kv=v_w_kv,
              attn_w_q=v_attn_w_q, attn_w_o=v_attn_w_o, final_norm_g=v_final_norm_g)
    order = list(wts)
    S, D = x.shape[1], x.shape[2]
    n_layers, n_conv = norm_g.shape[0], conv_w.shape[0]
    nm, sw = ada_w.shape[2], norm_g.shape[2]
    ix, iy, ic = (lax.axis_index(a) for a in AXES)
    chip = 2 * ix + iy

    pack = jnp.concatenate([norm_g.reshape(-1), conv_w.reshape(-1), c.reshape(-1)])
    npad = (-pack.size) % (8 * LANES)
    allp = _gather8(jnp.pad(pack, (0, npad)).reshape(-1, LANES)).reshape(8, -1)
    n1, n2 = norm_g.size, norm_g.size + conv_w.size
    by_chip = allp[0::2]
    norm_full = jnp.moveaxis(by_chip[:, :n1].reshape(4, n_layers, 3, sw), 0, 2).reshape(n_layers, 3, 4 * sw)
    conv_full = jnp.moveaxis(by_chip[:, n1:n2].reshape(4, n_conv, 3, sw), 0, 2).reshape(n_conv, 3, 4 * sw)
    c_all = allp[:, n2:n2 + D]

    b_l = lax.dynamic_slice_in_dim(ada_b, chip * nm, nm, axis=1)
    mod_sh = _ada_fwd(c_all, ada_w, b_l[:, None, :])
    nkv = kv_ada_w.shape[1]
    bkv = lax.dynamic_slice_in_dim(kv_ada_b, chip * nkv, nkv, axis=0)
    kv_sh = _ada_fwd(c_all, kv_ada_w[None], bkv[None, None, :])
    rows = jnp.concatenate([jnp.moveaxis(mod_sh, 0, 1), jnp.pad(jnp.moveaxis(kv_sh, 0, 1), ((0, 0), (0, 0), (0, nm - nkv)))], axis=1)
    rpad = (-rows.shape[1]) % 8
    mine = _mods_to_owner(jnp.pad(rows, ((0, 0), (0, rpad), (0, 0))))
    mods = jnp.moveaxis(mine[:, :n_layers], 0, 1).reshape(n_layers, 4 * nm)
    kvmods = mine[:, n_layers, :nkv].reshape(4 * nkv)

    place = jnp.stack([chip, ic]).astype(jnp.int32)
    halved = {k: _halved(wts[k]) for k in BIG}
    cast = {(k, l): _cast_shard(halved[k], l, place) for k in BIG for l in range(halved[k].shape[0])}

    loss_part, dx, parts, others, small = _local_step(x[0], loss_target[0], _rope_tables(positions), norm_full, conv_full,
                                                      kv_norm_g, final_norm_g, mods, kvmods, cast, place)
    loss = lax.psum(loss_part, AXES)

    gbufs = []
    for k in BIG:
        gbuf, n_l = None, halved[k].shape[0]
        for l in range(n_l):
            gbuf = _sum_partials(parts[k, l], others[k, l], gbuf, n_l, l, place)
        gbufs.append(gbuf)
    grads = {k: s.reshape(wts[k].shape) for k, s in zip(BIG, _comm_call("share", gbufs))}

    spad = (-small.size) % (8 * LANES)
    gath = _gather8(jnp.pad(small, (0, spad)).reshape(-1, LANES))
    tot = _sum8(gath).reshape(-1)
    gath = gath.reshape(8, -1)
    o = 0

    def take(n):
        nonlocal o
        o += n
        return tot[o - n:o]

    g_mods = take(n_layers * 9 * D).reshape(n_layers, 9 * D)
    g_kvmods = take(2 * D)
    g_norm = take(n_layers * 3 * D).reshape(n_layers, 3, D)
    g_convw = take(n_conv * 3 * D).reshape(n_conv, 3, D)
    g_kvn = take(D)
    g_fin = take(D)
    grads["ada_b"] = g_mods
    grads["kv_ada_b"] = g_kvmods
    grads["norm_g"] = lax.dynamic_slice_in_dim(g_norm, chip * sw, sw, axis=2)
    grads["conv_w"] = lax.dynamic_slice_in_dim(g_convw, chip * sw, sw, axis=2)
    grads["kv_norm_g"] = g_kvn
    grads["final_norm_g"] = g_fin

    dm_all = gath[:, :n_layers * 9 * D].reshape(8, n_layers, 9 * D)
    dm_mine = jnp.moveaxis(lax.dynamic_slice_in_dim(dm_all, chip * nm, nm, axis=2), 0, 1)
    dkv_all = gath[:, n_layers * 9 * D:n_layers * 9 * D + 2 * D]
    dkv_mine = lax.dynamic_slice_in_dim(dkv_all, chip * nkv, nkv, axis=1)[None]
    c_t = c_all.T
    grads["ada_w"] = _ada_wgrad(c_t, dm_mine)
    grads["kv_ada_w"] = _ada_wgrad(c_t, dkv_mine)[0]

    deltas, new_m, new_v = {}, {}, {}
    for k in order:
        shp = wts[k].shape
        two_d = (lambda t: t.reshape(1, -1)) if len(shp) == 1 else (lambda t: t)
        dlt, nmk, nvk = _adamw(two_d(wts[k]), two_d(grads[k]), two_d(ms[k]), two_d(vs[k]))
        deltas[k], new_m[k], new_v[k] = dlt.reshape(shp), nmk.reshape(shp), nvk.reshape(shp)
    return (loss, dx[None], *[grads[k] for k in order], *[deltas[k] for k in order], *[new_m[k] for k in order],
            *[new_v[k] for k in order])
```

```python
import functools
import math

import jax
import jax.numpy as jnp
from jax import lax
from jax.experimental import pallas as pl
from jax.experimental.pallas import tpu as pltpu

F32 = jnp.float32
BF = jnp.bfloat16
MESH = pl.DeviceIdType.MESH
AXES = ("x", "y", "c")

NORM_EPS = 1e-5
HEAD_DIM = 64
HEADS_PER_GROUP = 8
GROUP_W = HEADS_PER_GROUP * HEAD_DIM
DILATIONS = (1, 4, 16)
BAND = 128
ROPE_DIM = 16
ROPE_THETA = 500000.0
SM_SCALE = HEAD_DIM ** -0.5
NEG = -1e30
ADAM_LR, ADAM_B1, ADAM_B2, ADAM_EPS, ADAM_WD, ADAM_STEP = 0.001, 0.9, 0.999, 1e-08, 0.01, 10

V7X_VMEM_BYTES = 64 * 1024 * 1024
VMEM_LIMIT = V7X_VMEM_BYTES - 6 * 1024 * 1024
LANES = 128
TM = 512
TK = 512


def _cparams(ngrid):
    return pltpu.CompilerParams(dimension_semantics=("arbitrary",) * ngrid, vmem_limit_bytes=VMEM_LIMIT)


def _dot(a, b):
    return jnp.dot(a, b, preferred_element_type=F32)


def _dot_nt(a, b):
    return lax.dot_general(a, b, (((1,), (1,)), ((), ())), preferred_element_type=F32)


def _dot_tn(a, b):
    return lax.dot_general(a, b, (((0,), (0,)), ((), ())), preferred_element_type=F32)


def _tile(n, pref):
    t = min(n, pref)
    while n % t:
        t //= 2
    return t


def _row_tile(n, cap=256, mult=16):
    best = n
    for t in range(mult, min(n, cap) + 1, mult):
        if n % t == 0:
            best = t
    return best


def _rstd(x):
    return lax.rsqrt(jnp.mean(x * x, axis=-1, keepdims=True) + NORM_EPS)


def _norm_mod(x, v):
    return (x * _rstd(x) * v[0:1]) * (1.0 + v[1:2]) + v[2:3]


def _silu_parts(a):
    sg = jax.nn.sigmoid(a)
    return sg, a * sg


def _row(i):
    return lambda *_: (i, 0)


ANY = pl.BlockSpec(memory_space=pl.ANY)
IN_PLACE = ("gather_ici", "gather_d2d", "share")


def _place():
    x, y, c = (lax.axis_index(a) for a in AXES)
    chips = [(1 - x, y), (x, 1 - y), (1 - x, 1 - y)]
    return x, y, c, 2 * x + y, chips, [2 * cx + cy for cx, cy in chips]


def _transfers(kind, ins, outs):
    x, y, c, p, chips, qs = _place()
    sib = (x, y, 1 - c)
    rows = []
    for k, o in enumerate(outs):
        if kind == "gather_ici":
            rows.append([(o.at[p, c], o.at[p, c], (*chips[j], c), o.at[qs[j], c]) for j in range(3)])
        elif kind == "gather_d2d":
            rows.append([(o.at[qs[j], c], o.at[qs[j], c], sib, o.at[qs[j], 1 - c]) for j in range(3)])
        elif kind == "exchange":
            rows.append([(ins[k].at[:, 1 - c], o, sib, o)])
        elif kind == "scatter":
            rows.append([(ins[k].at[qs[j]], o.at[j], (*chips[j], c), o.at[j]) for j in range(3)])
        elif kind == "share":
            rows.append([(o.at[:, c], o.at[:, c], sib, o.at[:, 1 - c])])
    return rows


def _comm_out_shapes(kind, arrays):
    if kind in IN_PLACE:
        return [jax.ShapeDtypeStruct(a.shape, a.dtype) for a in arrays]
    if kind == "exchange":
        return [jax.ShapeDtypeStruct((4,) + a.shape[2:], a.dtype) for a in arrays]
    return [jax.ShapeDtypeStruct((3,) + a.shape[1:], a.dtype) for a in arrays]


def _comm_sems(n):
    return [pltpu.SemaphoreType.DMA((n, 3)), pltpu.SemaphoreType.DMA((n, 3))]


def _comm_start(rows, ssem, rsem):
    for k, row in enumerate(rows):
        for j, (src, dst, dev, _) in enumerate(row):
            pltpu.make_async_remote_copy(src_ref=src, dst_ref=dst, send_sem=ssem.at[k, j], recv_sem=rsem.at[k, j],
                                         device_id=dev, device_id_type=MESH).start()


def _comm_wait(rows, ssem, rsem):
    for k, row in enumerate(rows):
        for j, (src, dst, dev, land) in enumerate(row):
            pltpu.make_async_remote_copy(src_ref=src, dst_ref=dst, send_sem=ssem.at[k, j], recv_sem=rsem.at[k, j],
                                         device_id=dev, device_id_type=MESH).wait_send()
            pltpu.make_async_remote_copy(src_ref=land, dst_ref=land, send_sem=ssem.at[k, j], recv_sem=rsem.at[k, j],
                                         device_id=dev, device_id_type=MESH).wait_recv()


def _comm_call(kind, arrays):
    n = len(arrays)

    def body(*refs):
        rows = _transfers(kind, refs[:n], refs[n:2 * n])
        _comm_start(rows, *refs[2 * n:])
        _comm_wait(rows, *refs[2 * n:])

    return pl.pallas_call(
        body, name="comm_" + kind, in_specs=[ANY] * n, out_specs=[ANY] * n, out_shape=_comm_out_shapes(kind, arrays),
        input_output_aliases={k: k for k in range(n)} if kind in IN_PLACE else {},
        scratch_shapes=_comm_sems(n),
    )(*arrays)


def _ride(comm, n_in, n_out):
    if comm is None:
        return dict(arrays=[], in_specs=[], out_specs=[], out_shape=[], aliases={}, scratch=[], n=0)
    kind, arrays = comm
    n = len(arrays)
    aliases = {n_in + k: n_out + k for k in range(n)} if kind in IN_PLACE else {}
    return dict(arrays=list(arrays), in_specs=[ANY] * n, out_specs=[ANY] * n, out_shape=_comm_out_shapes(kind, arrays),
                aliases=aliases, scratch=_comm_sems(n), n=n)


def _ride_when(comm, ins, outs, sems, cond, action):
    if comm is None:
        return

    @pl.when(cond)
    def _():
        action(_transfers(comm[0], ins, outs), *sems)


def _ffn_fwd(x, vec, w_in4, w_out, comm=None):
    S, D = x.shape
    cs = w_in4.shape[2]
    F = 2 * cs
    tm = _tile(S, TM)
    ni = S // tm
    ride = _ride(comm, 5, 4)
    nc = ride["n"]

    def body(*refs):
        x_ref, vec_ref, wa_ref, wb_ref, wo_ref = refs[:5]
        xo_ref, h_ref, ab_ref, y_ref = refs[5 + nc:9 + nc]
        i, j = pl.program_id(0), pl.program_id(1)
        riding = (comm, refs[5:5 + nc], refs[9 + nc:9 + 2 * nc], refs[9 + 2 * nc:])
        _ride_when(*riding, jnp.logical_and(i == 0, j == 0), _comm_start)

        @pl.when(j == 0)
        def _():
            h_ref[...] = _norm_mod(x_ref[...], vec_ref[...]).astype(BF)
            y_ref[...] = jnp.zeros_like(y_ref)

        h = h_ref[...]
        a = _dot(h, wa_ref[...])
        b = _dot(h, wb_ref[...])
        ab_ref[0] = a.astype(BF)
        ab_ref[1] = b.astype(BF)
        _, s = _silu_parts(a)
        y_ref[...] += _dot((s * b).astype(BF), wo_ref[...])

        @pl.when(j == 1)
        def _():
            xo_ref[...] = x_ref[...] + (0.5 * (1.0 + vec_ref[3:4, :])) * y_ref[...]

        _ride_when(*riding, jnp.logical_and(i == ni - 1, j == 1), _comm_wait)

    tok = pl.BlockSpec((tm, D), lambda i, j: (i, 0))
    return pl.pallas_call(
        body, name="ffn_fwd", grid=(ni, 2),
        in_specs=[tok, pl.BlockSpec((8, D), lambda i, j: (0, 0)),
                  pl.BlockSpec((None, D, cs), lambda i, j: (j, 0, 0)),
                  pl.BlockSpec((None, D, cs), lambda i, j: (j + 2, 0, 0)),
                  pl.BlockSpec((cs, D), lambda i, j: (j, 0))] + ride["in_specs"],
        out_specs=[tok, tok, pl.BlockSpec((2, tm, cs), lambda i, j: (0, i, j)), tok] + ride["out_specs"],
        out_shape=[jax.ShapeDtypeStruct((S, D), F32), jax.ShapeDtypeStruct((S, D), BF),
                   jax.ShapeDtypeStruct((2, S, F), BF), jax.ShapeDtypeStruct((S, D), F32)] + ride["out_shape"],
        input_output_aliases=ride["aliases"], scratch_shapes=ride["scratch"],
        compiler_params=_cparams(2),
    )(x, vec, w_in4, w_in4, w_out, *ride["arrays"])


def _ffn_dgrad(dxo, vec, ab, w_in4, w_out, comm=None):
    S, D = dxo.shape
    cs = w_in4.shape[2]
    F = 2 * cs
    tm = _tile(S, TM)
    ni = S // tm
    ride = _ride(comm, 6, 4)
    nc = ride["n"]

    def body(*refs):
        dxo_ref, vec_ref, ab_ref, wa_ref, wb_ref, wo_ref = refs[:6]
        dy_ref, u_ref, dab_ref, dh_ref = refs[6 + nc:10 + nc]
        i, j = pl.program_id(0), pl.program_id(1)
        riding = (comm, refs[6:6 + nc], refs[10 + nc:10 + 2 * nc], refs[10 + 2 * nc:])
        _ride_when(*riding, jnp.logical_and(i == 0, j == 0), _comm_start)

        @pl.when(j == 0)
        def _():
            dy_ref[...] = ((0.5 * (1.0 + vec_ref[3:4, :])) * dxo_ref[...]).astype(BF)
            dh_ref[...] = jnp.zeros_like(dh_ref)

        du = _dot_nt(dy_ref[...], wo_ref[...])
        a = ab_ref[0].astype(F32)
        b = ab_ref[1].astype(F32)
        sg, s = _silu_parts(a)
        u_ref[...] = (s * b).astype(BF)
        da = (du * b * (sg * (1.0 + a * (1.0 - sg)))).astype(BF)
        db = (du * s).astype(BF)
        dab_ref[0] = da
        dab_ref[1] = db
        dh_ref[...] += _dot_nt(da, wa_ref[...]) + _dot_nt(db, wb_ref[...])
        _ride_when(*riding, jnp.logical_and(i == ni - 1, j == 1), _comm_wait)

    tok = pl.BlockSpec((tm, D), lambda i, j: (i, 0))
    return pl.pallas_call(
        body, name="ffn_dgrad", grid=(ni, 2),
        in_specs=[tok, pl.BlockSpec((8, D), lambda i, j: (0, 0)),
                  pl.BlockSpec((2, tm, cs), lambda i, j: (0, i, j)),
                  pl.BlockSpec((None, D, cs), lambda i, j: (j, 0, 0)),
                  pl.BlockSpec((None, D, cs), lambda i, j: (j + 2, 0, 0)),
                  pl.BlockSpec((cs, D), lambda i, j: (j, 0))] + ride["in_specs"],
        out_specs=[tok, pl.BlockSpec((tm, cs), lambda i, j: (i, j)),
                   pl.BlockSpec((2, tm, cs), lambda i, j: (0, i, j)), tok] + ride["out_specs"],
        out_shape=[jax.ShapeDtypeStruct((S, D), BF), jax.ShapeDtypeStruct((S, F), BF),
                   jax.ShapeDtypeStruct((2, S, F), BF), jax.ShapeDtypeStruct((S, D), F32)] + ride["out_shape"],
        input_output_aliases=ride["aliases"], scratch_shapes=ride["scratch"],
        compiler_params=_cparams(2),
    )(dxo, vec, ab, w_in4, w_in4, w_out, *ride["arrays"])


def _mm_tn(a, b3, cs):
    K, M = a.shape
    G, _, Nb = b3.shape
    N = G * Nb
    tn = math.gcd(cs, Nb)
    tmo = M if M <= 1024 else M // 2
    tk = _tile(K, TK)

    def body(a_ref, b_ref, o_ref):
        @pl.when(pl.program_id(2) == 0)
        def _():
            o_ref[...] = jnp.zeros_like(o_ref)

        o_ref[...] += _dot_tn(a_ref[...].astype(BF), b_ref[...].astype(BF))

    return pl.pallas_call(
        body, name="wgrad_tn", grid=(M // tmo, N // tn, K // tk),
        in_specs=[pl.BlockSpec((tk, tmo), lambda m, n, k: (k, m)),
                  pl.BlockSpec((None, tk, tn), lambda m, n, k: ((n * tn) // Nb, k, ((n * tn) % Nb) // tn))],
        out_specs=pl.BlockSpec((None, tmo, tn), lambda m, n, k: ((n * tn) // cs, m, ((n * tn) % cs) // tn)),
        out_shape=jax.ShapeDtypeStruct((N // cs, M, cs), F32),
        compiler_params=_cparams(3),
    )(a, b3)


def _norm_bwd(dxo, dhs, x, y, vec, coef):
    S, D = x.shape
    tm = _tile(S, TM)
    nh = len(dhs)
    has_y = y is not None

    def body(*refs):
        dxo_ref = refs[0]
        dh_refs = refs[1:1 + nh]
        x_ref = refs[1 + nh]
        y_ref = refs[2 + nh] if has_y else None
        vec_ref, dx_ref, red_ref = refs[-3:]

        @pl.when(pl.program_id(0) == 0)
        def _():
            red_ref[...] = jnp.zeros_like(red_ref)

        xv = x_ref[...]
        dxo = dxo_ref[...]
        dh = dh_refs[0][...]
        for r in dh_refs[1:]:
            dh = dh + r[...]
        g = vec_ref[0:1, :]
        sc = vec_ref[1:2, :]
        r = _rstd(xv)
        xh = xv * r
        dhn = dh * (1.0 + sc)
        dxh = dhn * g
        dx_ref[...] = dxo + r * (dxh - xh * jnp.mean(dxh * xh, axis=-1, keepdims=True))
        red_ref[0:1, :] += jnp.sum(dhn * xh, axis=0, keepdims=True)
        red_ref[1:2, :] += jnp.sum(dh * (xh * g), axis=0, keepdims=True)
        red_ref[2:3, :] += jnp.sum(dh, axis=0, keepdims=True)
        if has_y:
            red_ref[3:4, :] += coef * jnp.sum(dxo * y_ref[...], axis=0, keepdims=True)

    tok = pl.BlockSpec((tm, D), lambda i: (i, 0))
    small = pl.BlockSpec((8, D), lambda i: (0, 0))
    ops = [dxo, *dhs, x] + ([y] if has_y else []) + [vec]
    return pl.pallas_call(
        body, name="norm_bwd", grid=(S // tm,),
        in_specs=[tok] * (len(ops) - 1) + [small],
        out_specs=[tok, small],
        out_shape=[jax.ShapeDtypeStruct((S, D), F32), jax.ShapeDtypeStruct((8, D), F32)],
        compiler_params=_cparams(1),
    )(*ops)


def _conv_fwd(x, vec, cw, w_in4, w_out):
    S, D = x.shape
    cs = w_in4.shape[2]
    tm = _tile(S, TM)

    def body(x_ref, vec_ref, cw_ref, wi_ref, wo_ref, xo_ref, h_ref, s4_ref, z_ref, y_ref, vs_ref):
        @pl.when(pl.program_id(0) == 0)
        def _():
            vs_ref[0:8, :] = jnp.zeros((8, D), F32)

        xv = x_ref[...]
        h = _norm_mod(xv, vec_ref[...]).astype(BF)
        h_ref[...] = h
        bcu = jnp.concatenate([_dot(h, wi_ref[q]) for q in range(4)], axis=1)
        bg, cg, u = bcu[:, :D], bcu[:, D:2 * D], bcu[:, 2 * D:]
        v = cg * u
        vs_ref[8:8 + tm, :] = v
        conv = cw_ref[0:1, :] * vs_ref[pl.ds(6, tm), :] + cw_ref[1:2, :] * vs_ref[pl.ds(7, tm), :] + cw_ref[2:3, :] * v
        vs_ref[0:8, :] = vs_ref[tm:tm + 8, :]
        z = (bg * conv).astype(BF)
        s4_ref[0] = bg.astype(BF)
        s4_ref[1] = cg.astype(BF)
        s4_ref[2] = u.astype(BF)
        s4_ref[3] = conv.astype(BF)
        z_ref[...] = z
        y = _dot(z, wo_ref[...])
        y_ref[...] = y
        xo_ref[...] = xv + (1.0 + vec_ref[3:4, :]) * y

    tok = pl.BlockSpec((tm, D), lambda i: (i, 0))
    small = pl.BlockSpec((8, D), lambda i: (0, 0))
    return pl.pallas_call(
        body, name="conv_fwd", grid=(S // tm,),
        in_specs=[tok, small, small, pl.BlockSpec((4, D, cs), lambda i: (0, 0, 0)),
                  pl.BlockSpec((D, D), lambda i: (0, 0))],
        out_specs=[tok, tok, pl.BlockSpec((4, tm, D), lambda i: (0, i, 0)), tok, tok],
        out_shape=[jax.ShapeDtypeStruct((S, D), F32), jax.ShapeDtypeStruct((S, D), BF),
                   jax.ShapeDtypeStruct((4, S, D), BF), jax.ShapeDtypeStruct((S, D), BF),
                   jax.ShapeDtypeStruct((S, D), F32)],
        scratch_shapes=[pltpu.VMEM((tm + 8, D), F32)],
        compiler_params=_cparams(1),
    )(x, vec, cw, w_in4, w_out)


def _conv_dgrad(dxo, vec, cw, s4, w_in4, w_out):
    S, D = dxo.shape
    cs = w_in4.shape[2]
    tm = _tile(S, TM)
    nt = S // tm

    def body(dxo_ref, vec_ref, cw_ref, s4_ref, wi_ref, wo_ref, dy_ref, d3_ref, dh_ref, dcw_ref, ds_ref):
        @pl.when(pl.program_id(0) == 0)
        def _():
            ds_ref[tm:tm + 8, :] = jnp.zeros((8, D), F32)
            dcw_ref[...] = jnp.zeros_like(dcw_ref)

        dy = ((1.0 + vec_ref[3:4, :]) * dxo_ref[...]).astype(BF)
        dy_ref[...] = dy
        dz = _dot_nt(dy, wo_ref[...])
        bg = s4_ref[0].astype(F32)
        cg = s4_ref[1].astype(F32)
        u = s4_ref[2].astype(F32)
        conv = s4_ref[3].astype(F32)
        dbg = dz * conv
        dconv = dz * bg
        ds_ref[0:tm, :] = dconv
        d1 = ds_ref[pl.ds(1, tm), :]
        d2 = ds_ref[pl.ds(2, tm), :]
        ds_ref[tm:tm + 8, :] = ds_ref[0:8, :]
        dv = cw_ref[2:3, :] * dconv + cw_ref[1:2, :] * d1 + cw_ref[0:1, :] * d2
        v = cg * u
        dcw_ref[0:1, :] += jnp.sum(d2 * v, axis=0, keepdims=True)
        dcw_ref[1:2, :] += jnp.sum(d1 * v, axis=0, keepdims=True)
        dcw_ref[2:3, :] += jnp.sum(dconv * v, axis=0, keepdims=True)
        dbcu = jnp.concatenate([dbg, dv * u, dv * cg], axis=1).astype(BF)
        d3_ref[...] = dbcu
        dh = _dot_nt(dbcu[:, 0:cs], wi_ref[0])
        for q in range(1, 4):
            dh = dh + _dot_nt(dbcu[:, q * cs:(q + 1) * cs], wi_ref[q])
        dh_ref[...] = dh

    tok = pl.BlockSpec((tm, D), lambda i: (nt - 1 - i, 0))
    small = pl.BlockSpec((8, D), lambda i: (0, 0))
    return pl.pallas_call(
        body, name="conv_dgrad", grid=(nt,),
        in_specs=[tok, small, small, pl.BlockSpec((4, tm, D), lambda i: (0, nt - 1 - i, 0)),
                  pl.BlockSpec((4, D, cs), lambda i: (0, 0, 0)), pl.BlockSpec((D, D), lambda i: (0, 0))],
        out_specs=[tok, pl.BlockSpec((tm, 3 * D), lambda i: (nt - 1 - i, 0)), tok, small],
        out_shape=[jax.ShapeDtypeStruct((S, D), BF), jax.ShapeDtypeStruct((S, 3 * D), BF),
                   jax.ShapeDtypeStruct((S, D), F32), jax.ShapeDtypeStruct((8, D), F32)],
        scratch_shapes=[pltpu.VMEM((tm + 8, D), F32)],
        compiler_params=_cparams(1),
    )(dxo, vec, cw, s4, w_in4, w_out)


def _rope_tables(positions):
    S = positions.shape[-1]
    inv = ROPE_THETA ** (-jnp.arange(0, ROPE_DIM, 2, dtype=F32) / ROPE_DIM)
    ang = positions.reshape(S, 1).astype(F32) * inv
    cos = jnp.tile(jnp.cos(ang), (1, LANES // 8))
    sin = jnp.tile(jnp.sin(ang), (1, LANES // 8))
    l64 = jnp.arange(LANES) % HEAD_DIM
    return jnp.stack([jnp.where(l64 < ROPE_DIM, cos, 1.0),
                      jnp.where(l64 < ROPE_DIM // 2, -sin, 0.0),
                      jnp.where((l64 >= ROPE_DIM // 2) & (l64 < ROPE_DIM), sin, 0.0)])


def _rope(t, tab_ref):
    return t * tab_ref[0] + pltpu.roll(t, LANES - 8, 1) * tab_ref[1] + pltpu.roll(t, 8, 1) * tab_ref[2]


def _rope_t(d, tab_ref):
    return d * tab_ref[0] + pltpu.roll(d * tab_ref[1], 8, 1) + pltpu.roll(d * tab_ref[2], LANES - 8, 1)


GROUP_CH = GROUP_W // LANES


def _to_dilated(src_ref, c0, dst_ref, d):
    n = src_ref.shape[1]
    for r in range(d):
        for ch in range(GROUP_CH):
            rows = src_ref[c0 + ch] if d == 1 else src_ref.at[c0 + ch][pl.ds(r, n // d, stride=d), :]
            dst_ref[:, r * GROUP_W + ch * LANES:r * GROUP_W + (ch + 1) * LANES] = rows.astype(dst_ref.dtype)


def _from_dilated(val, dst_ref, c0, d):
    n = dst_ref.shape[1]
    for r in range(d):
        for ch in range(GROUP_CH):
            cols = val[:, r * GROUP_W + ch * LANES:r * GROUP_W + (ch + 1) * LANES]
            if d == 1:
                dst_ref[c0 + ch] = cols
            else:
                dst_ref.at[c0 + ch][pl.ds(r, n // d, stride=d), :] = cols


def _group(ref, c0):
    return jnp.concatenate([ref[c0 + ch] for ch in range(GROUP_CH)], axis=1)


def _put_group(ref, c0, val):
    for ch in range(GROUP_CH):
        ref[c0 + ch] = val[:, ch * LANES:(ch + 1) * LANES]


def _dil_spec(tm, d):
    return pl.BlockSpec((tm // d, d * GROUP_W), lambda i: (i, 0))


def _proj_fwd(x, vec, w4, tabs, n_rope, dils, scale=1.0):
    S, D = x.shape
    cs = w4.shape[2]
    N = 4 * cs
    tm = _tile(S, TM)

    def body(x_ref, vec_ref, w_ref, tab_ref, h_ref, *rest):
        outs, acc_ref = rest[:-1], rest[-1]
        h = _norm_mod(x_ref[...], vec_ref[...]).astype(BF)
        h_ref[...] = h
        per = cs // LANES
        for q in range(4):
            acc = _dot(h, w_ref[q])
            if scale != 1.0:
                acc = acc * scale
            cols = [acc[:, ch * LANES:(ch + 1) * LANES] for ch in range(per)]
            for ch, t in enumerate(cols):
                acc_ref[q * per + ch] = _rope(t, tab_ref) if q * per + ch < n_rope else t
        for j, d in enumerate(dils):
            _to_dilated(acc_ref, j * GROUP_CH, outs[j], d)

    h, *outs = pl.pallas_call(
        body, name="proj_fwd", grid=(S // tm,),
        in_specs=[pl.BlockSpec((tm, D), lambda i: (i, 0)), pl.BlockSpec((8, D), lambda i: (0, 0)),
                  pl.BlockSpec((4, D, cs), lambda i: (0, 0, 0)), pl.BlockSpec((3, tm, LANES), lambda i: (0, i, 0))],
        out_specs=[pl.BlockSpec((tm, D), lambda i: (i, 0))] + [_dil_spec(tm, d) for d in dils],
        out_shape=[jax.ShapeDtypeStruct((S, D), BF)] + [jax.ShapeDtypeStruct((S // d, d * GROUP_W), BF) for d in dils],
        scratch_shapes=[pltpu.VMEM((N // LANES, tm, LANES), F32)],
        compiler_params=_cparams(1),
    )(x, vec, w4, tabs)
    return h, outs


def _proj_dgrad(parts, tabs, w4, n_rope_groups, dils):
    cs, D = w4.shape[2], w4.shape[1]
    N = 4 * cs
    S = parts[0][0].shape[0] * dils[0]
    tm = _tile(S, TM)
    flat = [p for grp in parts for p in grp]
    counts = [len(grp) for grp in parts]

    def body(*refs):
        prefs = refs[:len(flat)]
        tab_ref, w_ref, dz_ref, dh_ref, z_ref = refs[len(flat):]
        k = 0
        for j, cnt in enumerate(counts):
            z = prefs[k][...]
            for r in prefs[k + 1:k + cnt]:
                z = z + r[...]
            k += cnt
            _from_dilated(z, z_ref, 0, dils[j])
            if j < n_rope_groups:
                z = jnp.concatenate([_rope_t(z_ref[ch], tab_ref) for ch in range(GROUP_CH)], axis=1)
            else:
                z = _group(z_ref, 0)
            dz_ref[:, j * GROUP_W:(j + 1) * GROUP_W] = z.astype(BF)
        dh = _dot_nt(dz_ref[:, 0:cs], w_ref[0])
        for q in range(1, 4):
            dh = dh + _dot_nt(dz_ref[:, q * cs:(q + 1) * cs], w_ref[q])
        dh_ref[...] = dh

    return pl.pallas_call(
        body, name="proj_dgrad", grid=(S // tm,),
        in_specs=[_dil_spec(tm, d) for d, cnt in zip(dils, counts) for _ in range(cnt)]
        + [pl.BlockSpec((3, tm, LANES), lambda i: (0, i, 0)), pl.BlockSpec((4, D, cs), lambda i: (0, 0, 0))],
        out_specs=[pl.BlockSpec((tm, N), lambda i: (i, 0)), pl.BlockSpec((tm, D), lambda i: (i, 0))],
        out_shape=[jax.ShapeDtypeStruct((S, N), BF), jax.ShapeDtypeStruct((S, D), F32)],
        scratch_shapes=[pltpu.VMEM((GROUP_CH, tm, LANES), F32)],
        compiler_params=_cparams(1),
    )(*flat, tabs, w4)


def _band_masks():
    qi = lax.broadcasted_iota(jnp.int32, (BAND, BAND), 0)
    kj = lax.broadcasted_iota(jnp.int32, (BAND, BAND), 1)
    return kj <= qi, kj >= qi


def _head(ref, h):
    return ref[:, h * HEAD_DIM:(h + 1) * HEAD_DIM]


def _attn_fwd(q, k, v, d):
    sd = q.shape[0]
    nb = sd // BAND

    def body(q_ref, kp_ref, kc_ref, vp_ref, vc_ref, o_ref, l_ref):
        same, prev = _band_masks()
        prev = jnp.logical_and(prev, pl.program_id(1) > 0)
        for h in range(HEADS_PER_GROUP):
            qh = _head(q_ref, h)
            sc = jnp.where(same, _dot_nt(qh, _head(kc_ref, h)), NEG)
            sp = jnp.where(prev, _dot_nt(qh, _head(kp_ref, h)), NEG)
            m = jnp.maximum(jnp.max(sc, axis=-1, keepdims=True), jnp.max(sp, axis=-1, keepdims=True))
            pc = jnp.exp(sc - m)
            pp = jnp.exp(sp - m)
            den = jnp.sum(pc, axis=-1, keepdims=True) + jnp.sum(pp, axis=-1, keepdims=True)
            o = (_dot(pc.astype(BF), _head(vc_ref, h)) + _dot(pp.astype(BF), _head(vp_ref, h))) / den
            o_ref[:, h * HEAD_DIM:(h + 1) * HEAD_DIM] = o
            l_ref[:, h * HEAD_DIM:(h + 1) * HEAD_DIM] = jnp.broadcast_to(m + jnp.log(den), (BAND, HEAD_DIM))

    blk = (BAND, GROUP_W)
    cur = pl.BlockSpec(blk, lambda r, i: (i, r))
    prv = pl.BlockSpec(blk, lambda r, i: (jnp.maximum(i - 1, 0), r))
    return pl.pallas_call(
        body, name=f"attn_fwd_d{d}", grid=(d, nb),
        in_specs=[cur, prv, cur, prv, cur], out_specs=[cur, cur],
        out_shape=[jax.ShapeDtypeStruct((sd, d * GROUP_W), F32)] * 2,
        compiler_params=_cparams(2),
    )(q, k, k, v, v)


def _attn_out(os, ls, x, vec, wo4):
    S, D = x.shape
    cs = wo4.shape[2]
    tm = _tile(S, TM)
    ng = len(DILATIONS)

    def body(*refs):
        o_refs, l_refs = refs[0:2 * ng:2], refs[1:2 * ng:2]
        x_ref, vec_ref, wo_ref = refs[2 * ng:2 * ng + 3]
        outs = refs[2 * ng + 3:]
        mix_refs, lj_refs = outs[0:2 * ng:2], outs[1:2 * ng:2]
        y_ref, xo_ref, nat_ref = outs[2 * ng:]
        ov, lv = [], []
        for g in range(ng):
            _from_dilated(o_refs[g][...], nat_ref, 2 * g * GROUP_CH, DILATIONS[g])
            _from_dilated(l_refs[g][...], nat_ref, (2 * g + 1) * GROUP_CH, DILATIONS[g])
            ov.append(_group(nat_ref, 2 * g * GROUP_CH))
            lv.append(_group(nat_ref, (2 * g + 1) * GROUP_CH))
        mx = jnp.maximum(jnp.maximum(lv[0], lv[1]), lv[2])
        es = [jnp.exp(t - mx) for t in lv]
        den = es[0] + es[1] + es[2]
        mix = (es[0] * ov[0] + es[1] * ov[1] + es[2] * ov[2]) / den
        at_mix, at_lj = 2 * ng * GROUP_CH, (2 * ng + 1) * GROUP_CH
        _put_group(nat_ref, at_mix, mix)
        _put_group(nat_ref, at_lj, mx + jnp.log(den))
        for g in range(ng):
            _to_dilated(nat_ref, at_mix, mix_refs[g], DILATIONS[g])
            _to_dilated(nat_ref, at_lj, lj_refs[g], DILATIONS[g])
        mb = mix.astype(BF)
        y = jnp.concatenate([_dot(mb, wo_ref[q]) for q in range(4)], axis=1)
        y_ref[...] = y
        xo_ref[...] = x_ref[...] + (1.0 + vec_ref[3:4, :]) * y

    tok = pl.BlockSpec((tm, D), lambda i: (i, 0))
    dil = [_dil_spec(tm, d) for d in DILATIONS for _ in range(2)]
    dil_shape = [jax.ShapeDtypeStruct((S // d, d * GROUP_W), F32) for d in DILATIONS for _ in range(2)]
    outs = pl.pallas_call(
        body, name="attn_out", grid=(S // tm,),
        in_specs=dil + [tok, pl.BlockSpec((8, D), lambda i: (0, 0)), pl.BlockSpec((4, GROUP_W, cs), lambda i: (0, 0, 0))],
        out_specs=dil + [tok, tok],
        out_shape=dil_shape + [jax.ShapeDtypeStruct((S, D), F32)] * 2,
        scratch_shapes=[pltpu.VMEM(((2 * ng + 2) * GROUP_CH, tm, LANES), F32)],
        compiler_params=_cparams(1),
    )(*[t for pair in zip(os, ls) for t in pair], x, vec, wo4)
    return list(outs[0:2 * ng:2]), list(outs[1:2 * ng:2]), outs[2 * ng], outs[2 * ng + 1]


def _attn_out_dgrad(dxo, vec, wo4):
    S, D = dxo.shape
    cs = wo4.shape[2]
    tm = _tile(S, TM)

    def body(dxo_ref, vec_ref, wo_ref, dy_ref, *rest):
        dm_refs, nat_ref = rest[:-1], rest[-1]
        dy = ((1.0 + vec_ref[3:4, :]) * dxo_ref[...]).astype(BF)
        dy_ref[...] = dy
        dm = _dot_nt(dy[:, 0:cs], wo_ref[0])
        for q in range(1, 4):
            dm = dm + _dot_nt(dy[:, q * cs:(q + 1) * cs], wo_ref[q])
        _put_group(nat_ref, 0, dm)
        for g, d in enumerate(DILATIONS):
            _to_dilated(nat_ref, 0, dm_refs[g], d)

    tok = pl.BlockSpec((tm, D), lambda i: (i, 0))
    dy, *dms = pl.pallas_call(
        body, name="attn_out_dgrad", grid=(S // tm,),
        in_specs=[tok, pl.BlockSpec((8, D), lambda i: (0, 0)), pl.BlockSpec((4, GROUP_W, cs), lambda i: (0, 0, 0))],
        out_specs=[tok] + [_dil_spec(tm, d) for d in DILATIONS],
        out_shape=[jax.ShapeDtypeStruct((S, D), BF)] + [jax.ShapeDtypeStruct((S // d, d * GROUP_W), F32) for d in DILATIONS],
        scratch_shapes=[pltpu.VMEM((GROUP_CH, tm, LANES), F32)],
        compiler_params=_cparams(1),
    )(dxo, vec, wo4)
    return dy, dms


def _prob_and_ds(qh, kh, vh, doh, oh, lrow, mask):
    s = jnp.where(mask, _dot_nt(qh, kh), NEG)
    p = jnp.exp(s - lrow)
    delta = jnp.sum(doh * oh, axis=-1, keepdims=True)
    dp = _dot_nt(doh.astype(BF), vh)
    return p, p * (dp - delta)


def _attn_dq(q, k, v, dmix, mix, lj, d):
    sd = q.shape[0]
    nb = sd // BAND

    def body(q_ref, kp_ref, kc_ref, vp_ref, vc_ref, do_ref, o_ref, l_ref, dq_ref):
        same, prev = _band_masks()
        prev = jnp.logical_and(prev, pl.program_id(1) > 0)
        for h in range(HEADS_PER_GROUP):
            qh, doh, oh = _head(q_ref, h), _head(do_ref, h), _head(o_ref, h)
            lrow = l_ref[:, h * HEAD_DIM:h * HEAD_DIM + 1]
            kc, kp = _head(kc_ref, h), _head(kp_ref, h)
            _, dsc = _prob_and_ds(qh, kc, _head(vc_ref, h), doh, oh, lrow, same)
            _, dsp = _prob_and_ds(qh, kp, _head(vp_ref, h), doh, oh, lrow, prev)
            dq_ref[:, h * HEAD_DIM:(h + 1) * HEAD_DIM] = (_dot(dsc.astype(BF), kc) + _dot(dsp.astype(BF), kp)) * SM_SCALE

    blk = (BAND, GROUP_W)
    cur = pl.BlockSpec(blk, lambda r, i: (i, r))
    prv = pl.BlockSpec(blk, lambda r, i: (jnp.maximum(i - 1, 0), r))
    return pl.pallas_call(
        body, name=f"attn_dq_d{d}", grid=(d, nb),
        in_specs=[cur, prv, cur, prv, cur, cur, cur, cur], out_specs=cur,
        out_shape=jax.ShapeDtypeStruct((sd, d * GROUP_W), F32),
        compiler_params=_cparams(2),
    )(q, k, k, v, v, dmix, mix, lj)


def _attn_dkv(q, k, v, dmix, mix, lj, d):
    sd = q.shape[0]
    nb = sd // BAND

    def body(k_ref, v_ref, qs_ref, qn_ref, dos_ref, don_ref, os_ref, on_ref, ls_ref, ln_ref, dk_ref, dv_ref):
        same, nxt = _band_masks()
        nxt = jnp.logical_and(nxt, pl.program_id(1) < nb - 1)
        for h in range(HEADS_PER_GROUP):
            kh, vh = _head(k_ref, h), _head(v_ref, h)
            dk = jnp.zeros((BAND, HEAD_DIM), F32)
            dv = jnp.zeros((BAND, HEAD_DIM), F32)
            for q_ref, do_ref, o_ref, l_ref, mask in ((qs_ref, dos_ref, os_ref, ls_ref, same),
                                                       (qn_ref, don_ref, on_ref, ln_ref, nxt)):
                qh, doh = _head(q_ref, h), _head(do_ref, h)
                p, ds = _prob_and_ds(qh, kh, vh, doh, _head(o_ref, h), l_ref[:, h * HEAD_DIM:h * HEAD_DIM + 1], mask)
                dv = dv + _dot_tn(p.astype(BF), doh.astype(BF))
                dk = dk + _dot_tn(ds.astype(BF), qh)
            dk_ref[:, h * HEAD_DIM:(h + 1) * HEAD_DIM] = dk
            dv_ref[:, h * HEAD_DIM:(h + 1) * HEAD_DIM] = dv

    blk = (BAND, GROUP_W)
    cur = pl.BlockSpec(blk, lambda r, i: (i, r))
    nxt_spec = pl.BlockSpec(blk, lambda r, i: (jnp.minimum(i + 1, nb - 1), r))
    return pl.pallas_call(
        body, name=f"attn_dkv_d{d}", grid=(d, nb),
        in_specs=[cur, cur, cur, nxt_spec, cur, nxt_spec, cur, nxt_spec, cur, nxt_spec],
        out_specs=[cur, cur],
        out_shape=[jax.ShapeDtypeStruct((sd, d * GROUP_W), F32)] * 2,
        compiler_params=_cparams(2),
    )(k, v, q, q, dmix, dmix, mix, mix, lj, lj)


def _final_loss(x, gvec, tgt):
    S, D = x.shape
    tm = _tile(S, TM)

    def body(x_ref, g_ref, t_ref, dx_ref, red_ref):
        @pl.when(pl.program_id(0) == 0)
        def _():
            red_ref[...] = jnp.zeros_like(red_ref)

        xv = x_ref[...]
        g = g_ref[0:1, :]
        r = _rstd(xv)
        xh = xv * r
        err = xh * g - t_ref[...]
        dy = err * (1.0 / D)
        dxh = dy * g
        dx_ref[...] = r * (dxh - xh * jnp.mean(dxh * xh, axis=-1, keepdims=True))
        red_ref[0:1, :] += jnp.sum(dy * xh, axis=0, keepdims=True)
        red_ref[1:2, :] += jnp.sum(err * err, axis=0, keepdims=True)

    tok = pl.BlockSpec((tm, D), lambda i: (i, 0))
    small = pl.BlockSpec((8, D), lambda i: (0, 0))
    return pl.pallas_call(
        body, name="final_loss", grid=(S // tm,),
        in_specs=[tok, small, tok], out_specs=[tok, small],
        out_shape=[jax.ShapeDtypeStruct((S, D), F32), jax.ShapeDtypeStruct((8, D), F32)],
        compiler_params=_cparams(1),
    )(x, gvec, tgt)


def _adamw(w, g, m, v):
    shape = w.shape
    C = shape[-1]
    R = w.size // C
    tr = 256 if R % 256 == 0 else R

    def body(w_ref, g_ref, m_ref, v_ref, d_ref, nm_ref, nv_ref):
        gv = g_ref[...]
        nm = ADAM_B1 * m_ref[...] + (1.0 - ADAM_B1) * gv
        nv = ADAM_B2 * v_ref[...] + (1.0 - ADAM_B2) * (gv * gv)
        m_hat = nm / (1.0 - ADAM_B1 ** ADAM_STEP)
        v_hat = nv / (1.0 - ADAM_B2 ** ADAM_STEP)
        d_ref[...] = -ADAM_LR * (m_hat / (jnp.sqrt(v_hat) + ADAM_EPS) + ADAM_WD * w_ref[...])
        nm_ref[...] = nm
        nv_ref[...] = nv

    spec = pl.BlockSpec((tr, C), lambda i: (i, 0))
    outs = pl.pallas_call(
        body, name="adamw", grid=(R // tr,),
        in_specs=[spec] * 4, out_specs=[spec] * 3,
        out_shape=[jax.ShapeDtypeStruct((R, C), F32)] * 3,
        compiler_params=_cparams(1),
    )(*(t.reshape(R, C) for t in (w, g, m, v)))
    return tuple(o.reshape(shape) for o in outs)


def _ada_fwd(c_all, w, b):
    L, D, N = w.shape
    tn = 768 if N % 768 == 0 else _tile(N, 512)

    def body(c_ref, w_ref, b_ref, o_ref):
        cv = c_ref[...]
        cond = (cv * jax.nn.sigmoid(cv)).astype(BF)
        o_ref[...] = _dot(cond, w_ref[...].astype(BF)) + b_ref[...]

    return pl.pallas_call(
        body, name="ada_fwd", grid=(L, N // tn),
        in_specs=[pl.BlockSpec((8, D), lambda l, n: (0, 0)), pl.BlockSpec((None, D, tn), lambda l, n: (l, 0, n)),
                  pl.BlockSpec((None, 1, tn), lambda l, n: (l, 0, n))],
        out_specs=pl.BlockSpec((None, 8, tn), lambda l, n: (l, 0, n)),
        out_shape=jax.ShapeDtypeStruct((L, 8, N), F32),
        compiler_params=_cparams(2),
    )(c_all, w, b)


def _ada_wgrad(c_all_t, dm):
    D = c_all_t.shape[0]
    L, _, N = dm.shape
    tn = 256

    def body(c_ref, dm_ref, o_ref):
        cv = c_ref[...]
        cond = cv * jax.nn.sigmoid(cv)
        acc = cond[:, 0:1] * dm_ref[0:1, :]
        for b in range(1, 8):
            acc = acc + cond[:, b:b + 1] * dm_ref[b:b + 1, :]
        o_ref[...] = acc

    return pl.pallas_call(
        body, name="ada_wgrad", grid=(L, N // tn),
        in_specs=[pl.BlockSpec((D, 8), lambda l, n: (0, 0)), pl.BlockSpec((None, 8, tn), lambda l, n: (l, 0, n))],
        out_specs=pl.BlockSpec((None, D, tn), lambda l, n: (l, 0, n)),
        out_shape=jax.ShapeDtypeStruct((L, D, N), F32),
        compiler_params=_cparams(2),
    )(c_all_t, dm)


def _sum8(g):
    _, R, C = g.shape

    def body(g_ref, o_ref):
        acc = g_ref[0]
        for b in range(1, 8):
            acc = acc + g_ref[b]
        o_ref[...] = acc

    return pl.pallas_call(body, name="sum8", out_shape=jax.ShapeDtypeStruct((R, C), F32),
                          in_specs=[pl.BlockSpec(memory_space=pltpu.VMEM)],
                          out_specs=pl.BlockSpec(memory_space=pltpu.VMEM))(g)


def _pair_add(g4, recv, cidx):
    _, _, ha, B = g4.shape
    tr = _row_tile(ha)

    def body(c_ref, g_ref, r_ref, o_ref):
        o_ref[...] = (g_ref[...] + r_ref[...]).astype(BF)

    return pl.pallas_call(
        body, name="pair_add",
        grid_spec=pltpu.PrefetchScalarGridSpec(
            num_scalar_prefetch=1, grid=(4, ha // tr),
            in_specs=[pl.BlockSpec((None, None, tr, B), lambda s, i, c: (s, c[0], i, 0)),
                      pl.BlockSpec((None, tr, B), lambda s, i, c: (s, i, 0))],
            out_specs=pl.BlockSpec((None, tr, B), lambda s, i, c: (s, i, 0))),
        out_shape=jax.ShapeDtypeStruct((4, ha, B), BF),
        compiler_params=_cparams(2),
    )(cidx, g4, recv)


def _sum_partials(part, recv, gbuf, n_layers, l, place):
    _, ha, B = part.shape
    tr = _row_tile(ha)

    def body(pc_ref, p_ref, r_ref, *rest):
        o_ref = rest[-1]
        o_ref[...] = ((p_ref[...].astype(F32) + r_ref[0].astype(F32)) + r_ref[1].astype(F32)) + r_ref[2].astype(F32)

    in_specs = [pl.BlockSpec((None, tr, B), lambda i, pc: (pc[0], i, 0)), pl.BlockSpec((3, tr, B), lambda i, pc: (0, i, 0))]
    ops = [part, recv]
    if gbuf is not None:
        in_specs.append(ANY)
        ops.append(gbuf)
    return pl.pallas_call(
        body, name="sum_partials",
        grid_spec=pltpu.PrefetchScalarGridSpec(
            num_scalar_prefetch=1, grid=(ha // tr,), in_specs=in_specs,
            out_specs=pl.BlockSpec((None, None, tr, B), lambda i, pc: (l, pc[1], i, 0))),
        out_shape=jax.ShapeDtypeStruct((n_layers, 2, ha, B), F32),
        input_output_aliases={} if gbuf is None else {3: 0},
        compiler_params=_cparams(1),
    )(place, *ops)


def _cast_shard(shard, l, place):
    _, _, ha, B = shard.shape
    tr = _row_tile(ha)

    def body(pc_ref, s_ref, o_ref):
        o_ref[...] = s_ref[...].astype(BF)

    return pl.pallas_call(
        body, name="cast_shard",
        grid_spec=pltpu.PrefetchScalarGridSpec(
            num_scalar_prefetch=1, grid=(2, ha // tr),
            in_specs=[pl.BlockSpec((None, None, tr, B), lambda h, i, pc: (l, h, i, 0))],
            out_specs=pl.BlockSpec((None, None, tr, B), lambda h, i, pc: (pc[0], h, i, 0))),
        out_shape=jax.ShapeDtypeStruct((4, 2, ha, B), BF),
        compiler_params=_cparams(2),
    )(place, shard)


def _gather8(v):
    R, C = v.shape

    def body(v_ref, o_ref, ssem, rsem):
        x, y, c, *_ = _place()
        me = 4 * x + 2 * y + c
        o_ref[me] = v_ref[...]
        cps = []
        for k in range(1, 8):
            fx, fy, fc = (k >> 2) & 1, (k >> 1) & 1, k & 1
            peer = (x ^ fx, y ^ fy, c ^ fc)
            cp = pltpu.make_async_remote_copy(src_ref=v_ref, dst_ref=o_ref.at[me], send_sem=ssem.at[k - 1],
                                              recv_sem=rsem.at[k - 1], device_id=peer, device_id_type=MESH)
            cp.start()
            cps.append((cp, 4 * peer[0] + 2 * peer[1] + peer[2]))
        for k, (cp, slot) in enumerate(cps):
            there = o_ref.at[slot]
            pltpu.make_async_remote_copy(src_ref=there, dst_ref=there, send_sem=ssem.at[k], recv_sem=rsem.at[k],
                                         device_id=(x, y, c), device_id_type=MESH).wait_recv()
        for cp, _ in cps:
            cp.wait_send()

    vm = pl.BlockSpec(memory_space=pltpu.VMEM)
    return pl.pallas_call(
        body, name="gather8", in_specs=[vm], out_specs=vm, out_shape=jax.ShapeDtypeStruct((8, R, C), F32),
        scratch_shapes=[pltpu.SemaphoreType.DMA((7,)), pltpu.SemaphoreType.DMA((7,))],
    )(v)


def _mods_to_owner(ms):
    _, R, C = ms.shape

    def body(m_ref, o_ref, ssem, rsem):
        x, y, c, p, chips, qs = _place()
        o_ref[p] = m_ref[2 * p + c]
        cps = []
        for j in range(3):
            cp = pltpu.make_async_remote_copy(src_ref=m_ref.at[2 * qs[j] + c], dst_ref=o_ref.at[p], send_sem=ssem.at[j],
                                              recv_sem=rsem.at[j], device_id=(*chips[j], c), device_id_type=MESH)
            cp.start()
            cps.append(cp)
        for j in range(3):
            there = o_ref.at[qs[j]]
            pltpu.make_async_remote_copy(src_ref=there, dst_ref=there, send_sem=ssem.at[j], recv_sem=rsem.at[j],
                                         device_id=(x, y, c), device_id_type=MESH).wait_recv()
        for cp in cps:
            cp.wait_send()

    vm = pl.BlockSpec(memory_space=pltpu.VMEM)
    return pl.pallas_call(
        body, name="mods_to_owner", in_specs=[vm], out_specs=vm, out_shape=jax.ShapeDtypeStruct((4, R, C), F32),
        scratch_shapes=[pltpu.SemaphoreType.DMA((3,)), pltpu.SemaphoreType.DMA((3,))],
    )(ms)


def _vec(*rows):
    D = rows[0].shape[-1]
    rows = [r.reshape(1, D) for r in rows]
    return jnp.concatenate(rows + [jnp.zeros((8 - len(rows), D), F32)], axis=0)


def _stage(l, n_conv):
    items = [("ffn1_w_in", l), ("ffn1_w_out", l), ("ffn2_w_in", l), ("ffn2_w_out", l)]
    if l < n_conv:
        items += [("conv_w_in", l), ("conv_w_out", l)]
    else:
        items += [("attn_w_q", l - n_conv), ("attn_w_o", l - n_conv)]
    if l == n_conv:
        items.append(("w_kv", 0))
    return items


def _as_weight(name, buf):
    _, _, ha, B = buf.shape
    return buf.reshape(8 * ha, B) if name.endswith("w_out") else buf.reshape(4, 2 * ha, B)


def _local_step(x, tgt, tabs, norm_g, conv_w, kv_norm_g, final_norm_g, mods, kvmods, cast, place):
    S, D = x.shape
    n_layers, n_conv = norm_g.shape[0], conv_w.shape[0]
    md = mods.reshape(n_layers, 9, D)

    def vec_of(l, k):
        return _vec(norm_g[l, k], md[l, 3 * k + 1], md[l, 3 * k], md[l, 3 * k + 2])

    W = {}

    def install(items, bufs):
        for (name, idx), b in zip(items, bufs):
            W[name, idx] = _as_weight(name, b)

    first = _stage(0, n_conv)
    install(first, _comm_call("gather_d2d", _comm_call("gather_ici", [cast[it] for it in first])))

    vec_kv = _vec(kv_norm_g, kvmods[D:], kvmods[:D])
    ng = len(DILATIONS)
    n_rope = ng * GROUP_W // LANES
    saved = {}
    xc = x
    ks = vs = None
    for l in range(n_layers):
        nxt = _stage(l + 1, n_conv) if l + 1 < n_layers else None
        if l == n_conv:
            h_kv, kv = _proj_fwd(xc, vec_kv, W["w_kv", 0], tabs, n_rope, DILATIONS * 2)
            ks, vs = kv[:ng], kv[ng:]
            saved["kv"] = (xc, h_kv)
        v0 = vec_of(l, 0)
        xo, h, ab, y, *arrived = _ffn_fwd(xc, v0, W["ffn1_w_in", l], W["ffn1_w_out", l],
                                          ("gather_ici", [cast[it] for it in nxt]) if nxt else None)
        saved[l, 0] = (xc, v0, h, ab, y)
        xc = xo
        v1 = vec_of(l, 1)
        if l < n_conv:
            cw = _vec(conv_w[l, 0], conv_w[l, 1], conv_w[l, 2])
            xo, h, s4, z, y = _conv_fwd(xc, v1, cw, W["conv_w_in", l], W["conv_w_out", l])
            saved[l, 1] = (xc, v1, h, y, cw, s4, z)
        else:
            j = l - n_conv
            h, qs = _proj_fwd(xc, v1, W["attn_w_q", j], tabs, n_rope, DILATIONS, SM_SCALE)
            og = [_attn_fwd(qs[g], ks[g], vs[g], DILATIONS[g]) for g in range(ng)]
            mixs, ljs, y, xo = _attn_out([o for o, _ in og], [s for _, s in og], xc, v1, W["attn_w_o", j])
            saved[l, 1] = (xc, v1, h, y, qs, mixs, ljs)
        xc = xo
        v2 = vec_of(l, 2)
        xo, h, ab, y, *full = _ffn_fwd(xc, v2, W["ffn2_w_in", l], W["ffn2_w_out", l],
                                       ("gather_d2d", arrived) if nxt else None)
        if nxt:
            install(nxt, full)
        saved[l, 2] = (xc, v2, h, ab, y)
        xc = xo

    dx, red_f = _final_loss(xc, _vec(final_norm_g), tgt)
    loss_part = 0.5 * jnp.sum(red_f[1]) / D
    G, parts, others = {}, {}, {}
    dmods = [[None] * 9 for _ in range(n_layers)]
    dnorm = [[None] * 3 for _ in range(n_layers)]
    dconvw = [None] * n_conv
    dkv_parts = []

    def halves(items):
        return [G[it].reshape(4, 2, G[it].shape[1] // 2, G[it].shape[2]) for it in items]

    def pair_sums(items, recv):
        for it, g, r in zip(items, halves(items), recv):
            parts[it] = _pair_add(g, r, place[1:])
        return [parts[it] for it in items]

    def ffn_bwd(dx, l, k, name, comm):
        xin, v, h, ab, y = saved[l, k]
        w_in, w_out = W[name + "_w_in", l], W[name + "_w_out", l]
        dy, u, dab, dh, *moved = _ffn_dgrad(dx, v, ab, w_in, w_out, comm)
        G[name + "_w_in", l] = _mm_tn(h, dab, w_in.shape[2])
        G[name + "_w_out", l] = _mm_tn(u, dy[None], D).reshape(4, -1, D)
        return (*_norm_bwd(dx, [dh], xin, y, v, 0.5), moved)

    def note(l, k, red):
        dnorm[l][k] = red[0]
        dmods[l][3 * k], dmods[l][3 * k + 1], dmods[l][3 * k + 2] = red[2], red[1], red[3]

    for l in reversed(range(n_layers)):
        done = _stage(l + 1, n_conv) if l + 1 < n_layers else None
        dx, red, recv = ffn_bwd(dx, l, 2, "ffn2", ("exchange", halves(done)) if done else None)
        note(l, 2, red)
        if l < n_conv:
            xin, v, h, y, cw, s4, z = saved[l, 1]
            w_in, w_out = W["conv_w_in", l], W["conv_w_out", l]
            dy, d3, dh, dcw = _conv_dgrad(dx, v, cw, s4, w_in, w_out)
            G["conv_w_in", l] = _mm_tn(h, d3[None], w_in.shape[2])
            G["conv_w_out", l] = _mm_tn(z, dy[None], D).reshape(4, -1, D)
            dconvw[l] = dcw[0:3]
        else:
            j = l - n_conv
            xin, v, h, y, qs, mixs, ljs = saved[l, 1]
            wq, wo = W["attn_w_q", j], W["attn_w_o", j]
            dy, dmixs = _attn_out_dgrad(dx, v, wo)
            G["attn_w_o", j] = _mm_tn(mixs[0], dy[None], wo.shape[2])
            per_group = [(qs[g], ks[g], vs[g], dmixs[g], mixs[g], ljs[g], DILATIONS[g]) for g in range(ng)]
            dqs = [_attn_dq(*a) for a in per_group]
            dkv_parts.append([_attn_dkv(*a) for a in per_group])
            dq, dh = _proj_dgrad([[t] for t in dqs], tabs, wq, ng, DILATIONS)
            G["attn_w_q", j] = _mm_tn(h, dq[None], wq.shape[2])
        dx, red = _norm_bwd(dx, [dh], xin, y, v, 1.0)
        note(l, 1, red)
        dx, red, got = ffn_bwd(dx, l, 0, "ffn1", ("scatter", pair_sums(done, recv)) if done else None)
        note(l, 0, red)
        if done:
            others.update(zip(done, got))
        if l == n_conv:
            xin, h_kv = saved["kv"]
            kv_parts = [[lay[g][0] for lay in dkv_parts] for g in range(ng)] + [[lay[g][1] for lay in dkv_parts] for g in range(ng)]
            dkv, dh = _proj_dgrad(kv_parts, tabs, W["w_kv", 0], ng, DILATIONS * 2)
            G["w_kv", 0] = _mm_tn(h_kv, dkv[None], W["w_kv", 0].shape[2])
            dx, red_kv = _norm_bwd(dx, [dh], xin, None, vec_kv, 1.0)
    others.update(zip(first, _comm_call("scatter", pair_sums(first, _comm_call("exchange", halves(first))))))
    small = jnp.concatenate(
        [jnp.stack([jnp.stack(r) for r in dmods]).reshape(-1), red_kv[2], red_kv[1],
         jnp.stack([jnp.stack(r) for r in dnorm]).reshape(-1), jnp.stack(dconvw).reshape(-1), red_kv[0], red_f[0]])
    return loss_part, dx, parts, others, small


BIG = ("ffn1_w_in", "ffn1_w_out", "ffn2_w_in", "ffn2_w_out", "conv_w_in", "conv_w_out", "w_kv", "attn_w_q", "attn_w_o")


def _halved(t):
    if t.ndim == 2:
        t = t[None]
    L, A, B = t.shape
    return t.reshape(L, 2, A // 2, B)


def kernel(x, c, positions, norm_g, ada_w, ada_b, ffn1_w_in, ffn1_w_out, ffn2_w_in, ffn2_w_out, conv_w_in, conv_w, conv_w_out, kv_norm_g, kv_ada_w, kv_ada_b, w_kv, attn_w_q, attn_w_o, final_norm_g, loss_target, m_norm_g, m_ada_w, m_ada_b, m_ffn1_w_in, m_ffn1_w_out, m_ffn2_w_in, m_ffn2_w_out, m_conv_w_in, m_conv_w, m_conv_w_out, m_kv_norm_g, m_kv_ada_w, m_kv_ada_b, m_w_kv, m_attn_w_q, m_attn_w_o, m_final_norm_g, v_norm_g, v_ada_w, v_ada_b, v_ffn1_w_in, v_ffn1_w_out, v_ffn2_w_in, v_ffn2_w_out, v_conv_w_in, v_conv_w, v_conv_w_out, v_kv_norm_g, v_kv_ada_w, v_kv_ada_b, v_w_kv, v_attn_w_q, v_attn_w_o, v_final_norm_g):
    wts = dict(norm_g=norm_g, ada_w=ada_w, ada_b=ada_b, ffn1_w_in=ffn1_w_in, ffn1_w_out=ffn1_w_out, ffn2_w_in=ffn2_w_in,
               ffn2_w_out=ffn2_w_out, conv_w_in=conv_w_in, conv_w=conv_w, conv_w_out=conv_w_out, kv_norm_g=kv_norm_g,
               kv_ada_w=kv_ada_w, kv_ada_b=kv_ada_b, w_kv=w_kv, attn_w_q=attn_w_q, attn_w_o=attn_w_o,
               final_norm_g=final_norm_g)
    ms = dict(norm_g=m_norm_g, ada_w=m_ada_w, ada_b=m_ada_b, ffn1_w_in=m_ffn1_w_in, ffn1_w_out=m_ffn1_w_out,
              ffn2_w_in=m_ffn2_w_in, ffn2_w_out=m_ffn2_w_out, conv_w_in=m_conv_w_in, conv_w=m_conv_w,
              conv_w_out=m_conv_w_out, kv_norm_g=m_kv_norm_g, kv_ada_w=m_kv_ada_w, kv_ada_b=m_kv_ada_b, w_kv=m_w_kv,
              attn_w_q=m_attn_w_q, attn_w_o=m_attn_w_o, final_norm_g=m_final_norm_g)
    vs = dict(norm_g=v_norm_g, ada_w=v_ada_w, ada_b=v_ada_b, ffn1_w_in=v_ffn1_w_in, ffn1_w_out=v_ffn1_w_out,
              ffn2_w_in=v_ffn2_w_in, ffn2_w_out=v_ffn2_w_out, conv_w_in=v_conv_w_in, conv_w=v_conv_w,
              conv_w_out=v_conv_w_out, kv_norm_g=v_kv_norm_g, kv_ada_w=v_kv_ada_w, kv_ada_b=v_kv_ada_b, w_kv=v_w_kv,
              attn_w_q=v_attn_w_q, attn_w_o=v_attn_w_o, final_norm_g=v_final_norm_g)
    order = list(wts)
    S, D = x.shape[1], x.shape[2]
    n_layers, n_conv = norm_g.shape[0], conv_w.shape[0]
    nm, sw = ada_w.shape[2], norm_g.shape[2]
    ix, iy, ic = (lax.axis_index(a) for a in AXES)
    chip = 2 * ix + iy

    pack = jnp.concatenate([norm_g.reshape(-1), conv_w.reshape(-1), c.reshape(-1)])
    npad = (-pack.size) % (8 * LANES)
    allp = _gather8(jnp.pad(pack, (0, npad)).reshape(-1, LANES)).reshape(8, -1)
    n1, n2 = norm_g.size, norm_g.size + conv_w.size
    by_chip = allp[0::2]
    norm_full = jnp.moveaxis(by_chip[:, :n1].reshape(4, n_layers, 3, sw), 0, 2).reshape(n_layers, 3, 4 * sw)
    conv_full = jnp.moveaxis(by_chip[:, n1:n2].reshape(4, n_conv, 3, sw), 0, 2).reshape(n_conv, 3, 4 * sw)
    c_all = allp[:, n2:n2 + D]

    b_l = lax.dynamic_slice_in_dim(ada_b, chip * nm, nm, axis=1)
    mod_sh = _ada_fwd(c_all, ada_w, b_l[:, None, :])
    nkv = kv_ada_w.shape[1]
    bkv = lax.dynamic_slice_in_dim(kv_ada_b, chip * nkv, nkv, axis=0)
    kv_sh = _ada_fwd(c_all, kv_ada_w[None], bkv[None, None, :])
    rows = jnp.concatenate([jnp.moveaxis(mod_sh, 0, 1), jnp.pad(jnp.moveaxis(kv_sh, 0, 1), ((0, 0), (0, 0), (0, nm - nkv)))], axis=1)
    rpad = (-rows.shape[1]) % 8
    mine = _mods_to_owner(jnp.pad(rows, ((0, 0), (0, rpad), (0, 0))))
    mods = jnp.moveaxis(mine[:, :n_layers], 0, 1).reshape(n_layers, 4 * nm)
    kvmods = mine[:, n_layers, :nkv].reshape(4 * nkv)

    place = jnp.stack([chip, ic]).astype(jnp.int32)
    halved = {k: _halved(wts[k]) for k in BIG}
    cast = {(k, l): _cast_shard(halved[k], l, place) for k in BIG for l in range(halved[k].shape[0])}

    loss_part, dx, parts, others, small = _local_step(x[0], loss_target[0], _rope_tables(positions), norm_full, conv_full,
                                                      kv_norm_g, final_norm_g, mods, kvmods, cast, place)
    loss = lax.psum(loss_part, AXES)

    gbufs = []
    for k in BIG:
        gbuf, n_l = None, halved[k].shape[0]
        for l in range(n_l):
            gbuf = _sum_partials(parts[k, l], others[k, l], gbuf, n_l, l, place)
        gbufs.append(gbuf)
    grads = {k: s.reshape(wts[k].shape) for k, s in zip(BIG, _comm_call("share", gbufs))}

    spad = (-small.size) % (8 * LANES)
    gath = _gather8(jnp.pad(small, (0, spad)).reshape(-1, LANES))
    tot = _sum8(gath).reshape(-1)
    gath = gath.reshape(8, -1)
    o = 0

    def take(n):
        nonlocal o
        o += n
        return tot[o - n:o]

    g_mods = take(n_layers * 9 * D).reshape(n_layers, 9 * D)
    g_kvmods = take(2 * D)
    g_norm = take(n_layers * 3 * D).reshape(n_layers, 3, D)
    g_convw = take(n_conv * 3 * D).reshape(n_conv, 3, D)
    g_kvn = take(D)
    g_fin = take(D)
    grads["ada_b"] = g_mods
    grads["kv_ada_b"] = g_kvmods
    grads["norm_g"] = lax.dynamic_slice_in_dim(g_norm, chip * sw, sw, axis=2)
    grads["conv_w"] = lax.dynamic_slice_in_dim(g_convw, chip * sw, sw, axis=2)
    grads["kv_norm_g"] = g_kvn
    grads["final_norm_g"] = g_fin

    dm_all = gath[:, :n_layers * 9 * D].reshape(8, n_layers, 9 * D)
    dm_mine = jnp.moveaxis(lax.dynamic_slice_in_dim(dm_all, chip * nm, nm, axis=2), 0, 1)
    dkv_all = gath[:, n_layers * 9 * D:n_layers * 9 * D + 2 * D]
    dkv_mine = lax.dynamic_slice_in_dim(dkv_all, chip * nkv, nkv, axis=1)[None]
    c_t = c_all.T
    grads["ada_w"] = _ada_wgrad(c_t, dm_mine)
    grads["kv_ada_w"] = _ada_wgrad(c_t, dkv_mine)[0]

    deltas, new_m, new_v = {}, {}, {}
    for k in order:
        shp = wts[k].shape
        two_d = (lambda t: t.reshape(1, -1)) if len(shp) == 1 else (lambda t: t)
        dlt, nmk, nvk = _adamw(two_d(wts[k]), two_d(grads[k]), two_d(ms[k]), two_d(vs[k]))
        deltas[k], new_m[k], new_v[k] = dlt.reshape(shp), nmk.reshape(shp), nvk.reshape(shp)
    return (loss, dx[None], *[grads[k] for k in order], *[deltas[k] for k in order], *[new_m[k] for k in order],
            *[new_v[k] for k in order])
```

```python
import functools
import math

import jax
import jax.numpy as jnp
from jax import lax
from jax.experimental import pallas as pl
from jax.experimental.pallas import tpu as pltpu

F32 = jnp.float32
BF = jnp.bfloat16
MESH = pl.DeviceIdType.MESH
AXES = ("x", "y", "c")

NORM_EPS = 1e-5
HEAD_DIM = 64
HEADS_PER_GROUP = 8
GROUP_W = HEADS_PER_GROUP * HEAD_DIM
DILATIONS = (1, 4, 16)
BAND = 128
ROPE_DIM = 16
ROPE_THETA = 500000.0
SM_SCALE = HEAD_DIM ** -0.5
NEG = -1e30
ADAM_LR, ADAM_B1, ADAM_B2, ADAM_EPS, ADAM_WD, ADAM_STEP = 0.001, 0.9, 0.999, 1e-08, 0.01, 10

V7X_VMEM_BYTES = 64 * 1024 * 1024
VMEM_LIMIT = V7X_VMEM_BYTES - 6 * 1024 * 1024
LANES = 128
TM = 512
TK = 1024


def _cparams(ngrid):
    return pltpu.CompilerParams(dimension_semantics=("arbitrary",) * ngrid, vmem_limit_bytes=VMEM_LIMIT)


def _dot(a, b):
    return jnp.dot(a, b, preferred_element_type=F32)


def _dot_nt(a, b):
    return lax.dot_general(a, b, (((1,), (1,)), ((), ())), preferred_element_type=F32)


def _dot_tn(a, b):
    return lax.dot_general(a, b, (((0,), (0,)), ((), ())), preferred_element_type=F32)


def _tile(n, pref):
    t = min(n, pref)
    while n % t:
        t //= 2
    return t


def _row_tile(n, cap=256, mult=16):
    best = n
    for t in range(mult, min(n, cap) + 1, mult):
        if n % t == 0:
            best = t
    return best


def _rstd(x):
    return lax.rsqrt(jnp.mean(x * x, axis=-1, keepdims=True) + NORM_EPS)


def _norm_mod(x, v):
    return (x * _rstd(x) * v[0:1]) * (1.0 + v[1:2]) + v[2:3]


def _silu_parts(a):
    sg = jax.nn.sigmoid(a)
    return sg, a * sg


def _row(i):
    return lambda *_: (i, 0)


ANY = pl.BlockSpec(memory_space=pl.ANY)
IN_PLACE = ("gather_ici", "gather_d2d", "share")


def _place():
    x, y, c = (lax.axis_index(a) for a in AXES)
    chips = [(1 - x, y), (x, 1 - y), (1 - x, 1 - y)]
    return x, y, c, 2 * x + y, chips, [2 * cx + cy for cx, cy in chips]


def _transfers(kind, ins, outs):
    x, y, c, p, chips, qs = _place()
    sib = (x, y, 1 - c)
    rows = []
    for k, o in enumerate(outs):
        if kind == "gather_ici":
            rows.append([(o.at[p, c], o.at[p, c], (*chips[j], c), o.at[qs[j], c]) for j in range(3)])
        elif kind == "gather_d2d":
            rows.append([(o.at[qs[j], c], o.at[qs[j], c], sib, o.at[qs[j], 1 - c]) for j in range(3)])
        elif kind == "exchange":
            rows.append([(ins[k].at[:, 1 - c], o, sib, o)])
        elif kind == "scatter":
            rows.append([(ins[k].at[qs[j]], o.at[j], (*chips[j], c), o.at[j]) for j in range(3)])
        elif kind == "share":
            rows.append([(o.at[:, c], o.at[:, c], sib, o.at[:, 1 - c])])
    return rows


def _comm_out_shapes(kind, arrays):
    if kind in IN_PLACE:
        return [jax.ShapeDtypeStruct(a.shape, a.dtype) for a in arrays]
    if kind == "exchange":
        return [jax.ShapeDtypeStruct((4,) + a.shape[2:], a.dtype) for a in arrays]
    return [jax.ShapeDtypeStruct((3,) + a.shape[1:], a.dtype) for a in arrays]


def _comm_sems(n):
    return [pltpu.SemaphoreType.DMA((n, 3)), pltpu.SemaphoreType.DMA((n, 3))]


def _comm_start(rows, ssem, rsem):
    for k, row in enumerate(rows):
        for j, (src, dst, dev, _) in enumerate(row):
            pltpu.make_async_remote_copy(src_ref=src, dst_ref=dst, send_sem=ssem.at[k, j], recv_sem=rsem.at[k, j],
                                         device_id=dev, device_id_type=MESH).start()


def _comm_wait(rows, ssem, rsem):
    for k, row in enumerate(rows):
        for j, (src, dst, dev, land) in enumerate(row):
            pltpu.make_async_remote_copy(src_ref=src, dst_ref=dst, send_sem=ssem.at[k, j], recv_sem=rsem.at[k, j],
                                         device_id=dev, device_id_type=MESH).wait_send()
            pltpu.make_async_remote_copy(src_ref=land, dst_ref=land, send_sem=ssem.at[k, j], recv_sem=rsem.at[k, j],
                                         device_id=dev, device_id_type=MESH).wait_recv()


def _comm_call(kind, arrays):
    n = len(arrays)

    def body(*refs):
        rows = _transfers(kind, refs[:n], refs[n:2 * n])
        _comm_start(rows, *refs[2 * n:])
        _comm_wait(rows, *refs[2 * n:])

    return pl.pallas_call(
        body, name="comm_" + kind, in_specs=[ANY] * n, out_specs=[ANY] * n, out_shape=_comm_out_shapes(kind, arrays),
        input_output_aliases={k: k for k in range(n)} if kind in IN_PLACE else {},
        scratch_shapes=_comm_sems(n),
    )(*arrays)


def _ride(comm, n_in, n_out):
    ride = dict(arrays=[], in_specs=[], out_specs=[], out_shape=[], aliases={}, scratch=[], n=0)
    for kind, arrays in comm or []:
        if kind in IN_PLACE:
            ride["aliases"].update({n_in + ride["n"] + k: n_out + ride["n"] + k for k in range(len(arrays))})
        ride["arrays"] += list(arrays)
        ride["out_shape"] += _comm_out_shapes(kind, arrays)
        ride["scratch"] += _comm_sems(len(arrays))
        ride["n"] += len(arrays)
    ride["in_specs"] = ride["out_specs"] = [ANY] * ride["n"]
    return ride


def _ride_when(comm, ins, outs, sems, cond, action):
    if not comm:
        return

    @pl.when(cond)
    def _():
        at = 0
        for e, (kind, arrays) in enumerate(comm):
            n = len(arrays)
            action(_transfers(kind, ins[at:at + n], outs[at:at + n]), *sems[2 * e:2 * e + 2])
            at += n


def _ffn_fwd(x, vec, w_in4, w_out, comm=None):
    S, D = x.shape
    cs = w_in4.shape[2]
    F = 2 * cs
    tm = _tile(S, TM)
    ni = S // tm
    ride = _ride(comm, 5, 4)
    nc = ride["n"]

    def body(*refs):
        x_ref, vec_ref, wa_ref, wb_ref, wo_ref = refs[:5]
        xo_ref, h_ref, ab_ref, y_ref = refs[5 + nc:9 + nc]
        i, j = pl.program_id(0), pl.program_id(1)
        riding = (comm, refs[5:5 + nc], refs[9 + nc:9 + 2 * nc], refs[9 + 2 * nc:])
        _ride_when(*riding, jnp.logical_and(i == 0, j == 0), _comm_start)

        @pl.when(j == 0)
        def _():
            h_ref[...] = _norm_mod(x_ref[...], vec_ref[...]).astype(BF)
            y_ref[...] = jnp.zeros_like(y_ref)

        h = h_ref[...]
        a = _dot(h, wa_ref[...])
        b = _dot(h, wb_ref[...])
        ab_ref[0] = a.astype(BF)
        ab_ref[1] = b.astype(BF)
        _, s = _silu_parts(a)
        y_ref[...] += _dot((s * b).astype(BF), wo_ref[...])

        @pl.when(j == 1)
        def _():
            xo_ref[...] = x_ref[...] + (0.5 * (1.0 + vec_ref[3:4, :])) * y_ref[...]

        _ride_when(*riding, jnp.logical_and(i == ni - 1, j == 1), _comm_wait)

    tok = pl.BlockSpec((tm, D), lambda i, j: (i, 0))
    return pl.pallas_call(
        body, name="ffn_fwd", grid=(ni, 2),
        in_specs=[tok, pl.BlockSpec((8, D), lambda i, j: (0, 0)),
                  pl.BlockSpec((None, D, cs), lambda i, j: (j, 0, 0)),
                  pl.BlockSpec((None, D, cs), lambda i, j: (j + 2, 0, 0)),
                  pl.BlockSpec((cs, D), lambda i, j: (j, 0))] + ride["in_specs"],
        out_specs=[tok, tok, pl.BlockSpec((2, tm, cs), lambda i, j: (0, i, j)), tok] + ride["out_specs"],
        out_shape=[jax.ShapeDtypeStruct((S, D), F32), jax.ShapeDtypeStruct((S, D), BF),
                   jax.ShapeDtypeStruct((2, S, F), BF), jax.ShapeDtypeStruct((S, D), F32)] + ride["out_shape"],
        input_output_aliases=ride["aliases"], scratch_shapes=ride["scratch"],
        compiler_params=_cparams(2),
    )(x, vec, w_in4, w_in4, w_out, *ride["arrays"])


def _ffn_dgrad(dxo, vec, ab, w_in4, w_out, comm=None):
    S, D = dxo.shape
    cs = w_in4.shape[2]
    F = 2 * cs
    tm = _tile(S, TM)
    ni = S // tm
    ride = _ride(comm, 6, 4)
    nc = ride["n"]

    def body(*refs):
        dxo_ref, vec_ref, ab_ref, wa_ref, wb_ref, wo_ref = refs[:6]
        dy_ref, u_ref, dab_ref, dh_ref = refs[6 + nc:10 + nc]
        i, j = pl.program_id(0), pl.program_id(1)
        riding = (comm, refs[6:6 + nc], refs[10 + nc:10 + 2 * nc], refs[10 + 2 * nc:])
        _ride_when(*riding, jnp.logical_and(i == 0, j == 0), _comm_start)

        @pl.when(j == 0)
        def _():
            dy_ref[...] = ((0.5 * (1.0 + vec_ref[3:4, :])) * dxo_ref[...]).astype(BF)
            dh_ref[...] = jnp.zeros_like(dh_ref)

        du = _dot_nt(dy_ref[...], wo_ref[...])
        a = ab_ref[0].astype(F32)
        b = ab_ref[1].astype(F32)
        sg, s = _silu_parts(a)
        u_ref[...] = (s * b).astype(BF)
        da = (du * b * (sg * (1.0 + a * (1.0 - sg)))).astype(BF)
        db = (du * s).astype(BF)
        dab_ref[0] = da
        dab_ref[1] = db
        dh_ref[...] += _dot_nt(da, wa_ref[...]) + _dot_nt(db, wb_ref[...])
        _ride_when(*riding, jnp.logical_and(i == ni - 1, j == 1), _comm_wait)

    tok = pl.BlockSpec((tm, D), lambda i, j: (i, 0))
    return pl.pallas_call(
        body, name="ffn_dgrad", grid=(ni, 2),
        in_specs=[tok, pl.BlockSpec((8, D), lambda i, j: (0, 0)),
                  pl.BlockSpec((2, tm, cs), lambda i, j: (0, i, j)),
                  pl.BlockSpec((None, D, cs), lambda i, j: (j, 0, 0)),
                  pl.BlockSpec((None, D, cs), lambda i, j: (j + 2, 0, 0)),
                  pl.BlockSpec((cs, D), lambda i, j: (j, 0))] + ride["in_specs"],
        out_specs=[tok, pl.BlockSpec((tm, cs), lambda i, j: (i, j)),
                   pl.BlockSpec((2, tm, cs), lambda i, j: (0, i, j)), tok] + ride["out_specs"],
        out_shape=[jax.ShapeDtypeStruct((S, D), BF), jax.ShapeDtypeStruct((S, F), BF),
                   jax.ShapeDtypeStruct((2, S, F), BF), jax.ShapeDtypeStruct((S, D), F32)] + ride["out_shape"],
        input_output_aliases=ride["aliases"], scratch_shapes=ride["scratch"],
        compiler_params=_cparams(2),
    )(dxo, vec, ab, w_in4, w_in4, w_out, *ride["arrays"])


def _mm_tn(a, b3, cs):
    K, M = a.shape
    G, _, Nb = b3.shape
    N = G * Nb
    tn = math.gcd(cs, Nb)
    tmo = M if M <= 1024 else M // 2
    tk = _tile(K, TK)

    def body(a_ref, b_ref, o_ref):
        @pl.when(pl.program_id(2) == 0)
        def _():
            o_ref[...] = jnp.zeros_like(o_ref)

        o_ref[...] += _dot_tn(a_ref[...].astype(BF), b_ref[...].astype(BF))

    return pl.pallas_call(
        body, name="wgrad_tn", grid=(M // tmo, N // tn, K // tk),
        in_specs=[pl.BlockSpec((tk, tmo), lambda m, n, k: (k, m)),
                  pl.BlockSpec((None, tk, tn), lambda m, n, k: ((n * tn) // Nb, k, ((n * tn) % Nb) // tn))],
        out_specs=pl.BlockSpec((None, tmo, tn), lambda m, n, k: ((n * tn) // cs, m, ((n * tn) % cs) // tn)),
        out_shape=jax.ShapeDtypeStruct((N // cs, M, cs), F32),
        compiler_params=_cparams(3),
    )(a, b3)


def _norm_bwd(dxo, dhs, x, y, vec, coef):
    S, D = x.shape
    tm = _tile(S, TM)
    nh = len(dhs)
    has_y = y is not None

    def body(*refs):
        dxo_ref = refs[0]
        dh_refs = refs[1:1 + nh]
        x_ref = refs[1 + nh]
        y_ref = refs[2 + nh] if has_y else None
        vec_ref, dx_ref, red_ref = refs[-3:]

        @pl.when(pl.program_id(0) == 0)
        def _():
            red_ref[...] = jnp.zeros_like(red_ref)

        xv = x_ref[...]
        dxo = dxo_ref[...]
        dh = dh_refs[0][...]
        for r in dh_refs[1:]:
            dh = dh + r[...]
        g = vec_ref[0:1, :]
        sc = vec_ref[1:2, :]
        r = _rstd(xv)
        xh = xv * r
        dhn = dh * (1.0 + sc)
        dxh = dhn * g
        dx_ref[...] = dxo + r * (dxh - xh * jnp.mean(dxh * xh, axis=-1, keepdims=True))
        red_ref[0:1, :] += jnp.sum(dhn * xh, axis=0, keepdims=True)
        red_ref[1:2, :] += jnp.sum(dh * (xh * g), axis=0, keepdims=True)
        red_ref[2:3, :] += jnp.sum(dh, axis=0, keepdims=True)
        if has_y:
            red_ref[3:4, :] += coef * jnp.sum(dxo * y_ref[...], axis=0, keepdims=True)

    tok = pl.BlockSpec((tm, D), lambda i: (i, 0))
    small = pl.BlockSpec((8, D), lambda i: (0, 0))
    ops = [dxo, *dhs, x] + ([y] if has_y else []) + [vec]
    return pl.pallas_call(
        body, name="norm_bwd", grid=(S // tm,),
        in_specs=[tok] * (len(ops) - 1) + [small],
        out_specs=[tok, small],
        out_shape=[jax.ShapeDtypeStruct((S, D), F32), jax.ShapeDtypeStruct((8, D), F32)],
        compiler_params=_cparams(1),
    )(*ops)


def _conv_fwd(x, vec, cw, w_in4, w_out):
    S, D = x.shape
    cs = w_in4.shape[2]
    tm = _tile(S, TM)

    def body(x_ref, vec_ref, cw_ref, wi_ref, wo_ref, xo_ref, h_ref, s4_ref, z_ref, y_ref, vs_ref):
        @pl.when(pl.program_id(0) == 0)
        def _():
            vs_ref[0:8, :] = jnp.zeros((8, D), F32)

        xv = x_ref[...]
        h = _norm_mod(xv, vec_ref[...]).astype(BF)
        h_ref[...] = h
        bcu = jnp.concatenate([_dot(h, wi_ref[q]) for q in range(4)], axis=1)
        bg, cg, u = bcu[:, :D], bcu[:, D:2 * D], bcu[:, 2 * D:]
        v = cg * u
        vs_ref[8:8 + tm, :] = v
        conv = cw_ref[0:1, :] * vs_ref[pl.ds(6, tm), :] + cw_ref[1:2, :] * vs_ref[pl.ds(7, tm), :] + cw_ref[2:3, :] * v
        vs_ref[0:8, :] = vs_ref[tm:tm + 8, :]
        z = (bg * conv).astype(BF)
        s4_ref[0] = bg.astype(BF)
        s4_ref[1] = cg.astype(BF)
        s4_ref[2] = u.astype(BF)
        s4_ref[3] = conv.astype(BF)
        z_ref[...] = z
        y = _dot(z, wo_ref[...])
        y_ref[...] = y
        xo_ref[...] = xv + (1.0 + vec_ref[3:4, :]) * y

    tok = pl.BlockSpec((tm, D), lambda i: (i, 0))
    small = pl.BlockSpec((8, D), lambda i: (0, 0))
    return pl.pallas_call(
        body, name="conv_fwd", grid=(S // tm,),
        in_specs=[tok, small, small, pl.BlockSpec((4, D, cs), lambda i: (0, 0, 0)),
                  pl.BlockSpec((D, D), lambda i: (0, 0))],
        out_specs=[tok, tok, pl.BlockSpec((4, tm, D), lambda i: (0, i, 0)), tok, tok],
        out_shape=[jax.ShapeDtypeStruct((S, D), F32), jax.ShapeDtypeStruct((S, D), BF),
                   jax.ShapeDtypeStruct((4, S, D), BF), jax.ShapeDtypeStruct((S, D), BF),
                   jax.ShapeDtypeStruct((S, D), F32)],
        scratch_shapes=[pltpu.VMEM((tm + 8, D), F32)],
        compiler_params=_cparams(1),
    )(x, vec, cw, w_in4, w_out)


def _conv_dgrad(dxo, vec, cw, s4, w_in4, w_out):
    S, D = dxo.shape
    cs = w_in4.shape[2]
    tm = _tile(S, TM)
    nt = S // tm

    def body(dxo_ref, vec_ref, cw_ref, s4_ref, wi_ref, wo_ref, dy_ref, d3_ref, dh_ref, dcw_ref, ds_ref):
        @pl.when(pl.program_id(0) == 0)
        def _():
            ds_ref[tm:tm + 8, :] = jnp.zeros((8, D), F32)
            dcw_ref[...] = jnp.zeros_like(dcw_ref)

        dy = ((1.0 + vec_ref[3:4, :]) * dxo_ref[...]).astype(BF)
        dy_ref[...] = dy
        dz = _dot_nt(dy, wo_ref[...])
        bg = s4_ref[0].astype(F32)
        cg = s4_ref[1].astype(F32)
        u = s4_ref[2].astype(F32)
        conv = s4_ref[3].astype(F32)
        dbg = dz * conv
        dconv = dz * bg
        ds_ref[0:tm, :] = dconv
        d1 = ds_ref[pl.ds(1, tm), :]
        d2 = ds_ref[pl.ds(2, tm), :]
        ds_ref[tm:tm + 8, :] = ds_ref[0:8, :]
        dv = cw_ref[2:3, :] * dconv + cw_ref[1:2, :] * d1 + cw_ref[0:1, :] * d2
        v = cg * u
        dcw_ref[0:1, :] += jnp.sum(d2 * v, axis=0, keepdims=True)
        dcw_ref[1:2, :] += jnp.sum(d1 * v, axis=0, keepdims=True)
        dcw_ref[2:3, :] += jnp.sum(dconv * v, axis=0, keepdims=True)
        dbcu = jnp.concatenate([dbg, dv * u, dv * cg], axis=1).astype(BF)
        d3_ref[...] = dbcu
        dh = _dot_nt(dbcu[:, 0:cs], wi_ref[0])
        for q in range(1, 4):
            dh = dh + _dot_nt(dbcu[:, q * cs:(q + 1) * cs], wi_ref[q])
        dh_ref[...] = dh

    tok = pl.BlockSpec((tm, D), lambda i: (nt - 1 - i, 0))
    small = pl.BlockSpec((8, D), lambda i: (0, 0))
    return pl.pallas_call(
        body, name="conv_dgrad", grid=(nt,),
        in_specs=[tok, small, small, pl.BlockSpec((4, tm, D), lambda i: (0, nt - 1 - i, 0)),
                  pl.BlockSpec((4, D, cs), lambda i: (0, 0, 0)), pl.BlockSpec((D, D), lambda i: (0, 0))],
        out_specs=[tok, pl.BlockSpec((tm, 3 * D), lambda i: (nt - 1 - i, 0)), tok, small],
        out_shape=[jax.ShapeDtypeStruct((S, D), BF), jax.ShapeDtypeStruct((S, 3 * D), BF),
                   jax.ShapeDtypeStruct((S, D), F32), jax.ShapeDtypeStruct((8, D), F32)],
        scratch_shapes=[pltpu.VMEM((tm + 8, D), F32)],
        compiler_params=_cparams(1),
    )(dxo, vec, cw, s4, w_in4, w_out)


def _rope_tables(positions):
    S = positions.shape[-1]
    inv = ROPE_THETA ** (-jnp.arange(0, ROPE_DIM, 2, dtype=F32) / ROPE_DIM)
    ang = positions.reshape(S, 1).astype(F32) * inv
    cos = jnp.tile(jnp.cos(ang), (1, LANES // 8))
    sin = jnp.tile(jnp.sin(ang), (1, LANES // 8))
    l64 = jnp.arange(LANES) % HEAD_DIM
    return jnp.stack([jnp.where(l64 < ROPE_DIM, cos, 1.0),
                      jnp.where(l64 < ROPE_DIM // 2, -sin, 0.0),
                      jnp.where((l64 >= ROPE_DIM // 2) & (l64 < ROPE_DIM), sin, 0.0)])


def _rope(t, tab_ref):
    return t * tab_ref[0] + pltpu.roll(t, LANES - 8, 1) * tab_ref[1] + pltpu.roll(t, 8, 1) * tab_ref[2]


def _rope_t(d, tab_ref):
    return d * tab_ref[0] + pltpu.roll(d * tab_ref[1], 8, 1) + pltpu.roll(d * tab_ref[2], LANES - 8, 1)


GROUP_CH = GROUP_W // LANES


def _to_dilated(src_ref, c0, dst_ref, d):
    n = src_ref.shape[1]
    for r in range(d):
        for ch in range(GROUP_CH):
            rows = src_ref[c0 + ch] if d == 1 else src_ref.at[c0 + ch][pl.ds(r, n // d, stride=d), :]
            dst_ref[:, r * GROUP_W + ch * LANES:r * GROUP_W + (ch + 1) * LANES] = rows.astype(dst_ref.dtype)


def _from_dilated(val, dst_ref, c0, d):
    n = dst_ref.shape[1]
    for r in range(d):
        for ch in range(GROUP_CH):
            cols = val[:, r * GROUP_W + ch * LANES:r * GROUP_W + (ch + 1) * LANES]
            if d == 1:
                dst_ref[c0 + ch] = cols
            else:
                dst_ref.at[c0 + ch][pl.ds(r, n // d, stride=d), :] = cols


def _group(ref, c0):
    return jnp.concatenate([ref[c0 + ch] for ch in range(GROUP_CH)], axis=1)


def _put_group(ref, c0, val):
    for ch in range(GROUP_CH):
        ref[c0 + ch] = val[:, ch * LANES:(ch + 1) * LANES]


def _dil_spec(tm, d):
    return pl.BlockSpec((tm // d, d * GROUP_W), lambda i: (i, 0))


def _proj_fwd(x, vec, w4, tabs, n_rope, dils, scale=1.0):
    S, D = x.shape
    cs = w4.shape[2]
    N = 4 * cs
    tm = _tile(S, TM)

    def body(x_ref, vec_ref, w_ref, tab_ref, h_ref, *rest):
        outs, acc_ref = rest[:-1], rest[-1]
        h = _norm_mod(x_ref[...], vec_ref[...]).astype(BF)
        h_ref[...] = h
        per = cs // LANES
        for q in range(4):
            acc = _dot(h, w_ref[q])
            if scale != 1.0:
                acc = acc * scale
            cols = [acc[:, ch * LANES:(ch + 1) * LANES] for ch in range(per)]
            for ch, t in enumerate(cols):
                acc_ref[q * per + ch] = _rope(t, tab_ref) if q * per + ch < n_rope else t
        for j, d in enumerate(dils):
            _to_dilated(acc_ref, j * GROUP_CH, outs[j], d)

    h, *outs = pl.pallas_call(
        body, name="proj_fwd", grid=(S // tm,),
        in_specs=[pl.BlockSpec((tm, D), lambda i: (i, 0)), pl.BlockSpec((8, D), lambda i: (0, 0)),
                  pl.BlockSpec((4, D, cs), lambda i: (0, 0, 0)), pl.BlockSpec((3, tm, LANES), lambda i: (0, i, 0))],
        out_specs=[pl.BlockSpec((tm, D), lambda i: (i, 0))] + [_dil_spec(tm, d) for d in dils],
        out_shape=[jax.ShapeDtypeStruct((S, D), BF)] + [jax.ShapeDtypeStruct((S // d, d * GROUP_W), BF) for d in dils],
        scratch_shapes=[pltpu.VMEM((N // LANES, tm, LANES), F32)],
        compiler_params=_cparams(1),
    )(x, vec, w4, tabs)
    return h, outs


def _proj_dgrad(parts, tabs, w4, n_rope_groups, dils):
    cs, D = w4.shape[2], w4.shape[1]
    N = 4 * cs
    S = parts[0][0].shape[0] * dils[0]
    tm = _tile(S, TM)
    flat = [p for grp in parts for p in grp]
    counts = [len(grp) for grp in parts]

    def body(*refs):
        prefs = refs[:len(flat)]
        tab_ref, w_ref, dz_ref, dh_ref, z_ref = refs[len(flat):]
        k = 0
        for j, cnt in enumerate(counts):
            z = prefs[k][...]
            for r in prefs[k + 1:k + cnt]:
                z = z + r[...]
            k += cnt
            _from_dilated(z, z_ref, 0, dils[j])
            if j < n_rope_groups:
                z = jnp.concatenate([_rope_t(z_ref[ch], tab_ref) for ch in range(GROUP_CH)], axis=1)
            else:
                z = _group(z_ref, 0)
            dz_ref[:, j * GROUP_W:(j + 1) * GROUP_W] = z.astype(BF)
        dh = _dot_nt(dz_ref[:, 0:cs], w_ref[0])
        for q in range(1, 4):
            dh = dh + _dot_nt(dz_ref[:, q * cs:(q + 1) * cs], w_ref[q])
        dh_ref[...] = dh

    return pl.pallas_call(
        body, name="proj_dgrad", grid=(S // tm,),
        in_specs=[_dil_spec(tm, d) for d, cnt in zip(dils, counts) for _ in range(cnt)]
        + [pl.BlockSpec((3, tm, LANES), lambda i: (0, i, 0)), pl.BlockSpec((4, D, cs), lambda i: (0, 0, 0))],
        out_specs=[pl.BlockSpec((tm, N), lambda i: (i, 0)), pl.BlockSpec((tm, D), lambda i: (i, 0))],
        out_shape=[jax.ShapeDtypeStruct((S, N), BF), jax.ShapeDtypeStruct((S, D), F32)],
        scratch_shapes=[pltpu.VMEM((GROUP_CH, tm, LANES), F32)],
        compiler_params=_cparams(1),
    )(*flat, tabs, w4)


def _band_masks():
    qi = lax.broadcasted_iota(jnp.int32, (BAND, BAND), 0)
    kj = lax.broadcasted_iota(jnp.int32, (BAND, BAND), 1)
    return kj <= qi, kj >= qi


def _head(ref, h):
    return ref[:, h * HEAD_DIM:(h + 1) * HEAD_DIM]


def _attn_fwd(q, k, v, d):
    sd = q.shape[0]
    nb = sd // BAND

    def body(q_ref, kp_ref, kc_ref, vp_ref, vc_ref, o_ref, l_ref):
        same, prev = _band_masks()
        prev = jnp.logical_and(prev, pl.program_id(1) > 0)
        for h in range(HEADS_PER_GROUP):
            qh = _head(q_ref, h)
            sc = jnp.where(same, _dot_nt(qh, _head(kc_ref, h)), NEG)
            sp = jnp.where(prev, _dot_nt(qh, _head(kp_ref, h)), NEG)
            m = jnp.maximum(jnp.max(sc, axis=-1, keepdims=True), jnp.max(sp, axis=-1, keepdims=True))
            pc = jnp.exp(sc - m)
            pp = jnp.exp(sp - m)
            den = jnp.sum(pc, axis=-1, keepdims=True) + jnp.sum(pp, axis=-1, keepdims=True)
            o = (_dot(pc.astype(BF), _head(vc_ref, h)) + _dot(pp.astype(BF), _head(vp_ref, h))) / den
            o_ref[:, h * HEAD_DIM:(h + 1) * HEAD_DIM] = o
            l_ref[:, h * HEAD_DIM:(h + 1) * HEAD_DIM] = jnp.broadcast_to(m + jnp.log(den), (BAND, HEAD_DIM))

    blk = (BAND, GROUP_W)
    cur = pl.BlockSpec(blk, lambda r, i: (i, r))
    prv = pl.BlockSpec(blk, lambda r, i: (jnp.maximum(i - 1, 0), r))
    return pl.pallas_call(
        body, name=f"attn_fwd_d{d}", grid=(d, nb),
        in_specs=[cur, prv, cur, prv, cur], out_specs=[cur, cur],
        out_shape=[jax.ShapeDtypeStruct((sd, d * GROUP_W), F32)] * 2,
        compiler_params=_cparams(2),
    )(q, k, k, v, v)


def _attn_out(os, ls, x, vec, wo4):
    S, D = x.shape
    cs = wo4.shape[2]
    tm = _tile(S, TM)
    ng = len(DILATIONS)

    def body(*refs):
        o_refs, l_refs = refs[0:2 * ng:2], refs[1:2 * ng:2]
        x_ref, vec_ref, wo_ref = refs[2 * ng:2 * ng + 3]
        outs = refs[2 * ng + 3:]
        mix_refs, lj_refs = outs[0:2 * ng:2], outs[1:2 * ng:2]
        y_ref, xo_ref, nat_ref = outs[2 * ng:]
        ov, lv = [], []
        for g in range(ng):
            _from_dilated(o_refs[g][...], nat_ref, 2 * g * GROUP_CH, DILATIONS[g])
            _from_dilated(l_refs[g][...], nat_ref, (2 * g + 1) * GROUP_CH, DILATIONS[g])
            ov.append(_group(nat_ref, 2 * g * GROUP_CH))
            lv.append(_group(nat_ref, (2 * g + 1) * GROUP_CH))
        mx = jnp.maximum(jnp.maximum(lv[0], lv[1]), lv[2])
        es = [jnp.exp(t - mx) for t in lv]
        den = es[0] + es[1] + es[2]
        mix = (es[0] * ov[0] + es[1] * ov[1] + es[2] * ov[2]) / den
        at_mix, at_lj = 2 * ng * GROUP_CH, (2 * ng + 1) * GROUP_CH
        _put_group(nat_ref, at_mix, mix)
        _put_group(nat_ref, at_lj, mx + jnp.log(den))
        for g in range(ng):
            _to_dilated(nat_ref, at_mix, mix_refs[g], DILATIONS[g])
            _to_dilated(nat_ref, at_lj, lj_refs[g], DILATIONS[g])
        mb = mix.astype(BF)
        y = jnp.concatenate([_dot(mb, wo_ref[q]) for q in range(4)], axis=1)
        y_ref[...] = y
        xo_ref[...] = x_ref[...] + (1.0 + vec_ref[3:4, :]) * y

    tok = pl.BlockSpec((tm, D), lambda i: (i, 0))
    dil = [_dil_spec(tm, d) for d in DILATIONS for _ in range(2)]
    dil_shape = [jax.ShapeDtypeStruct((S // d, d * GROUP_W), F32) for d in DILATIONS for _ in range(2)]
    outs = pl.pallas_call(
        body, name="attn_out", grid=(S // tm,),
        in_specs=dil + [tok, pl.BlockSpec((8, D), lambda i: (0, 0)), pl.BlockSpec((4, GROUP_W, cs), lambda i: (0, 0, 0))],
        out_specs=dil + [tok, tok],
        out_shape=dil_shape + [jax.ShapeDtypeStruct((S, D), F32)] * 2,
        scratch_shapes=[pltpu.VMEM(((2 * ng + 2) * GROUP_CH, tm, LANES), F32)],
        compiler_params=_cparams(1),
    )(*[t for pair in zip(os, ls) for t in pair], x, vec, wo4)
    return list(outs[0:2 * ng:2]), list(outs[1:2 * ng:2]), outs[2 * ng], outs[2 * ng + 1]


def _attn_out_dgrad(dxo, vec, wo4):
    S, D = dxo.shape
    cs = wo4.shape[2]
    tm = _tile(S, TM)

    def body(dxo_ref, vec_ref, wo_ref, dy_ref, *rest):
        dm_refs, nat_ref = rest[:-1], rest[-1]
        dy = ((1.0 + vec_ref[3:4, :]) * dxo_ref[...]).astype(BF)
        dy_ref[...] = dy
        dm = _dot_nt(dy[:, 0:cs], wo_ref[0])
        for q in range(1, 4):
            dm = dm + _dot_nt(dy[:, q * cs:(q + 1) * cs], wo_ref[q])
        _put_group(nat_ref, 0, dm)
        for g, d in enumerate(DILATIONS):
            _to_dilated(nat_ref, 0, dm_refs[g], d)

    tok = pl.BlockSpec((tm, D), lambda i: (i, 0))
    dy, *dms = pl.pallas_call(
        body, name="attn_out_dgrad", grid=(S // tm,),
        in_specs=[tok, pl.BlockSpec((8, D), lambda i: (0, 0)), pl.BlockSpec((4, GROUP_W, cs), lambda i: (0, 0, 0))],
        out_specs=[tok] + [_dil_spec(tm, d) for d in DILATIONS],
        out_shape=[jax.ShapeDtypeStruct((S, D), BF)] + [jax.ShapeDtypeStruct((S // d, d * GROUP_W), F32) for d in DILATIONS],
        scratch_shapes=[pltpu.VMEM((GROUP_CH, tm, LANES), F32)],
        compiler_params=_cparams(1),
    )(dxo, vec, wo4)
    return dy, dms


def _prob_and_ds(qh, kh, vh, doh, oh, lrow, mask):
    s = jnp.where(mask, _dot_nt(qh, kh), NEG)
    p = jnp.exp(s - lrow)
    delta = jnp.sum(doh * oh, axis=-1, keepdims=True)
    dp = _dot_nt(doh.astype(BF), vh)
    return p, p * (dp - delta)


def _attn_dq(q, k, v, dmix, mix, lj, d):
    sd = q.shape[0]
    nb = sd // BAND

    def body(q_ref, kp_ref, kc_ref, vp_ref, vc_ref, do_ref, o_ref, l_ref, dq_ref):
        same, prev = _band_masks()
        prev = jnp.logical_and(prev, pl.program_id(1) > 0)
        for h in range(HEADS_PER_GROUP):
            qh, doh, oh = _head(q_ref, h), _head(do_ref, h), _head(o_ref, h)
            lrow = l_ref[:, h * HEAD_DIM:h * HEAD_DIM + 1]
            kc, kp = _head(kc_ref, h), _head(kp_ref, h)
            _, dsc = _prob_and_ds(qh, kc, _head(vc_ref, h), doh, oh, lrow, same)
            _, dsp = _prob_and_ds(qh, kp, _head(vp_ref, h), doh, oh, lrow, prev)
            dq_ref[:, h * HEAD_DIM:(h + 1) * HEAD_DIM] = (_dot(dsc.astype(BF), kc) + _dot(dsp.astype(BF), kp)) * SM_SCALE

    blk = (BAND, GROUP_W)
    cur = pl.BlockSpec(blk, lambda r, i: (i, r))
    prv = pl.BlockSpec(blk, lambda r, i: (jnp.maximum(i - 1, 0), r))
    return pl.pallas_call(
        body, name=f"attn_dq_d{d}", grid=(d, nb),
        in_specs=[cur, prv, cur, prv, cur, cur, cur, cur], out_specs=cur,
        out_shape=jax.ShapeDtypeStruct((sd, d * GROUP_W), F32),
        compiler_params=_cparams(2),
    )(q, k, k, v, v, dmix, mix, lj)


def _attn_dkv(q, k, v, dmix, mix, lj, d):
    sd = q.shape[0]
    nb = sd // BAND

    def body(k_ref, v_ref, qs_ref, qn_ref, dos_ref, don_ref, os_ref, on_ref, ls_ref, ln_ref, dk_ref, dv_ref):
        same, nxt = _band_masks()
        nxt = jnp.logical_and(nxt, pl.program_id(1) < nb - 1)
        for h in range(HEADS_PER_GROUP):
            kh, vh = _head(k_ref, h), _head(v_ref, h)
            dk = jnp.zeros((BAND, HEAD_DIM), F32)
            dv = jnp.zeros((BAND, HEAD_DIM), F32)
            for q_ref, do_ref, o_ref, l_ref, mask in ((qs_ref, dos_ref, os_ref, ls_ref, same),
                                                       (qn_ref, don_ref, on_ref, ln_ref, nxt)):
                qh, doh = _head(q_ref, h), _head(do_ref, h)
                p, ds = _prob_and_ds(qh, kh, vh, doh, _head(o_ref, h), l_ref[:, h * HEAD_DIM:h * HEAD_DIM + 1], mask)
                dv = dv + _dot_tn(p.astype(BF), doh.astype(BF))
                dk = dk + _dot_tn(ds.astype(BF), qh)
            dk_ref[:, h * HEAD_DIM:(h + 1) * HEAD_DIM] = dk
            dv_ref[:, h * HEAD_DIM:(h + 1) * HEAD_DIM] = dv

    blk = (BAND, GROUP_W)
    cur = pl.BlockSpec(blk, lambda r, i: (i, r))
    nxt_spec = pl.BlockSpec(blk, lambda r, i: (jnp.minimum(i + 1, nb - 1), r))
    return pl.pallas_call(
        body, name=f"attn_dkv_d{d}", grid=(d, nb),
        in_specs=[cur, cur, cur, nxt_spec, cur, nxt_spec, cur, nxt_spec, cur, nxt_spec],
        out_specs=[cur, cur],
        out_shape=[jax.ShapeDtypeStruct((sd, d * GROUP_W), F32)] * 2,
        compiler_params=_cparams(2),
    )(k, v, q, q, dmix, dmix, mix, mix, lj, lj)


def _final_loss(x, gvec, tgt):
    S, D = x.shape
    tm = _tile(S, TM)

    def body(x_ref, g_ref, t_ref, dx_ref, red_ref):
        @pl.when(pl.program_id(0) == 0)
        def _():
            red_ref[...] = jnp.zeros_like(red_ref)

        xv = x_ref[...]
        g = g_ref[0:1, :]
        r = _rstd(xv)
        xh = xv * r
        err = xh * g - t_ref[...]
        dy = err * (1.0 / D)
        dxh = dy * g
        dx_ref[...] = r * (dxh - xh * jnp.mean(dxh * xh, axis=-1, keepdims=True))
        red_ref[0:1, :] += jnp.sum(dy * xh, axis=0, keepdims=True)
        red_ref[1:2, :] += jnp.sum(err * err, axis=0, keepdims=True)

    tok = pl.BlockSpec((tm, D), lambda i: (i, 0))
    small = pl.BlockSpec((8, D), lambda i: (0, 0))
    return pl.pallas_call(
        body, name="final_loss", grid=(S // tm,),
        in_specs=[tok, small, tok], out_specs=[tok, small],
        out_shape=[jax.ShapeDtypeStruct((S, D), F32), jax.ShapeDtypeStruct((8, D), F32)],
        compiler_params=_cparams(1),
    )(x, gvec, tgt)


def _adamw(w, g, m, v):
    shape = w.shape
    C = shape[-1]
    R = w.size // C
    tr = 256 if R % 256 == 0 else R

    def body(w_ref, g_ref, m_ref, v_ref, d_ref, nm_ref, nv_ref):
        gv = g_ref[...]
        nm = ADAM_B1 * m_ref[...] + (1.0 - ADAM_B1) * gv
        nv = ADAM_B2 * v_ref[...] + (1.0 - ADAM_B2) * (gv * gv)
        m_hat = nm / (1.0 - ADAM_B1 ** ADAM_STEP)
        v_hat = nv / (1.0 - ADAM_B2 ** ADAM_STEP)
        d_ref[...] = -ADAM_LR * (m_hat / (jnp.sqrt(v_hat) + ADAM_EPS) + ADAM_WD * w_ref[...])
        nm_ref[...] = nm
        nv_ref[...] = nv

    spec = pl.BlockSpec((tr, C), lambda i: (i, 0))
    outs = pl.pallas_call(
        body, name="adamw", grid=(R // tr,),
        in_specs=[spec] * 4, out_specs=[spec] * 3,
        out_shape=[jax.ShapeDtypeStruct((R, C), F32)] * 3,
        compiler_params=_cparams(1),
    )(*(t.reshape(R, C) for t in (w, g, m, v)))
    return tuple(o.reshape(shape) for o in outs)


def _ada_fwd(c_all, w, b):
    L, D, N = w.shape
    tn = 768 if N % 768 == 0 else _tile(N, 512)

    def body(c_ref, w_ref, b_ref, o_ref):
        cv = c_ref[...]
        cond = (cv * jax.nn.sigmoid(cv)).astype(BF)
        o_ref[...] = _dot(cond, w_ref[...].astype(BF)) + b_ref[...]

    return pl.pallas_call(
        body, name="ada_fwd", grid=(L, N // tn),
        in_specs=[pl.BlockSpec((8, D), lambda l, n: (0, 0)), pl.BlockSpec((None, D, tn), lambda l, n: (l, 0, n)),
                  pl.BlockSpec((None, 1, tn), lambda l, n: (l, 0, n))],
        out_specs=pl.BlockSpec((None, 8, tn), lambda l, n: (l, 0, n)),
        out_shape=jax.ShapeDtypeStruct((L, 8, N), F32),
        compiler_params=_cparams(2),
    )(c_all, w, b)


def _ada_wgrad(c_all_t, dm):
    D = c_all_t.shape[0]
    L, _, N = dm.shape
    tn = 256

    def body(c_ref, dm_ref, o_ref):
        cv = c_ref[...]
        cond = cv * jax.nn.sigmoid(cv)
        acc = cond[:, 0:1] * dm_ref[0:1, :]
        for b in range(1, 8):
            acc = acc + cond[:, b:b + 1] * dm_ref[b:b + 1, :]
        o_ref[...] = acc

    return pl.pallas_call(
        body, name="ada_wgrad", grid=(L, N // tn),
        in_specs=[pl.BlockSpec((D, 8), lambda l, n: (0, 0)), pl.BlockSpec((None, 8, tn), lambda l, n: (l, 0, n))],
        out_specs=pl.BlockSpec((None, D, tn), lambda l, n: (l, 0, n)),
        out_shape=jax.ShapeDtypeStruct((L, D, N), F32),
        compiler_params=_cparams(2),
    )(c_all_t, dm)


def _sum8(g):
    _, R, C = g.shape

    def body(g_ref, o_ref):
        acc = g_ref[0]
        for b in range(1, 8):
            acc = acc + g_ref[b]
        o_ref[...] = acc

    return pl.pallas_call(body, name="sum8", out_shape=jax.ShapeDtypeStruct((R, C), F32),
                          in_specs=[pl.BlockSpec(memory_space=pltpu.VMEM)],
                          out_specs=pl.BlockSpec(memory_space=pltpu.VMEM))(g)


def _pair_add(g4, recv, cidx):
    _, _, ha, B = g4.shape
    tr = _row_tile(ha)

    def body(c_ref, g_ref, r_ref, o_ref):
        o_ref[...] = (g_ref[...] + r_ref[...]).astype(BF)

    return pl.pallas_call(
        body, name="pair_add",
        grid_spec=pltpu.PrefetchScalarGridSpec(
            num_scalar_prefetch=1, grid=(4, ha // tr),
            in_specs=[pl.BlockSpec((None, None, tr, B), lambda s, i, c: (s, c[0], i, 0)),
                      pl.BlockSpec((None, tr, B), lambda s, i, c: (s, i, 0))],
            out_specs=pl.BlockSpec((None, tr, B), lambda s, i, c: (s, i, 0))),
        out_shape=jax.ShapeDtypeStruct((4, ha, B), BF),
        compiler_params=_cparams(2),
    )(cidx, g4, recv)


def _sum_partials(part, recv, gbuf, n_layers, l, place):
    _, ha, B = part.shape
    tr = _row_tile(ha)

    def body(pc_ref, p_ref, r_ref, *rest):
        o_ref = rest[-1]
        o_ref[...] = ((p_ref[...].astype(F32) + r_ref[0].astype(F32)) + r_ref[1].astype(F32)) + r_ref[2].astype(F32)

    in_specs = [pl.BlockSpec((None, tr, B), lambda i, pc: (pc[0], i, 0)), pl.BlockSpec((3, tr, B), lambda i, pc: (0, i, 0))]
    ops = [part, recv]
    if gbuf is not None:
        in_specs.append(ANY)
        ops.append(gbuf)
    return pl.pallas_call(
        body, name="sum_partials",
        grid_spec=pltpu.PrefetchScalarGridSpec(
            num_scalar_prefetch=1, grid=(ha // tr,), in_specs=in_specs,
            out_specs=pl.BlockSpec((None, None, tr, B), lambda i, pc: (l, pc[1], i, 0))),
        out_shape=jax.ShapeDtypeStruct((n_layers, 2, ha, B), F32),
        input_output_aliases={} if gbuf is None else {3: 0},
        compiler_params=_cparams(1),
    )(place, *ops)


def _cast_shard(shard, l, place):
    _, _, ha, B = shard.shape
    tr = _row_tile(ha)

    def body(pc_ref, s_ref, o_ref):
        o_ref[...] = s_ref[...].astype(BF)

    return pl.pallas_call(
        body, name="cast_shard",
        grid_spec=pltpu.PrefetchScalarGridSpec(
            num_scalar_prefetch=1, grid=(2, ha // tr),
            in_specs=[pl.BlockSpec((None, None, tr, B), lambda h, i, pc: (l, h, i, 0))],
            out_specs=pl.BlockSpec((None, None, tr, B), lambda h, i, pc: (pc[0], h, i, 0))),
        out_shape=jax.ShapeDtypeStruct((4, 2, ha, B), BF),
        compiler_params=_cparams(2),
    )(place, shard)


def _gather8(v):
    R, C = v.shape

    def body(v_ref, o_ref, ssem, rsem):
        x, y, c, *_ = _place()
        me = 4 * x + 2 * y + c
        o_ref[me] = v_ref[...]
        cps = []
        for k in range(1, 8):
            fx, fy, fc = (k >> 2) & 1, (k >> 1) & 1, k & 1
            peer = (x ^ fx, y ^ fy, c ^ fc)
            cp = pltpu.make_async_remote_copy(src_ref=v_ref, dst_ref=o_ref.at[me], send_sem=ssem.at[k - 1],
                                              recv_sem=rsem.at[k - 1], device_id=peer, device_id_type=MESH)
            cp.start()
            cps.append((cp, 4 * peer[0] + 2 * peer[1] + peer[2]))
        for k, (cp, slot) in enumerate(cps):
            there = o_ref.at[slot]
            pltpu.make_async_remote_copy(src_ref=there, dst_ref=there, send_sem=ssem.at[k], recv_sem=rsem.at[k],
                                         device_id=(x, y, c), device_id_type=MESH).wait_recv()
        for cp, _ in cps:
            cp.wait_send()

    vm = pl.BlockSpec(memory_space=pltpu.VMEM)
    return pl.pallas_call(
        body, name="gather8", in_specs=[vm], out_specs=vm, out_shape=jax.ShapeDtypeStruct((8, R, C), F32),
        scratch_shapes=[pltpu.SemaphoreType.DMA((7,)), pltpu.SemaphoreType.DMA((7,))],
    )(v)


def _mods_to_owner(ms):
    _, R, C = ms.shape

    def body(m_ref, o_ref, ssem, rsem):
        x, y, c, p, chips, qs = _place()
        o_ref[p] = m_ref[2 * p + c]
        cps = []
        for j in range(3):
            cp = pltpu.make_async_remote_copy(src_ref=m_ref.at[2 * qs[j] + c], dst_ref=o_ref.at[p], send_sem=ssem.at[j],
                                              recv_sem=rsem.at[j], device_id=(*chips[j], c), device_id_type=MESH)
            cp.start()
            cps.append(cp)
        for j in range(3):
            there = o_ref.at[qs[j]]
            pltpu.make_async_remote_copy(src_ref=there, dst_ref=there, send_sem=ssem.at[j], recv_sem=rsem.at[j],
                                         device_id=(x, y, c), device_id_type=MESH).wait_recv()
        for cp in cps:
            cp.wait_send()

    vm = pl.BlockSpec(memory_space=pltpu.VMEM)
    return pl.pallas_call(
        body, name="mods_to_owner", in_specs=[vm], out_specs=vm, out_shape=jax.ShapeDtypeStruct((4, R, C), F32),
        scratch_shapes=[pltpu.SemaphoreType.DMA((3,)), pltpu.SemaphoreType.DMA((3,))],
    )(ms)


def _vec(*rows):
    D = rows[0].shape[-1]
    rows = [r.reshape(1, D) for r in rows]
    return jnp.concatenate(rows + [jnp.zeros((8 - len(rows), D), F32)], axis=0)


def _sets(n_layers, n_conv):
    sets = []
    for l in range(n_layers):
        j = l - n_conv
        mixer = [("conv_w_in", l), ("conv_w_out", l)] if l < n_conv else [("attn_w_q", j), ("attn_w_o", j)]
        sets.append([("ffn1_w_in", l), ("ffn1_w_out", l)] + ([("w_kv", 0)] if l == n_conv else []))
        sets.append(mixer + [("ffn2_w_in", l), ("ffn2_w_out", l)])
    return sets


def _as_weight(name, buf):
    _, _, ha, B = buf.shape
    return buf.reshape(8 * ha, B) if name.endswith("w_out") else buf.reshape(4, 2 * ha, B)


def _local_step(x, tgt, tabs, norm_g, conv_w, kv_norm_g, final_norm_g, mods, kvmods, cast, place):
    S, D = x.shape
    n_layers, n_conv = norm_g.shape[0], conv_w.shape[0]
    md = mods.reshape(n_layers, 9, D)

    def vec_of(l, k):
        return _vec(norm_g[l, k], md[l, 3 * k + 1], md[l, 3 * k], md[l, 3 * k + 2])

    W = {}

    def install(items, bufs):
        for (name, idx), b in zip(items, bufs):
            W[name, idx] = _as_weight(name, b)

    Q = _sets(n_layers, n_conv)
    install(Q[0], _comm_call("gather_d2d", _comm_call("gather_ici", [cast[it] for it in Q[0]])))
    arrived = {}

    def ffn_fwd(n, *args):
        ici = [k for k in ([1] if n == 0 else []) + [n + 2] if k < len(Q)]
        d2d = [n + 1] if 2 <= n + 1 < len(Q) else []
        comm = [("gather_ici", [cast[it] for it in Q[k]]) for k in ici] + [("gather_d2d", arrived.pop(k)) for k in d2d]
        xo, h, ab, y, *moved = _ffn_fwd(*args, comm)
        for k in ici:
            arrived[k], moved = moved[:len(Q[k])], moved[len(Q[k]):]
        for k in d2d:
            install(Q[k], moved[:len(Q[k])])
            moved = moved[len(Q[k]):]
        if n == 0:
            install(Q[1], _comm_call("gather_d2d", arrived.pop(1)))
        return xo, h, ab, y

    vec_kv = _vec(kv_norm_g, kvmods[D:], kvmods[:D])
    ng = len(DILATIONS)
    n_rope = ng * GROUP_W // LANES
    saved = {}
    xc = x
    ks = vs = None
    for l in range(n_layers):
        if l == n_conv:
            h_kv, kv = _proj_fwd(xc, vec_kv, W["w_kv", 0], tabs, n_rope, DILATIONS * 2)
            ks, vs = kv[:ng], kv[ng:]
            saved["kv"] = (xc, h_kv)
        v0 = vec_of(l, 0)
        xo, h, ab, y = ffn_fwd(2 * l, xc, v0, W["ffn1_w_in", l], W["ffn1_w_out", l])
        saved[l, 0] = (xc, v0, h, ab, y)
        xc = xo
        v1 = vec_of(l, 1)
        if l < n_conv:
            cw = _vec(conv_w[l, 0], conv_w[l, 1], conv_w[l, 2])
            xo, h, s4, z, y = _conv_fwd(xc, v1, cw, W["conv_w_in", l], W["conv_w_out", l])
            saved[l, 1] = (xc, v1, h, y, cw, s4, z)
        else:
            j = l - n_conv
            h, qs = _proj_fwd(xc, v1, W["attn_w_q", j], tabs, n_rope, DILATIONS, SM_SCALE)
            og = [_attn_fwd(qs[g], ks[g], vs[g], DILATIONS[g]) for g in range(ng)]
            mixs, ljs, y, xo = _attn_out([o for o, _ in og], [s for _, s in og], xc, v1, W["attn_w_o", j])
            saved[l, 1] = (xc, v1, h, y, qs, mixs, ljs)
        xc = xo
        v2 = vec_of(l, 2)
        xo, h, ab, y = ffn_fwd(2 * l + 1, xc, v2, W["ffn2_w_in", l], W["ffn2_w_out", l])
        saved[l, 2] = (xc, v2, h, ab, y)
        xc = xo

    dx, red_f = _final_loss(xc, _vec(final_norm_g), tgt)
    loss_part = 0.5 * jnp.sum(red_f[1]) / D
    G, parts, others = {}, {}, {}
    dmods = [[None] * 9 for _ in range(n_layers)]
    dnorm = [[None] * 3 for _ in range(n_layers)]
    dconvw = [None] * n_conv
    dkv_parts = []

    def halves(items):
        return [G[it].reshape(4, 2, G[it].shape[1] // 2, G[it].shape[2]) for it in items]

    def pair_sums(items, recv):
        for it, g, r in zip(items, halves(items), recv):
            parts[it] = _pair_add(g, r, place[1:])
        return [parts[it] for it in items]

    R = Q[::-1]
    recvd = {}

    def ffn_bwd(dx, l, k, name):
        m = 2 * (n_layers - 1 - l) + (0 if k == 2 else 1)
        ex = [m - 1] if m >= 1 else []
        sc = [m - 2] if m >= 2 else []
        comm = [("exchange", halves(R[e])) for e in ex] + [("scatter", pair_sums(R[e], recvd.pop(e))) for e in sc]
        xin, v, h, ab, y = saved[l, k]
        w_in, w_out = W[name + "_w_in", l], W[name + "_w_out", l]
        dy, u, dab, dh, *moved = _ffn_dgrad(dx, v, ab, w_in, w_out, comm)
        for e in ex:
            recvd[e], moved = moved[:len(R[e])], moved[len(R[e]):]
        for e in sc:
            others.update(zip(R[e], moved[:len(R[e])]))
            moved = moved[len(R[e]):]
        G[name + "_w_in", l] = _mm_tn(h, dab, w_in.shape[2])
        G[name + "_w_out", l] = _mm_tn(u, dy[None], D).reshape(4, -1, D)
        return _norm_bwd(dx, [dh], xin, y, v, 0.5)

    def note(l, k, red):
        dnorm[l][k] = red[0]
        dmods[l][3 * k], dmods[l][3 * k + 1], dmods[l][3 * k + 2] = red[2], red[1], red[3]

    for l in reversed(range(n_layers)):
        dx, red = ffn_bwd(dx, l, 2, "ffn2")
        note(l, 2, red)
        if l < n_conv:
            xin, v, h, y, cw, s4, z = saved[l, 1]
            w_in, w_out = W["conv_w_in", l], W["conv_w_out", l]
            dy, d3, dh, dcw = _conv_dgrad(dx, v, cw, s4, w_in, w_out)
            G["conv_w_in", l] = _mm_tn(h, d3[None], w_in.shape[2])
            G["conv_w_out", l] = _mm_tn(z, dy[None], D).reshape(4, -1, D)
            dconvw[l] = dcw[0:3]
        else:
            j = l - n_conv
            xin, v, h, y, qs, mixs, ljs = saved[l, 1]
            wq, wo = W["attn_w_q", j], W["attn_w_o", j]
            dy, dmixs = _attn_out_dgrad(dx, v, wo)
            G["attn_w_o", j] = _mm_tn(mixs[0], dy[None], wo.shape[2])
            per_group = [(qs[g], ks[g], vs[g], dmixs[g], mixs[g], ljs[g], DILATIONS[g]) for g in range(ng)]
            dqs = [_attn_dq(*a) for a in per_group]
            dkv_parts.append([_attn_dkv(*a) for a in per_group])
            dq, dh = _proj_dgrad([[t] for t in dqs], tabs, wq, ng, DILATIONS)
            G["attn_w_q", j] = _mm_tn(h, dq[None], wq.shape[2])
        dx, red = _norm_bwd(dx, [dh], xin, y, v, 1.0)
        note(l, 1, red)
        dx, red = ffn_bwd(dx, l, 0, "ffn1")
        note(l, 0, red)
        if l == n_conv:
            xin, h_kv = saved["kv"]
            kv_parts = [[lay[g][0] for lay in dkv_parts] for g in range(ng)] + [[lay[g][1] for lay in dkv_parts] for g in range(ng)]
            dkv, dh = _proj_dgrad(kv_parts, tabs, W["w_kv", 0], ng, DILATIONS * 2)
            G["w_kv", 0] = _mm_tn(h_kv, dkv[None], W["w_kv", 0].shape[2])
            dx, red_kv = _norm_bwd(dx, [dh], xin, None, vec_kv, 1.0)
    recvd[len(R) - 1] = _comm_call("exchange", halves(R[-1]))
    tail = R[-2] + R[-1]
    others.update(zip(tail, _comm_call("scatter", pair_sums(R[-2], recvd.pop(len(R) - 2)) + pair_sums(R[-1], recvd.pop(len(R) - 1)))))
    small = jnp.concatenate(
        [jnp.stack([jnp.stack(r) for r in dmods]).reshape(-1), red_kv[2], red_kv[1],
         jnp.stack([jnp.stack(r) for r in dnorm]).reshape(-1), jnp.stack(dconvw).reshape(-1), red_kv[0], red_f[0]])
    return loss_part, dx, parts, others, small


BIG = ("ffn1_w_in", "ffn1_w_out", "ffn2_w_in", "ffn2_w_out", "conv_w_in", "conv_w_out", "w_kv", "attn_w_q", "attn_w_o")


def _halved(t):
    if t.ndim == 2:
        t = t[None]
    L, A, B = t.shape
    return t.reshape(L, 2, A // 2, B)


def kernel(x, c, positions, norm_g, ada_w, ada_b, ffn1_w_in, ffn1_w_out, ffn2_w_in, ffn2_w_out, conv_w_in, conv_w, conv_w_out, kv_norm_g, kv_ada_w, kv_ada_b, w_kv, attn_w_q, attn_w_o, final_norm_g, loss_target, m_norm_g, m_ada_w, m_ada_b, m_ffn1_w_in, m_ffn1_w_out, m_ffn2_w_in, m_ffn2_w_out, m_conv_w_in, m_conv_w, m_conv_w_out, m_kv_norm_g, m_kv_ada_w, m_kv_ada_b, m_w_kv, m_attn_w_q, m_attn_w_o, m_final_norm_g, v_norm_g, v_ada_w, v_ada_b, v_ffn1_w_in, v_ffn1_w_out, v_ffn2_w_in, v_ffn2_w_out, v_conv_w_in, v_conv_w, v_conv_w_out, v_kv_norm_g, v_kv_ada_w, v_kv_ada_b, v_w_kv, v_attn_w_q, v_attn_w_o, v_final_norm_g):
    wts = dict(norm_g=norm_g, ada_w=ada_w, ada_b=ada_b, ffn1_w_in=ffn1_w_in, ffn1_w_out=ffn1_w_out, ffn2_w_in=ffn2_w_in,
               ffn2_w_out=ffn2_w_out, conv_w_in=conv_w_in, conv_w=conv_w, conv_w_out=conv_w_out, kv_norm_g=kv_norm_g,
               kv_ada_w=kv_ada_w, kv_ada_b=kv_ada_b, w_kv=w_kv, attn_w_q=attn_w_q, attn_w_o=attn_w_o,
               final_norm_g=final_norm_g)
    ms = dict(norm_g=m_norm_g, ada_w=m_ada_w, ada_b=m_ada_b, ffn1_w_in=m_ffn1_w_in, ffn1_w_out=m_ffn1_w_out,
              ffn2_w_in=m_ffn2_w_in, ffn2_w_out=m_ffn2_w_out, conv_w_in=m_conv_w_in, conv_w=m_conv_w,
              conv_w_out=m_conv_w_out, kv_norm_g=m_kv_norm_g, kv_ada_w=m_kv_ada_w, kv_ada_b=m_kv_ada_b, w_kv=m_w_kv,
              attn_w_q=m_attn_w_q, attn_w_o=m_attn_w_o, final_norm_g=m_final_norm_g)
    vs = dict(norm_g=v_norm_g, ada_w=v_ada_w, ada_b=v_ada_b, ffn1_w_in=v_ffn1_w_in, ffn1_w_out=v_ffn1_w_out,
              ffn2_w_in=v_ffn2_w_in, ffn2_w_out=v_ffn2_w_out, conv_w_in=v_conv_w_in, conv_w=v_conv_w,
              conv_w_out=v_conv_w_out, kv_norm_g=v_kv_norm_g, kv_ada_w=v_kv_ada_w, kv_ada_b=v_kv_ada_b, w_kv=v_w_kv,
              attn_w_q=v_attn_w_q, attn_w_o=v_attn_w_o, final_norm_g=v_final_norm_g)
    order = list(wts)
    S, D = x.shape[1], x.shape[2]
    n_layers, n_conv = norm_g.shape[0], conv_w.shape[0]
    nm, sw = ada_w.shape[2], norm_g.shape[2]
    ix, iy, ic = (lax.axis_index(a) for a in AXES)
    chip = 2 * ix + iy

    pack = jnp.concatenate([norm_g.reshape(-1), conv_w.reshape(-1), c.reshape(-1)])
    npad = (-pack.size) % (8 * LANES)
    allp = _gather8(jnp.pad(pack, (0, npad)).reshape(-1, LANES)).reshape(8, -1)
    n1, n2 = norm_g.size, norm_g.size + conv_w.size
    by_chip = allp[0::2]
    norm_full = jnp.moveaxis(by_chip[:, :n1].reshape(4, n_layers, 3, sw), 0, 2).reshape(n_layers, 3, 4 * sw)
    conv_full = jnp.moveaxis(by_chip[:, n1:n2].reshape(4, n_conv, 3, sw), 0, 2).reshape(n_conv, 3, 4 * sw)
    c_all = allp[:, n2:n2 + D]

    b_l = lax.dynamic_slice_in_dim(ada_b, chip * nm, nm, axis=1)
    mod_sh = _ada_fwd(c_all, ada_w, b_l[:, None, :])
    nkv = kv_ada_w.shape[1]
    bkv = lax.dynamic_slice_in_dim(kv_ada_b, chip * nkv, nkv, axis=0)
    kv_sh = _ada_fwd(c_all, kv_ada_w[None], bkv[None, None, :])
    rows = jnp.concatenate([jnp.moveaxis(mod_sh, 0, 1), jnp.pad(jnp.moveaxis(kv_sh, 0, 1), ((0, 0), (0, 0), (0, nm - nkv)))], axis=1)
    rpad = (-rows.shape[1]) % 8
    mine = _mods_to_owner(jnp.pad(rows, ((0, 0), (0, rpad), (0, 0))))
    mods = jnp.moveaxis(mine[:, :n_layers], 0, 1).reshape(n_layers, 4 * nm)
    kvmods = mine[:, n_layers, :nkv].reshape(4 * nkv)

    place = jnp.stack([chip, ic]).astype(jnp.int32)
    halved = {k: _halved(wts[k]) for k in BIG}
    cast = {(k, l): _cast_shard(halved[k], l, place) for k in BIG for l in range(halved[k].shape[0])}

    loss_part, dx, parts, others, small = _local_step(x[0], loss_target[0], _rope_tables(positions), norm_full, conv_full,
                                                      kv_norm_g, final_norm_g, mods, kvmods, cast, place)
    loss = lax.psum(loss_part, AXES)

    gbufs = []
    for k in BIG:
        gbuf, n_l = None, halved[k].shape[0]
        for l in range(n_l):
            gbuf = _sum_partials(parts[k, l], others[k, l], gbuf, n_l, l, place)
        gbufs.append(gbuf)
    grads = {k: s.reshape(wts[k].shape) for k, s in zip(BIG, _comm_call("share", gbufs))}

    spad = (-small.size) % (8 * LANES)
    gath = _gather8(jnp.pad(small, (0, spad)).reshape(-1, LANES))
    tot = _sum8(gath).reshape(-1)
    gath = gath.reshape(8, -1)
    o = 0

    def take(n):
        nonlocal o
        o += n
        return tot[o - n:o]

    g_mods = take(n_layers * 9 * D).reshape(n_layers, 9 * D)
    g_kvmods = take(2 * D)
    g_norm = take(n_layers * 3 * D).reshape(n_layers, 3, D)
    g_convw = take(n_conv * 3 * D).reshape(n_conv, 3, D)
    g_kvn = take(D)
    g_fin = take(D)
    grads["ada_b"] = g_mods
    grads["kv_ada_b"] = g_kvmods
    grads["norm_g"] = lax.dynamic_slice_in_dim(g_norm, chip * sw, sw, axis=2)
    grads["conv_w"] = lax.dynamic_slice_in_dim(g_convw, chip * sw, sw, axis=2)
    grads["kv_norm_g"] = g_kvn
    grads["final_norm_g"] = g_fin

    dm_all = gath[:, :n_layers * 9 * D].reshape(8, n_layers, 9 * D)
    dm_mine = jnp.moveaxis(lax.dynamic_slice_in_dim(dm_all, chip * nm, nm, axis=2), 0, 1)
    dkv_all = gath[:, n_layers * 9 * D:n_layers * 9 * D + 2 * D]
    dkv_mine = lax.dynamic_slice_in_dim(dkv_all, chip * nkv, nkv, axis=1)[None]
    c_t = c_all.T
    grads["ada_w"] = _ada_wgrad(c_t, dm_mine)
    grads["kv_ada_w"] = _ada_wgrad(c_t, dkv_mine)[0]

    deltas, new_m, new_v = {}, {}, {}
    for k in order:
        shp = wts[k].shape
        two_d = (lambda t: t.reshape(1, -1)) if len(shp) == 1 else (lambda t: t)
        dlt, nmk, nvk = _adamw(two_d(wts[k]), two_d(grads[k]), two_d(ms[k]), two_d(vs[k]))
        deltas[k], new_m[k], new_v[k] = dlt.reshape(shp), nmk.reshape(shp), nvk.reshape(shp)
    return (loss, dx[None], *[grads[k] for k in order], *[deltas[k] for k in order], *[new_m[k] for k in order],
            *[new_v[k] for k in order])
```

```python
import functools
import math

import jax
import jax.numpy as jnp
from jax import lax
from jax.experimental import pallas as pl
from jax.experimental.pallas import tpu as pltpu

F32 = jnp.float32
BF = jnp.bfloat16
MESH = pl.DeviceIdType.MESH
AXES = ("x", "y", "c")

NORM_EPS = 1e-5
HEAD_DIM = 64
HEADS_PER_GROUP = 8
GROUP_W = HEADS_PER_GROUP * HEAD_DIM
DILATIONS = (1, 4, 16)
BAND = 128
ROPE_DIM = 16
ROPE_THETA = 500000.0
SM_SCALE = HEAD_DIM ** -0.5
NEG = -1e30
ADAM_LR, ADAM_B1, ADAM_B2, ADAM_EPS, ADAM_WD, ADAM_STEP = 0.001, 0.9, 0.999, 1e-08, 0.01, 10

V7X_VMEM_BYTES = 64 * 1024 * 1024
VMEM_LIMIT = V7X_VMEM_BYTES - 6 * 1024 * 1024
LANES = 128
TM = 512
TK = 1024


def _cparams(ngrid):
    return pltpu.CompilerParams(dimension_semantics=("arbitrary",) * ngrid, vmem_limit_bytes=VMEM_LIMIT)


def _dot(a, b):
    return jnp.dot(a, b, preferred_element_type=F32)


def _dot_nt(a, b):
    return lax.dot_general(a, b, (((1,), (1,)), ((), ())), preferred_element_type=F32)


def _dot_tn(a, b):
    return lax.dot_general(a, b, (((0,), (0,)), ((), ())), preferred_element_type=F32)


def _tile(n, pref):
    t = min(n, pref)
    while n % t:
        t //= 2
    return t


def _row_tile(n, cap=256, mult=16):
    best = n
    for t in range(mult, min(n, cap) + 1, mult):
        if n % t == 0:
            best = t
    return best


def _rstd(x):
    return lax.rsqrt(jnp.mean(x * x, axis=-1, keepdims=True) + NORM_EPS)


def _norm_mod(x, v):
    return (x * _rstd(x) * v[0:1]) * (1.0 + v[1:2]) + v[2:3]


def _silu_parts(a):
    sg = jax.nn.sigmoid(a)
    return sg, a * sg


def _row(i):
    return lambda *_: (i, 0)


ANY = pl.BlockSpec(memory_space=pl.ANY)
IN_PLACE = ("gather_ici", "gather_d2d", "share")


def _place():
    x, y, c = (lax.axis_index(a) for a in AXES)
    chips = [(1 - x, y), (x, 1 - y), (1 - x, 1 - y)]
    return x, y, c, 2 * x + y, chips, [2 * cx + cy for cx, cy in chips]


def _transfers(kind, ins, outs):
    x, y, c, p, chips, qs = _place()
    sib = (x, y, 1 - c)
    rows = []
    for k, o in enumerate(outs):
        if kind == "gather_ici":
            rows.append([(o.at[p, c], o.at[p, c], (*chips[j], c), o.at[qs[j], c]) for j in range(3)])
        elif kind == "gather_d2d":
            rows.append([(o.at[qs[j], c], o.at[qs[j], c], sib, o.at[qs[j], 1 - c]) for j in range(3)])
        elif kind == "exchange":
            rows.append([(ins[k].at[:, 1 - c], o, sib, o)])
        elif kind == "scatter":
            rows.append([(ins[k].at[qs[j]], o.at[j], (*chips[j], c), o.at[j]) for j in range(3)])
        elif kind == "share":
            rows.append([(o.at[:, c], o.at[:, c], sib, o.at[:, 1 - c])])
    return rows


def _comm_out_shapes(kind, arrays):
    if kind in IN_PLACE:
        return [jax.ShapeDtypeStruct(a.shape, a.dtype) for a in arrays]
    if kind == "exchange":
        return [jax.ShapeDtypeStruct((4,) + a.shape[2:], a.dtype) for a in arrays]
    return [jax.ShapeDtypeStruct((3,) + a.shape[1:], a.dtype) for a in arrays]


def _comm_sems(n):
    return [pltpu.SemaphoreType.DMA((n, 3)), pltpu.SemaphoreType.DMA((n, 3))]


def _comm_start(rows, ssem, rsem):
    for k, row in enumerate(rows):
        for j, (src, dst, dev, _) in enumerate(row):
            pltpu.make_async_remote_copy(src_ref=src, dst_ref=dst, send_sem=ssem.at[k, j], recv_sem=rsem.at[k, j],
                                         device_id=dev, device_id_type=MESH).start()


def _comm_wait(rows, ssem, rsem):
    for k, row in enumerate(rows):
        for j, (src, dst, dev, land) in enumerate(row):
            pltpu.make_async_remote_copy(src_ref=src, dst_ref=dst, send_sem=ssem.at[k, j], recv_sem=rsem.at[k, j],
                                         device_id=dev, device_id_type=MESH).wait_send()
            pltpu.make_async_remote_copy(src_ref=land, dst_ref=land, send_sem=ssem.at[k, j], recv_sem=rsem.at[k, j],
                                         device_id=dev, device_id_type=MESH).wait_recv()


def _comm_call(kind, arrays):
    n = len(arrays)

    def body(*refs):
        rows = _transfers(kind, refs[:n], refs[n:2 * n])
        _comm_start(rows, *refs[2 * n:])
        _comm_wait(rows, *refs[2 * n:])

    return pl.pallas_call(
        body, name="comm_" + kind, in_specs=[ANY] * n, out_specs=[ANY] * n, out_shape=_comm_out_shapes(kind, arrays),
        input_output_aliases={k: k for k in range(n)} if kind in IN_PLACE else {},
        scratch_shapes=_comm_sems(n),
    )(*arrays)


def _ride(comm, n_in, n_out):
    ride = dict(arrays=[], in_specs=[], out_specs=[], out_shape=[], aliases={}, scratch=[], n=0)
    for kind, arrays in comm or []:
        if kind in IN_PLACE:
            ride["aliases"].update({n_in + ride["n"] + k: n_out + ride["n"] + k for k in range(len(arrays))})
        ride["arrays"] += list(arrays)
        ride["out_shape"] += _comm_out_shapes(kind, arrays)
        ride["scratch"] += _comm_sems(len(arrays))
        ride["n"] += len(arrays)
    ride["in_specs"] = ride["out_specs"] = [ANY] * ride["n"]
    return ride


def _ride_when(comm, ins, outs, sems, cond, action):
    if not comm:
        return

    @pl.when(cond)
    def _():
        at = 0
        for e, (kind, arrays) in enumerate(comm):
            n = len(arrays)
            action(_transfers(kind, ins[at:at + n], outs[at:at + n]), *sems[2 * e:2 * e + 2])
            at += n


def _ffn_fwd(x, vec, w_in4, w_out, comm=None):
    S, D = x.shape
    cs = w_in4.shape[2]
    F = 2 * cs
    tm = _tile(S, TM)
    ni = S // tm
    ride = _ride(comm, 5, 4)
    nc = ride["n"]

    def body(*refs):
        x_ref, vec_ref, wa_ref, wb_ref, wo_ref = refs[:5]
        xo_ref, h_ref, ab_ref, y_ref = refs[5 + nc:9 + nc]
        i, j = pl.program_id(0), pl.program_id(1)
        riding = (comm, refs[5:5 + nc], refs[9 + nc:9 + 2 * nc], refs[9 + 2 * nc:])
        _ride_when(*riding, jnp.logical_and(i == 0, j == 0), _comm_start)

        @pl.when(j == 0)
        def _():
            h_ref[...] = _norm_mod(x_ref[...], vec_ref[...]).astype(BF)
            y_ref[...] = jnp.zeros_like(y_ref)

        h = h_ref[...]
        a = _dot(h, wa_ref[...])
        b = _dot(h, wb_ref[...])
        ab_ref[0] = a.astype(BF)
        ab_ref[1] = b.astype(BF)
        _, s = _silu_parts(a)
        y_ref[...] += _dot((s * b).astype(BF), wo_ref[...])

        @pl.when(j == 1)
        def _():
            xo_ref[...] = x_ref[...] + (0.5 * (1.0 + vec_ref[3:4, :])) * y_ref[...]

        _ride_when(*riding, jnp.logical_and(i == ni - 1, j == 1), _comm_wait)

    tok = pl.BlockSpec((tm, D), lambda i, j: (i, 0))
    return pl.pallas_call(
        body, name="ffn_fwd", grid=(ni, 2),
        in_specs=[tok, pl.BlockSpec((8, D), lambda i, j: (0, 0)),
                  pl.BlockSpec((None, D, cs), lambda i, j: (j, 0, 0)),
                  pl.BlockSpec((None, D, cs), lambda i, j: (j + 2, 0, 0)),
                  pl.BlockSpec((cs, D), lambda i, j: (j, 0))] + ride["in_specs"],
        out_specs=[tok, tok, pl.BlockSpec((2, tm, cs), lambda i, j: (0, i, j)), tok] + ride["out_specs"],
        out_shape=[jax.ShapeDtypeStruct((S, D), F32), jax.ShapeDtypeStruct((S, D), BF),
                   jax.ShapeDtypeStruct((2, S, F), BF), jax.ShapeDtypeStruct((S, D), F32)] + ride["out_shape"],
        input_output_aliases=ride["aliases"], scratch_shapes=ride["scratch"],
        compiler_params=_cparams(2),
    )(x, vec, w_in4, w_in4, w_out, *ride["arrays"])


def _ffn_dgrad(dxo, vec, ab, w_in4, w_out, comm=None):
    S, D = dxo.shape
    cs = w_in4.shape[2]
    F = 2 * cs
    tm = _tile(S, TM)
    ni = S // tm
    ride = _ride(comm, 6, 4)
    nc = ride["n"]

    def body(*refs):
        dxo_ref, vec_ref, ab_ref, wa_ref, wb_ref, wo_ref = refs[:6]
        dy_ref, u_ref, dab_ref, dh_ref = refs[6 + nc:10 + nc]
        i, j = pl.program_id(0), pl.program_id(1)
        riding = (comm, refs[6:6 + nc], refs[10 + nc:10 + 2 * nc], refs[10 + 2 * nc:])
        _ride_when(*riding, jnp.logical_and(i == 0, j == 0), _comm_start)

        @pl.when(j == 0)
        def _():
            dy_ref[...] = ((0.5 * (1.0 + vec_ref[3:4, :])) * dxo_ref[...]).astype(BF)
            dh_ref[...] = jnp.zeros_like(dh_ref)

        du = _dot_nt(dy_ref[...], wo_ref[...])
        a = ab_ref[0].astype(F32)
        b = ab_ref[1].astype(F32)
        sg, s = _silu_parts(a)
        u_ref[...] = (s * b).astype(BF)
        da = (du * b * (sg * (1.0 + a * (1.0 - sg)))).astype(BF)
        db = (du * s).astype(BF)
        dab_ref[0] = da
        dab_ref[1] = db
        dh_ref[...] += _dot_nt(da, wa_ref[...]) + _dot_nt(db, wb_ref[...])
        _ride_when(*riding, jnp.logical_and(i == ni - 1, j == 1), _comm_wait)

    tok = pl.BlockSpec((tm, D), lambda i, j: (i, 0))
    return pl.pallas_call(
        body, name="ffn_dgrad", grid=(ni, 2),
        in_specs=[tok, pl.BlockSpec((8, D), lambda i, j: (0, 0)),
                  pl.BlockSpec((2, tm, cs), lambda i, j: (0, i, j)),
                  pl.BlockSpec((None, D, cs), lambda i, j: (j, 0, 0)),
                  pl.BlockSpec((None, D, cs), lambda i, j: (j + 2, 0, 0)),
                  pl.BlockSpec((cs, D), lambda i, j: (j, 0))] + ride["in_specs"],
        out_specs=[tok, pl.BlockSpec((tm, cs), lambda i, j: (i, j)),
                   pl.BlockSpec((2, tm, cs), lambda i, j: (0, i, j)), tok] + ride["out_specs"],
        out_shape=[jax.ShapeDtypeStruct((S, D), BF), jax.ShapeDtypeStruct((S, F), BF),
                   jax.ShapeDtypeStruct((2, S, F), BF), jax.ShapeDtypeStruct((S, D), F32)] + ride["out_shape"],
        input_output_aliases=ride["aliases"], scratch_shapes=ride["scratch"],
        compiler_params=_cparams(2),
    )(dxo, vec, ab, w_in4, w_in4, w_out, *ride["arrays"])


def _mm_tn(a, b3, cs):
    K, M = a.shape
    G, _, Nb = b3.shape
    N = G * Nb
    tn = math.gcd(cs, Nb)
    tmo = M if M <= 1024 else M // 2
    tk = _tile(K, TK)

    def body(a_ref, b_ref, o_ref):
        @pl.when(pl.program_id(2) == 0)
        def _():
            o_ref[...] = jnp.zeros_like(o_ref)

        o_ref[...] += _dot_tn(a_ref[...].astype(BF), b_ref[...].astype(BF))

    return pl.pallas_call(
        body, name="wgrad_tn", grid=(M // tmo, N // tn, K // tk),
        in_specs=[pl.BlockSpec((tk, tmo), lambda m, n, k: (k, m)),
                  pl.BlockSpec((None, tk, tn), lambda m, n, k: ((n * tn) // Nb, k, ((n * tn) % Nb) // tn))],
        out_specs=pl.BlockSpec((None, tmo, tn), lambda m, n, k: ((n * tn) // cs, m, ((n * tn) % cs) // tn)),
        out_shape=jax.ShapeDtypeStruct((N // cs, M, cs), F32),
        compiler_params=_cparams(3),
    )(a, b3)


def _norm_bwd(dxo, dhs, x, y, vec, coef):
    S, D = x.shape
    tm = _tile(S, TM)
    nh = len(dhs)
    has_y = y is not None

    def body(*refs):
        dxo_ref = refs[0]
        dh_refs = refs[1:1 + nh]
        x_ref = refs[1 + nh]
        y_ref = refs[2 + nh] if has_y else None
        vec_ref, dx_ref, red_ref = refs[-3:]

        @pl.when(pl.program_id(0) == 0)
        def _():
            red_ref[...] = jnp.zeros_like(red_ref)

        xv = x_ref[...]
        dxo = dxo_ref[...]
        dh = dh_refs[0][...]
        for r in dh_refs[1:]:
            dh = dh + r[...]
        g = vec_ref[0:1, :]
        sc = vec_ref[1:2, :]
        r = _rstd(xv)
        xh = xv * r
        dhn = dh * (1.0 + sc)
        dxh = dhn * g
        dx_ref[...] = dxo + r * (dxh - xh * jnp.mean(dxh * xh, axis=-1, keepdims=True))
        red_ref[0:1, :] += jnp.sum(dhn * xh, axis=0, keepdims=True)
        red_ref[1:2, :] += jnp.sum(dh * (xh * g), axis=0, keepdims=True)
        red_ref[2:3, :] += jnp.sum(dh, axis=0, keepdims=True)
        if has_y:
            red_ref[3:4, :] += coef * jnp.sum(dxo * y_ref[...], axis=0, keepdims=True)

    tok = pl.BlockSpec((tm, D), lambda i: (i, 0))
    small = pl.BlockSpec((8, D), lambda i: (0, 0))
    ops = [dxo, *dhs, x] + ([y] if has_y else []) + [vec]
    return pl.pallas_call(
        body, name="norm_bwd", grid=(S // tm,),
        in_specs=[tok] * (len(ops) - 1) + [small],
        out_specs=[tok, small],
        out_shape=[jax.ShapeDtypeStruct((S, D), F32), jax.ShapeDtypeStruct((8, D), F32)],
        compiler_params=_cparams(1),
    )(*ops)


def _conv_fwd(x, vec, cw, w_in4, w_out):
    S, D = x.shape
    cs = w_in4.shape[2]
    tm = _tile(S, TM)

    def body(x_ref, vec_ref, cw_ref, wi_ref, wo_ref, xo_ref, h_ref, s4_ref, z_ref, y_ref, vs_ref):
        @pl.when(pl.program_id(0) == 0)
        def _():
            vs_ref[0:8, :] = jnp.zeros((8, D), F32)

        xv = x_ref[...]
        h = _norm_mod(xv, vec_ref[...]).astype(BF)
        h_ref[...] = h
        bcu = jnp.concatenate([_dot(h, wi_ref[q]) for q in range(4)], axis=1)
        bg, cg, u = bcu[:, :D], bcu[:, D:2 * D], bcu[:, 2 * D:]
        v = cg * u
        vs_ref[8:8 + tm, :] = v
        conv = cw_ref[0:1, :] * vs_ref[pl.ds(6, tm), :] + cw_ref[1:2, :] * vs_ref[pl.ds(7, tm), :] + cw_ref[2:3, :] * v
        vs_ref[0:8, :] = vs_ref[tm:tm + 8, :]
        z = (bg * conv).astype(BF)
        s4_ref[0] = bg.astype(BF)
        s4_ref[1] = cg.astype(BF)
        s4_ref[2] = u.astype(BF)
        s4_ref[3] = conv.astype(BF)
        z_ref[...] = z
        y = _dot(z, wo_ref[...])
        y_ref[...] = y
        xo_ref[...] = xv + (1.0 + vec_ref[3:4, :]) * y

    tok = pl.BlockSpec((tm, D), lambda i: (i, 0))
    small = pl.BlockSpec((8, D), lambda i: (0, 0))
    return pl.pallas_call(
        body, name="conv_fwd", grid=(S // tm,),
        in_specs=[tok, small, small, pl.BlockSpec((4, D, cs), lambda i: (0, 0, 0)),
                  pl.BlockSpec((D, D), lambda i: (0, 0))],
        out_specs=[tok, tok, pl.BlockSpec((4, tm, D), lambda i: (0, i, 0)), tok, tok],
        out_shape=[jax.ShapeDtypeStruct((S, D), F32), jax.ShapeDtypeStruct((S, D), BF),
                   jax.ShapeDtypeStruct((4, S, D), BF), jax.ShapeDtypeStruct((S, D), BF),
                   jax.ShapeDtypeStruct((S, D), F32)],
        scratch_shapes=[pltpu.VMEM((tm + 8, D), F32)],
        compiler_params=_cparams(1),
    )(x, vec, cw, w_in4, w_out)


def _conv_dgrad(dxo, vec, cw, s4, w_in4, w_out):
    S, D = dxo.shape
    cs = w_in4.shape[2]
    tm = _tile(S, TM)
    nt = S // tm

    def body(dxo_ref, vec_ref, cw_ref, s4_ref, wi_ref, wo_ref, dy_ref, d3_ref, dh_ref, dcw_ref, ds_ref):
        @pl.when(pl.program_id(0) == 0)
        def _():
            ds_ref[tm:tm + 8, :] = jnp.zeros((8, D), F32)
            dcw_ref[...] = jnp.zeros_like(dcw_ref)

        dy = ((1.0 + vec_ref[3:4, :]) * dxo_ref[...]).astype(BF)
        dy_ref[...] = dy
        dz = _dot_nt(dy, wo_ref[...])
        bg = s4_ref[0].astype(F32)
        cg = s4_ref[1].astype(F32)
        u = s4_ref[2].astype(F32)
        conv = s4_ref[3].astype(F32)
        dbg = dz * conv
        dconv = dz * bg
        ds_ref[0:tm, :] = dconv
        d1 = ds_ref[pl.ds(1, tm), :]
        d2 = ds_ref[pl.ds(2, tm), :]
        ds_ref[tm:tm + 8, :] = ds_ref[0:8, :]
        dv = cw_ref[2:3, :] * dconv + cw_ref[1:2, :] * d1 + cw_ref[0:1, :] * d2
        v = cg * u
        dcw_ref[0:1, :] += jnp.sum(d2 * v, axis=0, keepdims=True)
        dcw_ref[1:2, :] += jnp.sum(d1 * v, axis=0, keepdims=True)
        dcw_ref[2:3, :] += jnp.sum(dconv * v, axis=0, keepdims=True)
        dbcu = jnp.concatenate([dbg, dv * u, dv * cg], axis=1).astype(BF)
        d3_ref[...] = dbcu
        dh = _dot_nt(dbcu[:, 0:cs], wi_ref[0])
        for q in range(1, 4):
            dh = dh + _dot_nt(dbcu[:, q * cs:(q + 1) * cs], wi_ref[q])
        dh_ref[...] = dh

    tok = pl.BlockSpec((tm, D), lambda i: (nt - 1 - i, 0))
    small = pl.BlockSpec((8, D), lambda i: (0, 0))
    return pl.pallas_call(
        body, name="conv_dgrad", grid=(nt,),
        in_specs=[tok, small, small, pl.BlockSpec((4, tm, D), lambda i: (0, nt - 1 - i, 0)),
                  pl.BlockSpec((4, D, cs), lambda i: (0, 0, 0)), pl.BlockSpec((D, D), lambda i: (0, 0))],
        out_specs=[tok, pl.BlockSpec((tm, 3 * D), lambda i: (nt - 1 - i, 0)), tok, small],
        out_shape=[jax.ShapeDtypeStruct((S, D), BF), jax.ShapeDtypeStruct((S, 3 * D), BF),
                   jax.ShapeDtypeStruct((S, D), F32), jax.ShapeDtypeStruct((8, D), F32)],
        scratch_shapes=[pltpu.VMEM((tm + 8, D), F32)],
        compiler_params=_cparams(1),
    )(dxo, vec, cw, s4, w_in4, w_out)


def _rope_tables(positions):
    S = positions.shape[-1]
    inv = ROPE_THETA ** (-jnp.arange(0, ROPE_DIM, 2, dtype=F32) / ROPE_DIM)
    ang = positions.reshape(S, 1).astype(F32) * inv
    cos = jnp.tile(jnp.cos(ang), (1, LANES // 8))
    sin = jnp.tile(jnp.sin(ang), (1, LANES // 8))
    l64 = jnp.arange(LANES) % HEAD_DIM
    return jnp.stack([jnp.where(l64 < ROPE_DIM, cos, 1.0),
                      jnp.where(l64 < ROPE_DIM // 2, -sin, 0.0),
                      jnp.where((l64 >= ROPE_DIM // 2) & (l64 < ROPE_DIM), sin, 0.0)])


def _rope(t, tab_ref):
    return t * tab_ref[0] + pltpu.roll(t, LANES - 8, 1) * tab_ref[1] + pltpu.roll(t, 8, 1) * tab_ref[2]


def _rope_t(d, tab_ref):
    return d * tab_ref[0] + pltpu.roll(d * tab_ref[1], 8, 1) + pltpu.roll(d * tab_ref[2], LANES - 8, 1)


GROUP_CH = GROUP_W // LANES


def _to_dilated(src_ref, c0, dst_ref, d):
    n = src_ref.shape[1]
    for r in range(d):
        for ch in range(GROUP_CH):
            rows = src_ref[c0 + ch] if d == 1 else src_ref.at[c0 + ch][pl.ds(r, n // d, stride=d), :]
            dst_ref[:, r * GROUP_W + ch * LANES:r * GROUP_W + (ch + 1) * LANES] = rows.astype(dst_ref.dtype)


def _from_dilated(val, dst_ref, c0, d):
    n = dst_ref.shape[1]
    for r in range(d):
        for ch in range(GROUP_CH):
            cols = val[:, r * GROUP_W + ch * LANES:r * GROUP_W + (ch + 1) * LANES]
            if d == 1:
                dst_ref[c0 + ch] = cols
            else:
                dst_ref.at[c0 + ch][pl.ds(r, n // d, stride=d), :] = cols


def _group(ref, c0):
    return jnp.concatenate([ref[c0 + ch] for ch in range(GROUP_CH)], axis=1)


def _put_group(ref, c0, val):
    for ch in range(GROUP_CH):
        ref[c0 + ch] = val[:, ch * LANES:(ch + 1) * LANES]


def _dil_spec(tm, d):
    return pl.BlockSpec((tm // d, d * GROUP_W), lambda i: (i, 0))


def _proj_fwd(x, vec, w4, tabs, n_rope, dils, scale=1.0):
    S, D = x.shape
    cs = w4.shape[2]
    N = 4 * cs
    tm = _tile(S, TM)

    def body(x_ref, vec_ref, w_ref, tab_ref, h_ref, *rest):
        outs, acc_ref = rest[:-1], rest[-1]
        h = _norm_mod(x_ref[...], vec_ref[...]).astype(BF)
        h_ref[...] = h
        per = cs // LANES
        for q in range(4):
            acc = _dot(h, w_ref[q])
            if scale != 1.0:
                acc = acc * scale
            cols = [acc[:, ch * LANES:(ch + 1) * LANES] for ch in range(per)]
            for ch, t in enumerate(cols):
                acc_ref[q * per + ch] = _rope(t, tab_ref) if q * per + ch < n_rope else t
        for j, d in enumerate(dils):
            _to_dilated(acc_ref, j * GROUP_CH, outs[j], d)

    h, *outs = pl.pallas_call(
        body, name="proj_fwd", grid=(S // tm,),
        in_specs=[pl.BlockSpec((tm, D), lambda i: (i, 0)), pl.BlockSpec((8, D), lambda i: (0, 0)),
                  pl.BlockSpec((4, D, cs), lambda i: (0, 0, 0)), pl.BlockSpec((3, tm, LANES), lambda i: (0, i, 0))],
        out_specs=[pl.BlockSpec((tm, D), lambda i: (i, 0))] + [_dil_spec(tm, d) for d in dils],
        out_shape=[jax.ShapeDtypeStruct((S, D), BF)] + [jax.ShapeDtypeStruct((S // d, d * GROUP_W), BF) for d in dils],
        scratch_shapes=[pltpu.VMEM((N // LANES, tm, LANES), F32)],
        compiler_params=_cparams(1),
    )(x, vec, w4, tabs)
    return h, outs


def _proj_dgrad(parts, tabs, w4, n_rope_groups, dils):
    cs, D = w4.shape[2], w4.shape[1]
    N = 4 * cs
    S = parts[0][0].shape[0] * dils[0]
    tm = _tile(S, TM)
    flat = [p for grp in parts for p in grp]
    counts = [len(grp) for grp in parts]

    def body(*refs):
        prefs = refs[:len(flat)]
        tab_ref, w_ref, dz_ref, dh_ref, z_ref = refs[len(flat):]
        k = 0
        for j, cnt in enumerate(counts):
            z = prefs[k][...]
            for r in prefs[k + 1:k + cnt]:
                z = z + r[...]
            k += cnt
            _from_dilated(z, z_ref, 0, dils[j])
            if j < n_rope_groups:
                z = jnp.concatenate([_rope_t(z_ref[ch], tab_ref) for ch in range(GROUP_CH)], axis=1)
            else:
                z = _group(z_ref, 0)
            dz_ref[:, j * GROUP_W:(j + 1) * GROUP_W] = z.astype(BF)
        dh = _dot_nt(dz_ref[:, 0:cs], w_ref[0])
        for q in range(1, 4):
            dh = dh + _dot_nt(dz_ref[:, q * cs:(q + 1) * cs], w_ref[q])
        dh_ref[...] = dh

    return pl.pallas_call(
        body, name="proj_dgrad", grid=(S // tm,),
        in_specs=[_dil_spec(tm, d) for d, cnt in zip(dils, counts) for _ in range(cnt)]
        + [pl.BlockSpec((3, tm, LANES), lambda i: (0, i, 0)), pl.BlockSpec((4, D, cs), lambda i: (0, 0, 0))],
        out_specs=[pl.BlockSpec((tm, N), lambda i: (i, 0)), pl.BlockSpec((tm, D), lambda i: (i, 0))],
        out_shape=[jax.ShapeDtypeStruct((S, N), BF), jax.ShapeDtypeStruct((S, D), F32)],
        scratch_shapes=[pltpu.VMEM((GROUP_CH, tm, LANES), F32)],
        compiler_params=_cparams(1),
    )(*flat, tabs, w4)


def _band_masks():
    qi = lax.broadcasted_iota(jnp.int32, (BAND, BAND), 0)
    kj = lax.broadcasted_iota(jnp.int32, (BAND, BAND), 1)
    return kj <= qi, kj >= qi


def _head(ref, h):
    return ref[:, h * HEAD_DIM:(h + 1) * HEAD_DIM]


def _attn_fwd(q, k, v, d):
    sd = q.shape[0]
    nb = sd // BAND

    def body(q_ref, kp_ref, kc_ref, vp_ref, vc_ref, o_ref, l_ref):
        same, prev = _band_masks()
        prev = jnp.logical_and(prev, pl.program_id(1) > 0)
        for h in range(HEADS_PER_GROUP):
            qh = _head(q_ref, h)
            sc = jnp.where(same, _dot_nt(qh, _head(kc_ref, h)), NEG)
            sp = jnp.where(prev, _dot_nt(qh, _head(kp_ref, h)), NEG)
            m = jnp.maximum(jnp.max(sc, axis=-1, keepdims=True), jnp.max(sp, axis=-1, keepdims=True))
            pc = jnp.exp(sc - m)
            pp = jnp.exp(sp - m)
            den = jnp.sum(pc, axis=-1, keepdims=True) + jnp.sum(pp, axis=-1, keepdims=True)
            o = (_dot(pc.astype(BF), _head(vc_ref, h)) + _dot(pp.astype(BF), _head(vp_ref, h))) / den
            o_ref[:, h * HEAD_DIM:(h + 1) * HEAD_DIM] = o
            l_ref[:, h * HEAD_DIM:(h + 1) * HEAD_DIM] = jnp.broadcast_to(m + jnp.log(den), (BAND, HEAD_DIM))

    blk = (BAND, GROUP_W)
    cur = pl.BlockSpec(blk, lambda r, i: (i, r))
    prv = pl.BlockSpec(blk, lambda r, i: (jnp.maximum(i - 1, 0), r))
    return pl.pallas_call(
        body, name=f"attn_fwd_d{d}", grid=(d, nb),
        in_specs=[cur, prv, cur, prv, cur], out_specs=[cur, cur],
        out_shape=[jax.ShapeDtypeStruct((sd, d * GROUP_W), F32)] * 2,
        compiler_params=_cparams(2),
    )(q, k, k, v, v)


def _attn_out(os, ls, x, vec, wo4):
    S, D = x.shape
    cs = wo4.shape[2]
    tm = _tile(S, TM)
    ng = len(DILATIONS)

    def body(*refs):
        o_refs, l_refs = refs[0:2 * ng:2], refs[1:2 * ng:2]
        x_ref, vec_ref, wo_ref = refs[2 * ng:2 * ng + 3]
        outs = refs[2 * ng + 3:]
        mix_refs, lj_refs = outs[0:2 * ng:2], outs[1:2 * ng:2]
        y_ref, xo_ref, nat_ref = outs[2 * ng:]
        ov, lv = [], []
        for g in range(ng):
            _from_dilated(o_refs[g][...], nat_ref, 2 * g * GROUP_CH, DILATIONS[g])
            _from_dilated(l_refs[g][...], nat_ref, (2 * g + 1) * GROUP_CH, DILATIONS[g])
            ov.append(_group(nat_ref, 2 * g * GROUP_CH))
            lv.append(_group(nat_ref, (2 * g + 1) * GROUP_CH))
        mx = jnp.maximum(jnp.maximum(lv[0], lv[1]), lv[2])
        es = [jnp.exp(t - mx) for t in lv]
        den = es[0] + es[1] + es[2]
        mix = (es[0] * ov[0] + es[1] * ov[1] + es[2] * ov[2]) / den
        at_mix, at_lj = 2 * ng * GROUP_CH, (2 * ng + 1) * GROUP_CH
        _put_group(nat_ref, at_mix, mix)
        _put_group(nat_ref, at_lj, mx + jnp.log(den))
        for g in range(ng):
            _to_dilated(nat_ref, at_mix, mix_refs[g], DILATIONS[g])
            _to_dilated(nat_ref, at_lj, lj_refs[g], DILATIONS[g])
        mb = mix.astype(BF)
        y = jnp.concatenate([_dot(mb, wo_ref[q]) for q in range(4)], axis=1)
        y_ref[...] = y
        xo_ref[...] = x_ref[...] + (1.0 + vec_ref[3:4, :]) * y

    tok = pl.BlockSpec((tm, D), lambda i: (i, 0))
    dil = [_dil_spec(tm, d) for d in DILATIONS for _ in range(2)]
    dil_shape = [jax.ShapeDtypeStruct((S // d, d * GROUP_W), F32) for d in DILATIONS for _ in range(2)]
    outs = pl.pallas_call(
        body, name="attn_out", grid=(S // tm,),
        in_specs=dil + [tok, pl.BlockSpec((8, D), lambda i: (0, 0)), pl.BlockSpec((4, GROUP_W, cs), lambda i: (0, 0, 0))],
        out_specs=dil + [tok, tok],
        out_shape=dil_shape + [jax.ShapeDtypeStruct((S, D), F32)] * 2,
        scratch_shapes=[pltpu.VMEM(((2 * ng + 2) * GROUP_CH, tm, LANES), F32)],
        compiler_params=_cparams(1),
    )(*[t for pair in zip(os, ls) for t in pair], x, vec, wo4)
    return list(outs[0:2 * ng:2]), list(outs[1:2 * ng:2]), outs[2 * ng], outs[2 * ng + 1]


def _attn_out_dgrad(dxo, vec, wo4):
    S, D = dxo.shape
    cs = wo4.shape[2]
    tm = _tile(S, TM)

    def body(dxo_ref, vec_ref, wo_ref, dy_ref, *rest):
        dm_refs, nat_ref = rest[:-1], rest[-1]
        dy = ((1.0 + vec_ref[3:4, :]) * dxo_ref[...]).astype(BF)
        dy_ref[...] = dy
        dm = _dot_nt(dy[:, 0:cs], wo_ref[0])
        for q in range(1, 4):
            dm = dm + _dot_nt(dy[:, q * cs:(q + 1) * cs], wo_ref[q])
        _put_group(nat_ref, 0, dm)
        for g, d in enumerate(DILATIONS):
            _to_dilated(nat_ref, 0, dm_refs[g], d)

    tok = pl.BlockSpec((tm, D), lambda i: (i, 0))
    dy, *dms = pl.pallas_call(
        body, name="attn_out_dgrad", grid=(S // tm,),
        in_specs=[tok, pl.BlockSpec((8, D), lambda i: (0, 0)), pl.BlockSpec((4, GROUP_W, cs), lambda i: (0, 0, 0))],
        out_specs=[tok] + [_dil_spec(tm, d) for d in DILATIONS],
        out_shape=[jax.ShapeDtypeStruct((S, D), BF)] + [jax.ShapeDtypeStruct((S // d, d * GROUP_W), F32) for d in DILATIONS],
        scratch_shapes=[pltpu.VMEM((GROUP_CH, tm, LANES), F32)],
        compiler_params=_cparams(1),
    )(dxo, vec, wo4)
    return dy, dms


def _prob_and_ds(qh, kh, vh, doh, oh, lrow, mask):
    s = jnp.where(mask, _dot_nt(qh, kh), NEG)
    p = jnp.exp(s - lrow)
    delta = jnp.sum(doh * oh, axis=-1, keepdims=True)
    dp = _dot_nt(doh.astype(BF), vh)
    return p, p * (dp - delta)


def _attn_bwd(q, k, v, dmix, mix, lj, d):
    sd = q.shape[0]
    nb = sd // BAND

    def body(q_ref, kp_ref, kc_ref, vp_ref, vc_ref, do_ref, o_ref, l_ref, dq_ref, dk_ref, dv_ref, keep_k, keep_v):
        i = pl.program_id(1)

        @pl.when(i == 0)
        def _():
            keep_k[...] = jnp.zeros_like(keep_k)
            keep_v[...] = jnp.zeros_like(keep_v)

        @pl.when(i < nb)
        def _():
            same, prev = _band_masks()
            prev = jnp.logical_and(prev, i > 0)
            for h in range(HEADS_PER_GROUP):
                hs = slice(h * HEAD_DIM, (h + 1) * HEAD_DIM)
                qh, doh, oh = _head(q_ref, h), _head(do_ref, h), _head(o_ref, h)
                dob = doh.astype(BF)
                lrow = l_ref[:, h * HEAD_DIM:h * HEAD_DIM + 1]
                kc, kp = _head(kc_ref, h), _head(kp_ref, h)
                pc, dsc = _prob_and_ds(qh, kc, _head(vc_ref, h), doh, oh, lrow, same)
                pp, dsp = _prob_and_ds(qh, kp, _head(vp_ref, h), doh, oh, lrow, prev)
                dsc, dsp = dsc.astype(BF), dsp.astype(BF)
                dq_ref[:, hs] = (_dot(dsc, kc) + _dot(dsp, kp)) * SM_SCALE
                dk_ref[:, hs] = keep_k[:, hs] + _dot_tn(dsp, qh)
                dv_ref[:, hs] = keep_v[:, hs] + _dot_tn(pp.astype(BF), dob)
                keep_k[:, hs] = _dot_tn(dsc, qh)
                keep_v[:, hs] = _dot_tn(pc.astype(BF), dob)

        @pl.when(i == nb)
        def _():
            dk_ref[...] = keep_k[...]
            dv_ref[...] = keep_v[...]

    blk = (BAND, GROUP_W)
    cur = pl.BlockSpec(blk, lambda r, i: (jnp.minimum(i, nb - 1), r))
    prv = pl.BlockSpec(blk, lambda r, i: (jnp.clip(i - 1, 0, nb - 1), r))
    return pl.pallas_call(
        body, name=f"attn_bwd_d{d}", grid=(d, nb + 1),
        in_specs=[cur, prv, cur, prv, cur, cur, cur, cur], out_specs=[cur, prv, prv],
        out_shape=[jax.ShapeDtypeStruct((sd, d * GROUP_W), F32)] * 3,
        scratch_shapes=[pltpu.VMEM(blk, F32), pltpu.VMEM(blk, F32)],
        compiler_params=_cparams(2),
    )(q, k, k, v, v, dmix, mix, lj)


def _final_loss(x, gvec, tgt):
    S, D = x.shape
    tm = _tile(S, TM)

    def body(x_ref, g_ref, t_ref, dx_ref, red_ref):
        @pl.when(pl.program_id(0) == 0)
        def _():
            red_ref[...] = jnp.zeros_like(red_ref)

        xv = x_ref[...]
        g = g_ref[0:1, :]
        r = _rstd(xv)
        xh = xv * r
        err = xh * g - t_ref[...]
        dy = err * (1.0 / D)
        dxh = dy * g
        dx_ref[...] = r * (dxh - xh * jnp.mean(dxh * xh, axis=-1, keepdims=True))
        red_ref[0:1, :] += jnp.sum(dy * xh, axis=0, keepdims=True)
        red_ref[1:2, :] += jnp.sum(err * err, axis=0, keepdims=True)

    tok = pl.BlockSpec((tm, D), lambda i: (i, 0))
    small = pl.BlockSpec((8, D), lambda i: (0, 0))
    return pl.pallas_call(
        body, name="final_loss", grid=(S // tm,),
        in_specs=[tok, small, tok], out_specs=[tok, small],
        out_shape=[jax.ShapeDtypeStruct((S, D), F32), jax.ShapeDtypeStruct((8, D), F32)],
        compiler_params=_cparams(1),
    )(x, gvec, tgt)


def _adamw(w, g, m, v):
    shape = w.shape
    C = shape[-1]
    R = w.size // C
    tr = 256 if R % 256 == 0 else R

    def body(w_ref, g_ref, m_ref, v_ref, d_ref, nm_ref, nv_ref):
        gv = g_ref[...]
        nm = ADAM_B1 * m_ref[...] + (1.0 - ADAM_B1) * gv
        nv = ADAM_B2 * v_ref[...] + (1.0 - ADAM_B2) * (gv * gv)
        m_hat = nm / (1.0 - ADAM_B1 ** ADAM_STEP)
        v_hat = nv / (1.0 - ADAM_B2 ** ADAM_STEP)
        d_ref[...] = -ADAM_LR * (m_hat / (jnp.sqrt(v_hat) + ADAM_EPS) + ADAM_WD * w_ref[...])
        nm_ref[...] = nm
        nv_ref[...] = nv

    spec = pl.BlockSpec((tr, C), lambda i: (i, 0))
    outs = pl.pallas_call(
        body, name="adamw", grid=(R // tr,),
        in_specs=[spec] * 4, out_specs=[spec] * 3,
        out_shape=[jax.ShapeDtypeStruct((R, C), F32)] * 3,
        compiler_params=_cparams(1),
    )(*(t.reshape(R, C) for t in (w, g, m, v)))
    return tuple(o.reshape(shape) for o in outs)


def _ada_fwd(c_all, w, b):
    L, D, N = w.shape
    tn = 768 if N % 768 == 0 else _tile(N, 512)

    def body(c_ref, w_ref, b_ref, o_ref):
        cv = c_ref[...]
        cond = (cv * jax.nn.sigmoid(cv)).astype(BF)
        o_ref[...] = _dot(cond, w_ref[...].astype(BF)) + b_ref[...]

    return pl.pallas_call(
        body, name="ada_fwd", grid=(L, N // tn),
        in_specs=[pl.BlockSpec((8, D), lambda l, n: (0, 0)), pl.BlockSpec((None, D, tn), lambda l, n: (l, 0, n)),
                  pl.BlockSpec((None, 1, tn), lambda l, n: (l, 0, n))],
        out_specs=pl.BlockSpec((None, 8, tn), lambda l, n: (l, 0, n)),
        out_shape=jax.ShapeDtypeStruct((L, 8, N), F32),
        compiler_params=_cparams(2),
    )(c_all, w, b)


def _ada_wgrad(c_all_t, dm):
    D = c_all_t.shape[0]
    L, _, N = dm.shape
    tn = 256

    def body(c_ref, dm_ref, o_ref):
        cv = c_ref[...]
        cond = cv * jax.nn.sigmoid(cv)
        acc = cond[:, 0:1] * dm_ref[0:1, :]
        for b in range(1, 8):
            acc = acc + cond[:, b:b + 1] * dm_ref[b:b + 1, :]
        o_ref[...] = acc

    return pl.pallas_call(
        body, name="ada_wgrad", grid=(L, N // tn),
        in_specs=[pl.BlockSpec((D, 8), lambda l, n: (0, 0)), pl.BlockSpec((None, 8, tn), lambda l, n: (l, 0, n))],
        out_specs=pl.BlockSpec((None, D, tn), lambda l, n: (l, 0, n)),
        out_shape=jax.ShapeDtypeStruct((L, D, N), F32),
        compiler_params=_cparams(2),
    )(c_all_t, dm)


def _sum8(g):
    _, R, C = g.shape

    def body(g_ref, o_ref):
        acc = g_ref[0]
        for b in range(1, 8):
            acc = acc + g_ref[b]
        o_ref[...] = acc

    return pl.pallas_call(body, name="sum8", out_shape=jax.ShapeDtypeStruct((R, C), F32),
                          in_specs=[pl.BlockSpec(memory_space=pltpu.VMEM)],
                          out_specs=pl.BlockSpec(memory_space=pltpu.VMEM))(g)


def _pair_add(g4, recv, cidx):
    _, _, ha, B = g4.shape
    tr = _row_tile(ha)

    def body(c_ref, g_ref, r_ref, o_ref):
        o_ref[...] = (g_ref[...] + r_ref[...]).astype(BF)

    return pl.pallas_call(
        body, name="pair_add",
        grid_spec=pltpu.PrefetchScalarGridSpec(
            num_scalar_prefetch=1, grid=(4, ha // tr),
            in_specs=[pl.BlockSpec((None, None, tr, B), lambda s, i, c: (s, c[0], i, 0)),
                      pl.BlockSpec((None, tr, B), lambda s, i, c: (s, i, 0))],
            out_specs=pl.BlockSpec((None, tr, B), lambda s, i, c: (s, i, 0))),
        out_shape=jax.ShapeDtypeStruct((4, ha, B), BF),
        compiler_params=_cparams(2),
    )(cidx, g4, recv)


def _sum_partials(part, recv, gbuf, n_layers, l, place):
    _, ha, B = part.shape
    tr = _row_tile(ha)

    def body(pc_ref, p_ref, r_ref, *rest):
        o_ref = rest[-1]
        o_ref[...] = ((p_ref[...].astype(F32) + r_ref[0].astype(F32)) + r_ref[1].astype(F32)) + r_ref[2].astype(F32)

    in_specs = [pl.BlockSpec((None, tr, B), lambda i, pc: (pc[0], i, 0)), pl.BlockSpec((3, tr, B), lambda i, pc: (0, i, 0))]
    ops = [part, recv]
    if gbuf is not None:
        in_specs.append(ANY)
        ops.append(gbuf)
    return pl.pallas_call(
        body, name="sum_partials",
        grid_spec=pltpu.PrefetchScalarGridSpec(
            num_scalar_prefetch=1, grid=(ha // tr,), in_specs=in_specs,
            out_specs=pl.BlockSpec((None, None, tr, B), lambda i, pc: (l, pc[1], i, 0))),
        out_shape=jax.ShapeDtypeStruct((n_layers, 2, ha, B), F32),
        input_output_aliases={} if gbuf is None else {3: 0},
        compiler_params=_cparams(1),
    )(place, *ops)


def _cast_shard(shard, l, place):
    _, _, ha, B = shard.shape
    tr = _row_tile(ha)

    def body(pc_ref, s_ref, o_ref):
        o_ref[...] = s_ref[...].astype(BF)

    return pl.pallas_call(
        body, name="cast_shard",
        grid_spec=pltpu.PrefetchScalarGridSpec(
            num_scalar_prefetch=1, grid=(2, ha // tr),
            in_specs=[pl.BlockSpec((None, None, tr, B), lambda h, i, pc: (l, h, i, 0))],
            out_specs=pl.BlockSpec((None, None, tr, B), lambda h, i, pc: (pc[0], h, i, 0))),
        out_shape=jax.ShapeDtypeStruct((4, 2, ha, B), BF),
        compiler_params=_cparams(2),
    )(place, shard)


def _gather8(v):
    R, C = v.shape

    def body(v_ref, o_ref, ssem, rsem):
        x, y, c, *_ = _place()
        me = 4 * x + 2 * y + c
        o_ref[me] = v_ref[...]
        cps = []
        for k in range(1, 8):
            fx, fy, fc = (k >> 2) & 1, (k >> 1) & 1, k & 1
            peer = (x ^ fx, y ^ fy, c ^ fc)
            cp = pltpu.make_async_remote_copy(src_ref=v_ref, dst_ref=o_ref.at[me], send_sem=ssem.at[k - 1],
                                              recv_sem=rsem.at[k - 1], device_id=peer, device_id_type=MESH)
            cp.start()
            cps.append((cp, 4 * peer[0] + 2 * peer[1] + peer[2]))
        for k, (cp, slot) in enumerate(cps):
            there = o_ref.at[slot]
            pltpu.make_async_remote_copy(src_ref=there, dst_ref=there, send_sem=ssem.at[k], recv_sem=rsem.at[k],
                                         device_id=(x, y, c), device_id_type=MESH).wait_recv()
        for cp, _ in cps:
            cp.wait_send()

    vm = pl.BlockSpec(memory_space=pltpu.VMEM)
    return pl.pallas_call(
        body, name="gather8", in_specs=[vm], out_specs=vm, out_shape=jax.ShapeDtypeStruct((8, R, C), F32),
        scratch_shapes=[pltpu.SemaphoreType.DMA((7,)), pltpu.SemaphoreType.DMA((7,))],
    )(v)


def _mods_to_owner(ms):
    _, R, C = ms.shape

    def body(m_ref, o_ref, ssem, rsem):
        x, y, c, p, chips, qs = _place()
        o_ref[p] = m_ref[2 * p + c]
        cps = []
        for j in range(3):
            cp = pltpu.make_async_remote_copy(src_ref=m_ref.at[2 * qs[j] + c], dst_ref=o_ref.at[p], send_sem=ssem.at[j],
                                              recv_sem=rsem.at[j], device_id=(*chips[j], c), device_id_type=MESH)
            cp.start()
            cps.append(cp)
        for j in range(3):
            there = o_ref.at[qs[j]]
            pltpu.make_async_remote_copy(src_ref=there, dst_ref=there, send_sem=ssem.at[j], recv_sem=rsem.at[j],
                                         device_id=(x, y, c), device_id_type=MESH).wait_recv()
        for cp in cps:
            cp.wait_send()

    vm = pl.BlockSpec(memory_space=pltpu.VMEM)
    return pl.pallas_call(
        body, name="mods_to_owner", in_specs=[vm], out_specs=vm, out_shape=jax.ShapeDtypeStruct((4, R, C), F32),
        scratch_shapes=[pltpu.SemaphoreType.DMA((3,)), pltpu.SemaphoreType.DMA((3,))],
    )(ms)


def _vec(*rows):
    D = rows[0].shape[-1]
    rows = [r.reshape(1, D) for r in rows]
    return jnp.concatenate(rows + [jnp.zeros((8 - len(rows), D), F32)], axis=0)


def _sets(n_layers, n_conv):
    sets = []
    for l in range(n_layers):
        j = l - n_conv
        mixer = [("conv_w_in", l), ("conv_w_out", l)] if l < n_conv else [("attn_w_q", j), ("attn_w_o", j)]
        sets.append([("ffn1_w_in", l), ("ffn1_w_out", l)] + ([("w_kv", 0)] if l == n_conv else []))
        sets.append(mixer + [("ffn2_w_in", l), ("ffn2_w_out", l)])
    return sets


def _as_weight(name, buf):
    _, _, ha, B = buf.shape
    return buf.reshape(8 * ha, B) if name.endswith("w_out") else buf.reshape(4, 2 * ha, B)


def _local_step(x, tgt, tabs, norm_g, conv_w, kv_norm_g, final_norm_g, mods, kvmods, cast, place):
    S, D = x.shape
    n_layers, n_conv = norm_g.shape[0], conv_w.shape[0]
    md = mods.reshape(n_layers, 9, D)

    def vec_of(l, k):
        return _vec(norm_g[l, k], md[l, 3 * k + 1], md[l, 3 * k], md[l, 3 * k + 2])

    W = {}

    def install(items, bufs):
        for (name, idx), b in zip(items, bufs):
            W[name, idx] = _as_weight(name, b)

    Q = _sets(n_layers, n_conv)
    install(Q[0], _comm_call("gather_d2d", _comm_call("gather_ici", [cast[it] for it in Q[0]])))
    arrived = {}

    def ffn_fwd(n, *args):
        ici = [k for k in ([1] if n == 0 else []) + [n + 2] if k < len(Q)]
        d2d = [n + 1] if 2 <= n + 1 < len(Q) else []
        comm = [("gather_ici", [cast[it] for it in Q[k]]) for k in ici] + [("gather_d2d", arrived.pop(k)) for k in d2d]
        xo, h, ab, y, *moved = _ffn_fwd(*args, comm)
        for k in ici:
            arrived[k], moved = moved[:len(Q[k])], moved[len(Q[k]):]
        for k in d2d:
            install(Q[k], moved[:len(Q[k])])
            moved = moved[len(Q[k]):]
        if n == 0:
            install(Q[1], _comm_call("gather_d2d", arrived.pop(1)))
        return xo, h, ab, y

    vec_kv = _vec(kv_norm_g, kvmods[D:], kvmods[:D])
    ng = len(DILATIONS)
    n_rope = ng * GROUP_W // LANES
    saved = {}
    xc = x
    ks = vs = None
    for l in range(n_layers):
        if l == n_conv:
            h_kv, kv = _proj_fwd(xc, vec_kv, W["w_kv", 0], tabs, n_rope, DILATIONS * 2)
            ks, vs = kv[:ng], kv[ng:]
            saved["kv"] = (xc, h_kv)
        v0 = vec_of(l, 0)
        xo, h, ab, y = ffn_fwd(2 * l, xc, v0, W["ffn1_w_in", l], W["ffn1_w_out", l])
        saved[l, 0] = (xc, v0, h, ab, y)
        xc = xo
        v1 = vec_of(l, 1)
        if l < n_conv:
            cw = _vec(conv_w[l, 0], conv_w[l, 1], conv_w[l, 2])
            xo, h, s4, z, y = _conv_fwd(xc, v1, cw, W["conv_w_in", l], W["conv_w_out", l])
            saved[l, 1] = (xc, v1, h, y, cw, s4, z)
        else:
            j = l - n_conv
            h, qs = _proj_fwd(xc, v1, W["attn_w_q", j], tabs, n_rope, DILATIONS, SM_SCALE)
            og = [_attn_fwd(qs[g], ks[g], vs[g], DILATIONS[g]) for g in range(ng)]
            mixs, ljs, y, xo = _attn_out([o for o, _ in og], [s for _, s in og], xc, v1, W["attn_w_o", j])
            saved[l, 1] = (xc, v1, h, y, qs, mixs, ljs)
        xc = xo
        v2 = vec_of(l, 2)
        xo, h, ab, y = ffn_fwd(2 * l + 1, xc, v2, W["ffn2_w_in", l], W["ffn2_w_out", l])
        saved[l, 2] = (xc, v2, h, ab, y)
        xc = xo

    dx, red_f = _final_loss(xc, _vec(final_norm_g), tgt)
    loss_part = 0.5 * jnp.sum(red_f[1]) / D
    G, parts, others = {}, {}, {}
    dmods = [[None] * 9 for _ in range(n_layers)]
    dnorm = [[None] * 3 for _ in range(n_layers)]
    dconvw = [None] * n_conv
    dkv_parts = []

    def halves(items):
        return [G[it].reshape(4, 2, G[it].shape[1] // 2, G[it].shape[2]) for it in items]

    def pair_sums(items, recv):
        for it, g, r in zip(items, halves(items), recv):
            parts[it] = _pair_add(g, r, place[1:])
        return [parts[it] for it in items]

    R = Q[::-1]
    recvd = {}

    def ffn_bwd(dx, l, k, name):
        m = 2 * (n_layers - 1 - l) + (0 if k == 2 else 1)
        ex = [m - 1] if m >= 1 else []
        sc = [m - 2] if m >= 2 else []
        comm = [("exchange", halves(R[e])) for e in ex] + [("scatter", pair_sums(R[e], recvd.pop(e))) for e in sc]
        xin, v, h, ab, y = saved[l, k]
        w_in, w_out = W[name + "_w_in", l], W[name + "_w_out", l]
        dy, u, dab, dh, *moved = _ffn_dgrad(dx, v, ab, w_in, w_out, comm)
        for e in ex:
            recvd[e], moved = moved[:len(R[e])], moved[len(R[e]):]
        for e in sc:
            others.update(zip(R[e], moved[:len(R[e])]))
            moved = moved[len(R[e]):]
        G[name + "_w_in", l] = _mm_tn(h, dab, w_in.shape[2])
        G[name + "_w_out", l] = _mm_tn(u, dy[None], D).reshape(4, -1, D)
        return _norm_bwd(dx, [dh], xin, y, v, 0.5)

    def note(l, k, red):
        dnorm[l][k] = red[0]
        dmods[l][3 * k], dmods[l][3 * k + 1], dmods[l][3 * k + 2] = red[2], red[1], red[3]

    for l in reversed(range(n_layers)):
        dx, red = ffn_bwd(dx, l, 2, "ffn2")
        note(l, 2, red)
        if l < n_conv:
            xin, v, h, y, cw, s4, z = saved[l, 1]
            w_in, w_out = W["conv_w_in", l], W["conv_w_out", l]
            dy, d3, dh, dcw = _conv_dgrad(dx, v, cw, s4, w_in, w_out)
            G["conv_w_in", l] = _mm_tn(h, d3[None], w_in.shape[2])
            G["conv_w_out", l] = _mm_tn(z, dy[None], D).reshape(4, -1, D)
            dconvw[l] = dcw[0:3]
        else:
            j = l - n_conv
            xin, v, h, y, qs, mixs, ljs = saved[l, 1]
            wq, wo = W["attn_w_q", j], W["attn_w_o", j]
            dy, dmixs = _attn_out_dgrad(dx, v, wo)
            G["attn_w_o", j] = _mm_tn(mixs[0], dy[None], wo.shape[2])
            dqkv = [_attn_bwd(qs[g], ks[g], vs[g], dmixs[g], mixs[g], ljs[g], DILATIONS[g]) for g in range(ng)]
            dkv_parts.append([t[1:] for t in dqkv])
            dq, dh = _proj_dgrad([[t[0]] for t in dqkv], tabs, wq, ng, DILATIONS)
            G["attn_w_q", j] = _mm_tn(h, dq[None], wq.shape[2])
        dx, red = _norm_bwd(dx, [dh], xin, y, v, 1.0)
        note(l, 1, red)
        dx, red = ffn_bwd(dx, l, 0, "ffn1")
        note(l, 0, red)
        if l == n_conv:
            xin, h_kv = saved["kv"]
            kv_parts = [[lay[g][0] for lay in dkv_parts] for g in range(ng)] + [[lay[g][1] for lay in dkv_parts] for g in range(ng)]
            dkv, dh = _proj_dgrad(kv_parts, tabs, W["w_kv", 0], ng, DILATIONS * 2)
            G["w_kv", 0] = _mm_tn(h_kv, dkv[None], W["w_kv", 0].shape[2])
            dx, red_kv = _norm_bwd(dx, [dh], xin, None, vec_kv, 1.0)
    recvd[len(R) - 1] = _comm_call("exchange", halves(R[-1]))
    tail = R[-2] + R[-1]
    others.update(zip(tail, _comm_call("scatter", pair_sums(R[-2], recvd.pop(len(R) - 2)) + pair_sums(R[-1], recvd.pop(len(R) - 1)))))
    small = jnp.concatenate(
        [jnp.stack([jnp.stack(r) for r in dmods]).reshape(-1), red_kv[2], red_kv[1],
         jnp.stack([jnp.stack(r) for r in dnorm]).reshape(-1), jnp.stack(dconvw).reshape(-1), red_kv[0], red_f[0]])
    return loss_part, dx, parts, others, small


BIG = ("ffn1_w_in", "ffn1_w_out", "ffn2_w_in", "ffn2_w_out", "conv_w_in", "conv_w_out", "w_kv", "attn_w_q", "attn_w_o")


def _halved(t):
    if t.ndim == 2:
        t = t[None]
    L, A, B = t.shape
    return t.reshape(L, 2, A // 2, B)


def kernel(x, c, positions, norm_g, ada_w, ada_b, ffn1_w_in, ffn1_w_out, ffn2_w_in, ffn2_w_out, conv_w_in, conv_w, conv_w_out, kv_norm_g, kv_ada_w, kv_ada_b, w_kv, attn_w_q, attn_w_o, final_norm_g, loss_target, m_norm_g, m_ada_w, m_ada_b, m_ffn1_w_in, m_ffn1_w_out, m_ffn2_w_in, m_ffn2_w_out, m_conv_w_in, m_conv_w, m_conv_w_out, m_kv_norm_g, m_kv_ada_w, m_kv_ada_b, m_w_kv, m_attn_w_q, m_attn_w_o, m_final_norm_g, v_norm_g, v_ada_w, v_ada_b, v_ffn1_w_in, v_ffn1_w_out, v_ffn2_w_in, v_ffn2_w_out, v_conv_w_in, v_conv_w, v_conv_w_out, v_kv_norm_g, v_kv_ada_w, v_kv_ada_b, v_w_kv, v_attn_w_q, v_attn_w_o, v_final_norm_g):
    wts = dict(norm_g=norm_g, ada_w=ada_w, ada_b=ada_b, ffn1_w_in=ffn1_w_in, ffn1_w_out=ffn1_w_out, ffn2_w_in=ffn2_w_in,
               ffn2_w_out=ffn2_w_out, conv_w_in=conv_w_in, conv_w=conv_w, conv_w_out=conv_w_out, kv_norm_g=kv_norm_g,
               kv_ada_w=kv_ada_w, kv_ada_b=kv_ada_b, w_kv=w_kv, attn_w_q=attn_w_q, attn_w_o=attn_w_o,
               final_norm_g=final_norm_g)
    ms = dict(norm_g=m_norm_g, ada_w=m_ada_w, ada_b=m_ada_b, ffn1_w_in=m_ffn1_w_in, ffn1_w_out=m_ffn1_w_out,
              ffn2_w_in=m_ffn2_w_in, ffn2_w_out=m_ffn2_w_out, conv_w_in=m_conv_w_in, conv_w=m_conv_w,
              conv_w_out=m_conv_w_out, kv_norm_g=m_kv_norm_g, kv_ada_w=m_kv_ada_w, kv_ada_b=m_kv_ada_b, w_kv=m_w_kv,
              attn_w_q=m_attn_w_q, attn_w_o=m_attn_w_o, final_norm_g=m_final_norm_g)
    vs = dict(norm_g=v_norm_g, ada_w=v_ada_w, ada_b=v_ada_b, ffn1_w_in=v_ffn1_w_in, ffn1_w_out=v_ffn1_w_out,
              ffn2_w_in=v_ffn2_w_in, ffn2_w_out=v_ffn2_w_out, conv_w_in=v_conv_w_in, conv_w=v_conv_w,
              conv_w_out=v_conv_w_out, kv_norm_g=v_kv_norm_g, kv_ada_w=v_kv_ada_w, kv_ada_b=v_kv_ada_b, w_kv=v_w_kv,
              attn_w_q=v_attn_w_q, attn_w_o=v_attn_w_o, final_norm_g=v_final_norm_g)
    order = list(wts)
    S, D = x.shape[1], x.shape[2]
    n_layers, n_conv = norm_g.shape[0], conv_w.shape[0]
    nm, sw = ada_w.shape[2], norm_g.shape[2]
    ix, iy, ic = (lax.axis_index(a) for a in AXES)
    chip = 2 * ix + iy

    pack = jnp.concatenate([norm_g.reshape(-1), conv_w.reshape(-1), c.reshape(-1)])
    npad = (-pack.size) % (8 * LANES)
    allp = _gather8(jnp.pad(pack, (0, npad)).reshape(-1, LANES)).reshape(8, -1)
    n1, n2 = norm_g.size, norm_g.size + conv_w.size
    by_chip = allp[0::2]
    norm_full = jnp.moveaxis(by_chip[:, :n1].reshape(4, n_layers, 3, sw), 0, 2).reshape(n_layers, 3, 4 * sw)
    conv_full = jnp.moveaxis(by_chip[:, n1:n2].reshape(4, n_conv, 3, sw), 0, 2).reshape(n_conv, 3, 4 * sw)
    c_all = allp[:, n2:n2 + D]

    b_l = lax.dynamic_slice_in_dim(ada_b, chip * nm, nm, axis=1)
    mod_sh = _ada_fwd(c_all, ada_w, b_l[:, None, :])
    nkv = kv_ada_w.shape[1]
    bkv = lax.dynamic_slice_in_dim(kv_ada_b, chip * nkv, nkv, axis=0)
    kv_sh = _ada_fwd(c_all, kv_ada_w[None], bkv[None, None, :])
    rows = jnp.concatenate([jnp.moveaxis(mod_sh, 0, 1), jnp.pad(jnp.moveaxis(kv_sh, 0, 1), ((0, 0), (0, 0), (0, nm - nkv)))], axis=1)
    rpad = (-rows.shape[1]) % 8
    mine = _mods_to_owner(jnp.pad(rows, ((0, 0), (0, rpad), (0, 0))))
    mods = jnp.moveaxis(mine[:, :n_layers], 0, 1).reshape(n_layers, 4 * nm)
    kvmods = mine[:, n_layers, :nkv].reshape(4 * nkv)

    place = jnp.stack([chip, ic]).astype(jnp.int32)
    halved = {k: _halved(wts[k]) for k in BIG}
    cast = {(k, l): _cast_shard(halved[k], l, place) for k in BIG for l in range(halved[k].shape[0])}

    loss_part, dx, parts, others, small = _local_step(x[0], loss_target[0], _rope_tables(positions), norm_full, conv_full,
                                                      kv_norm_g, final_norm_g, mods, kvmods, cast, place)
    loss = lax.psum(loss_part, AXES)

    gbufs = []
    for k in BIG:
        gbuf, n_l = None, halved[k].shape[0]
        for l in range(n_l):
            gbuf = _sum_partials(parts[k, l], others[k, l], gbuf, n_l, l, place)
        gbufs.append(gbuf)
    grads = {k: s.reshape(wts[k].shape) for k, s in zip(BIG, _comm_call("share", gbufs))}

    spad = (-small.size) % (8 * LANES)
    gath = _gather8(jnp.pad(small, (0, spad)).reshape(-1, LANES))
    tot = _sum8(gath).reshape(-1)
    gath = gath.reshape(8, -1)
    o = 0

    def take(n):
        nonlocal o
        o += n
        return tot[o - n:o]

    g_mods = take(n_layers * 9 * D).reshape(n_layers, 9 * D)
    g_kvmods = take(2 * D)
    g_norm = take(n_layers * 3 * D).reshape(n_layers, 3, D)
    g_convw = take(n_conv * 3 * D).reshape(n_conv, 3, D)
    g_kvn = take(D)
    g_fin = take(D)
    grads["ada_b"] = g_mods
    grads["kv_ada_b"] = g_kvmods
    grads["norm_g"] = lax.dynamic_slice_in_dim(g_norm, chip * sw, sw, axis=2)
    grads["conv_w"] = lax.dynamic_slice_in_dim(g_convw, chip * sw, sw, axis=2)
    grads["kv_norm_g"] = g_kvn
    grads["final_norm_g"] = g_fin

    dm_all = gath[:, :n_layers * 9 * D].reshape(8, n_layers, 9 * D)
    dm_mine = jnp.moveaxis(lax.dynamic_slice_in_dim(dm_all, chip * nm, nm, axis=2), 0, 1)
    dkv_all = gath[:, n_layers * 9 * D:n_layers * 9 * D + 2 * D]
    dkv_mine = lax.dynamic_slice_in_dim(dkv_all, chip * nkv, nkv, axis=1)[None]
    c_t = c_all.T
    grads["ada_w"] = _ada_wgrad(c_t, dm_mine)
    grads["kv_ada_w"] = _ada_wgrad(c_t, dkv_mine)[0]

    deltas, new_m, new_v = {}, {}, {}
    for k in order:
        shp = wts[k].shape
        two_d = (lambda t: t.reshape(1, -1)) if len(shp) == 1 else (lambda t: t)
        dlt, nmk, nvk = _adamw(two_d(wts[k]), two_d(grads[k]), two_d(ms[k]), two_d(vs[k]))
        deltas[k], new_m[k], new_v[k] = dlt.reshape(shp), nmk.reshape(shp), nvk.reshape(shp)
    return (loss, dx[None], *[grads[k] for k in order], *[deltas[k] for k in order], *[new_m[k] for k in order],
            *[new_v[k] for k in order])
```

```python
import functools
import math

import jax
import jax.numpy as jnp
from jax import lax
from jax.experimental import pallas as pl
from jax.experimental.pallas import tpu as pltpu

F32 = jnp.float32
BF = jnp.bfloat16
MESH = pl.DeviceIdType.MESH
AXES = ("x", "y", "c")

NORM_EPS = 1e-5
HEAD_DIM = 64
HEADS_PER_GROUP = 8
GROUP_W = HEADS_PER_GROUP * HEAD_DIM
DILATIONS = (1, 4, 16)
BAND = 128
ROPE_DIM = 16
ROPE_THETA = 500000.0
SM_SCALE = HEAD_DIM ** -0.5
NEG = -1e30
ADAM_LR, ADAM_B1, ADAM_B2, ADAM_EPS, ADAM_WD, ADAM_STEP = 0.001, 0.9, 0.999, 1e-08, 0.01, 10

V7X_VMEM_BYTES = 64 * 1024 * 1024
VMEM_LIMIT = V7X_VMEM_BYTES - 6 * 1024 * 1024
LANES = 128
TM = 512
TK = 1024
WGRAD_COLS = 1536


def _cparams(ngrid):
    return pltpu.CompilerParams(dimension_semantics=("arbitrary",) * ngrid, vmem_limit_bytes=VMEM_LIMIT)


def _dot(a, b):
    return jnp.dot(a, b, preferred_element_type=F32)


def _dot_nt(a, b):
    return lax.dot_general(a, b, (((1,), (1,)), ((), ())), preferred_element_type=F32)


def _dot_tn(a, b):
    return lax.dot_general(a, b, (((0,), (0,)), ((), ())), preferred_element_type=F32)


def _tile(n, pref):
    t = min(n, pref)
    while n % t:
        t //= 2
    return t


def _row_tile(n, cap=256, mult=16):
    best = n
    for t in range(mult, min(n, cap) + 1, mult):
        if n % t == 0:
            best = t
    return best


def _rstd(x):
    return lax.rsqrt(jnp.mean(x * x, axis=-1, keepdims=True) + NORM_EPS)


def _norm_mod(x, v):
    return (x * _rstd(x) * v[0:1]) * (1.0 + v[1:2]) + v[2:3]


def _silu_parts(a):
    sg = jax.nn.sigmoid(a)
    return sg, a * sg


def _row(i):
    return lambda *_: (i, 0)


ANY = pl.BlockSpec(memory_space=pl.ANY)
IN_PLACE = ("gather_ici", "gather_d2d", "share")


def _place():
    x, y, c = (lax.axis_index(a) for a in AXES)
    chips = [(1 - x, y), (x, 1 - y), (1 - x, 1 - y)]
    return x, y, c, 2 * x + y, chips, [2 * cx + cy for cx, cy in chips]


def _transfers(kind, ins, outs):
    x, y, c, p, chips, qs = _place()
    sib = (x, y, 1 - c)
    rows = []
    for k, o in enumerate(outs):
        if kind == "gather_ici":
            rows.append([(o.at[p, c], o.at[p, c], (*chips[j], c), o.at[qs[j], c]) for j in range(3)])
        elif kind == "gather_d2d":
            rows.append([(o.at[qs[j], c], o.at[qs[j], c], sib, o.at[qs[j], 1 - c]) for j in range(3)])
        elif kind == "exchange":
            rows.append([(ins[k].at[:, 1 - c], o, sib, o)])
        elif kind == "scatter":
            rows.append([(ins[k].at[qs[j]], o.at[j], (*chips[j], c), o.at[j]) for j in range(3)])
        elif kind == "share":
            rows.append([(o.at[:, c], o.at[:, c], sib, o.at[:, 1 - c])])
    return rows


def _comm_out_shapes(kind, arrays):
    if kind in IN_PLACE:
        return [jax.ShapeDtypeStruct(a.shape, a.dtype) for a in arrays]
    if kind == "exchange":
        return [jax.ShapeDtypeStruct((4,) + a.shape[2:], a.dtype) for a in arrays]
    return [jax.ShapeDtypeStruct((3,) + a.shape[1:], a.dtype) for a in arrays]


def _comm_sems(n):
    return [pltpu.SemaphoreType.DMA((n, 3)), pltpu.SemaphoreType.DMA((n, 3))]


def _comm_start(rows, ssem, rsem):
    for k, row in enumerate(rows):
        for j, (src, dst, dev, _) in enumerate(row):
            pltpu.make_async_remote_copy(src_ref=src, dst_ref=dst, send_sem=ssem.at[k, j], recv_sem=rsem.at[k, j],
                                         device_id=dev, device_id_type=MESH).start()


def _comm_wait(rows, ssem, rsem):
    for k, row in enumerate(rows):
        for j, (src, dst, dev, land) in enumerate(row):
            pltpu.make_async_remote_copy(src_ref=src, dst_ref=dst, send_sem=ssem.at[k, j], recv_sem=rsem.at[k, j],
                                         device_id=dev, device_id_type=MESH).wait_send()
            pltpu.make_async_remote_copy(src_ref=land, dst_ref=land, send_sem=ssem.at[k, j], recv_sem=rsem.at[k, j],
                                         device_id=dev, device_id_type=MESH).wait_recv()


def _comm_call(kind, arrays):
    n = len(arrays)

    def body(*refs):
        rows = _transfers(kind, refs[:n], refs[n:2 * n])
        _comm_start(rows, *refs[2 * n:])
        _comm_wait(rows, *refs[2 * n:])

    return pl.pallas_call(
        body, name="comm_" + kind, in_specs=[ANY] * n, out_specs=[ANY] * n, out_shape=_comm_out_shapes(kind, arrays),
        input_output_aliases={k: k for k in range(n)} if kind in IN_PLACE else {},
        scratch_shapes=_comm_sems(n),
    )(*arrays)


def _ride(comm, n_in, n_out):
    ride = dict(arrays=[], in_specs=[], out_specs=[], out_shape=[], aliases={}, scratch=[], n=0)
    for kind, arrays in comm or []:
        if kind in IN_PLACE:
            ride["aliases"].update({n_in + ride["n"] + k: n_out + ride["n"] + k for k in range(len(arrays))})
        ride["arrays"] += list(arrays)
        ride["out_shape"] += _comm_out_shapes(kind, arrays)
        ride["scratch"] += _comm_sems(len(arrays))
        ride["n"] += len(arrays)
    ride["in_specs"] = ride["out_specs"] = [ANY] * ride["n"]
    return ride


def _ride_when(comm, ins, outs, sems, cond, action):
    if not comm:
        return

    @pl.when(cond)
    def _():
        at = 0
        for e, (kind, arrays) in enumerate(comm):
            n = len(arrays)
            action(_transfers(kind, ins[at:at + n], outs[at:at + n]), *sems[2 * e:2 * e + 2])
            at += n


def _ffn_fwd(x, vec, w_in4, w_out, comm=None):
    S, D = x.shape
    cs = w_in4.shape[2]
    F = 2 * cs
    tm = _tile(S, TM)
    ni = S // tm
    ride = _ride(comm, 5, 4)
    nc = ride["n"]

    def body(*refs):
        x_ref, vec_ref, wa_ref, wb_ref, wo_ref = refs[:5]
        xo_ref, h_ref, ab_ref, y_ref = refs[5 + nc:9 + nc]
        acc_ref = refs[-1]
        i, j = pl.program_id(0), pl.program_id(1)
        riding = (comm, refs[5:5 + nc], refs[9 + nc:9 + 2 * nc], refs[9 + 2 * nc:-1])
        _ride_when(*riding, jnp.logical_and(i == 0, j == 0), _comm_start)

        @pl.when(j == 0)
        def _():
            h_ref[...] = _norm_mod(x_ref[...], vec_ref[...]).astype(BF)
            acc_ref[...] = jnp.zeros_like(acc_ref)

        h = h_ref[...]
        a = _dot(h, wa_ref[...])
        b = _dot(h, wb_ref[...])
        ab_ref[0] = a.astype(BF)
        ab_ref[1] = b.astype(BF)
        _, s = _silu_parts(a)
        acc_ref[...] += _dot((s * b).astype(BF), wo_ref[...])

        @pl.when(j == 1)
        def _():
            y = acc_ref[...]
            y_ref[...] = y.astype(BF)
            xo_ref[...] = x_ref[...] + (0.5 * (1.0 + vec_ref[3:4, :])) * y

        _ride_when(*riding, jnp.logical_and(i == ni - 1, j == 1), _comm_wait)

    tok = pl.BlockSpec((tm, D), lambda i, j: (i, 0))
    return pl.pallas_call(
        body, name="ffn_fwd", grid=(ni, 2),
        in_specs=[tok, pl.BlockSpec((8, D), lambda i, j: (0, 0)),
                  pl.BlockSpec((None, D, cs), lambda i, j: (j, 0, 0)),
                  pl.BlockSpec((None, D, cs), lambda i, j: (j + 2, 0, 0)),
                  pl.BlockSpec((cs, D), lambda i, j: (j, 0))] + ride["in_specs"],
        out_specs=[tok, tok, pl.BlockSpec((2, tm, cs), lambda i, j: (0, i, j)), tok] + ride["out_specs"],
        out_shape=[jax.ShapeDtypeStruct((S, D), F32), jax.ShapeDtypeStruct((S, D), BF),
                   jax.ShapeDtypeStruct((2, S, F), BF), jax.ShapeDtypeStruct((S, D), BF)] + ride["out_shape"],
        input_output_aliases=ride["aliases"], scratch_shapes=ride["scratch"] + [pltpu.VMEM((tm, D), F32)],
        compiler_params=_cparams(2),
    )(x, vec, w_in4, w_in4, w_out, *ride["arrays"])


def _norm_bwd_tile(dxo, dh, xv, y, vec_ref, red_ref, coef):
    g = vec_ref[0:1, :]
    sc = vec_ref[1:2, :]
    r = _rstd(xv)
    xh = xv * r
    dhn = dh * (1.0 + sc)
    dxh = dhn * g
    red_ref[0:1, :] += jnp.sum(dhn * xh, axis=0, keepdims=True)
    red_ref[1:2, :] += jnp.sum(dh * (xh * g), axis=0, keepdims=True)
    red_ref[2:3, :] += jnp.sum(dh, axis=0, keepdims=True)
    if y is not None:
        red_ref[3:4, :] += coef * jnp.sum(dxo * y.astype(F32), axis=0, keepdims=True)
    return dxo + r * (dxh - xh * jnp.mean(dxh * xh, axis=-1, keepdims=True))


def _ffn_dgrad(dxo, vec, ab, w_in4, w_out, comm=None):
    S, D = dxo.shape
    cs = w_in4.shape[2]
    F = 2 * cs
    tm = _tile(S, TM)
    ni = S // tm
    ride = _ride(comm, 6, 4)
    nc = ride["n"]

    def body(*refs):
        dxo_ref, vec_ref, ab_ref, wa_ref, wb_ref, wo_ref = refs[:6]
        dy_ref, u_ref, dab_ref, dh_ref = refs[6 + nc:10 + nc]
        i, j = pl.program_id(0), pl.program_id(1)
        riding = (comm, refs[6:6 + nc], refs[10 + nc:10 + 2 * nc], refs[10 + 2 * nc:])
        _ride_when(*riding, jnp.logical_and(i == 0, j == 0), _comm_start)

        @pl.when(j == 0)
        def _():
            dy_ref[...] = ((0.5 * (1.0 + vec_ref[3:4, :])) * dxo_ref[...]).astype(BF)
            dh_ref[...] = jnp.zeros_like(dh_ref)

        du = _dot_nt(dy_ref[...], wo_ref[...])
        a = ab_ref[0].astype(F32)
        b = ab_ref[1].astype(F32)
        sg, s = _silu_parts(a)
        u_ref[...] = (s * b).astype(BF)
        da = (du * b * (sg * (1.0 + a * (1.0 - sg)))).astype(BF)
        db = (du * s).astype(BF)
        dab_ref[0] = da
        dab_ref[1] = db
        dh_ref[...] += _dot_nt(da, wa_ref[...]) + _dot_nt(db, wb_ref[...])
        _ride_when(*riding, jnp.logical_and(i == ni - 1, j == 1), _comm_wait)

    tok = pl.BlockSpec((tm, D), lambda i, j: (i, 0))
    return pl.pallas_call(
        body, name="ffn_dgrad", grid=(ni, 2),
        in_specs=[tok, pl.BlockSpec((8, D), lambda i, j: (0, 0)),
                  pl.BlockSpec((2, tm, cs), lambda i, j: (0, i, j)),
                  pl.BlockSpec((None, D, cs), lambda i, j: (j, 0, 0)),
                  pl.BlockSpec((None, D, cs), lambda i, j: (j + 2, 0, 0)),
                  pl.BlockSpec((cs, D), lambda i, j: (j, 0))] + ride["in_specs"],
        out_specs=[tok, pl.BlockSpec((tm, cs), lambda i, j: (i, j)),
                   pl.BlockSpec((2, tm, cs), lambda i, j: (0, i, j)), tok] + ride["out_specs"],
        out_shape=[jax.ShapeDtypeStruct((S, D), BF), jax.ShapeDtypeStruct((S, F), BF),
                   jax.ShapeDtypeStruct((2, S, F), BF), jax.ShapeDtypeStruct((S, D), F32)] + ride["out_shape"],
        input_output_aliases=ride["aliases"], scratch_shapes=ride["scratch"],
        compiler_params=_cparams(2),
    )(dxo, vec, ab, w_in4, w_in4, w_out, *ride["arrays"])


def _mm_tn(a, b3, cs):
    K, M = a.shape
    G, _, Nb = b3.shape
    N = G * Nb
    if cs >= Nb or Nb % cs:
        ns, tn = 1, math.gcd(cs, Nb)
    else:
        ns = max(s for s in range(1, Nb // cs + 1) if (Nb // cs) % s == 0 and s * cs <= max(WGRAD_COLS, cs))
        tn = ns * cs
    tmo = M if M <= 1024 else M // 2
    tk = _tile(K, TK)

    def body(a_ref, b_ref, o_ref):
        @pl.when(pl.program_id(2) == 0)
        def _():
            o_ref[...] = jnp.zeros_like(o_ref)

        av = a_ref[...].astype(BF)
        if ns == 1:
            o_ref[...] += _dot_tn(av, b_ref[...].astype(BF))
        else:
            for s in range(ns):
                o_ref[s] += _dot_tn(av, b_ref[:, s * cs:(s + 1) * cs].astype(BF))

    if ns == 1:
        out_spec = pl.BlockSpec((None, tmo, tn), lambda m, n, k: ((n * tn) // cs, m, ((n * tn) % cs) // tn))
    else:
        out_spec = pl.BlockSpec((ns, tmo, cs), lambda m, n, k: (n, m, 0))
    return pl.pallas_call(
        body, name="wgrad_tn", grid=(M // tmo, N // tn, K // tk),
        in_specs=[pl.BlockSpec((tk, tmo), lambda m, n, k: (k, m)),
                  pl.BlockSpec((None, tk, tn), lambda m, n, k: ((n * tn) // Nb, k, ((n * tn) % Nb) // tn))],
        out_specs=out_spec,
        out_shape=jax.ShapeDtypeStruct((N // cs, M, cs), F32),
        compiler_params=_cparams(3),
    )(a, b3)


def _norm_bwd(dxo, dhs, x, y, vec, coef):
    S, D = x.shape
    tm = _tile(S, TM)
    nh = len(dhs)
    has_y = y is not None

    def body(*refs):
        dxo_ref = refs[0]
        dh_refs = refs[1:1 + nh]
        x_ref = refs[1 + nh]
        y_ref = refs[2 + nh] if has_y else None
        vec_ref, dx_ref, red_ref = refs[-3:]

        @pl.when(pl.program_id(0) == 0)
        def _():
            red_ref[...] = jnp.zeros_like(red_ref)

        dh = dh_refs[0][...]
        for r in dh_refs[1:]:
            dh = dh + r[...]
        dx_ref[...] = _norm_bwd_tile(dxo_ref[...], dh, x_ref[...], y_ref[...] if has_y else None, vec_ref, red_ref, coef)

    tok = pl.BlockSpec((tm, D), lambda i: (i, 0))
    small = pl.BlockSpec((8, D), lambda i: (0, 0))
    ops = [dxo, *dhs, x] + ([y] if has_y else []) + [vec]
    return pl.pallas_call(
        body, name="norm_bwd", grid=(S // tm,),
        in_specs=[tok] * (len(ops) - 1) + [small],
        out_specs=[tok, small],
        out_shape=[jax.ShapeDtypeStruct((S, D), F32), jax.ShapeDtypeStruct((8, D), F32)],
        compiler_params=_cparams(1),
    )(*ops)


def _conv_fwd(x, vec, cw, w_in4, w_out):
    S, D = x.shape
    cs = w_in4.shape[2]
    tm = _tile(S, TM)

    def body(x_ref, vec_ref, cw_ref, wi_ref, wo_ref, xo_ref, h_ref, s4_ref, z_ref, y_ref, vs_ref):
        @pl.when(pl.program_id(0) == 0)
        def _():
            vs_ref[0:8, :] = jnp.zeros((8, D), F32)

        xv = x_ref[...]
        h = _norm_mod(xv, vec_ref[...]).astype(BF)
        h_ref[...] = h
        bcu = jnp.concatenate([_dot(h, wi_ref[q]) for q in range(4)], axis=1)
        bg, cg, u = bcu[:, :D], bcu[:, D:2 * D], bcu[:, 2 * D:]
        v = cg * u
        vs_ref[8:8 + tm, :] = v
        conv = cw_ref[0:1, :] * vs_ref[pl.ds(6, tm), :] + cw_ref[1:2, :] * vs_ref[pl.ds(7, tm), :] + cw_ref[2:3, :] * v
        vs_ref[0:8, :] = vs_ref[tm:tm + 8, :]
        z = (bg * conv).astype(BF)
        s4_ref[0] = bg.astype(BF)
        s4_ref[1] = cg.astype(BF)
        s4_ref[2] = u.astype(BF)
        s4_ref[3] = conv.astype(BF)
        z_ref[...] = z
        y = _dot(z, wo_ref[...])
        y_ref[...] = y.astype(BF)
        xo_ref[...] = xv + (1.0 + vec_ref[3:4, :]) * y

    tok = pl.BlockSpec((tm, D), lambda i: (i, 0))
    small = pl.BlockSpec((8, D), lambda i: (0, 0))
    return pl.pallas_call(
        body, name="conv_fwd", grid=(S // tm,),
        in_specs=[tok, small, small, pl.BlockSpec((4, D, cs), lambda i: (0, 0, 0)),
                  pl.BlockSpec((D, D), lambda i: (0, 0))],
        out_specs=[tok, tok, pl.BlockSpec((4, tm, D), lambda i: (0, i, 0)), tok, tok],
        out_shape=[jax.ShapeDtypeStruct((S, D), F32), jax.ShapeDtypeStruct((S, D), BF),
                   jax.ShapeDtypeStruct((4, S, D), BF), jax.ShapeDtypeStruct((S, D), BF),
                   jax.ShapeDtypeStruct((S, D), BF)],
        scratch_shapes=[pltpu.VMEM((tm + 8, D), F32)],
        compiler_params=_cparams(1),
    )(x, vec, cw, w_in4, w_out)


def _conv_dgrad(dxo, vec, cw, s4, w_in4, w_out):
    S, D = dxo.shape
    cs = w_in4.shape[2]
    tm = _tile(S, TM)
    nt = S // tm

    def body(dxo_ref, vec_ref, cw_ref, s4_ref, wi_ref, wo_ref, dy_ref, d3_ref, dh_ref, dcw_ref, ds_ref):
        @pl.when(pl.program_id(0) == 0)
        def _():
            ds_ref[tm:tm + 8, :] = jnp.zeros((8, D), F32)
            dcw_ref[...] = jnp.zeros_like(dcw_ref)

        dy = ((1.0 + vec_ref[3:4, :]) * dxo_ref[...]).astype(BF)
        dy_ref[...] = dy
        dz = _dot_nt(dy, wo_ref[...])
        bg = s4_ref[0].astype(F32)
        cg = s4_ref[1].astype(F32)
        u = s4_ref[2].astype(F32)
        conv = s4_ref[3].astype(F32)
        dbg = dz * conv
        dconv = dz * bg
        ds_ref[0:tm, :] = dconv
        d1 = ds_ref[pl.ds(1, tm), :]
        d2 = ds_ref[pl.ds(2, tm), :]
        ds_ref[tm:tm + 8, :] = ds_ref[0:8, :]
        dv = cw_ref[2:3, :] * dconv + cw_ref[1:2, :] * d1 + cw_ref[0:1, :] * d2
        v = cg * u
        dcw_ref[0:1, :] += jnp.sum(d2 * v, axis=0, keepdims=True)
        dcw_ref[1:2, :] += jnp.sum(d1 * v, axis=0, keepdims=True)
        dcw_ref[2:3, :] += jnp.sum(dconv * v, axis=0, keepdims=True)
        dbcu = jnp.concatenate([dbg, dv * u, dv * cg], axis=1).astype(BF)
        d3_ref[...] = dbcu
        dh = _dot_nt(dbcu[:, 0:cs], wi_ref[0])
        for q in range(1, 4):
            dh = dh + _dot_nt(dbcu[:, q * cs:(q + 1) * cs], wi_ref[q])
        dh_ref[...] = dh

    tok = pl.BlockSpec((tm, D), lambda i: (nt - 1 - i, 0))
    small = pl.BlockSpec((8, D), lambda i: (0, 0))
    return pl.pallas_call(
        body, name="conv_dgrad", grid=(nt,),
        in_specs=[tok, small, small, pl.BlockSpec((4, tm, D), lambda i: (0, nt - 1 - i, 0)),
                  pl.BlockSpec((4, D, cs), lambda i: (0, 0, 0)), pl.BlockSpec((D, D), lambda i: (0, 0))],
        out_specs=[tok, pl.BlockSpec((tm, 3 * D), lambda i: (nt - 1 - i, 0)), tok, small],
        out_shape=[jax.ShapeDtypeStruct((S, D), BF), jax.ShapeDtypeStruct((S, 3 * D), BF),
                   jax.ShapeDtypeStruct((S, D), F32), jax.ShapeDtypeStruct((8, D), F32)],
        scratch_shapes=[pltpu.VMEM((tm + 8, D), F32)],
        compiler_params=_cparams(1),
    )(dxo, vec, cw, s4, w_in4, w_out)


def _rope_tables(positions):
    S = positions.shape[-1]
    inv = ROPE_THETA ** (-jnp.arange(0, ROPE_DIM, 2, dtype=F32) / ROPE_DIM)
    ang = positions.reshape(S, 1).astype(F32) * inv
    cos = jnp.tile(jnp.cos(ang), (1, LANES // 8))
    sin = jnp.tile(jnp.sin(ang), (1, LANES // 8))
    l64 = jnp.arange(LANES) % HEAD_DIM
    return jnp.stack([jnp.where(l64 < ROPE_DIM, cos, 1.0),
                      jnp.where(l64 < ROPE_DIM // 2, -sin, 0.0),
                      jnp.where((l64 >= ROPE_DIM // 2) & (l64 < ROPE_DIM), sin, 0.0)])


def _rope(t, tab_ref):
    return t * tab_ref[0] + pltpu.roll(t, LANES - 8, 1) * tab_ref[1] + pltpu.roll(t, 8, 1) * tab_ref[2]


def _rope_t(d, tab_ref):
    return d * tab_ref[0] + pltpu.roll(d * tab_ref[1], 8, 1) + pltpu.roll(d * tab_ref[2], LANES - 8, 1)


GROUP_CH = GROUP_W // LANES


def _to_dilated(src_ref, c0, dst_ref, d):
    n = src_ref.shape[1]
    for r in range(d):
        for ch in range(GROUP_CH):
            rows = src_ref[c0 + ch] if d == 1 else src_ref.at[c0 + ch][pl.ds(r, n // d, stride=d), :]
            dst_ref[:, r * GROUP_W + ch * LANES:r * GROUP_W + (ch + 1) * LANES] = rows.astype(dst_ref.dtype)


def _from_dilated(val, dst_ref, c0, d):
    n = dst_ref.shape[1]
    for r in range(d):
        for ch in range(GROUP_CH):
            cols = val[:, r * GROUP_W + ch * LANES:r * GROUP_W + (ch + 1) * LANES]
            if d == 1:
                dst_ref[c0 + ch] = cols
            else:
                dst_ref.at[c0 + ch][pl.ds(r, n // d, stride=d), :] = cols


def _group(ref, c0):
    return jnp.concatenate([ref[c0 + ch] for ch in range(GROUP_CH)], axis=1)


def _put_group(ref, c0, val):
    for ch in range(GROUP_CH):
        ref[c0 + ch] = val[:, ch * LANES:(ch + 1) * LANES]


def _dil_spec(tm, d):
    return pl.BlockSpec((tm // d, d * GROUP_W), lambda i: (i, 0))


def _proj_fwd(x, vec, w4, tabs, n_rope, dils, scale=1.0):
    S, D = x.shape
    cs = w4.shape[2]
    N = 4 * cs
    tm = _tile(S, TM)

    def body(x_ref, vec_ref, w_ref, tab_ref, h_ref, *rest):
        outs, acc_ref = rest[:-1], rest[-1]
        h = _norm_mod(x_ref[...], vec_ref[...]).astype(BF)
        h_ref[...] = h
        per = cs // LANES
        for q in range(4):
            acc = _dot(h, w_ref[q])
            if scale != 1.0:
                acc = acc * scale
            cols = [acc[:, ch * LANES:(ch + 1) * LANES] for ch in range(per)]
            for ch, t in enumerate(cols):
                acc_ref[q * per + ch] = _rope(t, tab_ref) if q * per + ch < n_rope else t
        for j, d in enumerate(dils):
            _to_dilated(acc_ref, j * GROUP_CH, outs[j], d)

    h, *outs = pl.pallas_call(
        body, name="proj_fwd", grid=(S // tm,),
        in_specs=[pl.BlockSpec((tm, D), lambda i: (i, 0)), pl.BlockSpec((8, D), lambda i: (0, 0)),
                  pl.BlockSpec((4, D, cs), lambda i: (0, 0, 0)), pl.BlockSpec((3, tm, LANES), lambda i: (0, i, 0))],
        out_specs=[pl.BlockSpec((tm, D), lambda i: (i, 0))] + [_dil_spec(tm, d) for d in dils],
        out_shape=[jax.ShapeDtypeStruct((S, D), BF)] + [jax.ShapeDtypeStruct((S // d, d * GROUP_W), BF) for d in dils],
        scratch_shapes=[pltpu.VMEM((N // LANES, tm, LANES), F32)],
        compiler_params=_cparams(1),
    )(x, vec, w4, tabs)
    return h, outs


def _proj_dgrad(parts, tabs, w4, n_rope_groups, dils):
    cs, D = w4.shape[2], w4.shape[1]
    N = 4 * cs
    S = parts[0][0].shape[0] * dils[0]
    tm = _tile(S, TM)
    flat = [p for grp in parts for p in grp]
    counts = [len(grp) for grp in parts]

    def body(*refs):
        prefs = refs[:len(flat)]
        tab_ref, w_ref, dz_ref, dh_ref, z_ref = refs[len(flat):]
        k = 0
        for j, cnt in enumerate(counts):
            z = prefs[k][...]
            for r in prefs[k + 1:k + cnt]:
                z = z + r[...]
            k += cnt
            _from_dilated(z, z_ref, 0, dils[j])
            if j < n_rope_groups:
                z = jnp.concatenate([_rope_t(z_ref[ch], tab_ref) for ch in range(GROUP_CH)], axis=1)
            else:
                z = _group(z_ref, 0)
            dz_ref[:, j * GROUP_W:(j + 1) * GROUP_W] = z.astype(BF)
        dh = _dot_nt(dz_ref[:, 0:cs], w_ref[0])
        for q in range(1, 4):
            dh = dh + _dot_nt(dz_ref[:, q * cs:(q + 1) * cs], w_ref[q])
        dh_ref[...] = dh

    return pl.pallas_call(
        body, name="proj_dgrad", grid=(S // tm,),
        in_specs=[_dil_spec(tm, d) for d, cnt in zip(dils, counts) for _ in range(cnt)]
        + [pl.BlockSpec((3, tm, LANES), lambda i: (0, i, 0)), pl.BlockSpec((4, D, cs), lambda i: (0, 0, 0))],
        out_specs=[pl.BlockSpec((tm, N), lambda i: (i, 0)), pl.BlockSpec((tm, D), lambda i: (i, 0))],
        out_shape=[jax.ShapeDtypeStruct((S, N), BF), jax.ShapeDtypeStruct((S, D), F32)],
        scratch_shapes=[pltpu.VMEM((GROUP_CH, tm, LANES), F32)],
        compiler_params=_cparams(1),
    )(*flat, tabs, w4)


def _band_masks():
    qi = lax.broadcasted_iota(jnp.int32, (BAND, BAND), 0)
    kj = lax.broadcasted_iota(jnp.int32, (BAND, BAND), 1)
    return kj <= qi, kj >= qi


def _head(ref, h):
    return ref[:, h * HEAD_DIM:(h + 1) * HEAD_DIM]


def _attn_fwd(q, k, v, d):
    sd = q.shape[0]
    nb = sd // BAND

    def body(q_ref, kp_ref, kc_ref, vp_ref, vc_ref, o_ref, l_ref):
        same, prev = _band_masks()
        prev = jnp.logical_and(prev, pl.program_id(1) > 0)
        heads = range(HEADS_PER_GROUP)
        scores = [(jnp.where(same, _dot_nt(_head(q_ref, h), _head(kc_ref, h)), NEG),
                   jnp.where(prev, _dot_nt(_head(q_ref, h), _head(kp_ref, h)), NEG)) for h in heads]
        probs = []
        for sc, sp in scores:
            m = jnp.maximum(jnp.max(sc, axis=-1, keepdims=True), jnp.max(sp, axis=-1, keepdims=True))
            pc = jnp.exp(sc - m)
            pp = jnp.exp(sp - m)
            den = jnp.sum(pc, axis=-1, keepdims=True) + jnp.sum(pp, axis=-1, keepdims=True)
            probs.append((pc.astype(BF), pp.astype(BF), m, den))
        outs = [(_dot(pc, _head(vc_ref, h)) + _dot(pp, _head(vp_ref, h))) / den for h, (pc, pp, _, den) in zip(heads, probs)]
        o_ref[...] = jnp.concatenate(outs, axis=1)
        l_ref[...] = jnp.concatenate([jnp.broadcast_to(m + jnp.log(den), (BAND, HEAD_DIM)) for _, _, m, den in probs], axis=1)

    blk = (BAND, GROUP_W)
    cur = pl.BlockSpec(blk, lambda r, i: (i, r))
    prv = pl.BlockSpec(blk, lambda r, i: (jnp.maximum(i - 1, 0), r))
    return pl.pallas_call(
        body, name=f"attn_fwd_d{d}", grid=(d, nb),
        in_specs=[cur, prv, cur, prv, cur], out_specs=[cur, cur],
        out_shape=[jax.ShapeDtypeStruct((sd, d * GROUP_W), F32)] * 2,
        compiler_params=_cparams(2),
    )(q, k, k, v, v)


def _attn_out(os, ls, x, vec, wo4):
    S, D = x.shape
    cs = wo4.shape[2]
    tm = _tile(S, TM)
    ng = len(DILATIONS)

    def body(*refs):
        o_refs, l_refs = refs[0:2 * ng:2], refs[1:2 * ng:2]
        x_ref, vec_ref, wo_ref = refs[2 * ng:2 * ng + 3]
        outs = refs[2 * ng + 3:]
        mix_refs, lj_refs = outs[0:2 * ng:2], outs[1:2 * ng:2]
        y_ref, xo_ref, nat_ref = outs[2 * ng:]
        ov, lv = [], []
        for g in range(ng):
            _from_dilated(o_refs[g][...], nat_ref, 2 * g * GROUP_CH, DILATIONS[g])
            _from_dilated(l_refs[g][...], nat_ref, (2 * g + 1) * GROUP_CH, DILATIONS[g])
            ov.append(_group(nat_ref, 2 * g * GROUP_CH))
            lv.append(_group(nat_ref, (2 * g + 1) * GROUP_CH))
        mx = jnp.maximum(jnp.maximum(lv[0], lv[1]), lv[2])
        es = [jnp.exp(t - mx) for t in lv]
        den = es[0] + es[1] + es[2]
        mix = (es[0] * ov[0] + es[1] * ov[1] + es[2] * ov[2]) / den
        at_mix, at_lj = 2 * ng * GROUP_CH, (2 * ng + 1) * GROUP_CH
        _put_group(nat_ref, at_mix, mix)
        _put_group(nat_ref, at_lj, mx + jnp.log(den))
        for g in range(ng):
            _to_dilated(nat_ref, at_mix, mix_refs[g], DILATIONS[g])
            _to_dilated(nat_ref, at_lj, lj_refs[g], DILATIONS[g])
        mb = mix.astype(BF)
        y = jnp.concatenate([_dot(mb, wo_ref[q]) for q in range(4)], axis=1)
        y_ref[...] = y.astype(BF)
        xo_ref[...] = x_ref[...] + (1.0 + vec_ref[3:4, :]) * y

    tok = pl.BlockSpec((tm, D), lambda i: (i, 0))
    dil = [_dil_spec(tm, d) for d in DILATIONS for _ in range(2)]
    dil_shape = [jax.ShapeDtypeStruct((S // d, d * GROUP_W), F32) for d in DILATIONS for _ in range(2)]
    outs = pl.pallas_call(
        body, name="attn_out", grid=(S // tm,),
        in_specs=dil + [tok, pl.BlockSpec((8, D), lambda i: (0, 0)), pl.BlockSpec((4, GROUP_W, cs), lambda i: (0, 0, 0))],
        out_specs=dil + [tok, tok],
        out_shape=dil_shape + [jax.ShapeDtypeStruct((S, D), BF), jax.ShapeDtypeStruct((S, D), F32)],
        scratch_shapes=[pltpu.VMEM(((2 * ng + 2) * GROUP_CH, tm, LANES), F32)],
        compiler_params=_cparams(1),
    )(*[t for pair in zip(os, ls) for t in pair], x, vec, wo4)
    return list(outs[0:2 * ng:2]), list(outs[1:2 * ng:2]), outs[2 * ng], outs[2 * ng + 1]


def _attn_out_dgrad(dxo, vec, wo4):
    S, D = dxo.shape
    cs = wo4.shape[2]
    tm = _tile(S, TM)

    def body(dxo_ref, vec_ref, wo_ref, dy_ref, *rest):
        dm_refs, nat_ref = rest[:-1], rest[-1]
        dy = ((1.0 + vec_ref[3:4, :]) * dxo_ref[...]).astype(BF)
        dy_ref[...] = dy
        dm = _dot_nt(dy[:, 0:cs], wo_ref[0])
        for q in range(1, 4):
            dm = dm + _dot_nt(dy[:, q * cs:(q + 1) * cs], wo_ref[q])
        _put_group(nat_ref, 0, dm)
        for g, d in enumerate(DILATIONS):
            _to_dilated(nat_ref, 0, dm_refs[g], d)

    tok = pl.BlockSpec((tm, D), lambda i: (i, 0))
    dy, *dms = pl.pallas_call(
        body, name="attn_out_dgrad", grid=(S // tm,),
        in_specs=[tok, pl.BlockSpec((8, D), lambda i: (0, 0)), pl.BlockSpec((4, GROUP_W, cs), lambda i: (0, 0, 0))],
        out_specs=[tok] + [_dil_spec(tm, d) for d in DILATIONS],
        out_shape=[jax.ShapeDtypeStruct((S, D), BF)] + [jax.ShapeDtypeStruct((S // d, d * GROUP_W), F32) for d in DILATIONS],
        scratch_shapes=[pltpu.VMEM((GROUP_CH, tm, LANES), F32)],
        compiler_params=_cparams(1),
    )(dxo, vec, wo4)
    return dy, dms


def _attn_bwd(q, k, v, dmix, mix, lj, d):
    sd = q.shape[0]
    nb = sd // BAND

    def body(q_ref, kp_ref, kc_ref, vp_ref, vc_ref, do_ref, o_ref, l_ref, dq_ref, dk_ref, dv_ref, keep_k, keep_v):
        i = pl.program_id(1)

        @pl.when(i == 0)
        def _():
            keep_k[...] = jnp.zeros_like(keep_k)
            keep_v[...] = jnp.zeros_like(keep_v)

        @pl.when(i < nb)
        def _():
            same, prev = _band_masks()
            prev = jnp.logical_and(prev, i > 0)
            heads = range(HEADS_PER_GROUP)
            dob = [_head(do_ref, h).astype(BF) for h in heads]
            raw = [(_dot_nt(_head(q_ref, h), _head(kc_ref, h)), _dot_nt(_head(q_ref, h), _head(kp_ref, h)),
                    _dot_nt(dob[h], _head(vc_ref, h)), _dot_nt(dob[h], _head(vp_ref, h))) for h in heads]
            mid = []
            for h, (sc, sp, dpc, dpp) in zip(heads, raw):
                lrow = l_ref[:, h * HEAD_DIM:h * HEAD_DIM + 1]
                delta = jnp.sum(_head(do_ref, h) * _head(o_ref, h), axis=-1, keepdims=True)
                pc = jnp.exp(jnp.where(same, sc, NEG) - lrow)
                pp = jnp.exp(jnp.where(prev, sp, NEG) - lrow)
                mid.append((pc.astype(BF), pp.astype(BF), (pc * (dpc - delta)).astype(BF), (pp * (dpp - delta)).astype(BF)))
            dq = [(_dot(dsc, _head(kc_ref, h)) + _dot(dsp, _head(kp_ref, h))) * SM_SCALE for h, (_, _, dsc, dsp) in zip(heads, mid)]
            dk_prev = [_dot_tn(dsp, _head(q_ref, h)) for h, (_, _, _, dsp) in zip(heads, mid)]
            dv_prev = [_dot_tn(pp, dob[h]) for h, (_, pp, _, _) in zip(heads, mid)]
            dk_same = [_dot_tn(dsc, _head(q_ref, h)) for h, (_, _, dsc, _) in zip(heads, mid)]
            dv_same = [_dot_tn(pc, dob[h]) for h, (pc, _, _, _) in zip(heads, mid)]
            dq_ref[...] = jnp.concatenate(dq, axis=1)
            dk_ref[...] = keep_k[...] + jnp.concatenate(dk_prev, axis=1)
            dv_ref[...] = keep_v[...] + jnp.concatenate(dv_prev, axis=1)
            keep_k[...] = jnp.concatenate(dk_same, axis=1)
            keep_v[...] = jnp.concatenate(dv_same, axis=1)

        @pl.when(i == nb)
        def _():
            dk_ref[...] = keep_k[...]
            dv_ref[...] = keep_v[...]

    blk = (BAND, GROUP_W)
    cur = pl.BlockSpec(blk, lambda r, i: (jnp.minimum(i, nb - 1), r))
    prv = pl.BlockSpec(blk, lambda r, i: (jnp.clip(i - 1, 0, nb - 1), r))
    return pl.pallas_call(
        body, name=f"attn_bwd_d{d}", grid=(d, nb + 1),
        in_specs=[cur, prv, cur, prv, cur, cur, cur, cur], out_specs=[cur, prv, prv],
        out_shape=[jax.ShapeDtypeStruct((sd, d * GROUP_W), F32)] * 3,
        scratch_shapes=[pltpu.VMEM(blk, F32), pltpu.VMEM(blk, F32)],
        compiler_params=_cparams(2),
    )(q, k, k, v, v, dmix, mix, lj)


def _final_loss(x, gvec, tgt):
    S, D = x.shape
    tm = _tile(S, TM)

    def body(x_ref, g_ref, t_ref, dx_ref, red_ref):
        @pl.when(pl.program_id(0) == 0)
        def _():
            red_ref[...] = jnp.zeros_like(red_ref)

        xv = x_ref[...]
        g = g_ref[0:1, :]
        r = _rstd(xv)
        xh = xv * r
        err = xh * g - t_ref[...]
        dy = err * (1.0 / D)
        dxh = dy * g
        dx_ref[...] = r * (dxh - xh * jnp.mean(dxh * xh, axis=-1, keepdims=True))
        red_ref[0:1, :] += jnp.sum(dy * xh, axis=0, keepdims=True)
        red_ref[1:2, :] += jnp.sum(err * err, axis=0, keepdims=True)

    tok = pl.BlockSpec((tm, D), lambda i: (i, 0))
    small = pl.BlockSpec((8, D), lambda i: (0, 0))
    return pl.pallas_call(
        body, name="final_loss", grid=(S // tm,),
        in_specs=[tok, small, tok], out_specs=[tok, small],
        out_shape=[jax.ShapeDtypeStruct((S, D), F32), jax.ShapeDtypeStruct((8, D), F32)],
        compiler_params=_cparams(1),
    )(x, gvec, tgt)


def _adamw(w, g, m, v):
    shape = w.shape
    C = shape[-1]
    R = w.size // C
    tr = 256 if R % 256 == 0 else R

    def body(w_ref, g_ref, m_ref, v_ref, d_ref, nm_ref, nv_ref):
        gv = g_ref[...]
        nm = ADAM_B1 * m_ref[...] + (1.0 - ADAM_B1) * gv
        nv = ADAM_B2 * v_ref[...] + (1.0 - ADAM_B2) * (gv * gv)
        m_hat = nm / (1.0 - ADAM_B1 ** ADAM_STEP)
        v_hat = nv / (1.0 - ADAM_B2 ** ADAM_STEP)
        d_ref[...] = -ADAM_LR * (m_hat / (jnp.sqrt(v_hat) + ADAM_EPS) + ADAM_WD * w_ref[...])
        nm_ref[...] = nm
        nv_ref[...] = nv

    spec = pl.BlockSpec((tr, C), lambda i: (i, 0))
    outs = pl.pallas_call(
        body, name="adamw", grid=(R // tr,),
        in_specs=[spec] * 4, out_specs=[spec] * 3,
        out_shape=[jax.ShapeDtypeStruct((R, C), F32)] * 3,
        compiler_params=_cparams(1),
    )(*(t.reshape(R, C) for t in (w, g, m, v)))
    return tuple(o.reshape(shape) for o in outs)


def _ada_fwd(c_all, w, b):
    L, D, N = w.shape
    tn = 768 if N % 768 == 0 else _tile(N, 512)

    def body(c_ref, w_ref, b_ref, o_ref):
        cv = c_ref[...]
        cond = (cv * jax.nn.sigmoid(cv)).astype(BF)
        o_ref[...] = _dot(cond, w_ref[...].astype(BF)) + b_ref[...]

    return pl.pallas_call(
        body, name="ada_fwd", grid=(L, N // tn),
        in_specs=[pl.BlockSpec((8, D), lambda l, n: (0, 0)), pl.BlockSpec((None, D, tn), lambda l, n: (l, 0, n)),
                  pl.BlockSpec((None, 1, tn), lambda l, n: (l, 0, n))],
        out_specs=pl.BlockSpec((None, 8, tn), lambda l, n: (l, 0, n)),
        out_shape=jax.ShapeDtypeStruct((L, 8, N), F32),
        compiler_params=_cparams(2),
    )(c_all, w, b)


def _ada_wgrad(c_all_t, dm):
    D = c_all_t.shape[0]
    L, _, N = dm.shape
    tn = 256

    def body(c_ref, dm_ref, o_ref):
        cv = c_ref[...]
        cond = cv * jax.nn.sigmoid(cv)
        acc = cond[:, 0:1] * dm_ref[0:1, :]
        for b in range(1, 8):
            acc = acc + cond[:, b:b + 1] * dm_ref[b:b + 1, :]
        o_ref[...] = acc

    return pl.pallas_call(
        body, name="ada_wgrad", grid=(L, N // tn),
        in_specs=[pl.BlockSpec((D, 8), lambda l, n: (0, 0)), pl.BlockSpec((None, 8, tn), lambda l, n: (l, 0, n))],
        out_specs=pl.BlockSpec((None, D, tn), lambda l, n: (l, 0, n)),
        out_shape=jax.ShapeDtypeStruct((L, D, N), F32),
        compiler_params=_cparams(2),
    )(c_all_t, dm)


def _sum8(g):
    _, R, C = g.shape

    def body(g_ref, o_ref):
        acc = g_ref[0]
        for b in range(1, 8):
            acc = acc + g_ref[b]
        o_ref[...] = acc

    return pl.pallas_call(body, name="sum8", out_shape=jax.ShapeDtypeStruct((R, C), F32),
                          in_specs=[pl.BlockSpec(memory_space=pltpu.VMEM)],
                          out_specs=pl.BlockSpec(memory_space=pltpu.VMEM))(g)


def _pair_add(g4, recv, cidx):
    _, _, ha, B = g4.shape
    tr = _row_tile(ha)

    def body(c_ref, g_ref, r_ref, o_ref):
        o_ref[...] = (g_ref[...] + r_ref[...]).astype(BF)

    return pl.pallas_call(
        body, name="pair_add",
        grid_spec=pltpu.PrefetchScalarGridSpec(
            num_scalar_prefetch=1, grid=(4, ha // tr),
            in_specs=[pl.BlockSpec((None, None, tr, B), lambda s, i, c: (s, c[0], i, 0)),
                      pl.BlockSpec((None, tr, B), lambda s, i, c: (s, i, 0))],
            out_specs=pl.BlockSpec((None, tr, B), lambda s, i, c: (s, i, 0))),
        out_shape=jax.ShapeDtypeStruct((4, ha, B), BF),
        compiler_params=_cparams(2),
    )(cidx, g4, recv)


def _sum_partials(part, recv, gbuf, n_layers, l, place):
    _, ha, B = part.shape
    tr = _row_tile(ha)

    def body(pc_ref, p_ref, r_ref, *rest):
        o_ref = rest[-1]
        o_ref[...] = ((p_ref[...].astype(F32) + r_ref[0].astype(F32)) + r_ref[1].astype(F32)) + r_ref[2].astype(F32)

    in_specs = [pl.BlockSpec((None, tr, B), lambda i, pc: (pc[0], i, 0)), pl.BlockSpec((3, tr, B), lambda i, pc: (0, i, 0))]
    ops = [part, recv]
    if gbuf is not None:
        in_specs.append(ANY)
        ops.append(gbuf)
    return pl.pallas_call(
        body, name="sum_partials",
        grid_spec=pltpu.PrefetchScalarGridSpec(
            num_scalar_prefetch=1, grid=(ha // tr,), in_specs=in_specs,
            out_specs=pl.BlockSpec((None, None, tr, B), lambda i, pc: (l, pc[1], i, 0))),
        out_shape=jax.ShapeDtypeStruct((n_layers, 2, ha, B), F32),
        input_output_aliases={} if gbuf is None else {3: 0},
        compiler_params=_cparams(1),
    )(place, *ops)


def _cast_shard(shard, l, place):
    _, _, ha, B = shard.shape
    tr = _row_tile(ha)

    def body(pc_ref, s_ref, o_ref):
        o_ref[...] = s_ref[...].astype(BF)

    return pl.pallas_call(
        body, name="cast_shard",
        grid_spec=pltpu.PrefetchScalarGridSpec(
            num_scalar_prefetch=1, grid=(2, ha // tr),
            in_specs=[pl.BlockSpec((None, None, tr, B), lambda h, i, pc: (l, h, i, 0))],
            out_specs=pl.BlockSpec((None, None, tr, B), lambda h, i, pc: (pc[0], h, i, 0))),
        out_shape=jax.ShapeDtypeStruct((4, 2, ha, B), BF),
        compiler_params=_cparams(2),
    )(place, shard)


def _gather8(v):
    R, C = v.shape

    def body(v_ref, o_ref, ssem, rsem):
        x, y, c, *_ = _place()
        me = 4 * x + 2 * y + c
        o_ref[me] = v_ref[...]
        cps = []
        for k in range(1, 8):
            fx, fy, fc = (k >> 2) & 1, (k >> 1) & 1, k & 1
            peer = (x ^ fx, y ^ fy, c ^ fc)
            cp = pltpu.make_async_remote_copy(src_ref=v_ref, dst_ref=o_ref.at[me], send_sem=ssem.at[k - 1],
                                              recv_sem=rsem.at[k - 1], device_id=peer, device_id_type=MESH)
            cp.start()
            cps.append((cp, 4 * peer[0] + 2 * peer[1] + peer[2]))
        for k, (cp, slot) in enumerate(cps):
            there = o_ref.at[slot]
            pltpu.make_async_remote_copy(src_ref=there, dst_ref=there, send_sem=ssem.at[k], recv_sem=rsem.at[k],
                                         device_id=(x, y, c), device_id_type=MESH).wait_recv()
        for cp, _ in cps:
            cp.wait_send()

    vm = pl.BlockSpec(memory_space=pltpu.VMEM)
    return pl.pallas_call(
        body, name="gather8", in_specs=[vm], out_specs=vm, out_shape=jax.ShapeDtypeStruct((8, R, C), F32),
        scratch_shapes=[pltpu.SemaphoreType.DMA((7,)), pltpu.SemaphoreType.DMA((7,))],
    )(v)


def _mods_to_owner(ms):
    _, R, C = ms.shape

    def body(m_ref, o_ref, ssem, rsem):
        x, y, c, p, chips, qs = _place()
        o_ref[p] = m_ref[2 * p + c]
        cps = []
        for j in range(3):
            cp = pltpu.make_async_remote_copy(src_ref=m_ref.at[2 * qs[j] + c], dst_ref=o_ref.at[p], send_sem=ssem.at[j],
                                              recv_sem=rsem.at[j], device_id=(*chips[j], c), device_id_type=MESH)
            cp.start()
            cps.append(cp)
        for j in range(3):
            there = o_ref.at[qs[j]]
            pltpu.make_async_remote_copy(src_ref=there, dst_ref=there, send_sem=ssem.at[j], recv_sem=rsem.at[j],
                                         device_id=(x, y, c), device_id_type=MESH).wait_recv()
        for cp in cps:
            cp.wait_send()

    vm = pl.BlockSpec(memory_space=pltpu.VMEM)
    return pl.pallas_call(
        body, name="mods_to_owner", in_specs=[vm], out_specs=vm, out_shape=jax.ShapeDtypeStruct((4, R, C), F32),
        scratch_shapes=[pltpu.SemaphoreType.DMA((3,)), pltpu.SemaphoreType.DMA((3,))],
    )(ms)


def _vec(*rows):
    D = rows[0].shape[-1]
    rows = [r.reshape(1, D) for r in rows]
    return jnp.concatenate(rows + [jnp.zeros((8 - len(rows), D), F32)], axis=0)


def _sets(n_layers, n_conv):
    sets = []
    for l in range(n_layers):
        j = l - n_conv
        mixer = [("conv_w_in", l), ("conv_w_out", l)] if l < n_conv else [("attn_w_q", j), ("attn_w_o", j)]
        sets.append([("ffn1_w_in", l), ("ffn1_w_out", l)] + ([("w_kv", 0)] if l == n_conv else []))
        sets.append(mixer + [("ffn2_w_in", l), ("ffn2_w_out", l)])
    return sets


def _as_weight(name, buf):
    _, _, ha, B = buf.shape
    return buf.reshape(8 * ha, B) if name.endswith("w_out") else buf.reshape(4, 2 * ha, B)


def _local_step(x, tgt, tabs, norm_g, conv_w, kv_norm_g, final_norm_g, mods, kvmods, cast, place):
    S, D = x.shape
    n_layers, n_conv = norm_g.shape[0], conv_w.shape[0]
    md = mods.reshape(n_layers, 9, D)

    def vec_of(l, k):
        return _vec(norm_g[l, k], md[l, 3 * k + 1], md[l, 3 * k], md[l, 3 * k + 2])

    W = {}

    def install(items, bufs):
        for (name, idx), b in zip(items, bufs):
            W[name, idx] = _as_weight(name, b)

    Q = _sets(n_layers, n_conv)
    install(Q[0], _comm_call("gather_d2d", _comm_call("gather_ici", [cast[it] for it in Q[0]])))
    arrived = {}

    def ffn_fwd(n, *args):
        ici = [k for k in ([1] if n == 0 else []) + [n + 2] if k < len(Q)]
        d2d = [n + 1] if 2 <= n + 1 < len(Q) else []
        comm = [("gather_ici", [cast[it] for it in Q[k]]) for k in ici] + [("gather_d2d", arrived.pop(k)) for k in d2d]
        xo, h, ab, y, *moved = _ffn_fwd(*args, comm)
        for k in ici:
            arrived[k], moved = moved[:len(Q[k])], moved[len(Q[k]):]
        for k in d2d:
            install(Q[k], moved[:len(Q[k])])
            moved = moved[len(Q[k]):]
        if n == 0:
            install(Q[1], _comm_call("gather_d2d", arrived.pop(1)))
        return xo, h, ab, y

    vec_kv = _vec(kv_norm_g, kvmods[D:], kvmods[:D])
    ng = len(DILATIONS)
    n_rope = ng * GROUP_W // LANES
    saved = {}
    xc = x
    ks = vs = None
    for l in range(n_layers):
        if l == n_conv:
            h_kv, kv = _proj_fwd(xc, vec_kv, W["w_kv", 0], tabs, n_rope, DILATIONS * 2)
            ks, vs = kv[:ng], kv[ng:]
            saved["kv"] = (xc, h_kv)
        v0 = vec_of(l, 0)
        xo, h, ab, y = ffn_fwd(2 * l, xc, v0, W["ffn1_w_in", l], W["ffn1_w_out", l])
        saved[l, 0] = (xc, v0, h, ab, y)
        xc = xo
        v1 = vec_of(l, 1)
        if l < n_conv:
            cw = _vec(conv_w[l, 0], conv_w[l, 1], conv_w[l, 2])
            xo, h, s4, z, y = _conv_fwd(xc, v1, cw, W["conv_w_in", l], W["conv_w_out", l])
            saved[l, 1] = (xc, v1, h, y, cw, s4, z)
        else:
            j = l - n_conv
            h, qs = _proj_fwd(xc, v1, W["attn_w_q", j], tabs, n_rope, DILATIONS, SM_SCALE)
            og = [_attn_fwd(qs[g], ks[g], vs[g], DILATIONS[g]) for g in range(ng)]
            mixs, ljs, y, xo = _attn_out([o for o, _ in og], [s for _, s in og], xc, v1, W["attn_w_o", j])
            saved[l, 1] = (xc, v1, h, y, qs, mixs, ljs)
        xc = xo
        v2 = vec_of(l, 2)
        xo, h, ab, y = ffn_fwd(2 * l + 1, xc, v2, W["ffn2_w_in", l], W["ffn2_w_out", l])
        saved[l, 2] = (xc, v2, h, ab, y)
        xc = xo

    dx, red_f = _final_loss(xc, _vec(final_norm_g), tgt)
    loss_part = 0.5 * jnp.sum(red_f[1]) / D
    G, parts, others = {}, {}, {}
    dmods = [[None] * 9 for _ in range(n_layers)]
    dnorm = [[None] * 3 for _ in range(n_layers)]
    dconvw = [None] * n_conv
    dkv_parts = []

    def halves(items):
        return [G[it].reshape(4, 2, G[it].shape[1] // 2, G[it].shape[2]) for it in items]

    def pair_sums(items, recv):
        for it, g, r in zip(items, halves(items), recv):
            parts[it] = _pair_add(g, r, place[1:])
        return [parts[it] for it in items]

    R = Q[::-1]
    recvd = {}

    def ffn_bwd(dx, l, k, name):
        m = 2 * (n_layers - 1 - l) + (0 if k == 2 else 1)
        ex = [m - 1] if m >= 1 else []
        sc = [m - 2] if m >= 2 else []
        comm = [("exchange", halves(R[e])) for e in ex] + [("scatter", pair_sums(R[e], recvd.pop(e))) for e in sc]
        xin, v, h, ab, y = saved[l, k]
        w_in, w_out = W[name + "_w_in", l], W[name + "_w_out", l]
        dy, u, dab, dh, *moved = _ffn_dgrad(dx, v, ab, w_in, w_out, comm)
        for e in ex:
            recvd[e], moved = moved[:len(R[e])], moved[len(R[e]):]
        for e in sc:
            others.update(zip(R[e], moved[:len(R[e])]))
            moved = moved[len(R[e]):]
        G[name + "_w_in", l] = _mm_tn(h, dab, w_in.shape[2])
        G[name + "_w_out", l] = _mm_tn(u, dy[None], D).reshape(4, -1, D)
        return _norm_bwd(dx, [dh], xin, y, v, 0.5)

    def note(l, k, red):
        dnorm[l][k] = red[0]
        dmods[l][3 * k], dmods[l][3 * k + 1], dmods[l][3 * k + 2] = red[2], red[1], red[3]

    for l in reversed(range(n_layers)):
        dx, red = ffn_bwd(dx, l, 2, "ffn2")
        note(l, 2, red)
        if l < n_conv:
            xin, v, h, y, cw, s4, z = saved[l, 1]
            w_in, w_out = W["conv_w_in", l], W["conv_w_out", l]
            dy, d3, dh, dcw = _conv_dgrad(dx, v, cw, s4, w_in, w_out)
            G["conv_w_in", l] = _mm_tn(h, d3[None], w_in.shape[2])
            G["conv_w_out", l] = _mm_tn(z, dy[None], D).reshape(4, -1, D)
            dconvw[l] = dcw[0:3]
        else:
            j = l - n_conv
            xin, v, h, y, qs, mixs, ljs = saved[l, 1]
            wq, wo = W["attn_w_q", j], W["attn_w_o", j]
            dy, dmixs = _attn_out_dgrad(dx, v, wo)
            G["attn_w_o", j] = _mm_tn(mixs[0], dy[None], wo.shape[2])
            dqkv = [_attn_bwd(qs[g], ks[g], vs[g], dmixs[g], mixs[g], ljs[g], DILATIONS[g]) for g in range(ng)]
            dkv_parts.append([t[1:] for t in dqkv])
            dq, dh = _proj_dgrad([[t[0]] for t in dqkv], tabs, wq, ng, DILATIONS)
            G["attn_w_q", j] = _mm_tn(h, dq[None], wq.shape[2])
        dx, red = _norm_bwd(dx, [dh], xin, y, v, 1.0)
        note(l, 1, red)
        dx, red = ffn_bwd(dx, l, 0, "ffn1")
        note(l, 0, red)
        if l == n_conv:
            xin, h_kv = saved["kv"]
            kv_parts = [[lay[g][0] for lay in dkv_parts] for g in range(ng)] + [[lay[g][1] for lay in dkv_parts] for g in range(ng)]
            dkv, dh = _proj_dgrad(kv_parts, tabs, W["w_kv", 0], ng, DILATIONS * 2)
            G["w_kv", 0] = _mm_tn(h_kv, dkv[None], W["w_kv", 0].shape[2])
            dx, red_kv = _norm_bwd(dx, [dh], xin, None, vec_kv, 1.0)
    recvd[len(R) - 1] = _comm_call("exchange", halves(R[-1]))
    tail = R[-2] + R[-1]
    others.update(zip(tail, _comm_call("scatter", pair_sums(R[-2], recvd.pop(len(R) - 2)) + pair_sums(R[-1], recvd.pop(len(R) - 1)))))
    small = jnp.concatenate(
        [jnp.stack([jnp.stack(r) for r in dmods]).reshape(-1), red_kv[2], red_kv[1],
         jnp.stack([jnp.stack(r) for r in dnorm]).reshape(-1), jnp.stack(dconvw).reshape(-1), red_kv[0], red_f[0]])
    return loss_part, dx, parts, others, small


BIG = ("ffn1_w_in", "ffn1_w_out", "ffn2_w_in", "ffn2_w_out", "conv_w_in", "conv_w_out", "w_kv", "attn_w_q", "attn_w_o")


def _halved(t):
    if t.ndim == 2:
        t = t[None]
    L, A, B = t.shape
    return t.reshape(L, 2, A // 2, B)


def kernel(x, c, positions, norm_g, ada_w, ada_b, ffn1_w_in, ffn1_w_out, ffn2_w_in, ffn2_w_out, conv_w_in, conv_w, conv_w_out, kv_norm_g, kv_ada_w, kv_ada_b, w_kv, attn_w_q, attn_w_o, final_norm_g, loss_target, m_norm_g, m_ada_w, m_ada_b, m_ffn1_w_in, m_ffn1_w_out, m_ffn2_w_in, m_ffn2_w_out, m_conv_w_in, m_conv_w, m_conv_w_out, m_kv_norm_g, m_kv_ada_w, m_kv_ada_b, m_w_kv, m_attn_w_q, m_attn_w_o, m_final_norm_g, v_norm_g, v_ada_w, v_ada_b, v_ffn1_w_in, v_ffn1_w_out, v_ffn2_w_in, v_ffn2_w_out, v_conv_w_in, v_conv_w, v_conv_w_out, v_kv_norm_g, v_kv_ada_w, v_kv_ada_b, v_w_kv, v_attn_w_q, v_attn_w_o, v_final_norm_g):
    wts = dict(norm_g=norm_g, ada_w=ada_w, ada_b=ada_b, ffn1_w_in=ffn1_w_in, ffn1_w_out=ffn1_w_out, ffn2_w_in=ffn2_w_in,
               ffn2_w_out=ffn2_w_out, conv_w_in=conv_w_in, conv_w=conv_w, conv_w_out=conv_w_out, kv_norm_g=kv_norm_g,
               kv_ada_w=kv_ada_w, kv_ada_b=kv_ada_b, w_kv=w_kv, attn_w_q=attn_w_q, attn_w_o=attn_w_o,
               final_norm_g=final_norm_g)
    ms = dict(norm_g=m_norm_g, ada_w=m_ada_w, ada_b=m_ada_b, ffn1_w_in=m_ffn1_w_in, ffn1_w_out=m_ffn1_w_out,
              ffn2_w_in=m_ffn2_w_in, ffn2_w_out=m_ffn2_w_out, conv_w_in=m_conv_w_in, conv_w=m_conv_w,
              conv_w_out=m_conv_w_out, kv_norm_g=m_kv_norm_g, kv_ada_w=m_kv_ada_w, kv_ada_b=m_kv_ada_b, w_kv=m_w_kv,
              attn_w_q=m_attn_w_q, attn_w_o=m_attn_w_o, final_norm_g=m_final_norm_g)
    vs = dict(norm_g=v_norm_g, ada_w=v_ada_w, ada_b=v_ada_b, ffn1_w_in=v_ffn1_w_in, ffn1_w_out=v_ffn1_w_out,
              ffn2_w_in=v_ffn2_w_in, ffn2_w_out=v_ffn2_w_out, conv_w_in=v_conv_w_in, conv_w=v_conv_w,
              conv_w_out=v_conv_w_out, kv_norm_g=v_kv_norm_g, kv_ada_w=v_kv_ada_w, kv_ada_b=v_kv_ada_b, w_kv=v_w_kv,
              attn_w_q=v_attn_w_q, attn_w_o=v_attn_w_o, final_norm_g=v_final_norm_g)
    order = list(wts)
    S, D = x.shape[1], x.shape[2]
    n_layers, n_conv = norm_g.shape[0], conv_w.shape[0]
    nm, sw = ada_w.shape[2], norm_g.shape[2]
    ix, iy, ic = (lax.axis_index(a) for a in AXES)
    chip = 2 * ix + iy

    pack = jnp.concatenate([norm_g.reshape(-1), conv_w.reshape(-1), c.reshape(-1)])
    npad = (-pack.size) % (8 * LANES)
    allp = _gather8(jnp.pad(pack, (0, npad)).reshape(-1, LANES)).reshape(8, -1)
    n1, n2 = norm_g.size, norm_g.size + conv_w.size
    by_chip = allp[0::2]
    norm_full = jnp.moveaxis(by_chip[:, :n1].reshape(4, n_layers, 3, sw), 0, 2).reshape(n_layers, 3, 4 * sw)
    conv_full = jnp.moveaxis(by_chip[:, n1:n2].reshape(4, n_conv, 3, sw), 0, 2).reshape(n_conv, 3, 4 * sw)
    c_all = allp[:, n2:n2 + D]

    b_l = lax.dynamic_slice_in_dim(ada_b, chip * nm, nm, axis=1)
    mod_sh = _ada_fwd(c_all, ada_w, b_l[:, None, :])
    nkv = kv_ada_w.shape[1]
    bkv = lax.dynamic_slice_in_dim(kv_ada_b, chip * nkv, nkv, axis=0)
    kv_sh = _ada_fwd(c_all, kv_ada_w[None], bkv[None, None, :])
    rows = jnp.concatenate([jnp.moveaxis(mod_sh, 0, 1), jnp.pad(jnp.moveaxis(kv_sh, 0, 1), ((0, 0), (0, 0), (0, nm - nkv)))], axis=1)
    rpad = (-rows.shape[1]) % 8
    mine = _mods_to_owner(jnp.pad(rows, ((0, 0), (0, rpad), (0, 0))))
    mods = jnp.moveaxis(mine[:, :n_layers], 0, 1).reshape(n_layers, 4 * nm)
    kvmods = mine[:, n_layers, :nkv].reshape(4 * nkv)

    place = jnp.stack([chip, ic]).astype(jnp.int32)
    halved = {k: _halved(wts[k]) for k in BIG}
    cast = {(k, l): _cast_shard(halved[k], l, place) for k in BIG for l in range(halved[k].shape[0])}

    loss_part, dx, parts, others, small = _local_step(x[0], loss_target[0], _rope_tables(positions), norm_full, conv_full,
                                                      kv_norm_g, final_norm_g, mods, kvmods, cast, place)
    loss = lax.psum(loss_part, AXES)

    gbufs = []
    for k in BIG:
        gbuf, n_l = None, halved[k].shape[0]
        for l in range(n_l):
            gbuf = _sum_partials(parts[k, l], others[k, l], gbuf, n_l, l, place)
        gbufs.append(gbuf)
    grads = {k: s.reshape(wts[k].shape) for k, s in zip(BIG, _comm_call("share", gbufs))}

    spad = (-small.size) % (8 * LANES)
    gath = _gather8(jnp.pad(small, (0, spad)).reshape(-1, LANES))
    tot = _sum8(gath).reshape(-1)
    gath = gath.reshape(8, -1)
    o = 0

    def take(n):
        nonlocal o
        o += n
        return tot[o - n:o]

    g_mods = take(n_layers * 9 * D).reshape(n_layers, 9 * D)
    g_kvmods = take(2 * D)
    g_norm = take(n_layers * 3 * D).reshape(n_layers, 3, D)
    g_convw = take(n_conv * 3 * D).reshape(n_conv, 3, D)
    g_kvn = take(D)
    g_fin = take(D)
    grads["ada_b"] = g_mods
    grads["kv_ada_b"] = g_kvmods
    grads["norm_g"] = lax.dynamic_slice_in_dim(g_norm, chip * sw, sw, axis=2)
    grads["conv_w"] = lax.dynamic_slice_in_dim(g_convw, chip * sw, sw, axis=2)
    grads["kv_norm_g"] = g_kvn
    grads["final_norm_g"] = g_fin

    dm_all = gath[:, :n_layers * 9 * D].reshape(8, n_layers, 9 * D)
    dm_mine = jnp.moveaxis(lax.dynamic_slice_in_dim(dm_all, chip * nm, nm, axis=2), 0, 1)
    dkv_all = gath[:, n_layers * 9 * D:n_layers * 9 * D + 2 * D]
    dkv_mine = lax.dynamic_slice_in_dim(dkv_all, chip * nkv, nkv, axis=1)[None]
    c_t = c_all.T
    grads["ada_w"] = _ada_wgrad(c_t, dm_mine)
    grads["kv_ada_w"] = _ada_wgrad(c_t, dkv_mine)[0]

    deltas, new_m, new_v = {}, {}, {}
    for k in order:
        shp = wts[k].shape
        two_d = (lambda t: t.reshape(1, -1)) if len(shp) == 1 else (lambda t: t)
        dlt, nmk, nvk = _adamw(two_d(wts[k]), two_d(grads[k]), two_d(ms[k]), two_d(vs[k]))
        deltas[k], new_m[k], new_v[k] = dlt.reshape(shp), nmk.reshape(shp), nvk.reshape(shp)
    return (loss, dx[None], *[grads[k] for k in order], *[deltas[k] for k in order], *[new_m[k] for k in order],
            *[new_v[k] for k in order])
```

```python
import functools
import math

import jax
import jax.numpy as jnp
from jax import lax
from jax.experimental import pallas as pl
from jax.experimental.pallas import tpu as pltpu

F32 = jnp.float32
BF = jnp.bfloat16
MESH = pl.DeviceIdType.MESH
AXES = ("x", "y", "c")

NORM_EPS = 1e-5
HEAD_DIM = 64
HEADS_PER_GROUP = 8
GROUP_W = HEADS_PER_GROUP * HEAD_DIM
DILATIONS = (1, 4, 16)
BAND = 128
ROPE_DIM = 16
ROPE_THETA = 500000.0
SM_SCALE = HEAD_DIM ** -0.5
NEG = -1e30
ADAM_LR, ADAM_B1, ADAM_B2, ADAM_EPS, ADAM_WD, ADAM_STEP = 0.001, 0.9, 0.999, 1e-08, 0.01, 10

V7X_VMEM_BYTES = 64 * 1024 * 1024
VMEM_LIMIT = V7X_VMEM_BYTES - 6 * 1024 * 1024
LANES = 128
TM = 512
TK = 1024
WGRAD_COLS = 1536


def _cparams(ngrid):
    return pltpu.CompilerParams(dimension_semantics=("arbitrary",) * ngrid, vmem_limit_bytes=VMEM_LIMIT)


def _dot(a, b):
    return jnp.dot(a, b, preferred_element_type=F32)


def _dot_nt(a, b):
    return lax.dot_general(a, b, (((1,), (1,)), ((), ())), preferred_element_type=F32)


def _dot_tn(a, b):
    return lax.dot_general(a, b, (((0,), (0,)), ((), ())), preferred_element_type=F32)


def _tile(n, pref):
    t = min(n, pref)
    while n % t:
        t //= 2
    return t


def _row_tile(n, cap=256, mult=16):
    best = n
    for t in range(mult, min(n, cap) + 1, mult):
        if n % t == 0:
            best = t
    return best


def _rstd(x):
    return lax.rsqrt(jnp.mean(x * x, axis=-1, keepdims=True) + NORM_EPS)


def _norm_mod(x, v):
    return (x * _rstd(x) * v[0:1]) * (1.0 + v[1:2]) + v[2:3]


def _silu_parts(a):
    sg = jax.nn.sigmoid(a)
    return sg, a * sg


def _row(i):
    return lambda *_: (i, 0)


ANY = pl.BlockSpec(memory_space=pl.ANY)
IN_PLACE = ("gather_ici", "gather_d2d", "share")


def _place():
    x, y, c = (lax.axis_index(a) for a in AXES)
    chips = [(1 - x, y), (x, 1 - y), (1 - x, 1 - y)]
    return x, y, c, 2 * x + y, chips, [2 * cx + cy for cx, cy in chips]


def _transfers(kind, ins, outs):
    x, y, c, p, chips, qs = _place()
    sib = (x, y, 1 - c)
    rows = []
    for k, o in enumerate(outs):
        if kind == "gather_ici":
            rows.append([(o.at[p, c], o.at[p, c], (*chips[j], c), o.at[qs[j], c]) for j in range(3)])
        elif kind == "gather_d2d":
            rows.append([(o.at[qs[j], c], o.at[qs[j], c], sib, o.at[qs[j], 1 - c]) for j in range(3)])
        elif kind == "exchange":
            rows.append([(ins[k].at[:, 1 - c], o, sib, o)])
        elif kind == "scatter":
            rows.append([(ins[k].at[qs[j]], o.at[j], (*chips[j], c), o.at[j]) for j in range(3)])
        elif kind == "share":
            rows.append([(o.at[:, c], o.at[:, c], sib, o.at[:, 1 - c])])
    return rows


def _comm_out_shapes(kind, arrays):
    if kind in IN_PLACE:
        return [jax.ShapeDtypeStruct(a.shape, a.dtype) for a in arrays]
    if kind == "exchange":
        return [jax.ShapeDtypeStruct((4,) + a.shape[2:], a.dtype) for a in arrays]
    return [jax.ShapeDtypeStruct((3,) + a.shape[1:], a.dtype) for a in arrays]


def _comm_sems(n):
    return [pltpu.SemaphoreType.DMA((n, 3)), pltpu.SemaphoreType.DMA((n, 3))]


def _comm_start(rows, ssem, rsem):
    for k, row in enumerate(rows):
        for j, (src, dst, dev, _) in enumerate(row):
            pltpu.make_async_remote_copy(src_ref=src, dst_ref=dst, send_sem=ssem.at[k, j], recv_sem=rsem.at[k, j],
                                         device_id=dev, device_id_type=MESH).start()


def _comm_wait(rows, ssem, rsem):
    for k, row in enumerate(rows):
        for j, (src, dst, dev, land) in enumerate(row):
            pltpu.make_async_remote_copy(src_ref=src, dst_ref=dst, send_sem=ssem.at[k, j], recv_sem=rsem.at[k, j],
                                         device_id=dev, device_id_type=MESH).wait_send()
            pltpu.make_async_remote_copy(src_ref=land, dst_ref=land, send_sem=ssem.at[k, j], recv_sem=rsem.at[k, j],
                                         device_id=dev, device_id_type=MESH).wait_recv()


def _comm_call(kind, arrays):
    n = len(arrays)

    def body(*refs):
        rows = _transfers(kind, refs[:n], refs[n:2 * n])
        _comm_start(rows, *refs[2 * n:])
        _comm_wait(rows, *refs[2 * n:])

    return pl.pallas_call(
        body, name="comm_" + kind, in_specs=[ANY] * n, out_specs=[ANY] * n, out_shape=_comm_out_shapes(kind, arrays),
        input_output_aliases={k: k for k in range(n)} if kind in IN_PLACE else {},
        scratch_shapes=_comm_sems(n),
    )(*arrays)


def _ride(comm, n_in, n_out):
    ride = dict(arrays=[], in_specs=[], out_specs=[], out_shape=[], aliases={}, scratch=[], n=0)
    for kind, arrays in comm or []:
        if kind in IN_PLACE:
            ride["aliases"].update({n_in + ride["n"] + k: n_out + ride["n"] + k for k in range(len(arrays))})
        ride["arrays"] += list(arrays)
        ride["out_shape"] += _comm_out_shapes(kind, arrays)
        ride["scratch"] += _comm_sems(len(arrays))
        ride["n"] += len(arrays)
    ride["in_specs"] = ride["out_specs"] = [ANY] * ride["n"]
    return ride


def _ride_when(comm, ins, outs, sems, cond, action):
    if not comm:
        return

    @pl.when(cond)
    def _():
        at = 0
        for e, (kind, arrays) in enumerate(comm):
            n = len(arrays)
            action(_transfers(kind, ins[at:at + n], outs[at:at + n]), *sems[2 * e:2 * e + 2])
            at += n


def _ffn_fwd(x, vec, w_in4, w_out, comm=None):
    S, D = x.shape
    cs = w_in4.shape[2]
    F = 2 * cs
    tm = _tile(S, TM)
    ni = S // tm
    ride = _ride(comm, 5, 4)
    nc = ride["n"]

    def body(*refs):
        x_ref, vec_ref, wa_ref, wb_ref, wo_ref = refs[:5]
        xo_ref, h_ref, ab_ref, y_ref = refs[5 + nc:9 + nc]
        acc_ref = refs[-1]
        i, j = pl.program_id(0), pl.program_id(1)
        riding = (comm, refs[5:5 + nc], refs[9 + nc:9 + 2 * nc], refs[9 + 2 * nc:-1])
        _ride_when(*riding, jnp.logical_and(i == 0, j == 0), _comm_start)

        @pl.when(j == 0)
        def _():
            h_ref[...] = _norm_mod(x_ref[...], vec_ref[...]).astype(BF)
            acc_ref[...] = jnp.zeros_like(acc_ref)

        h = h_ref[...]
        a = _dot(h, wa_ref[...])
        b = _dot(h, wb_ref[...])
        ab_ref[0] = a.astype(BF)
        ab_ref[1] = b.astype(BF)
        _, s = _silu_parts(a)
        acc_ref[...] += _dot((s * b).astype(BF), wo_ref[...])

        @pl.when(j == 1)
        def _():
            y = acc_ref[...]
            y_ref[...] = y.astype(BF)
            xo_ref[...] = x_ref[...] + (0.5 * (1.0 + vec_ref[3:4, :])) * y

        _ride_when(*riding, jnp.logical_and(i == ni - 1, j == 1), _comm_wait)

    tok = pl.BlockSpec((tm, D), lambda i, j: (i, 0))
    return pl.pallas_call(
        body, name="ffn_fwd", grid=(ni, 2),
        in_specs=[tok, pl.BlockSpec((8, D), lambda i, j: (0, 0)),
                  pl.BlockSpec((None, D, cs), lambda i, j: (j, 0, 0)),
                  pl.BlockSpec((None, D, cs), lambda i, j: (j + 2, 0, 0)),
                  pl.BlockSpec((cs, D), lambda i, j: (j, 0))] + ride["in_specs"],
        out_specs=[tok, tok, pl.BlockSpec((2, tm, cs), lambda i, j: (0, i, j)), tok] + ride["out_specs"],
        out_shape=[jax.ShapeDtypeStruct((S, D), F32), jax.ShapeDtypeStruct((S, D), BF),
                   jax.ShapeDtypeStruct((2, S, F), BF), jax.ShapeDtypeStruct((S, D), BF)] + ride["out_shape"],
        input_output_aliases=ride["aliases"], scratch_shapes=ride["scratch"] + [pltpu.VMEM((tm, D), F32)],
        compiler_params=_cparams(2),
    )(x, vec, w_in4, w_in4, w_out, *ride["arrays"])


def _norm_bwd_tile(dxo, dh, xv, y, vec_ref, red_ref, coef):
    g = vec_ref[0:1, :]
    sc = vec_ref[1:2, :]
    r = _rstd(xv)
    xh = xv * r
    dhn = dh * (1.0 + sc)
    dxh = dhn * g
    red_ref[0:1, :] += jnp.sum(dhn * xh, axis=0, keepdims=True)
    red_ref[1:2, :] += jnp.sum(dh * (xh * g), axis=0, keepdims=True)
    red_ref[2:3, :] += jnp.sum(dh, axis=0, keepdims=True)
    if y is not None:
        red_ref[3:4, :] += coef * jnp.sum(dxo * y.astype(F32), axis=0, keepdims=True)
    return dxo + r * (dxh - xh * jnp.mean(dxh * xh, axis=-1, keepdims=True))


def _ffn_dgrad(dxo, vec, ab, w_in4, w_out, comm=None):
    S, D = dxo.shape
    cs = w_in4.shape[2]
    F = 2 * cs
    tm = _tile(S, TM)
    ni = S // tm
    ride = _ride(comm, 6, 4)
    nc = ride["n"]

    def body(*refs):
        dxo_ref, vec_ref, ab_ref, wa_ref, wb_ref, wo_ref = refs[:6]
        dy_ref, u_ref, dab_ref, dh_out = refs[6 + nc:10 + nc]
        dh_ref = refs[-1]
        i, j = pl.program_id(0), pl.program_id(1)
        riding = (comm, refs[6:6 + nc], refs[10 + nc:10 + 2 * nc], refs[10 + 2 * nc:-1])
        _ride_when(*riding, jnp.logical_and(i == 0, j == 0), _comm_start)

        @pl.when(j == 0)
        def _():
            dy_ref[...] = ((0.5 * (1.0 + vec_ref[3:4, :])) * dxo_ref[...]).astype(BF)
            dh_ref[...] = jnp.zeros_like(dh_ref)

        du = _dot_nt(dy_ref[...], wo_ref[...])
        a = ab_ref[0].astype(F32)
        b = ab_ref[1].astype(F32)
        sg, s = _silu_parts(a)
        u_ref[...] = (s * b).astype(BF)
        da = (du * b * (sg * (1.0 + a * (1.0 - sg)))).astype(BF)
        db = (du * s).astype(BF)
        dab_ref[0] = da
        dab_ref[1] = db
        dh_ref[...] += _dot_nt(da, wa_ref[...]) + _dot_nt(db, wb_ref[...])

        @pl.when(j == 1)
        def _():
            dh_out[...] = dh_ref[...].astype(BF)

        _ride_when(*riding, jnp.logical_and(i == ni - 1, j == 1), _comm_wait)

    tok = pl.BlockSpec((tm, D), lambda i, j: (i, 0))
    return pl.pallas_call(
        body, name="ffn_dgrad", grid=(ni, 2),
        in_specs=[tok, pl.BlockSpec((8, D), lambda i, j: (0, 0)),
                  pl.BlockSpec((2, tm, cs), lambda i, j: (0, i, j)),
                  pl.BlockSpec((None, D, cs), lambda i, j: (j, 0, 0)),
                  pl.BlockSpec((None, D, cs), lambda i, j: (j + 2, 0, 0)),
                  pl.BlockSpec((cs, D), lambda i, j: (j, 0))] + ride["in_specs"],
        out_specs=[tok, pl.BlockSpec((tm, cs), lambda i, j: (i, j)),
                   pl.BlockSpec((2, tm, cs), lambda i, j: (0, i, j)), tok] + ride["out_specs"],
        out_shape=[jax.ShapeDtypeStruct((S, D), BF), jax.ShapeDtypeStruct((S, F), BF),
                   jax.ShapeDtypeStruct((2, S, F), BF), jax.ShapeDtypeStruct((S, D), BF)] + ride["out_shape"],
        input_output_aliases=ride["aliases"], scratch_shapes=ride["scratch"] + [pltpu.VMEM((tm, D), F32)],
        compiler_params=_cparams(2),
    )(dxo, vec, ab, w_in4, w_in4, w_out, *ride["arrays"])


def _mm_tn(a, b3, cs, comm=None):
    K, M = a.shape
    G, _, Nb = b3.shape
    N = G * Nb
    if cs >= Nb or Nb % cs:
        ns, tn = 1, math.gcd(cs, Nb)
    else:
        ns = max(s for s in range(1, Nb // cs + 1) if (Nb // cs) % s == 0 and s * cs <= max(WGRAD_COLS, cs))
        tn = ns * cs
    tmo = M if M <= 1024 else M // 2
    tk = _tile(K, TK)
    gm, gn, gk = M // tmo, N // tn, K // tk
    ride = _ride(comm, 2, 1)
    nc = ride["n"]

    def body(*refs):
        a_ref, b_ref = refs[:2]
        o_ref, acc_ref = refs[2 + nc], refs[-1]
        at = [pl.program_id(t) for t in range(3)]
        riding = (comm, refs[2:2 + nc], refs[3 + nc:3 + 2 * nc], refs[3 + 2 * nc:-1])
        _ride_when(*riding, functools.reduce(jnp.logical_and, [t == 0 for t in at]), _comm_start)

        @pl.when(at[2] == 0)
        def _():
            acc_ref[...] = jnp.zeros_like(acc_ref)

        av = a_ref[...].astype(BF)
        if ns == 1:
            acc_ref[...] += _dot_tn(av, b_ref[...].astype(BF))
        else:
            for s in range(ns):
                acc_ref[s] += _dot_tn(av, b_ref[:, s * cs:(s + 1) * cs].astype(BF))

        @pl.when(at[2] == gk - 1)
        def _():
            o_ref[...] = acc_ref[...].astype(BF)

        _ride_when(*riding, functools.reduce(jnp.logical_and, [t == g - 1 for t, g in zip(at, (gm, gn, gk))]), _comm_wait)

    if ns == 1:
        out_spec = pl.BlockSpec((None, tmo, tn), lambda m, n, k: ((n * tn) // cs, m, ((n * tn) % cs) // tn))
        acc = pltpu.VMEM((tmo, tn), F32)
    else:
        out_spec = pl.BlockSpec((ns, tmo, cs), lambda m, n, k: (n, m, 0))
        acc = pltpu.VMEM((ns, tmo, cs), F32)
    g, *moved = pl.pallas_call(
        body, name="wgrad_tn", grid=(gm, gn, gk),
        in_specs=[pl.BlockSpec((tk, tmo), lambda m, n, k: (k, m)),
                  pl.BlockSpec((None, tk, tn), lambda m, n, k: ((n * tn) // Nb, k, ((n * tn) % Nb) // tn))] + ride["in_specs"],
        out_specs=[out_spec] + ride["out_specs"],
        out_shape=[jax.ShapeDtypeStruct((N // cs, M, cs), BF)] + ride["out_shape"],
        input_output_aliases=ride["aliases"], scratch_shapes=ride["scratch"] + [acc],
        compiler_params=_cparams(3),
    )(a, b3, *ride["arrays"])
    return (g, moved) if comm else g


def _norm_bwd(dxo, dhs, x, y, vec, coef):
    S, D = x.shape
    tm = _tile(S, TM)
    nh = len(dhs)
    has_y = y is not None

    def body(*refs):
        dxo_ref = refs[0]
        dh_refs = refs[1:1 + nh]
        x_ref = refs[1 + nh]
        y_ref = refs[2 + nh] if has_y else None
        vec_ref, dx_ref, red_ref = refs[-3:]

        @pl.when(pl.program_id(0) == 0)
        def _():
            red_ref[...] = jnp.zeros_like(red_ref)

        dh = dh_refs[0][...].astype(F32)
        for r in dh_refs[1:]:
            dh = dh + r[...].astype(F32)
        dx_ref[...] = _norm_bwd_tile(dxo_ref[...], dh, x_ref[...], y_ref[...] if has_y else None, vec_ref, red_ref, coef)

    tok = pl.BlockSpec((tm, D), lambda i: (i, 0))
    small = pl.BlockSpec((8, D), lambda i: (0, 0))
    ops = [dxo, *dhs, x] + ([y] if has_y else []) + [vec]
    return pl.pallas_call(
        body, name="norm_bwd", grid=(S // tm,),
        in_specs=[tok] * (len(ops) - 1) + [small],
        out_specs=[tok, small],
        out_shape=[jax.ShapeDtypeStruct((S, D), F32), jax.ShapeDtypeStruct((8, D), F32)],
        compiler_params=_cparams(1),
    )(*ops)


def _conv_fwd(x, vec, cw, w_in4, w_out):
    S, D = x.shape
    cs = w_in4.shape[2]
    tm = _tile(S, TM)

    def body(x_ref, vec_ref, cw_ref, wi_ref, wo_ref, xo_ref, h_ref, s4_ref, z_ref, y_ref, vs_ref):
        @pl.when(pl.program_id(0) == 0)
        def _():
            vs_ref[0:8, :] = jnp.zeros((8, D), F32)

        xv = x_ref[...]
        h = _norm_mod(xv, vec_ref[...]).astype(BF)
        h_ref[...] = h
        bcu = jnp.concatenate([_dot(h, wi_ref[q]) for q in range(4)], axis=1)
        bg, cg, u = bcu[:, :D], bcu[:, D:2 * D], bcu[:, 2 * D:]
        v = cg * u
        vs_ref[8:8 + tm, :] = v
        conv = cw_ref[0:1, :] * vs_ref[pl.ds(6, tm), :] + cw_ref[1:2, :] * vs_ref[pl.ds(7, tm), :] + cw_ref[2:3, :] * v
        vs_ref[0:8, :] = vs_ref[tm:tm + 8, :]
        z = (bg * conv).astype(BF)
        s4_ref[0] = bg.astype(BF)
        s4_ref[1] = cg.astype(BF)
        s4_ref[2] = u.astype(BF)
        s4_ref[3] = conv.astype(BF)
        z_ref[...] = z
        y = _dot(z, wo_ref[...])
        y_ref[...] = y.astype(BF)
        xo_ref[...] = xv + (1.0 + vec_ref[3:4, :]) * y

    tok = pl.BlockSpec((tm, D), lambda i: (i, 0))
    small = pl.BlockSpec((8, D), lambda i: (0, 0))
    return pl.pallas_call(
        body, name="conv_fwd", grid=(S // tm,),
        in_specs=[tok, small, small, pl.BlockSpec((4, D, cs), lambda i: (0, 0, 0)),
                  pl.BlockSpec((D, D), lambda i: (0, 0))],
        out_specs=[tok, tok, pl.BlockSpec((4, tm, D), lambda i: (0, i, 0)), tok, tok],
        out_shape=[jax.ShapeDtypeStruct((S, D), F32), jax.ShapeDtypeStruct((S, D), BF),
                   jax.ShapeDtypeStruct((4, S, D), BF), jax.ShapeDtypeStruct((S, D), BF),
                   jax.ShapeDtypeStruct((S, D), BF)],
        scratch_shapes=[pltpu.VMEM((tm + 8, D), F32)],
        compiler_params=_cparams(1),
    )(x, vec, cw, w_in4, w_out)


def _conv_dgrad(dxo, vec, cw, s4, w_in4, w_out):
    S, D = dxo.shape
    cs = w_in4.shape[2]
    tm = _tile(S, TM)
    nt = S // tm

    def body(dxo_ref, vec_ref, cw_ref, s4_ref, wi_ref, wo_ref, dy_ref, d3_ref, dh_ref, dcw_ref, ds_ref):
        @pl.when(pl.program_id(0) == 0)
        def _():
            ds_ref[tm:tm + 8, :] = jnp.zeros((8, D), F32)
            dcw_ref[...] = jnp.zeros_like(dcw_ref)

        dy = ((1.0 + vec_ref[3:4, :]) * dxo_ref[...]).astype(BF)
        dy_ref[...] = dy
        dz = _dot_nt(dy, wo_ref[...])
        bg = s4_ref[0].astype(F32)
        cg = s4_ref[1].astype(F32)
        u = s4_ref[2].astype(F32)
        conv = s4_ref[3].astype(F32)
        dbg = dz * conv
        dconv = dz * bg
        ds_ref[0:tm, :] = dconv
        d1 = ds_ref[pl.ds(1, tm), :]
        d2 = ds_ref[pl.ds(2, tm), :]
        ds_ref[tm:tm + 8, :] = ds_ref[0:8, :]
        dv = cw_ref[2:3, :] * dconv + cw_ref[1:2, :] * d1 + cw_ref[0:1, :] * d2
        v = cg * u
        dcw_ref[0:1, :] += jnp.sum(d2 * v, axis=0, keepdims=True)
        dcw_ref[1:2, :] += jnp.sum(d1 * v, axis=0, keepdims=True)
        dcw_ref[2:3, :] += jnp.sum(dconv * v, axis=0, keepdims=True)
        dbcu = jnp.concatenate([dbg, dv * u, dv * cg], axis=1).astype(BF)
        d3_ref[...] = dbcu
        dh = _dot_nt(dbcu[:, 0:cs], wi_ref[0])
        for q in range(1, 4):
            dh = dh + _dot_nt(dbcu[:, q * cs:(q + 1) * cs], wi_ref[q])
        dh_ref[...] = dh.astype(BF)

    tok = pl.BlockSpec((tm, D), lambda i: (nt - 1 - i, 0))
    small = pl.BlockSpec((8, D), lambda i: (0, 0))
    return pl.pallas_call(
        body, name="conv_dgrad", grid=(nt,),
        in_specs=[tok, small, small, pl.BlockSpec((4, tm, D), lambda i: (0, nt - 1 - i, 0)),
                  pl.BlockSpec((4, D, cs), lambda i: (0, 0, 0)), pl.BlockSpec((D, D), lambda i: (0, 0))],
        out_specs=[tok, pl.BlockSpec((tm, 3 * D), lambda i: (nt - 1 - i, 0)), tok, small],
        out_shape=[jax.ShapeDtypeStruct((S, D), BF), jax.ShapeDtypeStruct((S, 3 * D), BF),
                   jax.ShapeDtypeStruct((S, D), BF), jax.ShapeDtypeStruct((8, D), F32)],
        scratch_shapes=[pltpu.VMEM((tm + 8, D), F32)],
        compiler_params=_cparams(1),
    )(dxo, vec, cw, s4, w_in4, w_out)


def _rope_tables(positions):
    S = positions.shape[-1]
    inv = ROPE_THETA ** (-jnp.arange(0, ROPE_DIM, 2, dtype=F32) / ROPE_DIM)
    ang = positions.reshape(S, 1).astype(F32) * inv
    cos = jnp.tile(jnp.cos(ang), (1, LANES // 8))
    sin = jnp.tile(jnp.sin(ang), (1, LANES // 8))
    l64 = jnp.arange(LANES) % HEAD_DIM
    return jnp.stack([jnp.where(l64 < ROPE_DIM, cos, 1.0),
                      jnp.where(l64 < ROPE_DIM // 2, -sin, 0.0),
                      jnp.where((l64 >= ROPE_DIM // 2) & (l64 < ROPE_DIM), sin, 0.0)])


def _rope(t, tab_ref):
    return t * tab_ref[0] + pltpu.roll(t, LANES - 8, 1) * tab_ref[1] + pltpu.roll(t, 8, 1) * tab_ref[2]


def _rope_t(d, tab_ref):
    return d * tab_ref[0] + pltpu.roll(d * tab_ref[1], 8, 1) + pltpu.roll(d * tab_ref[2], LANES - 8, 1)


GROUP_CH = GROUP_W // LANES


def _to_dilated(src_ref, c0, dst_ref, d):
    n = src_ref.shape[1]
    for r in range(d):
        for ch in range(GROUP_CH):
            rows = src_ref[c0 + ch] if d == 1 else src_ref.at[c0 + ch][pl.ds(r, n // d, stride=d), :]
            dst_ref[:, r * GROUP_W + ch * LANES:r * GROUP_W + (ch + 1) * LANES] = rows.astype(dst_ref.dtype)


def _from_dilated(val, dst_ref, c0, d):
    n = dst_ref.shape[1]
    for r in range(d):
        for ch in range(GROUP_CH):
            cols = val[:, r * GROUP_W + ch * LANES:r * GROUP_W + (ch + 1) * LANES]
            if d == 1:
                dst_ref[c0 + ch] = cols
            else:
                dst_ref.at[c0 + ch][pl.ds(r, n // d, stride=d), :] = cols


def _group(ref, c0):
    return jnp.concatenate([ref[c0 + ch] for ch in range(GROUP_CH)], axis=1)


def _put_group(ref, c0, val):
    for ch in range(GROUP_CH):
        ref[c0 + ch] = val[:, ch * LANES:(ch + 1) * LANES]


def _dil_spec(tm, d):
    return pl.BlockSpec((tm // d, d * GROUP_W), lambda i: (i, 0))


def _proj_fwd(x, vec, w4, tabs, n_rope, dils, scale=1.0):
    S, D = x.shape
    cs = w4.shape[2]
    N = 4 * cs
    tm = _tile(S, TM)

    def body(x_ref, vec_ref, w_ref, tab_ref, h_ref, *rest):
        outs, acc_ref = rest[:-1], rest[-1]
        h = _norm_mod(x_ref[...], vec_ref[...]).astype(BF)
        h_ref[...] = h
        per = cs // LANES
        for q in range(4):
            acc = _dot(h, w_ref[q])
            if scale != 1.0:
                acc = acc * scale
            cols = [acc[:, ch * LANES:(ch + 1) * LANES] for ch in range(per)]
            for ch, t in enumerate(cols):
                acc_ref[q * per + ch] = _rope(t, tab_ref) if q * per + ch < n_rope else t
        for j, d in enumerate(dils):
            _to_dilated(acc_ref, j * GROUP_CH, outs[j], d)

    h, *outs = pl.pallas_call(
        body, name="proj_fwd", grid=(S // tm,),
        in_specs=[pl.BlockSpec((tm, D), lambda i: (i, 0)), pl.BlockSpec((8, D), lambda i: (0, 0)),
                  pl.BlockSpec((4, D, cs), lambda i: (0, 0, 0)), pl.BlockSpec((3, tm, LANES), lambda i: (0, i, 0))],
        out_specs=[pl.BlockSpec((tm, D), lambda i: (i, 0))] + [_dil_spec(tm, d) for d in dils],
        out_shape=[jax.ShapeDtypeStruct((S, D), BF)] + [jax.ShapeDtypeStruct((S // d, d * GROUP_W), BF) for d in dils],
        scratch_shapes=[pltpu.VMEM((N // LANES, tm, LANES), F32)],
        compiler_params=_cparams(1),
    )(x, vec, w4, tabs)
    return h, outs


def _proj_dgrad(parts, tabs, w4, n_rope_groups, dils):
    cs, D = w4.shape[2], w4.shape[1]
    N = 4 * cs
    S = parts[0][0].shape[0] * dils[0]
    tm = _tile(S, TM)
    flat = [p for grp in parts for p in grp]
    counts = [len(grp) for grp in parts]

    def body(*refs):
        prefs = refs[:len(flat)]
        tab_ref, w_ref, dz_ref, dh_ref, z_ref = refs[len(flat):]
        k = 0
        for j, cnt in enumerate(counts):
            z = prefs[k][...]
            for r in prefs[k + 1:k + cnt]:
                z = z + r[...]
            k += cnt
            _from_dilated(z, z_ref, 0, dils[j])
            if j < n_rope_groups:
                z = jnp.concatenate([_rope_t(z_ref[ch], tab_ref) for ch in range(GROUP_CH)], axis=1)
            else:
                z = _group(z_ref, 0)
            dz_ref[:, j * GROUP_W:(j + 1) * GROUP_W] = z.astype(BF)
        dh = _dot_nt(dz_ref[:, 0:cs], w_ref[0])
        for q in range(1, 4):
            dh = dh + _dot_nt(dz_ref[:, q * cs:(q + 1) * cs], w_ref[q])
        dh_ref[...] = dh.astype(BF)

    return pl.pallas_call(
        body, name="proj_dgrad", grid=(S // tm,),
        in_specs=[_dil_spec(tm, d) for d, cnt in zip(dils, counts) for _ in range(cnt)]
        + [pl.BlockSpec((3, tm, LANES), lambda i: (0, i, 0)), pl.BlockSpec((4, D, cs), lambda i: (0, 0, 0))],
        out_specs=[pl.BlockSpec((tm, N), lambda i: (i, 0)), pl.BlockSpec((tm, D), lambda i: (i, 0))],
        out_shape=[jax.ShapeDtypeStruct((S, N), BF), jax.ShapeDtypeStruct((S, D), BF)],
        scratch_shapes=[pltpu.VMEM((GROUP_CH, tm, LANES), F32)],
        compiler_params=_cparams(1),
    )(*flat, tabs, w4)


def _band_masks():
    qi = lax.broadcasted_iota(jnp.int32, (BAND, BAND), 0)
    kj = lax.broadcasted_iota(jnp.int32, (BAND, BAND), 1)
    return kj <= qi, kj >= qi


def _head(ref, h):
    return ref[:, h * HEAD_DIM:(h + 1) * HEAD_DIM]


def _attn_fwd(q, k, v, d):
    sd = q.shape[0]
    nb = sd // BAND

    def body(q_ref, kp_ref, kc_ref, vp_ref, vc_ref, o_ref, l_ref):
        same, prev = _band_masks()
        prev = jnp.logical_and(prev, pl.program_id(1) > 0)
        heads = range(HEADS_PER_GROUP)
        scores = [(jnp.where(same, _dot_nt(_head(q_ref, h), _head(kc_ref, h)), NEG),
                   jnp.where(prev, _dot_nt(_head(q_ref, h), _head(kp_ref, h)), NEG)) for h in heads]
        probs = []
        for sc, sp in scores:
            m = jnp.maximum(jnp.max(sc, axis=-1, keepdims=True), jnp.max(sp, axis=-1, keepdims=True))
            pc = jnp.exp(sc - m)
            pp = jnp.exp(sp - m)
            den = jnp.sum(pc, axis=-1, keepdims=True) + jnp.sum(pp, axis=-1, keepdims=True)
            probs.append((pc.astype(BF), pp.astype(BF), m, den))
        outs = [(_dot(pc, _head(vc_ref, h)) + _dot(pp, _head(vp_ref, h))) / den for h, (pc, pp, _, den) in zip(heads, probs)]
        o_ref[...] = jnp.concatenate(outs, axis=1)
        l_ref[...] = jnp.concatenate([jnp.broadcast_to(m + jnp.log(den), (BAND, HEAD_DIM)) for _, _, m, den in probs], axis=1)

    blk = (BAND, GROUP_W)
    cur = pl.BlockSpec(blk, lambda r, i: (i, r))
    prv = pl.BlockSpec(blk, lambda r, i: (jnp.maximum(i - 1, 0), r))
    return pl.pallas_call(
        body, name=f"attn_fwd_d{d}", grid=(d, nb),
        in_specs=[cur, prv, cur, prv, cur], out_specs=[cur, cur],
        out_shape=[jax.ShapeDtypeStruct((sd, d * GROUP_W), F32)] * 2,
        compiler_params=_cparams(2),
    )(q, k, k, v, v)


def _attn_out(os, ls, x, vec, wo4):
    S, D = x.shape
    cs = wo4.shape[2]
    tm = _tile(S, TM)
    ng = len(DILATIONS)

    def body(*refs):
        o_refs, l_refs = refs[0:2 * ng:2], refs[1:2 * ng:2]
        x_ref, vec_ref, wo_ref = refs[2 * ng:2 * ng + 3]
        outs = refs[2 * ng + 3:]
        mix_refs, lj_refs = outs[0:2 * ng:2], outs[1:2 * ng:2]
        y_ref, xo_ref, nat_ref = outs[2 * ng:]
        ov, lv = [], []
        for g in range(ng):
            _from_dilated(o_refs[g][...], nat_ref, 2 * g * GROUP_CH, DILATIONS[g])
            _from_dilated(l_refs[g][...], nat_ref, (2 * g + 1) * GROUP_CH, DILATIONS[g])
            ov.append(_group(nat_ref, 2 * g * GROUP_CH))
            lv.append(_group(nat_ref, (2 * g + 1) * GROUP_CH))
        mx = jnp.maximum(jnp.maximum(lv[0], lv[1]), lv[2])
        es = [jnp.exp(t - mx) for t in lv]
        den = es[0] + es[1] + es[2]
        mix = (es[0] * ov[0] + es[1] * ov[1] + es[2] * ov[2]) / den
        at_mix, at_lj = 2 * ng * GROUP_CH, (2 * ng + 1) * GROUP_CH
        _put_group(nat_ref, at_mix, mix)
        _put_group(nat_ref, at_lj, mx + jnp.log(den))
        for g in range(ng):
            _to_dilated(nat_ref, at_mix, mix_refs[g], DILATIONS[g])
            _to_dilated(nat_ref, at_lj, lj_refs[g], DILATIONS[g])
        mb = mix.astype(BF)
        y = jnp.concatenate([_dot(mb, wo_ref[q]) for q in range(4)], axis=1)
        y_ref[...] = y.astype(BF)
        xo_ref[...] = x_ref[...] + (1.0 + vec_ref[3:4, :]) * y

    tok = pl.BlockSpec((tm, D), lambda i: (i, 0))
    dil = [_dil_spec(tm, d) for d in DILATIONS for _ in range(2)]
    dil_shape = [jax.ShapeDtypeStruct((S // d, d * GROUP_W), F32) for d in DILATIONS for _ in range(2)]
    outs = pl.pallas_call(
        body, name="attn_out", grid=(S // tm,),
        in_specs=dil + [tok, pl.BlockSpec((8, D), lambda i: (0, 0)), pl.BlockSpec((4, GROUP_W, cs), lambda i: (0, 0, 0))],
        out_specs=dil + [tok, tok],
        out_shape=dil_shape + [jax.ShapeDtypeStruct((S, D), BF), jax.ShapeDtypeStruct((S, D), F32)],
        scratch_shapes=[pltpu.VMEM(((2 * ng + 2) * GROUP_CH, tm, LANES), F32)],
        compiler_params=_cparams(1),
    )(*[t for pair in zip(os, ls) for t in pair], x, vec, wo4)
    return list(outs[0:2 * ng:2]), list(outs[1:2 * ng:2]), outs[2 * ng], outs[2 * ng + 1]


def _attn_out_dgrad(dxo, vec, wo4):
    S, D = dxo.shape
    cs = wo4.shape[2]
    tm = _tile(S, TM)

    def body(dxo_ref, vec_ref, wo_ref, dy_ref, *rest):
        dm_refs, nat_ref = rest[:-1], rest[-1]
        dy = ((1.0 + vec_ref[3:4, :]) * dxo_ref[...]).astype(BF)
        dy_ref[...] = dy
        dm = _dot_nt(dy[:, 0:cs], wo_ref[0])
        for q in range(1, 4):
            dm = dm + _dot_nt(dy[:, q * cs:(q + 1) * cs], wo_ref[q])
        _put_group(nat_ref, 0, dm)
        for g, d in enumerate(DILATIONS):
            _to_dilated(nat_ref, 0, dm_refs[g], d)

    tok = pl.BlockSpec((tm, D), lambda i: (i, 0))
    dy, *dms = pl.pallas_call(
        body, name="attn_out_dgrad", grid=(S // tm,),
        in_specs=[tok, pl.BlockSpec((8, D), lambda i: (0, 0)), pl.BlockSpec((4, GROUP_W, cs), lambda i: (0, 0, 0))],
        out_specs=[tok] + [_dil_spec(tm, d) for d in DILATIONS],
        out_shape=[jax.ShapeDtypeStruct((S, D), BF)] + [jax.ShapeDtypeStruct((S // d, d * GROUP_W), F32) for d in DILATIONS],
        scratch_shapes=[pltpu.VMEM((GROUP_CH, tm, LANES), F32)],
        compiler_params=_cparams(1),
    )(dxo, vec, wo4)
    return dy, dms


def _attn_bwd(q, k, v, dmix, mix, lj, d):
    sd = q.shape[0]
    nb = sd // BAND

    def body(q_ref, kp_ref, kc_ref, vp_ref, vc_ref, do_ref, o_ref, l_ref, dq_ref, dk_ref, dv_ref, keep_k, keep_v):
        i = pl.program_id(1)

        @pl.when(i == 0)
        def _():
            keep_k[...] = jnp.zeros_like(keep_k)
            keep_v[...] = jnp.zeros_like(keep_v)

        @pl.when(i < nb)
        def _():
            same, prev = _band_masks()
            prev = jnp.logical_and(prev, i > 0)
            heads = range(HEADS_PER_GROUP)
            dob = [_head(do_ref, h).astype(BF) for h in heads]
            raw = [(_dot_nt(_head(q_ref, h), _head(kc_ref, h)), _dot_nt(_head(q_ref, h), _head(kp_ref, h)),
                    _dot_nt(dob[h], _head(vc_ref, h)), _dot_nt(dob[h], _head(vp_ref, h))) for h in heads]
            mid = []
            for h, (sc, sp, dpc, dpp) in zip(heads, raw):
                lrow = l_ref[:, h * HEAD_DIM:h * HEAD_DIM + 1]
                delta = jnp.sum(_head(do_ref, h) * _head(o_ref, h), axis=-1, keepdims=True)
                pc = jnp.exp(jnp.where(same, sc, NEG) - lrow)
                pp = jnp.exp(jnp.where(prev, sp, NEG) - lrow)
                mid.append((pc.astype(BF), pp.astype(BF), (pc * (dpc - delta)).astype(BF), (pp * (dpp - delta)).astype(BF)))
            dq = [(_dot(dsc, _head(kc_ref, h)) + _dot(dsp, _head(kp_ref, h))) * SM_SCALE for h, (_, _, dsc, dsp) in zip(heads, mid)]
            dk_prev = [_dot_tn(dsp, _head(q_ref, h)) for h, (_, _, _, dsp) in zip(heads, mid)]
            dv_prev = [_dot_tn(pp, dob[h]) for h, (_, pp, _, _) in zip(heads, mid)]
            dk_same = [_dot_tn(dsc, _head(q_ref, h)) for h, (_, _, dsc, _) in zip(heads, mid)]
            dv_same = [_dot_tn(pc, dob[h]) for h, (pc, _, _, _) in zip(heads, mid)]
            dq_ref[...] = jnp.concatenate(dq, axis=1)
            dk_ref[...] = keep_k[...] + jnp.concatenate(dk_prev, axis=1)
            dv_ref[...] = keep_v[...] + jnp.concatenate(dv_prev, axis=1)
            keep_k[...] = jnp.concatenate(dk_same, axis=1)
            keep_v[...] = jnp.concatenate(dv_same, axis=1)

        @pl.when(i == nb)
        def _():
            dk_ref[...] = keep_k[...]
            dv_ref[...] = keep_v[...]

    blk = (BAND, GROUP_W)
    cur = pl.BlockSpec(blk, lambda r, i: (jnp.minimum(i, nb - 1), r))
    prv = pl.BlockSpec(blk, lambda r, i: (jnp.clip(i - 1, 0, nb - 1), r))
    return pl.pallas_call(
        body, name=f"attn_bwd_d{d}", grid=(d, nb + 1),
        in_specs=[cur, prv, cur, prv, cur, cur, cur, cur], out_specs=[cur, prv, prv],
        out_shape=[jax.ShapeDtypeStruct((sd, d * GROUP_W), F32)] * 3,
        scratch_shapes=[pltpu.VMEM(blk, F32), pltpu.VMEM(blk, F32)],
        compiler_params=_cparams(2),
    )(q, k, k, v, v, dmix, mix, lj)


def _final_loss(x, gvec, tgt):
    S, D = x.shape
    tm = _tile(S, TM)

    def body(x_ref, g_ref, t_ref, dx_ref, red_ref):
        @pl.when(pl.program_id(0) == 0)
        def _():
            red_ref[...] = jnp.zeros_like(red_ref)

        xv = x_ref[...]
        g = g_ref[0:1, :]
        r = _rstd(xv)
        xh = xv * r
        err = xh * g - t_ref[...]
        dy = err * (1.0 / D)
        dxh = dy * g
        dx_ref[...] = r * (dxh - xh * jnp.mean(dxh * xh, axis=-1, keepdims=True))
        red_ref[0:1, :] += jnp.sum(dy * xh, axis=0, keepdims=True)
        red_ref[1:2, :] += jnp.sum(err * err, axis=0, keepdims=True)

    tok = pl.BlockSpec((tm, D), lambda i: (i, 0))
    small = pl.BlockSpec((8, D), lambda i: (0, 0))
    return pl.pallas_call(
        body, name="final_loss", grid=(S // tm,),
        in_specs=[tok, small, tok], out_specs=[tok, small],
        out_shape=[jax.ShapeDtypeStruct((S, D), F32), jax.ShapeDtypeStruct((8, D), F32)],
        compiler_params=_cparams(1),
    )(x, gvec, tgt)


def _adamw(w, g, m, v):
    shape = w.shape
    C = shape[-1]
    R = w.size // C
    tr = 256 if R % 256 == 0 else R

    def body(w_ref, g_ref, m_ref, v_ref, d_ref, nm_ref, nv_ref):
        gv = g_ref[...]
        nm = ADAM_B1 * m_ref[...] + (1.0 - ADAM_B1) * gv
        nv = ADAM_B2 * v_ref[...] + (1.0 - ADAM_B2) * (gv * gv)
        m_hat = nm / (1.0 - ADAM_B1 ** ADAM_STEP)
        v_hat = nv / (1.0 - ADAM_B2 ** ADAM_STEP)
        d_ref[...] = -ADAM_LR * (m_hat / (jnp.sqrt(v_hat) + ADAM_EPS) + ADAM_WD * w_ref[...])
        nm_ref[...] = nm
        nv_ref[...] = nv

    spec = pl.BlockSpec((tr, C), lambda i: (i, 0))
    outs = pl.pallas_call(
        body, name="adamw", grid=(R // tr,),
        in_specs=[spec] * 4, out_specs=[spec] * 3,
        out_shape=[jax.ShapeDtypeStruct((R, C), F32)] * 3,
        compiler_params=_cparams(1),
    )(*(t.reshape(R, C) for t in (w, g, m, v)))
    return tuple(o.reshape(shape) for o in outs)


def _ada_fwd(c_all, w, b):
    L, D, N = w.shape
    tn = 768 if N % 768 == 0 else _tile(N, 512)

    def body(c_ref, w_ref, b_ref, o_ref):
        cv = c_ref[...]
        cond = (cv * jax.nn.sigmoid(cv)).astype(BF)
        o_ref[...] = _dot(cond, w_ref[...].astype(BF)) + b_ref[...]

    return pl.pallas_call(
        body, name="ada_fwd", grid=(L, N // tn),
        in_specs=[pl.BlockSpec((8, D), lambda l, n: (0, 0)), pl.BlockSpec((None, D, tn), lambda l, n: (l, 0, n)),
                  pl.BlockSpec((None, 1, tn), lambda l, n: (l, 0, n))],
        out_specs=pl.BlockSpec((None, 8, tn), lambda l, n: (l, 0, n)),
        out_shape=jax.ShapeDtypeStruct((L, 8, N), F32),
        compiler_params=_cparams(2),
    )(c_all, w, b)


def _ada_wgrad(c_all_t, dm):
    D = c_all_t.shape[0]
    L, _, N = dm.shape
    tn = 256

    def body(c_ref, dm_ref, o_ref):
        cv = c_ref[...]
        cond = cv * jax.nn.sigmoid(cv)
        acc = cond[:, 0:1] * dm_ref[0:1, :]
        for b in range(1, 8):
            acc = acc + cond[:, b:b + 1] * dm_ref[b:b + 1, :]
        o_ref[...] = acc

    return pl.pallas_call(
        body, name="ada_wgrad", grid=(L, N // tn),
        in_specs=[pl.BlockSpec((D, 8), lambda l, n: (0, 0)), pl.BlockSpec((None, 8, tn), lambda l, n: (l, 0, n))],
        out_specs=pl.BlockSpec((None, D, tn), lambda l, n: (l, 0, n)),
        out_shape=jax.ShapeDtypeStruct((L, D, N), F32),
        compiler_params=_cparams(2),
    )(c_all_t, dm)


def _sum8(g):
    _, R, C = g.shape

    def body(g_ref, o_ref):
        acc = g_ref[0]
        for b in range(1, 8):
            acc = acc + g_ref[b]
        o_ref[...] = acc

    return pl.pallas_call(body, name="sum8", out_shape=jax.ShapeDtypeStruct((R, C), F32),
                          in_specs=[pl.BlockSpec(memory_space=pltpu.VMEM)],
                          out_specs=pl.BlockSpec(memory_space=pltpu.VMEM))(g)


def _pair_add(g4, recv, cidx):
    _, _, ha, B = g4.shape
    tr = _row_tile(ha)

    def body(c_ref, g_ref, r_ref, o_ref):
        o_ref[...] = (g_ref[...].astype(F32) + r_ref[...].astype(F32)).astype(BF)

    return pl.pallas_call(
        body, name="pair_add",
        grid_spec=pltpu.PrefetchScalarGridSpec(
            num_scalar_prefetch=1, grid=(4, ha // tr),
            in_specs=[pl.BlockSpec((None, None, tr, B), lambda s, i, c: (s, c[0], i, 0)),
                      pl.BlockSpec((None, tr, B), lambda s, i, c: (s, i, 0))],
            out_specs=pl.BlockSpec((None, tr, B), lambda s, i, c: (s, i, 0))),
        out_shape=jax.ShapeDtypeStruct((4, ha, B), BF),
        compiler_params=_cparams(2),
    )(cidx, g4, recv)


def _sum_partials(part, recv, gbuf, n_layers, l, place):
    _, ha, B = part.shape
    tr = _row_tile(ha)

    def body(pc_ref, p_ref, r_ref, *rest):
        o_ref = rest[-1]
        o_ref[...] = ((p_ref[...].astype(F32) + r_ref[0].astype(F32)) + r_ref[1].astype(F32)) + r_ref[2].astype(F32)

    in_specs = [pl.BlockSpec((None, tr, B), lambda i, pc: (pc[0], i, 0)), pl.BlockSpec((3, tr, B), lambda i, pc: (0, i, 0))]
    ops = [part, recv]
    if gbuf is not None:
        in_specs.append(ANY)
        ops.append(gbuf)
    return pl.pallas_call(
        body, name="sum_partials",
        grid_spec=pltpu.PrefetchScalarGridSpec(
            num_scalar_prefetch=1, grid=(ha // tr,), in_specs=in_specs,
            out_specs=pl.BlockSpec((None, None, tr, B), lambda i, pc: (l, pc[1], i, 0))),
        out_shape=jax.ShapeDtypeStruct((n_layers, 2, ha, B), F32),
        input_output_aliases={} if gbuf is None else {3: 0},
        compiler_params=_cparams(1),
    )(place, *ops)


def _cast_shard(shard, l, place):
    _, _, ha, B = shard.shape
    tr = _row_tile(ha)

    def body(pc_ref, s_ref, o_ref):
        o_ref[...] = s_ref[...].astype(BF)

    return pl.pallas_call(
        body, name="cast_shard",
        grid_spec=pltpu.PrefetchScalarGridSpec(
            num_scalar_prefetch=1, grid=(2, ha // tr),
            in_specs=[pl.BlockSpec((None, None, tr, B), lambda h, i, pc: (l, h, i, 0))],
            out_specs=pl.BlockSpec((None, None, tr, B), lambda h, i, pc: (pc[0], h, i, 0))),
        out_shape=jax.ShapeDtypeStruct((4, 2, ha, B), BF),
        compiler_params=_cparams(2),
    )(place, shard)


def _gather8(v):
    R, C = v.shape

    def body(v_ref, o_ref, ssem, rsem):
        x, y, c, *_ = _place()
        me = 4 * x + 2 * y + c
        o_ref[me] = v_ref[...]
        cps = []
        for k in range(1, 8):
            fx, fy, fc = (k >> 2) & 1, (k >> 1) & 1, k & 1
            peer = (x ^ fx, y ^ fy, c ^ fc)
            cp = pltpu.make_async_remote_copy(src_ref=v_ref, dst_ref=o_ref.at[me], send_sem=ssem.at[k - 1],
                                              recv_sem=rsem.at[k - 1], device_id=peer, device_id_type=MESH)
            cp.start()
            cps.append((cp, 4 * peer[0] + 2 * peer[1] + peer[2]))
        for k, (cp, slot) in enumerate(cps):
            there = o_ref.at[slot]
            pltpu.make_async_remote_copy(src_ref=there, dst_ref=there, send_sem=ssem.at[k], recv_sem=rsem.at[k],
                                         device_id=(x, y, c), device_id_type=MESH).wait_recv()
        for cp, _ in cps:
            cp.wait_send()

    vm = pl.BlockSpec(memory_space=pltpu.VMEM)
    return pl.pallas_call(
        body, name="gather8", in_specs=[vm], out_specs=vm, out_shape=jax.ShapeDtypeStruct((8, R, C), F32),
        scratch_shapes=[pltpu.SemaphoreType.DMA((7,)), pltpu.SemaphoreType.DMA((7,))],
    )(v)


def _mods_to_owner(ms):
    _, R, C = ms.shape

    def body(m_ref, o_ref, ssem, rsem):
        x, y, c, p, chips, qs = _place()
        o_ref[p] = m_ref[2 * p + c]
        cps = []
        for j in range(3):
            cp = pltpu.make_async_remote_copy(src_ref=m_ref.at[2 * qs[j] + c], dst_ref=o_ref.at[p], send_sem=ssem.at[j],
                                              recv_sem=rsem.at[j], device_id=(*chips[j], c), device_id_type=MESH)
            cp.start()
            cps.append(cp)
        for j in range(3):
            there = o_ref.at[qs[j]]
            pltpu.make_async_remote_copy(src_ref=there, dst_ref=there, send_sem=ssem.at[j], recv_sem=rsem.at[j],
                                         device_id=(x, y, c), device_id_type=MESH).wait_recv()
        for cp in cps:
            cp.wait_send()

    vm = pl.BlockSpec(memory_space=pltpu.VMEM)
    return pl.pallas_call(
        body, name="mods_to_owner", in_specs=[vm], out_specs=vm, out_shape=jax.ShapeDtypeStruct((4, R, C), F32),
        scratch_shapes=[pltpu.SemaphoreType.DMA((3,)), pltpu.SemaphoreType.DMA((3,))],
    )(ms)


def _vec(*rows):
    D = rows[0].shape[-1]
    rows = [r.reshape(1, D) for r in rows]
    return jnp.concatenate(rows + [jnp.zeros((8 - len(rows), D), F32)], axis=0)


def _sets(n_layers, n_conv):
    sets = []
    for l in range(n_layers):
        j = l - n_conv
        mixer = [("conv_w_in", l), ("conv_w_out", l)] if l < n_conv else [("attn_w_q", j), ("attn_w_o", j)]
        sets.append([("ffn1_w_in", l), ("ffn1_w_out", l)] + ([("w_kv", 0)] if l == n_conv else []))
        sets.append(mixer + [("ffn2_w_in", l), ("ffn2_w_out", l)])
    return sets


def _as_weight(name, buf):
    _, _, ha, B = buf.shape
    return buf.reshape(8 * ha, B) if name.endswith("w_out") else buf.reshape(4, 2 * ha, B)


def _local_step(x, tgt, tabs, norm_g, conv_w, kv_norm_g, final_norm_g, mods, kvmods, cast, place):
    S, D = x.shape
    n_layers, n_conv = norm_g.shape[0], conv_w.shape[0]
    md = mods.reshape(n_layers, 9, D)

    def vec_of(l, k):
        return _vec(norm_g[l, k], md[l, 3 * k + 1], md[l, 3 * k], md[l, 3 * k + 2])

    W = {}

    def install(items, bufs):
        for (name, idx), b in zip(items, bufs):
            W[name, idx] = _as_weight(name, b)

    Q = _sets(n_layers, n_conv)
    install(Q[0], _comm_call("gather_d2d", _comm_call("gather_ici", [cast[it] for it in Q[0]])))
    arrived = {}

    def ffn_fwd(n, *args):
        ici = [k for k in ([1] if n == 0 else []) + [n + 2] if k < len(Q)]
        d2d = [n + 1] if 2 <= n + 1 < len(Q) else []
        comm = [("gather_ici", [cast[it] for it in Q[k]]) for k in ici] + [("gather_d2d", arrived.pop(k)) for k in d2d]
        xo, h, ab, y, *moved = _ffn_fwd(*args, comm)
        for k in ici:
            arrived[k], moved = moved[:len(Q[k])], moved[len(Q[k]):]
        for k in d2d:
            install(Q[k], moved[:len(Q[k])])
            moved = moved[len(Q[k]):]
        if n == 0:
            install(Q[1], _comm_call("gather_d2d", arrived.pop(1)))
        return xo, h, ab, y

    vec_kv = _vec(kv_norm_g, kvmods[D:], kvmods[:D])
    ng = len(DILATIONS)
    n_rope = ng * GROUP_W // LANES
    saved = {}
    xc = x
    ks = vs = None
    for l in range(n_layers):
        if l == n_conv:
            h_kv, kv = _proj_fwd(xc, vec_kv, W["w_kv", 0], tabs, n_rope, DILATIONS * 2)
            ks, vs = kv[:ng], kv[ng:]
            saved["kv"] = (xc, h_kv)
        v0 = vec_of(l, 0)
        xo, h, ab, y = ffn_fwd(2 * l, xc, v0, W["ffn1_w_in", l], W["ffn1_w_out", l])
        saved[l, 0] = (xc, v0, h, ab, y)
        xc = xo
        v1 = vec_of(l, 1)
        if l < n_conv:
            cw = _vec(conv_w[l, 0], conv_w[l, 1], conv_w[l, 2])
            xo, h, s4, z, y = _conv_fwd(xc, v1, cw, W["conv_w_in", l], W["conv_w_out", l])
            saved[l, 1] = (xc, v1, h, y, cw, s4, z)
        else:
            j = l - n_conv
            h, qs = _proj_fwd(xc, v1, W["attn_w_q", j], tabs, n_rope, DILATIONS, SM_SCALE)
            og = [_attn_fwd(qs[g], ks[g], vs[g], DILATIONS[g]) for g in range(ng)]
            mixs, ljs, y, xo = _attn_out([o for o, _ in og], [s for _, s in og], xc, v1, W["attn_w_o", j])
            saved[l, 1] = (xc, v1, h, y, qs, mixs, ljs)
        xc = xo
        v2 = vec_of(l, 2)
        xo, h, ab, y = ffn_fwd(2 * l + 1, xc, v2, W["ffn2_w_in", l], W["ffn2_w_out", l])
        saved[l, 2] = (xc, v2, h, ab, y)
        xc = xo

    dx, red_f = _final_loss(xc, _vec(final_norm_g), tgt)
    loss_part = 0.5 * jnp.sum(red_f[1]) / D
    G, parts, others = {}, {}, {}
    dmods = [[None] * 9 for _ in range(n_layers)]
    dnorm = [[None] * 3 for _ in range(n_layers)]
    dconvw = [None] * n_conv
    dkv_parts = []

    def halves(items):
        return [G[it].reshape(4, 2, G[it].shape[1] // 2, G[it].shape[2]) for it in items]

    def pair_sums(items, recv):
        for it, g, r in zip(items, halves(items), recv):
            parts[it] = _pair_add(g, r, place[1:])
        return [parts[it] for it in items]

    R = Q[::-1]
    recvd = {}

    def ffn_bwd(dx, l, k, name):
        m = 2 * (n_layers - 1 - l) + (0 if k == 2 else 1)
        ex = [m - 1] if m >= 1 else []
        sc = [m - 2] if m >= 2 else []
        comm = [("exchange", halves(R[e])) for e in ex] + [("scatter", pair_sums(R[e], recvd.pop(e))) for e in sc]
        xin, v, h, ab, y = saved[l, k]
        w_in, w_out = W[name + "_w_in", l], W[name + "_w_out", l]
        dy, u, dab, dh, *moved = _ffn_dgrad(dx, v, ab, w_in, w_out, comm)
        for e in ex:
            recvd[e], moved = moved[:len(R[e])], moved[len(R[e]):]
        for e in sc:
            others.update(zip(R[e], moved[:len(R[e])]))
            moved = moved[len(R[e]):]
        if m == len(R) - 1:
            G[name + "_w_in", l], got = _mm_tn(h, dab, w_in.shape[2], [("scatter", pair_sums(R[m - 1], recvd.pop(m - 1)))])
            others.update(zip(R[m - 1], got))
        else:
            G[name + "_w_in", l] = _mm_tn(h, dab, w_in.shape[2])
        G[name + "_w_out", l] = _mm_tn(u, dy[None], D).reshape(4, -1, D)
        return _norm_bwd(dx, [dh], xin, y, v, 0.5)

    def note(l, k, red):
        dnorm[l][k] = red[0]
        dmods[l][3 * k], dmods[l][3 * k + 1], dmods[l][3 * k + 2] = red[2], red[1], red[3]

    for l in reversed(range(n_layers)):
        dx, red = ffn_bwd(dx, l, 2, "ffn2")
        note(l, 2, red)
        if l < n_conv:
            xin, v, h, y, cw, s4, z = saved[l, 1]
            w_in, w_out = W["conv_w_in", l], W["conv_w_out", l]
            dy, d3, dh, dcw = _conv_dgrad(dx, v, cw, s4, w_in, w_out)
            G["conv_w_in", l] = _mm_tn(h, d3[None], w_in.shape[2])
            G["conv_w_out", l] = _mm_tn(z, dy[None], D).reshape(4, -1, D)
            dconvw[l] = dcw[0:3]
        else:
            j = l - n_conv
            xin, v, h, y, qs, mixs, ljs = saved[l, 1]
            wq, wo = W["attn_w_q", j], W["attn_w_o", j]
            dy, dmixs = _attn_out_dgrad(dx, v, wo)
            G["attn_w_o", j] = _mm_tn(mixs[0], dy[None], wo.shape[2])
            dqkv = [_attn_bwd(qs[g], ks[g], vs[g], dmixs[g], mixs[g], ljs[g], DILATIONS[g]) for g in range(ng)]
            dkv_parts.append([t[1:] for t in dqkv])
            dq, dh = _proj_dgrad([[t[0]] for t in dqkv], tabs, wq, ng, DILATIONS)
            G["attn_w_q", j] = _mm_tn(h, dq[None], wq.shape[2])
        dx, red = _norm_bwd(dx, [dh], xin, y, v, 1.0)
        note(l, 1, red)
        dx, red = ffn_bwd(dx, l, 0, "ffn1")
        note(l, 0, red)
        if l == n_conv:
            xin, h_kv = saved["kv"]
            kv_parts = [[lay[g][0] for lay in dkv_parts] for g in range(ng)] + [[lay[g][1] for lay in dkv_parts] for g in range(ng)]
            dkv, dh = _proj_dgrad(kv_parts, tabs, W["w_kv", 0], ng, DILATIONS * 2)
            G["w_kv", 0] = _mm_tn(h_kv, dkv[None], W["w_kv", 0].shape[2])
            dx, red_kv = _norm_bwd(dx, [dh], xin, None, vec_kv, 1.0)
    others.update(zip(R[-1], _comm_call("scatter", pair_sums(R[-1], _comm_call("exchange", halves(R[-1]))))))
    small = jnp.concatenate(
        [jnp.stack([jnp.stack(r) for r in dmods]).reshape(-1), red_kv[2], red_kv[1],
         jnp.stack([jnp.stack(r) for r in dnorm]).reshape(-1), jnp.stack(dconvw).reshape(-1), red_kv[0], red_f[0]])
    return loss_part, dx, parts, others, small


BIG = ("ffn1_w_in", "ffn1_w_out", "ffn2_w_in", "ffn2_w_out", "conv_w_in", "conv_w_out", "w_kv", "attn_w_q", "attn_w_o")


def _halved(t):
    if t.ndim == 2:
        t = t[None]
    L, A, B = t.shape
    return t.reshape(L, 2, A // 2, B)


def kernel(x, c, positions, norm_g, ada_w, ada_b, ffn1_w_in, ffn1_w_out, ffn2_w_in, ffn2_w_out, conv_w_in, conv_w, conv_w_out, kv_norm_g, kv_ada_w, kv_ada_b, w_kv, attn_w_q, attn_w_o, final_norm_g, loss_target, m_norm_g, m_ada_w, m_ada_b, m_ffn1_w_in, m_ffn1_w_out, m_ffn2_w_in, m_ffn2_w_out, m_conv_w_in, m_conv_w, m_conv_w_out, m_kv_norm_g, m_kv_ada_w, m_kv_ada_b, m_w_kv, m_attn_w_q, m_attn_w_o, m_final_norm_g, v_norm_g, v_ada_w, v_ada_b, v_ffn1_w_in, v_ffn1_w_out, v_ffn2_w_in, v_ffn2_w_out, v_conv_w_in, v_conv_w, v_conv_w_out, v_kv_norm_g, v_kv_ada_w, v_kv_ada_b, v_w_kv, v_attn_w_q, v_attn_w_o, v_final_norm_g):
    wts = dict(norm_g=norm_g, ada_w=ada_w, ada_b=ada_b, ffn1_w_in=ffn1_w_in, ffn1_w_out=ffn1_w_out, ffn2_w_in=ffn2_w_in,
               ffn2_w_out=ffn2_w_out, conv_w_in=conv_w_in, conv_w=conv_w, conv_w_out=conv_w_out, kv_norm_g=kv_norm_g,
               kv_ada_w=kv_ada_w, kv_ada_b=kv_ada_b, w_kv=w_kv, attn_w_q=attn_w_q, attn_w_o=attn_w_o,
               final_norm_g=final_norm_g)
    ms = dict(norm_g=m_norm_g, ada_w=m_ada_w, ada_b=m_ada_b, ffn1_w_in=m_ffn1_w_in, ffn1_w_out=m_ffn1_w_out,
              ffn2_w_in=m_ffn2_w_in, ffn2_w_out=m_ffn2_w_out, conv_w_in=m_conv_w_in, conv_w=m_conv_w,
              conv_w_out=m_conv_w_out, kv_norm_g=m_kv_norm_g, kv_ada_w=m_kv_ada_w, kv_ada_b=m_kv_ada_b, w_kv=m_w_kv,
              attn_w_q=m_attn_w_q, attn_w_o=m_attn_w_o, final_norm_g=m_final_norm_g)
    vs = dict(norm_g=v_norm_g, ada_w=v_ada_w, ada_b=v_ada_b, ffn1_w_in=v_ffn1_w_in, ffn1_w_out=v_ffn1_w_out,
              ffn2_w_in=v_ffn2_w_in, ffn2_w_out=v_ffn2_w_out, conv_w_in=v_conv_w_in, conv_w=v_conv_w,
              conv_w_out=v_conv_w_out, kv_norm_g=v_kv_norm_g, kv_ada_w=v_kv_ada_w, kv_ada_b=v_kv_ada_b, w_kv=v_w_kv,
              attn_w_q=v_attn_w_q, attn_w_o=v_attn_w_o, final_norm_g=v_final_norm_g)
    order = list(wts)
    S, D = x.shape[1], x.shape[2]
    n_layers, n_conv = norm_g.shape[0], conv_w.shape[0]
    nm, sw = ada_w.shape[2], norm_g.shape[2]
    ix, iy, ic = (lax.axis_index(a) for a in AXES)
    chip = 2 * ix + iy

    pack = jnp.concatenate([norm_g.reshape(-1), conv_w.reshape(-1), c.reshape(-1)])
    npad = (-pack.size) % (8 * LANES)
    allp = _gather8(jnp.pad(pack, (0, npad)).reshape(-1, LANES)).reshape(8, -1)
    n1, n2 = norm_g.size, norm_g.size + conv_w.size
    by_chip = allp[0::2]
    norm_full = jnp.moveaxis(by_chip[:, :n1].reshape(4, n_layers, 3, sw), 0, 2).reshape(n_layers, 3, 4 * sw)
    conv_full = jnp.moveaxis(by_chip[:, n1:n2].reshape(4, n_conv, 3, sw), 0, 2).reshape(n_conv, 3, 4 * sw)
    c_all = allp[:, n2:n2 + D]

    b_l = lax.dynamic_slice_in_dim(ada_b, chip * nm, nm, axis=1)
    mod_sh = _ada_fwd(c_all, ada_w, b_l[:, None, :])
    nkv = kv_ada_w.shape[1]
    bkv = lax.dynamic_slice_in_dim(kv_ada_b, chip * nkv, nkv, axis=0)
    kv_sh = _ada_fwd(c_all, kv_ada_w[None], bkv[None, None, :])
    rows = jnp.concatenate([jnp.moveaxis(mod_sh, 0, 1), jnp.pad(jnp.moveaxis(kv_sh, 0, 1), ((0, 0), (0, 0), (0, nm - nkv)))], axis=1)
    rpad = (-rows.shape[1]) % 8
    mine = _mods_to_owner(jnp.pad(rows, ((0, 0), (0, rpad), (0, 0))))
    mods = jnp.moveaxis(mine[:, :n_layers], 0, 1).reshape(n_layers, 4 * nm)
    kvmods = mine[:, n_layers, :nkv].reshape(4 * nkv)

    place = jnp.stack([chip, ic]).astype(jnp.int32)
    halved = {k: _halved(wts[k]) for k in BIG}
    cast = {(k, l): _cast_shard(halved[k], l, place) for k in BIG for l in range(halved[k].shape[0])}

    loss_part, dx, parts, others, small = _local_step(x[0], loss_target[0], _rope_tables(positions), norm_full, conv_full,
                                                      kv_norm_g, final_norm_g, mods, kvmods, cast, place)
    loss = lax.psum(loss_part, AXES)

    gbufs = []
    for k in BIG:
        gbuf, n_l = None, halved[k].shape[0]
        for l in range(n_l):
            gbuf = _sum_partials(parts[k, l], others[k, l], gbuf, n_l, l, place)
        gbufs.append(gbuf)
    grads = {k: s.reshape(wts[k].shape) for k, s in zip(BIG, _comm_call("share", gbufs))}

    spad = (-small.size) % (8 * LANES)
    gath = _gather8(jnp.pad(small, (0, spad)).reshape(-1, LANES))
    tot = _sum8(gath).reshape(-1)
    gath = gath.reshape(8, -1)
    o = 0

    def take(n):
        nonlocal o
        o += n
        return tot[o - n:o]

    g_mods = take(n_layers * 9 * D).reshape(n_layers, 9 * D)
    g_kvmods = take(2 * D)
    g_norm = take(n_layers * 3 * D).reshape(n_layers, 3, D)
    g_convw = take(n_conv * 3 * D).reshape(n_conv, 3, D)
    g_kvn = take(D)
    g_fin = take(D)
    grads["ada_b"] = g_mods
    grads["kv_ada_b"] = g_kvmods
    grads["norm_g"] = lax.dynamic_slice_in_dim(g_norm, chip * sw, sw, axis=2)
    grads["conv_w"] = lax.dynamic_slice_in_dim(g_convw, chip * sw, sw, axis=2)
    grads["kv_norm_g"] = g_kvn
    grads["final_norm_g"] = g_fin

    dm_all = gath[:, :n_layers * 9 * D].reshape(8, n_layers, 9 * D)
    dm_mine = jnp.moveaxis(lax.dynamic_slice_in_dim(dm_all, chip * nm, nm, axis=2), 0, 1)
    dkv_all = gath[:, n_layers * 9 * D:n_layers * 9 * D + 2 * D]
    dkv_mine = lax.dynamic_slice_in_dim(dkv_all, chip * nkv, nkv, axis=1)[None]
    c_t = c_all.T
    grads["ada_w"] = _ada_wgrad(c_t, dm_mine)
    grads["kv_ada_w"] = _ada_wgrad(c_t, dkv_mine)[0]

    deltas, new_m, new_v = {}, {}, {}
    for k in order:
        shp = wts[k].shape
        two_d = (lambda t: t.reshape(1, -1)) if len(shp) == 1 else (lambda t: t)
        dlt, nmk, nvk = _adamw(two_d(wts[k]), two_d(grads[k]), two_d(ms[k]), two_d(vs[k]))
        deltas[k], new_m[k], new_v[k] = dlt.reshape(shp), nmk.reshape(shp), nvk.reshape(shp)
    return (loss, dx[None], *[grads[k] for k in order], *[deltas[k] for k in order], *[new_m[k] for k in order],
            *[new_v[k] for k in order])
```

```python
import functools
import math

import jax
import jax.numpy as jnp
from jax import lax
from jax.experimental import pallas as pl
from jax.experimental.pallas import tpu as pltpu

F32 = jnp.float32
BF = jnp.bfloat16
MESH = pl.DeviceIdType.MESH
AXES = ("x", "y", "c")

NORM_EPS = 1e-5
HEAD_DIM = 64
HEADS_PER_GROUP = 8
GROUP_W = HEADS_PER_GROUP * HEAD_DIM
DILATIONS = (1, 4, 16)
BAND = 128
ROPE_DIM = 16
ROPE_THETA = 500000.0
SM_SCALE = HEAD_DIM ** -0.5
NEG = -1e30
ADAM_LR, ADAM_B1, ADAM_B2, ADAM_EPS, ADAM_WD, ADAM_STEP = 0.001, 0.9, 0.999, 1e-08, 0.01, 10

V7X_VMEM_BYTES = 64 * 1024 * 1024
VMEM_LIMIT = V7X_VMEM_BYTES - 6 * 1024 * 1024
LANES = 128
TM = 512
TM_DGRAD = 256
TK = 1024
WGRAD_COLS = 1536


def _cparams(ngrid):
    return pltpu.CompilerParams(dimension_semantics=("arbitrary",) * ngrid, vmem_limit_bytes=VMEM_LIMIT)


def _dot(a, b):
    return jnp.dot(a, b, preferred_element_type=F32)


def _dot_nt(a, b):
    return lax.dot_general(a, b, (((1,), (1,)), ((), ())), preferred_element_type=F32)


def _dot_tn(a, b):
    return lax.dot_general(a, b, (((0,), (0,)), ((), ())), preferred_element_type=F32)


def _tile(n, pref):
    t = min(n, pref)
    while n % t:
        t //= 2
    return t


def _row_tile(n, cap=256, mult=16):
    best = n
    for t in range(mult, min(n, cap) + 1, mult):
        if n % t == 0:
            best = t
    return best


def _rstd(x):
    return lax.rsqrt(jnp.mean(x * x, axis=-1, keepdims=True) + NORM_EPS)


def _norm_mod(x, v):
    return (x * _rstd(x) * v[0:1]) * (1.0 + v[1:2]) + v[2:3]


def _silu_parts(a):
    sg = jax.nn.sigmoid(a)
    return sg, a * sg


def _row(i):
    return lambda *_: (i, 0)


ANY = pl.BlockSpec(memory_space=pl.ANY)
IN_PLACE = ("gather_ici", "gather_d2d", "share")


def _place():
    x, y, c = (lax.axis_index(a) for a in AXES)
    chips = [(1 - x, y), (x, 1 - y), (1 - x, 1 - y)]
    return x, y, c, 2 * x + y, chips, [2 * cx + cy for cx, cy in chips]


def _transfers(kind, ins, outs):
    x, y, c, p, chips, qs = _place()
    sib = (x, y, 1 - c)
    rows = []
    for k, o in enumerate(outs):
        if kind == "gather_ici":
            rows.append([(o.at[p, c], o.at[p, c], (*chips[j], c), o.at[qs[j], c]) for j in range(3)])
        elif kind == "gather_d2d":
            rows.append([(o.at[qs[j], c], o.at[qs[j], c], sib, o.at[qs[j], 1 - c]) for j in range(3)])
        elif kind == "exchange":
            rows.append([(ins[k].at[:, 1 - c], o, sib, o)])
        elif kind == "scatter":
            rows.append([(ins[k].at[qs[j]], o.at[j], (*chips[j], c), o.at[j]) for j in range(3)])
        elif kind == "share":
            rows.append([(o.at[:, c], o.at[:, c], sib, o.at[:, 1 - c])])
    return rows


def _comm_out_shapes(kind, arrays):
    if kind in IN_PLACE:
        return [jax.ShapeDtypeStruct(a.shape, a.dtype) for a in arrays]
    if kind == "exchange":
        return [jax.ShapeDtypeStruct((4,) + a.shape[2:], a.dtype) for a in arrays]
    return [jax.ShapeDtypeStruct((3,) + a.shape[1:], a.dtype) for a in arrays]


def _comm_sems(n):
    return [pltpu.SemaphoreType.DMA((n, 3)), pltpu.SemaphoreType.DMA((n, 3))]


def _comm_start(rows, ssem, rsem):
    for k, row in enumerate(rows):
        for j, (src, dst, dev, _) in enumerate(row):
            pltpu.make_async_remote_copy(src_ref=src, dst_ref=dst, send_sem=ssem.at[k, j], recv_sem=rsem.at[k, j],
                                         device_id=dev, device_id_type=MESH).start()


def _comm_wait(rows, ssem, rsem):
    for k, row in enumerate(rows):
        for j, (src, dst, dev, land) in enumerate(row):
            pltpu.make_async_remote_copy(src_ref=src, dst_ref=dst, send_sem=ssem.at[k, j], recv_sem=rsem.at[k, j],
                                         device_id=dev, device_id_type=MESH).wait_send()
            pltpu.make_async_remote_copy(src_ref=land, dst_ref=land, send_sem=ssem.at[k, j], recv_sem=rsem.at[k, j],
                                         device_id=dev, device_id_type=MESH).wait_recv()


def _comm_call(kind, arrays):
    n = len(arrays)

    def body(*refs):
        rows = _transfers(kind, refs[:n], refs[n:2 * n])
        _comm_start(rows, *refs[2 * n:])
        _comm_wait(rows, *refs[2 * n:])

    return pl.pallas_call(
        body, name="comm_" + kind, in_specs=[ANY] * n, out_specs=[ANY] * n, out_shape=_comm_out_shapes(kind, arrays),
        input_output_aliases={k: k for k in range(n)} if kind in IN_PLACE else {},
        scratch_shapes=_comm_sems(n),
    )(*arrays)


def _ride(comm, n_in, n_out):
    ride = dict(arrays=[], in_specs=[], out_specs=[], out_shape=[], aliases={}, scratch=[], n=0)
    for kind, arrays in comm or []:
        if kind in IN_PLACE:
            ride["aliases"].update({n_in + ride["n"] + k: n_out + ride["n"] + k for k in range(len(arrays))})
        ride["arrays"] += list(arrays)
        ride["out_shape"] += _comm_out_shapes(kind, arrays)
        ride["scratch"] += _comm_sems(len(arrays))
        ride["n"] += len(arrays)
    ride["in_specs"] = ride["out_specs"] = [ANY] * ride["n"]
    return ride


def _ride_when(comm, ins, outs, sems, cond, action):
    if not comm:
        return

    @pl.when(cond)
    def _():
        at = 0
        for e, (kind, arrays) in enumerate(comm):
            n = len(arrays)
            action(_transfers(kind, ins[at:at + n], outs[at:at + n]), *sems[2 * e:2 * e + 2])
            at += n


def _ffn_fwd(x, vec, w_in4, w_out, comm=None):
    S, D = x.shape
    cs = w_in4.shape[2]
    F = 2 * cs
    tm = _tile(S, TM)
    ni = S // tm
    ride = _ride(comm, 5, 4)
    nc = ride["n"]

    def body(*refs):
        x_ref, vec_ref, wa_ref, wb_ref, wo_ref = refs[:5]
        xo_ref, h_ref, ab_ref, y_ref = refs[5 + nc:9 + nc]
        i = pl.program_id(0)
        riding = (comm, refs[5:5 + nc], refs[9 + nc:9 + 2 * nc], refs[9 + 2 * nc:])
        _ride_when(*riding, i == 0, _comm_start)
        xv = x_ref[...]
        h = _norm_mod(xv, vec_ref[...]).astype(BF)
        h_ref[...] = h
        y = None
        for c in range(2):
            cols = slice(c * cs, (c + 1) * cs)
            a = _dot(h, wa_ref[c])
            b = _dot(h, wb_ref[c])
            ab_ref[0, :, cols] = a.astype(BF)
            ab_ref[1, :, cols] = b.astype(BF)
            _, s = _silu_parts(a)
            part = _dot((s * b).astype(BF), wo_ref[cols, :])
            y = part if y is None else y + part
        y_ref[...] = y.astype(BF)
        xo_ref[...] = xv + (0.5 * (1.0 + vec_ref[3:4, :])) * y
        _ride_when(*riding, i == ni - 1, _comm_wait)

    tok = pl.BlockSpec((tm, D), lambda i: (i, 0))
    once = pl.Buffered(1)
    return pl.pallas_call(
        body, name="ffn_fwd", grid=(ni,),
        in_specs=[tok, pl.BlockSpec((8, D), lambda i: (0, 0)),
                  pl.BlockSpec((2, D, cs), lambda i: (0, 0, 0), pipeline_mode=once),
                  pl.BlockSpec((2, D, cs), lambda i: (1, 0, 0), pipeline_mode=once),
                  pl.BlockSpec((F, D), lambda i: (0, 0), pipeline_mode=once)] + ride["in_specs"],
        out_specs=[tok, tok, pl.BlockSpec((2, tm, F), lambda i: (0, i, 0)), tok] + ride["out_specs"],
        out_shape=[jax.ShapeDtypeStruct((S, D), F32), jax.ShapeDtypeStruct((S, D), BF),
                   jax.ShapeDtypeStruct((2, S, F), BF), jax.ShapeDtypeStruct((S, D), BF)] + ride["out_shape"],
        input_output_aliases=ride["aliases"], scratch_shapes=ride["scratch"],
        compiler_params=_cparams(1),
    )(x, vec, w_in4, w_in4, w_out, *ride["arrays"])


def _norm_bwd_tile(dxo, dh, xv, y, vec_ref, red_ref, coef):
    g = vec_ref[0:1, :]
    sc = vec_ref[1:2, :]
    r = _rstd(xv)
    xh = xv * r
    dhn = dh * (1.0 + sc)
    dxh = dhn * g
    red_ref[0:1, :] += jnp.sum(dhn * xh, axis=0, keepdims=True)
    red_ref[1:2, :] += jnp.sum(dh * (xh * g), axis=0, keepdims=True)
    red_ref[2:3, :] += jnp.sum(dh, axis=0, keepdims=True)
    if y is not None:
        red_ref[3:4, :] += coef * jnp.sum(dxo * y.astype(F32), axis=0, keepdims=True)
    return dxo + r * (dxh - xh * jnp.mean(dxh * xh, axis=-1, keepdims=True))


def _ffn_dgrad(dxo, vec, ab, w_in4, w_out, comm=None):
    S, D = dxo.shape
    cs = w_in4.shape[2]
    F = 2 * cs
    tm = _tile(S, TM_DGRAD)
    ni = S // tm
    ride = _ride(comm, 6, 4)
    nc = ride["n"]

    def body(*refs):
        dxo_ref, vec_ref, ab_ref, wa_ref, wb_ref, wo_ref = refs[:6]
        dy_ref, u_ref, dab_ref, dh_ref = refs[6 + nc:10 + nc]
        i = pl.program_id(0)
        riding = (comm, refs[6:6 + nc], refs[10 + nc:10 + 2 * nc], refs[10 + 2 * nc:])
        _ride_when(*riding, i == 0, _comm_start)
        dy = ((0.5 * (1.0 + vec_ref[3:4, :])) * dxo_ref[...]).astype(BF)
        dy_ref[...] = dy
        dh = None
        for c in range(2):
            cols = slice(c * cs, (c + 1) * cs)
            du = _dot_nt(dy, wo_ref[cols, :])
            a = ab_ref[0, :, cols].astype(F32)
            b = ab_ref[1, :, cols].astype(F32)
            sg, s = _silu_parts(a)
            u_ref[:, cols] = (s * b).astype(BF)
            da = (du * b * (sg * (1.0 + a * (1.0 - sg)))).astype(BF)
            db = (du * s).astype(BF)
            dab_ref[0, :, cols] = da
            dab_ref[1, :, cols] = db
            part = _dot_nt(da, wa_ref[c]) + _dot_nt(db, wb_ref[c])
            dh = part if dh is None else dh + part
        dh_ref[...] = dh.astype(BF)
        _ride_when(*riding, i == ni - 1, _comm_wait)

    tok = pl.BlockSpec((tm, D), lambda i: (i, 0))
    once = pl.Buffered(1)
    return pl.pallas_call(
        body, name="ffn_dgrad", grid=(ni,),
        in_specs=[tok, pl.BlockSpec((8, D), lambda i: (0, 0)),
                  pl.BlockSpec((2, tm, F), lambda i: (0, i, 0)),
                  pl.BlockSpec((2, D, cs), lambda i: (0, 0, 0), pipeline_mode=once),
                  pl.BlockSpec((2, D, cs), lambda i: (1, 0, 0), pipeline_mode=once),
                  pl.BlockSpec((F, D), lambda i: (0, 0), pipeline_mode=once)] + ride["in_specs"],
        out_specs=[tok, pl.BlockSpec((tm, F), lambda i: (i, 0)),
                   pl.BlockSpec((2, tm, F), lambda i: (0, i, 0)), tok] + ride["out_specs"],
        out_shape=[jax.ShapeDtypeStruct((S, D), BF), jax.ShapeDtypeStruct((S, F), BF),
                   jax.ShapeDtypeStruct((2, S, F), BF), jax.ShapeDtypeStruct((S, D), BF)] + ride["out_shape"],
        input_output_aliases=ride["aliases"], scratch_shapes=ride["scratch"],
        compiler_params=_cparams(1),
    )(dxo, vec, ab, w_in4, w_in4, w_out, *ride["arrays"])


def _mm_tn(a, b3, cs, comm=None):
    K, M = a.shape
    G, _, Nb = b3.shape
    N = G * Nb
    if cs >= Nb or Nb % cs:
        ns, tn = 1, math.gcd(cs, Nb)
    else:
        ns = max(s for s in range(1, Nb // cs + 1) if (Nb // cs) % s == 0 and s * cs <= max(WGRAD_COLS, cs))
        tn = ns * cs
    tmo = M if M <= 1024 else M // 2
    tk = _tile(K, TK)
    gm, gn, gk = M // tmo, N // tn, K // tk
    ride = _ride(comm, 2, 1)
    nc = ride["n"]

    def body(*refs):
        a_ref, b_ref = refs[:2]
        o_ref, acc_ref = refs[2 + nc], refs[-1]
        at = [pl.program_id(t) for t in range(3)]
        riding = (comm, refs[2:2 + nc], refs[3 + nc:3 + 2 * nc], refs[3 + 2 * nc:-1])
        _ride_when(*riding, functools.reduce(jnp.logical_and, [t == 0 for t in at]), _comm_start)

        @pl.when(at[2] == 0)
        def _():
            acc_ref[...] = jnp.zeros_like(acc_ref)

        av = a_ref[...].astype(BF)
        if ns == 1:
            acc_ref[...] += _dot_tn(av, b_ref[...].astype(BF))
        else:
            for s in range(ns):
                acc_ref[s] += _dot_tn(av, b_ref[:, s * cs:(s + 1) * cs].astype(BF))

        @pl.when(at[2] == gk - 1)
        def _():
            o_ref[...] = acc_ref[...].astype(BF)

        _ride_when(*riding, functools.reduce(jnp.logical_and, [t == g - 1 for t, g in zip(at, (gm, gn, gk))]), _comm_wait)

    if ns == 1:
        out_spec = pl.BlockSpec((None, tmo, tn), lambda m, n, k: ((n * tn) // cs, m, ((n * tn) % cs) // tn))
        acc = pltpu.VMEM((tmo, tn), F32)
    else:
        out_spec = pl.BlockSpec((ns, tmo, cs), lambda m, n, k: (n, m, 0))
        acc = pltpu.VMEM((ns, tmo, cs), F32)
    g, *moved = pl.pallas_call(
        body, name="wgrad_tn", grid=(gm, gn, gk),
        in_specs=[pl.BlockSpec((tk, tmo), lambda m, n, k: (k, m)),
                  pl.BlockSpec((None, tk, tn), lambda m, n, k: ((n * tn) // Nb, k, ((n * tn) % Nb) // tn))] + ride["in_specs"],
        out_specs=[out_spec] + ride["out_specs"],
        out_shape=[jax.ShapeDtypeStruct((N // cs, M, cs), BF)] + ride["out_shape"],
        input_output_aliases=ride["aliases"], scratch_shapes=ride["scratch"] + [acc],
        compiler_params=_cparams(3),
    )(a, b3, *ride["arrays"])
    return (g, moved) if comm else g


def _norm_bwd(dxo, dhs, x, y, vec, coef):
    S, D = x.shape
    tm = _tile(S, TM)
    nh = len(dhs)
    has_y = y is not None

    def body(*refs):
        dxo_ref = refs[0]
        dh_refs = refs[1:1 + nh]
        x_ref = refs[1 + nh]
        y_ref = refs[2 + nh] if has_y else None
        vec_ref, dx_ref, red_ref = refs[-3:]

        @pl.when(pl.program_id(0) == 0)
        def _():
            red_ref[...] = jnp.zeros_like(red_ref)

        dh = dh_refs[0][...].astype(F32)
        for r in dh_refs[1:]:
            dh = dh + r[...].astype(F32)
        dx_ref[...] = _norm_bwd_tile(dxo_ref[...], dh, x_ref[...], y_ref[...] if has_y else None, vec_ref, red_ref, coef)

    tok = pl.BlockSpec((tm, D), lambda i: (i, 0))
    small = pl.BlockSpec((8, D), lambda i: (0, 0))
    ops = [dxo, *dhs, x] + ([y] if has_y else []) + [vec]
    return pl.pallas_call(
        body, name="norm_bwd", grid=(S // tm,),
        in_specs=[tok] * (len(ops) - 1) + [small],
        out_specs=[tok, small],
        out_shape=[jax.ShapeDtypeStruct((S, D), F32), jax.ShapeDtypeStruct((8, D), F32)],
        compiler_params=_cparams(1),
    )(*ops)


def _conv_fwd(x, vec, cw, w_in4, w_out):
    S, D = x.shape
    cs = w_in4.shape[2]
    tm = _tile(S, TM)

    def body(x_ref, vec_ref, cw_ref, wi_ref, wo_ref, xo_ref, h_ref, s4_ref, z_ref, y_ref, vs_ref):
        @pl.when(pl.program_id(0) == 0)
        def _():
            vs_ref[0:8, :] = jnp.zeros((8, D), F32)

        xv = x_ref[...]
        h = _norm_mod(xv, vec_ref[...]).astype(BF)
        h_ref[...] = h
        bcu = jnp.concatenate([_dot(h, wi_ref[q]) for q in range(4)], axis=1)
        bg, cg, u = bcu[:, :D], bcu[:, D:2 * D], bcu[:, 2 * D:]
        v = cg * u
        vs_ref[8:8 + tm, :] = v
        conv = cw_ref[0:1, :] * vs_ref[pl.ds(6, tm), :] + cw_ref[1:2, :] * vs_ref[pl.ds(7, tm), :] + cw_ref[2:3, :] * v
        vs_ref[0:8, :] = vs_ref[tm:tm + 8, :]
        z = (bg * conv).astype(BF)
        s4_ref[0] = bg.astype(BF)
        s4_ref[1] = cg.astype(BF)
        s4_ref[2] = u.astype(BF)
        s4_ref[3] = conv.astype(BF)
        z_ref[...] = z
        y = _dot(z, wo_ref[...])
        y_ref[...] = y.astype(BF)
        xo_ref[...] = xv + (1.0 + vec_ref[3:4, :]) * y

    tok = pl.BlockSpec((tm, D), lambda i: (i, 0))
    small = pl.BlockSpec((8, D), lambda i: (0, 0))
    return pl.pallas_call(
        body, name="conv_fwd", grid=(S // tm,),
        in_specs=[tok, small, small, pl.BlockSpec((4, D, cs), lambda i: (0, 0, 0)),
                  pl.BlockSpec((D, D), lambda i: (0, 0))],
        out_specs=[tok, tok, pl.BlockSpec((4, tm, D), lambda i: (0, i, 0)), tok, tok],
        out_shape=[jax.ShapeDtypeStruct((S, D), F32), jax.ShapeDtypeStruct((S, D), BF),
                   jax.ShapeDtypeStruct((4, S, D), BF), jax.ShapeDtypeStruct((S, D), BF),
                   jax.ShapeDtypeStruct((S, D), BF)],
        scratch_shapes=[pltpu.VMEM((tm + 8, D), F32)],
        compiler_params=_cparams(1),
    )(x, vec, cw, w_in4, w_out)


def _conv_dgrad(dxo, vec, cw, s4, w_in4, w_out):
    S, D = dxo.shape
    cs = w_in4.shape[2]
    tm = _tile(S, TM)
    nt = S // tm

    def body(dxo_ref, vec_ref, cw_ref, s4_ref, wi_ref, wo_ref, dy_ref, d3_ref, dh_ref, dcw_ref, ds_ref):
        @pl.when(pl.program_id(0) == 0)
        def _():
            ds_ref[tm:tm + 8, :] = jnp.zeros((8, D), F32)
            dcw_ref[...] = jnp.zeros_like(dcw_ref)

        dy = ((1.0 + vec_ref[3:4, :]) * dxo_ref[...]).astype(BF)
        dy_ref[...] = dy
        dz = _dot_nt(dy, wo_ref[...])
        bg = s4_ref[0].astype(F32)
        cg = s4_ref[1].astype(F32)
        u = s4_ref[2].astype(F32)
        conv = s4_ref[3].astype(F32)
        dbg = dz * conv
        dconv = dz * bg
        ds_ref[0:tm, :] = dconv
        d1 = ds_ref[pl.ds(1, tm), :]
        d2 = ds_ref[pl.ds(2, tm), :]
        ds_ref[tm:tm + 8, :] = ds_ref[0:8, :]
        dv = cw_ref[2:3, :] * dconv + cw_ref[1:2, :] * d1 + cw_ref[0:1, :] * d2
        v = cg * u
        dcw_ref[0:1, :] += jnp.sum(d2 * v, axis=0, keepdims=True)
        dcw_ref[1:2, :] += jnp.sum(d1 * v, axis=0, keepdims=True)
        dcw_ref[2:3, :] += jnp.sum(dconv * v, axis=0, keepdims=True)
        dbcu = jnp.concatenate([dbg, dv * u, dv * cg], axis=1).astype(BF)
        d3_ref[...] = dbcu
        dh = _dot_nt(dbcu[:, 0:cs], wi_ref[0])
        for q in range(1, 4):
            dh = dh + _dot_nt(dbcu[:, q * cs:(q + 1) * cs], wi_ref[q])
        dh_ref[...] = dh.astype(BF)

    tok = pl.BlockSpec((tm, D), lambda i: (nt - 1 - i, 0))
    small = pl.BlockSpec((8, D), lambda i: (0, 0))
    return pl.pallas_call(
        body, name="conv_dgrad", grid=(nt,),
        in_specs=[tok, small, small, pl.BlockSpec((4, tm, D), lambda i: (0, nt - 1 - i, 0)),
                  pl.BlockSpec((4, D, cs), lambda i: (0, 0, 0)), pl.BlockSpec((D, D), lambda i: (0, 0))],
        out_specs=[tok, pl.BlockSpec((tm, 3 * D), lambda i: (nt - 1 - i, 0)), tok, small],
        out_shape=[jax.ShapeDtypeStruct((S, D), BF), jax.ShapeDtypeStruct((S, 3 * D), BF),
                   jax.ShapeDtypeStruct((S, D), BF), jax.ShapeDtypeStruct((8, D), F32)],
        scratch_shapes=[pltpu.VMEM((tm + 8, D), F32)],
        compiler_params=_cparams(1),
    )(dxo, vec, cw, s4, w_in4, w_out)


def _rope_tables(positions):
    S = positions.shape[-1]
    inv = ROPE_THETA ** (-jnp.arange(0, ROPE_DIM, 2, dtype=F32) / ROPE_DIM)
    ang = positions.reshape(S, 1).astype(F32) * inv
    cos = jnp.tile(jnp.cos(ang), (1, LANES // 8))
    sin = jnp.tile(jnp.sin(ang), (1, LANES // 8))
    l64 = jnp.arange(LANES) % HEAD_DIM
    return jnp.stack([jnp.where(l64 < ROPE_DIM, cos, 1.0),
                      jnp.where(l64 < ROPE_DIM // 2, -sin, 0.0),
                      jnp.where((l64 >= ROPE_DIM // 2) & (l64 < ROPE_DIM), sin, 0.0)])


def _rope(t, tab_ref):
    return t * tab_ref[0] + pltpu.roll(t, LANES - 8, 1) * tab_ref[1] + pltpu.roll(t, 8, 1) * tab_ref[2]


def _rope_t(d, tab_ref):
    return d * tab_ref[0] + pltpu.roll(d * tab_ref[1], 8, 1) + pltpu.roll(d * tab_ref[2], LANES - 8, 1)


GROUP_CH = GROUP_W // LANES


def _to_dilated(src_ref, c0, dst_ref, d):
    n = src_ref.shape[1]
    for r in range(d):
        for ch in range(GROUP_CH):
            rows = src_ref[c0 + ch] if d == 1 else src_ref.at[c0 + ch][pl.ds(r, n // d, stride=d), :]
            dst_ref[:, r * GROUP_W + ch * LANES:r * GROUP_W + (ch + 1) * LANES] = rows.astype(dst_ref.dtype)


def _from_dilated(val, dst_ref, c0, d):
    n = dst_ref.shape[1]
    for r in range(d):
        for ch in range(GROUP_CH):
            cols = val[:, r * GROUP_W + ch * LANES:r * GROUP_W + (ch + 1) * LANES]
            if d == 1:
                dst_ref[c0 + ch] = cols
            else:
                dst_ref.at[c0 + ch][pl.ds(r, n // d, stride=d), :] = cols


def _group(ref, c0):
    return jnp.concatenate([ref[c0 + ch] for ch in range(GROUP_CH)], axis=1)


def _put_group(ref, c0, val):
    for ch in range(GROUP_CH):
        ref[c0 + ch] = val[:, ch * LANES:(ch + 1) * LANES]


def _dil_spec(tm, d):
    return pl.BlockSpec((tm // d, d * GROUP_W), lambda i: (i, 0))


def _proj_fwd(x, vec, w4, tabs, n_rope, dils, scale=1.0):
    S, D = x.shape
    cs = w4.shape[2]
    N = 4 * cs
    tm = _tile(S, TM)

    def body(x_ref, vec_ref, w_ref, tab_ref, h_ref, *rest):
        outs, acc_ref = rest[:-1], rest[-1]
        h = _norm_mod(x_ref[...], vec_ref[...]).astype(BF)
        h_ref[...] = h
        per = cs // LANES
        for q in range(4):
            acc = _dot(h, w_ref[q])
            if scale != 1.0:
                acc = acc * scale
            cols = [acc[:, ch * LANES:(ch + 1) * LANES] for ch in range(per)]
            for ch, t in enumerate(cols):
                acc_ref[q * per + ch] = _rope(t, tab_ref) if q * per + ch < n_rope else t
        for j, d in enumerate(dils):
            _to_dilated(acc_ref, j * GROUP_CH, outs[j], d)

    h, *outs = pl.pallas_call(
        body, name="proj_fwd", grid=(S // tm,),
        in_specs=[pl.BlockSpec((tm, D), lambda i: (i, 0)), pl.BlockSpec((8, D), lambda i: (0, 0)),
                  pl.BlockSpec((4, D, cs), lambda i: (0, 0, 0)), pl.BlockSpec((3, tm, LANES), lambda i: (0, i, 0))],
        out_specs=[pl.BlockSpec((tm, D), lambda i: (i, 0))] + [_dil_spec(tm, d) for d in dils],
        out_shape=[jax.ShapeDtypeStruct((S, D), BF)] + [jax.ShapeDtypeStruct((S // d, d * GROUP_W), BF) for d in dils],
        scratch_shapes=[pltpu.VMEM((N // LANES, tm, LANES), F32)],
        compiler_params=_cparams(1),
    )(x, vec, w4, tabs)
    return h, outs


def _proj_dgrad(parts, tabs, w4, n_rope_groups, dils):
    cs, D = w4.shape[2], w4.shape[1]
    N = 4 * cs
    S = parts[0][0].shape[0] * dils[0]
    tm = _tile(S, TM)
    flat = [p for grp in parts for p in grp]
    counts = [len(grp) for grp in parts]

    def body(*refs):
        prefs = refs[:len(flat)]
        tab_ref, w_ref, dz_ref, dh_ref, z_ref = refs[len(flat):]
        k = 0
        for j, cnt in enumerate(counts):
            z = prefs[k][...]
            for r in prefs[k + 1:k + cnt]:
                z = z + r[...]
            k += cnt
            _from_dilated(z, z_ref, 0, dils[j])
            if j < n_rope_groups:
                z = jnp.concatenate([_rope_t(z_ref[ch], tab_ref) for ch in range(GROUP_CH)], axis=1)
            else:
                z = _group(z_ref, 0)
            dz_ref[:, j * GROUP_W:(j + 1) * GROUP_W] = z.astype(BF)
        dh = _dot_nt(dz_ref[:, 0:cs], w_ref[0])
        for q in range(1, 4):
            dh = dh + _dot_nt(dz_ref[:, q * cs:(q + 1) * cs], w_ref[q])
        dh_ref[...] = dh.astype(BF)

    return pl.pallas_call(
        body, name="proj_dgrad", grid=(S // tm,),
        in_specs=[_dil_spec(tm, d) for d, cnt in zip(dils, counts) for _ in range(cnt)]
        + [pl.BlockSpec((3, tm, LANES), lambda i: (0, i, 0)), pl.BlockSpec((4, D, cs), lambda i: (0, 0, 0))],
        out_specs=[pl.BlockSpec((tm, N), lambda i: (i, 0)), pl.BlockSpec((tm, D), lambda i: (i, 0))],
        out_shape=[jax.ShapeDtypeStruct((S, N), BF), jax.ShapeDtypeStruct((S, D), BF)],
        scratch_shapes=[pltpu.VMEM((GROUP_CH, tm, LANES), F32)],
        compiler_params=_cparams(1),
    )(*flat, tabs, w4)


def _band_masks():
    qi = lax.broadcasted_iota(jnp.int32, (BAND, BAND), 0)
    kj = lax.broadcasted_iota(jnp.int32, (BAND, BAND), 1)
    return kj <= qi, kj >= qi


def _head(ref, h):
    return ref[:, h * HEAD_DIM:(h + 1) * HEAD_DIM]


def _attn_fwd(q, k, v, d):
    sd = q.shape[0]
    nb = sd // BAND

    def body(q_ref, kp_ref, kc_ref, vp_ref, vc_ref, o_ref, l_ref):
        same, prev = _band_masks()
        prev = jnp.logical_and(prev, pl.program_id(1) > 0)
        heads = range(HEADS_PER_GROUP)
        scores = [(jnp.where(same, _dot_nt(_head(q_ref, h), _head(kc_ref, h)), NEG),
                   jnp.where(prev, _dot_nt(_head(q_ref, h), _head(kp_ref, h)), NEG)) for h in heads]
        probs = []
        for sc, sp in scores:
            m = jnp.maximum(jnp.max(sc, axis=-1, keepdims=True), jnp.max(sp, axis=-1, keepdims=True))
            pc = jnp.exp(sc - m)
            pp = jnp.exp(sp - m)
            den = jnp.sum(pc, axis=-1, keepdims=True) + jnp.sum(pp, axis=-1, keepdims=True)
            probs.append((pc.astype(BF), pp.astype(BF), m, den))
        outs = [(_dot(pc, _head(vc_ref, h)) + _dot(pp, _head(vp_ref, h))) / den for h, (pc, pp, _, den) in zip(heads, probs)]
        o_ref[...] = jnp.concatenate(outs, axis=1)
        l_ref[...] = jnp.concatenate([jnp.broadcast_to(m + jnp.log(den), (BAND, HEAD_DIM)) for _, _, m, den in probs], axis=1)

    blk = (BAND, GROUP_W)
    cur = pl.BlockSpec(blk, lambda r, i: (i, r))
    prv = pl.BlockSpec(blk, lambda r, i: (jnp.maximum(i - 1, 0), r))
    return pl.pallas_call(
        body, name=f"attn_fwd_d{d}", grid=(d, nb),
        in_specs=[cur, prv, cur, prv, cur], out_specs=[cur, cur],
        out_shape=[jax.ShapeDtypeStruct((sd, d * GROUP_W), F32)] * 2,
        compiler_params=_cparams(2),
    )(q, k, k, v, v)


def _attn_out(os, ls, x, vec, wo4):
    S, D = x.shape
    cs = wo4.shape[2]
    tm = _tile(S, TM)
    ng = len(DILATIONS)

    def body(*refs):
        o_refs, l_refs = refs[0:2 * ng:2], refs[1:2 * ng:2]
        x_ref, vec_ref, wo_ref = refs[2 * ng:2 * ng + 3]
        outs = refs[2 * ng + 3:]
        mix_refs, lj_refs = outs[0:2 * ng:2], outs[1:2 * ng:2]
        y_ref, xo_ref, nat_ref = outs[2 * ng:]
        ov, lv = [], []
        for g in range(ng):
            _from_dilated(o_refs[g][...], nat_ref, 2 * g * GROUP_CH, DILATIONS[g])
            _from_dilated(l_refs[g][...], nat_ref, (2 * g + 1) * GROUP_CH, DILATIONS[g])
            ov.append(_group(nat_ref, 2 * g * GROUP_CH))
            lv.append(_group(nat_ref, (2 * g + 1) * GROUP_CH))
        mx = jnp.maximum(jnp.maximum(lv[0], lv[1]), lv[2])
        es = [jnp.exp(t - mx) for t in lv]
        den = es[0] + es[1] + es[2]
        mix = (es[0] * ov[0] + es[1] * ov[1] + es[2] * ov[2]) / den
        at_mix, at_lj = 2 * ng * GROUP_CH, (2 * ng + 1) * GROUP_CH
        _put_group(nat_ref, at_mix, mix)
        _put_group(nat_ref, at_lj, mx + jnp.log(den))
        for g in range(ng):
            _to_dilated(nat_ref, at_mix, mix_refs[g], DILATIONS[g])
            _to_dilated(nat_ref, at_lj, lj_refs[g], DILATIONS[g])
        mb = mix.astype(BF)
        y = jnp.concatenate([_dot(mb, wo_ref[q]) for q in range(4)], axis=1)
        y_ref[...] = y.astype(BF)
        xo_ref[...] = x_ref[...] + (1.0 + vec_ref[3:4, :]) * y

    tok = pl.BlockSpec((tm, D), lambda i: (i, 0))
    dil = [_dil_spec(tm, d) for d in DILATIONS for _ in range(2)]
    dil_shape = [jax.ShapeDtypeStruct((S // d, d * GROUP_W), F32) for d in DILATIONS for _ in range(2)]
    outs = pl.pallas_call(
        body, name="attn_out", grid=(S // tm,),
        in_specs=dil + [tok, pl.BlockSpec((8, D), lambda i: (0, 0)), pl.BlockSpec((4, GROUP_W, cs), lambda i: (0, 0, 0))],
        out_specs=dil + [tok, tok],
        out_shape=dil_shape + [jax.ShapeDtypeStruct((S, D), BF), jax.ShapeDtypeStruct((S, D), F32)],
        scratch_shapes=[pltpu.VMEM(((2 * ng + 2) * GROUP_CH, tm, LANES), F32)],
        compiler_params=_cparams(1),
    )(*[t for pair in zip(os, ls) for t in pair], x, vec, wo4)
    return list(outs[0:2 * ng:2]), list(outs[1:2 * ng:2]), outs[2 * ng], outs[2 * ng + 1]


def _attn_out_dgrad(dxo, vec, wo4):
    S, D = dxo.shape
    cs = wo4.shape[2]
    tm = _tile(S, TM)

    def body(dxo_ref, vec_ref, wo_ref, dy_ref, *rest):
        dm_refs, nat_ref = rest[:-1], rest[-1]
        dy = ((1.0 + vec_ref[3:4, :]) * dxo_ref[...]).astype(BF)
        dy_ref[...] = dy
        dm = _dot_nt(dy[:, 0:cs], wo_ref[0])
        for q in range(1, 4):
            dm = dm + _dot_nt(dy[:, q * cs:(q + 1) * cs], wo_ref[q])
        _put_group(nat_ref, 0, dm)
        for g, d in enumerate(DILATIONS):
            _to_dilated(nat_ref, 0, dm_refs[g], d)

    tok = pl.BlockSpec((tm, D), lambda i: (i, 0))
    dy, *dms = pl.pallas_call(
        body, name="attn_out_dgrad", grid=(S // tm,),
        in_specs=[tok, pl.BlockSpec((8, D), lambda i: (0, 0)), pl.BlockSpec((4, GROUP_W, cs), lambda i: (0, 0, 0))],
        out_specs=[tok] + [_dil_spec(tm, d) for d in DILATIONS],
        out_shape=[jax.ShapeDtypeStruct((S, D), BF)] + [jax.ShapeDtypeStruct((S // d, d * GROUP_W), F32) for d in DILATIONS],
        scratch_shapes=[pltpu.VMEM((GROUP_CH, tm, LANES), F32)],
        compiler_params=_cparams(1),
    )(dxo, vec, wo4)
    return dy, dms


def _attn_bwd(q, k, v, dmix, mix, lj, d):
    sd = q.shape[0]
    nb = sd // BAND

    def body(q_ref, kp_ref, kc_ref, vp_ref, vc_ref, do_ref, o_ref, l_ref, dq_ref, dk_ref, dv_ref, keep_k, keep_v):
        i = pl.program_id(1)

        @pl.when(i == 0)
        def _():
            keep_k[...] = jnp.zeros_like(keep_k)
            keep_v[...] = jnp.zeros_like(keep_v)

        @pl.when(i < nb)
        def _():
            same, prev = _band_masks()
            prev = jnp.logical_and(prev, i > 0)
            heads = range(HEADS_PER_GROUP)
            dob = [_head(do_ref, h).astype(BF) for h in heads]
            raw = [(_dot_nt(_head(q_ref, h), _head(kc_ref, h)), _dot_nt(_head(q_ref, h), _head(kp_ref, h)),
                    _dot_nt(dob[h], _head(vc_ref, h)), _dot_nt(dob[h], _head(vp_ref, h))) for h in heads]
            mid = []
            for h, (sc, sp, dpc, dpp) in zip(heads, raw):
                lrow = l_ref[:, h * HEAD_DIM:h * HEAD_DIM + 1]
                delta = jnp.sum(_head(do_ref, h) * _head(o_ref, h), axis=-1, keepdims=True)
                pc = jnp.exp(jnp.where(same, sc, NEG) - lrow)
                pp = jnp.exp(jnp.where(prev, sp, NEG) - lrow)
                mid.append((pc.astype(BF), pp.astype(BF), (pc * (dpc - delta)).astype(BF), (pp * (dpp - delta)).astype(BF)))
            dq = [(_dot(dsc, _head(kc_ref, h)) + _dot(dsp, _head(kp_ref, h))) * SM_SCALE for h, (_, _, dsc, dsp) in zip(heads, mid)]
            dk_prev = [_dot_tn(dsp, _head(q_ref, h)) for h, (_, _, _, dsp) in zip(heads, mid)]
            dv_prev = [_dot_tn(pp, dob[h]) for h, (_, pp, _, _) in zip(heads, mid)]
            dk_same = [_dot_tn(dsc, _head(q_ref, h)) for h, (_, _, dsc, _) in zip(heads, mid)]
            dv_same = [_dot_tn(pc, dob[h]) for h, (pc, _, _, _) in zip(heads, mid)]
            dq_ref[...] = jnp.concatenate(dq, axis=1)
            dk_ref[...] = keep_k[...] + jnp.concatenate(dk_prev, axis=1)
            dv_ref[...] = keep_v[...] + jnp.concatenate(dv_prev, axis=1)
            keep_k[...] = jnp.concatenate(dk_same, axis=1)
            keep_v[...] = jnp.concatenate(dv_same, axis=1)

        @pl.when(i == nb)
        def _():
            dk_ref[...] = keep_k[...]
            dv_ref[...] = keep_v[...]

    blk = (BAND, GROUP_W)
    cur = pl.BlockSpec(blk, lambda r, i: (jnp.minimum(i, nb - 1), r))
    prv = pl.BlockSpec(blk, lambda r, i: (jnp.clip(i - 1, 0, nb - 1), r))
    return pl.pallas_call(
        body, name=f"attn_bwd_d{d}", grid=(d, nb + 1),
        in_specs=[cur, prv, cur, prv, cur, cur, cur, cur], out_specs=[cur, prv, prv],
        out_shape=[jax.ShapeDtypeStruct((sd, d * GROUP_W), F32)] * 3,
        scratch_shapes=[pltpu.VMEM(blk, F32), pltpu.VMEM(blk, F32)],
        compiler_params=_cparams(2),
    )(q, k, k, v, v, dmix, mix, lj)


def _final_loss(x, gvec, tgt):
    S, D = x.shape
    tm = _tile(S, TM)

    def body(x_ref, g_ref, t_ref, dx_ref, red_ref):
        @pl.when(pl.program_id(0) == 0)
        def _():
            red_ref[...] = jnp.zeros_like(red_ref)

        xv = x_ref[...]
        g = g_ref[0:1, :]
        r = _rstd(xv)
        xh = xv * r
        err = xh * g - t_ref[...]
        dy = err * (1.0 / D)
        dxh = dy * g
        dx_ref[...] = r * (dxh - xh * jnp.mean(dxh * xh, axis=-1, keepdims=True))
        red_ref[0:1, :] += jnp.sum(dy * xh, axis=0, keepdims=True)
        red_ref[1:2, :] += jnp.sum(err * err, axis=0, keepdims=True)

    tok = pl.BlockSpec((tm, D), lambda i: (i, 0))
    small = pl.BlockSpec((8, D), lambda i: (0, 0))
    return pl.pallas_call(
        body, name="final_loss", grid=(S // tm,),
        in_specs=[tok, small, tok], out_specs=[tok, small],
        out_shape=[jax.ShapeDtypeStruct((S, D), F32), jax.ShapeDtypeStruct((8, D), F32)],
        compiler_params=_cparams(1),
    )(x, gvec, tgt)


def _adamw(w, g, m, v):
    shape = w.shape
    C = shape[-1]
    R = w.size // C
    tr = 256 if R % 256 == 0 else R

    def body(w_ref, g_ref, m_ref, v_ref, d_ref, nm_ref, nv_ref):
        gv = g_ref[...]
        nm = ADAM_B1 * m_ref[...] + (1.0 - ADAM_B1) * gv
        nv = ADAM_B2 * v_ref[...] + (1.0 - ADAM_B2) * (gv * gv)
        m_hat = nm / (1.0 - ADAM_B1 ** ADAM_STEP)
        v_hat = nv / (1.0 - ADAM_B2 ** ADAM_STEP)
        d_ref[...] = -ADAM_LR * (m_hat / (jnp.sqrt(v_hat) + ADAM_EPS) + ADAM_WD * w_ref[...])
        nm_ref[...] = nm
        nv_ref[...] = nv

    spec = pl.BlockSpec((tr, C), lambda i: (i, 0))
    outs = pl.pallas_call(
        body, name="adamw", grid=(R // tr,),
        in_specs=[spec] * 4, out_specs=[spec] * 3,
        out_shape=[jax.ShapeDtypeStruct((R, C), F32)] * 3,
        compiler_params=_cparams(1),
    )(*(t.reshape(R, C) for t in (w, g, m, v)))
    return tuple(o.reshape(shape) for o in outs)


def _ada_fwd(c_all, w, b):
    L, D, N = w.shape
    tn = 768 if N % 768 == 0 else _tile(N, 512)

    def body(c_ref, w_ref, b_ref, o_ref):
        cv = c_ref[...]
        cond = (cv * jax.nn.sigmoid(cv)).astype(BF)
        o_ref[...] = _dot(cond, w_ref[...].astype(BF)) + b_ref[...]

    return pl.pallas_call(
        body, name="ada_fwd", grid=(L, N // tn),
        in_specs=[pl.BlockSpec((8, D), lambda l, n: (0, 0)), pl.BlockSpec((None, D, tn), lambda l, n: (l, 0, n)),
                  pl.BlockSpec((None, 1, tn), lambda l, n: (l, 0, n))],
        out_specs=pl.BlockSpec((None, 8, tn), lambda l, n: (l, 0, n)),
        out_shape=jax.ShapeDtypeStruct((L, 8, N), F32),
        compiler_params=_cparams(2),
    )(c_all, w, b)


def _ada_wgrad(c_all_t, dm):
    D = c_all_t.shape[0]
    L, _, N = dm.shape
    tn = 256

    def body(c_ref, dm_ref, o_ref):
        cv = c_ref[...]
        cond = cv * jax.nn.sigmoid(cv)
        acc = cond[:, 0:1] * dm_ref[0:1, :]
        for b in range(1, 8):
            acc = acc + cond[:, b:b + 1] * dm_ref[b:b + 1, :]
        o_ref[...] = acc

    return pl.pallas_call(
        body, name="ada_wgrad", grid=(L, N // tn),
        in_specs=[pl.BlockSpec((D, 8), lambda l, n: (0, 0)), pl.BlockSpec((None, 8, tn), lambda l, n: (l, 0, n))],
        out_specs=pl.BlockSpec((None, D, tn), lambda l, n: (l, 0, n)),
        out_shape=jax.ShapeDtypeStruct((L, D, N), F32),
        compiler_params=_cparams(2),
    )(c_all_t, dm)


def _sum8(g):
    _, R, C = g.shape

    def body(g_ref, o_ref):
        acc = g_ref[0]
        for b in range(1, 8):
            acc = acc + g_ref[b]
        o_ref[...] = acc

    return pl.pallas_call(body, name="sum8", out_shape=jax.ShapeDtypeStruct((R, C), F32),
                          in_specs=[pl.BlockSpec(memory_space=pltpu.VMEM)],
                          out_specs=pl.BlockSpec(memory_space=pltpu.VMEM))(g)


def _pair_add(g4, recv, cidx):
    _, _, ha, B = g4.shape
    tr = _row_tile(ha)

    def body(c_ref, g_ref, r_ref, o_ref):
        o_ref[...] = (g_ref[...].astype(F32) + r_ref[...].astype(F32)).astype(BF)

    return pl.pallas_call(
        body, name="pair_add",
        grid_spec=pltpu.PrefetchScalarGridSpec(
            num_scalar_prefetch=1, grid=(4, ha // tr),
            in_specs=[pl.BlockSpec((None, None, tr, B), lambda s, i, c: (s, c[0], i, 0)),
                      pl.BlockSpec((None, tr, B), lambda s, i, c: (s, i, 0))],
            out_specs=pl.BlockSpec((None, tr, B), lambda s, i, c: (s, i, 0))),
        out_shape=jax.ShapeDtypeStruct((4, ha, B), BF),
        compiler_params=_cparams(2),
    )(cidx, g4, recv)


def _sum_partials(part, recv, gbuf, n_layers, l, place):
    _, ha, B = part.shape
    tr = _row_tile(ha)

    def body(pc_ref, p_ref, r_ref, *rest):
        o_ref = rest[-1]
        o_ref[...] = ((p_ref[...].astype(F32) + r_ref[0].astype(F32)) + r_ref[1].astype(F32)) + r_ref[2].astype(F32)

    in_specs = [pl.BlockSpec((None, tr, B), lambda i, pc: (pc[0], i, 0)), pl.BlockSpec((3, tr, B), lambda i, pc: (0, i, 0))]
    ops = [part, recv]
    if gbuf is not None:
        in_specs.append(ANY)
        ops.append(gbuf)
    return pl.pallas_call(
        body, name="sum_partials",
        grid_spec=pltpu.PrefetchScalarGridSpec(
            num_scalar_prefetch=1, grid=(ha // tr,), in_specs=in_specs,
            out_specs=pl.BlockSpec((None, None, tr, B), lambda i, pc: (l, pc[1], i, 0))),
        out_shape=jax.ShapeDtypeStruct((n_layers, 2, ha, B), F32),
        input_output_aliases={} if gbuf is None else {3: 0},
        compiler_params=_cparams(1),
    )(place, *ops)


def _cast_shard(shard, l, place):
    _, _, ha, B = shard.shape
    tr = _row_tile(ha)

    def body(pc_ref, s_ref, o_ref):
        o_ref[...] = s_ref[...].astype(BF)

    return pl.pallas_call(
        body, name="cast_shard",
        grid_spec=pltpu.PrefetchScalarGridSpec(
            num_scalar_prefetch=1, grid=(2, ha // tr),
            in_specs=[pl.BlockSpec((None, None, tr, B), lambda h, i, pc: (l, h, i, 0))],
            out_specs=pl.BlockSpec((None, None, tr, B), lambda h, i, pc: (pc[0], h, i, 0))),
        out_shape=jax.ShapeDtypeStruct((4, 2, ha, B), BF),
        compiler_params=_cparams(2),
    )(place, shard)


def _gather8(v):
    R, C = v.shape

    def body(v_ref, o_ref, ssem, rsem):
        x, y, c, *_ = _place()
        me = 4 * x + 2 * y + c
        o_ref[me] = v_ref[...]
        cps = []
        for k in range(1, 8):
            fx, fy, fc = (k >> 2) & 1, (k >> 1) & 1, k & 1
            peer = (x ^ fx, y ^ fy, c ^ fc)
            cp = pltpu.make_async_remote_copy(src_ref=v_ref, dst_ref=o_ref.at[me], send_sem=ssem.at[k - 1],
                                              recv_sem=rsem.at[k - 1], device_id=peer, device_id_type=MESH)
            cp.start()
            cps.append((cp, 4 * peer[0] + 2 * peer[1] + peer[2]))
        for k, (cp, slot) in enumerate(cps):
            there = o_ref.at[slot]
            pltpu.make_async_remote_copy(src_ref=there, dst_ref=there, send_sem=ssem.at[k], recv_sem=rsem.at[k],
                                         device_id=(x, y, c), device_id_type=MESH).wait_recv()
        for cp, _ in cps:
            cp.wait_send()

    vm = pl.BlockSpec(memory_space=pltpu.VMEM)
    return pl.pallas_call(
        body, name="gather8", in_specs=[vm], out_specs=vm, out_shape=jax.ShapeDtypeStruct((8, R, C), F32),
        scratch_shapes=[pltpu.SemaphoreType.DMA((7,)), pltpu.SemaphoreType.DMA((7,))],
    )(v)


def _mods_to_owner(ms):
    _, R, C = ms.shape

    def body(m_ref, o_ref, ssem, rsem):
        x, y, c, p, chips, qs = _place()
        o_ref[p] = m_ref[2 * p + c]
        cps = []
        for j in range(3):
            cp = pltpu.make_async_remote_copy(src_ref=m_ref.at[2 * qs[j] + c], dst_ref=o_ref.at[p], send_sem=ssem.at[j],
                                              recv_sem=rsem.at[j], device_id=(*chips[j], c), device_id_type=MESH)
            cp.start()
            cps.append(cp)
        for j in range(3):
            there = o_ref.at[qs[j]]
            pltpu.make_async_remote_copy(src_ref=there, dst_ref=there, send_sem=ssem.at[j], recv_sem=rsem.at[j],
                                         device_id=(x, y, c), device_id_type=MESH).wait_recv()
        for cp in cps:
            cp.wait_send()

    vm = pl.BlockSpec(memory_space=pltpu.VMEM)
    return pl.pallas_call(
        body, name="mods_to_owner", in_specs=[vm], out_specs=vm, out_shape=jax.ShapeDtypeStruct((4, R, C), F32),
        scratch_shapes=[pltpu.SemaphoreType.DMA((3,)), pltpu.SemaphoreType.DMA((3,))],
    )(ms)


def _vec(*rows):
    D = rows[0].shape[-1]
    rows = [r.reshape(1, D) for r in rows]
    return jnp.concatenate(rows + [jnp.zeros((8 - len(rows), D), F32)], axis=0)


def _sets(n_layers, n_conv):
    sets = []
    for l in range(n_layers):
        j = l - n_conv
        mixer = [("conv_w_in", l), ("conv_w_out", l)] if l < n_conv else [("attn_w_q", j), ("attn_w_o", j)]
        sets.append([("ffn1_w_in", l), ("ffn1_w_out", l)] + ([("w_kv", 0)] if l == n_conv else []))
        sets.append(mixer + [("ffn2_w_in", l), ("ffn2_w_out", l)])
    return sets


def _as_weight(name, buf):
    _, _, ha, B = buf.shape
    return buf.reshape(8 * ha, B) if name.endswith("w_out") else buf.reshape(4, 2 * ha, B)


def _local_step(x, tgt, tabs, norm_g, conv_w, kv_norm_g, final_norm_g, mods, kvmods, cast, place):
    S, D = x.shape
    n_layers, n_conv = norm_g.shape[0], conv_w.shape[0]
    md = mods.reshape(n_layers, 9, D)

    def vec_of(l, k):
        return _vec(norm_g[l, k], md[l, 3 * k + 1], md[l, 3 * k], md[l, 3 * k + 2])

    W = {}

    def install(items, bufs):
        for (name, idx), b in zip(items, bufs):
            W[name, idx] = _as_weight(name, b)

    Q = _sets(n_layers, n_conv)
    install(Q[0], _comm_call("gather_d2d", _comm_call("gather_ici", [cast[it] for it in Q[0]])))
    arrived = {}

    def ffn_fwd(n, *args):
        ici = [k for k in ([1] if n == 0 else []) + [n + 2] if k < len(Q)]
        d2d = [n + 1] if 2 <= n + 1 < len(Q) else []
        comm = [("gather_ici", [cast[it] for it in Q[k]]) for k in ici] + [("gather_d2d", arrived.pop(k)) for k in d2d]
        xo, h, ab, y, *moved = _ffn_fwd(*args, comm)
        for k in ici:
            arrived[k], moved = moved[:len(Q[k])], moved[len(Q[k]):]
        for k in d2d:
            install(Q[k], moved[:len(Q[k])])
            moved = moved[len(Q[k]):]
        if n == 0:
            install(Q[1], _comm_call("gather_d2d", arrived.pop(1)))
        return xo, h, ab, y

    vec_kv = _vec(kv_norm_g, kvmods[D:], kvmods[:D])
    ng = len(DILATIONS)
    n_rope = ng * GROUP_W // LANES
    saved = {}
    xc = x
    ks = vs = None
    for l in range(n_layers):
        if l == n_conv:
            h_kv, kv = _proj_fwd(xc, vec_kv, W["w_kv", 0], tabs, n_rope, DILATIONS * 2)
            ks, vs = kv[:ng], kv[ng:]
            saved["kv"] = (xc, h_kv)
        v0 = vec_of(l, 0)
        xo, h, ab, y = ffn_fwd(2 * l, xc, v0, W["ffn1_w_in", l], W["ffn1_w_out", l])
        saved[l, 0] = (xc, v0, h, ab, y)
        xc = xo
        v1 = vec_of(l, 1)
        if l < n_conv:
            cw = _vec(conv_w[l, 0], conv_w[l, 1], conv_w[l, 2])
            xo, h, s4, z, y = _conv_fwd(xc, v1, cw, W["conv_w_in", l], W["conv_w_out", l])
            saved[l, 1] = (xc, v1, h, y, cw, s4, z)
        else:
            j = l - n_conv
            h, qs = _proj_fwd(xc, v1, W["attn_w_q", j], tabs, n_rope, DILATIONS, SM_SCALE)
            og = [_attn_fwd(qs[g], ks[g], vs[g], DILATIONS[g]) for g in range(ng)]
            mixs, ljs, y, xo = _attn_out([o for o, _ in og], [s for _, s in og], xc, v1, W["attn_w_o", j])
            saved[l, 1] = (xc, v1, h, y, qs, mixs, ljs)
        xc = xo
        v2 = vec_of(l, 2)
        xo, h, ab, y = ffn_fwd(2 * l + 1, xc, v2, W["ffn2_w_in", l], W["ffn2_w_out", l])
        saved[l, 2] = (xc, v2, h, ab, y)
        xc = xo

    dx, red_f = _final_loss(xc, _vec(final_norm_g), tgt)
    loss_part = 0.5 * jnp.sum(red_f[1]) / D
    G, parts, others = {}, {}, {}
    dmods = [[None] * 9 for _ in range(n_layers)]
    dnorm = [[None] * 3 for _ in range(n_layers)]
    dconvw = [None] * n_conv
    dkv_parts = []

    def halves(items):
        return [G[it].reshape(4, 2, G[it].shape[1] // 2, G[it].shape[2]) for it in items]

    def pair_sums(items, recv):
        for it, g, r in zip(items, halves(items), recv):
            parts[it] = _pair_add(g, r, place[1:])
        return [parts[it] for it in items]

    R = Q[::-1]
    recvd = {}

    def ffn_bwd(dx, l, k, name):
        m = 2 * (n_layers - 1 - l) + (0 if k == 2 else 1)
        ex = [m - 1] if m >= 1 else []
        sc = [m - 2] if m >= 2 else []
        comm = [("exchange", halves(R[e])) for e in ex] + [("scatter", pair_sums(R[e], recvd.pop(e))) for e in sc]
        xin, v, h, ab, y = saved[l, k]
        w_in, w_out = W[name + "_w_in", l], W[name + "_w_out", l]
        dy, u, dab, dh, *moved = _ffn_dgrad(dx, v, ab, w_in, w_out, comm)
        for e in ex:
            recvd[e], moved = moved[:len(R[e])], moved[len(R[e]):]
        for e in sc:
            others.update(zip(R[e], moved[:len(R[e])]))
            moved = moved[len(R[e]):]
        if m == len(R) - 1:
            G[name + "_w_in", l], got = _mm_tn(h, dab, w_in.shape[2], [("scatter", pair_sums(R[m - 1], recvd.pop(m - 1)))])
            others.update(zip(R[m - 1], got))
        else:
            G[name + "_w_in", l] = _mm_tn(h, dab, w_in.shape[2])
        G[name + "_w_out", l] = _mm_tn(u, dy[None], D).reshape(4, -1, D)
        return _norm_bwd(dx, [dh], xin, y, v, 0.5)

    def note(l, k, red):
        dnorm[l][k] = red[0]
        dmods[l][3 * k], dmods[l][3 * k + 1], dmods[l][3 * k + 2] = red[2], red[1], red[3]

    for l in reversed(range(n_layers)):
        dx, red = ffn_bwd(dx, l, 2, "ffn2")
        note(l, 2, red)
        if l < n_conv:
            xin, v, h, y, cw, s4, z = saved[l, 1]
            w_in, w_out = W["conv_w_in", l], W["conv_w_out", l]
            dy, d3, dh, dcw = _conv_dgrad(dx, v, cw, s4, w_in, w_out)
            G["conv_w_in", l] = _mm_tn(h, d3[None], w_in.shape[2])
            G["conv_w_out", l] = _mm_tn(z, dy[None], D).reshape(4, -1, D)
            dconvw[l] = dcw[0:3]
        else:
            j = l - n_conv
            xin, v, h, y, qs, mixs, ljs = saved[l, 1]
            wq, wo = W["attn_w_q", j], W["attn_w_o", j]
            dy, dmixs = _attn_out_dgrad(dx, v, wo)
            G["attn_w_o", j] = _mm_tn(mixs[0], dy[None], wo.shape[2])
            dqkv = [_attn_bwd(qs[g], ks[g], vs[g], dmixs[g], mixs[g], ljs[g], DILATIONS[g]) for g in range(ng)]
            dkv_parts.append([t[1:] for t in dqkv])
            dq, dh = _proj_dgrad([[t[0]] for t in dqkv], tabs, wq, ng, DILATIONS)
            G["attn_w_q", j] = _mm_tn(h, dq[None], wq.shape[2])
        dx, red = _norm_bwd(dx, [dh], xin, y, v, 1.0)
        note(l, 1, red)
        dx, red = ffn_bwd(dx, l, 0, "ffn1")
        note(l, 0, red)
        if l == n_conv:
            xin, h_kv = saved["kv"]
            kv_parts = [[lay[g][0] for lay in dkv_parts] for g in range(ng)] + [[lay[g][1] for lay in dkv_parts] for g in range(ng)]
            dkv, dh = _proj_dgrad(kv_parts, tabs, W["w_kv", 0], ng, DILATIONS * 2)
            G["w_kv", 0] = _mm_tn(h_kv, dkv[None], W["w_kv", 0].shape[2])
            dx, red_kv = _norm_bwd(dx, [dh], xin, None, vec_kv, 1.0)
    others.update(zip(R[-1], _comm_call("scatter", pair_sums(R[-1], _comm_call("exchange", halves(R[-1]))))))
    small = jnp.concatenate(
        [jnp.stack([jnp.stack(r) for r in dmods]).reshape(-1), red_kv[2], red_kv[1],
         jnp.stack([jnp.stack(r) for r in dnorm]).reshape(-1), jnp.stack(dconvw).reshape(-1), red_kv[0], red_f[0]])
    return loss_part, dx, parts, others, small


BIG = ("ffn1_w_in", "ffn1_w_out", "ffn2_w_in", "ffn2_w_out", "conv_w_in", "conv_w_out", "w_kv", "attn_w_q", "attn_w_o")


def _halved(t):
    if t.ndim == 2:
        t = t[None]
    L, A, B = t.shape
    return t.reshape(L, 2, A // 2, B)


def kernel(x, c, positions, norm_g, ada_w, ada_b, ffn1_w_in, ffn1_w_out, ffn2_w_in, ffn2_w_out, conv_w_in, conv_w, conv_w_out, kv_norm_g, kv_ada_w, kv_ada_b, w_kv, attn_w_q, attn_w_o, final_norm_g, loss_target, m_norm_g, m_ada_w, m_ada_b, m_ffn1_w_in, m_ffn1_w_out, m_ffn2_w_in, m_ffn2_w_out, m_conv_w_in, m_conv_w, m_conv_w_out, m_kv_norm_g, m_kv_ada_w, m_kv_ada_b, m_w_kv, m_attn_w_q, m_attn_w_o, m_final_norm_g, v_norm_g, v_ada_w, v_ada_b, v_ffn1_w_in, v_ffn1_w_out, v_ffn2_w_in, v_ffn2_w_out, v_conv_w_in, v_conv_w, v_conv_w_out, v_kv_norm_g, v_kv_ada_w, v_kv_ada_b, v_w_kv, v_attn_w_q, v_attn_w_o, v_final_norm_g):
    wts = dict(norm_g=norm_g, ada_w=ada_w, ada_b=ada_b, ffn1_w_in=ffn1_w_in, ffn1_w_out=ffn1_w_out, ffn2_w_in=ffn2_w_in,
               ffn2_w_out=ffn2_w_out, conv_w_in=conv_w_in, conv_w=conv_w, conv_w_out=conv_w_out, kv_norm_g=kv_norm_g,
               kv_ada_w=kv_ada_w, kv_ada_b=kv_ada_b, w_kv=w_kv, attn_w_q=attn_w_q, attn_w_o=attn_w_o,
               final_norm_g=final_norm_g)
    ms = dict(norm_g=m_norm_g, ada_w=m_ada_w, ada_b=m_ada_b, ffn1_w_in=m_ffn1_w_in, ffn1_w_out=m_ffn1_w_out,
              ffn2_w_in=m_ffn2_w_in, ffn2_w_out=m_ffn2_w_out, conv_w_in=m_conv_w_in, conv_w=m_conv_w,
              conv_w_out=m_conv_w_out, kv_norm_g=m_kv_norm_g, kv_ada_w=m_kv_ada_w, kv_ada_b=m_kv_ada_b, w_kv=m_w_kv,
              attn_w_q=m_attn_w_q, attn_w_o=m_attn_w_o, final_norm_g=m_final_norm_g)
    vs = dict(norm_g=v_norm_g, ada_w=v_ada_w, ada_b=v_ada_b, ffn1_w_in=v_ffn1_w_in, ffn1_w_out=v_ffn1_w_out,
              ffn2_w_in=v_ffn2_w_in, ffn2_w_out=v_ffn2_w_out, conv_w_in=v_conv_w_in, conv_w=v_conv_w,
              conv_w_out=v_conv_w_out, kv_norm_g=v_kv_norm_g, kv_ada_w=v_kv_ada_w, kv_ada_b=v_kv_ada_b, w_kv=v_w_kv,
              attn_w_q=v_attn_w_q, attn_w_o=v_attn_w_o, final_norm_g=v_final_norm_g)
    order = list(wts)
    S, D = x.shape[1], x.shape[2]
    n_layers, n_conv = norm_g.shape[0], conv_w.shape[0]
    nm, sw = ada_w.shape[2], norm_g.shape[2]
    ix, iy, ic = (lax.axis_index(a) for a in AXES)
    chip = 2 * ix + iy

    pack = jnp.concatenate([norm_g.reshape(-1), conv_w.reshape(-1), c.reshape(-1)])
    npad = (-pack.size) % (8 * LANES)
    allp = _gather8(jnp.pad(pack, (0, npad)).reshape(-1, LANES)).reshape(8, -1)
    n1, n2 = norm_g.size, norm_g.size + conv_w.size
    by_chip = allp[0::2]
    norm_full = jnp.moveaxis(by_chip[:, :n1].reshape(4, n_layers, 3, sw), 0, 2).reshape(n_layers, 3, 4 * sw)
    conv_full = jnp.moveaxis(by_chip[:, n1:n2].reshape(4, n_conv, 3, sw), 0, 2).reshape(n_conv, 3, 4 * sw)
    c_all = allp[:, n2:n2 + D]

    b_l = lax.dynamic_slice_in_dim(ada_b, chip * nm, nm, axis=1)
    mod_sh = _ada_fwd(c_all, ada_w, b_l[:, None, :])
    nkv = kv_ada_w.shape[1]
    bkv = lax.dynamic_slice_in_dim(kv_ada_b, chip * nkv, nkv, axis=0)
    kv_sh = _ada_fwd(c_all, kv_ada_w[None], bkv[None, None, :])
    rows = jnp.concatenate([jnp.moveaxis(mod_sh, 0, 1), jnp.pad(jnp.moveaxis(kv_sh, 0, 1), ((0, 0), (0, 0), (0, nm - nkv)))], axis=1)
    rpad = (-rows.shape[1]) % 8
    mine = _mods_to_owner(jnp.pad(rows, ((0, 0), (0, rpad), (0, 0))))
    mods = jnp.moveaxis(mine[:, :n_layers], 0, 1).reshape(n_layers, 4 * nm)
    kvmods = mine[:, n_layers, :nkv].reshape(4 * nkv)

    place = jnp.stack([chip, ic]).astype(jnp.int32)
    halved = {k: _halved(wts[k]) for k in BIG}
    cast = {(k, l): _cast_shard(halved[k], l, place) for k in BIG for l in range(halved[k].shape[0])}

    loss_part, dx, parts, others, small = _local_step(x[0], loss_target[0], _rope_tables(positions), norm_full, conv_full,
                                                      kv_norm_g, final_norm_g, mods, kvmods, cast, place)
    loss = lax.psum(loss_part, AXES)

    gbufs = []
    for k in BIG:
        gbuf, n_l = None, halved[k].shape[0]
        for l in range(n_l):
            gbuf = _sum_partials(parts[k, l], others[k, l], gbuf, n_l, l, place)
        gbufs.append(gbuf)
    grads = {k: s.reshape(wts[k].shape) for k, s in zip(BIG, _comm_call("share", gbufs))}

    spad = (-small.size) % (8 * LANES)
    gath = _gather8(jnp.pad(small, (0, spad)).reshape(-1, LANES))
    tot = _sum8(gath).reshape(-1)
    gath = gath.reshape(8, -1)
    o = 0

    def take(n):
        nonlocal o
        o += n
        return tot[o - n:o]

    g_mods = take(n_layers * 9 * D).reshape(n_layers, 9 * D)
    g_kvmods = take(2 * D)
    g_norm = take(n_layers * 3 * D).reshape(n_layers, 3, D)
    g_convw = take(n_conv * 3 * D).reshape(n_conv, 3, D)
    g_kvn = take(D)
    g_fin = take(D)
    grads["ada_b"] = g_mods
    grads["kv_ada_b"] = g_kvmods
    grads["norm_g"] = lax.dynamic_slice_in_dim(g_norm, chip * sw, sw, axis=2)
    grads["conv_w"] = lax.dynamic_slice_in_dim(g_convw, chip * sw, sw, axis=2)
    grads["kv_norm_g"] = g_kvn
    grads["final_norm_g"] = g_fin

    dm_all = gath[:, :n_layers * 9 * D].reshape(8, n_layers, 9 * D)
    dm_mine = jnp.moveaxis(lax.dynamic_slice_in_dim(dm_all, chip * nm, nm, axis=2), 0, 1)
    dkv_all = gath[:, n_layers * 9 * D:n_layers * 9 * D + 2 * D]
    dkv_mine = lax.dynamic_slice_in_dim(dkv_all, chip * nkv, nkv, axis=1)[None]
    c_t = c_all.T
    grads["ada_w"] = _ada_wgrad(c_t, dm_mine)
    grads["kv_ada_w"] = _ada_wgrad(c_t, dkv_mine)[0]

    deltas, new_m, new_v = {}, {}, {}
    for k in order:
        shp = wts[k].shape
        two_d = (lambda t: t.reshape(1, -1)) if len(shp) == 1 else (lambda t: t)
        dlt, nmk, nvk = _adamw(two_d(wts[k]), two_d(grads[k]), two_d(ms[k]), two_d(vs[k]))
        deltas[k], new_m[k], new_v[k] = dlt.reshape(shp), nmk.reshape(shp), nvk.reshape(shp)
    return (loss, dx[None], *[grads[k] for k in order], *[deltas[k] for k in order], *[new_m[k] for k in order],
            *[new_v[k] for k in order])
```

```python
import functools
import math

import jax
import jax.numpy as jnp
from jax import lax
from jax.experimental import pallas as pl
from jax.experimental.pallas import tpu as pltpu

F32 = jnp.float32
BF = jnp.bfloat16
MESH = pl.DeviceIdType.MESH
AXES = ("x", "y", "c")

NORM_EPS = 1e-5
HEAD_DIM = 64
HEADS_PER_GROUP = 8
GROUP_W = HEADS_PER_GROUP * HEAD_DIM
DILATIONS = (1, 4, 16)
BAND = 128
ROPE_DIM = 16
ROPE_THETA = 500000.0
SM_SCALE = HEAD_DIM ** -0.5
NEG = -1e30
ADAM_LR, ADAM_B1, ADAM_B2, ADAM_EPS, ADAM_WD, ADAM_STEP = 0.001, 0.9, 0.999, 1e-08, 0.01, 10

V7X_VMEM_BYTES = 64 * 1024 * 1024
VMEM_LIMIT = V7X_VMEM_BYTES - 6 * 1024 * 1024
LANES = 128
TM = 512
TM_DGRAD = 256
TM_STREAM = 1024
TK = 2048
WGRAD_COLS = 1536


def _cparams(ngrid):
    return pltpu.CompilerParams(dimension_semantics=("arbitrary",) * ngrid, vmem_limit_bytes=VMEM_LIMIT)


def _dot(a, b):
    return jnp.dot(a, b, preferred_element_type=F32)


def _dot_nt(a, b):
    return lax.dot_general(a, b, (((1,), (1,)), ((), ())), preferred_element_type=F32)


def _dot_tn(a, b):
    return lax.dot_general(a, b, (((0,), (0,)), ((), ())), preferred_element_type=F32)


def _tile(n, pref):
    t = min(n, pref)
    while n % t:
        t //= 2
    return t


def _row_tile(n, cap=256, mult=16):
    best = n
    for t in range(mult, min(n, cap) + 1, mult):
        if n % t == 0:
            best = t
    return best


def _rstd(x):
    return lax.rsqrt(jnp.mean(x * x, axis=-1, keepdims=True) + NORM_EPS)


def _norm_mod(x, v):
    return (x * _rstd(x) * v[0:1]) * (1.0 + v[1:2]) + v[2:3]


def _silu_parts(a):
    sg = jax.nn.sigmoid(a)
    return sg, a * sg


def _row(i):
    return lambda *_: (i, 0)


ANY = pl.BlockSpec(memory_space=pl.ANY)
IN_PLACE = ("gather_ici", "gather_d2d", "share")


def _place():
    x, y, c = (lax.axis_index(a) for a in AXES)
    chips = [(1 - x, y), (x, 1 - y), (1 - x, 1 - y)]
    return x, y, c, 2 * x + y, chips, [2 * cx + cy for cx, cy in chips]


def _transfers(kind, ins, outs):
    x, y, c, p, chips, qs = _place()
    sib = (x, y, 1 - c)
    rows = []
    for k, o in enumerate(outs):
        if kind == "gather_ici":
            rows.append([(o.at[p, c], o.at[p, c], (*chips[j], c), o.at[qs[j], c]) for j in range(3)])
        elif kind == "gather_d2d":
            rows.append([(o.at[qs[j], c], o.at[qs[j], c], sib, o.at[qs[j], 1 - c]) for j in range(3)])
        elif kind == "exchange":
            rows.append([(ins[k].at[:, 1 - c], o, sib, o)])
        elif kind == "scatter":
            rows.append([(ins[k].at[qs[j]], o.at[j], (*chips[j], c), o.at[j]) for j in range(3)])
        elif kind == "share":
            rows.append([(o.at[:, c], o.at[:, c], sib, o.at[:, 1 - c])])
    return rows


def _comm_out_shapes(kind, arrays):
    if kind in IN_PLACE:
        return [jax.ShapeDtypeStruct(a.shape, a.dtype) for a in arrays]
    if kind == "exchange":
        return [jax.ShapeDtypeStruct((4,) + a.shape[2:], a.dtype) for a in arrays]
    return [jax.ShapeDtypeStruct((3,) + a.shape[1:], a.dtype) for a in arrays]


def _comm_sems(n):
    return [pltpu.SemaphoreType.DMA((n, 3)), pltpu.SemaphoreType.DMA((n, 3))]


def _comm_start(rows, ssem, rsem):
    for k, row in enumerate(rows):
        for j, (src, dst, dev, _) in enumerate(row):
            pltpu.make_async_remote_copy(src_ref=src, dst_ref=dst, send_sem=ssem.at[k, j], recv_sem=rsem.at[k, j],
                                         device_id=dev, device_id_type=MESH).start()


def _comm_wait(rows, ssem, rsem):
    for k, row in enumerate(rows):
        for j, (src, dst, dev, land) in enumerate(row):
            pltpu.make_async_remote_copy(src_ref=src, dst_ref=dst, send_sem=ssem.at[k, j], recv_sem=rsem.at[k, j],
                                         device_id=dev, device_id_type=MESH).wait_send()
            pltpu.make_async_remote_copy(src_ref=land, dst_ref=land, send_sem=ssem.at[k, j], recv_sem=rsem.at[k, j],
                                         device_id=dev, device_id_type=MESH).wait_recv()


def _comm_call(kind, arrays):
    n = len(arrays)

    def body(*refs):
        rows = _transfers(kind, refs[:n], refs[n:2 * n])
        _comm_start(rows, *refs[2 * n:])
        _comm_wait(rows, *refs[2 * n:])

    return pl.pallas_call(
        body, name="comm_" + kind, in_specs=[ANY] * n, out_specs=[ANY] * n, out_shape=_comm_out_shapes(kind, arrays),
        input_output_aliases={k: k for k in range(n)} if kind in IN_PLACE else {},
        scratch_shapes=_comm_sems(n),
    )(*arrays)


def _ride(comm, n_in, n_out):
    ride = dict(arrays=[], in_specs=[], out_specs=[], out_shape=[], aliases={}, scratch=[], n=0)
    for kind, arrays in comm or []:
        if kind in IN_PLACE:
            ride["aliases"].update({n_in + ride["n"] + k: n_out + ride["n"] + k for k in range(len(arrays))})
        ride["arrays"] += list(arrays)
        ride["out_shape"] += _comm_out_shapes(kind, arrays)
        ride["scratch"] += _comm_sems(len(arrays))
        ride["n"] += len(arrays)
    ride["in_specs"] = ride["out_specs"] = [ANY] * ride["n"]
    return ride


def _ride_when(comm, ins, outs, sems, cond, action):
    if not comm:
        return

    @pl.when(cond)
    def _():
        at = 0
        for e, (kind, arrays) in enumerate(comm):
            n = len(arrays)
            action(_transfers(kind, ins[at:at + n], outs[at:at + n]), *sems[2 * e:2 * e + 2])
            at += n


def _ffn_fwd(x, vec, w_in4, w_out, comm=None):
    S, D = x.shape
    cs = w_in4.shape[2]
    F = 2 * cs
    tm = _tile(S, TM)
    ni = S // tm
    ride = _ride(comm, 5, 4)
    nc = ride["n"]

    def body(*refs):
        x_ref, vec_ref, wa_ref, wb_ref, wo_ref = refs[:5]
        xo_ref, h_ref, ab_ref, y_ref = refs[5 + nc:9 + nc]
        i = pl.program_id(0)
        riding = (comm, refs[5:5 + nc], refs[9 + nc:9 + 2 * nc], refs[9 + 2 * nc:])
        _ride_when(*riding, i == 0, _comm_start)
        xv = x_ref[...]
        h = _norm_mod(xv, vec_ref[...]).astype(BF)
        h_ref[...] = h
        y = None
        for c in range(2):
            cols = slice(c * cs, (c + 1) * cs)
            a = _dot(h, wa_ref[c])
            b = _dot(h, wb_ref[c])
            ab_ref[0, :, cols] = a.astype(BF)
            ab_ref[1, :, cols] = b.astype(BF)
            _, s = _silu_parts(a)
            part = _dot((s * b).astype(BF), wo_ref[cols, :])
            y = part if y is None else y + part
        y_ref[...] = y.astype(BF)
        xo_ref[...] = xv + (0.5 * (1.0 + vec_ref[3:4, :])) * y
        _ride_when(*riding, i == ni - 1, _comm_wait)

    tok = pl.BlockSpec((tm, D), lambda i: (i, 0))
    once = pl.Buffered(1)
    return pl.pallas_call(
        body, name="ffn_fwd", grid=(ni,),
        in_specs=[tok, pl.BlockSpec((8, D), lambda i: (0, 0)),
                  pl.BlockSpec((2, D, cs), lambda i: (0, 0, 0), pipeline_mode=once),
                  pl.BlockSpec((2, D, cs), lambda i: (1, 0, 0), pipeline_mode=once),
                  pl.BlockSpec((F, D), lambda i: (0, 0), pipeline_mode=once)] + ride["in_specs"],
        out_specs=[tok, tok, pl.BlockSpec((2, tm, F), lambda i: (0, i, 0)), tok] + ride["out_specs"],
        out_shape=[jax.ShapeDtypeStruct((S, D), F32), jax.ShapeDtypeStruct((S, D), BF),
                   jax.ShapeDtypeStruct((2, S, F), BF), jax.ShapeDtypeStruct((S, D), BF)] + ride["out_shape"],
        input_output_aliases=ride["aliases"], scratch_shapes=ride["scratch"],
        compiler_params=_cparams(1),
    )(x, vec, w_in4, w_in4, w_out, *ride["arrays"])


def _norm_bwd_tile(dxo, dh, xv, y, vec_ref, red_ref, coef):
    g = vec_ref[0:1, :]
    sc = vec_ref[1:2, :]
    r = _rstd(xv)
    xh = xv * r
    dhn = dh * (1.0 + sc)
    dxh = dhn * g
    red_ref[0:1, :] += jnp.sum(dhn * xh, axis=0, keepdims=True)
    red_ref[1:2, :] += jnp.sum(dh * (xh * g), axis=0, keepdims=True)
    red_ref[2:3, :] += jnp.sum(dh, axis=0, keepdims=True)
    if y is not None:
        red_ref[3:4, :] += coef * jnp.sum(dxo * y.astype(F32), axis=0, keepdims=True)
    return dxo + r * (dxh - xh * jnp.mean(dxh * xh, axis=-1, keepdims=True))


def _ffn_dgrad(dxo, vec, ab, w_in4, w_out, comm=None):
    S, D = dxo.shape
    cs = w_in4.shape[2]
    F = 2 * cs
    tm = _tile(S, TM_DGRAD)
    ni = S // tm
    ride = _ride(comm, 6, 4)
    nc = ride["n"]

    def body(*refs):
        dxo_ref, vec_ref, ab_ref, wa_ref, wb_ref, wo_ref = refs[:6]
        dy_ref, u_ref, dab_ref, dh_ref = refs[6 + nc:10 + nc]
        i = pl.program_id(0)
        riding = (comm, refs[6:6 + nc], refs[10 + nc:10 + 2 * nc], refs[10 + 2 * nc:])
        _ride_when(*riding, i == 0, _comm_start)
        dy = ((0.5 * (1.0 + vec_ref[3:4, :])) * dxo_ref[...]).astype(BF)
        dy_ref[...] = dy
        dh = None
        for c in range(2):
            cols = slice(c * cs, (c + 1) * cs)
            du = _dot_nt(dy, wo_ref[cols, :])
            a = ab_ref[0, :, cols].astype(F32)
            b = ab_ref[1, :, cols].astype(F32)
            sg, s = _silu_parts(a)
            u_ref[:, cols] = (s * b).astype(BF)
            da = (du * b * (sg * (1.0 + a * (1.0 - sg)))).astype(BF)
            db = (du * s).astype(BF)
            dab_ref[0, :, cols] = da
            dab_ref[1, :, cols] = db
            part = _dot_nt(da, wa_ref[c]) + _dot_nt(db, wb_ref[c])
            dh = part if dh is None else dh + part
        dh_ref[...] = dh.astype(BF)
        _ride_when(*riding, i == ni - 1, _comm_wait)

    tok = pl.BlockSpec((tm, D), lambda i: (i, 0))
    once = pl.Buffered(1)
    return pl.pallas_call(
        body, name="ffn_dgrad", grid=(ni,),
        in_specs=[tok, pl.BlockSpec((8, D), lambda i: (0, 0)),
                  pl.BlockSpec((2, tm, F), lambda i: (0, i, 0)),
                  pl.BlockSpec((2, D, cs), lambda i: (0, 0, 0), pipeline_mode=once),
                  pl.BlockSpec((2, D, cs), lambda i: (1, 0, 0), pipeline_mode=once),
                  pl.BlockSpec((F, D), lambda i: (0, 0), pipeline_mode=once)] + ride["in_specs"],
        out_specs=[tok, pl.BlockSpec((tm, F), lambda i: (i, 0)),
                   pl.BlockSpec((2, tm, F), lambda i: (0, i, 0)), tok] + ride["out_specs"],
        out_shape=[jax.ShapeDtypeStruct((S, D), BF), jax.ShapeDtypeStruct((S, F), BF),
                   jax.ShapeDtypeStruct((2, S, F), BF), jax.ShapeDtypeStruct((S, D), BF)] + ride["out_shape"],
        input_output_aliases=ride["aliases"], scratch_shapes=ride["scratch"],
        compiler_params=_cparams(1),
    )(dxo, vec, ab, w_in4, w_in4, w_out, *ride["arrays"])


def _mm_tn(a, b3, cs, comm=None):
    K, M = a.shape
    G, _, Nb = b3.shape
    N = G * Nb
    if cs >= Nb or Nb % cs:
        ns, tn = 1, math.gcd(cs, Nb)
    else:
        ns = max(s for s in range(1, Nb // cs + 1) if (Nb // cs) % s == 0 and s * cs <= max(WGRAD_COLS, cs))
        tn = ns * cs
    tmo = M if M <= 1024 else M // 2
    tk = _tile(K, TK)
    gm, gn, gk = M // tmo, N // tn, K // tk
    ride = _ride(comm, 2, 1)
    nc = ride["n"]

    def body(*refs):
        a_ref, b_ref = refs[:2]
        o_ref, acc_ref = refs[2 + nc], refs[-1]
        at = [pl.program_id(t) for t in range(3)]
        riding = (comm, refs[2:2 + nc], refs[3 + nc:3 + 2 * nc], refs[3 + 2 * nc:-1])
        _ride_when(*riding, functools.reduce(jnp.logical_and, [t == 0 for t in at]), _comm_start)

        @pl.when(at[2] == 0)
        def _():
            acc_ref[...] = jnp.zeros_like(acc_ref)

        av = a_ref[...].astype(BF)
        if ns == 1:
            acc_ref[...] += _dot_tn(av, b_ref[...].astype(BF))
        else:
            for s in range(ns):
                acc_ref[s] += _dot_tn(av, b_ref[:, s * cs:(s + 1) * cs].astype(BF))

        @pl.when(at[2] == gk - 1)
        def _():
            o_ref[...] = acc_ref[...].astype(BF)

        _ride_when(*riding, functools.reduce(jnp.logical_and, [t == g - 1 for t, g in zip(at, (gm, gn, gk))]), _comm_wait)

    if ns == 1:
        out_spec = pl.BlockSpec((None, tmo, tn), lambda m, n, k: ((n * tn) // cs, m, ((n * tn) % cs) // tn))
        acc = pltpu.VMEM((tmo, tn), F32)
    else:
        out_spec = pl.BlockSpec((ns, tmo, cs), lambda m, n, k: (n, m, 0))
        acc = pltpu.VMEM((ns, tmo, cs), F32)
    g, *moved = pl.pallas_call(
        body, name="wgrad_tn", grid=(gm, gn, gk),
        in_specs=[pl.BlockSpec((tk, tmo), lambda m, n, k: (k, m)),
                  pl.BlockSpec((None, tk, tn), lambda m, n, k: ((n * tn) // Nb, k, ((n * tn) % Nb) // tn))] + ride["in_specs"],
        out_specs=[out_spec] + ride["out_specs"],
        out_shape=[jax.ShapeDtypeStruct((N // cs, M, cs), BF)] + ride["out_shape"],
        input_output_aliases=ride["aliases"], scratch_shapes=ride["scratch"] + [acc],
        compiler_params=_cparams(3),
    )(a, b3, *ride["arrays"])
    return (g, moved) if comm else g


def _norm_bwd(dxo, dhs, x, y, vec, coef):
    S, D = x.shape
    tm = _tile(S, TM_STREAM)
    nh = len(dhs)
    has_y = y is not None

    def body(*refs):
        dxo_ref = refs[0]
        dh_refs = refs[1:1 + nh]
        x_ref = refs[1 + nh]
        y_ref = refs[2 + nh] if has_y else None
        vec_ref, dx_ref, red_ref = refs[-3:]

        @pl.when(pl.program_id(0) == 0)
        def _():
            red_ref[...] = jnp.zeros_like(red_ref)

        dh = dh_refs[0][...].astype(F32)
        for r in dh_refs[1:]:
            dh = dh + r[...].astype(F32)
        dx_ref[...] = _norm_bwd_tile(dxo_ref[...], dh, x_ref[...], y_ref[...] if has_y else None, vec_ref, red_ref, coef)

    tok = pl.BlockSpec((tm, D), lambda i: (i, 0))
    small = pl.BlockSpec((8, D), lambda i: (0, 0))
    ops = [dxo, *dhs, x] + ([y] if has_y else []) + [vec]
    return pl.pallas_call(
        body, name="norm_bwd", grid=(S // tm,),
        in_specs=[tok] * (len(ops) - 1) + [small],
        out_specs=[tok, small],
        out_shape=[jax.ShapeDtypeStruct((S, D), F32), jax.ShapeDtypeStruct((8, D), F32)],
        compiler_params=_cparams(1),
    )(*ops)


def _conv_fwd(x, vec, cw, w_in4, w_out):
    S, D = x.shape
    cs = w_in4.shape[2]
    tm = _tile(S, TM)

    def body(x_ref, vec_ref, cw_ref, wi_ref, wo_ref, xo_ref, h_ref, s4_ref, z_ref, y_ref, vs_ref):
        @pl.when(pl.program_id(0) == 0)
        def _():
            vs_ref[0:8, :] = jnp.zeros((8, D), F32)

        xv = x_ref[...]
        h = _norm_mod(xv, vec_ref[...]).astype(BF)
        h_ref[...] = h
        bcu = jnp.concatenate([_dot(h, wi_ref[q]) for q in range(4)], axis=1)
        bg, cg, u = bcu[:, :D], bcu[:, D:2 * D], bcu[:, 2 * D:]
        v = cg * u
        vs_ref[8:8 + tm, :] = v
        conv = cw_ref[0:1, :] * vs_ref[pl.ds(6, tm), :] + cw_ref[1:2, :] * vs_ref[pl.ds(7, tm), :] + cw_ref[2:3, :] * v
        vs_ref[0:8, :] = vs_ref[tm:tm + 8, :]
        z = (bg * conv).astype(BF)
        s4_ref[0] = bg.astype(BF)
        s4_ref[1] = cg.astype(BF)
        s4_ref[2] = u.astype(BF)
        s4_ref[3] = conv.astype(BF)
        z_ref[...] = z
        y = _dot(z, wo_ref[...])
        y_ref[...] = y.astype(BF)
        xo_ref[...] = xv + (1.0 + vec_ref[3:4, :]) * y

    tok = pl.BlockSpec((tm, D), lambda i: (i, 0))
    small = pl.BlockSpec((8, D), lambda i: (0, 0))
    return pl.pallas_call(
        body, name="conv_fwd", grid=(S // tm,),
        in_specs=[tok, small, small, pl.BlockSpec((4, D, cs), lambda i: (0, 0, 0)),
                  pl.BlockSpec((D, D), lambda i: (0, 0))],
        out_specs=[tok, tok, pl.BlockSpec((4, tm, D), lambda i: (0, i, 0)), tok, tok],
        out_shape=[jax.ShapeDtypeStruct((S, D), F32), jax.ShapeDtypeStruct((S, D), BF),
                   jax.ShapeDtypeStruct((4, S, D), BF), jax.ShapeDtypeStruct((S, D), BF),
                   jax.ShapeDtypeStruct((S, D), BF)],
        scratch_shapes=[pltpu.VMEM((tm + 8, D), F32)],
        compiler_params=_cparams(1),
    )(x, vec, cw, w_in4, w_out)


def _conv_dgrad(dxo, vec, cw, s4, w_in4, w_out):
    S, D = dxo.shape
    cs = w_in4.shape[2]
    tm = _tile(S, TM)
    nt = S // tm

    def body(dxo_ref, vec_ref, cw_ref, s4_ref, wi_ref, wo_ref, dy_ref, d3_ref, dh_ref, dcw_ref, ds_ref):
        @pl.when(pl.program_id(0) == 0)
        def _():
            ds_ref[tm:tm + 8, :] = jnp.zeros((8, D), F32)
            dcw_ref[...] = jnp.zeros_like(dcw_ref)

        dy = ((1.0 + vec_ref[3:4, :]) * dxo_ref[...]).astype(BF)
        dy_ref[...] = dy
        dz = _dot_nt(dy, wo_ref[...])
        bg = s4_ref[0].astype(F32)
        cg = s4_ref[1].astype(F32)
        u = s4_ref[2].astype(F32)
        conv = s4_ref[3].astype(F32)
        dbg = dz * conv
        dconv = dz * bg
        ds_ref[0:tm, :] = dconv
        d1 = ds_ref[pl.ds(1, tm), :]
        d2 = ds_ref[pl.ds(2, tm), :]
        ds_ref[tm:tm + 8, :] = ds_ref[0:8, :]
        dv = cw_ref[2:3, :] * dconv + cw_ref[1:2, :] * d1 + cw_ref[0:1, :] * d2
        v = cg * u
        dcw_ref[0:1, :] += jnp.sum(d2 * v, axis=0, keepdims=True)
        dcw_ref[1:2, :] += jnp.sum(d1 * v, axis=0, keepdims=True)
        dcw_ref[2:3, :] += jnp.sum(dconv * v, axis=0, keepdims=True)
        dbcu = jnp.concatenate([dbg, dv * u, dv * cg], axis=1).astype(BF)
        d3_ref[...] = dbcu
        dh = _dot_nt(dbcu[:, 0:cs], wi_ref[0])
        for q in range(1, 4):
            dh = dh + _dot_nt(dbcu[:, q * cs:(q + 1) * cs], wi_ref[q])
        dh_ref[...] = dh.astype(BF)

    tok = pl.BlockSpec((tm, D), lambda i: (nt - 1 - i, 0))
    small = pl.BlockSpec((8, D), lambda i: (0, 0))
    return pl.pallas_call(
        body, name="conv_dgrad", grid=(nt,),
        in_specs=[tok, small, small, pl.BlockSpec((4, tm, D), lambda i: (0, nt - 1 - i, 0)),
                  pl.BlockSpec((4, D, cs), lambda i: (0, 0, 0)), pl.BlockSpec((D, D), lambda i: (0, 0))],
        out_specs=[tok, pl.BlockSpec((tm, 3 * D), lambda i: (nt - 1 - i, 0)), tok, small],
        out_shape=[jax.ShapeDtypeStruct((S, D), BF), jax.ShapeDtypeStruct((S, 3 * D), BF),
                   jax.ShapeDtypeStruct((S, D), BF), jax.ShapeDtypeStruct((8, D), F32)],
        scratch_shapes=[pltpu.VMEM((tm + 8, D), F32)],
        compiler_params=_cparams(1),
    )(dxo, vec, cw, s4, w_in4, w_out)


def _rope_tables(positions):
    S = positions.shape[-1]
    inv = ROPE_THETA ** (-jnp.arange(0, ROPE_DIM, 2, dtype=F32) / ROPE_DIM)
    ang = positions.reshape(S, 1).astype(F32) * inv
    cos = jnp.tile(jnp.cos(ang), (1, LANES // 8))
    sin = jnp.tile(jnp.sin(ang), (1, LANES // 8))
    l64 = jnp.arange(LANES) % HEAD_DIM
    return jnp.stack([jnp.where(l64 < ROPE_DIM, cos, 1.0),
                      jnp.where(l64 < ROPE_DIM // 2, -sin, 0.0),
                      jnp.where((l64 >= ROPE_DIM // 2) & (l64 < ROPE_DIM), sin, 0.0)])


def _rope(t, tab_ref):
    return t * tab_ref[0] + pltpu.roll(t, LANES - 8, 1) * tab_ref[1] + pltpu.roll(t, 8, 1) * tab_ref[2]


def _rope_t(d, tab_ref):
    return d * tab_ref[0] + pltpu.roll(d * tab_ref[1], 8, 1) + pltpu.roll(d * tab_ref[2], LANES - 8, 1)


GROUP_CH = GROUP_W // LANES


def _to_dilated(src_ref, c0, dst_ref, d):
    n = src_ref.shape[1]
    for r in range(d):
        for ch in range(GROUP_CH):
            rows = src_ref[c0 + ch] if d == 1 else src_ref.at[c0 + ch][pl.ds(r, n // d, stride=d), :]
            dst_ref[:, r * GROUP_W + ch * LANES:r * GROUP_W + (ch + 1) * LANES] = rows.astype(dst_ref.dtype)


def _from_dilated(val, dst_ref, c0, d):
    n = dst_ref.shape[1]
    for r in range(d):
        for ch in range(GROUP_CH):
            cols = val[:, r * GROUP_W + ch * LANES:r * GROUP_W + (ch + 1) * LANES]
            if d == 1:
                dst_ref[c0 + ch] = cols
            else:
                dst_ref.at[c0 + ch][pl.ds(r, n // d, stride=d), :] = cols


def _group(ref, c0):
    return jnp.concatenate([ref[c0 + ch] for ch in range(GROUP_CH)], axis=1)


def _put_group(ref, c0, val):
    for ch in range(GROUP_CH):
        ref[c0 + ch] = val[:, ch * LANES:(ch + 1) * LANES]


def _dil_spec(tm, d):
    return pl.BlockSpec((tm // d, d * GROUP_W), lambda i: (i, 0))


def _proj_fwd(x, vec, w4, tabs, n_rope, dils, scale=1.0):
    S, D = x.shape
    cs = w4.shape[2]
    N = 4 * cs
    tm = _tile(S, TM)

    def body(x_ref, vec_ref, w_ref, tab_ref, h_ref, *rest):
        outs, acc_ref = rest[:-1], rest[-1]
        h = _norm_mod(x_ref[...], vec_ref[...]).astype(BF)
        h_ref[...] = h
        per = cs // LANES
        for q in range(4):
            acc = _dot(h, w_ref[q])
            if scale != 1.0:
                acc = acc * scale
            cols = [acc[:, ch * LANES:(ch + 1) * LANES] for ch in range(per)]
            for ch, t in enumerate(cols):
                acc_ref[q * per + ch] = _rope(t, tab_ref) if q * per + ch < n_rope else t
        for j, d in enumerate(dils):
            _to_dilated(acc_ref, j * GROUP_CH, outs[j], d)

    h, *outs = pl.pallas_call(
        body, name="proj_fwd", grid=(S // tm,),
        in_specs=[pl.BlockSpec((tm, D), lambda i: (i, 0)), pl.BlockSpec((8, D), lambda i: (0, 0)),
                  pl.BlockSpec((4, D, cs), lambda i: (0, 0, 0)), pl.BlockSpec((3, tm, LANES), lambda i: (0, i, 0))],
        out_specs=[pl.BlockSpec((tm, D), lambda i: (i, 0))] + [_dil_spec(tm, d) for d in dils],
        out_shape=[jax.ShapeDtypeStruct((S, D), BF)] + [jax.ShapeDtypeStruct((S // d, d * GROUP_W), BF) for d in dils],
        scratch_shapes=[pltpu.VMEM((N // LANES, tm, LANES), F32)],
        compiler_params=_cparams(1),
    )(x, vec, w4, tabs)
    return h, outs


def _proj_dgrad(parts, tabs, w4, n_rope_groups, dils):
    cs, D = w4.shape[2], w4.shape[1]
    N = 4 * cs
    S = parts[0][0].shape[0] * dils[0]
    tm = _tile(S, TM)
    flat = [p for grp in parts for p in grp]
    counts = [len(grp) for grp in parts]

    def body(*refs):
        prefs = refs[:len(flat)]
        tab_ref, w_ref, dz_ref, dh_ref, z_ref = refs[len(flat):]
        k = 0
        for j, cnt in enumerate(counts):
            z = prefs[k][...]
            for r in prefs[k + 1:k + cnt]:
                z = z + r[...]
            k += cnt
            _from_dilated(z, z_ref, 0, dils[j])
            if j < n_rope_groups:
                z = jnp.concatenate([_rope_t(z_ref[ch], tab_ref) for ch in range(GROUP_CH)], axis=1)
            else:
                z = _group(z_ref, 0)
            dz_ref[:, j * GROUP_W:(j + 1) * GROUP_W] = z.astype(BF)
        dh = _dot_nt(dz_ref[:, 0:cs], w_ref[0])
        for q in range(1, 4):
            dh = dh + _dot_nt(dz_ref[:, q * cs:(q + 1) * cs], w_ref[q])
        dh_ref[...] = dh.astype(BF)

    return pl.pallas_call(
        body, name="proj_dgrad", grid=(S // tm,),
        in_specs=[_dil_spec(tm, d) for d, cnt in zip(dils, counts) for _ in range(cnt)]
        + [pl.BlockSpec((3, tm, LANES), lambda i: (0, i, 0)), pl.BlockSpec((4, D, cs), lambda i: (0, 0, 0))],
        out_specs=[pl.BlockSpec((tm, N), lambda i: (i, 0)), pl.BlockSpec((tm, D), lambda i: (i, 0))],
        out_shape=[jax.ShapeDtypeStruct((S, N), BF), jax.ShapeDtypeStruct((S, D), BF)],
        scratch_shapes=[pltpu.VMEM((GROUP_CH, tm, LANES), F32)],
        compiler_params=_cparams(1),
    )(*flat, tabs, w4)


def _band_masks():
    qi = lax.broadcasted_iota(jnp.int32, (BAND, BAND), 0)
    kj = lax.broadcasted_iota(jnp.int32, (BAND, BAND), 1)
    return kj <= qi, kj >= qi


def _head(ref, h):
    return ref[:, h * HEAD_DIM:(h + 1) * HEAD_DIM]


def _attn_fwd(q, k, v, d):
    sd = q.shape[0]
    nb = sd // BAND

    def body(q_ref, kp_ref, kc_ref, vp_ref, vc_ref, o_ref, l_ref):
        same, prev = _band_masks()
        prev = jnp.logical_and(prev, pl.program_id(1) > 0)
        heads = range(HEADS_PER_GROUP)
        scores = [(jnp.where(same, _dot_nt(_head(q_ref, h), _head(kc_ref, h)), NEG),
                   jnp.where(prev, _dot_nt(_head(q_ref, h), _head(kp_ref, h)), NEG)) for h in heads]
        probs = []
        for sc, sp in scores:
            m = jnp.maximum(jnp.max(sc, axis=-1, keepdims=True), jnp.max(sp, axis=-1, keepdims=True))
            pc = jnp.exp(sc - m)
            pp = jnp.exp(sp - m)
            den = jnp.sum(pc, axis=-1, keepdims=True) + jnp.sum(pp, axis=-1, keepdims=True)
            probs.append((pc.astype(BF), pp.astype(BF), m, den))
        outs = [(_dot(pc, _head(vc_ref, h)) + _dot(pp, _head(vp_ref, h))) / den for h, (pc, pp, _, den) in zip(heads, probs)]
        o_ref[...] = jnp.concatenate(outs, axis=1)
        l_ref[...] = jnp.concatenate([jnp.broadcast_to(m + jnp.log(den), (BAND, HEAD_DIM)) for _, _, m, den in probs], axis=1)

    blk = (BAND, GROUP_W)
    cur = pl.BlockSpec(blk, lambda r, i: (i, r))
    prv = pl.BlockSpec(blk, lambda r, i: (jnp.maximum(i - 1, 0), r))
    return pl.pallas_call(
        body, name=f"attn_fwd_d{d}", grid=(d, nb),
        in_specs=[cur, prv, cur, prv, cur], out_specs=[cur, cur],
        out_shape=[jax.ShapeDtypeStruct((sd, d * GROUP_W), F32)] * 2,
        compiler_params=_cparams(2),
    )(q, k, k, v, v)


def _attn_out(os, ls, x, vec, wo4):
    S, D = x.shape
    cs = wo4.shape[2]
    tm = _tile(S, TM)
    ng = len(DILATIONS)

    def body(*refs):
        o_refs, l_refs = refs[0:2 * ng:2], refs[1:2 * ng:2]
        x_ref, vec_ref, wo_ref = refs[2 * ng:2 * ng + 3]
        outs = refs[2 * ng + 3:]
        mix_refs, lj_refs = outs[0:2 * ng:2], outs[1:2 * ng:2]
        y_ref, xo_ref, nat_ref = outs[2 * ng:]
        ov, lv = [], []
        for g in range(ng):
            _from_dilated(o_refs[g][...], nat_ref, 2 * g * GROUP_CH, DILATIONS[g])
            _from_dilated(l_refs[g][...], nat_ref, (2 * g + 1) * GROUP_CH, DILATIONS[g])
            ov.append(_group(nat_ref, 2 * g * GROUP_CH))
            lv.append(_group(nat_ref, (2 * g + 1) * GROUP_CH))
        mx = jnp.maximum(jnp.maximum(lv[0], lv[1]), lv[2])
        es = [jnp.exp(t - mx) for t in lv]
        den = es[0] + es[1] + es[2]
        mix = (es[0] * ov[0] + es[1] * ov[1] + es[2] * ov[2]) / den
        at_mix, at_lj = 2 * ng * GROUP_CH, (2 * ng + 1) * GROUP_CH
        _put_group(nat_ref, at_mix, mix)
        _put_group(nat_ref, at_lj, mx + jnp.log(den))
        for g in range(ng):
            _to_dilated(nat_ref, at_mix, mix_refs[g], DILATIONS[g])
            _to_dilated(nat_ref, at_lj, lj_refs[g], DILATIONS[g])
        mb = mix.astype(BF)
        y = jnp.concatenate([_dot(mb, wo_ref[q]) for q in range(4)], axis=1)
        y_ref[...] = y.astype(BF)
        xo_ref[...] = x_ref[...] + (1.0 + vec_ref[3:4, :]) * y

    tok = pl.BlockSpec((tm, D), lambda i: (i, 0))
    dil = [_dil_spec(tm, d) for d in DILATIONS for _ in range(2)]
    dil_shape = [jax.ShapeDtypeStruct((S // d, d * GROUP_W), F32) for d in DILATIONS for _ in range(2)]
    outs = pl.pallas_call(
        body, name="attn_out", grid=(S // tm,),
        in_specs=dil + [tok, pl.BlockSpec((8, D), lambda i: (0, 0)), pl.BlockSpec((4, GROUP_W, cs), lambda i: (0, 0, 0))],
        out_specs=dil + [tok, tok],
        out_shape=dil_shape + [jax.ShapeDtypeStruct((S, D), BF), jax.ShapeDtypeStruct((S, D), F32)],
        scratch_shapes=[pltpu.VMEM(((2 * ng + 2) * GROUP_CH, tm, LANES), F32)],
        compiler_params=_cparams(1),
    )(*[t for pair in zip(os, ls) for t in pair], x, vec, wo4)
    return list(outs[0:2 * ng:2]), list(outs[1:2 * ng:2]), outs[2 * ng], outs[2 * ng + 1]


def _attn_out_dgrad(dxo, vec, wo4):
    S, D = dxo.shape
    cs = wo4.shape[2]
    tm = _tile(S, TM)

    def body(dxo_ref, vec_ref, wo_ref, dy_ref, *rest):
        dm_refs, nat_ref = rest[:-1], rest[-1]
        dy = ((1.0 + vec_ref[3:4, :]) * dxo_ref[...]).astype(BF)
        dy_ref[...] = dy
        dm = _dot_nt(dy[:, 0:cs], wo_ref[0])
        for q in range(1, 4):
            dm = dm + _dot_nt(dy[:, q * cs:(q + 1) * cs], wo_ref[q])
        _put_group(nat_ref, 0, dm)
        for g, d in enumerate(DILATIONS):
            _to_dilated(nat_ref, 0, dm_refs[g], d)

    tok = pl.BlockSpec((tm, D), lambda i: (i, 0))
    dy, *dms = pl.pallas_call(
        body, name="attn_out_dgrad", grid=(S // tm,),
        in_specs=[tok, pl.BlockSpec((8, D), lambda i: (0, 0)), pl.BlockSpec((4, GROUP_W, cs), lambda i: (0, 0, 0))],
        out_specs=[tok] + [_dil_spec(tm, d) for d in DILATIONS],
        out_shape=[jax.ShapeDtypeStruct((S, D), BF)] + [jax.ShapeDtypeStruct((S // d, d * GROUP_W), F32) for d in DILATIONS],
        scratch_shapes=[pltpu.VMEM((GROUP_CH, tm, LANES), F32)],
        compiler_params=_cparams(1),
    )(dxo, vec, wo4)
    return dy, dms


def _attn_bwd(q, k, v, dmix, mix, lj, d):
    sd = q.shape[0]
    nb = sd // BAND

    def body(q_ref, kp_ref, kc_ref, vp_ref, vc_ref, do_ref, o_ref, l_ref, dq_ref, dk_ref, dv_ref, keep_k, keep_v):
        i = pl.program_id(1)

        @pl.when(i == 0)
        def _():
            keep_k[...] = jnp.zeros_like(keep_k)
            keep_v[...] = jnp.zeros_like(keep_v)

        @pl.when(i < nb)
        def _():
            same, prev = _band_masks()
            prev = jnp.logical_and(prev, i > 0)
            heads = range(HEADS_PER_GROUP)
            dob = [_head(do_ref, h).astype(BF) for h in heads]
            raw = [(_dot_nt(_head(q_ref, h), _head(kc_ref, h)), _dot_nt(_head(q_ref, h), _head(kp_ref, h)),
                    _dot_nt(dob[h], _head(vc_ref, h)), _dot_nt(dob[h], _head(vp_ref, h))) for h in heads]
            mid = []
            for h, (sc, sp, dpc, dpp) in zip(heads, raw):
                lrow = l_ref[:, h * HEAD_DIM:h * HEAD_DIM + 1]
                delta = jnp.sum(_head(do_ref, h) * _head(o_ref, h), axis=-1, keepdims=True)
                pc = jnp.exp(jnp.where(same, sc, NEG) - lrow)
                pp = jnp.exp(jnp.where(prev, sp, NEG) - lrow)
                mid.append((pc.astype(BF), pp.astype(BF), (pc * (dpc - delta)).astype(BF), (pp * (dpp - delta)).astype(BF)))
            dq = [(_dot(dsc, _head(kc_ref, h)) + _dot(dsp, _head(kp_ref, h))) * SM_SCALE for h, (_, _, dsc, dsp) in zip(heads, mid)]
            dk_prev = [_dot_tn(dsp, _head(q_ref, h)) for h, (_, _, _, dsp) in zip(heads, mid)]
            dv_prev = [_dot_tn(pp, dob[h]) for h, (_, pp, _, _) in zip(heads, mid)]
            dk_same = [_dot_tn(dsc, _head(q_ref, h)) for h, (_, _, dsc, _) in zip(heads, mid)]
            dv_same = [_dot_tn(pc, dob[h]) for h, (pc, _, _, _) in zip(heads, mid)]
            dq_ref[...] = jnp.concatenate(dq, axis=1)
            dk_ref[...] = keep_k[...] + jnp.concatenate(dk_prev, axis=1)
            dv_ref[...] = keep_v[...] + jnp.concatenate(dv_prev, axis=1)
            keep_k[...] = jnp.concatenate(dk_same, axis=1)
            keep_v[...] = jnp.concatenate(dv_same, axis=1)

        @pl.when(i == nb)
        def _():
            dk_ref[...] = keep_k[...]
            dv_ref[...] = keep_v[...]

    blk = (BAND, GROUP_W)
    cur = pl.BlockSpec(blk, lambda r, i: (jnp.minimum(i, nb - 1), r))
    prv = pl.BlockSpec(blk, lambda r, i: (jnp.clip(i - 1, 0, nb - 1), r))
    return pl.pallas_call(
        body, name=f"attn_bwd_d{d}", grid=(d, nb + 1),
        in_specs=[cur, prv, cur, prv, cur, cur, cur, cur], out_specs=[cur, prv, prv],
        out_shape=[jax.ShapeDtypeStruct((sd, d * GROUP_W), F32)] * 3,
        scratch_shapes=[pltpu.VMEM(blk, F32), pltpu.VMEM(blk, F32)],
        compiler_params=_cparams(2),
    )(q, k, k, v, v, dmix, mix, lj)


def _final_loss(x, gvec, tgt):
    S, D = x.shape
    tm = _tile(S, TM)

    def body(x_ref, g_ref, t_ref, dx_ref, red_ref):
        @pl.when(pl.program_id(0) == 0)
        def _():
            red_ref[...] = jnp.zeros_like(red_ref)

        xv = x_ref[...]
        g = g_ref[0:1, :]
        r = _rstd(xv)
        xh = xv * r
        err = xh * g - t_ref[...]
        dy = err * (1.0 / D)
        dxh = dy * g
        dx_ref[...] = r * (dxh - xh * jnp.mean(dxh * xh, axis=-1, keepdims=True))
        red_ref[0:1, :] += jnp.sum(dy * xh, axis=0, keepdims=True)
        red_ref[1:2, :] += jnp.sum(err * err, axis=0, keepdims=True)

    tok = pl.BlockSpec((tm, D), lambda i: (i, 0))
    small = pl.BlockSpec((8, D), lambda i: (0, 0))
    return pl.pallas_call(
        body, name="final_loss", grid=(S // tm,),
        in_specs=[tok, small, tok], out_specs=[tok, small],
        out_shape=[jax.ShapeDtypeStruct((S, D), F32), jax.ShapeDtypeStruct((8, D), F32)],
        compiler_params=_cparams(1),
    )(x, gvec, tgt)


def _adamw(w, g, m, v, comm=None):
    shape = w.shape
    C = shape[-1]
    R = w.size // C
    tr = 256 if R % 256 == 0 else R
    steps = R // tr
    ride = _ride(comm, 4, 3)
    nc = ride["n"]

    def body(*refs):
        w_ref, g_ref, m_ref, v_ref = refs[:4]
        d_ref, nm_ref, nv_ref = refs[4 + nc:7 + nc]
        riding = (comm, refs[4:4 + nc], refs[7 + nc:7 + 2 * nc], refs[7 + 2 * nc:])
        _ride_when(*riding, pl.program_id(0) == 0, _comm_start)
        gv = g_ref[...]
        nm = ADAM_B1 * m_ref[...] + (1.0 - ADAM_B1) * gv
        nv = ADAM_B2 * v_ref[...] + (1.0 - ADAM_B2) * (gv * gv)
        m_hat = nm / (1.0 - ADAM_B1 ** ADAM_STEP)
        v_hat = nv / (1.0 - ADAM_B2 ** ADAM_STEP)
        d_ref[...] = -ADAM_LR * (m_hat / (jnp.sqrt(v_hat) + ADAM_EPS) + ADAM_WD * w_ref[...])
        nm_ref[...] = nm
        nv_ref[...] = nv
        _ride_when(*riding, pl.program_id(0) == steps - 1, _comm_wait)

    spec = pl.BlockSpec((tr, C), lambda i: (i, 0))
    d, nm, nv, *moved = pl.pallas_call(
        body, name="adamw", grid=(steps,),
        in_specs=[spec] * 4 + ride["in_specs"], out_specs=[spec] * 3 + ride["out_specs"],
        out_shape=[jax.ShapeDtypeStruct((R, C), F32)] * 3 + ride["out_shape"],
        input_output_aliases=ride["aliases"], scratch_shapes=ride["scratch"],
        compiler_params=_cparams(1),
    )(*(t.reshape(R, C) for t in (w, g, m, v)), *ride["arrays"])
    res = tuple(o.reshape(shape) for o in (d, nm, nv))
    return (*res, moved) if comm else res


def _ada_fwd(c_all, w, b):
    L, D, N = w.shape
    tn = 768 if N % 768 == 0 else _tile(N, 512)

    def body(c_ref, w_ref, b_ref, o_ref):
        cv = c_ref[...]
        cond = (cv * jax.nn.sigmoid(cv)).astype(BF)
        o_ref[...] = _dot(cond, w_ref[...].astype(BF)) + b_ref[...]

    return pl.pallas_call(
        body, name="ada_fwd", grid=(L, N // tn),
        in_specs=[pl.BlockSpec((8, D), lambda l, n: (0, 0)), pl.BlockSpec((None, D, tn), lambda l, n: (l, 0, n)),
                  pl.BlockSpec((None, 1, tn), lambda l, n: (l, 0, n))],
        out_specs=pl.BlockSpec((None, 8, tn), lambda l, n: (l, 0, n)),
        out_shape=jax.ShapeDtypeStruct((L, 8, N), F32),
        compiler_params=_cparams(2),
    )(c_all, w, b)


def _ada_wgrad(c_all_t, dm):
    D = c_all_t.shape[0]
    L, _, N = dm.shape
    tn = 256

    def body(c_ref, dm_ref, o_ref):
        cv = c_ref[...]
        cond = cv * jax.nn.sigmoid(cv)
        acc = cond[:, 0:1] * dm_ref[0:1, :]
        for b in range(1, 8):
            acc = acc + cond[:, b:b + 1] * dm_ref[b:b + 1, :]
        o_ref[...] = acc

    return pl.pallas_call(
        body, name="ada_wgrad", grid=(L, N // tn),
        in_specs=[pl.BlockSpec((D, 8), lambda l, n: (0, 0)), pl.BlockSpec((None, 8, tn), lambda l, n: (l, 0, n))],
        out_specs=pl.BlockSpec((None, D, tn), lambda l, n: (l, 0, n)),
        out_shape=jax.ShapeDtypeStruct((L, D, N), F32),
        compiler_params=_cparams(2),
    )(c_all_t, dm)


def _sum8(g):
    _, R, C = g.shape

    def body(g_ref, o_ref):
        acc = g_ref[0]
        for b in range(1, 8):
            acc = acc + g_ref[b]
        o_ref[...] = acc

    return pl.pallas_call(body, name="sum8", out_shape=jax.ShapeDtypeStruct((R, C), F32),
                          in_specs=[pl.BlockSpec(memory_space=pltpu.VMEM)],
                          out_specs=pl.BlockSpec(memory_space=pltpu.VMEM))(g)


def _pair_add(g4, recv, cidx):
    _, _, ha, B = g4.shape
    tr = _row_tile(ha)

    def body(c_ref, g_ref, r_ref, o_ref):
        o_ref[...] = (g_ref[...].astype(F32) + r_ref[...].astype(F32)).astype(BF)

    return pl.pallas_call(
        body, name="pair_add",
        grid_spec=pltpu.PrefetchScalarGridSpec(
            num_scalar_prefetch=1, grid=(4, ha // tr),
            in_specs=[pl.BlockSpec((None, None, tr, B), lambda s, i, c: (s, c[0], i, 0)),
                      pl.BlockSpec((None, tr, B), lambda s, i, c: (s, i, 0))],
            out_specs=pl.BlockSpec((None, tr, B), lambda s, i, c: (s, i, 0))),
        out_shape=jax.ShapeDtypeStruct((4, ha, B), BF),
        compiler_params=_cparams(2),
    )(cidx, g4, recv)


def _sum_partials(part, recv, gbuf, n_layers, l, place):
    _, ha, B = part.shape
    tr = _row_tile(ha)

    def body(pc_ref, p_ref, r_ref, *rest):
        o_ref = rest[-1]
        o_ref[...] = ((p_ref[...].astype(F32) + r_ref[0].astype(F32)) + r_ref[1].astype(F32)) + r_ref[2].astype(F32)

    in_specs = [pl.BlockSpec((None, tr, B), lambda i, pc: (pc[0], i, 0)), pl.BlockSpec((3, tr, B), lambda i, pc: (0, i, 0))]
    ops = [part, recv]
    if gbuf is not None:
        in_specs.append(ANY)
        ops.append(gbuf)
    return pl.pallas_call(
        body, name="sum_partials",
        grid_spec=pltpu.PrefetchScalarGridSpec(
            num_scalar_prefetch=1, grid=(ha // tr,), in_specs=in_specs,
            out_specs=pl.BlockSpec((None, None, tr, B), lambda i, pc: (l, pc[1], i, 0))),
        out_shape=jax.ShapeDtypeStruct((n_layers, 2, ha, B), F32),
        input_output_aliases={} if gbuf is None else {3: 0},
        compiler_params=_cparams(1),
    )(place, *ops)


def _cast_shard(shard, l, place):
    _, _, ha, B = shard.shape
    tr = _row_tile(ha)

    def body(pc_ref, s_ref, o_ref):
        o_ref[...] = s_ref[...].astype(BF)

    return pl.pallas_call(
        body, name="cast_shard",
        grid_spec=pltpu.PrefetchScalarGridSpec(
            num_scalar_prefetch=1, grid=(2, ha // tr),
            in_specs=[pl.BlockSpec((None, None, tr, B), lambda h, i, pc: (l, h, i, 0))],
            out_specs=pl.BlockSpec((None, None, tr, B), lambda h, i, pc: (pc[0], h, i, 0))),
        out_shape=jax.ShapeDtypeStruct((4, 2, ha, B), BF),
        compiler_params=_cparams(2),
    )(place, shard)


def _gather8(v):
    R, C = v.shape

    def body(v_ref, o_ref, ssem, rsem):
        x, y, c, *_ = _place()
        me = 4 * x + 2 * y + c
        o_ref[me] = v_ref[...]
        cps = []
        for k in range(1, 8):
            fx, fy, fc = (k >> 2) & 1, (k >> 1) & 1, k & 1
            peer = (x ^ fx, y ^ fy, c ^ fc)
            cp = pltpu.make_async_remote_copy(src_ref=v_ref, dst_ref=o_ref.at[me], send_sem=ssem.at[k - 1],
                                              recv_sem=rsem.at[k - 1], device_id=peer, device_id_type=MESH)
            cp.start()
            cps.append((cp, 4 * peer[0] + 2 * peer[1] + peer[2]))
        for k, (cp, slot) in enumerate(cps):
            there = o_ref.at[slot]
            pltpu.make_async_remote_copy(src_ref=there, dst_ref=there, send_sem=ssem.at[k], recv_sem=rsem.at[k],
                                         device_id=(x, y, c), device_id_type=MESH).wait_recv()
        for cp, _ in cps:
            cp.wait_send()

    vm = pl.BlockSpec(memory_space=pltpu.VMEM)
    return pl.pallas_call(
        body, name="gather8", in_specs=[vm], out_specs=vm, out_shape=jax.ShapeDtypeStruct((8, R, C), F32),
        scratch_shapes=[pltpu.SemaphoreType.DMA((7,)), pltpu.SemaphoreType.DMA((7,))],
    )(v)


def _mods_to_owner(ms):
    _, R, C = ms.shape

    def body(m_ref, o_ref, ssem, rsem):
        x, y, c, p, chips, qs = _place()
        o_ref[p] = m_ref[2 * p + c]
        cps = []
        for j in range(3):
            cp = pltpu.make_async_remote_copy(src_ref=m_ref.at[2 * qs[j] + c], dst_ref=o_ref.at[p], send_sem=ssem.at[j],
                                              recv_sem=rsem.at[j], device_id=(*chips[j], c), device_id_type=MESH)
            cp.start()
            cps.append(cp)
        for j in range(3):
            there = o_ref.at[qs[j]]
            pltpu.make_async_remote_copy(src_ref=there, dst_ref=there, send_sem=ssem.at[j], recv_sem=rsem.at[j],
                                         device_id=(x, y, c), device_id_type=MESH).wait_recv()
        for cp in cps:
            cp.wait_send()

    vm = pl.BlockSpec(memory_space=pltpu.VMEM)
    return pl.pallas_call(
        body, name="mods_to_owner", in_specs=[vm], out_specs=vm, out_shape=jax.ShapeDtypeStruct((4, R, C), F32),
        scratch_shapes=[pltpu.SemaphoreType.DMA((3,)), pltpu.SemaphoreType.DMA((3,))],
    )(ms)


def _vec(*rows):
    D = rows[0].shape[-1]
    rows = [r.reshape(1, D) for r in rows]
    return jnp.concatenate(rows + [jnp.zeros((8 - len(rows), D), F32)], axis=0)


def _sets(n_layers, n_conv):
    sets = []
    for l in range(n_layers):
        j = l - n_conv
        mixer = [("conv_w_in", l), ("conv_w_out", l)] if l < n_conv else [("attn_w_q", j), ("attn_w_o", j)]
        sets.append([("ffn1_w_in", l), ("ffn1_w_out", l)] + ([("w_kv", 0)] if l == n_conv else []))
        sets.append(mixer + [("ffn2_w_in", l), ("ffn2_w_out", l)])
    return sets


def _as_weight(name, buf):
    _, _, ha, B = buf.shape
    return buf.reshape(8 * ha, B) if name.endswith("w_out") else buf.reshape(4, 2 * ha, B)


def _local_step(x, tgt, tabs, norm_g, conv_w, kv_norm_g, final_norm_g, mods, kvmods, cast, place):
    S, D = x.shape
    n_layers, n_conv = norm_g.shape[0], conv_w.shape[0]
    md = mods.reshape(n_layers, 9, D)

    def vec_of(l, k):
        return _vec(norm_g[l, k], md[l, 3 * k + 1], md[l, 3 * k], md[l, 3 * k + 2])

    W = {}

    def install(items, bufs):
        for (name, idx), b in zip(items, bufs):
            W[name, idx] = _as_weight(name, b)

    Q = _sets(n_layers, n_conv)
    install(Q[0], _comm_call("gather_d2d", _comm_call("gather_ici", [cast[it] for it in Q[0]])))
    arrived = {}

    def ffn_fwd(n, *args):
        ici = [k for k in ([1] if n == 0 else []) + [n + 2] if k < len(Q)]
        d2d = [n + 1] if 2 <= n + 1 < len(Q) else []
        comm = [("gather_ici", [cast[it] for it in Q[k]]) for k in ici] + [("gather_d2d", arrived.pop(k)) for k in d2d]
        xo, h, ab, y, *moved = _ffn_fwd(*args, comm)
        for k in ici:
            arrived[k], moved = moved[:len(Q[k])], moved[len(Q[k]):]
        for k in d2d:
            install(Q[k], moved[:len(Q[k])])
            moved = moved[len(Q[k]):]
        if n == 0:
            install(Q[1], _comm_call("gather_d2d", arrived.pop(1)))
        return xo, h, ab, y

    vec_kv = _vec(kv_norm_g, kvmods[D:], kvmods[:D])
    ng = len(DILATIONS)
    n_rope = ng * GROUP_W // LANES
    saved = {}
    xc = x
    ks = vs = None
    for l in range(n_layers):
        if l == n_conv:
            h_kv, kv = _proj_fwd(xc, vec_kv, W["w_kv", 0], tabs, n_rope, DILATIONS * 2)
            ks, vs = kv[:ng], kv[ng:]
            saved["kv"] = (xc, h_kv)
        v0 = vec_of(l, 0)
        xo, h, ab, y = ffn_fwd(2 * l, xc, v0, W["ffn1_w_in", l], W["ffn1_w_out", l])
        saved[l, 0] = (xc, v0, h, ab, y)
        xc = xo
        v1 = vec_of(l, 1)
        if l < n_conv:
            cw = _vec(conv_w[l, 0], conv_w[l, 1], conv_w[l, 2])
            xo, h, s4, z, y = _conv_fwd(xc, v1, cw, W["conv_w_in", l], W["conv_w_out", l])
            saved[l, 1] = (xc, v1, h, y, cw, s4, z)
        else:
            j = l - n_conv
            h, qs = _proj_fwd(xc, v1, W["attn_w_q", j], tabs, n_rope, DILATIONS, SM_SCALE)
            og = [_attn_fwd(qs[g], ks[g], vs[g], DILATIONS[g]) for g in range(ng)]
            mixs, ljs, y, xo = _attn_out([o for o, _ in og], [s for _, s in og], xc, v1, W["attn_w_o", j])
            saved[l, 1] = (xc, v1, h, y, qs, mixs, ljs)
        xc = xo
        v2 = vec_of(l, 2)
        xo, h, ab, y = ffn_fwd(2 * l + 1, xc, v2, W["ffn2_w_in", l], W["ffn2_w_out", l])
        saved[l, 2] = (xc, v2, h, ab, y)
        xc = xo

    dx, red_f = _final_loss(xc, _vec(final_norm_g), tgt)
    loss_part = 0.5 * jnp.sum(red_f[1]) / D
    G, parts, others = {}, {}, {}
    dmods = [[None] * 9 for _ in range(n_layers)]
    dnorm = [[None] * 3 for _ in range(n_layers)]
    dconvw = [None] * n_conv
    dkv_parts = []

    def halves(items):
        return [G[it].reshape(4, 2, G[it].shape[1] // 2, G[it].shape[2]) for it in items]

    def pair_sums(items, recv):
        for it, g, r in zip(items, halves(items), recv):
            parts[it] = _pair_add(g, r, place[1:])
        return [parts[it] for it in items]

    R = Q[::-1]
    recvd = {}

    def ffn_bwd(dx, l, k, name):
        m = 2 * (n_layers - 1 - l) + (0 if k == 2 else 1)
        ex = [m - 1] if m >= 1 else []
        sc = [m - 2] if m >= 2 else []
        comm = [("exchange", halves(R[e])) for e in ex] + [("scatter", pair_sums(R[e], recvd.pop(e))) for e in sc]
        xin, v, h, ab, y = saved[l, k]
        w_in, w_out = W[name + "_w_in", l], W[name + "_w_out", l]
        dy, u, dab, dh, *moved = _ffn_dgrad(dx, v, ab, w_in, w_out, comm)
        for e in ex:
            recvd[e], moved = moved[:len(R[e])], moved[len(R[e]):]
        for e in sc:
            others.update(zip(R[e], moved[:len(R[e])]))
            moved = moved[len(R[e]):]
        if m == len(R) - 1:
            G[name + "_w_in", l], got = _mm_tn(h, dab, w_in.shape[2], [("scatter", pair_sums(R[m - 1], recvd.pop(m - 1)))])
            others.update(zip(R[m - 1], got))
        else:
            G[name + "_w_in", l] = _mm_tn(h, dab, w_in.shape[2])
        G[name + "_w_out", l] = _mm_tn(u, dy[None], D).reshape(4, -1, D)
        return _norm_bwd(dx, [dh], xin, y, v, 0.5)

    def note(l, k, red):
        dnorm[l][k] = red[0]
        dmods[l][3 * k], dmods[l][3 * k + 1], dmods[l][3 * k + 2] = red[2], red[1], red[3]

    for l in reversed(range(n_layers)):
        dx, red = ffn_bwd(dx, l, 2, "ffn2")
        note(l, 2, red)
        if l < n_conv:
            xin, v, h, y, cw, s4, z = saved[l, 1]
            w_in, w_out = W["conv_w_in", l], W["conv_w_out", l]
            dy, d3, dh, dcw = _conv_dgrad(dx, v, cw, s4, w_in, w_out)
            G["conv_w_in", l] = _mm_tn(h, d3[None], w_in.shape[2])
            G["conv_w_out", l] = _mm_tn(z, dy[None], D).reshape(4, -1, D)
            dconvw[l] = dcw[0:3]
        else:
            j = l - n_conv
            xin, v, h, y, qs, mixs, ljs = saved[l, 1]
            wq, wo = W["attn_w_q", j], W["attn_w_o", j]
            dy, dmixs = _attn_out_dgrad(dx, v, wo)
            G["attn_w_o", j] = _mm_tn(mixs[0], dy[None], wo.shape[2])
            dqkv = [_attn_bwd(qs[g], ks[g], vs[g], dmixs[g], mixs[g], ljs[g], DILATIONS[g]) for g in range(ng)]
            dkv_parts.append([t[1:] for t in dqkv])
            dq, dh = _proj_dgrad([[t[0]] for t in dqkv], tabs, wq, ng, DILATIONS)
            G["attn_w_q", j] = _mm_tn(h, dq[None], wq.shape[2])
        dx, red = _norm_bwd(dx, [dh], xin, y, v, 1.0)
        note(l, 1, red)
        dx, red = ffn_bwd(dx, l, 0, "ffn1")
        note(l, 0, red)
        if l == n_conv:
            xin, h_kv = saved["kv"]
            kv_parts = [[lay[g][0] for lay in dkv_parts] for g in range(ng)] + [[lay[g][1] for lay in dkv_parts] for g in range(ng)]
            dkv, dh = _proj_dgrad(kv_parts, tabs, W["w_kv", 0], ng, DILATIONS * 2)
            G["w_kv", 0] = _mm_tn(h_kv, dkv[None], W["w_kv", 0].shape[2])
            dx, red_kv = _norm_bwd(dx, [dh], xin, None, vec_kv, 1.0)
    others.update(zip(R[-1], _comm_call("scatter", pair_sums(R[-1], _comm_call("exchange", halves(R[-1]))))))
    small = jnp.concatenate(
        [jnp.stack([jnp.stack(r) for r in dmods]).reshape(-1), red_kv[2], red_kv[1],
         jnp.stack([jnp.stack(r) for r in dnorm]).reshape(-1), jnp.stack(dconvw).reshape(-1), red_kv[0], red_f[0]])
    return loss_part, dx, parts, others, small


BIG = ("ffn1_w_in", "ffn1_w_out", "ffn2_w_in", "ffn2_w_out", "conv_w_in", "conv_w_out", "w_kv", "attn_w_q", "attn_w_o")


def _halved(t):
    if t.ndim == 2:
        t = t[None]
    L, A, B = t.shape
    return t.reshape(L, 2, A // 2, B)


def kernel(x, c, positions, norm_g, ada_w, ada_b, ffn1_w_in, ffn1_w_out, ffn2_w_in, ffn2_w_out, conv_w_in, conv_w, conv_w_out, kv_norm_g, kv_ada_w, kv_ada_b, w_kv, attn_w_q, attn_w_o, final_norm_g, loss_target, m_norm_g, m_ada_w, m_ada_b, m_ffn1_w_in, m_ffn1_w_out, m_ffn2_w_in, m_ffn2_w_out, m_conv_w_in, m_conv_w, m_conv_w_out, m_kv_norm_g, m_kv_ada_w, m_kv_ada_b, m_w_kv, m_attn_w_q, m_attn_w_o, m_final_norm_g, v_norm_g, v_ada_w, v_ada_b, v_ffn1_w_in, v_ffn1_w_out, v_ffn2_w_in, v_ffn2_w_out, v_conv_w_in, v_conv_w, v_conv_w_out, v_kv_norm_g, v_kv_ada_w, v_kv_ada_b, v_w_kv, v_attn_w_q, v_attn_w_o, v_final_norm_g):
    wts = dict(norm_g=norm_g, ada_w=ada_w, ada_b=ada_b, ffn1_w_in=ffn1_w_in, ffn1_w_out=ffn1_w_out, ffn2_w_in=ffn2_w_in,
               ffn2_w_out=ffn2_w_out, conv_w_in=conv_w_in, conv_w=conv_w, conv_w_out=conv_w_out, kv_norm_g=kv_norm_g,
               kv_ada_w=kv_ada_w, kv_ada_b=kv_ada_b, w_kv=w_kv, attn_w_q=attn_w_q, attn_w_o=attn_w_o,
               final_norm_g=final_norm_g)
    ms = dict(norm_g=m_norm_g, ada_w=m_ada_w, ada_b=m_ada_b, ffn1_w_in=m_ffn1_w_in, ffn1_w_out=m_ffn1_w_out,
              ffn2_w_in=m_ffn2_w_in, ffn2_w_out=m_ffn2_w_out, conv_w_in=m_conv_w_in, conv_w=m_conv_w,
              conv_w_out=m_conv_w_out, kv_norm_g=m_kv_norm_g, kv_ada_w=m_kv_ada_w, kv_ada_b=m_kv_ada_b, w_kv=m_w_kv,
              attn_w_q=m_attn_w_q, attn_w_o=m_attn_w_o, final_norm_g=m_final_norm_g)
    vs = dict(norm_g=v_norm_g, ada_w=v_ada_w, ada_b=v_ada_b, ffn1_w_in=v_ffn1_w_in, ffn1_w_out=v_ffn1_w_out,
              ffn2_w_in=v_ffn2_w_in, ffn2_w_out=v_ffn2_w_out, conv_w_in=v_conv_w_in, conv_w=v_conv_w,
              conv_w_out=v_conv_w_out, kv_norm_g=v_kv_norm_g, kv_ada_w=v_kv_ada_w, kv_ada_b=v_kv_ada_b, w_kv=v_w_kv,
              attn_w_q=v_attn_w_q, attn_w_o=v_attn_w_o, final_norm_g=v_final_norm_g)
    order = list(wts)
    S, D = x.shape[1], x.shape[2]
    n_layers, n_conv = norm_g.shape[0], conv_w.shape[0]
    nm, sw = ada_w.shape[2], norm_g.shape[2]
    ix, iy, ic = (lax.axis_index(a) for a in AXES)
    chip = 2 * ix + iy

    pack = jnp.concatenate([norm_g.reshape(-1), conv_w.reshape(-1), c.reshape(-1)])
    npad = (-pack.size) % (8 * LANES)
    allp = _gather8(jnp.pad(pack, (0, npad)).reshape(-1, LANES)).reshape(8, -1)
    n1, n2 = norm_g.size, norm_g.size + conv_w.size
    by_chip = allp[0::2]
    norm_full = jnp.moveaxis(by_chip[:, :n1].reshape(4, n_layers, 3, sw), 0, 2).reshape(n_layers, 3, 4 * sw)
    conv_full = jnp.moveaxis(by_chip[:, n1:n2].reshape(4, n_conv, 3, sw), 0, 2).reshape(n_conv, 3, 4 * sw)
    c_all = allp[:, n2:n2 + D]

    b_l = lax.dynamic_slice_in_dim(ada_b, chip * nm, nm, axis=1)
    mod_sh = _ada_fwd(c_all, ada_w, b_l[:, None, :])
    nkv = kv_ada_w.shape[1]
    bkv = lax.dynamic_slice_in_dim(kv_ada_b, chip * nkv, nkv, axis=0)
    kv_sh = _ada_fwd(c_all, kv_ada_w[None], bkv[None, None, :])
    rows = jnp.concatenate([jnp.moveaxis(mod_sh, 0, 1), jnp.pad(jnp.moveaxis(kv_sh, 0, 1), ((0, 0), (0, 0), (0, nm - nkv)))], axis=1)
    rpad = (-rows.shape[1]) % 8
    mine = _mods_to_owner(jnp.pad(rows, ((0, 0), (0, rpad), (0, 0))))
    mods = jnp.moveaxis(mine[:, :n_layers], 0, 1).reshape(n_layers, 4 * nm)
    kvmods = mine[:, n_layers, :nkv].reshape(4 * nkv)

    place = jnp.stack([chip, ic]).astype(jnp.int32)
    halved = {k: _halved(wts[k]) for k in BIG}
    cast = {(k, l): _cast_shard(halved[k], l, place) for k in BIG for l in range(halved[k].shape[0])}

    loss_part, dx, parts, others, small = _local_step(x[0], loss_target[0], _rope_tables(positions), norm_full, conv_full,
                                                      kv_norm_g, final_norm_g, mods, kvmods, cast, place)
    loss = lax.psum(loss_part, AXES)

    gbufs = []
    for k in BIG:
        gbuf, n_l = None, halved[k].shape[0]
        for l in range(n_l):
            gbuf = _sum_partials(parts[k, l], others[k, l], gbuf, n_l, l, place)
        gbufs.append(gbuf)
    grads = {}

    spad = (-small.size) % (8 * LANES)
    gath = _gather8(jnp.pad(small, (0, spad)).reshape(-1, LANES))
    tot = _sum8(gath).reshape(-1)
    gath = gath.reshape(8, -1)
    o = 0

    def take(n):
        nonlocal o
        o += n
        return tot[o - n:o]

    g_mods = take(n_layers * 9 * D).reshape(n_layers, 9 * D)
    g_kvmods = take(2 * D)
    g_norm = take(n_layers * 3 * D).reshape(n_layers, 3, D)
    g_convw = take(n_conv * 3 * D).reshape(n_conv, 3, D)
    g_kvn = take(D)
    g_fin = take(D)
    grads["ada_b"] = g_mods
    grads["kv_ada_b"] = g_kvmods
    grads["norm_g"] = lax.dynamic_slice_in_dim(g_norm, chip * sw, sw, axis=2)
    grads["conv_w"] = lax.dynamic_slice_in_dim(g_convw, chip * sw, sw, axis=2)
    grads["kv_norm_g"] = g_kvn
    grads["final_norm_g"] = g_fin

    dm_all = gath[:, :n_layers * 9 * D].reshape(8, n_layers, 9 * D)
    dm_mine = jnp.moveaxis(lax.dynamic_slice_in_dim(dm_all, chip * nm, nm, axis=2), 0, 1)
    dkv_all = gath[:, n_layers * 9 * D:n_layers * 9 * D + 2 * D]
    dkv_mine = lax.dynamic_slice_in_dim(dkv_all, chip * nkv, nkv, axis=1)[None]
    c_t = c_all.T
    grads["ada_w"] = _ada_wgrad(c_t, dm_mine)
    grads["kv_ada_w"] = _ada_wgrad(c_t, dkv_mine)[0]

    deltas, new_m, new_v = {}, {}, {}
    deltas["ada_w"], new_m["ada_w"], new_v["ada_w"], shared = _adamw(ada_w, grads["ada_w"], m_ada_w, v_ada_w, [("share", gbufs)])
    grads.update({k: s.reshape(wts[k].shape) for k, s in zip(BIG, shared)})
    for k in order:
        if k == "ada_w":
            continue
        shp = wts[k].shape
        two_d = (lambda t: t.reshape(1, -1)) if len(shp) == 1 else (lambda t: t)
        dlt, nmk, nvk = _adamw(two_d(wts[k]), two_d(grads[k]), two_d(ms[k]), two_d(vs[k]))
        deltas[k], new_m[k], new_v[k] = dlt.reshape(shp), nmk.reshape(shp), nvk.reshape(shp)
    return (loss, dx[None], *[grads[k] for k in order], *[deltas[k] for k in order], *[new_m[k] for k in order],
            *[new_v[k] for k in order])
```

```python
import functools
import math

import jax
import jax.numpy as jnp
from jax import lax
from jax.experimental import pallas as pl
from jax.experimental.pallas import tpu as pltpu

F32 = jnp.float32
BF = jnp.bfloat16
MESH = pl.DeviceIdType.MESH
AXES = ("x", "y", "c")

NORM_EPS = 1e-5
HEAD_DIM = 64
HEADS_PER_GROUP = 8
GROUP_W = HEADS_PER_GROUP * HEAD_DIM
DILATIONS = (1, 4, 16)
BAND = 128
ROPE_DIM = 16
ROPE_THETA = 500000.0
SM_SCALE = HEAD_DIM ** -0.5
NEG = -1e30
ADAM_LR, ADAM_B1, ADAM_B2, ADAM_EPS, ADAM_WD, ADAM_STEP = 0.001, 0.9, 0.999, 1e-08, 0.01, 10

V7X_VMEM_BYTES = 64 * 1024 * 1024
VMEM_LIMIT = V7X_VMEM_BYTES - 6 * 1024 * 1024
LANES = 128
TM = 512
TM_DGRAD = 256
TM_STREAM = 1024
TK = 2048
WGRAD_COLS = 1536


def _cparams(ngrid):
    return pltpu.CompilerParams(dimension_semantics=("arbitrary",) * ngrid, vmem_limit_bytes=VMEM_LIMIT)


def _dot(a, b):
    return jnp.dot(a, b, preferred_element_type=F32)


def _dot_nt(a, b):
    return lax.dot_general(a, b, (((1,), (1,)), ((), ())), preferred_element_type=F32)


def _dot_tn(a, b):
    return lax.dot_general(a, b, (((0,), (0,)), ((), ())), preferred_element_type=F32)


def _tile(n, pref):
    t = min(n, pref)
    while n % t:
        t //= 2
    return t


def _row_tile(n, cap=256, mult=16):
    best = n
    for t in range(mult, min(n, cap) + 1, mult):
        if n % t == 0:
            best = t
    return best


def _rstd(x):
    return lax.rsqrt(jnp.mean(x * x, axis=-1, keepdims=True) + NORM_EPS)


def _norm_mod(x, v):
    return (x * _rstd(x) * v[0:1]) * (1.0 + v[1:2]) + v[2:3]


def _silu_parts(a):
    sg = jax.nn.sigmoid(a)
    return sg, a * sg


def _row(i):
    return lambda *_: (i, 0)


ANY = pl.BlockSpec(memory_space=pl.ANY)
IN_PLACE = ("gather_ici", "gather_d2d", "share")


def _place():
    x, y, c = (lax.axis_index(a) for a in AXES)
    chips = [(1 - x, y), (x, 1 - y), (1 - x, 1 - y)]
    return x, y, c, 2 * x + y, chips, [2 * cx + cy for cx, cy in chips]


def _transfers(kind, ins, outs):
    x, y, c, p, chips, qs = _place()
    sib = (x, y, 1 - c)
    rows = []
    for k, o in enumerate(outs):
        if kind == "gather_ici":
            rows.append([(o.at[p, c], o.at[p, c], (*chips[j], c), o.at[qs[j], c]) for j in range(3)])
        elif kind == "gather_d2d":
            rows.append([(o.at[qs[j], c], o.at[qs[j], c], sib, o.at[qs[j], 1 - c]) for j in range(3)])
        elif kind == "exchange":
            rows.append([(ins[k].at[:, 1 - c], o, sib, o)])
        elif kind == "scatter":
            rows.append([(ins[k].at[qs[j]], o.at[j], (*chips[j], c), o.at[j]) for j in range(3)])
        elif kind == "share":
            rows.append([(o.at[:, c], o.at[:, c], sib, o.at[:, 1 - c])])
    return rows


def _comm_out_shapes(kind, arrays):
    if kind in IN_PLACE:
        return [jax.ShapeDtypeStruct(a.shape, a.dtype) for a in arrays]
    if kind == "exchange":
        return [jax.ShapeDtypeStruct((4,) + a.shape[2:], a.dtype) for a in arrays]
    return [jax.ShapeDtypeStruct((3,) + a.shape[1:], a.dtype) for a in arrays]


def _comm_sems(n):
    return [pltpu.SemaphoreType.DMA((n, 3)), pltpu.SemaphoreType.DMA((n, 3))]


def _comm_start(rows, ssem, rsem):
    for k, row in enumerate(rows):
        for j, (src, dst, dev, _) in enumerate(row):
            pltpu.make_async_remote_copy(src_ref=src, dst_ref=dst, send_sem=ssem.at[k, j], recv_sem=rsem.at[k, j],
                                         device_id=dev, device_id_type=MESH).start()


def _comm_wait(rows, ssem, rsem):
    for k, row in enumerate(rows):
        for j, (src, dst, dev, land) in enumerate(row):
            pltpu.make_async_remote_copy(src_ref=src, dst_ref=dst, send_sem=ssem.at[k, j], recv_sem=rsem.at[k, j],
                                         device_id=dev, device_id_type=MESH).wait_send()
            pltpu.make_async_remote_copy(src_ref=land, dst_ref=land, send_sem=ssem.at[k, j], recv_sem=rsem.at[k, j],
                                         device_id=dev, device_id_type=MESH).wait_recv()


def _comm_call(kind, arrays):
    n = len(arrays)

    def body(*refs):
        rows = _transfers(kind, refs[:n], refs[n:2 * n])
        _comm_start(rows, *refs[2 * n:])
        _comm_wait(rows, *refs[2 * n:])

    return pl.pallas_call(
        body, name="comm_" + kind, in_specs=[ANY] * n, out_specs=[ANY] * n, out_shape=_comm_out_shapes(kind, arrays),
        input_output_aliases={k: k for k in range(n)} if kind in IN_PLACE else {},
        scratch_shapes=_comm_sems(n),
    )(*arrays)


def _ride(comm, n_in, n_out):
    ride = dict(arrays=[], in_specs=[], out_specs=[], out_shape=[], aliases={}, scratch=[], n=0)
    for kind, arrays in comm or []:
        if kind in IN_PLACE:
            ride["aliases"].update({n_in + ride["n"] + k: n_out + ride["n"] + k for k in range(len(arrays))})
        ride["arrays"] += list(arrays)
        ride["out_shape"] += _comm_out_shapes(kind, arrays)
        ride["scratch"] += _comm_sems(len(arrays))
        ride["n"] += len(arrays)
    ride["in_specs"] = ride["out_specs"] = [ANY] * ride["n"]
    return ride


def _ride_when(comm, ins, outs, sems, cond, action):
    if not comm:
        return

    @pl.when(cond)
    def _():
        at = 0
        for e, (kind, arrays) in enumerate(comm):
            n = len(arrays)
            action(_transfers(kind, ins[at:at + n], outs[at:at + n]), *sems[2 * e:2 * e + 2])
            at += n


def _ffn_fwd(x, vec, w_in4, w_out, comm=None):
    S, D = x.shape
    cs = w_in4.shape[2]
    F = 2 * cs
    tm = _tile(S, TM)
    ni = S // tm
    ride = _ride(comm, 5, 4)
    nc = ride["n"]

    def body(*refs):
        x_ref, vec_ref, wa_ref, wb_ref, wo_ref = refs[:5]
        xo_ref, h_ref, ab_ref, y_ref = refs[5 + nc:9 + nc]
        i = pl.program_id(0)
        riding = (comm, refs[5:5 + nc], refs[9 + nc:9 + 2 * nc], refs[9 + 2 * nc:])
        _ride_when(*riding, i == 0, _comm_start)
        xv = x_ref[...]
        h = _norm_mod(xv, vec_ref[...]).astype(BF)
        h_ref[...] = h
        y = None
        for c in range(2):
            cols = slice(c * cs, (c + 1) * cs)
            a = _dot(h, wa_ref[c])
            b = _dot(h, wb_ref[c])
            ab_ref[0, :, cols] = a.astype(BF)
            ab_ref[1, :, cols] = b.astype(BF)
            _, s = _silu_parts(a)
            part = _dot((s * b).astype(BF), wo_ref[cols, :])
            y = part if y is None else y + part
        y_ref[...] = y.astype(BF)
        xo_ref[...] = xv + (0.5 * (1.0 + vec_ref[3:4, :])) * y
        _ride_when(*riding, i == ni - 1, _comm_wait)

    tok = pl.BlockSpec((tm, D), lambda i: (i, 0))
    once = pl.Buffered(1)
    return pl.pallas_call(
        body, name="ffn_fwd", grid=(ni,),
        in_specs=[tok, pl.BlockSpec((8, D), lambda i: (0, 0)),
                  pl.BlockSpec((2, D, cs), lambda i: (0, 0, 0), pipeline_mode=once),
                  pl.BlockSpec((2, D, cs), lambda i: (1, 0, 0), pipeline_mode=once),
                  pl.BlockSpec((F, D), lambda i: (0, 0), pipeline_mode=once)] + ride["in_specs"],
        out_specs=[tok, tok, pl.BlockSpec((2, tm, F), lambda i: (0, i, 0)), tok] + ride["out_specs"],
        out_shape=[jax.ShapeDtypeStruct((S, D), F32), jax.ShapeDtypeStruct((S, D), BF),
                   jax.ShapeDtypeStruct((2, S, F), BF), jax.ShapeDtypeStruct((S, D), BF)] + ride["out_shape"],
        input_output_aliases=ride["aliases"], scratch_shapes=ride["scratch"],
        compiler_params=_cparams(1),
    )(x, vec, w_in4, w_in4, w_out, *ride["arrays"])


def _norm_bwd_tile(dxo, dh, xv, y, vec_ref, red_ref, coef):
    g = vec_ref[0:1, :]
    sc = vec_ref[1:2, :]
    r = _rstd(xv)
    xh = xv * r
    dhn = dh * (1.0 + sc)
    dxh = dhn * g
    red_ref[0:1, :] += jnp.sum(dhn * xh, axis=0, keepdims=True)
    red_ref[1:2, :] += jnp.sum(dh * (xh * g), axis=0, keepdims=True)
    red_ref[2:3, :] += jnp.sum(dh, axis=0, keepdims=True)
    if y is not None:
        red_ref[3:4, :] += coef * jnp.sum(dxo * y.astype(F32), axis=0, keepdims=True)
    return dxo + r * (dxh - xh * jnp.mean(dxh * xh, axis=-1, keepdims=True))


def _ffn_dgrad(dxo, vec, ab, w_in4, w_out, comm=None):
    S, D = dxo.shape
    cs = w_in4.shape[2]
    F = 2 * cs
    tm = _tile(S, TM_DGRAD)
    ni = S // tm
    ride = _ride(comm, 6, 4)
    nc = ride["n"]

    def body(*refs):
        dxo_ref, vec_ref, ab_ref, wa_ref, wb_ref, wo_ref = refs[:6]
        dy_ref, u_ref, dab_ref, dh_ref = refs[6 + nc:10 + nc]
        i = pl.program_id(0)
        riding = (comm, refs[6:6 + nc], refs[10 + nc:10 + 2 * nc], refs[10 + 2 * nc:])
        _ride_when(*riding, i == 0, _comm_start)
        dy = ((0.5 * (1.0 + vec_ref[3:4, :])) * dxo_ref[...]).astype(BF)
        dy_ref[...] = dy
        dh = None
        for c in range(2):
            cols = slice(c * cs, (c + 1) * cs)
            du = _dot_nt(dy, wo_ref[cols, :])
            a = ab_ref[0, :, cols].astype(F32)
            b = ab_ref[1, :, cols].astype(F32)
            sg, s = _silu_parts(a)
            u_ref[:, cols] = (s * b).astype(BF)
            da = (du * b * (sg * (1.0 + a * (1.0 - sg)))).astype(BF)
            db = (du * s).astype(BF)
            dab_ref[0, :, cols] = da
            dab_ref[1, :, cols] = db
            part = _dot_nt(da, wa_ref[c]) + _dot_nt(db, wb_ref[c])
            dh = part if dh is None else dh + part
        dh_ref[...] = dh.astype(BF)
        _ride_when(*riding, i == ni - 1, _comm_wait)

    tok = pl.BlockSpec((tm, D), lambda i: (i, 0))
    once = pl.Buffered(1)
    return pl.pallas_call(
        body, name="ffn_dgrad", grid=(ni,),
        in_specs=[tok, pl.BlockSpec((8, D), lambda i: (0, 0)),
                  pl.BlockSpec((2, tm, F), lambda i: (0, i, 0)),
                  pl.BlockSpec((2, D, cs), lambda i: (0, 0, 0), pipeline_mode=once),
                  pl.BlockSpec((2, D, cs), lambda i: (1, 0, 0), pipeline_mode=once),
                  pl.BlockSpec((F, D), lambda i: (0, 0), pipeline_mode=once)] + ride["in_specs"],
        out_specs=[tok, pl.BlockSpec((tm, F), lambda i: (i, 0)),
                   pl.BlockSpec((2, tm, F), lambda i: (0, i, 0)), tok] + ride["out_specs"],
        out_shape=[jax.ShapeDtypeStruct((S, D), BF), jax.ShapeDtypeStruct((S, F), BF),
                   jax.ShapeDtypeStruct((2, S, F), BF), jax.ShapeDtypeStruct((S, D), BF)] + ride["out_shape"],
        input_output_aliases=ride["aliases"], scratch_shapes=ride["scratch"],
        compiler_params=_cparams(1),
    )(dxo, vec, ab, w_in4, w_in4, w_out, *ride["arrays"])


def _mm_tn(a, b3, cs, comm=None):
    K, M = a.shape
    G, _, Nb = b3.shape
    N = G * Nb
    if cs >= Nb or Nb % cs:
        ns, tn = 1, math.gcd(cs, Nb)
    else:
        ns = max(s for s in range(1, Nb // cs + 1) if (Nb // cs) % s == 0 and s * cs <= max(WGRAD_COLS, cs))
        tn = ns * cs
    tmo = M if M <= 1024 else M // 2
    tk = _tile(K, TK)
    gm, gn, gk = M // tmo, N // tn, K // tk
    ride = _ride(comm, 2, 1)
    nc = ride["n"]

    def body(*refs):
        a_ref, b_ref = refs[:2]
        o_ref, acc_ref = refs[2 + nc], refs[-1]
        at = [pl.program_id(t) for t in range(3)]
        riding = (comm, refs[2:2 + nc], refs[3 + nc:3 + 2 * nc], refs[3 + 2 * nc:-1])
        _ride_when(*riding, functools.reduce(jnp.logical_and, [t == 0 for t in at]), _comm_start)

        @pl.when(at[2] == 0)
        def _():
            acc_ref[...] = jnp.zeros_like(acc_ref)

        av = a_ref[...].astype(BF)
        if ns == 1:
            acc_ref[...] += _dot_tn(av, b_ref[...].astype(BF))
        else:
            for s in range(ns):
                acc_ref[s] += _dot_tn(av, b_ref[:, s * cs:(s + 1) * cs].astype(BF))

        @pl.when(at[2] == gk - 1)
        def _():
            o_ref[...] = acc_ref[...].astype(BF)

        _ride_when(*riding, functools.reduce(jnp.logical_and, [t == g - 1 for t, g in zip(at, (gm, gn, gk))]), _comm_wait)

    if ns == 1:
        out_spec = pl.BlockSpec((None, tmo, tn), lambda m, n, k: ((n * tn) // cs, m, ((n * tn) % cs) // tn))
        acc = pltpu.VMEM((tmo, tn), F32)
    else:
        out_spec = pl.BlockSpec((ns, tmo, cs), lambda m, n, k: (n, m, 0))
        acc = pltpu.VMEM((ns, tmo, cs), F32)
    g, *moved = pl.pallas_call(
        body, name="wgrad_tn", grid=(gm, gn, gk),
        in_specs=[pl.BlockSpec((tk, tmo), lambda m, n, k: (k, m)),
                  pl.BlockSpec((None, tk, tn), lambda m, n, k: ((n * tn) // Nb, k, ((n * tn) % Nb) // tn))] + ride["in_specs"],
        out_specs=[out_spec] + ride["out_specs"],
        out_shape=[jax.ShapeDtypeStruct((N // cs, M, cs), BF)] + ride["out_shape"],
        input_output_aliases=ride["aliases"], scratch_shapes=ride["scratch"] + [acc],
        compiler_params=_cparams(3),
    )(a, b3, *ride["arrays"])
    return (g, moved) if comm else g


def _norm_bwd(dxo, dhs, x, y, vec, coef):
    S, D = x.shape
    tm = _tile(S, TM_STREAM)
    nh = len(dhs)
    has_y = y is not None

    def body(*refs):
        dxo_ref = refs[0]
        dh_refs = refs[1:1 + nh]
        x_ref = refs[1 + nh]
        y_ref = refs[2 + nh] if has_y else None
        vec_ref, dx_ref, red_ref = refs[-3:]

        @pl.when(pl.program_id(0) == 0)
        def _():
            red_ref[...] = jnp.zeros_like(red_ref)

        dh = dh_refs[0][...].astype(F32)
        for r in dh_refs[1:]:
            dh = dh + r[...].astype(F32)
        dx_ref[...] = _norm_bwd_tile(dxo_ref[...], dh, x_ref[...], y_ref[...] if has_y else None, vec_ref, red_ref, coef)

    tok = pl.BlockSpec((tm, D), lambda i: (i, 0))
    small = pl.BlockSpec((8, D), lambda i: (0, 0))
    ops = [dxo, *dhs, x] + ([y] if has_y else []) + [vec]
    return pl.pallas_call(
        body, name="norm_bwd", grid=(S // tm,),
        in_specs=[tok] * (len(ops) - 1) + [small],
        out_specs=[tok, small],
        out_shape=[jax.ShapeDtypeStruct((S, D), F32), jax.ShapeDtypeStruct((8, D), F32)],
        compiler_params=_cparams(1),
    )(*ops)


def _conv_fwd(x, vec, cw, w_in4, w_out, comm=None):
    S, D = x.shape
    cs = w_in4.shape[2]
    tm = _tile(S, TM)
    nt = S // tm
    ride = _ride(comm, 5, 5)
    nc = ride["n"]

    def body(*refs):
        x_ref, vec_ref, cw_ref, wi_ref, wo_ref = refs[:5]
        xo_ref, h_ref, s4_ref, z_ref, y_ref = refs[5 + nc:10 + nc]
        vs_ref = refs[-1]
        riding = (comm, refs[5:5 + nc], refs[10 + nc:10 + 2 * nc], refs[10 + 2 * nc:-1])
        _ride_when(*riding, pl.program_id(0) == 0, _comm_start)

        @pl.when(pl.program_id(0) == 0)
        def _():
            vs_ref[0:8, :] = jnp.zeros((8, D), F32)

        xv = x_ref[...]
        h = _norm_mod(xv, vec_ref[...]).astype(BF)
        h_ref[...] = h
        bcu = jnp.concatenate([_dot(h, wi_ref[q]) for q in range(4)], axis=1)
        bg, cg, u = bcu[:, :D], bcu[:, D:2 * D], bcu[:, 2 * D:]
        v = cg * u
        vs_ref[8:8 + tm, :] = v
        conv = cw_ref[0:1, :] * vs_ref[pl.ds(6, tm), :] + cw_ref[1:2, :] * vs_ref[pl.ds(7, tm), :] + cw_ref[2:3, :] * v
        vs_ref[0:8, :] = vs_ref[tm:tm + 8, :]
        z = (bg * conv).astype(BF)
        s4_ref[0] = bg.astype(BF)
        s4_ref[1] = cg.astype(BF)
        s4_ref[2] = u.astype(BF)
        s4_ref[3] = conv.astype(BF)
        z_ref[...] = z
        y = _dot(z, wo_ref[...])
        y_ref[...] = y.astype(BF)
        xo_ref[...] = xv + (1.0 + vec_ref[3:4, :]) * y
        _ride_when(*riding, pl.program_id(0) == nt - 1, _comm_wait)

    tok = pl.BlockSpec((tm, D), lambda i: (i, 0))
    small = pl.BlockSpec((8, D), lambda i: (0, 0))
    return pl.pallas_call(
        body, name="conv_fwd", grid=(nt,),
        in_specs=[tok, small, small, pl.BlockSpec((4, D, cs), lambda i: (0, 0, 0)),
                  pl.BlockSpec((D, D), lambda i: (0, 0))] + ride["in_specs"],
        out_specs=[tok, tok, pl.BlockSpec((4, tm, D), lambda i: (0, i, 0)), tok, tok] + ride["out_specs"],
        out_shape=[jax.ShapeDtypeStruct((S, D), F32), jax.ShapeDtypeStruct((S, D), BF),
                   jax.ShapeDtypeStruct((4, S, D), BF), jax.ShapeDtypeStruct((S, D), BF),
                   jax.ShapeDtypeStruct((S, D), BF)] + ride["out_shape"],
        input_output_aliases=ride["aliases"], scratch_shapes=ride["scratch"] + [pltpu.VMEM((tm + 8, D), F32)],
        compiler_params=_cparams(1),
    )(x, vec, cw, w_in4, w_out, *ride["arrays"])


def _conv_dgrad(dxo, vec, cw, s4, w_in4, w_out):
    S, D = dxo.shape
    cs = w_in4.shape[2]
    tm = _tile(S, TM)
    nt = S // tm

    def body(dxo_ref, vec_ref, cw_ref, s4_ref, wi_ref, wo_ref, dy_ref, d3_ref, dh_ref, dcw_ref, ds_ref):
        @pl.when(pl.program_id(0) == 0)
        def _():
            ds_ref[tm:tm + 8, :] = jnp.zeros((8, D), F32)
            dcw_ref[...] = jnp.zeros_like(dcw_ref)

        dy = ((1.0 + vec_ref[3:4, :]) * dxo_ref[...]).astype(BF)
        dy_ref[...] = dy
        dz = _dot_nt(dy, wo_ref[...])
        bg = s4_ref[0].astype(F32)
        cg = s4_ref[1].astype(F32)
        u = s4_ref[2].astype(F32)
        conv = s4_ref[3].astype(F32)
        dbg = dz * conv
        dconv = dz * bg
        ds_ref[0:tm, :] = dconv
        d1 = ds_ref[pl.ds(1, tm), :]
        d2 = ds_ref[pl.ds(2, tm), :]
        ds_ref[tm:tm + 8, :] = ds_ref[0:8, :]
        dv = cw_ref[2:3, :] * dconv + cw_ref[1:2, :] * d1 + cw_ref[0:1, :] * d2
        v = cg * u
        dcw_ref[0:1, :] += jnp.sum(d2 * v, axis=0, keepdims=True)
        dcw_ref[1:2, :] += jnp.sum(d1 * v, axis=0, keepdims=True)
        dcw_ref[2:3, :] += jnp.sum(dconv * v, axis=0, keepdims=True)
        dbcu = jnp.concatenate([dbg, dv * u, dv * cg], axis=1).astype(BF)
        d3_ref[...] = dbcu
        dh = _dot_nt(dbcu[:, 0:cs], wi_ref[0])
        for q in range(1, 4):
            dh = dh + _dot_nt(dbcu[:, q * cs:(q + 1) * cs], wi_ref[q])
        dh_ref[...] = dh.astype(BF)

    tok = pl.BlockSpec((tm, D), lambda i: (nt - 1 - i, 0))
    small = pl.BlockSpec((8, D), lambda i: (0, 0))
    return pl.pallas_call(
        body, name="conv_dgrad", grid=(nt,),
        in_specs=[tok, small, small, pl.BlockSpec((4, tm, D), lambda i: (0, nt - 1 - i, 0)),
                  pl.BlockSpec((4, D, cs), lambda i: (0, 0, 0)), pl.BlockSpec((D, D), lambda i: (0, 0))],
        out_specs=[tok, pl.BlockSpec((tm, 3 * D), lambda i: (nt - 1 - i, 0)), tok, small],
        out_shape=[jax.ShapeDtypeStruct((S, D), BF), jax.ShapeDtypeStruct((S, 3 * D), BF),
                   jax.ShapeDtypeStruct((S, D), BF), jax.ShapeDtypeStruct((8, D), F32)],
        scratch_shapes=[pltpu.VMEM((tm + 8, D), F32)],
        compiler_params=_cparams(1),
    )(dxo, vec, cw, s4, w_in4, w_out)


def _rope_tables(positions):
    S = positions.shape[-1]
    inv = ROPE_THETA ** (-jnp.arange(0, ROPE_DIM, 2, dtype=F32) / ROPE_DIM)
    ang = positions.reshape(S, 1).astype(F32) * inv
    cos = jnp.tile(jnp.cos(ang), (1, LANES // 8))
    sin = jnp.tile(jnp.sin(ang), (1, LANES // 8))
    l64 = jnp.arange(LANES) % HEAD_DIM
    return jnp.stack([jnp.where(l64 < ROPE_DIM, cos, 1.0),
                      jnp.where(l64 < ROPE_DIM // 2, -sin, 0.0),
                      jnp.where((l64 >= ROPE_DIM // 2) & (l64 < ROPE_DIM), sin, 0.0)])


def _rope(t, tab_ref):
    return t * tab_ref[0] + pltpu.roll(t, LANES - 8, 1) * tab_ref[1] + pltpu.roll(t, 8, 1) * tab_ref[2]


def _rope_t(d, tab_ref):
    return d * tab_ref[0] + pltpu.roll(d * tab_ref[1], 8, 1) + pltpu.roll(d * tab_ref[2], LANES - 8, 1)


GROUP_CH = GROUP_W // LANES


def _to_dilated(src_ref, c0, dst_ref, d):
    n = src_ref.shape[1]
    for r in range(d):
        for ch in range(GROUP_CH):
            rows = src_ref[c0 + ch] if d == 1 else src_ref.at[c0 + ch][pl.ds(r, n // d, stride=d), :]
            dst_ref[:, r * GROUP_W + ch * LANES:r * GROUP_W + (ch + 1) * LANES] = rows.astype(dst_ref.dtype)


def _from_dilated(val, dst_ref, c0, d):
    n = dst_ref.shape[1]
    for r in range(d):
        for ch in range(GROUP_CH):
            cols = val[:, r * GROUP_W + ch * LANES:r * GROUP_W + (ch + 1) * LANES]
            if d == 1:
                dst_ref[c0 + ch] = cols
            else:
                dst_ref.at[c0 + ch][pl.ds(r, n // d, stride=d), :] = cols


def _group(ref, c0):
    return jnp.concatenate([ref[c0 + ch] for ch in range(GROUP_CH)], axis=1)


def _put_group(ref, c0, val):
    for ch in range(GROUP_CH):
        ref[c0 + ch] = val[:, ch * LANES:(ch + 1) * LANES]


def _dil_spec(tm, d):
    return pl.BlockSpec((tm // d, d * GROUP_W), lambda i: (i, 0))


def _proj_fwd(x, vec, w4, tabs, n_rope, dils, scale=1.0):
    S, D = x.shape
    cs = w4.shape[2]
    N = 4 * cs
    tm = _tile(S, TM)

    def body(x_ref, vec_ref, w_ref, tab_ref, h_ref, *rest):
        outs, acc_ref = rest[:-1], rest[-1]
        h = _norm_mod(x_ref[...], vec_ref[...]).astype(BF)
        h_ref[...] = h
        per = cs // LANES
        for q in range(4):
            acc = _dot(h, w_ref[q])
            if scale != 1.0:
                acc = acc * scale
            cols = [acc[:, ch * LANES:(ch + 1) * LANES] for ch in range(per)]
            for ch, t in enumerate(cols):
                acc_ref[q * per + ch] = _rope(t, tab_ref) if q * per + ch < n_rope else t
        for j, d in enumerate(dils):
            _to_dilated(acc_ref, j * GROUP_CH, outs[j], d)

    h, *outs = pl.pallas_call(
        body, name="proj_fwd", grid=(S // tm,),
        in_specs=[pl.BlockSpec((tm, D), lambda i: (i, 0)), pl.BlockSpec((8, D), lambda i: (0, 0)),
                  pl.BlockSpec((4, D, cs), lambda i: (0, 0, 0)), pl.BlockSpec((3, tm, LANES), lambda i: (0, i, 0))],
        out_specs=[pl.BlockSpec((tm, D), lambda i: (i, 0))] + [_dil_spec(tm, d) for d in dils],
        out_shape=[jax.ShapeDtypeStruct((S, D), BF)] + [jax.ShapeDtypeStruct((S // d, d * GROUP_W), BF) for d in dils],
        scratch_shapes=[pltpu.VMEM((N // LANES, tm, LANES), F32)],
        compiler_params=_cparams(1),
    )(x, vec, w4, tabs)
    return h, outs


def _proj_dgrad(parts, tabs, w4, n_rope_groups, dils):
    cs, D = w4.shape[2], w4.shape[1]
    N = 4 * cs
    S = parts[0][0].shape[0] * dils[0]
    tm = _tile(S, TM)
    flat = [p for grp in parts for p in grp]
    counts = [len(grp) for grp in parts]

    def body(*refs):
        prefs = refs[:len(flat)]
        tab_ref, w_ref, dz_ref, dh_ref, z_ref = refs[len(flat):]
        k = 0
        for j, cnt in enumerate(counts):
            z = prefs[k][...]
            for r in prefs[k + 1:k + cnt]:
                z = z + r[...]
            k += cnt
            _from_dilated(z, z_ref, 0, dils[j])
            if j < n_rope_groups:
                z = jnp.concatenate([_rope_t(z_ref[ch], tab_ref) for ch in range(GROUP_CH)], axis=1)
            else:
                z = _group(z_ref, 0)
            dz_ref[:, j * GROUP_W:(j + 1) * GROUP_W] = z.astype(BF)
        dh = _dot_nt(dz_ref[:, 0:cs], w_ref[0])
        for q in range(1, 4):
            dh = dh + _dot_nt(dz_ref[:, q * cs:(q + 1) * cs], w_ref[q])
        dh_ref[...] = dh.astype(BF)

    return pl.pallas_call(
        body, name="proj_dgrad", grid=(S // tm,),
        in_specs=[_dil_spec(tm, d) for d, cnt in zip(dils, counts) for _ in range(cnt)]
        + [pl.BlockSpec((3, tm, LANES), lambda i: (0, i, 0)), pl.BlockSpec((4, D, cs), lambda i: (0, 0, 0))],
        out_specs=[pl.BlockSpec((tm, N), lambda i: (i, 0)), pl.BlockSpec((tm, D), lambda i: (i, 0))],
        out_shape=[jax.ShapeDtypeStruct((S, N), BF), jax.ShapeDtypeStruct((S, D), BF)],
        scratch_shapes=[pltpu.VMEM((GROUP_CH, tm, LANES), F32)],
        compiler_params=_cparams(1),
    )(*flat, tabs, w4)


def _band_masks():
    qi = lax.broadcasted_iota(jnp.int32, (BAND, BAND), 0)
    kj = lax.broadcasted_iota(jnp.int32, (BAND, BAND), 1)
    return kj <= qi, kj >= qi


def _head(ref, h):
    return ref[:, h * HEAD_DIM:(h + 1) * HEAD_DIM]


def _attn_fwd(q, k, v, d):
    sd = q.shape[0]
    nb = sd // BAND

    def body(q_ref, kp_ref, kc_ref, vp_ref, vc_ref, o_ref, l_ref):
        same, prev = _band_masks()
        prev = jnp.logical_and(prev, pl.program_id(1) > 0)
        heads = range(HEADS_PER_GROUP)
        scores = [(jnp.where(same, _dot_nt(_head(q_ref, h), _head(kc_ref, h)), NEG),
                   jnp.where(prev, _dot_nt(_head(q_ref, h), _head(kp_ref, h)), NEG)) for h in heads]
        probs = []
        for sc, sp in scores:
            m = jnp.maximum(jnp.max(sc, axis=-1, keepdims=True), jnp.max(sp, axis=-1, keepdims=True))
            pc = jnp.exp(sc - m)
            pp = jnp.exp(sp - m)
            den = jnp.sum(pc, axis=-1, keepdims=True) + jnp.sum(pp, axis=-1, keepdims=True)
            probs.append((pc.astype(BF), pp.astype(BF), m, den))
        outs = [(_dot(pc, _head(vc_ref, h)) + _dot(pp, _head(vp_ref, h))) / den for h, (pc, pp, _, den) in zip(heads, probs)]
        o_ref[...] = jnp.concatenate(outs, axis=1)
        l_ref[...] = jnp.concatenate([jnp.broadcast_to(m + jnp.log(den), (BAND, HEAD_DIM)) for _, _, m, den in probs], axis=1)

    blk = (BAND, GROUP_W)
    cur = pl.BlockSpec(blk, lambda r, i: (i, r))
    prv = pl.BlockSpec(blk, lambda r, i: (jnp.maximum(i - 1, 0), r))
    return pl.pallas_call(
        body, name=f"attn_fwd_d{d}", grid=(d, nb),
        in_specs=[cur, prv, cur, prv, cur], out_specs=[cur, cur],
        out_shape=[jax.ShapeDtypeStruct((sd, d * GROUP_W), F32)] * 2,
        compiler_params=_cparams(2),
    )(q, k, k, v, v)


def _attn_out(os, ls, x, vec, wo4):
    S, D = x.shape
    cs = wo4.shape[2]
    tm = _tile(S, TM)
    ng = len(DILATIONS)

    def body(*refs):
        o_refs, l_refs = refs[0:2 * ng:2], refs[1:2 * ng:2]
        x_ref, vec_ref, wo_ref = refs[2 * ng:2 * ng + 3]
        outs = refs[2 * ng + 3:]
        mix_refs, lj_refs = outs[0:2 * ng:2], outs[1:2 * ng:2]
        y_ref, xo_ref, nat_ref = outs[2 * ng:]
        ov, lv = [], []
        for g in range(ng):
            _from_dilated(o_refs[g][...], nat_ref, 2 * g * GROUP_CH, DILATIONS[g])
            _from_dilated(l_refs[g][...], nat_ref, (2 * g + 1) * GROUP_CH, DILATIONS[g])
            ov.append(_group(nat_ref, 2 * g * GROUP_CH))
            lv.append(_group(nat_ref, (2 * g + 1) * GROUP_CH))
        mx = jnp.maximum(jnp.maximum(lv[0], lv[1]), lv[2])
        es = [jnp.exp(t - mx) for t in lv]
        den = es[0] + es[1] + es[2]
        mix = (es[0] * ov[0] + es[1] * ov[1] + es[2] * ov[2]) / den
        at_mix, at_lj = 2 * ng * GROUP_CH, (2 * ng + 1) * GROUP_CH
        _put_group(nat_ref, at_mix, mix)
        _put_group(nat_ref, at_lj, mx + jnp.log(den))
        for g in range(ng):
            _to_dilated(nat_ref, at_mix, mix_refs[g], DILATIONS[g])
            _to_dilated(nat_ref, at_lj, lj_refs[g], DILATIONS[g])
        mb = mix.astype(BF)
        y = jnp.concatenate([_dot(mb, wo_ref[q]) for q in range(4)], axis=1)
        y_ref[...] = y.astype(BF)
        xo_ref[...] = x_ref[...] + (1.0 + vec_ref[3:4, :]) * y

    tok = pl.BlockSpec((tm, D), lambda i: (i, 0))
    dil = [_dil_spec(tm, d) for d in DILATIONS for _ in range(2)]
    dil_shape = [jax.ShapeDtypeStruct((S // d, d * GROUP_W), F32) for d in DILATIONS for _ in range(2)]
    outs = pl.pallas_call(
        body, name="attn_out", grid=(S // tm,),
        in_specs=dil + [tok, pl.BlockSpec((8, D), lambda i: (0, 0)), pl.BlockSpec((4, GROUP_W, cs), lambda i: (0, 0, 0))],
        out_specs=dil + [tok, tok],
        out_shape=dil_shape + [jax.ShapeDtypeStruct((S, D), BF), jax.ShapeDtypeStruct((S, D), F32)],
        scratch_shapes=[pltpu.VMEM(((2 * ng + 2) * GROUP_CH, tm, LANES), F32)],
        compiler_params=_cparams(1),
    )(*[t for pair in zip(os, ls) for t in pair], x, vec, wo4)
    return list(outs[0:2 * ng:2]), list(outs[1:2 * ng:2]), outs[2 * ng], outs[2 * ng + 1]


def _attn_out_dgrad(dxo, vec, wo4):
    S, D = dxo.shape
    cs = wo4.shape[2]
    tm = _tile(S, TM)

    def body(dxo_ref, vec_ref, wo_ref, dy_ref, *rest):
        dm_refs, nat_ref = rest[:-1], rest[-1]
        dy = ((1.0 + vec_ref[3:4, :]) * dxo_ref[...]).astype(BF)
        dy_ref[...] = dy
        dm = _dot_nt(dy[:, 0:cs], wo_ref[0])
        for q in range(1, 4):
            dm = dm + _dot_nt(dy[:, q * cs:(q + 1) * cs], wo_ref[q])
        _put_group(nat_ref, 0, dm)
        for g, d in enumerate(DILATIONS):
            _to_dilated(nat_ref, 0, dm_refs[g], d)

    tok = pl.BlockSpec((tm, D), lambda i: (i, 0))
    dy, *dms = pl.pallas_call(
        body, name="attn_out_dgrad", grid=(S // tm,),
        in_specs=[tok, pl.BlockSpec((8, D), lambda i: (0, 0)), pl.BlockSpec((4, GROUP_W, cs), lambda i: (0, 0, 0))],
        out_specs=[tok] + [_dil_spec(tm, d) for d in DILATIONS],
        out_shape=[jax.ShapeDtypeStruct((S, D), BF)] + [jax.ShapeDtypeStruct((S // d, d * GROUP_W), F32) for d in DILATIONS],
        scratch_shapes=[pltpu.VMEM((GROUP_CH, tm, LANES), F32)],
        compiler_params=_cparams(1),
    )(dxo, vec, wo4)
    return dy, dms


def _attn_bwd(q, k, v, dmix, mix, lj, d):
    sd = q.shape[0]
    nb = sd // BAND

    def body(q_ref, kp_ref, kc_ref, vp_ref, vc_ref, do_ref, o_ref, l_ref, dq_ref, dk_ref, dv_ref, keep_k, keep_v):
        i = pl.program_id(1)

        @pl.when(i == 0)
        def _():
            keep_k[...] = jnp.zeros_like(keep_k)
            keep_v[...] = jnp.zeros_like(keep_v)

        @pl.when(i < nb)
        def _():
            same, prev = _band_masks()
            prev = jnp.logical_and(prev, i > 0)
            heads = range(HEADS_PER_GROUP)
            dob = [_head(do_ref, h).astype(BF) for h in heads]
            raw = [(_dot_nt(_head(q_ref, h), _head(kc_ref, h)), _dot_nt(_head(q_ref, h), _head(kp_ref, h)),
                    _dot_nt(dob[h], _head(vc_ref, h)), _dot_nt(dob[h], _head(vp_ref, h))) for h in heads]
            mid = []
            for h, (sc, sp, dpc, dpp) in zip(heads, raw):
                lrow = l_ref[:, h * HEAD_DIM:h * HEAD_DIM + 1]
                delta = jnp.sum(_head(do_ref, h) * _head(o_ref, h), axis=-1, keepdims=True)
                pc = jnp.exp(jnp.where(same, sc, NEG) - lrow)
                pp = jnp.exp(jnp.where(prev, sp, NEG) - lrow)
                mid.append((pc.astype(BF), pp.astype(BF), (pc * (dpc - delta)).astype(BF), (pp * (dpp - delta)).astype(BF)))
            dq = [(_dot(dsc, _head(kc_ref, h)) + _dot(dsp, _head(kp_ref, h))) * SM_SCALE for h, (_, _, dsc, dsp) in zip(heads, mid)]
            dk_prev = [_dot_tn(dsp, _head(q_ref, h)) for h, (_, _, _, dsp) in zip(heads, mid)]
            dv_prev = [_dot_tn(pp, dob[h]) for h, (_, pp, _, _) in zip(heads, mid)]
            dk_same = [_dot_tn(dsc, _head(q_ref, h)) for h, (_, _, dsc, _) in zip(heads, mid)]
            dv_same = [_dot_tn(pc, dob[h]) for h, (pc, _, _, _) in zip(heads, mid)]
            dq_ref[...] = jnp.concatenate(dq, axis=1)
            dk_ref[...] = keep_k[...] + jnp.concatenate(dk_prev, axis=1)
            dv_ref[...] = keep_v[...] + jnp.concatenate(dv_prev, axis=1)
            keep_k[...] = jnp.concatenate(dk_same, axis=1)
            keep_v[...] = jnp.concatenate(dv_same, axis=1)

        @pl.when(i == nb)
        def _():
            dk_ref[...] = keep_k[...]
            dv_ref[...] = keep_v[...]

    blk = (BAND, GROUP_W)
    cur = pl.BlockSpec(blk, lambda r, i: (jnp.minimum(i, nb - 1), r))
    prv = pl.BlockSpec(blk, lambda r, i: (jnp.clip(i - 1, 0, nb - 1), r))
    return pl.pallas_call(
        body, name=f"attn_bwd_d{d}", grid=(d, nb + 1),
        in_specs=[cur, prv, cur, prv, cur, cur, cur, cur], out_specs=[cur, prv, prv],
        out_shape=[jax.ShapeDtypeStruct((sd, d * GROUP_W), F32)] * 3,
        scratch_shapes=[pltpu.VMEM(blk, F32), pltpu.VMEM(blk, F32)],
        compiler_params=_cparams(2),
    )(q, k, k, v, v, dmix, mix, lj)


def _final_loss(x, gvec, tgt):
    S, D = x.shape
    tm = _tile(S, TM)

    def body(x_ref, g_ref, t_ref, dx_ref, red_ref):
        @pl.when(pl.program_id(0) == 0)
        def _():
            red_ref[...] = jnp.zeros_like(red_ref)

        xv = x_ref[...]
        g = g_ref[0:1, :]
        r = _rstd(xv)
        xh = xv * r
        err = xh * g - t_ref[...]
        dy = err * (1.0 / D)
        dxh = dy * g
        dx_ref[...] = r * (dxh - xh * jnp.mean(dxh * xh, axis=-1, keepdims=True))
        red_ref[0:1, :] += jnp.sum(dy * xh, axis=0, keepdims=True)
        red_ref[1:2, :] += jnp.sum(err * err, axis=0, keepdims=True)

    tok = pl.BlockSpec((tm, D), lambda i: (i, 0))
    small = pl.BlockSpec((8, D), lambda i: (0, 0))
    return pl.pallas_call(
        body, name="final_loss", grid=(S // tm,),
        in_specs=[tok, small, tok], out_specs=[tok, small],
        out_shape=[jax.ShapeDtypeStruct((S, D), F32), jax.ShapeDtypeStruct((8, D), F32)],
        compiler_params=_cparams(1),
    )(x, gvec, tgt)


def _adamw(w, g, m, v, copy_g=False):
    shape = w.shape
    C = shape[-1]
    R = w.size // C
    tr = 256 if R % 256 == 0 else R

    def body(w_ref, g_ref, m_ref, v_ref, d_ref, nm_ref, nv_ref, *g_out):
        gv = g_ref[...]
        nm = ADAM_B1 * m_ref[...] + (1.0 - ADAM_B1) * gv
        nv = ADAM_B2 * v_ref[...] + (1.0 - ADAM_B2) * (gv * gv)
        m_hat = nm / (1.0 - ADAM_B1 ** ADAM_STEP)
        v_hat = nv / (1.0 - ADAM_B2 ** ADAM_STEP)
        d_ref[...] = -ADAM_LR * (m_hat / (jnp.sqrt(v_hat) + ADAM_EPS) + ADAM_WD * w_ref[...])
        nm_ref[...] = nm
        nv_ref[...] = nv
        if copy_g:
            g_out[0][...] = gv

    spec = pl.BlockSpec((tr, C), lambda i: (i, 0))
    n_out = 4 if copy_g else 3
    outs = pl.pallas_call(
        body, name="adamw", grid=(R // tr,),
        in_specs=[spec] * 4, out_specs=[spec] * n_out,
        out_shape=[jax.ShapeDtypeStruct((R, C), F32)] * n_out,
        compiler_params=_cparams(1),
    )(*(t.reshape(R, C) for t in (w, g, m, v)))
    return tuple(o.reshape(shape) for o in (*outs[:3], outs[3] if copy_g else g))


def _ada_fwd(c_all, w, b):
    L, D, N = w.shape
    tn = 768 if N % 768 == 0 else _tile(N, 512)

    def body(c_ref, w_ref, b_ref, o_ref):
        cv = c_ref[...]
        cond = (cv * jax.nn.sigmoid(cv)).astype(BF)
        o_ref[...] = _dot(cond, w_ref[...].astype(BF)) + b_ref[...]

    return pl.pallas_call(
        body, name="ada_fwd", grid=(L, N // tn),
        in_specs=[pl.BlockSpec((8, D), lambda l, n: (0, 0)), pl.BlockSpec((None, D, tn), lambda l, n: (l, 0, n)),
                  pl.BlockSpec((None, 1, tn), lambda l, n: (l, 0, n))],
        out_specs=pl.BlockSpec((None, 8, tn), lambda l, n: (l, 0, n)),
        out_shape=jax.ShapeDtypeStruct((L, 8, N), F32),
        compiler_params=_cparams(2),
    )(c_all, w, b)


def _ada_wgrad(c_all_t, dm):
    D = c_all_t.shape[0]
    L, _, N = dm.shape
    tn = 256

    def body(c_ref, dm_ref, o_ref):
        cv = c_ref[...]
        cond = cv * jax.nn.sigmoid(cv)
        acc = cond[:, 0:1] * dm_ref[0:1, :]
        for b in range(1, 8):
            acc = acc + cond[:, b:b + 1] * dm_ref[b:b + 1, :]
        o_ref[...] = acc

    return pl.pallas_call(
        body, name="ada_wgrad", grid=(L, N // tn),
        in_specs=[pl.BlockSpec((D, 8), lambda l, n: (0, 0)), pl.BlockSpec((None, 8, tn), lambda l, n: (l, 0, n))],
        out_specs=pl.BlockSpec((None, D, tn), lambda l, n: (l, 0, n)),
        out_shape=jax.ShapeDtypeStruct((L, D, N), F32),
        compiler_params=_cparams(2),
    )(c_all_t, dm)


def _sum8(g):
    _, R, C = g.shape

    def body(g_ref, o_ref):
        acc = g_ref[0]
        for b in range(1, 8):
            acc = acc + g_ref[b]
        o_ref[...] = acc

    return pl.pallas_call(body, name="sum8", out_shape=jax.ShapeDtypeStruct((R, C), F32),
                          in_specs=[pl.BlockSpec(memory_space=pltpu.VMEM)],
                          out_specs=pl.BlockSpec(memory_space=pltpu.VMEM))(g)


def _pair_add(g4, recv, cidx):
    _, _, ha, B = g4.shape
    tr = _row_tile(ha)

    def body(c_ref, g_ref, r_ref, o_ref):
        o_ref[...] = (g_ref[...].astype(F32) + r_ref[...].astype(F32)).astype(BF)

    return pl.pallas_call(
        body, name="pair_add",
        grid_spec=pltpu.PrefetchScalarGridSpec(
            num_scalar_prefetch=1, grid=(4, ha // tr),
            in_specs=[pl.BlockSpec((None, None, tr, B), lambda s, i, c: (s, c[0], i, 0)),
                      pl.BlockSpec((None, tr, B), lambda s, i, c: (s, i, 0))],
            out_specs=pl.BlockSpec((None, tr, B), lambda s, i, c: (s, i, 0))),
        out_shape=jax.ShapeDtypeStruct((4, ha, B), BF),
        compiler_params=_cparams(2),
    )(cidx, g4, recv)


def _sum_partials(part, recv, gbuf, n_layers, l, place):
    _, ha, B = part.shape
    tr = _row_tile(ha)

    def body(pc_ref, p_ref, r_ref, *rest):
        o_ref = rest[-1]
        o_ref[...] = ((p_ref[...].astype(F32) + r_ref[0].astype(F32)) + r_ref[1].astype(F32)) + r_ref[2].astype(F32)

    in_specs = [pl.BlockSpec((None, tr, B), lambda i, pc: (pc[0], i, 0)), pl.BlockSpec((3, tr, B), lambda i, pc: (0, i, 0))]
    ops = [part, recv]
    if gbuf is not None:
        in_specs.append(ANY)
        ops.append(gbuf)
    return pl.pallas_call(
        body, name="sum_partials",
        grid_spec=pltpu.PrefetchScalarGridSpec(
            num_scalar_prefetch=1, grid=(ha // tr,), in_specs=in_specs,
            out_specs=pl.BlockSpec((None, None, tr, B), lambda i, pc: (l, pc[1], i, 0))),
        out_shape=jax.ShapeDtypeStruct((n_layers, 2, ha, B), F32),
        input_output_aliases={} if gbuf is None else {3: 0},
        compiler_params=_cparams(1),
    )(place, *ops)


def _cast_shard(shard, l, place):
    _, _, ha, B = shard.shape
    tr = _row_tile(ha)

    def body(pc_ref, s_ref, o_ref):
        o_ref[...] = s_ref[...].astype(BF)

    return pl.pallas_call(
        body, name="cast_shard",
        grid_spec=pltpu.PrefetchScalarGridSpec(
            num_scalar_prefetch=1, grid=(2, ha // tr),
            in_specs=[pl.BlockSpec((None, None, tr, B), lambda h, i, pc: (l, h, i, 0))],
            out_specs=pl.BlockSpec((None, None, tr, B), lambda h, i, pc: (pc[0], h, i, 0))),
        out_shape=jax.ShapeDtypeStruct((4, 2, ha, B), BF),
        compiler_params=_cparams(2),
    )(place, shard)


def _gather8(v):
    R, C = v.shape

    def body(v_ref, o_ref, ssem, rsem):
        x, y, c, *_ = _place()
        me = 4 * x + 2 * y + c
        o_ref[me] = v_ref[...]
        cps = []
        for k in range(1, 8):
            fx, fy, fc = (k >> 2) & 1, (k >> 1) & 1, k & 1
            peer = (x ^ fx, y ^ fy, c ^ fc)
            cp = pltpu.make_async_remote_copy(src_ref=v_ref, dst_ref=o_ref.at[me], send_sem=ssem.at[k - 1],
                                              recv_sem=rsem.at[k - 1], device_id=peer, device_id_type=MESH)
            cp.start()
            cps.append((cp, 4 * peer[0] + 2 * peer[1] + peer[2]))
        for k, (cp, slot) in enumerate(cps):
            there = o_ref.at[slot]
            pltpu.make_async_remote_copy(src_ref=there, dst_ref=there, send_sem=ssem.at[k], recv_sem=rsem.at[k],
                                         device_id=(x, y, c), device_id_type=MESH).wait_recv()
        for cp, _ in cps:
            cp.wait_send()

    vm = pl.BlockSpec(memory_space=pltpu.VMEM)
    return pl.pallas_call(
        body, name="gather8", in_specs=[vm], out_specs=vm, out_shape=jax.ShapeDtypeStruct((8, R, C), F32),
        scratch_shapes=[pltpu.SemaphoreType.DMA((7,)), pltpu.SemaphoreType.DMA((7,))],
    )(v)


def _mods_to_owner(ms):
    _, R, C = ms.shape

    def body(m_ref, o_ref, ssem, rsem):
        x, y, c, p, chips, qs = _place()
        o_ref[p] = m_ref[2 * p + c]
        cps = []
        for j in range(3):
            cp = pltpu.make_async_remote_copy(src_ref=m_ref.at[2 * qs[j] + c], dst_ref=o_ref.at[p], send_sem=ssem.at[j],
                                              recv_sem=rsem.at[j], device_id=(*chips[j], c), device_id_type=MESH)
            cp.start()
            cps.append(cp)
        for j in range(3):
            there = o_ref.at[qs[j]]
            pltpu.make_async_remote_copy(src_ref=there, dst_ref=there, send_sem=ssem.at[j], recv_sem=rsem.at[j],
                                         device_id=(x, y, c), device_id_type=MESH).wait_recv()
        for cp in cps:
            cp.wait_send()

    vm = pl.BlockSpec(memory_space=pltpu.VMEM)
    return pl.pallas_call(
        body, name="mods_to_owner", in_specs=[vm], out_specs=vm, out_shape=jax.ShapeDtypeStruct((4, R, C), F32),
        scratch_shapes=[pltpu.SemaphoreType.DMA((3,)), pltpu.SemaphoreType.DMA((3,))],
    )(ms)


def _vec(*rows):
    D = rows[0].shape[-1]
    rows = [r.reshape(1, D) for r in rows]
    return jnp.concatenate(rows + [jnp.zeros((8 - len(rows), D), F32)], axis=0)


def _sets(n_layers, n_conv):
    sets = []
    for l in range(n_layers):
        j = l - n_conv
        mixer = [("conv_w_in", l), ("conv_w_out", l)] if l < n_conv else [("attn_w_q", j), ("attn_w_o", j)]
        sets.append([("ffn1_w_in", l), ("ffn1_w_out", l)] + ([("w_kv", 0)] if l == n_conv else []))
        sets.append(mixer + [("ffn2_w_in", l), ("ffn2_w_out", l)])
    return sets


def _as_weight(name, buf):
    _, _, ha, B = buf.shape
    return buf.reshape(8 * ha, B) if name.endswith("w_out") else buf.reshape(4, 2 * ha, B)


def _local_step(x, tgt, tabs, norm_g, conv_w, kv_norm_g, final_norm_g, mods, kvmods, cast, place):
    S, D = x.shape
    n_layers, n_conv = norm_g.shape[0], conv_w.shape[0]
    md = mods.reshape(n_layers, 9, D)

    def vec_of(l, k):
        return _vec(norm_g[l, k], md[l, 3 * k + 1], md[l, 3 * k], md[l, 3 * k + 2])

    W = {}

    def install(items, bufs):
        for (name, idx), b in zip(items, bufs):
            W[name, idx] = _as_weight(name, b)

    Q = _sets(n_layers, n_conv)
    assert n_conv >= 1, "the first layer's conv mixer carries a weight set"
    install(Q[0], _comm_call("gather_d2d", _comm_call("gather_ici", [cast[it] for it in Q[0]])))
    arrived = {}

    def ffn_fwd(n, *args):
        ici = [k for k in ([1] if n == 0 else [n + 2]) if k < len(Q)]
        d2d = [n + 1] if 2 <= n + 1 < len(Q) else []
        comm = [("gather_ici", [cast[it] for it in Q[k]]) for k in ici] + [("gather_d2d", arrived.pop(k)) for k in d2d]
        xo, h, ab, y, *moved = _ffn_fwd(*args, comm)
        for k in ici:
            arrived[k], moved = moved[:len(Q[k])], moved[len(Q[k]):]
        for k in d2d:
            install(Q[k], moved[:len(Q[k])])
            moved = moved[len(Q[k]):]
        if n == 0:
            install(Q[1], _comm_call("gather_d2d", arrived.pop(1)))
        return xo, h, ab, y

    vec_kv = _vec(kv_norm_g, kvmods[D:], kvmods[:D])
    ng = len(DILATIONS)
    n_rope = ng * GROUP_W // LANES
    saved = {}
    xc = x
    ks = vs = None
    for l in range(n_layers):
        if l == n_conv:
            h_kv, kv = _proj_fwd(xc, vec_kv, W["w_kv", 0], tabs, n_rope, DILATIONS * 2)
            ks, vs = kv[:ng], kv[ng:]
            saved["kv"] = (xc, h_kv)
        v0 = vec_of(l, 0)
        xo, h, ab, y = ffn_fwd(2 * l, xc, v0, W["ffn1_w_in", l], W["ffn1_w_out", l])
        saved[l, 0] = (xc, v0, h, ab, y)
        xc = xo
        v1 = vec_of(l, 1)
        if l < n_conv:
            cw = _vec(conv_w[l, 0], conv_w[l, 1], conv_w[l, 2])
            early = [("gather_ici", [cast[it] for it in Q[2]])] if l == 0 and len(Q) > 2 else None
            xo, h, s4, z, y, *moved = _conv_fwd(xc, v1, cw, W["conv_w_in", l], W["conv_w_out", l], early)
            if early:
                arrived[2] = moved
            saved[l, 1] = (xc, v1, h, y, cw, s4, z)
        else:
            j = l - n_conv
            h, qs = _proj_fwd(xc, v1, W["attn_w_q", j], tabs, n_rope, DILATIONS, SM_SCALE)
            og = [_attn_fwd(qs[g], ks[g], vs[g], DILATIONS[g]) for g in range(ng)]
            mixs, ljs, y, xo = _attn_out([o for o, _ in og], [s for _, s in og], xc, v1, W["attn_w_o", j])
            saved[l, 1] = (xc, v1, h, y, qs, mixs, ljs)
        xc = xo
        v2 = vec_of(l, 2)
        xo, h, ab, y = ffn_fwd(2 * l + 1, xc, v2, W["ffn2_w_in", l], W["ffn2_w_out", l])
        saved[l, 2] = (xc, v2, h, ab, y)
        xc = xo

    dx, red_f = _final_loss(xc, _vec(final_norm_g), tgt)
    loss_part = 0.5 * jnp.sum(red_f[1]) / D
    G, parts, others = {}, {}, {}
    dmods = [[None] * 9 for _ in range(n_layers)]
    dnorm = [[None] * 3 for _ in range(n_layers)]
    dconvw = [None] * n_conv
    dkv_parts = []

    def halves(items):
        return [G[it].reshape(4, 2, G[it].shape[1] // 2, G[it].shape[2]) for it in items]

    def pair_sums(items, recv):
        for it, g, r in zip(items, halves(items), recv):
            parts[it] = _pair_add(g, r, place[1:])
        return [parts[it] for it in items]

    R = Q[::-1]
    recvd = {}

    def ffn_bwd(dx, l, k, name):
        m = 2 * (n_layers - 1 - l) + (0 if k == 2 else 1)
        ex = [m - 1] if m >= 1 else []
        sc = [m - 2] if m >= 2 else []
        comm = [("exchange", halves(R[e])) for e in ex] + [("scatter", pair_sums(R[e], recvd.pop(e))) for e in sc]
        xin, v, h, ab, y = saved[l, k]
        w_in, w_out = W[name + "_w_in", l], W[name + "_w_out", l]
        dy, u, dab, dh, *moved = _ffn_dgrad(dx, v, ab, w_in, w_out, comm)
        for e in ex:
            recvd[e], moved = moved[:len(R[e])], moved[len(R[e]):]
        for e in sc:
            others.update(zip(R[e], moved[:len(R[e])]))
            moved = moved[len(R[e]):]
        if m == len(R) - 1:
            G[name + "_w_in", l], got = _mm_tn(h, dab, w_in.shape[2], [("scatter", pair_sums(R[m - 1], recvd.pop(m - 1)))])
            others.update(zip(R[m - 1], got))
        else:
            G[name + "_w_in", l] = _mm_tn(h, dab, w_in.shape[2])
        G[name + "_w_out", l] = _mm_tn(u, dy[None], D).reshape(4, -1, D)
        return _norm_bwd(dx, [dh], xin, y, v, 0.5)

    def note(l, k, red):
        dnorm[l][k] = red[0]
        dmods[l][3 * k], dmods[l][3 * k + 1], dmods[l][3 * k + 2] = red[2], red[1], red[3]

    for l in reversed(range(n_layers)):
        dx, red = ffn_bwd(dx, l, 2, "ffn2")
        note(l, 2, red)
        if l < n_conv:
            xin, v, h, y, cw, s4, z = saved[l, 1]
            w_in, w_out = W["conv_w_in", l], W["conv_w_out", l]
            dy, d3, dh, dcw = _conv_dgrad(dx, v, cw, s4, w_in, w_out)
            G["conv_w_in", l] = _mm_tn(h, d3[None], w_in.shape[2])
            G["conv_w_out", l] = _mm_tn(z, dy[None], D).reshape(4, -1, D)
            dconvw[l] = dcw[0:3]
        else:
            j = l - n_conv
            xin, v, h, y, qs, mixs, ljs = saved[l, 1]
            wq, wo = W["attn_w_q", j], W["attn_w_o", j]
            dy, dmixs = _attn_out_dgrad(dx, v, wo)
            G["attn_w_o", j] = _mm_tn(mixs[0], dy[None], wo.shape[2])
            dqkv = [_attn_bwd(qs[g], ks[g], vs[g], dmixs[g], mixs[g], ljs[g], DILATIONS[g]) for g in range(ng)]
            dkv_parts.append([t[1:] for t in dqkv])
            dq, dh = _proj_dgrad([[t[0]] for t in dqkv], tabs, wq, ng, DILATIONS)
            G["attn_w_q", j] = _mm_tn(h, dq[None], wq.shape[2])
        dx, red = _norm_bwd(dx, [dh], xin, y, v, 1.0)
        note(l, 1, red)
        dx, red = ffn_bwd(dx, l, 0, "ffn1")
        note(l, 0, red)
        if l == n_conv:
            xin, h_kv = saved["kv"]
            kv_parts = [[lay[g][0] for lay in dkv_parts] for g in range(ng)] + [[lay[g][1] for lay in dkv_parts] for g in range(ng)]
            dkv, dh = _proj_dgrad(kv_parts, tabs, W["w_kv", 0], ng, DILATIONS * 2)
            G["w_kv", 0] = _mm_tn(h_kv, dkv[None], W["w_kv", 0].shape[2])
            dx, red_kv = _norm_bwd(dx, [dh], xin, None, vec_kv, 1.0)
    others.update(zip(R[-1], _comm_call("scatter", pair_sums(R[-1], _comm_call("exchange", halves(R[-1]))))))
    small = jnp.concatenate(
        [jnp.stack([jnp.stack(r) for r in dmods]).reshape(-1), red_kv[2], red_kv[1],
         jnp.stack([jnp.stack(r) for r in dnorm]).reshape(-1), jnp.stack(dconvw).reshape(-1), red_kv[0], red_f[0]])
    return loss_part, dx, parts, others, small


BIG = ("ffn1_w_in", "ffn1_w_out", "ffn2_w_in", "ffn2_w_out", "conv_w_in", "conv_w_out", "w_kv", "attn_w_q", "attn_w_o")


def _halved(t):
    if t.ndim == 2:
        t = t[None]
    L, A, B = t.shape
    return t.reshape(L, 2, A // 2, B)


def kernel(x, c, positions, norm_g, ada_w, ada_b, ffn1_w_in, ffn1_w_out, ffn2_w_in, ffn2_w_out, conv_w_in, conv_w, conv_w_out, kv_norm_g, kv_ada_w, kv_ada_b, w_kv, attn_w_q, attn_w_o, final_norm_g, loss_target, m_norm_g, m_ada_w, m_ada_b, m_ffn1_w_in, m_ffn1_w_out, m_ffn2_w_in, m_ffn2_w_out, m_conv_w_in, m_conv_w, m_conv_w_out, m_kv_norm_g, m_kv_ada_w, m_kv_ada_b, m_w_kv, m_attn_w_q, m_attn_w_o, m_final_norm_g, v_norm_g, v_ada_w, v_ada_b, v_ffn1_w_in, v_ffn1_w_out, v_ffn2_w_in, v_ffn2_w_out, v_conv_w_in, v_conv_w, v_conv_w_out, v_kv_norm_g, v_kv_ada_w, v_kv_ada_b, v_w_kv, v_attn_w_q, v_attn_w_o, v_final_norm_g):
    wts = dict(norm_g=norm_g, ada_w=ada_w, ada_b=ada_b, ffn1_w_in=ffn1_w_in, ffn1_w_out=ffn1_w_out, ffn2_w_in=ffn2_w_in,
               ffn2_w_out=ffn2_w_out, conv_w_in=conv_w_in, conv_w=conv_w, conv_w_out=conv_w_out, kv_norm_g=kv_norm_g,
               kv_ada_w=kv_ada_w, kv_ada_b=kv_ada_b, w_kv=w_kv, attn_w_q=attn_w_q, attn_w_o=attn_w_o,
               final_norm_g=final_norm_g)
    ms = dict(norm_g=m_norm_g, ada_w=m_ada_w, ada_b=m_ada_b, ffn1_w_in=m_ffn1_w_in, ffn1_w_out=m_ffn1_w_out,
              ffn2_w_in=m_ffn2_w_in, ffn2_w_out=m_ffn2_w_out, conv_w_in=m_conv_w_in, conv_w=m_conv_w,
              conv_w_out=m_conv_w_out, kv_norm_g=m_kv_norm_g, kv_ada_w=m_kv_ada_w, kv_ada_b=m_kv_ada_b, w_kv=m_w_kv,
              attn_w_q=m_attn_w_q, attn_w_o=m_attn_w_o, final_norm_g=m_final_norm_g)
    vs = dict(norm_g=v_norm_g, ada_w=v_ada_w, ada_b=v_ada_b, ffn1_w_in=v_ffn1_w_in, ffn1_w_out=v_ffn1_w_out,
              ffn2_w_in=v_ffn2_w_in, ffn2_w_out=v_ffn2_w_out, conv_w_in=v_conv_w_in, conv_w=v_conv_w,
              conv_w_out=v_conv_w_out, kv_norm_g=v_kv_norm_g, kv_ada_w=v_kv_ada_w, kv_ada_b=v_kv_ada_b, w_kv=v_w_kv,
              attn_w_q=v_attn_w_q, attn_w_o=v_attn_w_o, final_norm_g=v_final_norm_g)
    order = list(wts)
    S, D = x.shape[1], x.shape[2]
    n_layers, n_conv = norm_g.shape[0], conv_w.shape[0]
    nm, sw = ada_w.shape[2], norm_g.shape[2]
    ix, iy, ic = (lax.axis_index(a) for a in AXES)
    chip = 2 * ix + iy

    pack = jnp.concatenate([norm_g.reshape(-1), conv_w.reshape(-1), c.reshape(-1)])
    npad = (-pack.size) % (8 * LANES)
    allp = _gather8(jnp.pad(pack, (0, npad)).reshape(-1, LANES)).reshape(8, -1)
    n1, n2 = norm_g.size, norm_g.size + conv_w.size
    by_chip = allp[0::2]
    norm_full = jnp.moveaxis(by_chip[:, :n1].reshape(4, n_layers, 3, sw), 0, 2).reshape(n_layers, 3, 4 * sw)
    conv_full = jnp.moveaxis(by_chip[:, n1:n2].reshape(4, n_conv, 3, sw), 0, 2).reshape(n_conv, 3, 4 * sw)
    c_all = allp[:, n2:n2 + D]

    b_l = lax.dynamic_slice_in_dim(ada_b, chip * nm, nm, axis=1)
    mod_sh = _ada_fwd(c_all, ada_w, b_l[:, None, :])
    nkv = kv_ada_w.shape[1]
    bkv = lax.dynamic_slice_in_dim(kv_ada_b, chip * nkv, nkv, axis=0)
    kv_sh = _ada_fwd(c_all, kv_ada_w[None], bkv[None, None, :])
    rows = jnp.concatenate([jnp.moveaxis(mod_sh, 0, 1), jnp.pad(jnp.moveaxis(kv_sh, 0, 1), ((0, 0), (0, 0), (0, nm - nkv)))], axis=1)
    rpad = (-rows.shape[1]) % 8
    mine = _mods_to_owner(jnp.pad(rows, ((0, 0), (0, rpad), (0, 0))))
    mods = jnp.moveaxis(mine[:, :n_layers], 0, 1).reshape(n_layers, 4 * nm)
    kvmods = mine[:, n_layers, :nkv].reshape(4 * nkv)

    place = jnp.stack([chip, ic]).astype(jnp.int32)
    halved = {k: _halved(wts[k]) for k in BIG}
    cast = {(k, l): _cast_shard(halved[k], l, place) for k in BIG for l in range(halved[k].shape[0])}

    loss_part, dx, parts, others, small = _local_step(x[0], loss_target[0], _rope_tables(positions), norm_full, conv_full,
                                                      kv_norm_g, final_norm_g, mods, kvmods, cast, place)
    loss = lax.psum(loss_part, AXES)

    gbufs = []
    for k in BIG:
        gbuf, n_l = None, halved[k].shape[0]
        for l in range(n_l):
            gbuf = _sum_partials(parts[k, l], others[k, l], gbuf, n_l, l, place)
        gbufs.append(gbuf)
    grads = {}

    spad = (-small.size) % (8 * LANES)
    gath = _gather8(jnp.pad(small, (0, spad)).reshape(-1, LANES))
    tot = _sum8(gath).reshape(-1)
    gath = gath.reshape(8, -1)
    o = 0

    def take(n):
        nonlocal o
        o += n
        return tot[o - n:o]

    g_mods = take(n_layers * 9 * D).reshape(n_layers, 9 * D)
    g_kvmods = take(2 * D)
    g_norm = take(n_layers * 3 * D).reshape(n_layers, 3, D)
    g_convw = take(n_conv * 3 * D).reshape(n_conv, 3, D)
    g_kvn = take(D)
    g_fin = take(D)
    grads["ada_b"] = g_mods
    grads["kv_ada_b"] = g_kvmods
    grads["norm_g"] = lax.dynamic_slice_in_dim(g_norm, chip * sw, sw, axis=2)
    grads["conv_w"] = lax.dynamic_slice_in_dim(g_convw, chip * sw, sw, axis=2)
    grads["kv_norm_g"] = g_kvn
    grads["final_norm_g"] = g_fin

    dm_all = gath[:, :n_layers * 9 * D].reshape(8, n_layers, 9 * D)
    dm_mine = jnp.moveaxis(lax.dynamic_slice_in_dim(dm_all, chip * nm, nm, axis=2), 0, 1)
    dkv_all = gath[:, n_layers * 9 * D:n_layers * 9 * D + 2 * D]
    dkv_mine = lax.dynamic_slice_in_dim(dkv_all, chip * nkv, nkv, axis=1)[None]
    c_t = c_all.T
    grads["ada_w"] = _ada_wgrad(c_t, dm_mine)
    grads["kv_ada_w"] = _ada_wgrad(c_t, dkv_mine)[0]

    grads.update({k: s.reshape(wts[k].shape) for k, s in zip(BIG, _comm_call("share", gbufs))})
    deltas, new_m, new_v = {}, {}, {}
    for k in order:
        shp = wts[k].shape
        two_d = (lambda t: t.reshape(1, -1)) if len(shp) == 1 else (lambda t: t)
        dlt, nmk, nvk, gk = _adamw(two_d(wts[k]), two_d(grads[k]), two_d(ms[k]), two_d(vs[k]), k in BIG)
        deltas[k], new_m[k], new_v[k] = dlt.reshape(shp), nmk.reshape(shp), nvk.reshape(shp)
        grads[k] = gk.reshape(shp)
    return (loss, dx[None], *[grads[k] for k in order], *[deltas[k] for k in order], *[new_m[k] for k in order],
            *[new_v[k] for k in order])
```

```python
import functools
import math

import jax
import jax.numpy as jnp
from jax import lax
from jax.experimental import pallas as pl
from jax.experimental.pallas import tpu as pltpu

F32 = jnp.float32
BF = jnp.bfloat16
MESH = pl.DeviceIdType.MESH
AXES = ("x", "y", "c")

NORM_EPS = 1e-5
HEAD_DIM = 64
HEADS_PER_GROUP = 8
GROUP_W = HEADS_PER_GROUP * HEAD_DIM
DILATIONS = (1, 4, 16)
BAND = 128
ROPE_DIM = 16
ROPE_THETA = 500000.0
SM_SCALE = HEAD_DIM ** -0.5
NEG = -1e30
ADAM_LR, ADAM_B1, ADAM_B2, ADAM_EPS, ADAM_WD, ADAM_STEP = 0.001, 0.9, 0.999, 1e-08, 0.01, 10

V7X_VMEM_BYTES = 64 * 1024 * 1024
VMEM_LIMIT = V7X_VMEM_BYTES - 6 * 1024 * 1024
LANES = 128
TM = 512
TM_DGRAD = 256
TM_STREAM = 1024
TK = 2048
WGRAD_COLS = 1536


def _cparams(ngrid):
    return pltpu.CompilerParams(dimension_semantics=("arbitrary",) * ngrid, vmem_limit_bytes=VMEM_LIMIT)


def _dot(a, b):
    return jnp.dot(a, b, preferred_element_type=F32)


def _dot_nt(a, b):
    return lax.dot_general(a, b, (((1,), (1,)), ((), ())), preferred_element_type=F32)


def _dot_tn(a, b):
    return lax.dot_general(a, b, (((0,), (0,)), ((), ())), preferred_element_type=F32)


def _tile(n, pref):
    t = min(n, pref)
    while n % t:
        t //= 2
    return t


def _row_tile(n, cap=512, mult=16):
    best = n
    for t in range(mult, min(n, cap) + 1, mult):
        if n % t == 0:
            best = t
    return best


def _rstd(x):
    return lax.rsqrt(jnp.mean(x * x, axis=-1, keepdims=True) + NORM_EPS)


def _norm_mod(x, v):
    return (x * _rstd(x) * v[0:1]) * (1.0 + v[1:2]) + v[2:3]


def _silu_parts(a):
    sg = jax.nn.sigmoid(a)
    return sg, a * sg


def _row(i):
    return lambda *_: (i, 0)


ANY = pl.BlockSpec(memory_space=pl.ANY)
IN_PLACE = ("gather_ici", "gather_d2d", "share")


def _place():
    x, y, c = (lax.axis_index(a) for a in AXES)
    chips = [(1 - x, y), (x, 1 - y), (1 - x, 1 - y)]
    return x, y, c, 2 * x + y, chips, [2 * cx + cy for cx, cy in chips]


def _transfers(kind, ins, outs):
    x, y, c, p, chips, qs = _place()
    sib = (x, y, 1 - c)
    rows = []
    for k, o in enumerate(outs):
        if kind == "gather_ici":
            rows.append([(o.at[p, c], o.at[p, c], (*chips[j], c), o.at[qs[j], c]) for j in range(3)])
        elif kind == "gather_d2d":
            rows.append([(o.at[qs[j], c], o.at[qs[j], c], sib, o.at[qs[j], 1 - c]) for j in range(3)])
        elif kind == "exchange":
            rows.append([(ins[k].at[:, 1 - c], o, sib, o)])
        elif kind == "scatter":
            rows.append([(ins[k].at[qs[j]], o.at[j], (*chips[j], c), o.at[j]) for j in range(3)])
        elif kind == "share":
            rows.append([(o.at[:, c], o.at[:, c], sib, o.at[:, 1 - c])])
    return rows


def _comm_out_shapes(kind, arrays):
    if kind in IN_PLACE:
        return [jax.ShapeDtypeStruct(a.shape, a.dtype) for a in arrays]
    if kind == "exchange":
        return [jax.ShapeDtypeStruct((4,) + a.shape[2:], a.dtype) for a in arrays]
    return [jax.ShapeDtypeStruct((3,) + a.shape[1:], a.dtype) for a in arrays]


def _comm_sems(n):
    return [pltpu.SemaphoreType.DMA((n, 3)), pltpu.SemaphoreType.DMA((n, 3))]


def _comm_start(rows, ssem, rsem):
    for k, row in enumerate(rows):
        for j, (src, dst, dev, _) in enumerate(row):
            pltpu.make_async_remote_copy(src_ref=src, dst_ref=dst, send_sem=ssem.at[k, j], recv_sem=rsem.at[k, j],
                                         device_id=dev, device_id_type=MESH).start()


def _comm_wait(rows, ssem, rsem):
    for k, row in enumerate(rows):
        for j, (src, dst, dev, land) in enumerate(row):
            pltpu.make_async_remote_copy(src_ref=src, dst_ref=dst, send_sem=ssem.at[k, j], recv_sem=rsem.at[k, j],
                                         device_id=dev, device_id_type=MESH).wait_send()
            pltpu.make_async_remote_copy(src_ref=land, dst_ref=land, send_sem=ssem.at[k, j], recv_sem=rsem.at[k, j],
                                         device_id=dev, device_id_type=MESH).wait_recv()


def _comm_call(kind, arrays):
    n = len(arrays)

    def body(*refs):
        rows = _transfers(kind, refs[:n], refs[n:2 * n])
        _comm_start(rows, *refs[2 * n:])
        _comm_wait(rows, *refs[2 * n:])

    return pl.pallas_call(
        body, name="comm_" + kind, in_specs=[ANY] * n, out_specs=[ANY] * n, out_shape=_comm_out_shapes(kind, arrays),
        input_output_aliases={k: k for k in range(n)} if kind in IN_PLACE else {},
        scratch_shapes=_comm_sems(n),
    )(*arrays)


def _ride(comm, n_in, n_out):
    ride = dict(arrays=[], in_specs=[], out_specs=[], out_shape=[], aliases={}, scratch=[], n=0)
    for kind, arrays in comm or []:
        if kind in IN_PLACE:
            ride["aliases"].update({n_in + ride["n"] + k: n_out + ride["n"] + k for k in range(len(arrays))})
        ride["arrays"] += list(arrays)
        ride["out_shape"] += _comm_out_shapes(kind, arrays)
        ride["scratch"] += _comm_sems(len(arrays))
        ride["n"] += len(arrays)
    ride["in_specs"] = ride["out_specs"] = [ANY] * ride["n"]
    return ride


def _ride_when(comm, ins, outs, sems, cond, action):
    if not comm:
        return

    @pl.when(cond)
    def _():
        at = 0
        for e, (kind, arrays) in enumerate(comm):
            n = len(arrays)
            action(_transfers(kind, ins[at:at + n], outs[at:at + n]), *sems[2 * e:2 * e + 2])
            at += n


def _ffn_fwd(x, vec, w_in4, w_out, comm=None):
    S, D = x.shape
    cs = w_in4.shape[2]
    F = 2 * cs
    tm = _tile(S, TM)
    ni = S // tm
    ride = _ride(comm, 5, 4)
    nc = ride["n"]

    def body(*refs):
        x_ref, vec_ref, wa_ref, wb_ref, wo_ref = refs[:5]
        xo_ref, h_ref, ab_ref, y_ref = refs[5 + nc:9 + nc]
        i = pl.program_id(0)
        riding = (comm, refs[5:5 + nc], refs[9 + nc:9 + 2 * nc], refs[9 + 2 * nc:])
        _ride_when(*riding, i == 0, _comm_start)
        xv = x_ref[...]
        h = _norm_mod(xv, vec_ref[...]).astype(BF)
        h_ref[...] = h
        y = None
        for c in range(2):
            cols = slice(c * cs, (c + 1) * cs)
            a = _dot(h, wa_ref[c])
            b = _dot(h, wb_ref[c])
            ab_ref[0, :, cols] = a.astype(BF)
            ab_ref[1, :, cols] = b.astype(BF)
            _, s = _silu_parts(a)
            part = _dot((s * b).astype(BF), wo_ref[cols, :])
            y = part if y is None else y + part
        y_ref[...] = y.astype(BF)
        xo_ref[...] = xv + (0.5 * (1.0 + vec_ref[3:4, :])) * y
        _ride_when(*riding, i == ni - 1, _comm_wait)

    tok = pl.BlockSpec((tm, D), lambda i: (i, 0))
    once = pl.Buffered(1)
    return pl.pallas_call(
        body, name="ffn_fwd", grid=(ni,),
        in_specs=[tok, pl.BlockSpec((8, D), lambda i: (0, 0)),
                  pl.BlockSpec((2, D, cs), lambda i: (0, 0, 0), pipeline_mode=once),
                  pl.BlockSpec((2, D, cs), lambda i: (1, 0, 0), pipeline_mode=once),
                  pl.BlockSpec((F, D), lambda i: (0, 0), pipeline_mode=once)] + ride["in_specs"],
        out_specs=[tok, tok, pl.BlockSpec((2, tm, F), lambda i: (0, i, 0)), tok] + ride["out_specs"],
        out_shape=[jax.ShapeDtypeStruct((S, D), F32), jax.ShapeDtypeStruct((S, D), BF),
                   jax.ShapeDtypeStruct((2, S, F), BF), jax.ShapeDtypeStruct((S, D), BF)] + ride["out_shape"],
        input_output_aliases=ride["aliases"], scratch_shapes=ride["scratch"],
        compiler_params=_cparams(1),
    )(x, vec, w_in4, w_in4, w_out, *ride["arrays"])


def _norm_bwd_tile(dxo, dh, xv, y, vec_ref, red_ref, coef):
    g = vec_ref[0:1, :]
    sc = vec_ref[1:2, :]
    r = _rstd(xv)
    xh = xv * r
    dhn = dh * (1.0 + sc)
    dxh = dhn * g
    red_ref[0:1, :] += jnp.sum(dhn * xh, axis=0, keepdims=True)
    red_ref[1:2, :] += jnp.sum(dh * (xh * g), axis=0, keepdims=True)
    red_ref[2:3, :] += jnp.sum(dh, axis=0, keepdims=True)
    if y is not None:
        red_ref[3:4, :] += coef * jnp.sum(dxo * y.astype(F32), axis=0, keepdims=True)
    return dxo + r * (dxh - xh * jnp.mean(dxh * xh, axis=-1, keepdims=True))


def _ffn_dgrad(dxo, vec, ab, w_in4, w_out, comm=None):
    S, D = dxo.shape
    cs = w_in4.shape[2]
    F = 2 * cs
    tm = _tile(S, TM_DGRAD)
    ni = S // tm
    ride = _ride(comm, 6, 4)
    nc = ride["n"]

    def body(*refs):
        dxo_ref, vec_ref, ab_ref, wa_ref, wb_ref, wo_ref = refs[:6]
        dy_ref, u_ref, dab_ref, dh_ref = refs[6 + nc:10 + nc]
        i = pl.program_id(0)
        riding = (comm, refs[6:6 + nc], refs[10 + nc:10 + 2 * nc], refs[10 + 2 * nc:])
        _ride_when(*riding, i == 0, _comm_start)
        dy = ((0.5 * (1.0 + vec_ref[3:4, :])) * dxo_ref[...]).astype(BF)
        dy_ref[...] = dy
        dh = None
        for c in range(2):
            cols = slice(c * cs, (c + 1) * cs)
            du = _dot_nt(dy, wo_ref[cols, :])
            a = ab_ref[0, :, cols].astype(F32)
            b = ab_ref[1, :, cols].astype(F32)
            sg, s = _silu_parts(a)
            u_ref[:, cols] = (s * b).astype(BF)
            da = (du * b * (sg * (1.0 + a * (1.0 - sg)))).astype(BF)
            db = (du * s).astype(BF)
            dab_ref[0, :, cols] = da
            dab_ref[1, :, cols] = db
            part = _dot_nt(da, wa_ref[c]) + _dot_nt(db, wb_ref[c])
            dh = part if dh is None else dh + part
        dh_ref[...] = dh.astype(BF)
        _ride_when(*riding, i == ni - 1, _comm_wait)

    tok = pl.BlockSpec((tm, D), lambda i: (i, 0))
    once = pl.Buffered(1)
    return pl.pallas_call(
        body, name="ffn_dgrad", grid=(ni,),
        in_specs=[tok, pl.BlockSpec((8, D), lambda i: (0, 0)),
                  pl.BlockSpec((2, tm, F), lambda i: (0, i, 0)),
                  pl.BlockSpec((2, D, cs), lambda i: (0, 0, 0), pipeline_mode=once),
                  pl.BlockSpec((2, D, cs), lambda i: (1, 0, 0), pipeline_mode=once),
                  pl.BlockSpec((F, D), lambda i: (0, 0), pipeline_mode=once)] + ride["in_specs"],
        out_specs=[tok, pl.BlockSpec((tm, F), lambda i: (i, 0)),
                   pl.BlockSpec((2, tm, F), lambda i: (0, i, 0)), tok] + ride["out_specs"],
        out_shape=[jax.ShapeDtypeStruct((S, D), BF), jax.ShapeDtypeStruct((S, F), BF),
                   jax.ShapeDtypeStruct((2, S, F), BF), jax.ShapeDtypeStruct((S, D), BF)] + ride["out_shape"],
        input_output_aliases=ride["aliases"], scratch_shapes=ride["scratch"],
        compiler_params=_cparams(1),
    )(dxo, vec, ab, w_in4, w_in4, w_out, *ride["arrays"])


def _mm_tn(a, b3, cs, comm=None):
    K, M = a.shape
    G, _, Nb = b3.shape
    N = G * Nb
    if cs >= Nb or Nb % cs:
        ns, tn = 1, math.gcd(cs, Nb)
    else:
        ns = max(s for s in range(1, Nb // cs + 1) if (Nb // cs) % s == 0 and s * cs <= max(WGRAD_COLS, cs))
        tn = ns * cs
    tmo = M if M <= 1024 else M // 2
    tk = _tile(K, TK)
    gm, gn, gk = M // tmo, N // tn, K // tk
    ride = _ride(comm, 2, 1)
    nc = ride["n"]

    def body(*refs):
        a_ref, b_ref = refs[:2]
        o_ref, acc_ref = refs[2 + nc], refs[-1]
        at = [pl.program_id(t) for t in range(3)]
        riding = (comm, refs[2:2 + nc], refs[3 + nc:3 + 2 * nc], refs[3 + 2 * nc:-1])
        _ride_when(*riding, functools.reduce(jnp.logical_and, [t == 0 for t in at]), _comm_start)

        @pl.when(at[2] == 0)
        def _():
            acc_ref[...] = jnp.zeros_like(acc_ref)

        av = a_ref[...].astype(BF)
        if ns == 1:
            acc_ref[...] += _dot_tn(av, b_ref[...].astype(BF))
        else:
            for s in range(ns):
                acc_ref[s] += _dot_tn(av, b_ref[:, s * cs:(s + 1) * cs].astype(BF))

        @pl.when(at[2] == gk - 1)
        def _():
            o_ref[...] = acc_ref[...].astype(BF)

        _ride_when(*riding, functools.reduce(jnp.logical_and, [t == g - 1 for t, g in zip(at, (gm, gn, gk))]), _comm_wait)

    if ns == 1:
        out_spec = pl.BlockSpec((None, tmo, tn), lambda m, n, k: ((n * tn) // cs, m, ((n * tn) % cs) // tn))
        acc = pltpu.VMEM((tmo, tn), F32)
    else:
        out_spec = pl.BlockSpec((ns, tmo, cs), lambda m, n, k: (n, m, 0))
        acc = pltpu.VMEM((ns, tmo, cs), F32)
    g, *moved = pl.pallas_call(
        body, name="wgrad_tn", grid=(gm, gn, gk),
        in_specs=[pl.BlockSpec((tk, tmo), lambda m, n, k: (k, m)),
                  pl.BlockSpec((None, tk, tn), lambda m, n, k: ((n * tn) // Nb, k, ((n * tn) % Nb) // tn))] + ride["in_specs"],
        out_specs=[out_spec] + ride["out_specs"],
        out_shape=[jax.ShapeDtypeStruct((N // cs, M, cs), BF)] + ride["out_shape"],
        input_output_aliases=ride["aliases"], scratch_shapes=ride["scratch"] + [acc],
        compiler_params=_cparams(3),
    )(a, b3, *ride["arrays"])
    return (g, moved) if comm else g


def _norm_bwd(dxo, dhs, x, y, vec, coef):
    S, D = x.shape
    tm = _tile(S, TM_STREAM)
    nh = len(dhs)
    has_y = y is not None

    def body(*refs):
        dxo_ref = refs[0]
        dh_refs = refs[1:1 + nh]
        x_ref = refs[1 + nh]
        y_ref = refs[2 + nh] if has_y else None
        vec_ref, dx_ref, red_ref = refs[-3:]

        @pl.when(pl.program_id(0) == 0)
        def _():
            red_ref[...] = jnp.zeros_like(red_ref)

        dh = dh_refs[0][...].astype(F32)
        for r in dh_refs[1:]:
            dh = dh + r[...].astype(F32)
        dx_ref[...] = _norm_bwd_tile(dxo_ref[...], dh, x_ref[...], y_ref[...] if has_y else None, vec_ref, red_ref, coef)

    tok = pl.BlockSpec((tm, D), lambda i: (i, 0))
    small = pl.BlockSpec((8, D), lambda i: (0, 0))
    ops = [dxo, *dhs, x] + ([y] if has_y else []) + [vec]
    return pl.pallas_call(
        body, name="norm_bwd", grid=(S // tm,),
        in_specs=[tok] * (len(ops) - 1) + [small],
        out_specs=[tok, small],
        out_shape=[jax.ShapeDtypeStruct((S, D), F32), jax.ShapeDtypeStruct((8, D), F32)],
        compiler_params=_cparams(1),
    )(*ops)


def _conv_fwd(x, vec, cw, w_in4, w_out, comm=None):
    S, D = x.shape
    cs = w_in4.shape[2]
    tm = _tile(S, TM)
    nt = S // tm
    ride = _ride(comm, 5, 5)
    nc = ride["n"]

    def body(*refs):
        x_ref, vec_ref, cw_ref, wi_ref, wo_ref = refs[:5]
        xo_ref, h_ref, s4_ref, z_ref, y_ref = refs[5 + nc:10 + nc]
        vs_ref = refs[-1]
        riding = (comm, refs[5:5 + nc], refs[10 + nc:10 + 2 * nc], refs[10 + 2 * nc:-1])
        _ride_when(*riding, pl.program_id(0) == 0, _comm_start)

        @pl.when(pl.program_id(0) == 0)
        def _():
            vs_ref[0:8, :] = jnp.zeros((8, D), F32)

        xv = x_ref[...]
        h = _norm_mod(xv, vec_ref[...]).astype(BF)
        h_ref[...] = h
        bcu = jnp.concatenate([_dot(h, wi_ref[q]) for q in range(4)], axis=1)
        bg, cg, u = bcu[:, :D], bcu[:, D:2 * D], bcu[:, 2 * D:]
        v = cg * u
        vs_ref[8:8 + tm, :] = v
        conv = cw_ref[0:1, :] * vs_ref[pl.ds(6, tm), :] + cw_ref[1:2, :] * vs_ref[pl.ds(7, tm), :] + cw_ref[2:3, :] * v
        vs_ref[0:8, :] = vs_ref[tm:tm + 8, :]
        z = (bg * conv).astype(BF)
        s4_ref[0] = bg.astype(BF)
        s4_ref[1] = cg.astype(BF)
        s4_ref[2] = u.astype(BF)
        s4_ref[3] = conv.astype(BF)
        z_ref[...] = z
        y = _dot(z, wo_ref[...])
        y_ref[...] = y.astype(BF)
        xo_ref[...] = xv + (1.0 + vec_ref[3:4, :]) * y
        _ride_when(*riding, pl.program_id(0) == nt - 1, _comm_wait)

    tok = pl.BlockSpec((tm, D), lambda i: (i, 0))
    small = pl.BlockSpec((8, D), lambda i: (0, 0))
    return pl.pallas_call(
        body, name="conv_fwd", grid=(nt,),
        in_specs=[tok, small, small, pl.BlockSpec((4, D, cs), lambda i: (0, 0, 0)),
                  pl.BlockSpec((D, D), lambda i: (0, 0))] + ride["in_specs"],
        out_specs=[tok, tok, pl.BlockSpec((4, tm, D), lambda i: (0, i, 0)), tok, tok] + ride["out_specs"],
        out_shape=[jax.ShapeDtypeStruct((S, D), F32), jax.ShapeDtypeStruct((S, D), BF),
                   jax.ShapeDtypeStruct((4, S, D), BF), jax.ShapeDtypeStruct((S, D), BF),
                   jax.ShapeDtypeStruct((S, D), BF)] + ride["out_shape"],
        input_output_aliases=ride["aliases"], scratch_shapes=ride["scratch"] + [pltpu.VMEM((tm + 8, D), F32)],
        compiler_params=_cparams(1),
    )(x, vec, cw, w_in4, w_out, *ride["arrays"])


def _conv_dgrad(dxo, vec, cw, s4, w_in4, w_out):
    S, D = dxo.shape
    cs = w_in4.shape[2]
    tm = _tile(S, TM)
    nt = S // tm

    def body(dxo_ref, vec_ref, cw_ref, s4_ref, wi_ref, wo_ref, dy_ref, d3_ref, dh_ref, dcw_ref, ds_ref):
        @pl.when(pl.program_id(0) == 0)
        def _():
            ds_ref[tm:tm + 8, :] = jnp.zeros((8, D), F32)
            dcw_ref[...] = jnp.zeros_like(dcw_ref)

        dy = ((1.0 + vec_ref[3:4, :]) * dxo_ref[...]).astype(BF)
        dy_ref[...] = dy
        dz = _dot_nt(dy, wo_ref[...])
        bg = s4_ref[0].astype(F32)
        cg = s4_ref[1].astype(F32)
        u = s4_ref[2].astype(F32)
        conv = s4_ref[3].astype(F32)
        dbg = dz * conv
        dconv = dz * bg
        ds_ref[0:tm, :] = dconv
        d1 = ds_ref[pl.ds(1, tm), :]
        d2 = ds_ref[pl.ds(2, tm), :]
        ds_ref[tm:tm + 8, :] = ds_ref[0:8, :]
        dv = cw_ref[2:3, :] * dconv + cw_ref[1:2, :] * d1 + cw_ref[0:1, :] * d2
        v = cg * u
        dcw_ref[0:1, :] += jnp.sum(d2 * v, axis=0, keepdims=True)
        dcw_ref[1:2, :] += jnp.sum(d1 * v, axis=0, keepdims=True)
        dcw_ref[2:3, :] += jnp.sum(dconv * v, axis=0, keepdims=True)
        dbcu = jnp.concatenate([dbg, dv * u, dv * cg], axis=1).astype(BF)
        d3_ref[...] = dbcu
        dh = _dot_nt(dbcu[:, 0:cs], wi_ref[0])
        for q in range(1, 4):
            dh = dh + _dot_nt(dbcu[:, q * cs:(q + 1) * cs], wi_ref[q])
        dh_ref[...] = dh.astype(BF)

    tok = pl.BlockSpec((tm, D), lambda i: (nt - 1 - i, 0))
    small = pl.BlockSpec((8, D), lambda i: (0, 0))
    return pl.pallas_call(
        body, name="conv_dgrad", grid=(nt,),
        in_specs=[tok, small, small, pl.BlockSpec((4, tm, D), lambda i: (0, nt - 1 - i, 0)),
                  pl.BlockSpec((4, D, cs), lambda i: (0, 0, 0)), pl.BlockSpec((D, D), lambda i: (0, 0))],
        out_specs=[tok, pl.BlockSpec((tm, 3 * D), lambda i: (nt - 1 - i, 0)), tok, small],
        out_shape=[jax.ShapeDtypeStruct((S, D), BF), jax.ShapeDtypeStruct((S, 3 * D), BF),
                   jax.ShapeDtypeStruct((S, D), BF), jax.ShapeDtypeStruct((8, D), F32)],
        scratch_shapes=[pltpu.VMEM((tm + 8, D), F32)],
        compiler_params=_cparams(1),
    )(dxo, vec, cw, s4, w_in4, w_out)


def _rope_tables(positions):
    S = positions.shape[-1]
    inv = ROPE_THETA ** (-jnp.arange(0, ROPE_DIM, 2, dtype=F32) / ROPE_DIM)
    ang = positions.reshape(S, 1).astype(F32) * inv
    cos = jnp.tile(jnp.cos(ang), (1, LANES // 8))
    sin = jnp.tile(jnp.sin(ang), (1, LANES // 8))
    l64 = jnp.arange(LANES) % HEAD_DIM
    return jnp.stack([jnp.where(l64 < ROPE_DIM, cos, 1.0),
                      jnp.where(l64 < ROPE_DIM // 2, -sin, 0.0),
                      jnp.where((l64 >= ROPE_DIM // 2) & (l64 < ROPE_DIM), sin, 0.0)])


def _rope(t, tab_ref):
    return t * tab_ref[0] + pltpu.roll(t, LANES - 8, 1) * tab_ref[1] + pltpu.roll(t, 8, 1) * tab_ref[2]


def _rope_t(d, tab_ref):
    return d * tab_ref[0] + pltpu.roll(d * tab_ref[1], 8, 1) + pltpu.roll(d * tab_ref[2], LANES - 8, 1)


GROUP_CH = GROUP_W // LANES


def _to_dilated(src_ref, c0, dst_ref, d):
    n = src_ref.shape[1]
    for r in range(d):
        for ch in range(GROUP_CH):
            rows = src_ref[c0 + ch] if d == 1 else src_ref.at[c0 + ch][pl.ds(r, n // d, stride=d), :]
            dst_ref[:, r * GROUP_W + ch * LANES:r * GROUP_W + (ch + 1) * LANES] = rows.astype(dst_ref.dtype)


def _from_dilated(val, dst_ref, c0, d):
    n = dst_ref.shape[1]
    for r in range(d):
        for ch in range(GROUP_CH):
            cols = val[:, r * GROUP_W + ch * LANES:r * GROUP_W + (ch + 1) * LANES]
            if d == 1:
                dst_ref[c0 + ch] = cols
            else:
                dst_ref.at[c0 + ch][pl.ds(r, n // d, stride=d), :] = cols


def _group(ref, c0):
    return jnp.concatenate([ref[c0 + ch] for ch in range(GROUP_CH)], axis=1)


def _put_group(ref, c0, val):
    for ch in range(GROUP_CH):
        ref[c0 + ch] = val[:, ch * LANES:(ch + 1) * LANES]


def _dil_spec(tm, d):
    return pl.BlockSpec((tm // d, d * GROUP_W), lambda i: (i, 0))


def _proj_fwd(x, vec, w4, tabs, n_rope, dils, scale=1.0):
    S, D = x.shape
    cs = w4.shape[2]
    N = 4 * cs
    tm = _tile(S, TM)

    def body(x_ref, vec_ref, w_ref, tab_ref, h_ref, *rest):
        outs, acc_ref = rest[:-1], rest[-1]
        h = _norm_mod(x_ref[...], vec_ref[...]).astype(BF)
        h_ref[...] = h
        per = cs // LANES
        for q in range(4):
            acc = _dot(h, w_ref[q])
            if scale != 1.0:
                acc = acc * scale
            cols = [acc[:, ch * LANES:(ch + 1) * LANES] for ch in range(per)]
            for ch, t in enumerate(cols):
                acc_ref[q * per + ch] = _rope(t, tab_ref) if q * per + ch < n_rope else t
        for j, d in enumerate(dils):
            _to_dilated(acc_ref, j * GROUP_CH, outs[j], d)

    h, *outs = pl.pallas_call(
        body, name="proj_fwd", grid=(S // tm,),
        in_specs=[pl.BlockSpec((tm, D), lambda i: (i, 0)), pl.BlockSpec((8, D), lambda i: (0, 0)),
                  pl.BlockSpec((4, D, cs), lambda i: (0, 0, 0)), pl.BlockSpec((3, tm, LANES), lambda i: (0, i, 0))],
        out_specs=[pl.BlockSpec((tm, D), lambda i: (i, 0))] + [_dil_spec(tm, d) for d in dils],
        out_shape=[jax.ShapeDtypeStruct((S, D), BF)] + [jax.ShapeDtypeStruct((S // d, d * GROUP_W), BF) for d in dils],
        scratch_shapes=[pltpu.VMEM((N // LANES, tm, LANES), F32)],
        compiler_params=_cparams(1),
    )(x, vec, w4, tabs)
    return h, outs


def _proj_dgrad(parts, tabs, w4, n_rope_groups, dils, dxo, x, y, vec):
    cs, D = w4.shape[2], w4.shape[1]
    N = 4 * cs
    S = parts[0][0].shape[0] * dils[0]
    tm = _tile(S, TM)
    flat = [p for grp in parts for p in grp]
    counts = [len(grp) for grp in parts]
    has_y = y is not None

    def body(*refs):
        prefs = refs[:len(flat)]
        tab_ref, w_ref, dxo_ref, x_ref = refs[len(flat):len(flat) + 4]
        y_ref = refs[len(flat) + 4] if has_y else None
        vec_ref, dz_ref, dx_ref, red_ref, z_ref = refs[-5:]

        @pl.when(pl.program_id(0) == 0)
        def _():
            red_ref[...] = jnp.zeros_like(red_ref)

        k = 0
        for j, cnt in enumerate(counts):
            z = prefs[k][...]
            for r in prefs[k + 1:k + cnt]:
                z = z + r[...]
            k += cnt
            _from_dilated(z, z_ref, 0, dils[j])
            if j < n_rope_groups:
                z = jnp.concatenate([_rope_t(z_ref[ch], tab_ref) for ch in range(GROUP_CH)], axis=1)
            else:
                z = _group(z_ref, 0)
            dz_ref[:, j * GROUP_W:(j + 1) * GROUP_W] = z.astype(BF)
        dh = _dot_nt(dz_ref[:, 0:cs], w_ref[0])
        for q in range(1, 4):
            dh = dh + _dot_nt(dz_ref[:, q * cs:(q + 1) * cs], w_ref[q])
        dx_ref[...] = _norm_bwd_tile(dxo_ref[...], dh, x_ref[...], y_ref[...] if has_y else None, vec_ref, red_ref, 1.0)

    tok = pl.BlockSpec((tm, D), lambda i: (i, 0))
    small = pl.BlockSpec((8, D), lambda i: (0, 0))
    return pl.pallas_call(
        body, name="proj_dgrad", grid=(S // tm,),
        in_specs=[_dil_spec(tm, d) for d, cnt in zip(dils, counts) for _ in range(cnt)]
        + [pl.BlockSpec((3, tm, LANES), lambda i: (0, i, 0)), pl.BlockSpec((4, D, cs), lambda i: (0, 0, 0)), tok, tok]
        + ([tok] if has_y else []) + [small],
        out_specs=[pl.BlockSpec((tm, N), lambda i: (i, 0)), tok, small],
        out_shape=[jax.ShapeDtypeStruct((S, N), BF), jax.ShapeDtypeStruct((S, D), F32), jax.ShapeDtypeStruct((8, D), F32)],
        scratch_shapes=[pltpu.VMEM((GROUP_CH, tm, LANES), F32)],
        compiler_params=_cparams(1),
    )(*flat, tabs, w4, dxo, x, *([y] if has_y else []), vec)


def _band_masks():
    qi = lax.broadcasted_iota(jnp.int32, (BAND, BAND), 0)
    kj = lax.broadcasted_iota(jnp.int32, (BAND, BAND), 1)
    return kj <= qi, kj >= qi


def _head(ref, h):
    return ref[:, h * HEAD_DIM:(h + 1) * HEAD_DIM]


def _attn_fwd(q, k, v, d):
    sd = q.shape[0]
    nb = sd // BAND

    def body(q_ref, kp_ref, kc_ref, vp_ref, vc_ref, o_ref, l_ref):
        same, prev = _band_masks()
        prev = jnp.logical_and(prev, pl.program_id(1) > 0)
        heads = range(HEADS_PER_GROUP)
        scores = [(jnp.where(same, _dot_nt(_head(q_ref, h), _head(kc_ref, h)), NEG),
                   jnp.where(prev, _dot_nt(_head(q_ref, h), _head(kp_ref, h)), NEG)) for h in heads]
        probs = []
        for sc, sp in scores:
            m = jnp.maximum(jnp.max(sc, axis=-1, keepdims=True), jnp.max(sp, axis=-1, keepdims=True))
            pc = jnp.exp(sc - m)
            pp = jnp.exp(sp - m)
            den = jnp.sum(pc, axis=-1, keepdims=True) + jnp.sum(pp, axis=-1, keepdims=True)
            probs.append((pc.astype(BF), pp.astype(BF), m, den))
        outs = [(_dot(pc, _head(vc_ref, h)) + _dot(pp, _head(vp_ref, h))) / den for h, (pc, pp, _, den) in zip(heads, probs)]
        o_ref[...] = jnp.concatenate(outs, axis=1)
        l_ref[...] = jnp.concatenate([jnp.broadcast_to(m + jnp.log(den), (BAND, HEAD_DIM)) for _, _, m, den in probs], axis=1)

    blk = (BAND, GROUP_W)
    cur = pl.BlockSpec(blk, lambda r, i: (i, r))
    prv = pl.BlockSpec(blk, lambda r, i: (jnp.maximum(i - 1, 0), r))
    return pl.pallas_call(
        body, name=f"attn_fwd_d{d}", grid=(d, nb),
        in_specs=[cur, prv, cur, prv, cur], out_specs=[cur, cur],
        out_shape=[jax.ShapeDtypeStruct((sd, d * GROUP_W), F32)] * 2,
        compiler_params=_cparams(2),
    )(q, k, k, v, v)


def _attn_out(os, ls, x, vec, wo4):
    S, D = x.shape
    cs = wo4.shape[2]
    tm = _tile(S, TM)
    ng = len(DILATIONS)

    def body(*refs):
        o_refs, l_refs = refs[0:2 * ng:2], refs[1:2 * ng:2]
        x_ref, vec_ref, wo_ref = refs[2 * ng:2 * ng + 3]
        outs = refs[2 * ng + 3:]
        mix_refs, lj_refs = outs[0:2 * ng:2], outs[1:2 * ng:2]
        y_ref, xo_ref, nat_ref = outs[2 * ng:]
        ov, lv = [], []
        for g in range(ng):
            _from_dilated(o_refs[g][...], nat_ref, 2 * g * GROUP_CH, DILATIONS[g])
            _from_dilated(l_refs[g][...], nat_ref, (2 * g + 1) * GROUP_CH, DILATIONS[g])
            ov.append(_group(nat_ref, 2 * g * GROUP_CH))
            lv.append(_group(nat_ref, (2 * g + 1) * GROUP_CH))
        mx = jnp.maximum(jnp.maximum(lv[0], lv[1]), lv[2])
        es = [jnp.exp(t - mx) for t in lv]
        den = es[0] + es[1] + es[2]
        mix = (es[0] * ov[0] + es[1] * ov[1] + es[2] * ov[2]) / den
        at_mix, at_lj = 2 * ng * GROUP_CH, (2 * ng + 1) * GROUP_CH
        _put_group(nat_ref, at_mix, mix)
        _put_group(nat_ref, at_lj, mx + jnp.log(den))
        for g in range(ng):
            _to_dilated(nat_ref, at_mix, mix_refs[g], DILATIONS[g])
            _to_dilated(nat_ref, at_lj, lj_refs[g], DILATIONS[g])
        mb = mix.astype(BF)
        y = jnp.concatenate([_dot(mb, wo_ref[q]) for q in range(4)], axis=1)
        y_ref[...] = y.astype(BF)
        xo_ref[...] = x_ref[...] + (1.0 + vec_ref[3:4, :]) * y

    tok = pl.BlockSpec((tm, D), lambda i: (i, 0))
    dil = [_dil_spec(tm, d) for d in DILATIONS for _ in range(2)]
    dil_shape = [jax.ShapeDtypeStruct((S // d, d * GROUP_W), F32) for d in DILATIONS for _ in range(2)]
    outs = pl.pallas_call(
        body, name="attn_out", grid=(S // tm,),
        in_specs=dil + [tok, pl.BlockSpec((8, D), lambda i: (0, 0)), pl.BlockSpec((4, GROUP_W, cs), lambda i: (0, 0, 0))],
        out_specs=dil + [tok, tok],
        out_shape=dil_shape + [jax.ShapeDtypeStruct((S, D), BF), jax.ShapeDtypeStruct((S, D), F32)],
        scratch_shapes=[pltpu.VMEM(((2 * ng + 2) * GROUP_CH, tm, LANES), F32)],
        compiler_params=_cparams(1),
    )(*[t for pair in zip(os, ls) for t in pair], x, vec, wo4)
    return list(outs[0:2 * ng:2]), list(outs[1:2 * ng:2]), outs[2 * ng], outs[2 * ng + 1]


def _attn_out_dgrad(dxo, vec, wo4):
    S, D = dxo.shape
    cs = wo4.shape[2]
    tm = _tile(S, TM)

    def body(dxo_ref, vec_ref, wo_ref, dy_ref, *rest):
        dm_refs, nat_ref = rest[:-1], rest[-1]
        dy = ((1.0 + vec_ref[3:4, :]) * dxo_ref[...]).astype(BF)
        dy_ref[...] = dy
        dm = _dot_nt(dy[:, 0:cs], wo_ref[0])
        for q in range(1, 4):
            dm = dm + _dot_nt(dy[:, q * cs:(q + 1) * cs], wo_ref[q])
        _put_group(nat_ref, 0, dm)
        for g, d in enumerate(DILATIONS):
            _to_dilated(nat_ref, 0, dm_refs[g], d)

    tok = pl.BlockSpec((tm, D), lambda i: (i, 0))
    dy, *dms = pl.pallas_call(
        body, name="attn_out_dgrad", grid=(S // tm,),
        in_specs=[tok, pl.BlockSpec((8, D), lambda i: (0, 0)), pl.BlockSpec((4, GROUP_W, cs), lambda i: (0, 0, 0))],
        out_specs=[tok] + [_dil_spec(tm, d) for d in DILATIONS],
        out_shape=[jax.ShapeDtypeStruct((S, D), BF)] + [jax.ShapeDtypeStruct((S // d, d * GROUP_W), F32) for d in DILATIONS],
        scratch_shapes=[pltpu.VMEM((GROUP_CH, tm, LANES), F32)],
        compiler_params=_cparams(1),
    )(dxo, vec, wo4)
    return dy, dms


def _attn_bwd(q, k, v, dmix, mix, lj, d):
    sd = q.shape[0]
    nb = sd // BAND

    def body(q_ref, kp_ref, kc_ref, vp_ref, vc_ref, do_ref, o_ref, l_ref, dq_ref, dk_ref, dv_ref, keep_k, keep_v):
        i = pl.program_id(1)

        @pl.when(i == 0)
        def _():
            keep_k[...] = jnp.zeros_like(keep_k)
            keep_v[...] = jnp.zeros_like(keep_v)

        @pl.when(i < nb)
        def _():
            same, prev = _band_masks()
            prev = jnp.logical_and(prev, i > 0)
            heads = range(HEADS_PER_GROUP)
            dob = [_head(do_ref, h).astype(BF) for h in heads]
            raw = [(_dot_nt(_head(q_ref, h), _head(kc_ref, h)), _dot_nt(_head(q_ref, h), _head(kp_ref, h)),
                    _dot_nt(dob[h], _head(vc_ref, h)), _dot_nt(dob[h], _head(vp_ref, h))) for h in heads]
            mid = []
            for h, (sc, sp, dpc, dpp) in zip(heads, raw):
                lrow = l_ref[:, h * HEAD_DIM:h * HEAD_DIM + 1]
                delta = jnp.sum(_head(do_ref, h) * _head(o_ref, h), axis=-1, keepdims=True)
                pc = jnp.exp(jnp.where(same, sc, NEG) - lrow)
                pp = jnp.exp(jnp.where(prev, sp, NEG) - lrow)
                mid.append((pc.astype(BF), pp.astype(BF), (pc * (dpc - delta)).astype(BF), (pp * (dpp - delta)).astype(BF)))
            dq = [(_dot(dsc, _head(kc_ref, h)) + _dot(dsp, _head(kp_ref, h))) * SM_SCALE for h, (_, _, dsc, dsp) in zip(heads, mid)]
            dk_prev = [_dot_tn(dsp, _head(q_ref, h)) for h, (_, _, _, dsp) in zip(heads, mid)]
            dv_prev = [_dot_tn(pp, dob[h]) for h, (_, pp, _, _) in zip(heads, mid)]
            dk_same = [_dot_tn(dsc, _head(q_ref, h)) for h, (_, _, dsc, _) in zip(heads, mid)]
            dv_same = [_dot_tn(pc, dob[h]) for h, (pc, _, _, _) in zip(heads, mid)]
            dq_ref[...] = jnp.concatenate(dq, axis=1)
            dk_ref[...] = keep_k[...] + jnp.concatenate(dk_prev, axis=1)
            dv_ref[...] = keep_v[...] + jnp.concatenate(dv_prev, axis=1)
            keep_k[...] = jnp.concatenate(dk_same, axis=1)
            keep_v[...] = jnp.concatenate(dv_same, axis=1)

        @pl.when(i == nb)
        def _():
            dk_ref[...] = keep_k[...]
            dv_ref[...] = keep_v[...]

    blk = (BAND, GROUP_W)
    cur = pl.BlockSpec(blk, lambda r, i: (jnp.minimum(i, nb - 1), r))
    prv = pl.BlockSpec(blk, lambda r, i: (jnp.clip(i - 1, 0, nb - 1), r))
    return pl.pallas_call(
        body, name=f"attn_bwd_d{d}", grid=(d, nb + 1),
        in_specs=[cur, prv, cur, prv, cur, cur, cur, cur], out_specs=[cur, prv, prv],
        out_shape=[jax.ShapeDtypeStruct((sd, d * GROUP_W), F32)] * 3,
        scratch_shapes=[pltpu.VMEM(blk, F32), pltpu.VMEM(blk, F32)],
        compiler_params=_cparams(2),
    )(q, k, k, v, v, dmix, mix, lj)


def _final_loss(x, gvec, tgt):
    S, D = x.shape
    tm = _tile(S, TM)

    def body(x_ref, g_ref, t_ref, dx_ref, red_ref):
        @pl.when(pl.program_id(0) == 0)
        def _():
            red_ref[...] = jnp.zeros_like(red_ref)

        xv = x_ref[...]
        g = g_ref[0:1, :]
        r = _rstd(xv)
        xh = xv * r
        err = xh * g - t_ref[...]
        dy = err * (1.0 / D)
        dxh = dy * g
        dx_ref[...] = r * (dxh - xh * jnp.mean(dxh * xh, axis=-1, keepdims=True))
        red_ref[0:1, :] += jnp.sum(dy * xh, axis=0, keepdims=True)
        red_ref[1:2, :] += jnp.sum(err * err, axis=0, keepdims=True)

    tok = pl.BlockSpec((tm, D), lambda i: (i, 0))
    small = pl.BlockSpec((8, D), lambda i: (0, 0))
    return pl.pallas_call(
        body, name="final_loss", grid=(S // tm,),
        in_specs=[tok, small, tok], out_specs=[tok, small],
        out_shape=[jax.ShapeDtypeStruct((S, D), F32), jax.ShapeDtypeStruct((8, D), F32)],
        compiler_params=_cparams(1),
    )(x, gvec, tgt)


def _adamw(w, g, m, v, copy_g=False):
    shape = w.shape
    C = shape[-1]
    R = w.size // C
    tr = 256 if R % 256 == 0 else R

    def body(w_ref, g_ref, m_ref, v_ref, d_ref, nm_ref, nv_ref, *g_out):
        gv = g_ref[...]
        nm = ADAM_B1 * m_ref[...] + (1.0 - ADAM_B1) * gv
        nv = ADAM_B2 * v_ref[...] + (1.0 - ADAM_B2) * (gv * gv)
        m_hat = nm / (1.0 - ADAM_B1 ** ADAM_STEP)
        v_hat = nv / (1.0 - ADAM_B2 ** ADAM_STEP)
        d_ref[...] = -ADAM_LR * (m_hat / (jnp.sqrt(v_hat) + ADAM_EPS) + ADAM_WD * w_ref[...])
        nm_ref[...] = nm
        nv_ref[...] = nv
        if copy_g:
            g_out[0][...] = gv

    spec = pl.BlockSpec((tr, C), lambda i: (i, 0))
    n_out = 4 if copy_g else 3
    outs = pl.pallas_call(
        body, name="adamw", grid=(R // tr,),
        in_specs=[spec] * 4, out_specs=[spec] * n_out,
        out_shape=[jax.ShapeDtypeStruct((R, C), F32)] * n_out,
        compiler_params=_cparams(1),
    )(*(t.reshape(R, C) for t in (w, g, m, v)))
    return tuple(o.reshape(shape) for o in (*outs[:3], outs[3] if copy_g else g))


def _ada_fwd(c_all, w, b):
    L, D, N = w.shape
    tn = 768 if N % 768 == 0 else _tile(N, 512)

    def body(c_ref, w_ref, b_ref, o_ref):
        cv = c_ref[...]
        cond = (cv * jax.nn.sigmoid(cv)).astype(BF)
        o_ref[...] = _dot(cond, w_ref[...].astype(BF)) + b_ref[...]

    return pl.pallas_call(
        body, name="ada_fwd", grid=(L, N // tn),
        in_specs=[pl.BlockSpec((8, D), lambda l, n: (0, 0)), pl.BlockSpec((None, D, tn), lambda l, n: (l, 0, n)),
                  pl.BlockSpec((None, 1, tn), lambda l, n: (l, 0, n))],
        out_specs=pl.BlockSpec((None, 8, tn), lambda l, n: (l, 0, n)),
        out_shape=jax.ShapeDtypeStruct((L, 8, N), F32),
        compiler_params=_cparams(2),
    )(c_all, w, b)


def _ada_wgrad(c_all_t, dm):
    D = c_all_t.shape[0]
    L, _, N = dm.shape
    tn = 256

    def body(c_ref, dm_ref, o_ref):
        cv = c_ref[...]
        cond = cv * jax.nn.sigmoid(cv)
        acc = cond[:, 0:1] * dm_ref[0:1, :]
        for b in range(1, 8):
            acc = acc + cond[:, b:b + 1] * dm_ref[b:b + 1, :]
        o_ref[...] = acc

    return pl.pallas_call(
        body, name="ada_wgrad", grid=(L, N // tn),
        in_specs=[pl.BlockSpec((D, 8), lambda l, n: (0, 0)), pl.BlockSpec((None, 8, tn), lambda l, n: (l, 0, n))],
        out_specs=pl.BlockSpec((None, D, tn), lambda l, n: (l, 0, n)),
        out_shape=jax.ShapeDtypeStruct((L, D, N), F32),
        compiler_params=_cparams(2),
    )(c_all_t, dm)


def _sum8(g):
    _, R, C = g.shape

    def body(g_ref, o_ref):
        acc = g_ref[0]
        for b in range(1, 8):
            acc = acc + g_ref[b]
        o_ref[...] = acc

    return pl.pallas_call(body, name="sum8", out_shape=jax.ShapeDtypeStruct((R, C), F32),
                          in_specs=[pl.BlockSpec(memory_space=pltpu.VMEM)],
                          out_specs=pl.BlockSpec(memory_space=pltpu.VMEM))(g)


def _pair_add(g4, recv, cidx):
    _, _, ha, B = g4.shape
    tr = _row_tile(ha)

    def body(c_ref, g_ref, r_ref, o_ref):
        o_ref[...] = (g_ref[...].astype(F32) + r_ref[...].astype(F32)).astype(BF)

    return pl.pallas_call(
        body, name="pair_add",
        grid_spec=pltpu.PrefetchScalarGridSpec(
            num_scalar_prefetch=1, grid=(4, ha // tr),
            in_specs=[pl.BlockSpec((None, None, tr, B), lambda s, i, c: (s, c[0], i, 0)),
                      pl.BlockSpec((None, tr, B), lambda s, i, c: (s, i, 0))],
            out_specs=pl.BlockSpec((None, tr, B), lambda s, i, c: (s, i, 0))),
        out_shape=jax.ShapeDtypeStruct((4, ha, B), BF),
        compiler_params=_cparams(2),
    )(cidx, g4, recv)


def _sum_partials(part, recv, gbuf, n_layers, l, place):
    _, ha, B = part.shape
    tr = _row_tile(ha)

    def body(pc_ref, p_ref, r_ref, *rest):
        o_ref = rest[-1]
        o_ref[...] = ((p_ref[...].astype(F32) + r_ref[0].astype(F32)) + r_ref[1].astype(F32)) + r_ref[2].astype(F32)

    in_specs = [pl.BlockSpec((None, tr, B), lambda i, pc: (pc[0], i, 0)), pl.BlockSpec((3, tr, B), lambda i, pc: (0, i, 0))]
    ops = [part, recv]
    if gbuf is not None:
        in_specs.append(ANY)
        ops.append(gbuf)
    return pl.pallas_call(
        body, name="sum_partials",
        grid_spec=pltpu.PrefetchScalarGridSpec(
            num_scalar_prefetch=1, grid=(ha // tr,), in_specs=in_specs,
            out_specs=pl.BlockSpec((None, None, tr, B), lambda i, pc: (l, pc[1], i, 0))),
        out_shape=jax.ShapeDtypeStruct((n_layers, 2, ha, B), F32),
        input_output_aliases={} if gbuf is None else {3: 0},
        compiler_params=_cparams(1),
    )(place, *ops)


def _cast_shard(shard, l, place):
    _, _, ha, B = shard.shape
    tr = _row_tile(ha)

    def body(pc_ref, s_ref, o_ref):
        o_ref[...] = s_ref[...].astype(BF)

    return pl.pallas_call(
        body, name="cast_shard",
        grid_spec=pltpu.PrefetchScalarGridSpec(
            num_scalar_prefetch=1, grid=(2, ha // tr),
            in_specs=[pl.BlockSpec((None, None, tr, B), lambda h, i, pc: (l, h, i, 0))],
            out_specs=pl.BlockSpec((None, None, tr, B), lambda h, i, pc: (pc[0], h, i, 0))),
        out_shape=jax.ShapeDtypeStruct((4, 2, ha, B), BF),
        compiler_params=_cparams(2),
    )(place, shard)


def _gather8(v):
    R, C = v.shape

    def body(v_ref, o_ref, ssem, rsem):
        x, y, c, *_ = _place()
        me = 4 * x + 2 * y + c
        o_ref[me] = v_ref[...]
        cps = []
        for k in range(1, 8):
            fx, fy, fc = (k >> 2) & 1, (k >> 1) & 1, k & 1
            peer = (x ^ fx, y ^ fy, c ^ fc)
            cp = pltpu.make_async_remote_copy(src_ref=v_ref, dst_ref=o_ref.at[me], send_sem=ssem.at[k - 1],
                                              recv_sem=rsem.at[k - 1], device_id=peer, device_id_type=MESH)
            cp.start()
            cps.append((cp, 4 * peer[0] + 2 * peer[1] + peer[2]))
        for k, (cp, slot) in enumerate(cps):
            there = o_ref.at[slot]
            pltpu.make_async_remote_copy(src_ref=there, dst_ref=there, send_sem=ssem.at[k], recv_sem=rsem.at[k],
                                         device_id=(x, y, c), device_id_type=MESH).wait_recv()
        for cp, _ in cps:
            cp.wait_send()

    vm = pl.BlockSpec(memory_space=pltpu.VMEM)
    return pl.pallas_call(
        body, name="gather8", in_specs=[vm], out_specs=vm, out_shape=jax.ShapeDtypeStruct((8, R, C), F32),
        scratch_shapes=[pltpu.SemaphoreType.DMA((7,)), pltpu.SemaphoreType.DMA((7,))],
    )(v)


def _mods_to_owner(ms):
    _, R, C = ms.shape

    def body(m_ref, o_ref, ssem, rsem):
        x, y, c, p, chips, qs = _place()
        o_ref[p] = m_ref[2 * p + c]
        cps = []
        for j in range(3):
            cp = pltpu.make_async_remote_copy(src_ref=m_ref.at[2 * qs[j] + c], dst_ref=o_ref.at[p], send_sem=ssem.at[j],
                                              recv_sem=rsem.at[j], device_id=(*chips[j], c), device_id_type=MESH)
            cp.start()
            cps.append(cp)
        for j in range(3):
            there = o_ref.at[qs[j]]
            pltpu.make_async_remote_copy(src_ref=there, dst_ref=there, send_sem=ssem.at[j], recv_sem=rsem.at[j],
                                         device_id=(x, y, c), device_id_type=MESH).wait_recv()
        for cp in cps:
            cp.wait_send()

    vm = pl.BlockSpec(memory_space=pltpu.VMEM)
    return pl.pallas_call(
        body, name="mods_to_owner", in_specs=[vm], out_specs=vm, out_shape=jax.ShapeDtypeStruct((4, R, C), F32),
        scratch_shapes=[pltpu.SemaphoreType.DMA((3,)), pltpu.SemaphoreType.DMA((3,))],
    )(ms)


def _vec(*rows):
    D = rows[0].shape[-1]
    rows = [r.reshape(1, D) for r in rows]
    return jnp.concatenate(rows + [jnp.zeros((8 - len(rows), D), F32)], axis=0)


def _sets(n_layers, n_conv):
    sets = []
    for l in range(n_layers):
        j = l - n_conv
        mixer = [("conv_w_in", l), ("conv_w_out", l)] if l < n_conv else [("attn_w_q", j), ("attn_w_o", j)]
        sets.append([("ffn1_w_in", l), ("ffn1_w_out", l)] + ([("w_kv", 0)] if l == n_conv else []))
        sets.append(mixer + [("ffn2_w_in", l), ("ffn2_w_out", l)])
    return sets


def _as_weight(name, buf):
    _, _, ha, B = buf.shape
    return buf.reshape(8 * ha, B) if name.endswith("w_out") else buf.reshape(4, 2 * ha, B)


def _local_step(x, tgt, tabs, norm_g, conv_w, kv_norm_g, final_norm_g, mods, kvmods, cast, place):
    S, D = x.shape
    n_layers, n_conv = norm_g.shape[0], conv_w.shape[0]
    md = mods.reshape(n_layers, 9, D)

    def vec_of(l, k):
        return _vec(norm_g[l, k], md[l, 3 * k + 1], md[l, 3 * k], md[l, 3 * k + 2])

    W = {}

    def install(items, bufs):
        for (name, idx), b in zip(items, bufs):
            W[name, idx] = _as_weight(name, b)

    Q = _sets(n_layers, n_conv)
    assert n_conv >= 1, "the first layer's conv mixer carries a weight set"
    install(Q[0], _comm_call("gather_d2d", _comm_call("gather_ici", [cast[it] for it in Q[0]])))
    arrived = {}

    def ffn_fwd(n, *args):
        ici = [k for k in ([1] if n == 0 else [n + 2]) if k < len(Q)]
        d2d = [n + 1] if 2 <= n + 1 < len(Q) else []
        comm = [("gather_ici", [cast[it] for it in Q[k]]) for k in ici] + [("gather_d2d", arrived.pop(k)) for k in d2d]
        xo, h, ab, y, *moved = _ffn_fwd(*args, comm)
        for k in ici:
            arrived[k], moved = moved[:len(Q[k])], moved[len(Q[k]):]
        for k in d2d:
            install(Q[k], moved[:len(Q[k])])
            moved = moved[len(Q[k]):]
        if n == 0:
            install(Q[1], _comm_call("gather_d2d", arrived.pop(1)))
        return xo, h, ab, y

    vec_kv = _vec(kv_norm_g, kvmods[D:], kvmods[:D])
    ng = len(DILATIONS)
    n_rope = ng * GROUP_W // LANES
    saved = {}
    xc = x
    ks = vs = None
    for l in range(n_layers):
        if l == n_conv:
            h_kv, kv = _proj_fwd(xc, vec_kv, W["w_kv", 0], tabs, n_rope, DILATIONS * 2)
            ks, vs = kv[:ng], kv[ng:]
            saved["kv"] = (xc, h_kv)
        v0 = vec_of(l, 0)
        xo, h, ab, y = ffn_fwd(2 * l, xc, v0, W["ffn1_w_in", l], W["ffn1_w_out", l])
        saved[l, 0] = (xc, v0, h, ab, y)
        xc = xo
        v1 = vec_of(l, 1)
        if l < n_conv:
            cw = _vec(conv_w[l, 0], conv_w[l, 1], conv_w[l, 2])
            early = [("gather_ici", [cast[it] for it in Q[2]])] if l == 0 and len(Q) > 2 else None
            xo, h, s4, z, y, *moved = _conv_fwd(xc, v1, cw, W["conv_w_in", l], W["conv_w_out", l], early)
            if early:
                arrived[2] = moved
            saved[l, 1] = (xc, v1, h, y, cw, s4, z)
        else:
            j = l - n_conv
            h, qs = _proj_fwd(xc, v1, W["attn_w_q", j], tabs, n_rope, DILATIONS, SM_SCALE)
            og = [_attn_fwd(qs[g], ks[g], vs[g], DILATIONS[g]) for g in range(ng)]
            mixs, ljs, y, xo = _attn_out([o for o, _ in og], [s for _, s in og], xc, v1, W["attn_w_o", j])
            saved[l, 1] = (xc, v1, h, y, qs, mixs, ljs)
        xc = xo
        v2 = vec_of(l, 2)
        xo, h, ab, y = ffn_fwd(2 * l + 1, xc, v2, W["ffn2_w_in", l], W["ffn2_w_out", l])
        saved[l, 2] = (xc, v2, h, ab, y)
        xc = xo

    dx, red_f = _final_loss(xc, _vec(final_norm_g), tgt)
    loss_part = 0.5 * jnp.sum(red_f[1]) / D
    G, parts, others = {}, {}, {}
    dmods = [[None] * 9 for _ in range(n_layers)]
    dnorm = [[None] * 3 for _ in range(n_layers)]
    dconvw = [None] * n_conv
    dkv_parts = []

    def halves(items):
        return [G[it].reshape(4, 2, G[it].shape[1] // 2, G[it].shape[2]) for it in items]

    def pair_sums(items, recv):
        for it, g, r in zip(items, halves(items), recv):
            parts[it] = _pair_add(g, r, place[1:])
        return [parts[it] for it in items]

    R = Q[::-1]
    recvd = {}

    def ffn_bwd(dx, l, k, name):
        m = 2 * (n_layers - 1 - l) + (0 if k == 2 else 1)
        ex = [m - 1] if m >= 1 else []
        sc = [m - 2] if m >= 2 else []
        comm = [("exchange", halves(R[e])) for e in ex] + [("scatter", pair_sums(R[e], recvd.pop(e))) for e in sc]
        xin, v, h, ab, y = saved[l, k]
        w_in, w_out = W[name + "_w_in", l], W[name + "_w_out", l]
        dy, u, dab, dh, *moved = _ffn_dgrad(dx, v, ab, w_in, w_out, comm)
        for e in ex:
            recvd[e], moved = moved[:len(R[e])], moved[len(R[e]):]
        for e in sc:
            others.update(zip(R[e], moved[:len(R[e])]))
            moved = moved[len(R[e]):]
        if m == len(R) - 1:
            G[name + "_w_in", l], got = _mm_tn(h, dab, w_in.shape[2], [("scatter", pair_sums(R[m - 1], recvd.pop(m - 1)))])
            others.update(zip(R[m - 1], got))
        else:
            G[name + "_w_in", l] = _mm_tn(h, dab, w_in.shape[2])
        G[name + "_w_out", l] = _mm_tn(u, dy[None], D).reshape(4, -1, D)
        return _norm_bwd(dx, [dh], xin, y, v, 0.5)

    def note(l, k, red):
        dnorm[l][k] = red[0]
        dmods[l][3 * k], dmods[l][3 * k + 1], dmods[l][3 * k + 2] = red[2], red[1], red[3]

    for l in reversed(range(n_layers)):
        dx, red = ffn_bwd(dx, l, 2, "ffn2")
        note(l, 2, red)
        if l < n_conv:
            xin, v, h, y, cw, s4, z = saved[l, 1]
            w_in, w_out = W["conv_w_in", l], W["conv_w_out", l]
            dy, d3, dh, dcw = _conv_dgrad(dx, v, cw, s4, w_in, w_out)
            G["conv_w_in", l] = _mm_tn(h, d3[None], w_in.shape[2])
            G["conv_w_out", l] = _mm_tn(z, dy[None], D).reshape(4, -1, D)
            dconvw[l] = dcw[0:3]
            dx, red = _norm_bwd(dx, [dh], xin, y, v, 1.0)
        else:
            j = l - n_conv
            xin, v, h, y, qs, mixs, ljs = saved[l, 1]
            wq, wo = W["attn_w_q", j], W["attn_w_o", j]
            dy, dmixs = _attn_out_dgrad(dx, v, wo)
            G["attn_w_o", j] = _mm_tn(mixs[0], dy[None], wo.shape[2])
            dqkv = [_attn_bwd(qs[g], ks[g], vs[g], dmixs[g], mixs[g], ljs[g], DILATIONS[g]) for g in range(ng)]
            dkv_parts.append([t[1:] for t in dqkv])
            dq, dx, red = _proj_dgrad([[t[0]] for t in dqkv], tabs, wq, ng, DILATIONS, dx, xin, y, v)
            G["attn_w_q", j] = _mm_tn(h, dq[None], wq.shape[2])
        note(l, 1, red)
        dx, red = ffn_bwd(dx, l, 0, "ffn1")
        note(l, 0, red)
        if l == n_conv:
            xin, h_kv = saved["kv"]
            kv_parts = [[lay[g][0] for lay in dkv_parts] for g in range(ng)] + [[lay[g][1] for lay in dkv_parts] for g in range(ng)]
            dkv, dx, red_kv = _proj_dgrad(kv_parts, tabs, W["w_kv", 0], ng, DILATIONS * 2, dx, xin, None, vec_kv)
            G["w_kv", 0] = _mm_tn(h_kv, dkv[None], W["w_kv", 0].shape[2])
    others.update(zip(R[-1], _comm_call("scatter", pair_sums(R[-1], _comm_call("exchange", halves(R[-1]))))))
    small = jnp.concatenate(
        [jnp.stack([jnp.stack(r) for r in dmods]).reshape(-1), red_kv[2], red_kv[1],
         jnp.stack([jnp.stack(r) for r in dnorm]).reshape(-1), jnp.stack(dconvw).reshape(-1), red_kv[0], red_f[0]])
    return loss_part, dx, parts, others, small


BIG = ("ffn1_w_in", "ffn1_w_out", "ffn2_w_in", "ffn2_w_out", "conv_w_in", "conv_w_out", "w_kv", "attn_w_q", "attn_w_o")


def _halved(t):
    if t.ndim == 2:
        t = t[None]
    L, A, B = t.shape
    return t.reshape(L, 2, A // 2, B)


def kernel(x, c, positions, norm_g, ada_w, ada_b, ffn1_w_in, ffn1_w_out, ffn2_w_in, ffn2_w_out, conv_w_in, conv_w, conv_w_out, kv_norm_g, kv_ada_w, kv_ada_b, w_kv, attn_w_q, attn_w_o, final_norm_g, loss_target, m_norm_g, m_ada_w, m_ada_b, m_ffn1_w_in, m_ffn1_w_out, m_ffn2_w_in, m_ffn2_w_out, m_conv_w_in, m_conv_w, m_conv_w_out, m_kv_norm_g, m_kv_ada_w, m_kv_ada_b, m_w_kv, m_attn_w_q, m_attn_w_o, m_final_norm_g, v_norm_g, v_ada_w, v_ada_b, v_ffn1_w_in, v_ffn1_w_out, v_ffn2_w_in, v_ffn2_w_out, v_conv_w_in, v_conv_w, v_conv_w_out, v_kv_norm_g, v_kv_ada_w, v_kv_ada_b, v_w_kv, v_attn_w_q, v_attn_w_o, v_final_norm_g):
    wts = dict(norm_g=norm_g, ada_w=ada_w, ada_b=ada_b, ffn1_w_in=ffn1_w_in, ffn1_w_out=ffn1_w_out, ffn2_w_in=ffn2_w_in,
               ffn2_w_out=ffn2_w_out, conv_w_in=conv_w_in, conv_w=conv_w, conv_w_out=conv_w_out, kv_norm_g=kv_norm_g,
               kv_ada_w=kv_ada_w, kv_ada_b=kv_ada_b, w_kv=w_kv, attn_w_q=attn_w_q, attn_w_o=attn_w_o,
               final_norm_g=final_norm_g)
    ms = dict(norm_g=m_norm_g, ada_w=m_ada_w, ada_b=m_ada_b, ffn1_w_in=m_ffn1_w_in, ffn1_w_out=m_ffn1_w_out,
              ffn2_w_in=m_ffn2_w_in, ffn2_w_out=m_ffn2_w_out, conv_w_in=m_conv_w_in, conv_w=m_conv_w,
              conv_w_out=m_conv_w_out, kv_norm_g=m_kv_norm_g, kv_ada_w=m_kv_ada_w, kv_ada_b=m_kv_ada_b, w_kv=m_w_kv,
              attn_w_q=m_attn_w_q, attn_w_o=m_attn_w_o, final_norm_g=m_final_norm_g)
    vs = dict(norm_g=v_norm_g, ada_w=v_ada_w, ada_b=v_ada_b, ffn1_w_in=v_ffn1_w_in, ffn1_w_out=v_ffn1_w_out,
              ffn2_w_in=v_ffn2_w_in, ffn2_w_out=v_ffn2_w_out, conv_w_in=v_conv_w_in, conv_w=v_conv_w,
              conv_w_out=v_conv_w_out, kv_norm_g=v_kv_norm_g, kv_ada_w=v_kv_ada_w, kv_ada_b=v_kv_ada_b, w_kv=v_w_kv,
              attn_w_q=v_attn_w_q, attn_w_o=v_attn_w_o, final_norm_g=v_final_norm_g)
    order = list(wts)
    S, D = x.shape[1], x.shape[2]
    n_layers, n_conv = norm_g.shape[0], conv_w.shape[0]
    nm, sw = ada_w.shape[2], norm_g.shape[2]
    ix, iy, ic = (lax.axis_index(a) for a in AXES)
    chip = 2 * ix + iy

    pack = jnp.concatenate([norm_g.reshape(-1), conv_w.reshape(-1), c.reshape(-1)])
    npad = (-pack.size) % (8 * LANES)
    allp = _gather8(jnp.pad(pack, (0, npad)).reshape(-1, LANES)).reshape(8, -1)
    n1, n2 = norm_g.size, norm_g.size + conv_w.size
    by_chip = allp[0::2]
    norm_full = jnp.moveaxis(by_chip[:, :n1].reshape(4, n_layers, 3, sw), 0, 2).reshape(n_layers, 3, 4 * sw)
    conv_full = jnp.moveaxis(by_chip[:, n1:n2].reshape(4, n_conv, 3, sw), 0, 2).reshape(n_conv, 3, 4 * sw)
    c_all = allp[:, n2:n2 + D]

    b_l = lax.dynamic_slice_in_dim(ada_b, chip * nm, nm, axis=1)
    mod_sh = _ada_fwd(c_all, ada_w, b_l[:, None, :])
    nkv = kv_ada_w.shape[1]
    bkv = lax.dynamic_slice_in_dim(kv_ada_b, chip * nkv, nkv, axis=0)
    kv_sh = _ada_fwd(c_all, kv_ada_w[None], bkv[None, None, :])
    rows = jnp.concatenate([jnp.moveaxis(mod_sh, 0, 1), jnp.pad(jnp.moveaxis(kv_sh, 0, 1), ((0, 0), (0, 0), (0, nm - nkv)))], axis=1)
    rpad = (-rows.shape[1]) % 8
    mine = _mods_to_owner(jnp.pad(rows, ((0, 0), (0, rpad), (0, 0))))
    mods = jnp.moveaxis(mine[:, :n_layers], 0, 1).reshape(n_layers, 4 * nm)
    kvmods = mine[:, n_layers, :nkv].reshape(4 * nkv)

    place = jnp.stack([chip, ic]).astype(jnp.int32)
    halved = {k: _halved(wts[k]) for k in BIG}
    cast = {(k, l): _cast_shard(halved[k], l, place) for k in BIG for l in range(halved[k].shape[0])}

    loss_part, dx, parts, others, small = _local_step(x[0], loss_target[0], _rope_tables(positions), norm_full, conv_full,
                                                      kv_norm_g, final_norm_g, mods, kvmods, cast, place)
    loss = lax.psum(loss_part, AXES)

    gbufs = []
    for k in BIG:
        gbuf, n_l = None, halved[k].shape[0]
        for l in range(n_l):
            gbuf = _sum_partials(parts[k, l], others[k, l], gbuf, n_l, l, place)
        gbufs.append(gbuf)
    grads = {}

    spad = (-small.size) % (8 * LANES)
    gath = _gather8(jnp.pad(small, (0, spad)).reshape(-1, LANES))
    tot = _sum8(gath).reshape(-1)
    gath = gath.reshape(8, -1)
    o = 0

    def take(n):
        nonlocal o
        o += n
        return tot[o - n:o]

    g_mods = take(n_layers * 9 * D).reshape(n_layers, 9 * D)
    g_kvmods = take(2 * D)
    g_norm = take(n_layers * 3 * D).reshape(n_layers, 3, D)
    g_convw = take(n_conv * 3 * D).reshape(n_conv, 3, D)
    g_kvn = take(D)
    g_fin = take(D)
    grads["ada_b"] = g_mods
    grads["kv_ada_b"] = g_kvmods
    grads["norm_g"] = lax.dynamic_slice_in_dim(g_norm, chip * sw, sw, axis=2)
    grads["conv_w"] = lax.dynamic_slice_in_dim(g_convw, chip * sw, sw, axis=2)
    grads["kv_norm_g"] = g_kvn
    grads["final_norm_g"] = g_fin

    dm_all = gath[:, :n_layers * 9 * D].reshape(8, n_layers, 9 * D)
    dm_mine = jnp.moveaxis(lax.dynamic_slice_in_dim(dm_all, chip * nm, nm, axis=2), 0, 1)
    dkv_all = gath[:, n_layers * 9 * D:n_layers * 9 * D + 2 * D]
    dkv_mine = lax.dynamic_slice_in_dim(dkv_all, chip * nkv, nkv, axis=1)[None]
    c_t = c_all.T
    grads["ada_w"] = _ada_wgrad(c_t, dm_mine)
    grads["kv_ada_w"] = _ada_wgrad(c_t, dkv_mine)[0]

    grads.update({k: s.reshape(wts[k].shape) for k, s in zip(BIG, _comm_call("share", gbufs))})
    deltas, new_m, new_v = {}, {}, {}
    for k in order:
        shp = wts[k].shape
        two_d = (lambda t: t.reshape(1, -1)) if len(shp) == 1 else (lambda t: t)
        dlt, nmk, nvk, gk = _adamw(two_d(wts[k]), two_d(grads[k]), two_d(ms[k]), two_d(vs[k]), k in BIG)
        deltas[k], new_m[k], new_v[k] = dlt.reshape(shp), nmk.reshape(shp), nvk.reshape(shp)
        grads[k] = gk.reshape(shp)
    return (loss, dx[None], *[grads[k] for k in order], *[deltas[k] for k in order], *[new_m[k] for k in order],
            *[new_v[k] for k in order])
```

```python
import functools
import math

import jax
import jax.numpy as jnp
from jax import lax
from jax.experimental import pallas as pl
from jax.experimental.pallas import tpu as pltpu

F32 = jnp.float32
BF = jnp.bfloat16
MESH = pl.DeviceIdType.MESH
AXES = ("x", "y", "c")

NORM_EPS = 1e-5
HEAD_DIM = 64
HEADS_PER_GROUP = 8
GROUP_W = HEADS_PER_GROUP * HEAD_DIM
DILATIONS = (1, 4, 16)
BAND = 128
ROPE_DIM = 16
ROPE_THETA = 500000.0
SM_SCALE = HEAD_DIM ** -0.5
NEG = -1e30
ADAM_LR, ADAM_B1, ADAM_B2, ADAM_EPS, ADAM_WD, ADAM_STEP = 0.001, 0.9, 0.999, 1e-08, 0.01, 10

V7X_VMEM_BYTES = 64 * 1024 * 1024
VMEM_LIMIT = V7X_VMEM_BYTES - 6 * 1024 * 1024
LANES = 128
TM = 512
TM_DGRAD = 256
TM_STREAM = 1024
TK = 2048
WGRAD_COLS = 1536


def _cparams(ngrid):
    return pltpu.CompilerParams(dimension_semantics=("arbitrary",) * ngrid, vmem_limit_bytes=VMEM_LIMIT)


def _dot(a, b):
    return jnp.dot(a, b, preferred_element_type=F32)


def _dot_nt(a, b):
    return lax.dot_general(a, b, (((1,), (1,)), ((), ())), preferred_element_type=F32)


def _dot_tn(a, b):
    return lax.dot_general(a, b, (((0,), (0,)), ((), ())), preferred_element_type=F32)


def _tile(n, pref):
    t = min(n, pref)
    while n % t:
        t //= 2
    return t


def _row_tile(n, cap=512, mult=16):
    best = n
    for t in range(mult, min(n, cap) + 1, mult):
        if n % t == 0:
            best = t
    return best


def _rstd(x):
    return lax.rsqrt(jnp.mean(x * x, axis=-1, keepdims=True) + NORM_EPS)


def _norm_mod(x, v):
    return (x * _rstd(x) * v[0:1]) * (1.0 + v[1:2]) + v[2:3]


def _silu_parts(a):
    sg = jax.nn.sigmoid(a)
    return sg, a * sg


def _row(i):
    return lambda *_: (i, 0)


ANY = pl.BlockSpec(memory_space=pl.ANY)
IN_PLACE = ("gather_ici", "gather_d2d", "share")


def _place():
    x, y, c = (lax.axis_index(a) for a in AXES)
    chips = [(1 - x, y), (x, 1 - y), (1 - x, 1 - y)]
    return x, y, c, 2 * x + y, chips, [2 * cx + cy for cx, cy in chips]


def _transfers(kind, ins, outs):
    x, y, c, p, chips, qs = _place()
    sib = (x, y, 1 - c)
    rows = []
    for k, o in enumerate(outs):
        if kind == "gather_ici":
            rows.append([(o.at[p, c], o.at[p, c], (*chips[j], c), o.at[qs[j], c]) for j in range(3)])
        elif kind == "gather_d2d":
            rows.append([(o.at[qs[j], c], o.at[qs[j], c], sib, o.at[qs[j], 1 - c]) for j in range(3)])
        elif kind == "exchange":
            rows.append([(ins[k].at[:, 1 - c], o, sib, o)])
        elif kind == "scatter":
            rows.append([(ins[k].at[qs[j]], o.at[j], (*chips[j], c), o.at[j]) for j in range(3)])
        elif kind == "share":
            rows.append([(o.at[:, c], o.at[:, c], sib, o.at[:, 1 - c])])
    return rows


def _comm_out_shapes(kind, arrays):
    if kind in IN_PLACE:
        return [jax.ShapeDtypeStruct(a.shape, a.dtype) for a in arrays]
    if kind == "exchange":
        return [jax.ShapeDtypeStruct((4,) + a.shape[2:], a.dtype) for a in arrays]
    return [jax.ShapeDtypeStruct((3,) + a.shape[1:], a.dtype) for a in arrays]


def _comm_sems(n):
    return [pltpu.SemaphoreType.DMA((n, 3)), pltpu.SemaphoreType.DMA((n, 3))]


def _comm_start(rows, ssem, rsem):
    for k, row in enumerate(rows):
        for j, (src, dst, dev, _) in enumerate(row):
            pltpu.make_async_remote_copy(src_ref=src, dst_ref=dst, send_sem=ssem.at[k, j], recv_sem=rsem.at[k, j],
                                         device_id=dev, device_id_type=MESH).start()


def _comm_wait(rows, ssem, rsem):
    for k, row in enumerate(rows):
        for j, (src, dst, dev, land) in enumerate(row):
            pltpu.make_async_remote_copy(src_ref=src, dst_ref=dst, send_sem=ssem.at[k, j], recv_sem=rsem.at[k, j],
                                         device_id=dev, device_id_type=MESH).wait_send()
            pltpu.make_async_remote_copy(src_ref=land, dst_ref=land, send_sem=ssem.at[k, j], recv_sem=rsem.at[k, j],
                                         device_id=dev, device_id_type=MESH).wait_recv()


def _comm_call(kind, arrays):
    n = len(arrays)

    def body(*refs):
        rows = _transfers(kind, refs[:n], refs[n:2 * n])
        _comm_start(rows, *refs[2 * n:])
        _comm_wait(rows, *refs[2 * n:])

    return pl.pallas_call(
        body, name="comm_" + kind, in_specs=[ANY] * n, out_specs=[ANY] * n, out_shape=_comm_out_shapes(kind, arrays),
        input_output_aliases={k: k for k in range(n)} if kind in IN_PLACE else {},
        scratch_shapes=_comm_sems(n),
    )(*arrays)


def _ride(comm, n_in, n_out):
    ride = dict(arrays=[], in_specs=[], out_specs=[], out_shape=[], aliases={}, scratch=[], n=0)
    for kind, arrays in comm or []:
        if kind in IN_PLACE:
            ride["aliases"].update({n_in + ride["n"] + k: n_out + ride["n"] + k for k in range(len(arrays))})
        ride["arrays"] += list(arrays)
        ride["out_shape"] += _comm_out_shapes(kind, arrays)
        ride["scratch"] += _comm_sems(len(arrays))
        ride["n"] += len(arrays)
    ride["in_specs"] = ride["out_specs"] = [ANY] * ride["n"]
    return ride


def _ride_when(comm, ins, outs, sems, cond, action):
    if not comm:
        return

    @pl.when(cond)
    def _():
        at = 0
        for e, (kind, arrays) in enumerate(comm):
            n = len(arrays)
            action(_transfers(kind, ins[at:at + n], outs[at:at + n]), *sems[2 * e:2 * e + 2])
            at += n


def _ffn_fwd(x, vec, w_in4, w_out, comm=None):
    S, D = x.shape
    cs = w_in4.shape[2]
    F = 2 * cs
    tm = _tile(S, TM)
    ni = S // tm
    ride = _ride(comm, 5, 4)
    nc = ride["n"]

    def body(*refs):
        x_ref, vec_ref, wa_ref, wb_ref, wo_ref = refs[:5]
        xo_ref, h_ref, ab_ref, y_ref = refs[5 + nc:9 + nc]
        i = pl.program_id(0)
        riding = (comm, refs[5:5 + nc], refs[9 + nc:9 + 2 * nc], refs[9 + 2 * nc:])
        _ride_when(*riding, i == 0, _comm_start)
        xv = x_ref[...]
        h = _norm_mod(xv, vec_ref[...]).astype(BF)
        h_ref[...] = h
        y = None
        for c in range(2):
            cols = slice(c * cs, (c + 1) * cs)
            a = _dot(h, wa_ref[c])
            b = _dot(h, wb_ref[c])
            ab_ref[0, :, cols] = a.astype(BF)
            ab_ref[1, :, cols] = b.astype(BF)
            _, s = _silu_parts(a)
            part = _dot((s * b).astype(BF), wo_ref[cols, :])
            y = part if y is None else y + part
        y_ref[...] = y.astype(BF)
        xo_ref[...] = xv + (0.5 * (1.0 + vec_ref[3:4, :])) * y
        _ride_when(*riding, i == ni - 1, _comm_wait)

    tok = pl.BlockSpec((tm, D), lambda i: (i, 0))
    once = pl.Buffered(1)
    return pl.pallas_call(
        body, name="ffn_fwd", grid=(ni,),
        in_specs=[tok, pl.BlockSpec((8, D), lambda i: (0, 0)),
                  pl.BlockSpec((2, D, cs), lambda i: (0, 0, 0), pipeline_mode=once),
                  pl.BlockSpec((2, D, cs), lambda i: (1, 0, 0), pipeline_mode=once),
                  pl.BlockSpec((F, D), lambda i: (0, 0), pipeline_mode=once)] + ride["in_specs"],
        out_specs=[tok, tok, pl.BlockSpec((2, tm, F), lambda i: (0, i, 0)), tok] + ride["out_specs"],
        out_shape=[jax.ShapeDtypeStruct((S, D), F32), jax.ShapeDtypeStruct((S, D), BF),
                   jax.ShapeDtypeStruct((2, S, F), BF), jax.ShapeDtypeStruct((S, D), BF)] + ride["out_shape"],
        input_output_aliases=ride["aliases"], scratch_shapes=ride["scratch"],
        compiler_params=_cparams(1),
    )(x, vec, w_in4, w_in4, w_out, *ride["arrays"])


def _norm_bwd_tile(dxo, dh, xv, y, vec_ref, red_ref, coef):
    g = vec_ref[0:1, :]
    sc = vec_ref[1:2, :]
    r = _rstd(xv)
    xh = xv * r
    dhn = dh * (1.0 + sc)
    dxh = dhn * g
    red_ref[0:1, :] += jnp.sum(dhn * xh, axis=0, keepdims=True)
    red_ref[1:2, :] += jnp.sum(dh * (xh * g), axis=0, keepdims=True)
    red_ref[2:3, :] += jnp.sum(dh, axis=0, keepdims=True)
    if y is not None:
        red_ref[3:4, :] += coef * jnp.sum(dxo * y.astype(F32), axis=0, keepdims=True)
    return dxo + r * (dxh - xh * jnp.mean(dxh * xh, axis=-1, keepdims=True))


def _ffn_dgrad(dxo, vec, ab, w_in4, w_out, comm=None):
    S, D = dxo.shape
    cs = w_in4.shape[2]
    F = 2 * cs
    tm = _tile(S, TM_DGRAD)
    ni = S // tm
    ride = _ride(comm, 6, 4)
    nc = ride["n"]

    def body(*refs):
        dxo_ref, vec_ref, ab_ref, wa_ref, wb_ref, wo_ref = refs[:6]
        dy_ref, u_ref, dab_ref, dh_ref = refs[6 + nc:10 + nc]
        i = pl.program_id(0)
        riding = (comm, refs[6:6 + nc], refs[10 + nc:10 + 2 * nc], refs[10 + 2 * nc:])
        _ride_when(*riding, i == 0, _comm_start)
        dy = ((0.5 * (1.0 + vec_ref[3:4, :])) * dxo_ref[...]).astype(BF)
        dy_ref[...] = dy
        dh = None
        for c in range(2):
            cols = slice(c * cs, (c + 1) * cs)
            du = _dot_nt(dy, wo_ref[cols, :])
            a = ab_ref[0, :, cols].astype(F32)
            b = ab_ref[1, :, cols].astype(F32)
            sg, s = _silu_parts(a)
            u_ref[:, cols] = (s * b).astype(BF)
            da = (du * b * (sg * (1.0 + a * (1.0 - sg)))).astype(BF)
            db = (du * s).astype(BF)
            dab_ref[0, :, cols] = da
            dab_ref[1, :, cols] = db
            part = _dot_nt(da, wa_ref[c]) + _dot_nt(db, wb_ref[c])
            dh = part if dh is None else dh + part
        dh_ref[...] = dh.astype(BF)
        _ride_when(*riding, i == ni - 1, _comm_wait)

    tok = pl.BlockSpec((tm, D), lambda i: (i, 0))
    once = pl.Buffered(1)
    return pl.pallas_call(
        body, name="ffn_dgrad", grid=(ni,),
        in_specs=[tok, pl.BlockSpec((8, D), lambda i: (0, 0)),
                  pl.BlockSpec((2, tm, F), lambda i: (0, i, 0)),
                  pl.BlockSpec((2, D, cs), lambda i: (0, 0, 0), pipeline_mode=once),
                  pl.BlockSpec((2, D, cs), lambda i: (1, 0, 0), pipeline_mode=once),
                  pl.BlockSpec((F, D), lambda i: (0, 0), pipeline_mode=once)] + ride["in_specs"],
        out_specs=[tok, pl.BlockSpec((tm, F), lambda i: (i, 0)),
                   pl.BlockSpec((2, tm, F), lambda i: (0, i, 0)), tok] + ride["out_specs"],
        out_shape=[jax.ShapeDtypeStruct((S, D), BF), jax.ShapeDtypeStruct((S, F), BF),
                   jax.ShapeDtypeStruct((2, S, F), BF), jax.ShapeDtypeStruct((S, D), BF)] + ride["out_shape"],
        input_output_aliases=ride["aliases"], scratch_shapes=ride["scratch"],
        compiler_params=_cparams(1),
    )(dxo, vec, ab, w_in4, w_in4, w_out, *ride["arrays"])


def _mm_tn(a, b3, cs, comm=None):
    K, M = a.shape
    G, _, Nb = b3.shape
    N = G * Nb
    if cs >= Nb or Nb % cs:
        ns, tn = 1, math.gcd(cs, Nb)
    else:
        ns = max(s for s in range(1, Nb // cs + 1) if (Nb // cs) % s == 0 and s * cs <= max(WGRAD_COLS, cs))
        tn = ns * cs
    tmo = M if M <= 1024 else M // 2
    tk = _tile(K, TK)
    gm, gn, gk = M // tmo, N // tn, K // tk
    ride = _ride(comm, 2, 1)
    nc = ride["n"]

    def body(*refs):
        a_ref, b_ref = refs[:2]
        o_ref, acc_ref = refs[2 + nc], refs[-1]
        at = [pl.program_id(t) for t in range(3)]
        riding = (comm, refs[2:2 + nc], refs[3 + nc:3 + 2 * nc], refs[3 + 2 * nc:-1])
        _ride_when(*riding, functools.reduce(jnp.logical_and, [t == 0 for t in at]), _comm_start)

        @pl.when(at[2] == 0)
        def _():
            acc_ref[...] = jnp.zeros_like(acc_ref)

        av = a_ref[...].astype(BF)
        if ns == 1:
            acc_ref[...] += _dot_tn(av, b_ref[...].astype(BF))
        else:
            for s in range(ns):
                acc_ref[s] += _dot_tn(av, b_ref[:, s * cs:(s + 1) * cs].astype(BF))

        @pl.when(at[2] == gk - 1)
        def _():
            o_ref[...] = acc_ref[...].astype(BF)

        _ride_when(*riding, functools.reduce(jnp.logical_and, [t == g - 1 for t, g in zip(at, (gm, gn, gk))]), _comm_wait)

    if ns == 1:
        out_spec = pl.BlockSpec((None, tmo, tn), lambda m, n, k: ((n * tn) // cs, m, ((n * tn) % cs) // tn))
        acc = pltpu.VMEM((tmo, tn), F32)
    else:
        out_spec = pl.BlockSpec((ns, tmo, cs), lambda m, n, k: (n, m, 0))
        acc = pltpu.VMEM((ns, tmo, cs), F32)
    g, *moved = pl.pallas_call(
        body, name="wgrad_tn", grid=(gm, gn, gk),
        in_specs=[pl.BlockSpec((tk, tmo), lambda m, n, k: (k, m)),
                  pl.BlockSpec((None, tk, tn), lambda m, n, k: ((n * tn) // Nb, k, ((n * tn) % Nb) // tn))] + ride["in_specs"],
        out_specs=[out_spec] + ride["out_specs"],
        out_shape=[jax.ShapeDtypeStruct((N // cs, M, cs), BF)] + ride["out_shape"],
        input_output_aliases=ride["aliases"], scratch_shapes=ride["scratch"] + [acc],
        compiler_params=_cparams(3),
    )(a, b3, *ride["arrays"])
    return (g, moved) if comm else g


def _norm_bwd(dxo, dhs, x, y, vec, coef):
    S, D = x.shape
    tm = _tile(S, TM_STREAM)
    nh = len(dhs)
    has_y = y is not None

    def body(*refs):
        dxo_ref = refs[0]
        dh_refs = refs[1:1 + nh]
        x_ref = refs[1 + nh]
        y_ref = refs[2 + nh] if has_y else None
        vec_ref, dx_ref, red_ref = refs[-3:]

        @pl.when(pl.program_id(0) == 0)
        def _():
            red_ref[...] = jnp.zeros_like(red_ref)

        dh = dh_refs[0][...].astype(F32)
        for r in dh_refs[1:]:
            dh = dh + r[...].astype(F32)
        dx_ref[...] = _norm_bwd_tile(dxo_ref[...], dh, x_ref[...], y_ref[...] if has_y else None, vec_ref, red_ref, coef)

    tok = pl.BlockSpec((tm, D), lambda i: (i, 0))
    small = pl.BlockSpec((8, D), lambda i: (0, 0))
    ops = [dxo, *dhs, x] + ([y] if has_y else []) + [vec]
    return pl.pallas_call(
        body, name="norm_bwd", grid=(S // tm,),
        in_specs=[tok] * (len(ops) - 1) + [small],
        out_specs=[tok, small],
        out_shape=[jax.ShapeDtypeStruct((S, D), F32), jax.ShapeDtypeStruct((8, D), F32)],
        compiler_params=_cparams(1),
    )(*ops)


def _conv_fwd(x, vec, cw, w_in4, w_out, comm=None):
    S, D = x.shape
    cs = w_in4.shape[2]
    tm = _tile(S, TM)
    nt = S // tm
    ride = _ride(comm, 5, 5)
    nc = ride["n"]

    def body(*refs):
        x_ref, vec_ref, cw_ref, wi_ref, wo_ref = refs[:5]
        xo_ref, h_ref, s4_ref, z_ref, y_ref = refs[5 + nc:10 + nc]
        vs_ref = refs[-1]
        riding = (comm, refs[5:5 + nc], refs[10 + nc:10 + 2 * nc], refs[10 + 2 * nc:-1])
        _ride_when(*riding, pl.program_id(0) == 0, _comm_start)

        @pl.when(pl.program_id(0) == 0)
        def _():
            vs_ref[0:8, :] = jnp.zeros((8, D), F32)

        xv = x_ref[...]
        h = _norm_mod(xv, vec_ref[...]).astype(BF)
        h_ref[...] = h
        bcu = jnp.concatenate([_dot(h, wi_ref[q]) for q in range(4)], axis=1)
        bg, cg, u = bcu[:, :D], bcu[:, D:2 * D], bcu[:, 2 * D:]
        v = cg * u
        vs_ref[8:8 + tm, :] = v
        conv = cw_ref[0:1, :] * vs_ref[pl.ds(6, tm), :] + cw_ref[1:2, :] * vs_ref[pl.ds(7, tm), :] + cw_ref[2:3, :] * v
        vs_ref[0:8, :] = vs_ref[tm:tm + 8, :]
        z = (bg * conv).astype(BF)
        s4_ref[0] = bg.astype(BF)
        s4_ref[1] = cg.astype(BF)
        s4_ref[2] = u.astype(BF)
        s4_ref[3] = conv.astype(BF)
        z_ref[...] = z
        y = _dot(z, wo_ref[...])
        y_ref[...] = y.astype(BF)
        xo_ref[...] = xv + (1.0 + vec_ref[3:4, :]) * y
        _ride_when(*riding, pl.program_id(0) == nt - 1, _comm_wait)

    tok = pl.BlockSpec((tm, D), lambda i: (i, 0))
    small = pl.BlockSpec((8, D), lambda i: (0, 0))
    return pl.pallas_call(
        body, name="conv_fwd", grid=(nt,),
        in_specs=[tok, small, small, pl.BlockSpec((4, D, cs), lambda i: (0, 0, 0)),
                  pl.BlockSpec((D, D), lambda i: (0, 0))] + ride["in_specs"],
        out_specs=[tok, tok, pl.BlockSpec((4, tm, D), lambda i: (0, i, 0)), tok, tok] + ride["out_specs"],
        out_shape=[jax.ShapeDtypeStruct((S, D), F32), jax.ShapeDtypeStruct((S, D), BF),
                   jax.ShapeDtypeStruct((4, S, D), BF), jax.ShapeDtypeStruct((S, D), BF),
                   jax.ShapeDtypeStruct((S, D), BF)] + ride["out_shape"],
        input_output_aliases=ride["aliases"], scratch_shapes=ride["scratch"] + [pltpu.VMEM((tm + 8, D), F32)],
        compiler_params=_cparams(1),
    )(x, vec, cw, w_in4, w_out, *ride["arrays"])


def _conv_dgrad(dxo, vec, cw, s4, w_in4, w_out, x, y):
    S, D = dxo.shape
    cs = w_in4.shape[2]
    tm = _tile(S, TM)
    nt = S // tm

    def body(dxo_ref, vec_ref, cw_ref, s4_ref, wi_ref, wo_ref, x_ref, y_ref, dy_ref, d3_ref, dx_ref, red_ref, dcw_ref, ds_ref):
        @pl.when(pl.program_id(0) == 0)
        def _():
            ds_ref[tm:tm + 8, :] = jnp.zeros((8, D), F32)
            dcw_ref[...] = jnp.zeros_like(dcw_ref)
            red_ref[...] = jnp.zeros_like(red_ref)

        dy = ((1.0 + vec_ref[3:4, :]) * dxo_ref[...]).astype(BF)
        dy_ref[...] = dy
        dz = _dot_nt(dy, wo_ref[...])
        bg = s4_ref[0].astype(F32)
        cg = s4_ref[1].astype(F32)
        u = s4_ref[2].astype(F32)
        conv = s4_ref[3].astype(F32)
        dbg = dz * conv
        dconv = dz * bg
        ds_ref[0:tm, :] = dconv
        d1 = ds_ref[pl.ds(1, tm), :]
        d2 = ds_ref[pl.ds(2, tm), :]
        ds_ref[tm:tm + 8, :] = ds_ref[0:8, :]
        dv = cw_ref[2:3, :] * dconv + cw_ref[1:2, :] * d1 + cw_ref[0:1, :] * d2
        v = cg * u
        dcw_ref[0:1, :] += jnp.sum(d2 * v, axis=0, keepdims=True)
        dcw_ref[1:2, :] += jnp.sum(d1 * v, axis=0, keepdims=True)
        dcw_ref[2:3, :] += jnp.sum(dconv * v, axis=0, keepdims=True)
        dbcu = jnp.concatenate([dbg, dv * u, dv * cg], axis=1).astype(BF)
        d3_ref[...] = dbcu
        dh = _dot_nt(dbcu[:, 0:cs], wi_ref[0])
        for q in range(1, 4):
            dh = dh + _dot_nt(dbcu[:, q * cs:(q + 1) * cs], wi_ref[q])
        dx_ref[...] = _norm_bwd_tile(dxo_ref[...], dh, x_ref[...], y_ref[...], vec_ref, red_ref, 1.0)

    tok = pl.BlockSpec((tm, D), lambda i: (nt - 1 - i, 0))
    small = pl.BlockSpec((8, D), lambda i: (0, 0))
    once = pl.Buffered(1)
    return pl.pallas_call(
        body, name="conv_dgrad", grid=(nt,),
        in_specs=[tok, small, small, pl.BlockSpec((4, tm, D), lambda i: (0, nt - 1 - i, 0)),
                  pl.BlockSpec((4, D, cs), lambda i: (0, 0, 0), pipeline_mode=once),
                  pl.BlockSpec((D, D), lambda i: (0, 0), pipeline_mode=once), tok, tok],
        out_specs=[tok, pl.BlockSpec((tm, 3 * D), lambda i: (nt - 1 - i, 0)), tok, small, small],
        out_shape=[jax.ShapeDtypeStruct((S, D), BF), jax.ShapeDtypeStruct((S, 3 * D), BF),
                   jax.ShapeDtypeStruct((S, D), F32), jax.ShapeDtypeStruct((8, D), F32), jax.ShapeDtypeStruct((8, D), F32)],
        scratch_shapes=[pltpu.VMEM((tm + 8, D), F32)],
        compiler_params=_cparams(1),
    )(dxo, vec, cw, s4, w_in4, w_out, x, y)


def _rope_tables(positions):
    S = positions.shape[-1]
    inv = ROPE_THETA ** (-jnp.arange(0, ROPE_DIM, 2, dtype=F32) / ROPE_DIM)
    ang = positions.reshape(S, 1).astype(F32) * inv
    cos = jnp.tile(jnp.cos(ang), (1, LANES // 8))
    sin = jnp.tile(jnp.sin(ang), (1, LANES // 8))
    l64 = jnp.arange(LANES) % HEAD_DIM
    return jnp.stack([jnp.where(l64 < ROPE_DIM, cos, 1.0),
                      jnp.where(l64 < ROPE_DIM // 2, -sin, 0.0),
                      jnp.where((l64 >= ROPE_DIM // 2) & (l64 < ROPE_DIM), sin, 0.0)])


def _rope(t, tab_ref):
    return t * tab_ref[0] + pltpu.roll(t, LANES - 8, 1) * tab_ref[1] + pltpu.roll(t, 8, 1) * tab_ref[2]


def _rope_t(d, tab_ref):
    return d * tab_ref[0] + pltpu.roll(d * tab_ref[1], 8, 1) + pltpu.roll(d * tab_ref[2], LANES - 8, 1)


GROUP_CH = GROUP_W // LANES


def _to_dilated(src_ref, c0, dst_ref, d):
    n = src_ref.shape[1]
    for r in range(d):
        for ch in range(GROUP_CH):
            rows = src_ref[c0 + ch] if d == 1 else src_ref.at[c0 + ch][pl.ds(r, n // d, stride=d), :]
            dst_ref[:, r * GROUP_W + ch * LANES:r * GROUP_W + (ch + 1) * LANES] = rows.astype(dst_ref.dtype)


def _from_dilated(val, dst_ref, c0, d):
    n = dst_ref.shape[1]
    for r in range(d):
        for ch in range(GROUP_CH):
            cols = val[:, r * GROUP_W + ch * LANES:r * GROUP_W + (ch + 1) * LANES]
            if d == 1:
                dst_ref[c0 + ch] = cols
            else:
                dst_ref.at[c0 + ch][pl.ds(r, n // d, stride=d), :] = cols


def _group(ref, c0):
    return jnp.concatenate([ref[c0 + ch] for ch in range(GROUP_CH)], axis=1)


def _put_group(ref, c0, val):
    for ch in range(GROUP_CH):
        ref[c0 + ch] = val[:, ch * LANES:(ch + 1) * LANES]


def _dil_spec(tm, d):
    return pl.BlockSpec((tm // d, d * GROUP_W), lambda i: (i, 0))


def _proj_fwd(x, vec, w4, tabs, n_rope, dils, scale=1.0):
    S, D = x.shape
    cs = w4.shape[2]
    N = 4 * cs
    tm = _tile(S, TM)

    def body(x_ref, vec_ref, w_ref, tab_ref, h_ref, *rest):
        outs, acc_ref = rest[:-1], rest[-1]
        h = _norm_mod(x_ref[...], vec_ref[...]).astype(BF)
        h_ref[...] = h
        per = cs // LANES
        for q in range(4):
            acc = _dot(h, w_ref[q])
            if scale != 1.0:
                acc = acc * scale
            cols = [acc[:, ch * LANES:(ch + 1) * LANES] for ch in range(per)]
            for ch, t in enumerate(cols):
                acc_ref[q * per + ch] = _rope(t, tab_ref) if q * per + ch < n_rope else t
        for j, d in enumerate(dils):
            _to_dilated(acc_ref, j * GROUP_CH, outs[j], d)

    h, *outs = pl.pallas_call(
        body, name="proj_fwd", grid=(S // tm,),
        in_specs=[pl.BlockSpec((tm, D), lambda i: (i, 0)), pl.BlockSpec((8, D), lambda i: (0, 0)),
                  pl.BlockSpec((4, D, cs), lambda i: (0, 0, 0)), pl.BlockSpec((3, tm, LANES), lambda i: (0, i, 0))],
        out_specs=[pl.BlockSpec((tm, D), lambda i: (i, 0))] + [_dil_spec(tm, d) for d in dils],
        out_shape=[jax.ShapeDtypeStruct((S, D), BF)] + [jax.ShapeDtypeStruct((S // d, d * GROUP_W), BF) for d in dils],
        scratch_shapes=[pltpu.VMEM((N // LANES, tm, LANES), F32)],
        compiler_params=_cparams(1),
    )(x, vec, w4, tabs)
    return h, outs


def _proj_dgrad(parts, tabs, w4, n_rope_groups, dils, dxo, x, y, vec):
    cs, D = w4.shape[2], w4.shape[1]
    N = 4 * cs
    S = parts[0][0].shape[0] * dils[0]
    tm = _tile(S, TM)
    flat = [p for grp in parts for p in grp]
    counts = [len(grp) for grp in parts]
    has_y = y is not None

    def body(*refs):
        prefs = refs[:len(flat)]
        tab_ref, w_ref, dxo_ref, x_ref = refs[len(flat):len(flat) + 4]
        y_ref = refs[len(flat) + 4] if has_y else None
        vec_ref, dz_ref, dx_ref, red_ref, z_ref = refs[-5:]

        @pl.when(pl.program_id(0) == 0)
        def _():
            red_ref[...] = jnp.zeros_like(red_ref)

        k = 0
        for j, cnt in enumerate(counts):
            z = prefs[k][...]
            for r in prefs[k + 1:k + cnt]:
                z = z + r[...]
            k += cnt
            _from_dilated(z, z_ref, 0, dils[j])
            if j < n_rope_groups:
                z = jnp.concatenate([_rope_t(z_ref[ch], tab_ref) for ch in range(GROUP_CH)], axis=1)
            else:
                z = _group(z_ref, 0)
            dz_ref[:, j * GROUP_W:(j + 1) * GROUP_W] = z.astype(BF)
        dh = _dot_nt(dz_ref[:, 0:cs], w_ref[0])
        for q in range(1, 4):
            dh = dh + _dot_nt(dz_ref[:, q * cs:(q + 1) * cs], w_ref[q])
        dx_ref[...] = _norm_bwd_tile(dxo_ref[...], dh, x_ref[...], y_ref[...] if has_y else None, vec_ref, red_ref, 1.0)

    tok = pl.BlockSpec((tm, D), lambda i: (i, 0))
    small = pl.BlockSpec((8, D), lambda i: (0, 0))
    return pl.pallas_call(
        body, name="proj_dgrad", grid=(S // tm,),
        in_specs=[_dil_spec(tm, d) for d, cnt in zip(dils, counts) for _ in range(cnt)]
        + [pl.BlockSpec((3, tm, LANES), lambda i: (0, i, 0)), pl.BlockSpec((4, D, cs), lambda i: (0, 0, 0)), tok, tok]
        + ([tok] if has_y else []) + [small],
        out_specs=[pl.BlockSpec((tm, N), lambda i: (i, 0)), tok, small],
        out_shape=[jax.ShapeDtypeStruct((S, N), BF), jax.ShapeDtypeStruct((S, D), F32), jax.ShapeDtypeStruct((8, D), F32)],
        scratch_shapes=[pltpu.VMEM((GROUP_CH, tm, LANES), F32)],
        compiler_params=_cparams(1),
    )(*flat, tabs, w4, dxo, x, *([y] if has_y else []), vec)


def _band_masks():
    qi = lax.broadcasted_iota(jnp.int32, (BAND, BAND), 0)
    kj = lax.broadcasted_iota(jnp.int32, (BAND, BAND), 1)
    return kj <= qi, kj >= qi


def _head(ref, h):
    return ref[:, h * HEAD_DIM:(h + 1) * HEAD_DIM]


def _attn_fwd(q, k, v, d):
    sd = q.shape[0]
    nb = sd // BAND

    def body(q_ref, kp_ref, kc_ref, vp_ref, vc_ref, o_ref, l_ref):
        same, prev = _band_masks()
        prev = jnp.logical_and(prev, pl.program_id(1) > 0)
        heads = range(HEADS_PER_GROUP)
        scores = [(jnp.where(same, _dot_nt(_head(q_ref, h), _head(kc_ref, h)), NEG),
                   jnp.where(prev, _dot_nt(_head(q_ref, h), _head(kp_ref, h)), NEG)) for h in heads]
        probs = []
        for sc, sp in scores:
            m = jnp.maximum(jnp.max(sc, axis=-1, keepdims=True), jnp.max(sp, axis=-1, keepdims=True))
            pc = jnp.exp(sc - m)
            pp = jnp.exp(sp - m)
            den = jnp.sum(pc, axis=-1, keepdims=True) + jnp.sum(pp, axis=-1, keepdims=True)
            probs.append((pc.astype(BF), pp.astype(BF), m, den))
        outs = [(_dot(pc, _head(vc_ref, h)) + _dot(pp, _head(vp_ref, h))) / den for h, (pc, pp, _, den) in zip(heads, probs)]
        o_ref[...] = jnp.concatenate(outs, axis=1)
        l_ref[...] = jnp.concatenate([jnp.broadcast_to(m + jnp.log(den), (BAND, HEAD_DIM)) for _, _, m, den in probs], axis=1)

    blk = (BAND, GROUP_W)
    cur = pl.BlockSpec(blk, lambda r, i: (i, r))
    prv = pl.BlockSpec(blk, lambda r, i: (jnp.maximum(i - 1, 0), r))
    return pl.pallas_call(
        body, name=f"attn_fwd_d{d}", grid=(d, nb),
        in_specs=[cur, prv, cur, prv, cur], out_specs=[cur, cur],
        out_shape=[jax.ShapeDtypeStruct((sd, d * GROUP_W), F32)] * 2,
        compiler_params=_cparams(2),
    )(q, k, k, v, v)


def _attn_out(os, ls, x, vec, wo4):
    S, D = x.shape
    cs = wo4.shape[2]
    tm = _tile(S, TM)
    ng = len(DILATIONS)

    def body(*refs):
        o_refs, l_refs = refs[0:2 * ng:2], refs[1:2 * ng:2]
        x_ref, vec_ref, wo_ref = refs[2 * ng:2 * ng + 3]
        outs = refs[2 * ng + 3:]
        mix_refs, lj_refs = outs[0:2 * ng:2], outs[1:2 * ng:2]
        y_ref, xo_ref, nat_ref = outs[2 * ng:]
        ov, lv = [], []
        for g in range(ng):
            _from_dilated(o_refs[g][...], nat_ref, 2 * g * GROUP_CH, DILATIONS[g])
            _from_dilated(l_refs[g][...], nat_ref, (2 * g + 1) * GROUP_CH, DILATIONS[g])
            ov.append(_group(nat_ref, 2 * g * GROUP_CH))
            lv.append(_group(nat_ref, (2 * g + 1) * GROUP_CH))
        mx = jnp.maximum(jnp.maximum(lv[0], lv[1]), lv[2])
        es = [jnp.exp(t - mx) for t in lv]
        den = es[0] + es[1] + es[2]
        mix = (es[0] * ov[0] + es[1] * ov[1] + es[2] * ov[2]) / den
        at_mix, at_lj = 2 * ng * GROUP_CH, (2 * ng + 1) * GROUP_CH
        _put_group(nat_ref, at_mix, mix)
        _put_group(nat_ref, at_lj, mx + jnp.log(den))
        for g in range(ng):
            _to_dilated(nat_ref, at_mix, mix_refs[g], DILATIONS[g])
            _to_dilated(nat_ref, at_lj, lj_refs[g], DILATIONS[g])
        mb = mix.astype(BF)
        y = jnp.concatenate([_dot(mb, wo_ref[q]) for q in range(4)], axis=1)
        y_ref[...] = y.astype(BF)
        xo_ref[...] = x_ref[...] + (1.0 + vec_ref[3:4, :]) * y

    tok = pl.BlockSpec((tm, D), lambda i: (i, 0))
    dil = [_dil_spec(tm, d) for d in DILATIONS for _ in range(2)]
    dil_shape = [jax.ShapeDtypeStruct((S // d, d * GROUP_W), F32) for d in DILATIONS for _ in range(2)]
    outs = pl.pallas_call(
        body, name="attn_out", grid=(S // tm,),
        in_specs=dil + [tok, pl.BlockSpec((8, D), lambda i: (0, 0)), pl.BlockSpec((4, GROUP_W, cs), lambda i: (0, 0, 0))],
        out_specs=dil + [tok, tok],
        out_shape=dil_shape + [jax.ShapeDtypeStruct((S, D), BF), jax.ShapeDtypeStruct((S, D), F32)],
        scratch_shapes=[pltpu.VMEM(((2 * ng + 2) * GROUP_CH, tm, LANES), F32)],
        compiler_params=_cparams(1),
    )(*[t for pair in zip(os, ls) for t in pair], x, vec, wo4)
    return list(outs[0:2 * ng:2]), list(outs[1:2 * ng:2]), outs[2 * ng], outs[2 * ng + 1]


def _attn_out_dgrad(dxo, vec, wo4):
    S, D = dxo.shape
    cs = wo4.shape[2]
    tm = _tile(S, TM)

    def body(dxo_ref, vec_ref, wo_ref, dy_ref, *rest):
        dm_refs, nat_ref = rest[:-1], rest[-1]
        dy = ((1.0 + vec_ref[3:4, :]) * dxo_ref[...]).astype(BF)
        dy_ref[...] = dy
        dm = _dot_nt(dy[:, 0:cs], wo_ref[0])
        for q in range(1, 4):
            dm = dm + _dot_nt(dy[:, q * cs:(q + 1) * cs], wo_ref[q])
        _put_group(nat_ref, 0, dm)
        for g, d in enumerate(DILATIONS):
            _to_dilated(nat_ref, 0, dm_refs[g], d)

    tok = pl.BlockSpec((tm, D), lambda i: (i, 0))
    dy, *dms = pl.pallas_call(
        body, name="attn_out_dgrad", grid=(S // tm,),
        in_specs=[tok, pl.BlockSpec((8, D), lambda i: (0, 0)), pl.BlockSpec((4, GROUP_W, cs), lambda i: (0, 0, 0))],
        out_specs=[tok] + [_dil_spec(tm, d) for d in DILATIONS],
        out_shape=[jax.ShapeDtypeStruct((S, D), BF)] + [jax.ShapeDtypeStruct((S // d, d * GROUP_W), F32) for d in DILATIONS],
        scratch_shapes=[pltpu.VMEM((GROUP_CH, tm, LANES), F32)],
        compiler_params=_cparams(1),
    )(dxo, vec, wo4)
    return dy, dms


def _attn_bwd(q, k, v, dmix, mix, lj, d):
    sd = q.shape[0]
    nb = sd // BAND

    def body(q_ref, kp_ref, kc_ref, vp_ref, vc_ref, do_ref, o_ref, l_ref, dq_ref, dk_ref, dv_ref, keep_k, keep_v):
        i = pl.program_id(1)

        @pl.when(i == 0)
        def _():
            keep_k[...] = jnp.zeros_like(keep_k)
            keep_v[...] = jnp.zeros_like(keep_v)

        @pl.when(i < nb)
        def _():
            same, prev = _band_masks()
            prev = jnp.logical_and(prev, i > 0)
            heads = range(HEADS_PER_GROUP)
            dob = [_head(do_ref, h).astype(BF) for h in heads]
            raw = [(_dot_nt(_head(q_ref, h), _head(kc_ref, h)), _dot_nt(_head(q_ref, h), _head(kp_ref, h)),
                    _dot_nt(dob[h], _head(vc_ref, h)), _dot_nt(dob[h], _head(vp_ref, h))) for h in heads]
            mid = []
            for h, (sc, sp, dpc, dpp) in zip(heads, raw):
                lrow = l_ref[:, h * HEAD_DIM:h * HEAD_DIM + 1]
                delta = jnp.sum(_head(do_ref, h) * _head(o_ref, h), axis=-1, keepdims=True)
                pc = jnp.exp(jnp.where(same, sc, NEG) - lrow)
                pp = jnp.exp(jnp.where(prev, sp, NEG) - lrow)
                mid.append((pc.astype(BF), pp.astype(BF), (pc * (dpc - delta)).astype(BF), (pp * (dpp - delta)).astype(BF)))
            dq = [(_dot(dsc, _head(kc_ref, h)) + _dot(dsp, _head(kp_ref, h))) * SM_SCALE for h, (_, _, dsc, dsp) in zip(heads, mid)]
            dk_prev = [_dot_tn(dsp, _head(q_ref, h)) for h, (_, _, _, dsp) in zip(heads, mid)]
            dv_prev = [_dot_tn(pp, dob[h]) for h, (_, pp, _, _) in zip(heads, mid)]
            dk_same = [_dot_tn(dsc, _head(q_ref, h)) for h, (_, _, dsc, _) in zip(heads, mid)]
            dv_same = [_dot_tn(pc, dob[h]) for h, (pc, _, _, _) in zip(heads, mid)]
            dq_ref[...] = jnp.concatenate(dq, axis=1)
            dk_ref[...] = keep_k[...] + jnp.concatenate(dk_prev, axis=1)
            dv_ref[...] = keep_v[...] + jnp.concatenate(dv_prev, axis=1)
            keep_k[...] = jnp.concatenate(dk_same, axis=1)
            keep_v[...] = jnp.concatenate(dv_same, axis=1)

        @pl.when(i == nb)
        def _():
            dk_ref[...] = keep_k[...]
            dv_ref[...] = keep_v[...]

    blk = (BAND, GROUP_W)
    cur = pl.BlockSpec(blk, lambda r, i: (jnp.minimum(i, nb - 1), r))
    prv = pl.BlockSpec(blk, lambda r, i: (jnp.clip(i - 1, 0, nb - 1), r))
    return pl.pallas_call(
        body, name=f"attn_bwd_d{d}", grid=(d, nb + 1),
        in_specs=[cur, prv, cur, prv, cur, cur, cur, cur], out_specs=[cur, prv, prv],
        out_shape=[jax.ShapeDtypeStruct((sd, d * GROUP_W), F32)] * 3,
        scratch_shapes=[pltpu.VMEM(blk, F32), pltpu.VMEM(blk, F32)],
        compiler_params=_cparams(2),
    )(q, k, k, v, v, dmix, mix, lj)


def _final_loss(x, gvec, tgt):
    S, D = x.shape
    tm = _tile(S, TM)

    def body(x_ref, g_ref, t_ref, dx_ref, red_ref):
        @pl.when(pl.program_id(0) == 0)
        def _():
            red_ref[...] = jnp.zeros_like(red_ref)

        xv = x_ref[...]
        g = g_ref[0:1, :]
        r = _rstd(xv)
        xh = xv * r
        err = xh * g - t_ref[...]
        dy = err * (1.0 / D)
        dxh = dy * g
        dx_ref[...] = r * (dxh - xh * jnp.mean(dxh * xh, axis=-1, keepdims=True))
        red_ref[0:1, :] += jnp.sum(dy * xh, axis=0, keepdims=True)
        red_ref[1:2, :] += jnp.sum(err * err, axis=0, keepdims=True)

    tok = pl.BlockSpec((tm, D), lambda i: (i, 0))
    small = pl.BlockSpec((8, D), lambda i: (0, 0))
    return pl.pallas_call(
        body, name="final_loss", grid=(S // tm,),
        in_specs=[tok, small, tok], out_specs=[tok, small],
        out_shape=[jax.ShapeDtypeStruct((S, D), F32), jax.ShapeDtypeStruct((8, D), F32)],
        compiler_params=_cparams(1),
    )(x, gvec, tgt)


def _adamw(w, g, m, v, copy_g=False):
    shape = w.shape
    C = shape[-1]
    R = w.size // C
    tr = 256 if R % 256 == 0 else R

    def body(w_ref, g_ref, m_ref, v_ref, d_ref, nm_ref, nv_ref, *g_out):
        gv = g_ref[...]
        nm = ADAM_B1 * m_ref[...] + (1.0 - ADAM_B1) * gv
        nv = ADAM_B2 * v_ref[...] + (1.0 - ADAM_B2) * (gv * gv)
        m_hat = nm / (1.0 - ADAM_B1 ** ADAM_STEP)
        v_hat = nv / (1.0 - ADAM_B2 ** ADAM_STEP)
        d_ref[...] = -ADAM_LR * (m_hat / (jnp.sqrt(v_hat) + ADAM_EPS) + ADAM_WD * w_ref[...])
        nm_ref[...] = nm
        nv_ref[...] = nv
        if copy_g:
            g_out[0][...] = gv

    spec = pl.BlockSpec((tr, C), lambda i: (i, 0))
    n_out = 4 if copy_g else 3
    outs = pl.pallas_call(
        body, name="adamw", grid=(R // tr,),
        in_specs=[spec] * 4, out_specs=[spec] * n_out,
        out_shape=[jax.ShapeDtypeStruct((R, C), F32)] * n_out,
        compiler_params=_cparams(1),
    )(*(t.reshape(R, C) for t in (w, g, m, v)))
    return tuple(o.reshape(shape) for o in (*outs[:3], outs[3] if copy_g else g))


def _ada_fwd(c_all, w, b):
    L, D, N = w.shape
    tn = 768 if N % 768 == 0 else _tile(N, 512)

    def body(c_ref, w_ref, b_ref, o_ref):
        cv = c_ref[...]
        cond = (cv * jax.nn.sigmoid(cv)).astype(BF)
        o_ref[...] = _dot(cond, w_ref[...].astype(BF)) + b_ref[...]

    return pl.pallas_call(
        body, name="ada_fwd", grid=(L, N // tn),
        in_specs=[pl.BlockSpec((8, D), lambda l, n: (0, 0)), pl.BlockSpec((None, D, tn), lambda l, n: (l, 0, n)),
                  pl.BlockSpec((None, 1, tn), lambda l, n: (l, 0, n))],
        out_specs=pl.BlockSpec((None, 8, tn), lambda l, n: (l, 0, n)),
        out_shape=jax.ShapeDtypeStruct((L, 8, N), F32),
        compiler_params=_cparams(2),
    )(c_all, w, b)


def _ada_wgrad(c_all_t, dm):
    D = c_all_t.shape[0]
    L, _, N = dm.shape
    tn = 256

    def body(c_ref, dm_ref, o_ref):
        cv = c_ref[...]
        cond = cv * jax.nn.sigmoid(cv)
        acc = cond[:, 0:1] * dm_ref[0:1, :]
        for b in range(1, 8):
            acc = acc + cond[:, b:b + 1] * dm_ref[b:b + 1, :]
        o_ref[...] = acc

    return pl.pallas_call(
        body, name="ada_wgrad", grid=(L, N // tn),
        in_specs=[pl.BlockSpec((D, 8), lambda l, n: (0, 0)), pl.BlockSpec((None, 8, tn), lambda l, n: (l, 0, n))],
        out_specs=pl.BlockSpec((None, D, tn), lambda l, n: (l, 0, n)),
        out_shape=jax.ShapeDtypeStruct((L, D, N), F32),
        compiler_params=_cparams(2),
    )(c_all_t, dm)


def _sum8(g):
    _, R, C = g.shape

    def body(g_ref, o_ref):
        acc = g_ref[0]
        for b in range(1, 8):
            acc = acc + g_ref[b]
        o_ref[...] = acc

    return pl.pallas_call(body, name="sum8", out_shape=jax.ShapeDtypeStruct((R, C), F32),
                          in_specs=[pl.BlockSpec(memory_space=pltpu.VMEM)],
                          out_specs=pl.BlockSpec(memory_space=pltpu.VMEM))(g)


def _pair_add(g4, recv, cidx):
    _, _, ha, B = g4.shape
    tr = _row_tile(ha)

    def body(c_ref, g_ref, r_ref, o_ref):
        o_ref[...] = (g_ref[...].astype(F32) + r_ref[...].astype(F32)).astype(BF)

    return pl.pallas_call(
        body, name="pair_add",
        grid_spec=pltpu.PrefetchScalarGridSpec(
            num_scalar_prefetch=1, grid=(4, ha // tr),
            in_specs=[pl.BlockSpec((None, None, tr, B), lambda s, i, c: (s, c[0], i, 0)),
                      pl.BlockSpec((None, tr, B), lambda s, i, c: (s, i, 0))],
            out_specs=pl.BlockSpec((None, tr, B), lambda s, i, c: (s, i, 0))),
        out_shape=jax.ShapeDtypeStruct((4, ha, B), BF),
        compiler_params=_cparams(2),
    )(cidx, g4, recv)


def _sum_partials(part, recv, gbuf, n_layers, l, place):
    _, ha, B = part.shape
    tr = _row_tile(ha, cap=256)

    def body(pc_ref, p_ref, r_ref, *rest):
        o_ref = rest[-1]
        o_ref[...] = ((p_ref[...].astype(F32) + r_ref[0].astype(F32)) + r_ref[1].astype(F32)) + r_ref[2].astype(F32)

    in_specs = [pl.BlockSpec((None, tr, B), lambda i, pc: (pc[0], i, 0)), pl.BlockSpec((3, tr, B), lambda i, pc: (0, i, 0))]
    ops = [part, recv]
    if gbuf is not None:
        in_specs.append(ANY)
        ops.append(gbuf)
    return pl.pallas_call(
        body, name="sum_partials",
        grid_spec=pltpu.PrefetchScalarGridSpec(
            num_scalar_prefetch=1, grid=(ha // tr,), in_specs=in_specs,
            out_specs=pl.BlockSpec((None, None, tr, B), lambda i, pc: (l, pc[1], i, 0))),
        out_shape=jax.ShapeDtypeStruct((n_layers, 2, ha, B), F32),
        input_output_aliases={} if gbuf is None else {3: 0},
        compiler_params=_cparams(1),
    )(place, *ops)


def _cast_shard(shard, l, place):
    _, _, ha, B = shard.shape
    tr = _row_tile(ha)

    def body(pc_ref, s_ref, o_ref):
        o_ref[...] = s_ref[...].astype(BF)

    return pl.pallas_call(
        body, name="cast_shard",
        grid_spec=pltpu.PrefetchScalarGridSpec(
            num_scalar_prefetch=1, grid=(2, ha // tr),
            in_specs=[pl.BlockSpec((None, None, tr, B), lambda h, i, pc: (l, h, i, 0))],
            out_specs=pl.BlockSpec((None, None, tr, B), lambda h, i, pc: (pc[0], h, i, 0))),
        out_shape=jax.ShapeDtypeStruct((4, 2, ha, B), BF),
        compiler_params=_cparams(2),
    )(place, shard)


def _gather8(v):
    R, C = v.shape

    def body(v_ref, o_ref, ssem, rsem):
        x, y, c, *_ = _place()
        me = 4 * x + 2 * y + c
        o_ref[me] = v_ref[...]
        cps = []
        for k in range(1, 8):
            fx, fy, fc = (k >> 2) & 1, (k >> 1) & 1, k & 1
            peer = (x ^ fx, y ^ fy, c ^ fc)
            cp = pltpu.make_async_remote_copy(src_ref=v_ref, dst_ref=o_ref.at[me], send_sem=ssem.at[k - 1],
                                              recv_sem=rsem.at[k - 1], device_id=peer, device_id_type=MESH)
            cp.start()
            cps.append((cp, 4 * peer[0] + 2 * peer[1] + peer[2]))
        for k, (cp, slot) in enumerate(cps):
            there = o_ref.at[slot]
            pltpu.make_async_remote_copy(src_ref=there, dst_ref=there, send_sem=ssem.at[k], recv_sem=rsem.at[k],
                                         device_id=(x, y, c), device_id_type=MESH).wait_recv()
        for cp, _ in cps:
            cp.wait_send()

    vm = pl.BlockSpec(memory_space=pltpu.VMEM)
    return pl.pallas_call(
        body, name="gather8", in_specs=[vm], out_specs=vm, out_shape=jax.ShapeDtypeStruct((8, R, C), F32),
        scratch_shapes=[pltpu.SemaphoreType.DMA((7,)), pltpu.SemaphoreType.DMA((7,))],
    )(v)


def _mods_to_owner(ms):
    _, R, C = ms.shape

    def body(m_ref, o_ref, ssem, rsem):
        x, y, c, p, chips, qs = _place()
        o_ref[p] = m_ref[2 * p + c]
        cps = []
        for j in range(3):
            cp = pltpu.make_async_remote_copy(src_ref=m_ref.at[2 * qs[j] + c], dst_ref=o_ref.at[p], send_sem=ssem.at[j],
                                              recv_sem=rsem.at[j], device_id=(*chips[j], c), device_id_type=MESH)
            cp.start()
            cps.append(cp)
        for j in range(3):
            there = o_ref.at[qs[j]]
            pltpu.make_async_remote_copy(src_ref=there, dst_ref=there, send_sem=ssem.at[j], recv_sem=rsem.at[j],
                                         device_id=(x, y, c), device_id_type=MESH).wait_recv()
        for cp in cps:
            cp.wait_send()

    vm = pl.BlockSpec(memory_space=pltpu.VMEM)
    return pl.pallas_call(
        body, name="mods_to_owner", in_specs=[vm], out_specs=vm, out_shape=jax.ShapeDtypeStruct((4, R, C), F32),
        scratch_shapes=[pltpu.SemaphoreType.DMA((3,)), pltpu.SemaphoreType.DMA((3,))],
    )(ms)


def _vec(*rows):
    D = rows[0].shape[-1]
    rows = [r.reshape(1, D) for r in rows]
    return jnp.concatenate(rows + [jnp.zeros((8 - len(rows), D), F32)], axis=0)


def _sets(n_layers, n_conv):
    sets = []
    for l in range(n_layers):
        j = l - n_conv
        mixer = [("conv_w_in", l), ("conv_w_out", l)] if l < n_conv else [("attn_w_q", j), ("attn_w_o", j)]
        sets.append([("ffn1_w_in", l), ("ffn1_w_out", l)] + ([("w_kv", 0)] if l == n_conv else []))
        sets.append(mixer + [("ffn2_w_in", l), ("ffn2_w_out", l)])
    return sets


def _as_weight(name, buf):
    _, _, ha, B = buf.shape
    return buf.reshape(8 * ha, B) if name.endswith("w_out") else buf.reshape(4, 2 * ha, B)


def _local_step(x, tgt, tabs, norm_g, conv_w, kv_norm_g, final_norm_g, mods, kvmods, cast, place):
    S, D = x.shape
    n_layers, n_conv = norm_g.shape[0], conv_w.shape[0]
    md = mods.reshape(n_layers, 9, D)

    def vec_of(l, k):
        return _vec(norm_g[l, k], md[l, 3 * k + 1], md[l, 3 * k], md[l, 3 * k + 2])

    W = {}

    def install(items, bufs):
        for (name, idx), b in zip(items, bufs):
            W[name, idx] = _as_weight(name, b)

    Q = _sets(n_layers, n_conv)
    assert n_conv >= 1, "the first layer's conv mixer carries a weight set"
    install(Q[0], _comm_call("gather_d2d", _comm_call("gather_ici", [cast[it] for it in Q[0]])))
    arrived = {}

    def ffn_fwd(n, *args):
        ici = [k for k in ([1] if n == 0 else [n + 2]) if k < len(Q)]
        d2d = [n + 1] if 2 <= n + 1 < len(Q) else []
        comm = [("gather_ici", [cast[it] for it in Q[k]]) for k in ici] + [("gather_d2d", arrived.pop(k)) for k in d2d]
        xo, h, ab, y, *moved = _ffn_fwd(*args, comm)
        for k in ici:
            arrived[k], moved = moved[:len(Q[k])], moved[len(Q[k]):]
        for k in d2d:
            install(Q[k], moved[:len(Q[k])])
            moved = moved[len(Q[k]):]
        if n == 0:
            install(Q[1], _comm_call("gather_d2d", arrived.pop(1)))
        return xo, h, ab, y

    vec_kv = _vec(kv_norm_g, kvmods[D:], kvmods[:D])
    ng = len(DILATIONS)
    n_rope = ng * GROUP_W // LANES
    saved = {}
    xc = x
    ks = vs = None
    for l in range(n_layers):
        if l == n_conv:
            h_kv, kv = _proj_fwd(xc, vec_kv, W["w_kv", 0], tabs, n_rope, DILATIONS * 2)
            ks, vs = kv[:ng], kv[ng:]
            saved["kv"] = (xc, h_kv)
        v0 = vec_of(l, 0)
        xo, h, ab, y = ffn_fwd(2 * l, xc, v0, W["ffn1_w_in", l], W["ffn1_w_out", l])
        saved[l, 0] = (xc, v0, h, ab, y)
        xc = xo
        v1 = vec_of(l, 1)
        if l < n_conv:
            cw = _vec(conv_w[l, 0], conv_w[l, 1], conv_w[l, 2])
            early = [("gather_ici", [cast[it] for it in Q[2]])] if l == 0 and len(Q) > 2 else None
            xo, h, s4, z, y, *moved = _conv_fwd(xc, v1, cw, W["conv_w_in", l], W["conv_w_out", l], early)
            if early:
                arrived[2] = moved
            saved[l, 1] = (xc, v1, h, y, cw, s4, z)
        else:
            j = l - n_conv
            h, qs = _proj_fwd(xc, v1, W["attn_w_q", j], tabs, n_rope, DILATIONS, SM_SCALE)
            og = [_attn_fwd(qs[g], ks[g], vs[g], DILATIONS[g]) for g in range(ng)]
            mixs, ljs, y, xo = _attn_out([o for o, _ in og], [s for _, s in og], xc, v1, W["attn_w_o", j])
            saved[l, 1] = (xc, v1, h, y, qs, mixs, ljs)
        xc = xo
        v2 = vec_of(l, 2)
        xo, h, ab, y = ffn_fwd(2 * l + 1, xc, v2, W["ffn2_w_in", l], W["ffn2_w_out", l])
        saved[l, 2] = (xc, v2, h, ab, y)
        xc = xo

    dx, red_f = _final_loss(xc, _vec(final_norm_g), tgt)
    loss_part = 0.5 * jnp.sum(red_f[1]) / D
    G, parts, others = {}, {}, {}
    dmods = [[None] * 9 for _ in range(n_layers)]
    dnorm = [[None] * 3 for _ in range(n_layers)]
    dconvw = [None] * n_conv
    dkv_parts = []

    def halves(items):
        return [G[it].reshape(4, 2, G[it].shape[1] // 2, G[it].shape[2]) for it in items]

    def pair_sums(items, recv):
        for it, g, r in zip(items, halves(items), recv):
            parts[it] = _pair_add(g, r, place[1:])
        return [parts[it] for it in items]

    R = Q[::-1]
    recvd = {}

    def ffn_bwd(dx, l, k, name):
        m = 2 * (n_layers - 1 - l) + (0 if k == 2 else 1)
        ex = [m - 1] if m >= 1 else []
        sc = [m - 2] if m >= 2 else []
        comm = [("exchange", halves(R[e])) for e in ex] + [("scatter", pair_sums(R[e], recvd.pop(e))) for e in sc]
        xin, v, h, ab, y = saved[l, k]
        w_in, w_out = W[name + "_w_in", l], W[name + "_w_out", l]
        dy, u, dab, dh, *moved = _ffn_dgrad(dx, v, ab, w_in, w_out, comm)
        for e in ex:
            recvd[e], moved = moved[:len(R[e])], moved[len(R[e]):]
        for e in sc:
            others.update(zip(R[e], moved[:len(R[e])]))
            moved = moved[len(R[e]):]
        if m == len(R) - 1:
            G[name + "_w_in", l], got = _mm_tn(h, dab, w_in.shape[2], [("scatter", pair_sums(R[m - 1], recvd.pop(m - 1)))])
            others.update(zip(R[m - 1], got))
        else:
            G[name + "_w_in", l] = _mm_tn(h, dab, w_in.shape[2])
        G[name + "_w_out", l] = _mm_tn(u, dy[None], D).reshape(4, -1, D)
        return _norm_bwd(dx, [dh], xin, y, v, 0.5)

    def note(l, k, red):
        dnorm[l][k] = red[0]
        dmods[l][3 * k], dmods[l][3 * k + 1], dmods[l][3 * k + 2] = red[2], red[1], red[3]

    for l in reversed(range(n_layers)):
        dx, red = ffn_bwd(dx, l, 2, "ffn2")
        note(l, 2, red)
        if l < n_conv:
            xin, v, h, y, cw, s4, z = saved[l, 1]
            w_in, w_out = W["conv_w_in", l], W["conv_w_out", l]
            dy, d3, dx, red, dcw = _conv_dgrad(dx, v, cw, s4, w_in, w_out, xin, y)
            G["conv_w_in", l] = _mm_tn(h, d3[None], w_in.shape[2])
            G["conv_w_out", l] = _mm_tn(z, dy[None], D).reshape(4, -1, D)
            dconvw[l] = dcw[0:3]
        else:
            j = l - n_conv
            xin, v, h, y, qs, mixs, ljs = saved[l, 1]
            wq, wo = W["attn_w_q", j], W["attn_w_o", j]
            dy, dmixs = _attn_out_dgrad(dx, v, wo)
            G["attn_w_o", j] = _mm_tn(mixs[0], dy[None], wo.shape[2])
            dqkv = [_attn_bwd(qs[g], ks[g], vs[g], dmixs[g], mixs[g], ljs[g], DILATIONS[g]) for g in range(ng)]
            dkv_parts.append([t[1:] for t in dqkv])
            dq, dx, red = _proj_dgrad([[t[0]] for t in dqkv], tabs, wq, ng, DILATIONS, dx, xin, y, v)
            G["attn_w_q", j] = _mm_tn(h, dq[None], wq.shape[2])
        note(l, 1, red)
        dx, red = ffn_bwd(dx, l, 0, "ffn1")
        note(l, 0, red)
        if l == n_conv:
            xin, h_kv = saved["kv"]
            kv_parts = [[lay[g][0] for lay in dkv_parts] for g in range(ng)] + [[lay[g][1] for lay in dkv_parts] for g in range(ng)]
            dkv, dx, red_kv = _proj_dgrad(kv_parts, tabs, W["w_kv", 0], ng, DILATIONS * 2, dx, xin, None, vec_kv)
            G["w_kv", 0] = _mm_tn(h_kv, dkv[None], W["w_kv", 0].shape[2])
    others.update(zip(R[-1], _comm_call("scatter", pair_sums(R[-1], _comm_call("exchange", halves(R[-1]))))))
    small = jnp.concatenate(
        [jnp.stack([jnp.stack(r) for r in dmods]).reshape(-1), red_kv[2], red_kv[1],
         jnp.stack([jnp.stack(r) for r in dnorm]).reshape(-1), jnp.stack(dconvw).reshape(-1), red_kv[0], red_f[0]])
    return loss_part, dx, parts, others, small


BIG = ("ffn1_w_in", "ffn1_w_out", "ffn2_w_in", "ffn2_w_out", "conv_w_in", "conv_w_out", "w_kv", "attn_w_q", "attn_w_o")


def _halved(t):
    if t.ndim == 2:
        t = t[None]
    L, A, B = t.shape
    return t.reshape(L, 2, A // 2, B)


def kernel(x, c, positions, norm_g, ada_w, ada_b, ffn1_w_in, ffn1_w_out, ffn2_w_in, ffn2_w_out, conv_w_in, conv_w, conv_w_out, kv_norm_g, kv_ada_w, kv_ada_b, w_kv, attn_w_q, attn_w_o, final_norm_g, loss_target, m_norm_g, m_ada_w, m_ada_b, m_ffn1_w_in, m_ffn1_w_out, m_ffn2_w_in, m_ffn2_w_out, m_conv_w_in, m_conv_w, m_conv_w_out, m_kv_norm_g, m_kv_ada_w, m_kv_ada_b, m_w_kv, m_attn_w_q, m_attn_w_o, m_final_norm_g, v_norm_g, v_ada_w, v_ada_b, v_ffn1_w_in, v_ffn1_w_out, v_ffn2_w_in, v_ffn2_w_out, v_conv_w_in, v_conv_w, v_conv_w_out, v_kv_norm_g, v_kv_ada_w, v_kv_ada_b, v_w_kv, v_attn_w_q, v_attn_w_o, v_final_norm_g):
    wts = dict(norm_g=norm_g, ada_w=ada_w, ada_b=ada_b, ffn1_w_in=ffn1_w_in, ffn1_w_out=ffn1_w_out, ffn2_w_in=ffn2_w_in,
               ffn2_w_out=ffn2_w_out, conv_w_in=conv_w_in, conv_w=conv_w, conv_w_out=conv_w_out, kv_norm_g=kv_norm_g,
               kv_ada_w=kv_ada_w, kv_ada_b=kv_ada_b, w_kv=w_kv, attn_w_q=attn_w_q, attn_w_o=attn_w_o,
               final_norm_g=final_norm_g)
    ms = dict(norm_g=m_norm_g, ada_w=m_ada_w, ada_b=m_ada_b, ffn1_w_in=m_ffn1_w_in, ffn1_w_out=m_ffn1_w_out,
              ffn2_w_in=m_ffn2_w_in, ffn2_w_out=m_ffn2_w_out, conv_w_in=m_conv_w_in, conv_w=m_conv_w,
              conv_w_out=m_conv_w_out, kv_norm_g=m_kv_norm_g, kv_ada_w=m_kv_ada_w, kv_ada_b=m_kv_ada_b, w_kv=m_w_kv,
              attn_w_q=m_attn_w_q, attn_w_o=m_attn_w_o, final_norm_g=m_final_norm_g)
    vs = dict(norm_g=v_norm_g, ada_w=v_ada_w, ada_b=v_ada_b, ffn1_w_in=v_ffn1_w_in, ffn1_w_out=v_ffn1_w_out,
              ffn2_w_in=v_ffn2_w_in, ffn2_w_out=v_ffn2_w_out, conv_w_in=v_conv_w_in, conv_w=v_conv_w,
              conv_w_out=v_conv_w_out, kv_norm_g=v_kv_norm_g, kv_ada_w=v_kv_ada_w, kv_ada_b=v_kv_ada_b, w_kv=v_w_kv,
              attn_w_q=v_attn_w_q, attn_w_o=v_attn_w_o, final_norm_g=v_final_norm_g)
    order = list(wts)
    S, D = x.shape[1], x.shape[2]
    n_layers, n_conv = norm_g.shape[0], conv_w.shape[0]
    nm, sw = ada_w.shape[2], norm_g.shape[2]
    ix, iy, ic = (lax.axis_index(a) for a in AXES)
    chip = 2 * ix + iy

    pack = jnp.concatenate([norm_g.reshape(-1), conv_w.reshape(-1), c.reshape(-1)])
    npad = (-pack.size) % (8 * LANES)
    allp = _gather8(jnp.pad(pack, (0, npad)).reshape(-1, LANES)).reshape(8, -1)
    n1, n2 = norm_g.size, norm_g.size + conv_w.size
    by_chip = allp[0::2]
    norm_full = jnp.moveaxis(by_chip[:, :n1].reshape(4, n_layers, 3, sw), 0, 2).reshape(n_layers, 3, 4 * sw)
    conv_full = jnp.moveaxis(by_chip[:, n1:n2].reshape(4, n_conv, 3, sw), 0, 2).reshape(n_conv, 3, 4 * sw)
    c_all = allp[:, n2:n2 + D]

    b_l = lax.dynamic_slice_in_dim(ada_b, chip * nm, nm, axis=1)
    mod_sh = _ada_fwd(c_all, ada_w, b_l[:, None, :])
    nkv = kv_ada_w.shape[1]
    bkv = lax.dynamic_slice_in_dim(kv_ada_b, chip * nkv, nkv, axis=0)
    kv_sh = _ada_fwd(c_all, kv_ada_w[None], bkv[None, None, :])
    rows = jnp.concatenate([jnp.moveaxis(mod_sh, 0, 1), jnp.pad(jnp.moveaxis(kv_sh, 0, 1), ((0, 0), (0, 0), (0, nm - nkv)))], axis=1)
    rpad = (-rows.shape[1]) % 8
    mine = _mods_to_owner(jnp.pad(rows, ((0, 0), (0, rpad), (0, 0))))
    mods = jnp.moveaxis(mine[:, :n_layers], 0, 1).reshape(n_layers, 4 * nm)
    kvmods = mine[:, n_layers, :nkv].reshape(4 * nkv)

    place = jnp.stack([chip, ic]).astype(jnp.int32)
    halved = {k: _halved(wts[k]) for k in BIG}
    cast = {(k, l): _cast_shard(halved[k], l, place) for k in BIG for l in range(halved[k].shape[0])}

    loss_part, dx, parts, others, small = _local_step(x[0], loss_target[0], _rope_tables(positions), norm_full, conv_full,
                                                      kv_norm_g, final_norm_g, mods, kvmods, cast, place)
    loss = lax.psum(loss_part, AXES)

    gbufs = []
    for k in BIG:
        gbuf, n_l = None, halved[k].shape[0]
        for l in range(n_l):
            gbuf = _sum_partials(parts[k, l], others[k, l], gbuf, n_l, l, place)
        gbufs.append(gbuf)
    grads = {}

    spad = (-small.size) % (8 * LANES)
    gath = _gather8(jnp.pad(small, (0, spad)).reshape(-1, LANES))
    tot = _sum8(gath).reshape(-1)
    gath = gath.reshape(8, -1)
    o = 0

    def take(n):
        nonlocal o
        o += n
        return tot[o - n:o]

    g_mods = take(n_layers * 9 * D).reshape(n_layers, 9 * D)
    g_kvmods = take(2 * D)
    g_norm = take(n_layers * 3 * D).reshape(n_layers, 3, D)
    g_convw = take(n_conv * 3 * D).reshape(n_conv, 3, D)
    g_kvn = take(D)
    g_fin = take(D)
    grads["ada_b"] = g_mods
    grads["kv_ada_b"] = g_kvmods
    grads["norm_g"] = lax.dynamic_slice_in_dim(g_norm, chip * sw, sw, axis=2)
    grads["conv_w"] = lax.dynamic_slice_in_dim(g_convw, chip * sw, sw, axis=2)
    grads["kv_norm_g"] = g_kvn
    grads["final_norm_g"] = g_fin

    dm_all = gath[:, :n_layers * 9 * D].reshape(8, n_layers, 9 * D)
    dm_mine = jnp.moveaxis(lax.dynamic_slice_in_dim(dm_all, chip * nm, nm, axis=2), 0, 1)
    dkv_all = gath[:, n_layers * 9 * D:n_layers * 9 * D + 2 * D]
    dkv_mine = lax.dynamic_slice_in_dim(dkv_all, chip * nkv, nkv, axis=1)[None]
    c_t = c_all.T
    grads["ada_w"] = _ada_wgrad(c_t, dm_mine)
    grads["kv_ada_w"] = _ada_wgrad(c_t, dkv_mine)[0]

    grads.update({k: s.reshape(wts[k].shape) for k, s in zip(BIG, _comm_call("share", gbufs))})
    deltas, new_m, new_v = {}, {}, {}
    for k in order:
        shp = wts[k].shape
        two_d = (lambda t: t.reshape(1, -1)) if len(shp) == 1 else (lambda t: t)
        dlt, nmk, nvk, gk = _adamw(two_d(wts[k]), two_d(grads[k]), two_d(ms[k]), two_d(vs[k]), k in BIG)
        deltas[k], new_m[k], new_v[k] = dlt.reshape(shp), nmk.reshape(shp), nvk.reshape(shp)
        grads[k] = gk.reshape(shp)
    return (loss, dx[None], *[grads[k] for k in order], *[deltas[k] for k in order], *[new_m[k] for k in order],
            *[new_v[k] for k in order])
```

```python
import functools
import math

import jax
import jax.numpy as jnp
from jax import lax
from jax.experimental import pallas as pl
from jax.experimental.pallas import tpu as pltpu

F32 = jnp.float32
BF = jnp.bfloat16
MESH = pl.DeviceIdType.MESH
AXES = ("x", "y", "c")

NORM_EPS = 1e-5
HEAD_DIM = 64
HEADS_PER_GROUP = 8
GROUP_W = HEADS_PER_GROUP * HEAD_DIM
DILATIONS = (1, 4, 16)
BAND = 128
ROPE_DIM = 16
ROPE_THETA = 500000.0
SM_SCALE = HEAD_DIM ** -0.5
NEG = -1e30
ADAM_LR, ADAM_B1, ADAM_B2, ADAM_EPS, ADAM_WD, ADAM_STEP = 0.001, 0.9, 0.999, 1e-08, 0.01, 10

V7X_VMEM_BYTES = 64 * 1024 * 1024
VMEM_LIMIT = V7X_VMEM_BYTES - 6 * 1024 * 1024
LANES = 128
TM = 512
TM_DGRAD = 256
TM_STREAM = 1024
TK = 2048
WGRAD_COLS = 1536


def _cparams(ngrid):
    return pltpu.CompilerParams(dimension_semantics=("arbitrary",) * ngrid, vmem_limit_bytes=VMEM_LIMIT)


def _dot(a, b):
    return jnp.dot(a, b, preferred_element_type=F32)


def _dot_nt(a, b):
    return lax.dot_general(a, b, (((1,), (1,)), ((), ())), preferred_element_type=F32)


def _dot_tn(a, b):
    return lax.dot_general(a, b, (((0,), (0,)), ((), ())), preferred_element_type=F32)


def _tile(n, pref):
    t = min(n, pref)
    while n % t:
        t //= 2
    return t


def _row_tile(n, cap=512, mult=16):
    best = n
    for t in range(mult, min(n, cap) + 1, mult):
        if n % t == 0:
            best = t
    return best


def _rstd(x):
    return lax.rsqrt(jnp.mean(x * x, axis=-1, keepdims=True) + NORM_EPS)


def _norm_mod(x, v):
    return (x * _rstd(x) * v[0:1]) * (1.0 + v[1:2]) + v[2:3]


def _silu_parts(a):
    sg = jax.nn.sigmoid(a)
    return sg, a * sg


def _row(i):
    return lambda *_: (i, 0)


ANY = pl.BlockSpec(memory_space=pl.ANY)
IN_PLACE = ("gather_ici", "gather_d2d", "share")


def _place():
    x, y, c = (lax.axis_index(a) for a in AXES)
    chips = [(1 - x, y), (x, 1 - y), (1 - x, 1 - y)]
    return x, y, c, 2 * x + y, chips, [2 * cx + cy for cx, cy in chips]


def _transfers(kind, ins, outs):
    x, y, c, p, chips, qs = _place()
    sib = (x, y, 1 - c)
    rows = []
    for k, o in enumerate(outs):
        if kind == "gather_ici":
            rows.append([(o.at[p, c], o.at[p, c], (*chips[j], c), o.at[qs[j], c]) for j in range(3)])
        elif kind == "gather_d2d":
            rows.append([(o.at[qs[j], c], o.at[qs[j], c], sib, o.at[qs[j], 1 - c]) for j in range(3)])
        elif kind == "exchange":
            rows.append([(ins[k].at[:, 1 - c], o, sib, o)])
        elif kind == "scatter":
            rows.append([(ins[k].at[qs[j]], o.at[j], (*chips[j], c), o.at[j]) for j in range(3)])
        elif kind == "share":
            rows.append([(o.at[:, c], o.at[:, c], sib, o.at[:, 1 - c])])
    return rows


def _comm_out_shapes(kind, arrays):
    if kind in IN_PLACE:
        return [jax.ShapeDtypeStruct(a.shape, a.dtype) for a in arrays]
    if kind == "exchange":
        return [jax.ShapeDtypeStruct((4,) + a.shape[2:], a.dtype) for a in arrays]
    return [jax.ShapeDtypeStruct((3,) + a.shape[1:], a.dtype) for a in arrays]


def _comm_sems(n):
    return [pltpu.SemaphoreType.DMA((n, 3)), pltpu.SemaphoreType.DMA((n, 3))]


def _comm_start(rows, ssem, rsem):
    for k, row in enumerate(rows):
        for j, (src, dst, dev, _) in enumerate(row):
            pltpu.make_async_remote_copy(src_ref=src, dst_ref=dst, send_sem=ssem.at[k, j], recv_sem=rsem.at[k, j],
                                         device_id=dev, device_id_type=MESH).start()


def _comm_wait(rows, ssem, rsem):
    for k, row in enumerate(rows):
        for j, (src, dst, dev, land) in enumerate(row):
            pltpu.make_async_remote_copy(src_ref=src, dst_ref=dst, send_sem=ssem.at[k, j], recv_sem=rsem.at[k, j],
                                         device_id=dev, device_id_type=MESH).wait_send()
            pltpu.make_async_remote_copy(src_ref=land, dst_ref=land, send_sem=ssem.at[k, j], recv_sem=rsem.at[k, j],
                                         device_id=dev, device_id_type=MESH).wait_recv()


def _comm_call(kind, arrays):
    n = len(arrays)

    def body(*refs):
        rows = _transfers(kind, refs[:n], refs[n:2 * n])
        _comm_start(rows, *refs[2 * n:])
        _comm_wait(rows, *refs[2 * n:])

    return pl.pallas_call(
        body, name="comm_" + kind, in_specs=[ANY] * n, out_specs=[ANY] * n, out_shape=_comm_out_shapes(kind, arrays),
        input_output_aliases={k: k for k in range(n)} if kind in IN_PLACE else {},
        scratch_shapes=_comm_sems(n),
    )(*arrays)


def _ride(comm, n_in, n_out):
    ride = dict(arrays=[], in_specs=[], out_specs=[], out_shape=[], aliases={}, scratch=[], n=0)
    for kind, arrays in comm or []:
        if kind in IN_PLACE:
            ride["aliases"].update({n_in + ride["n"] + k: n_out + ride["n"] + k for k in range(len(arrays))})
        ride["arrays"] += list(arrays)
        ride["out_shape"] += _comm_out_shapes(kind, arrays)
        ride["scratch"] += _comm_sems(len(arrays))
        ride["n"] += len(arrays)
    ride["in_specs"] = ride["out_specs"] = [ANY] * ride["n"]
    return ride


def _ride_when(comm, ins, outs, sems, cond, action):
    if not comm:
        return

    @pl.when(cond)
    def _():
        at = 0
        for e, (kind, arrays) in enumerate(comm):
            n = len(arrays)
            action(_transfers(kind, ins[at:at + n], outs[at:at + n]), *sems[2 * e:2 * e + 2])
            at += n


def _ffn_fwd(x, vec, w_in4, w_out, comm=None):
    S, D = x.shape
    cs = w_in4.shape[2]
    F = 2 * cs
    tm = _tile(S, TM)
    ni = S // tm
    ride = _ride(comm, 5, 4)
    nc = ride["n"]

    def body(*refs):
        x_ref, vec_ref, wa_ref, wb_ref, wo_ref = refs[:5]
        xo_ref, h_ref, ab_ref, y_ref = refs[5 + nc:9 + nc]
        i = pl.program_id(0)
        riding = (comm, refs[5:5 + nc], refs[9 + nc:9 + 2 * nc], refs[9 + 2 * nc:])
        _ride_when(*riding, i == 0, _comm_start)
        xv = x_ref[...]
        h = _norm_mod(xv, vec_ref[...]).astype(BF)
        h_ref[...] = h
        y = None
        for c in range(2):
            cols = slice(c * cs, (c + 1) * cs)
            a = _dot(h, wa_ref[c])
            b = _dot(h, wb_ref[c])
            ab_ref[0, :, cols] = a.astype(BF)
            ab_ref[1, :, cols] = b.astype(BF)
            _, s = _silu_parts(a)
            part = _dot((s * b).astype(BF), wo_ref[cols, :])
            y = part if y is None else y + part
        y_ref[...] = y.astype(BF)
        xo_ref[...] = xv + (0.5 * (1.0 + vec_ref[3:4, :])) * y
        _ride_when(*riding, i == ni - 1, _comm_wait)

    tok = pl.BlockSpec((tm, D), lambda i: (i, 0))
    once = pl.Buffered(1)
    return pl.pallas_call(
        body, name="ffn_fwd", grid=(ni,),
        in_specs=[tok, pl.BlockSpec((8, D), lambda i: (0, 0)),
                  pl.BlockSpec((2, D, cs), lambda i: (0, 0, 0), pipeline_mode=once),
                  pl.BlockSpec((2, D, cs), lambda i: (1, 0, 0), pipeline_mode=once),
                  pl.BlockSpec((F, D), lambda i: (0, 0), pipeline_mode=once)] + ride["in_specs"],
        out_specs=[tok, tok, pl.BlockSpec((2, tm, F), lambda i: (0, i, 0)), tok] + ride["out_specs"],
        out_shape=[jax.ShapeDtypeStruct((S, D), F32), jax.ShapeDtypeStruct((S, D), BF),
                   jax.ShapeDtypeStruct((2, S, F), BF), jax.ShapeDtypeStruct((S, D), BF)] + ride["out_shape"],
        input_output_aliases=ride["aliases"], scratch_shapes=ride["scratch"],
        compiler_params=_cparams(1),
    )(x, vec, w_in4, w_in4, w_out, *ride["arrays"])


def _norm_bwd_tile(dxo, dh, xv, y, vec_ref, red_ref, coef):
    g = vec_ref[0:1, :]
    sc = vec_ref[1:2, :]
    r = _rstd(xv)
    xh = xv * r
    dhn = dh * (1.0 + sc)
    dxh = dhn * g
    red_ref[0:1, :] += jnp.sum(dhn * xh, axis=0, keepdims=True)
    red_ref[1:2, :] += jnp.sum(dh * (xh * g), axis=0, keepdims=True)
    red_ref[2:3, :] += jnp.sum(dh, axis=0, keepdims=True)
    if y is not None:
        red_ref[3:4, :] += coef * jnp.sum(dxo * y.astype(F32), axis=0, keepdims=True)
    return dxo + r * (dxh - xh * jnp.mean(dxh * xh, axis=-1, keepdims=True))


def _ffn_dgrad(dxo, vec, ab, w_in4, w_out, x, y, comm=None):
    S, D = dxo.shape
    cs = w_in4.shape[2]
    F = 2 * cs
    tm = _tile(S, TM_DGRAD)
    ni = S // tm
    ride = _ride(comm, 8, 5)
    nc = ride["n"]

    def body(*refs):
        dxo_ref, vec_ref, ab_ref, wa_ref, wb_ref, wo_ref, x_ref, y_ref = refs[:8]
        dy_ref, u_ref, dab_ref, dx_ref, red_ref = refs[8 + nc:13 + nc]
        i = pl.program_id(0)
        riding = (comm, refs[8:8 + nc], refs[13 + nc:13 + 2 * nc], refs[13 + 2 * nc:])
        _ride_when(*riding, i == 0, _comm_start)

        @pl.when(i == 0)
        def _():
            red_ref[...] = jnp.zeros_like(red_ref)

        dy = ((0.5 * (1.0 + vec_ref[3:4, :])) * dxo_ref[...]).astype(BF)
        dy_ref[...] = dy
        dh = None
        for c in range(2):
            cols = slice(c * cs, (c + 1) * cs)
            du = _dot_nt(dy, wo_ref[cols, :])
            a = ab_ref[0, :, cols].astype(F32)
            b = ab_ref[1, :, cols].astype(F32)
            sg, s = _silu_parts(a)
            u_ref[:, cols] = (s * b).astype(BF)
            da = (du * b * (sg * (1.0 + a * (1.0 - sg)))).astype(BF)
            db = (du * s).astype(BF)
            dab_ref[0, :, cols] = da
            dab_ref[1, :, cols] = db
            part = _dot_nt(da, wa_ref[c]) + _dot_nt(db, wb_ref[c])
            dh = part if dh is None else dh + part
        dx_ref[...] = _norm_bwd_tile(dxo_ref[...], dh, x_ref[...], y_ref[...], vec_ref, red_ref, 0.5)
        _ride_when(*riding, i == ni - 1, _comm_wait)

    tok = pl.BlockSpec((tm, D), lambda i: (i, 0))
    small = pl.BlockSpec((8, D), lambda i: (0, 0))
    once = pl.Buffered(1)
    return pl.pallas_call(
        body, name="ffn_dgrad", grid=(ni,),
        in_specs=[tok, small, pl.BlockSpec((2, tm, F), lambda i: (0, i, 0)),
                  pl.BlockSpec((2, D, cs), lambda i: (0, 0, 0), pipeline_mode=once),
                  pl.BlockSpec((2, D, cs), lambda i: (1, 0, 0), pipeline_mode=once),
                  pl.BlockSpec((F, D), lambda i: (0, 0), pipeline_mode=once), tok, tok] + ride["in_specs"],
        out_specs=[tok, pl.BlockSpec((tm, F), lambda i: (i, 0)),
                   pl.BlockSpec((2, tm, F), lambda i: (0, i, 0)), tok, small] + ride["out_specs"],
        out_shape=[jax.ShapeDtypeStruct((S, D), BF), jax.ShapeDtypeStruct((S, F), BF), jax.ShapeDtypeStruct((2, S, F), BF),
                   jax.ShapeDtypeStruct((S, D), F32), jax.ShapeDtypeStruct((8, D), F32)] + ride["out_shape"],
        input_output_aliases=ride["aliases"], scratch_shapes=ride["scratch"],
        compiler_params=_cparams(1),
    )(dxo, vec, ab, w_in4, w_in4, w_out, x, y, *ride["arrays"])


def _mm_tn(a, b3, cs, comm=None):
    K, M = a.shape
    G, _, Nb = b3.shape
    N = G * Nb
    if cs >= Nb or Nb % cs:
        ns, tn = 1, math.gcd(cs, Nb)
    else:
        ns = max(s for s in range(1, Nb // cs + 1) if (Nb // cs) % s == 0 and s * cs <= max(WGRAD_COLS, cs))
        tn = ns * cs
    tmo = M if M <= 1024 else M // 2
    tk = _tile(K, TK)
    gm, gn, gk = M // tmo, N // tn, K // tk
    ride = _ride(comm, 2, 1)
    nc = ride["n"]

    def body(*refs):
        a_ref, b_ref = refs[:2]
        o_ref, acc_ref = refs[2 + nc], refs[-1]
        at = [pl.program_id(t) for t in range(3)]
        riding = (comm, refs[2:2 + nc], refs[3 + nc:3 + 2 * nc], refs[3 + 2 * nc:-1])
        _ride_when(*riding, functools.reduce(jnp.logical_and, [t == 0 for t in at]), _comm_start)

        @pl.when(at[2] == 0)
        def _():
            acc_ref[...] = jnp.zeros_like(acc_ref)

        av = a_ref[...].astype(BF)
        if ns == 1:
            acc_ref[...] += _dot_tn(av, b_ref[...].astype(BF))
        else:
            for s in range(ns):
                acc_ref[s] += _dot_tn(av, b_ref[:, s * cs:(s + 1) * cs].astype(BF))

        @pl.when(at[2] == gk - 1)
        def _():
            o_ref[...] = acc_ref[...].astype(BF)

        _ride_when(*riding, functools.reduce(jnp.logical_and, [t == g - 1 for t, g in zip(at, (gm, gn, gk))]), _comm_wait)

    if ns == 1:
        out_spec = pl.BlockSpec((None, tmo, tn), lambda m, n, k: ((n * tn) // cs, m, ((n * tn) % cs) // tn))
        acc = pltpu.VMEM((tmo, tn), F32)
    else:
        out_spec = pl.BlockSpec((ns, tmo, cs), lambda m, n, k: (n, m, 0))
        acc = pltpu.VMEM((ns, tmo, cs), F32)
    g, *moved = pl.pallas_call(
        body, name="wgrad_tn", grid=(gm, gn, gk),
        in_specs=[pl.BlockSpec((tk, tmo), lambda m, n, k: (k, m)),
                  pl.BlockSpec((None, tk, tn), lambda m, n, k: ((n * tn) // Nb, k, ((n * tn) % Nb) // tn))] + ride["in_specs"],
        out_specs=[out_spec] + ride["out_specs"],
        out_shape=[jax.ShapeDtypeStruct((N // cs, M, cs), BF)] + ride["out_shape"],
        input_output_aliases=ride["aliases"], scratch_shapes=ride["scratch"] + [acc],
        compiler_params=_cparams(3),
    )(a, b3, *ride["arrays"])
    return (g, moved) if comm else g


def _conv_fwd(x, vec, cw, w_in4, w_out, comm=None):
    S, D = x.shape
    cs = w_in4.shape[2]
    tm = _tile(S, TM)
    nt = S // tm
    ride = _ride(comm, 5, 5)
    nc = ride["n"]

    def body(*refs):
        x_ref, vec_ref, cw_ref, wi_ref, wo_ref = refs[:5]
        xo_ref, h_ref, s4_ref, z_ref, y_ref = refs[5 + nc:10 + nc]
        vs_ref = refs[-1]
        riding = (comm, refs[5:5 + nc], refs[10 + nc:10 + 2 * nc], refs[10 + 2 * nc:-1])
        _ride_when(*riding, pl.program_id(0) == 0, _comm_start)

        @pl.when(pl.program_id(0) == 0)
        def _():
            vs_ref[0:8, :] = jnp.zeros((8, D), F32)

        xv = x_ref[...]
        h = _norm_mod(xv, vec_ref[...]).astype(BF)
        h_ref[...] = h
        bcu = jnp.concatenate([_dot(h, wi_ref[q]) for q in range(4)], axis=1)
        bg, cg, u = bcu[:, :D], bcu[:, D:2 * D], bcu[:, 2 * D:]
        v = cg * u
        vs_ref[8:8 + tm, :] = v
        conv = cw_ref[0:1, :] * vs_ref[pl.ds(6, tm), :] + cw_ref[1:2, :] * vs_ref[pl.ds(7, tm), :] + cw_ref[2:3, :] * v
        vs_ref[0:8, :] = vs_ref[tm:tm + 8, :]
        z = (bg * conv).astype(BF)
        s4_ref[0] = bg.astype(BF)
        s4_ref[1] = cg.astype(BF)
        s4_ref[2] = u.astype(BF)
        s4_ref[3] = conv.astype(BF)
        z_ref[...] = z
        y = _dot(z, wo_ref[...])
        y_ref[...] = y.astype(BF)
        xo_ref[...] = xv + (1.0 + vec_ref[3:4, :]) * y
        _ride_when(*riding, pl.program_id(0) == nt - 1, _comm_wait)

    tok = pl.BlockSpec((tm, D), lambda i: (i, 0))
    small = pl.BlockSpec((8, D), lambda i: (0, 0))
    return pl.pallas_call(
        body, name="conv_fwd", grid=(nt,),
        in_specs=[tok, small, small, pl.BlockSpec((4, D, cs), lambda i: (0, 0, 0)),
                  pl.BlockSpec((D, D), lambda i: (0, 0))] + ride["in_specs"],
        out_specs=[tok, tok, pl.BlockSpec((4, tm, D), lambda i: (0, i, 0)), tok, tok] + ride["out_specs"],
        out_shape=[jax.ShapeDtypeStruct((S, D), F32), jax.ShapeDtypeStruct((S, D), BF),
                   jax.ShapeDtypeStruct((4, S, D), BF), jax.ShapeDtypeStruct((S, D), BF),
                   jax.ShapeDtypeStruct((S, D), BF)] + ride["out_shape"],
        input_output_aliases=ride["aliases"], scratch_shapes=ride["scratch"] + [pltpu.VMEM((tm + 8, D), F32)],
        compiler_params=_cparams(1),
    )(x, vec, cw, w_in4, w_out, *ride["arrays"])


def _conv_dgrad(dxo, vec, cw, s4, w_in4, w_out, x, y):
    S, D = dxo.shape
    cs = w_in4.shape[2]
    tm = _tile(S, TM)
    nt = S // tm

    def body(dxo_ref, vec_ref, cw_ref, s4_ref, wi_ref, wo_ref, x_ref, y_ref, dy_ref, d3_ref, dx_ref, red_ref, dcw_ref, ds_ref):
        @pl.when(pl.program_id(0) == 0)
        def _():
            ds_ref[tm:tm + 8, :] = jnp.zeros((8, D), F32)
            dcw_ref[...] = jnp.zeros_like(dcw_ref)
            red_ref[...] = jnp.zeros_like(red_ref)

        dy = ((1.0 + vec_ref[3:4, :]) * dxo_ref[...]).astype(BF)
        dy_ref[...] = dy
        dz = _dot_nt(dy, wo_ref[...])
        bg = s4_ref[0].astype(F32)
        cg = s4_ref[1].astype(F32)
        u = s4_ref[2].astype(F32)
        conv = s4_ref[3].astype(F32)
        dbg = dz * conv
        dconv = dz * bg
        ds_ref[0:tm, :] = dconv
        d1 = ds_ref[pl.ds(1, tm), :]
        d2 = ds_ref[pl.ds(2, tm), :]
        ds_ref[tm:tm + 8, :] = ds_ref[0:8, :]
        dv = cw_ref[2:3, :] * dconv + cw_ref[1:2, :] * d1 + cw_ref[0:1, :] * d2
        v = cg * u
        dcw_ref[0:1, :] += jnp.sum(d2 * v, axis=0, keepdims=True)
        dcw_ref[1:2, :] += jnp.sum(d1 * v, axis=0, keepdims=True)
        dcw_ref[2:3, :] += jnp.sum(dconv * v, axis=0, keepdims=True)
        dbcu = jnp.concatenate([dbg, dv * u, dv * cg], axis=1).astype(BF)
        d3_ref[...] = dbcu
        dh = _dot_nt(dbcu[:, 0:cs], wi_ref[0])
        for q in range(1, 4):
            dh = dh + _dot_nt(dbcu[:, q * cs:(q + 1) * cs], wi_ref[q])
        dx_ref[...] = _norm_bwd_tile(dxo_ref[...], dh, x_ref[...], y_ref[...], vec_ref, red_ref, 1.0)

    tok = pl.BlockSpec((tm, D), lambda i: (nt - 1 - i, 0))
    small = pl.BlockSpec((8, D), lambda i: (0, 0))
    once = pl.Buffered(1)
    return pl.pallas_call(
        body, name="conv_dgrad", grid=(nt,),
        in_specs=[tok, small, small, pl.BlockSpec((4, tm, D), lambda i: (0, nt - 1 - i, 0)),
                  pl.BlockSpec((4, D, cs), lambda i: (0, 0, 0), pipeline_mode=once),
                  pl.BlockSpec((D, D), lambda i: (0, 0), pipeline_mode=once), tok, tok],
        out_specs=[tok, pl.BlockSpec((tm, 3 * D), lambda i: (nt - 1 - i, 0)), tok, small, small],
        out_shape=[jax.ShapeDtypeStruct((S, D), BF), jax.ShapeDtypeStruct((S, 3 * D), BF),
                   jax.ShapeDtypeStruct((S, D), F32), jax.ShapeDtypeStruct((8, D), F32), jax.ShapeDtypeStruct((8, D), F32)],
        scratch_shapes=[pltpu.VMEM((tm + 8, D), F32)],
        compiler_params=_cparams(1),
    )(dxo, vec, cw, s4, w_in4, w_out, x, y)


def _rope_tables(positions):
    S = positions.shape[-1]
    inv = ROPE_THETA ** (-jnp.arange(0, ROPE_DIM, 2, dtype=F32) / ROPE_DIM)
    ang = positions.reshape(S, 1).astype(F32) * inv
    cos = jnp.tile(jnp.cos(ang), (1, LANES // 8))
    sin = jnp.tile(jnp.sin(ang), (1, LANES // 8))
    l64 = jnp.arange(LANES) % HEAD_DIM
    return jnp.stack([jnp.where(l64 < ROPE_DIM, cos, 1.0),
                      jnp.where(l64 < ROPE_DIM // 2, -sin, 0.0),
                      jnp.where((l64 >= ROPE_DIM // 2) & (l64 < ROPE_DIM), sin, 0.0)])


def _rope(t, tab_ref):
    return t * tab_ref[0] + pltpu.roll(t, LANES - 8, 1) * tab_ref[1] + pltpu.roll(t, 8, 1) * tab_ref[2]


def _rope_t(d, tab_ref):
    return d * tab_ref[0] + pltpu.roll(d * tab_ref[1], 8, 1) + pltpu.roll(d * tab_ref[2], LANES - 8, 1)


GROUP_CH = GROUP_W // LANES


def _to_dilated(src_ref, c0, dst_ref, d):
    n = src_ref.shape[1]
    for r in range(d):
        for ch in range(GROUP_CH):
            rows = src_ref[c0 + ch] if d == 1 else src_ref.at[c0 + ch][pl.ds(r, n // d, stride=d), :]
            dst_ref[:, r * GROUP_W + ch * LANES:r * GROUP_W + (ch + 1) * LANES] = rows.astype(dst_ref.dtype)


def _from_dilated(val, dst_ref, c0, d):
    n = dst_ref.shape[1]
    for r in range(d):
        for ch in range(GROUP_CH):
            cols = val[:, r * GROUP_W + ch * LANES:r * GROUP_W + (ch + 1) * LANES]
            if d == 1:
                dst_ref[c0 + ch] = cols
            else:
                dst_ref.at[c0 + ch][pl.ds(r, n // d, stride=d), :] = cols


def _group(ref, c0):
    return jnp.concatenate([ref[c0 + ch] for ch in range(GROUP_CH)], axis=1)


def _put_group(ref, c0, val):
    for ch in range(GROUP_CH):
        ref[c0 + ch] = val[:, ch * LANES:(ch + 1) * LANES]


def _dil_spec(tm, d):
    return pl.BlockSpec((tm // d, d * GROUP_W), lambda i: (i, 0))


def _proj_fwd(x, vec, w4, tabs, n_rope, dils, scale=1.0):
    S, D = x.shape
    cs = w4.shape[2]
    N = 4 * cs
    tm = _tile(S, TM)

    def body(x_ref, vec_ref, w_ref, tab_ref, h_ref, *rest):
        outs, acc_ref = rest[:-1], rest[-1]
        h = _norm_mod(x_ref[...], vec_ref[...]).astype(BF)
        h_ref[...] = h
        per = cs // LANES
        for q in range(4):
            acc = _dot(h, w_ref[q])
            if scale != 1.0:
                acc = acc * scale
            cols = [acc[:, ch * LANES:(ch + 1) * LANES] for ch in range(per)]
            for ch, t in enumerate(cols):
                acc_ref[q * per + ch] = _rope(t, tab_ref) if q * per + ch < n_rope else t
        for j, d in enumerate(dils):
            _to_dilated(acc_ref, j * GROUP_CH, outs[j], d)

    h, *outs = pl.pallas_call(
        body, name="proj_fwd", grid=(S // tm,),
        in_specs=[pl.BlockSpec((tm, D), lambda i: (i, 0)), pl.BlockSpec((8, D), lambda i: (0, 0)),
                  pl.BlockSpec((4, D, cs), lambda i: (0, 0, 0)), pl.BlockSpec((3, tm, LANES), lambda i: (0, i, 0))],
        out_specs=[pl.BlockSpec((tm, D), lambda i: (i, 0))] + [_dil_spec(tm, d) for d in dils],
        out_shape=[jax.ShapeDtypeStruct((S, D), BF)] + [jax.ShapeDtypeStruct((S // d, d * GROUP_W), BF) for d in dils],
        scratch_shapes=[pltpu.VMEM((N // LANES, tm, LANES), F32)],
        compiler_params=_cparams(1),
    )(x, vec, w4, tabs)
    return h, outs


def _proj_dgrad(parts, tabs, w4, n_rope_groups, dils, dxo, x, y, vec):
    cs, D = w4.shape[2], w4.shape[1]
    N = 4 * cs
    S = parts[0][0].shape[0] * dils[0]
    tm = _tile(S, TM)
    flat = [p for grp in parts for p in grp]
    counts = [len(grp) for grp in parts]
    has_y = y is not None

    def body(*refs):
        prefs = refs[:len(flat)]
        tab_ref, w_ref, dxo_ref, x_ref = refs[len(flat):len(flat) + 4]
        y_ref = refs[len(flat) + 4] if has_y else None
        vec_ref, dz_ref, dx_ref, red_ref, z_ref = refs[-5:]

        @pl.when(pl.program_id(0) == 0)
        def _():
            red_ref[...] = jnp.zeros_like(red_ref)

        k = 0
        for j, cnt in enumerate(counts):
            z = prefs[k][...]
            for r in prefs[k + 1:k + cnt]:
                z = z + r[...]
            k += cnt
            _from_dilated(z, z_ref, 0, dils[j])
            if j < n_rope_groups:
                z = jnp.concatenate([_rope_t(z_ref[ch], tab_ref) for ch in range(GROUP_CH)], axis=1)
            else:
                z = _group(z_ref, 0)
            dz_ref[:, j * GROUP_W:(j + 1) * GROUP_W] = z.astype(BF)
        dh = _dot_nt(dz_ref[:, 0:cs], w_ref[0])
        for q in range(1, 4):
            dh = dh + _dot_nt(dz_ref[:, q * cs:(q + 1) * cs], w_ref[q])
        dx_ref[...] = _norm_bwd_tile(dxo_ref[...], dh, x_ref[...], y_ref[...] if has_y else None, vec_ref, red_ref, 1.0)

    tok = pl.BlockSpec((tm, D), lambda i: (i, 0))
    small = pl.BlockSpec((8, D), lambda i: (0, 0))
    return pl.pallas_call(
        body, name="proj_dgrad", grid=(S // tm,),
        in_specs=[_dil_spec(tm, d) for d, cnt in zip(dils, counts) for _ in range(cnt)]
        + [pl.BlockSpec((3, tm, LANES), lambda i: (0, i, 0)), pl.BlockSpec((4, D, cs), lambda i: (0, 0, 0)), tok, tok]
        + ([tok] if has_y else []) + [small],
        out_specs=[pl.BlockSpec((tm, N), lambda i: (i, 0)), tok, small],
        out_shape=[jax.ShapeDtypeStruct((S, N), BF), jax.ShapeDtypeStruct((S, D), F32), jax.ShapeDtypeStruct((8, D), F32)],
        scratch_shapes=[pltpu.VMEM((GROUP_CH, tm, LANES), F32)],
        compiler_params=_cparams(1),
    )(*flat, tabs, w4, dxo, x, *([y] if has_y else []), vec)


def _band_masks():
    qi = lax.broadcasted_iota(jnp.int32, (BAND, BAND), 0)
    kj = lax.broadcasted_iota(jnp.int32, (BAND, BAND), 1)
    return kj <= qi, kj >= qi


def _head(ref, h):
    return ref[:, h * HEAD_DIM:(h + 1) * HEAD_DIM]


def _attn_fwd(q, k, v, d):
    sd = q.shape[0]
    nb = sd // BAND

    def body(q_ref, kp_ref, kc_ref, vp_ref, vc_ref, o_ref, l_ref):
        same, prev = _band_masks()
        prev = jnp.logical_and(prev, pl.program_id(1) > 0)
        heads = range(HEADS_PER_GROUP)
        scores = [(jnp.where(same, _dot_nt(_head(q_ref, h), _head(kc_ref, h)), NEG),
                   jnp.where(prev, _dot_nt(_head(q_ref, h), _head(kp_ref, h)), NEG)) for h in heads]
        probs = []
        for sc, sp in scores:
            m = jnp.maximum(jnp.max(sc, axis=-1, keepdims=True), jnp.max(sp, axis=-1, keepdims=True))
            pc = jnp.exp(sc - m)
            pp = jnp.exp(sp - m)
            den = jnp.sum(pc, axis=-1, keepdims=True) + jnp.sum(pp, axis=-1, keepdims=True)
            probs.append((pc.astype(BF), pp.astype(BF), m, den))
        outs = [(_dot(pc, _head(vc_ref, h)) + _dot(pp, _head(vp_ref, h))) / den for h, (pc, pp, _, den) in zip(heads, probs)]
        o_ref[...] = jnp.concatenate(outs, axis=1)
        l_ref[...] = jnp.concatenate([jnp.broadcast_to(m + jnp.log(den), (BAND, HEAD_DIM)) for _, _, m, den in probs], axis=1)

    blk = (BAND, GROUP_W)
    cur = pl.BlockSpec(blk, lambda r, i: (i, r))
    prv = pl.BlockSpec(blk, lambda r, i: (jnp.maximum(i - 1, 0), r))
    return pl.pallas_call(
        body, name=f"attn_fwd_d{d}", grid=(d, nb),
        in_specs=[cur, prv, cur, prv, cur], out_specs=[cur, cur],
        out_shape=[jax.ShapeDtypeStruct((sd, d * GROUP_W), F32)] * 2,
        compiler_params=_cparams(2),
    )(q, k, k, v, v)


def _attn_out(os, ls, x, vec, wo4):
    S, D = x.shape
    cs = wo4.shape[2]
    tm = _tile(S, TM)
    ng = len(DILATIONS)

    def body(*refs):
        o_refs, l_refs = refs[0:2 * ng:2], refs[1:2 * ng:2]
        x_ref, vec_ref, wo_ref = refs[2 * ng:2 * ng + 3]
        outs = refs[2 * ng + 3:]
        mix_refs, lj_refs = outs[0:2 * ng:2], outs[1:2 * ng:2]
        y_ref, xo_ref, nat_ref = outs[2 * ng:]
        ov, lv = [], []
        for g in range(ng):
            _from_dilated(o_refs[g][...], nat_ref, 2 * g * GROUP_CH, DILATIONS[g])
            _from_dilated(l_refs[g][...], nat_ref, (2 * g + 1) * GROUP_CH, DILATIONS[g])
            ov.append(_group(nat_ref, 2 * g * GROUP_CH))
            lv.append(_group(nat_ref, (2 * g + 1) * GROUP_CH))
        mx = jnp.maximum(jnp.maximum(lv[0], lv[1]), lv[2])
        es = [jnp.exp(t - mx) for t in lv]
        den = es[0] + es[1] + es[2]
        mix = (es[0] * ov[0] + es[1] * ov[1] + es[2] * ov[2]) / den
        at_mix, at_lj = 2 * ng * GROUP_CH, (2 * ng + 1) * GROUP_CH
        _put_group(nat_ref, at_mix, mix)
        _put_group(nat_ref, at_lj, mx + jnp.log(den))
        for g in range(ng):
            _to_dilated(nat_ref, at_mix, mix_refs[g], DILATIONS[g])
            _to_dilated(nat_ref, at_lj, lj_refs[g], DILATIONS[g])
        mb = mix.astype(BF)
        y = jnp.concatenate([_dot(mb, wo_ref[q]) for q in range(4)], axis=1)
        y_ref[...] = y.astype(BF)
        xo_ref[...] = x_ref[...] + (1.0 + vec_ref[3:4, :]) * y

    tok = pl.BlockSpec((tm, D), lambda i: (i, 0))
    dil = [_dil_spec(tm, d) for d in DILATIONS for _ in range(2)]
    dil_shape = [jax.ShapeDtypeStruct((S // d, d * GROUP_W), F32) for d in DILATIONS for _ in range(2)]
    outs = pl.pallas_call(
        body, name="attn_out", grid=(S // tm,),
        in_specs=dil + [tok, pl.BlockSpec((8, D), lambda i: (0, 0)), pl.BlockSpec((4, GROUP_W, cs), lambda i: (0, 0, 0))],
        out_specs=dil + [tok, tok],
        out_shape=dil_shape + [jax.ShapeDtypeStruct((S, D), BF), jax.ShapeDtypeStruct((S, D), F32)],
        scratch_shapes=[pltpu.VMEM(((2 * ng + 2) * GROUP_CH, tm, LANES), F32)],
        compiler_params=_cparams(1),
    )(*[t for pair in zip(os, ls) for t in pair], x, vec, wo4)
    return list(outs[0:2 * ng:2]), list(outs[1:2 * ng:2]), outs[2 * ng], outs[2 * ng + 1]


def _attn_out_dgrad(dxo, vec, wo4):
    S, D = dxo.shape
    cs = wo4.shape[2]
    tm = _tile(S, TM)

    def body(dxo_ref, vec_ref, wo_ref, dy_ref, *rest):
        dm_refs, nat_ref = rest[:-1], rest[-1]
        dy = ((1.0 + vec_ref[3:4, :]) * dxo_ref[...]).astype(BF)
        dy_ref[...] = dy
        dm = _dot_nt(dy[:, 0:cs], wo_ref[0])
        for q in range(1, 4):
            dm = dm + _dot_nt(dy[:, q * cs:(q + 1) * cs], wo_ref[q])
        _put_group(nat_ref, 0, dm)
        for g, d in enumerate(DILATIONS):
            _to_dilated(nat_ref, 0, dm_refs[g], d)

    tok = pl.BlockSpec((tm, D), lambda i: (i, 0))
    dy, *dms = pl.pallas_call(
        body, name="attn_out_dgrad", grid=(S // tm,),
        in_specs=[tok, pl.BlockSpec((8, D), lambda i: (0, 0)), pl.BlockSpec((4, GROUP_W, cs), lambda i: (0, 0, 0))],
        out_specs=[tok] + [_dil_spec(tm, d) for d in DILATIONS],
        out_shape=[jax.ShapeDtypeStruct((S, D), BF)] + [jax.ShapeDtypeStruct((S // d, d * GROUP_W), F32) for d in DILATIONS],
        scratch_shapes=[pltpu.VMEM((GROUP_CH, tm, LANES), F32)],
        compiler_params=_cparams(1),
    )(dxo, vec, wo4)
    return dy, dms


def _attn_bwd(q, k, v, dmix, mix, lj, d):
    sd = q.shape[0]
    nb = sd // BAND

    def body(q_ref, kp_ref, kc_ref, vp_ref, vc_ref, do_ref, o_ref, l_ref, dq_ref, dk_ref, dv_ref, keep_k, keep_v):
        i = pl.program_id(1)

        @pl.when(i == 0)
        def _():
            keep_k[...] = jnp.zeros_like(keep_k)
            keep_v[...] = jnp.zeros_like(keep_v)

        @pl.when(i < nb)
        def _():
            same, prev = _band_masks()
            prev = jnp.logical_and(prev, i > 0)
            heads = range(HEADS_PER_GROUP)
            dob = [_head(do_ref, h).astype(BF) for h in heads]
            raw = [(_dot_nt(_head(q_ref, h), _head(kc_ref, h)), _dot_nt(_head(q_ref, h), _head(kp_ref, h)),
                    _dot_nt(dob[h], _head(vc_ref, h)), _dot_nt(dob[h], _head(vp_ref, h))) for h in heads]
            mid = []
            for h, (sc, sp, dpc, dpp) in zip(heads, raw):
                lrow = l_ref[:, h * HEAD_DIM:h * HEAD_DIM + 1]
                delta = jnp.sum(_head(do_ref, h) * _head(o_ref, h), axis=-1, keepdims=True)
                pc = jnp.exp(jnp.where(same, sc, NEG) - lrow)
                pp = jnp.exp(jnp.where(prev, sp, NEG) - lrow)
                mid.append((pc.astype(BF), pp.astype(BF), (pc * (dpc - delta)).astype(BF), (pp * (dpp - delta)).astype(BF)))
            dq = [(_dot(dsc, _head(kc_ref, h)) + _dot(dsp, _head(kp_ref, h))) * SM_SCALE for h, (_, _, dsc, dsp) in zip(heads, mid)]
            dk_prev = [_dot_tn(dsp, _head(q_ref, h)) for h, (_, _, _, dsp) in zip(heads, mid)]
            dv_prev = [_dot_tn(pp, dob[h]) for h, (_, pp, _, _) in zip(heads, mid)]
            dk_same = [_dot_tn(dsc, _head(q_ref, h)) for h, (_, _, dsc, _) in zip(heads, mid)]
            dv_same = [_dot_tn(pc, dob[h]) for h, (pc, _, _, _) in zip(heads, mid)]
            dq_ref[...] = jnp.concatenate(dq, axis=1)
            dk_ref[...] = keep_k[...] + jnp.concatenate(dk_prev, axis=1)
            dv_ref[...] = keep_v[...] + jnp.concatenate(dv_prev, axis=1)
            keep_k[...] = jnp.concatenate(dk_same, axis=1)
            keep_v[...] = jnp.concatenate(dv_same, axis=1)

        @pl.when(i == nb)
        def _():
            dk_ref[...] = keep_k[...]
            dv_ref[...] = keep_v[...]

    blk = (BAND, GROUP_W)
    cur = pl.BlockSpec(blk, lambda r, i: (jnp.minimum(i, nb - 1), r))
    prv = pl.BlockSpec(blk, lambda r, i: (jnp.clip(i - 1, 0, nb - 1), r))
    return pl.pallas_call(
        body, name=f"attn_bwd_d{d}", grid=(d, nb + 1),
        in_specs=[cur, prv, cur, prv, cur, cur, cur, cur], out_specs=[cur, prv, prv],
        out_shape=[jax.ShapeDtypeStruct((sd, d * GROUP_W), F32)] * 3,
        scratch_shapes=[pltpu.VMEM(blk, F32), pltpu.VMEM(blk, F32)],
        compiler_params=_cparams(2),
    )(q, k, k, v, v, dmix, mix, lj)


def _final_loss(x, gvec, tgt):
    S, D = x.shape
    tm = _tile(S, TM)

    def body(x_ref, g_ref, t_ref, dx_ref, red_ref):
        @pl.when(pl.program_id(0) == 0)
        def _():
            red_ref[...] = jnp.zeros_like(red_ref)

        xv = x_ref[...]
        g = g_ref[0:1, :]
        r = _rstd(xv)
        xh = xv * r
        err = xh * g - t_ref[...]
        dy = err * (1.0 / D)
        dxh = dy * g
        dx_ref[...] = r * (dxh - xh * jnp.mean(dxh * xh, axis=-1, keepdims=True))
        red_ref[0:1, :] += jnp.sum(dy * xh, axis=0, keepdims=True)
        red_ref[1:2, :] += jnp.sum(err * err, axis=0, keepdims=True)

    tok = pl.BlockSpec((tm, D), lambda i: (i, 0))
    small = pl.BlockSpec((8, D), lambda i: (0, 0))
    return pl.pallas_call(
        body, name="final_loss", grid=(S // tm,),
        in_specs=[tok, small, tok], out_specs=[tok, small],
        out_shape=[jax.ShapeDtypeStruct((S, D), F32), jax.ShapeDtypeStruct((8, D), F32)],
        compiler_params=_cparams(1),
    )(x, gvec, tgt)


def _adamw(w, g, m, v, copy_g=False):
    shape = w.shape
    C = shape[-1]
    R = w.size // C
    tr = 256 if R % 256 == 0 else R

    def body(w_ref, g_ref, m_ref, v_ref, d_ref, nm_ref, nv_ref, *g_out):
        gv = g_ref[...]
        nm = ADAM_B1 * m_ref[...] + (1.0 - ADAM_B1) * gv
        nv = ADAM_B2 * v_ref[...] + (1.0 - ADAM_B2) * (gv * gv)
        m_hat = nm / (1.0 - ADAM_B1 ** ADAM_STEP)
        v_hat = nv / (1.0 - ADAM_B2 ** ADAM_STEP)
        d_ref[...] = -ADAM_LR * (m_hat / (jnp.sqrt(v_hat) + ADAM_EPS) + ADAM_WD * w_ref[...])
        nm_ref[...] = nm
        nv_ref[...] = nv
        if copy_g:
            g_out[0][...] = gv

    spec = pl.BlockSpec((tr, C), lambda i: (i, 0))
    n_out = 4 if copy_g else 3
    outs = pl.pallas_call(
        body, name="adamw", grid=(R // tr,),
        in_specs=[spec] * 4, out_specs=[spec] * n_out,
        out_shape=[jax.ShapeDtypeStruct((R, C), F32)] * n_out,
        compiler_params=_cparams(1),
    )(*(t.reshape(R, C) for t in (w, g, m, v)))
    return tuple(o.reshape(shape) for o in (*outs[:3], outs[3] if copy_g else g))


def _ada_fwd(c_all, w, b):
    L, D, N = w.shape
    tn = 768 if N % 768 == 0 else _tile(N, 512)

    def body(c_ref, w_ref, b_ref, o_ref):
        cv = c_ref[...]
        cond = (cv * jax.nn.sigmoid(cv)).astype(BF)
        o_ref[...] = _dot(cond, w_ref[...].astype(BF)) + b_ref[...]

    return pl.pallas_call(
        body, name="ada_fwd", grid=(L, N // tn),
        in_specs=[pl.BlockSpec((8, D), lambda l, n: (0, 0)), pl.BlockSpec((None, D, tn), lambda l, n: (l, 0, n)),
                  pl.BlockSpec((None, 1, tn), lambda l, n: (l, 0, n))],
        out_specs=pl.BlockSpec((None, 8, tn), lambda l, n: (l, 0, n)),
        out_shape=jax.ShapeDtypeStruct((L, 8, N), F32),
        compiler_params=_cparams(2),
    )(c_all, w, b)


def _ada_wgrad(c_all_t, dm):
    D = c_all_t.shape[0]
    L, _, N = dm.shape
    tn = 256

    def body(c_ref, dm_ref, o_ref):
        cv = c_ref[...]
        cond = cv * jax.nn.sigmoid(cv)
        acc = cond[:, 0:1] * dm_ref[0:1, :]
        for b in range(1, 8):
            acc = acc + cond[:, b:b + 1] * dm_ref[b:b + 1, :]
        o_ref[...] = acc

    return pl.pallas_call(
        body, name="ada_wgrad", grid=(L, N // tn),
        in_specs=[pl.BlockSpec((D, 8), lambda l, n: (0, 0)), pl.BlockSpec((None, 8, tn), lambda l, n: (l, 0, n))],
        out_specs=pl.BlockSpec((None, D, tn), lambda l, n: (l, 0, n)),
        out_shape=jax.ShapeDtypeStruct((L, D, N), F32),
        compiler_params=_cparams(2),
    )(c_all_t, dm)


def _sum8(g):
    _, R, C = g.shape

    def body(g_ref, o_ref):
        acc = g_ref[0]
        for b in range(1, 8):
            acc = acc + g_ref[b]
        o_ref[...] = acc

    return pl.pallas_call(body, name="sum8", out_shape=jax.ShapeDtypeStruct((R, C), F32),
                          in_specs=[pl.BlockSpec(memory_space=pltpu.VMEM)],
                          out_specs=pl.BlockSpec(memory_space=pltpu.VMEM))(g)


def _pair_add(g4, recv, cidx):
    _, _, ha, B = g4.shape
    tr = _row_tile(ha)

    def body(c_ref, g_ref, r_ref, o_ref):
        o_ref[...] = (g_ref[...].astype(F32) + r_ref[...].astype(F32)).astype(BF)

    return pl.pallas_call(
        body, name="pair_add",
        grid_spec=pltpu.PrefetchScalarGridSpec(
            num_scalar_prefetch=1, grid=(4, ha // tr),
            in_specs=[pl.BlockSpec((None, None, tr, B), lambda s, i, c: (s, c[0], i, 0)),
                      pl.BlockSpec((None, tr, B), lambda s, i, c: (s, i, 0))],
            out_specs=pl.BlockSpec((None, tr, B), lambda s, i, c: (s, i, 0))),
        out_shape=jax.ShapeDtypeStruct((4, ha, B), BF),
        compiler_params=_cparams(2),
    )(cidx, g4, recv)


def _sum_partials(part, recv, gbuf, n_layers, l, place):
    _, ha, B = part.shape
    tr = _row_tile(ha, cap=256)

    def body(pc_ref, p_ref, r_ref, *rest):
        o_ref = rest[-1]
        o_ref[...] = ((p_ref[...].astype(F32) + r_ref[0].astype(F32)) + r_ref[1].astype(F32)) + r_ref[2].astype(F32)

    in_specs = [pl.BlockSpec((None, tr, B), lambda i, pc: (pc[0], i, 0)), pl.BlockSpec((3, tr, B), lambda i, pc: (0, i, 0))]
    ops = [part, recv]
    if gbuf is not None:
        in_specs.append(ANY)
        ops.append(gbuf)
    return pl.pallas_call(
        body, name="sum_partials",
        grid_spec=pltpu.PrefetchScalarGridSpec(
            num_scalar_prefetch=1, grid=(ha // tr,), in_specs=in_specs,
            out_specs=pl.BlockSpec((None, None, tr, B), lambda i, pc: (l, pc[1], i, 0))),
        out_shape=jax.ShapeDtypeStruct((n_layers, 2, ha, B), F32),
        input_output_aliases={} if gbuf is None else {3: 0},
        compiler_params=_cparams(1),
    )(place, *ops)


def _cast_shard(shard, l, place):
    _, _, ha, B = shard.shape
    tr = _row_tile(ha)

    def body(pc_ref, s_ref, o_ref):
        o_ref[...] = s_ref[...].astype(BF)

    return pl.pallas_call(
        body, name="cast_shard",
        grid_spec=pltpu.PrefetchScalarGridSpec(
            num_scalar_prefetch=1, grid=(2, ha // tr),
            in_specs=[pl.BlockSpec((None, None, tr, B), lambda h, i, pc: (l, h, i, 0))],
            out_specs=pl.BlockSpec((None, None, tr, B), lambda h, i, pc: (pc[0], h, i, 0))),
        out_shape=jax.ShapeDtypeStruct((4, 2, ha, B), BF),
        compiler_params=_cparams(2),
    )(place, shard)


def _gather8(v):
    R, C = v.shape

    def body(v_ref, o_ref, ssem, rsem):
        x, y, c, *_ = _place()
        me = 4 * x + 2 * y + c
        o_ref[me] = v_ref[...]
        cps = []
        for k in range(1, 8):
            fx, fy, fc = (k >> 2) & 1, (k >> 1) & 1, k & 1
            peer = (x ^ fx, y ^ fy, c ^ fc)
            cp = pltpu.make_async_remote_copy(src_ref=v_ref, dst_ref=o_ref.at[me], send_sem=ssem.at[k - 1],
                                              recv_sem=rsem.at[k - 1], device_id=peer, device_id_type=MESH)
            cp.start()
            cps.append((cp, 4 * peer[0] + 2 * peer[1] + peer[2]))
        for k, (cp, slot) in enumerate(cps):
            there = o_ref.at[slot]
            pltpu.make_async_remote_copy(src_ref=there, dst_ref=there, send_sem=ssem.at[k], recv_sem=rsem.at[k],
                                         device_id=(x, y, c), device_id_type=MESH).wait_recv()
        for cp, _ in cps:
            cp.wait_send()

    vm = pl.BlockSpec(memory_space=pltpu.VMEM)
    return pl.pallas_call(
        body, name="gather8", in_specs=[vm], out_specs=vm, out_shape=jax.ShapeDtypeStruct((8, R, C), F32),
        scratch_shapes=[pltpu.SemaphoreType.DMA((7,)), pltpu.SemaphoreType.DMA((7,))],
    )(v)


def _mods_to_owner(ms):
    _, R, C = ms.shape

    def body(m_ref, o_ref, ssem, rsem):
        x, y, c, p, chips, qs = _place()
        o_ref[p] = m_ref[2 * p + c]
        cps = []
        for j in range(3):
            cp = pltpu.make_async_remote_copy(src_ref=m_ref.at[2 * qs[j] + c], dst_ref=o_ref.at[p], send_sem=ssem.at[j],
                                              recv_sem=rsem.at[j], device_id=(*chips[j], c), device_id_type=MESH)
            cp.start()
            cps.append(cp)
        for j in range(3):
            there = o_ref.at[qs[j]]
            pltpu.make_async_remote_copy(src_ref=there, dst_ref=there, send_sem=ssem.at[j], recv_sem=rsem.at[j],
                                         device_id=(x, y, c), device_id_type=MESH).wait_recv()
        for cp in cps:
            cp.wait_send()

    vm = pl.BlockSpec(memory_space=pltpu.VMEM)
    return pl.pallas_call(
        body, name="mods_to_owner", in_specs=[vm], out_specs=vm, out_shape=jax.ShapeDtypeStruct((4, R, C), F32),
        scratch_shapes=[pltpu.SemaphoreType.DMA((3,)), pltpu.SemaphoreType.DMA((3,))],
    )(ms)


def _vec(*rows):
    D = rows[0].shape[-1]
    rows = [r.reshape(1, D) for r in rows]
    return jnp.concatenate(rows + [jnp.zeros((8 - len(rows), D), F32)], axis=0)


def _sets(n_layers, n_conv):
    sets = []
    for l in range(n_layers):
        j = l - n_conv
        mixer = [("conv_w_in", l), ("conv_w_out", l)] if l < n_conv else [("attn_w_q", j), ("attn_w_o", j)]
        sets.append([("ffn1_w_in", l), ("ffn1_w_out", l)] + ([("w_kv", 0)] if l == n_conv else []))
        sets.append(mixer + [("ffn2_w_in", l), ("ffn2_w_out", l)])
    return sets


def _as_weight(name, buf):
    _, _, ha, B = buf.shape
    return buf.reshape(8 * ha, B) if name.endswith("w_out") else buf.reshape(4, 2 * ha, B)


def _local_step(x, tgt, tabs, norm_g, conv_w, kv_norm_g, final_norm_g, mods, kvmods, cast, place):
    S, D = x.shape
    n_layers, n_conv = norm_g.shape[0], conv_w.shape[0]
    md = mods.reshape(n_layers, 9, D)

    def vec_of(l, k):
        return _vec(norm_g[l, k], md[l, 3 * k + 1], md[l, 3 * k], md[l, 3 * k + 2])

    W = {}

    def install(items, bufs):
        for (name, idx), b in zip(items, bufs):
            W[name, idx] = _as_weight(name, b)

    Q = _sets(n_layers, n_conv)
    assert n_conv >= 1, "the first layer's conv mixer carries a weight set"
    install(Q[0], _comm_call("gather_d2d", _comm_call("gather_ici", [cast[it] for it in Q[0]])))
    arrived = {}

    def ffn_fwd(n, *args):
        ici = [k for k in ([1] if n == 0 else [n + 2]) if k < len(Q)]
        d2d = [n + 1] if 2 <= n + 1 < len(Q) else []
        comm = [("gather_ici", [cast[it] for it in Q[k]]) for k in ici] + [("gather_d2d", arrived.pop(k)) for k in d2d]
        xo, h, ab, y, *moved = _ffn_fwd(*args, comm)
        for k in ici:
            arrived[k], moved = moved[:len(Q[k])], moved[len(Q[k]):]
        for k in d2d:
            install(Q[k], moved[:len(Q[k])])
            moved = moved[len(Q[k]):]
        if n == 0:
            install(Q[1], _comm_call("gather_d2d", arrived.pop(1)))
        return xo, h, ab, y

    vec_kv = _vec(kv_norm_g, kvmods[D:], kvmods[:D])
    ng = len(DILATIONS)
    n_rope = ng * GROUP_W // LANES
    saved = {}
    xc = x
    ks = vs = None
    for l in range(n_layers):
        if l == n_conv:
            h_kv, kv = _proj_fwd(xc, vec_kv, W["w_kv", 0], tabs, n_rope, DILATIONS * 2)
            ks, vs = kv[:ng], kv[ng:]
            saved["kv"] = (xc, h_kv)
        v0 = vec_of(l, 0)
        xo, h, ab, y = ffn_fwd(2 * l, xc, v0, W["ffn1_w_in", l], W["ffn1_w_out", l])
        saved[l, 0] = (xc, v0, h, ab, y)
        xc = xo
        v1 = vec_of(l, 1)
        if l < n_conv:
            cw = _vec(conv_w[l, 0], conv_w[l, 1], conv_w[l, 2])
            early = [("gather_ici", [cast[it] for it in Q[2]])] if l == 0 and len(Q) > 2 else None
            xo, h, s4, z, y, *moved = _conv_fwd(xc, v1, cw, W["conv_w_in", l], W["conv_w_out", l], early)
            if early:
                arrived[2] = moved
            saved[l, 1] = (xc, v1, h, y, cw, s4, z)
        else:
            j = l - n_conv
            h, qs = _proj_fwd(xc, v1, W["attn_w_q", j], tabs, n_rope, DILATIONS, SM_SCALE)
            og = [_attn_fwd(qs[g], ks[g], vs[g], DILATIONS[g]) for g in range(ng)]
            mixs, ljs, y, xo = _attn_out([o for o, _ in og], [s for _, s in og], xc, v1, W["attn_w_o", j])
            saved[l, 1] = (xc, v1, h, y, qs, mixs, ljs)
        xc = xo
        v2 = vec_of(l, 2)
        xo, h, ab, y = ffn_fwd(2 * l + 1, xc, v2, W["ffn2_w_in", l], W["ffn2_w_out", l])
        saved[l, 2] = (xc, v2, h, ab, y)
        xc = xo

    dx, red_f = _final_loss(xc, _vec(final_norm_g), tgt)
    loss_part = 0.5 * jnp.sum(red_f[1]) / D
    G, parts, others = {}, {}, {}
    dmods = [[None] * 9 for _ in range(n_layers)]
    dnorm = [[None] * 3 for _ in range(n_layers)]
    dconvw = [None] * n_conv
    dkv_parts = []

    def halves(items):
        return [G[it].reshape(4, 2, G[it].shape[1] // 2, G[it].shape[2]) for it in items]

    def pair_sums(items, recv):
        for it, g, r in zip(items, halves(items), recv):
            parts[it] = _pair_add(g, r, place[1:])
        return [parts[it] for it in items]

    R = Q[::-1]
    recvd = {}

    def ffn_bwd(dx, l, k, name):
        m = 2 * (n_layers - 1 - l) + (0 if k == 2 else 1)
        ex = [m - 1] if m >= 1 else []
        sc = [m - 2] if m >= 2 else []
        comm = [("exchange", halves(R[e])) for e in ex] + [("scatter", pair_sums(R[e], recvd.pop(e))) for e in sc]
        xin, v, h, ab, y = saved[l, k]
        w_in, w_out = W[name + "_w_in", l], W[name + "_w_out", l]
        dy, u, dab, dx, red, *moved = _ffn_dgrad(dx, v, ab, w_in, w_out, xin, y, comm)
        for e in ex:
            recvd[e], moved = moved[:len(R[e])], moved[len(R[e]):]
        for e in sc:
            others.update(zip(R[e], moved[:len(R[e])]))
            moved = moved[len(R[e]):]
        if m == len(R) - 1:
            G[name + "_w_in", l], got = _mm_tn(h, dab, w_in.shape[2], [("scatter", pair_sums(R[m - 1], recvd.pop(m - 1)))])
            others.update(zip(R[m - 1], got))
        else:
            G[name + "_w_in", l] = _mm_tn(h, dab, w_in.shape[2])
        G[name + "_w_out", l] = _mm_tn(u, dy[None], D).reshape(4, -1, D)
        return dx, red

    def note(l, k, red):
        dnorm[l][k] = red[0]
        dmods[l][3 * k], dmods[l][3 * k + 1], dmods[l][3 * k + 2] = red[2], red[1], red[3]

    for l in reversed(range(n_layers)):
        dx, red = ffn_bwd(dx, l, 2, "ffn2")
        note(l, 2, red)
        if l < n_conv:
            xin, v, h, y, cw, s4, z = saved[l, 1]
            w_in, w_out = W["conv_w_in", l], W["conv_w_out", l]
            dy, d3, dx, red, dcw = _conv_dgrad(dx, v, cw, s4, w_in, w_out, xin, y)
            G["conv_w_in", l] = _mm_tn(h, d3[None], w_in.shape[2])
            G["conv_w_out", l] = _mm_tn(z, dy[None], D).reshape(4, -1, D)
            dconvw[l] = dcw[0:3]
        else:
            j = l - n_conv
            xin, v, h, y, qs, mixs, ljs = saved[l, 1]
            wq, wo = W["attn_w_q", j], W["attn_w_o", j]
            dy, dmixs = _attn_out_dgrad(dx, v, wo)
            G["attn_w_o", j] = _mm_tn(mixs[0], dy[None], wo.shape[2])
            dqkv = [_attn_bwd(qs[g], ks[g], vs[g], dmixs[g], mixs[g], ljs[g], DILATIONS[g]) for g in range(ng)]
            dkv_parts.append([t[1:] for t in dqkv])
            dq, dx, red = _proj_dgrad([[t[0]] for t in dqkv], tabs, wq, ng, DILATIONS, dx, xin, y, v)
            G["attn_w_q", j] = _mm_tn(h, dq[None], wq.shape[2])
        note(l, 1, red)
        dx, red = ffn_bwd(dx, l, 0, "ffn1")
        note(l, 0, red)
        if l == n_conv:
            xin, h_kv = saved["kv"]
            kv_parts = [[lay[g][0] for lay in dkv_parts] for g in range(ng)] + [[lay[g][1] for lay in dkv_parts] for g in range(ng)]
            dkv, dx, red_kv = _proj_dgrad(kv_parts, tabs, W["w_kv", 0], ng, DILATIONS * 2, dx, xin, None, vec_kv)
            G["w_kv", 0] = _mm_tn(h_kv, dkv[None], W["w_kv", 0].shape[2])
    others.update(zip(R[-1], _comm_call("scatter", pair_sums(R[-1], _comm_call("exchange", halves(R[-1]))))))
    small = jnp.concatenate(
        [jnp.stack([jnp.stack(r) for r in dmods]).reshape(-1), red_kv[2], red_kv[1],
         jnp.stack([jnp.stack(r) for r in dnorm]).reshape(-1), jnp.stack(dconvw).reshape(-1), red_kv[0], red_f[0]])
    return loss_part, dx, parts, others, small


BIG = ("ffn1_w_in", "ffn1_w_out", "ffn2_w_in", "ffn2_w_out", "conv_w_in", "conv_w_out", "w_kv", "attn_w_q", "attn_w_o")


def _halved(t):
    if t.ndim == 2:
        t = t[None]
    L, A, B = t.shape
    return t.reshape(L, 2, A // 2, B)


def kernel(x, c, positions, norm_g, ada_w, ada_b, ffn1_w_in, ffn1_w_out, ffn2_w_in, ffn2_w_out, conv_w_in, conv_w, conv_w_out, kv_norm_g, kv_ada_w, kv_ada_b, w_kv, attn_w_q, attn_w_o, final_norm_g, loss_target, m_norm_g, m_ada_w, m_ada_b, m_ffn1_w_in, m_ffn1_w_out, m_ffn2_w_in, m_ffn2_w_out, m_conv_w_in, m_conv_w, m_conv_w_out, m_kv_norm_g, m_kv_ada_w, m_kv_ada_b, m_w_kv, m_attn_w_q, m_attn_w_o, m_final_norm_g, v_norm_g, v_ada_w, v_ada_b, v_ffn1_w_in, v_ffn1_w_out, v_ffn2_w_in, v_ffn2_w_out, v_conv_w_in, v_conv_w, v_conv_w_out, v_kv_norm_g, v_kv_ada_w, v_kv_ada_b, v_w_kv, v_attn_w_q, v_attn_w_o, v_final_norm_g):
    wts = dict(norm_g=norm_g, ada_w=ada_w, ada_b=ada_b, ffn1_w_in=ffn1_w_in, ffn1_w_out=ffn1_w_out, ffn2_w_in=ffn2_w_in,
               ffn2_w_out=ffn2_w_out, conv_w_in=conv_w_in, conv_w=conv_w, conv_w_out=conv_w_out, kv_norm_g=kv_norm_g,
               kv_ada_w=kv_ada_w, kv_ada_b=kv_ada_b, w_kv=w_kv, attn_w_q=attn_w_q, attn_w_o=attn_w_o,
               final_norm_g=final_norm_g)
    ms = dict(norm_g=m_norm_g, ada_w=m_ada_w, ada_b=m_ada_b, ffn1_w_in=m_ffn1_w_in, ffn1_w_out=m_ffn1_w_out,
              ffn2_w_in=m_ffn2_w_in, ffn2_w_out=m_ffn2_w_out, conv_w_in=m_conv_w_in, conv_w=m_conv_w,
              conv_w_out=m_conv_w_out, kv_norm_g=m_kv_norm_g, kv_ada_w=m_kv_ada_w, kv_ada_b=m_kv_ada_b, w_kv=m_w_kv,
              attn_w_q=m_attn_w_q, attn_w_o=m_attn_w_o, final_norm_g=m_final_norm_g)
    vs = dict(norm_g=v_norm_g, ada_w=v_ada_w, ada_b=v_ada_b, ffn1_w_in=v_ffn1_w_in, ffn1_w_out=v_ffn1_w_out,
              ffn2_w_in=v_ffn2_w_in, ffn2_w_out=v_ffn2_w_out, conv_w_in=v_conv_w_in, conv_w=v_conv_w,
              conv_w_out=v_conv_w_out, kv_norm_g=v_kv_norm_g, kv_ada_w=v_kv_ada_w, kv_ada_b=v_kv_ada_b, w_kv=v_w_kv,
              attn_w_q=v_attn_w_q, attn_w_o=v_attn_w_o, final_norm_g=v_final_norm_g)
    order = list(wts)
    S, D = x.shape[1], x.shape[2]
    n_layers, n_conv = norm_g.shape[0], conv_w.shape[0]
    nm, sw = ada_w.shape[2], norm_g.shape[2]
    ix, iy, ic = (lax.axis_index(a) for a in AXES)
    chip = 2 * ix + iy

    pack = jnp.concatenate([norm_g.reshape(-1), conv_w.reshape(-1), c.reshape(-1)])
    npad = (-pack.size) % (8 * LANES)
    allp = _gather8(jnp.pad(pack, (0, npad)).reshape(-1, LANES)).reshape(8, -1)
    n1, n2 = norm_g.size, norm_g.size + conv_w.size
    by_chip = allp[0::2]
    norm_full = jnp.moveaxis(by_chip[:, :n1].reshape(4, n_layers, 3, sw), 0, 2).reshape(n_layers, 3, 4 * sw)
    conv_full = jnp.moveaxis(by_chip[:, n1:n2].reshape(4, n_conv, 3, sw), 0, 2).reshape(n_conv, 3, 4 * sw)
    c_all = allp[:, n2:n2 + D]

    b_l = lax.dynamic_slice_in_dim(ada_b, chip * nm, nm, axis=1)
    mod_sh = _ada_fwd(c_all, ada_w, b_l[:, None, :])
    nkv = kv_ada_w.shape[1]
    bkv = lax.dynamic_slice_in_dim(kv_ada_b, chip * nkv, nkv, axis=0)
    kv_sh = _ada_fwd(c_all, kv_ada_w[None], bkv[None, None, :])
    rows = jnp.concatenate([jnp.moveaxis(mod_sh, 0, 1), jnp.pad(jnp.moveaxis(kv_sh, 0, 1), ((0, 0), (0, 0), (0, nm - nkv)))], axis=1)
    rpad = (-rows.shape[1]) % 8
    mine = _mods_to_owner(jnp.pad(rows, ((0, 0), (0, rpad), (0, 0))))
    mods = jnp.moveaxis(mine[:, :n_layers], 0, 1).reshape(n_layers, 4 * nm)
    kvmods = mine[:, n_layers, :nkv].reshape(4 * nkv)

    place = jnp.stack([chip, ic]).astype(jnp.int32)
    halved = {k: _halved(wts[k]) for k in BIG}
    cast = {(k, l): _cast_shard(halved[k], l, place) for k in BIG for l in range(halved[k].shape[0])}

    loss_part, dx, parts, others, small = _local_step(x[0], loss_target[0], _rope_tables(positions), norm_full, conv_full,
                                                      kv_norm_g, final_norm_g, mods, kvmods, cast, place)
    loss = lax.psum(loss_part, AXES)

    gbufs = []
    for k in BIG:
        gbuf, n_l = None, halved[k].shape[0]
        for l in range(n_l):
            gbuf = _sum_partials(parts[k, l], others[k, l], gbuf, n_l, l, place)
        gbufs.append(gbuf)
    grads = {}

    spad = (-small.size) % (8 * LANES)
    gath = _gather8(jnp.pad(small, (0, spad)).reshape(-1, LANES))
    tot = _sum8(gath).reshape(-1)
    gath = gath.reshape(8, -1)
    o = 0

    def take(n):
        nonlocal o
        o += n
        return tot[o - n:o]

    g_mods = take(n_layers * 9 * D).reshape(n_layers, 9 * D)
    g_kvmods = take(2 * D)
    g_norm = take(n_layers * 3 * D).reshape(n_layers, 3, D)
    g_convw = take(n_conv * 3 * D).reshape(n_conv, 3, D)
    g_kvn = take(D)
    g_fin = take(D)
    grads["ada_b"] = g_mods
    grads["kv_ada_b"] = g_kvmods
    grads["norm_g"] = lax.dynamic_slice_in_dim(g_norm, chip * sw, sw, axis=2)
    grads["conv_w"] = lax.dynamic_slice_in_dim(g_convw, chip * sw, sw, axis=2)
    grads["kv_norm_g"] = g_kvn
    grads["final_norm_g"] = g_fin

    dm_all = gath[:, :n_layers * 9 * D].reshape(8, n_layers, 9 * D)
    dm_mine = jnp.moveaxis(lax.dynamic_slice_in_dim(dm_all, chip * nm, nm, axis=2), 0, 1)
    dkv_all = gath[:, n_layers * 9 * D:n_layers * 9 * D + 2 * D]
    dkv_mine = lax.dynamic_slice_in_dim(dkv_all, chip * nkv, nkv, axis=1)[None]
    c_t = c_all.T
    grads["ada_w"] = _ada_wgrad(c_t, dm_mine)
    grads["kv_ada_w"] = _ada_wgrad(c_t, dkv_mine)[0]

    grads.update({k: s.reshape(wts[k].shape) for k, s in zip(BIG, _comm_call("share", gbufs))})
    deltas, new_m, new_v = {}, {}, {}
    for k in order:
        shp = wts[k].shape
        two_d = (lambda t: t.reshape(1, -1)) if len(shp) == 1 else (lambda t: t)
        dlt, nmk, nvk, gk = _adamw(two_d(wts[k]), two_d(grads[k]), two_d(ms[k]), two_d(vs[k]), k in BIG)
        deltas[k], new_m[k], new_v[k] = dlt.reshape(shp), nmk.reshape(shp), nvk.reshape(shp)
        grads[k] = gk.reshape(shp)
    return (loss, dx[None], *[grads[k] for k in order], *[deltas[k] for k in order], *[new_m[k] for k in order],
            *[new_v[k] for k in order])
```

```python
import functools
import math

import jax
import jax.numpy as jnp
from jax import lax
from jax.experimental import pallas as pl
from jax.experimental.pallas import tpu as pltpu

F32 = jnp.float32
BF = jnp.bfloat16
MESH = pl.DeviceIdType.MESH
AXES = ("x", "y", "c")

NORM_EPS = 1e-5
HEAD_DIM = 64
HEADS_PER_GROUP = 8
GROUP_W = HEADS_PER_GROUP * HEAD_DIM
DILATIONS = (1, 4, 16)
BAND = 128
ROPE_DIM = 16
ROPE_THETA = 500000.0
SM_SCALE = HEAD_DIM ** -0.5
NEG = -1e30
ADAM_LR, ADAM_B1, ADAM_B2, ADAM_EPS, ADAM_WD, ADAM_STEP = 0.001, 0.9, 0.999, 1e-08, 0.01, 10

V7X_VMEM_BYTES = 64 * 1024 * 1024
VMEM_LIMIT = V7X_VMEM_BYTES - 6 * 1024 * 1024
LANES = 128
TM = 512
TM_DGRAD = 256
TM_STREAM = 1024
TK = 2048
WGRAD_COLS = 1536


def _cparams(ngrid):
    return pltpu.CompilerParams(dimension_semantics=("arbitrary",) * ngrid, vmem_limit_bytes=VMEM_LIMIT)


def _dot(a, b):
    return jnp.dot(a, b, preferred_element_type=F32)


def _dot_nt(a, b):
    return lax.dot_general(a, b, (((1,), (1,)), ((), ())), preferred_element_type=F32)


def _dot_tn(a, b):
    return lax.dot_general(a, b, (((0,), (0,)), ((), ())), preferred_element_type=F32)


def _tile(n, pref):
    t = min(n, pref)
    while n % t:
        t //= 2
    return t


def _row_tile(n, cap=512, mult=16):
    best = n
    for t in range(mult, min(n, cap) + 1, mult):
        if n % t == 0:
            best = t
    return best


def _rstd(x):
    return lax.rsqrt(jnp.mean(x * x, axis=-1, keepdims=True) + NORM_EPS)


def _norm_mod(x, v):
    return (x * _rstd(x) * v[0:1]) * (1.0 + v[1:2]) + v[2:3]


def _silu_parts(a):
    sg = jax.nn.sigmoid(a)
    return sg, a * sg


def _row(i):
    return lambda *_: (i, 0)


ANY = pl.BlockSpec(memory_space=pl.ANY)
IN_PLACE = ("gather_ici", "gather_d2d", "share")


def _place():
    x, y, c = (lax.axis_index(a) for a in AXES)
    chips = [(1 - x, y), (x, 1 - y), (1 - x, 1 - y)]
    return x, y, c, 2 * x + y, chips, [2 * cx + cy for cx, cy in chips]


def _transfers(kind, ins, outs):
    x, y, c, p, chips, qs = _place()
    sib = (x, y, 1 - c)
    rows = []
    for k, o in enumerate(outs):
        if kind == "gather_ici":
            rows.append([(o.at[p, c], o.at[p, c], (*chips[j], c), o.at[qs[j], c]) for j in range(3)])
        elif kind == "gather_d2d":
            rows.append([(o.at[qs[j], c], o.at[qs[j], c], sib, o.at[qs[j], 1 - c]) for j in range(3)])
        elif kind == "exchange":
            rows.append([(ins[k].at[:, 1 - c], o, sib, o)])
        elif kind == "scatter":
            rows.append([(ins[k].at[qs[j]], o.at[j], (*chips[j], c), o.at[j]) for j in range(3)])
        elif kind == "share":
            rows.append([(o.at[:, c], o.at[:, c], sib, o.at[:, 1 - c])])
    return rows


def _comm_out_shapes(kind, arrays):
    if kind in IN_PLACE:
        return [jax.ShapeDtypeStruct(a.shape, a.dtype) for a in arrays]
    if kind == "exchange":
        return [jax.ShapeDtypeStruct((4,) + a.shape[2:], a.dtype) for a in arrays]
    return [jax.ShapeDtypeStruct((3,) + a.shape[1:], a.dtype) for a in arrays]


def _comm_sems(n):
    return [pltpu.SemaphoreType.DMA((n, 3)), pltpu.SemaphoreType.DMA((n, 3))]


def _comm_start(rows, ssem, rsem):
    for k, row in enumerate(rows):
        for j, (src, dst, dev, _) in enumerate(row):
            pltpu.make_async_remote_copy(src_ref=src, dst_ref=dst, send_sem=ssem.at[k, j], recv_sem=rsem.at[k, j],
                                         device_id=dev, device_id_type=MESH).start()


def _comm_wait(rows, ssem, rsem):
    for k, row in enumerate(rows):
        for j, (src, dst, dev, land) in enumerate(row):
            pltpu.make_async_remote_copy(src_ref=src, dst_ref=dst, send_sem=ssem.at[k, j], recv_sem=rsem.at[k, j],
                                         device_id=dev, device_id_type=MESH).wait_send()
            pltpu.make_async_remote_copy(src_ref=land, dst_ref=land, send_sem=ssem.at[k, j], recv_sem=rsem.at[k, j],
                                         device_id=dev, device_id_type=MESH).wait_recv()


def _comm_call(kind, arrays):
    n = len(arrays)

    def body(*refs):
        rows = _transfers(kind, refs[:n], refs[n:2 * n])
        _comm_start(rows, *refs[2 * n:])
        _comm_wait(rows, *refs[2 * n:])

    return pl.pallas_call(
        body, name="comm_" + kind, in_specs=[ANY] * n, out_specs=[ANY] * n, out_shape=_comm_out_shapes(kind, arrays),
        input_output_aliases={k: k for k in range(n)} if kind in IN_PLACE else {},
        scratch_shapes=_comm_sems(n),
    )(*arrays)


def _ride(comm, n_in, n_out):
    ride = dict(arrays=[], in_specs=[], out_specs=[], out_shape=[], aliases={}, scratch=[], n=0)
    for kind, arrays in comm or []:
        if kind in IN_PLACE:
            ride["aliases"].update({n_in + ride["n"] + k: n_out + ride["n"] + k for k in range(len(arrays))})
        ride["arrays"] += list(arrays)
        ride["out_shape"] += _comm_out_shapes(kind, arrays)
        ride["scratch"] += _comm_sems(len(arrays))
        ride["n"] += len(arrays)
    ride["in_specs"] = ride["out_specs"] = [ANY] * ride["n"]
    return ride


def _ride_when(comm, ins, outs, sems, cond, action):
    if not comm:
        return

    @pl.when(cond)
    def _():
        at = 0
        for e, (kind, arrays) in enumerate(comm):
            n = len(arrays)
            action(_transfers(kind, ins[at:at + n], outs[at:at + n]), *sems[2 * e:2 * e + 2])
            at += n


def _ffn_fwd(x, vec, w_in4, w_out, comm=None):
    S, D = x.shape
    cs = w_in4.shape[2]
    F = 2 * cs
    tm = _tile(S, TM)
    ni = S // tm
    ride = _ride(comm, 5, 4)
    nc = ride["n"]

    def body(*refs):
        x_ref, vec_ref, wa_ref, wb_ref, wo_ref = refs[:5]
        xo_ref, h_ref, ab_ref, y_ref = refs[5 + nc:9 + nc]
        i = pl.program_id(0)
        riding = (comm, refs[5:5 + nc], refs[9 + nc:9 + 2 * nc], refs[9 + 2 * nc:])
        _ride_when(*riding, i == 0, _comm_start)
        xv = x_ref[...]
        h = _norm_mod(xv, vec_ref[...]).astype(BF)
        h_ref[...] = h
        y = None
        for c in range(2):
            cols = slice(c * cs, (c + 1) * cs)
            a = _dot(h, wa_ref[c])
            b = _dot(h, wb_ref[c])
            ab_ref[0, :, cols] = a.astype(BF)
            ab_ref[1, :, cols] = b.astype(BF)
            _, s = _silu_parts(a)
            part = _dot((s * b).astype(BF), wo_ref[cols, :])
            y = part if y is None else y + part
        y_ref[...] = y.astype(BF)
        xo_ref[...] = xv + (0.5 * (1.0 + vec_ref[3:4, :])) * y
        _ride_when(*riding, i == ni - 1, _comm_wait)

    tok = pl.BlockSpec((tm, D), lambda i: (i, 0))
    once = pl.Buffered(1)
    return pl.pallas_call(
        body, name="ffn_fwd", grid=(ni,),
        in_specs=[tok, pl.BlockSpec((8, D), lambda i: (0, 0)),
                  pl.BlockSpec((2, D, cs), lambda i: (0, 0, 0), pipeline_mode=once),
                  pl.BlockSpec((2, D, cs), lambda i: (1, 0, 0), pipeline_mode=once),
                  pl.BlockSpec((F, D), lambda i: (0, 0), pipeline_mode=once)] + ride["in_specs"],
        out_specs=[tok, tok, pl.BlockSpec((2, tm, F), lambda i: (0, i, 0)), tok] + ride["out_specs"],
        out_shape=[jax.ShapeDtypeStruct((S, D), F32), jax.ShapeDtypeStruct((S, D), BF),
                   jax.ShapeDtypeStruct((2, S, F), BF), jax.ShapeDtypeStruct((S, D), BF)] + ride["out_shape"],
        input_output_aliases=ride["aliases"], scratch_shapes=ride["scratch"],
        compiler_params=_cparams(1),
    )(x, vec, w_in4, w_in4, w_out, *ride["arrays"])


def _norm_bwd_tile(dxo, dh, xv, y, vec_ref, red_ref, coef):
    g = vec_ref[0:1, :]
    sc = vec_ref[1:2, :]
    r = _rstd(xv)
    xh = xv * r
    dhn = dh * (1.0 + sc)
    dxh = dhn * g
    red_ref[0:1, :] += jnp.sum(dhn * xh, axis=0, keepdims=True)
    red_ref[1:2, :] += jnp.sum(dh * (xh * g), axis=0, keepdims=True)
    red_ref[2:3, :] += jnp.sum(dh, axis=0, keepdims=True)
    if y is not None:
        red_ref[3:4, :] += coef * jnp.sum(dxo * y.astype(F32), axis=0, keepdims=True)
    return dxo + r * (dxh - xh * jnp.mean(dxh * xh, axis=-1, keepdims=True))


def _ffn_dgrad(dxo, vec, ab, w_in4, w_out, x, y, comm=None):
    S, D = dxo.shape
    cs = w_in4.shape[2]
    F = 2 * cs
    tm = _tile(S, TM_DGRAD)
    ni = S // tm
    ride = _ride(comm, 8, 5)
    nc = ride["n"]

    def body(*refs):
        dxo_ref, vec_ref, ab_ref, wa_ref, wb_ref, wo_ref, x_ref, y_ref = refs[:8]
        dy_ref, u_ref, dab_ref, dx_ref, red_ref = refs[8 + nc:13 + nc]
        i = pl.program_id(0)
        riding = (comm, refs[8:8 + nc], refs[13 + nc:13 + 2 * nc], refs[13 + 2 * nc:])
        _ride_when(*riding, i == 0, _comm_start)

        @pl.when(i == 0)
        def _():
            red_ref[...] = jnp.zeros_like(red_ref)

        dy = ((0.5 * (1.0 + vec_ref[3:4, :])) * dxo_ref[...]).astype(BF)
        dy_ref[...] = dy
        dh = None
        for c in range(2):
            cols = slice(c * cs, (c + 1) * cs)
            du = _dot_nt(dy, wo_ref[cols, :])
            a = ab_ref[0, :, cols].astype(F32)
            b = ab_ref[1, :, cols].astype(F32)
            sg, s = _silu_parts(a)
            u_ref[:, cols] = (s * b).astype(BF)
            da = (du * b * (sg * (1.0 + a * (1.0 - sg)))).astype(BF)
            db = (du * s).astype(BF)
            dab_ref[0, :, cols] = da
            dab_ref[1, :, cols] = db
            part = _dot_nt(da, wa_ref[c]) + _dot_nt(db, wb_ref[c])
            dh = part if dh is None else dh + part
        dx_ref[...] = _norm_bwd_tile(dxo_ref[...], dh, x_ref[...], y_ref[...], vec_ref, red_ref, 0.5)
        _ride_when(*riding, i == ni - 1, _comm_wait)

    tok = pl.BlockSpec((tm, D), lambda i: (i, 0))
    small = pl.BlockSpec((8, D), lambda i: (0, 0))
    once = pl.Buffered(1)
    return pl.pallas_call(
        body, name="ffn_dgrad", grid=(ni,),
        in_specs=[tok, small, pl.BlockSpec((2, tm, F), lambda i: (0, i, 0)),
                  pl.BlockSpec((2, D, cs), lambda i: (0, 0, 0), pipeline_mode=once),
                  pl.BlockSpec((2, D, cs), lambda i: (1, 0, 0), pipeline_mode=once),
                  pl.BlockSpec((F, D), lambda i: (0, 0), pipeline_mode=once), tok, tok] + ride["in_specs"],
        out_specs=[tok, pl.BlockSpec((tm, F), lambda i: (i, 0)),
                   pl.BlockSpec((2, tm, F), lambda i: (0, i, 0)), tok, small] + ride["out_specs"],
        out_shape=[jax.ShapeDtypeStruct((S, D), BF), jax.ShapeDtypeStruct((S, F), BF), jax.ShapeDtypeStruct((2, S, F), BF),
                   jax.ShapeDtypeStruct((S, D), F32), jax.ShapeDtypeStruct((8, D), F32)] + ride["out_shape"],
        input_output_aliases=ride["aliases"], scratch_shapes=ride["scratch"],
        compiler_params=_cparams(1),
    )(dxo, vec, ab, w_in4, w_in4, w_out, x, y, *ride["arrays"])


def _mm_tn(a, b3, cs, comm=None):
    K, M = a.shape
    G, _, Nb = b3.shape
    N = G * Nb
    if cs >= Nb or Nb % cs:
        ns, tn = 1, math.gcd(cs, Nb)
    else:
        ns = max(s for s in range(1, Nb // cs + 1) if (Nb // cs) % s == 0 and s * cs <= max(WGRAD_COLS, cs))
        tn = ns * cs
    tmo = M if M <= 1024 else M // 2
    tk = _tile(K, TK)
    gm, gn, gk = M // tmo, N // tn, K // tk
    ride = _ride(comm, 2, 1)
    nc = ride["n"]

    def body(*refs):
        a_ref, b_ref = refs[:2]
        o_ref, acc_ref = refs[2 + nc], refs[-1]
        at = [pl.program_id(t) for t in range(3)]
        riding = (comm, refs[2:2 + nc], refs[3 + nc:3 + 2 * nc], refs[3 + 2 * nc:-1])
        _ride_when(*riding, functools.reduce(jnp.logical_and, [t == 0 for t in at]), _comm_start)

        @pl.when(at[2] == 0)
        def _():
            acc_ref[...] = jnp.zeros_like(acc_ref)

        av = a_ref[...].astype(BF)
        if ns == 1:
            acc_ref[...] += _dot_tn(av, b_ref[...].astype(BF))
        else:
            for s in range(ns):
                acc_ref[s] += _dot_tn(av, b_ref[:, s * cs:(s + 1) * cs].astype(BF))

        @pl.when(at[2] == gk - 1)
        def _():
            o_ref[...] = acc_ref[...].astype(BF)

        _ride_when(*riding, functools.reduce(jnp.logical_and, [t == g - 1 for t, g in zip(at, (gm, gn, gk))]), _comm_wait)

    if ns == 1:
        out_spec = pl.BlockSpec((None, tmo, tn), lambda m, n, k: ((n * tn) // cs, m, ((n * tn) % cs) // tn))
        acc = pltpu.VMEM((tmo, tn), F32)
    else:
        out_spec = pl.BlockSpec((ns, tmo, cs), lambda m, n, k: (n, m, 0))
        acc = pltpu.VMEM((ns, tmo, cs), F32)
    g, *moved = pl.pallas_call(
        body, name="wgrad_tn", grid=(gm, gn, gk),
        in_specs=[pl.BlockSpec((tk, tmo), lambda m, n, k: (k, m)),
                  pl.BlockSpec((None, tk, tn), lambda m, n, k: ((n * tn) // Nb, k, ((n * tn) % Nb) // tn))] + ride["in_specs"],
        out_specs=[out_spec] + ride["out_specs"],
        out_shape=[jax.ShapeDtypeStruct((N // cs, M, cs), BF)] + ride["out_shape"],
        input_output_aliases=ride["aliases"], scratch_shapes=ride["scratch"] + [acc],
        compiler_params=_cparams(3),
    )(a, b3, *ride["arrays"])
    return (g, moved) if comm else g


def _conv_fwd(x, vec, cw, w_in4, w_out, comm=None):
    S, D = x.shape
    cs = w_in4.shape[2]
    tm = _tile(S, TM)
    nt = S // tm
    ride = _ride(comm, 5, 5)
    nc = ride["n"]

    def body(*refs):
        x_ref, vec_ref, cw_ref, wi_ref, wo_ref = refs[:5]
        xo_ref, h_ref, s4_ref, z_ref, y_ref = refs[5 + nc:10 + nc]
        vs_ref = refs[-1]
        riding = (comm, refs[5:5 + nc], refs[10 + nc:10 + 2 * nc], refs[10 + 2 * nc:-1])
        _ride_when(*riding, pl.program_id(0) == 0, _comm_start)

        @pl.when(pl.program_id(0) == 0)
        def _():
            vs_ref[0:8, :] = jnp.zeros((8, D), F32)

        xv = x_ref[...]
        h = _norm_mod(xv, vec_ref[...]).astype(BF)
        h_ref[...] = h
        bcu = jnp.concatenate([_dot(h, wi_ref[q]) for q in range(4)], axis=1)
        bg, cg, u = bcu[:, :D], bcu[:, D:2 * D], bcu[:, 2 * D:]
        v = cg * u
        vs_ref[8:8 + tm, :] = v
        conv = cw_ref[0:1, :] * vs_ref[pl.ds(6, tm), :] + cw_ref[1:2, :] * vs_ref[pl.ds(7, tm), :] + cw_ref[2:3, :] * v
        vs_ref[0:8, :] = vs_ref[tm:tm + 8, :]
        z = (bg * conv).astype(BF)
        s4_ref[0] = bg.astype(BF)
        s4_ref[1] = cg.astype(BF)
        s4_ref[2] = u.astype(BF)
        s4_ref[3] = conv.astype(BF)
        z_ref[...] = z
        y = _dot(z, wo_ref[...])
        y_ref[...] = y.astype(BF)
        xo_ref[...] = xv + (1.0 + vec_ref[3:4, :]) * y
        _ride_when(*riding, pl.program_id(0) == nt - 1, _comm_wait)

    tok = pl.BlockSpec((tm, D), lambda i: (i, 0))
    small = pl.BlockSpec((8, D), lambda i: (0, 0))
    return pl.pallas_call(
        body, name="conv_fwd", grid=(nt,),
        in_specs=[tok, small, small, pl.BlockSpec((4, D, cs), lambda i: (0, 0, 0)),
                  pl.BlockSpec((D, D), lambda i: (0, 0))] + ride["in_specs"],
        out_specs=[tok, tok, pl.BlockSpec((4, tm, D), lambda i: (0, i, 0)), tok, tok] + ride["out_specs"],
        out_shape=[jax.ShapeDtypeStruct((S, D), F32), jax.ShapeDtypeStruct((S, D), BF),
                   jax.ShapeDtypeStruct((4, S, D), BF), jax.ShapeDtypeStruct((S, D), BF),
                   jax.ShapeDtypeStruct((S, D), BF)] + ride["out_shape"],
        input_output_aliases=ride["aliases"], scratch_shapes=ride["scratch"] + [pltpu.VMEM((tm + 8, D), F32)],
        compiler_params=_cparams(1),
    )(x, vec, cw, w_in4, w_out, *ride["arrays"])


def _conv_dgrad(dxo, vec, cw, s4, w_in4, w_out, x, y):
    S, D = dxo.shape
    cs = w_in4.shape[2]
    tm = _tile(S, TM)
    nt = S // tm

    def body(dxo_ref, vec_ref, cw_ref, s4_ref, wi_ref, wo_ref, x_ref, y_ref, dy_ref, d3_ref, dx_ref, red_ref, dcw_ref, ds_ref):
        @pl.when(pl.program_id(0) == 0)
        def _():
            ds_ref[tm:tm + 8, :] = jnp.zeros((8, D), F32)
            dcw_ref[...] = jnp.zeros_like(dcw_ref)
            red_ref[...] = jnp.zeros_like(red_ref)

        dy = ((1.0 + vec_ref[3:4, :]) * dxo_ref[...]).astype(BF)
        dy_ref[...] = dy
        dz = _dot_nt(dy, wo_ref[...])
        bg = s4_ref[0].astype(F32)
        cg = s4_ref[1].astype(F32)
        u = s4_ref[2].astype(F32)
        conv = s4_ref[3].astype(F32)
        dbg = dz * conv
        dconv = dz * bg
        ds_ref[0:tm, :] = dconv
        d1 = ds_ref[pl.ds(1, tm), :]
        d2 = ds_ref[pl.ds(2, tm), :]
        ds_ref[tm:tm + 8, :] = ds_ref[0:8, :]
        dv = cw_ref[2:3, :] * dconv + cw_ref[1:2, :] * d1 + cw_ref[0:1, :] * d2
        v = cg * u
        dcw_ref[0:1, :] += jnp.sum(d2 * v, axis=0, keepdims=True)
        dcw_ref[1:2, :] += jnp.sum(d1 * v, axis=0, keepdims=True)
        dcw_ref[2:3, :] += jnp.sum(dconv * v, axis=0, keepdims=True)
        dbcu = jnp.concatenate([dbg, dv * u, dv * cg], axis=1).astype(BF)
        d3_ref[...] = dbcu
        dh = _dot_nt(dbcu[:, 0:cs], wi_ref[0])
        for q in range(1, 4):
            dh = dh + _dot_nt(dbcu[:, q * cs:(q + 1) * cs], wi_ref[q])
        dx_ref[...] = _norm_bwd_tile(dxo_ref[...], dh, x_ref[...], y_ref[...], vec_ref, red_ref, 1.0)

    tok = pl.BlockSpec((tm, D), lambda i: (nt - 1 - i, 0))
    small = pl.BlockSpec((8, D), lambda i: (0, 0))
    once = pl.Buffered(1)
    return pl.pallas_call(
        body, name="conv_dgrad", grid=(nt,),
        in_specs=[tok, small, small, pl.BlockSpec((4, tm, D), lambda i: (0, nt - 1 - i, 0)),
                  pl.BlockSpec((4, D, cs), lambda i: (0, 0, 0), pipeline_mode=once),
                  pl.BlockSpec((D, D), lambda i: (0, 0), pipeline_mode=once), tok, tok],
        out_specs=[tok, pl.BlockSpec((tm, 3 * D), lambda i: (nt - 1 - i, 0)), tok, small, small],
        out_shape=[jax.ShapeDtypeStruct((S, D), BF), jax.ShapeDtypeStruct((S, 3 * D), BF),
                   jax.ShapeDtypeStruct((S, D), F32), jax.ShapeDtypeStruct((8, D), F32), jax.ShapeDtypeStruct((8, D), F32)],
        scratch_shapes=[pltpu.VMEM((tm + 8, D), F32)],
        compiler_params=_cparams(1),
    )(dxo, vec, cw, s4, w_in4, w_out, x, y)


def _rope_tables(positions):
    S = positions.shape[-1]
    inv = ROPE_THETA ** (-jnp.arange(0, ROPE_DIM, 2, dtype=F32) / ROPE_DIM)
    ang = positions.reshape(S, 1).astype(F32) * inv
    cos = jnp.tile(jnp.cos(ang), (1, LANES // 8))
    sin = jnp.tile(jnp.sin(ang), (1, LANES // 8))
    l64 = jnp.arange(LANES) % HEAD_DIM
    return jnp.stack([jnp.where(l64 < ROPE_DIM, cos, 1.0),
                      jnp.where(l64 < ROPE_DIM // 2, -sin, 0.0),
                      jnp.where((l64 >= ROPE_DIM // 2) & (l64 < ROPE_DIM), sin, 0.0)])


def _rope(t, tab_ref):
    return t * tab_ref[0] + pltpu.roll(t, LANES - 8, 1) * tab_ref[1] + pltpu.roll(t, 8, 1) * tab_ref[2]


def _rope_t(d, tab_ref):
    return d * tab_ref[0] + pltpu.roll(d * tab_ref[1], 8, 1) + pltpu.roll(d * tab_ref[2], LANES - 8, 1)


GROUP_CH = GROUP_W // LANES


def _to_dilated(src_ref, c0, dst_ref, d):
    n = src_ref.shape[1]
    for r in range(d):
        for ch in range(GROUP_CH):
            rows = src_ref[c0 + ch] if d == 1 else src_ref.at[c0 + ch][pl.ds(r, n // d, stride=d), :]
            dst_ref[:, r * GROUP_W + ch * LANES:r * GROUP_W + (ch + 1) * LANES] = rows.astype(dst_ref.dtype)


def _from_dilated(val, dst_ref, c0, d):
    n = dst_ref.shape[1]
    for r in range(d):
        for ch in range(GROUP_CH):
            cols = val[:, r * GROUP_W + ch * LANES:r * GROUP_W + (ch + 1) * LANES]
            if d == 1:
                dst_ref[c0 + ch] = cols
            else:
                dst_ref.at[c0 + ch][pl.ds(r, n // d, stride=d), :] = cols


def _group(ref, c0):
    return jnp.concatenate([ref[c0 + ch] for ch in range(GROUP_CH)], axis=1)


def _put_group(ref, c0, val):
    for ch in range(GROUP_CH):
        ref[c0 + ch] = val[:, ch * LANES:(ch + 1) * LANES]


def _dil_spec(tm, d):
    return pl.BlockSpec((tm // d, d * GROUP_W), lambda i: (i, 0))


def _proj_fwd(x, vec, w4, tabs, n_rope, dils, scale=1.0):
    S, D = x.shape
    cs = w4.shape[2]
    N = 4 * cs
    tm = _tile(S, TM)

    def body(x_ref, vec_ref, w_ref, tab_ref, h_ref, *rest):
        outs, acc_ref = rest[:-1], rest[-1]
        h = _norm_mod(x_ref[...], vec_ref[...]).astype(BF)
        h_ref[...] = h
        per = cs // LANES
        for q in range(4):
            acc = _dot(h, w_ref[q])
            if scale != 1.0:
                acc = acc * scale
            cols = [acc[:, ch * LANES:(ch + 1) * LANES] for ch in range(per)]
            for ch, t in enumerate(cols):
                acc_ref[q * per + ch] = _rope(t, tab_ref) if q * per + ch < n_rope else t
        for j, d in enumerate(dils):
            _to_dilated(acc_ref, j * GROUP_CH, outs[j], d)

    h, *outs = pl.pallas_call(
        body, name="proj_fwd", grid=(S // tm,),
        in_specs=[pl.BlockSpec((tm, D), lambda i: (i, 0)), pl.BlockSpec((8, D), lambda i: (0, 0)),
                  pl.BlockSpec((4, D, cs), lambda i: (0, 0, 0)), pl.BlockSpec((3, tm, LANES), lambda i: (0, i, 0))],
        out_specs=[pl.BlockSpec((tm, D), lambda i: (i, 0))] + [_dil_spec(tm, d) for d in dils],
        out_shape=[jax.ShapeDtypeStruct((S, D), BF)] + [jax.ShapeDtypeStruct((S // d, d * GROUP_W), BF) for d in dils],
        scratch_shapes=[pltpu.VMEM((N // LANES, tm, LANES), F32)],
        compiler_params=_cparams(1),
    )(x, vec, w4, tabs)
    return h, outs


def _proj_dgrad(parts, tabs, w4, n_rope_groups, dils, dxo, x, y, vec):
    cs, D = w4.shape[2], w4.shape[1]
    N = 4 * cs
    S = parts[0][0].shape[0] * dils[0]
    tm = _tile(S, TM)
    flat = [p for grp in parts for p in grp]
    counts = [len(grp) for grp in parts]
    has_y = y is not None

    def body(*refs):
        prefs = refs[:len(flat)]
        tab_ref, w_ref, dxo_ref, x_ref = refs[len(flat):len(flat) + 4]
        y_ref = refs[len(flat) + 4] if has_y else None
        vec_ref, dz_ref, dx_ref, red_ref, z_ref = refs[-5:]

        @pl.when(pl.program_id(0) == 0)
        def _():
            red_ref[...] = jnp.zeros_like(red_ref)

        k = 0
        for j, cnt in enumerate(counts):
            z = prefs[k][...]
            for r in prefs[k + 1:k + cnt]:
                z = z + r[...]
            k += cnt
            _from_dilated(z, z_ref, 0, dils[j])
            if j < n_rope_groups:
                z = jnp.concatenate([_rope_t(z_ref[ch], tab_ref) for ch in range(GROUP_CH)], axis=1)
            else:
                z = _group(z_ref, 0)
            dz_ref[:, j * GROUP_W:(j + 1) * GROUP_W] = z.astype(BF)
        dh = _dot_nt(dz_ref[:, 0:cs], w_ref[0])
        for q in range(1, 4):
            dh = dh + _dot_nt(dz_ref[:, q * cs:(q + 1) * cs], w_ref[q])
        dx_ref[...] = _norm_bwd_tile(dxo_ref[...], dh, x_ref[...], y_ref[...] if has_y else None, vec_ref, red_ref, 1.0)

    tok = pl.BlockSpec((tm, D), lambda i: (i, 0))
    small = pl.BlockSpec((8, D), lambda i: (0, 0))
    return pl.pallas_call(
        body, name="proj_dgrad", grid=(S // tm,),
        in_specs=[_dil_spec(tm, d) for d, cnt in zip(dils, counts) for _ in range(cnt)]
        + [pl.BlockSpec((3, tm, LANES), lambda i: (0, i, 0)), pl.BlockSpec((4, D, cs), lambda i: (0, 0, 0)), tok, tok]
        + ([tok] if has_y else []) + [small],
        out_specs=[pl.BlockSpec((tm, N), lambda i: (i, 0)), tok, small],
        out_shape=[jax.ShapeDtypeStruct((S, N), BF), jax.ShapeDtypeStruct((S, D), F32), jax.ShapeDtypeStruct((8, D), F32)],
        scratch_shapes=[pltpu.VMEM((GROUP_CH, tm, LANES), F32)],
        compiler_params=_cparams(1),
    )(*flat, tabs, w4, dxo, x, *([y] if has_y else []), vec)


def _band_masks():
    qi = lax.broadcasted_iota(jnp.int32, (BAND, BAND), 0)
    kj = lax.broadcasted_iota(jnp.int32, (BAND, BAND), 1)
    return kj <= qi, kj >= qi


def _head(ref, h):
    return ref[:, h * HEAD_DIM:(h + 1) * HEAD_DIM]


def _attn_fwd(q, k, v, d):
    sd = q.shape[0]
    nb = sd // BAND

    def body(q_ref, kp_ref, kc_ref, vp_ref, vc_ref, o_ref, l_ref):
        same, prev = _band_masks()
        prev = jnp.logical_and(prev, pl.program_id(1) > 0)
        heads = range(HEADS_PER_GROUP)
        scores = [(jnp.where(same, _dot_nt(_head(q_ref, h), _head(kc_ref, h)), NEG),
                   jnp.where(prev, _dot_nt(_head(q_ref, h), _head(kp_ref, h)), NEG)) for h in heads]
        probs = []
        ones = jnp.ones((BAND, LANES), BF)
        for sc, sp in scores:
            m = jnp.maximum(jnp.max(sc, axis=-1, keepdims=True), jnp.max(sp, axis=-1, keepdims=True))
            pc = jnp.exp(sc - m).astype(BF)
            pp = jnp.exp(sp - m).astype(BF)
            den = (_dot(pc, ones) + _dot(pp, ones))[:, 0:1]
            probs.append((pc, pp, m, den))
        outs = [(_dot(pc, _head(vc_ref, h)) + _dot(pp, _head(vp_ref, h))) / den for h, (pc, pp, _, den) in zip(heads, probs)]
        o_ref[...] = jnp.concatenate(outs, axis=1)
        l_ref[...] = jnp.concatenate([jnp.broadcast_to(m + jnp.log(den), (BAND, HEAD_DIM)) for _, _, m, den in probs], axis=1)

    blk = (BAND, GROUP_W)
    cur = pl.BlockSpec(blk, lambda r, i: (i, r))
    prv = pl.BlockSpec(blk, lambda r, i: (jnp.maximum(i - 1, 0), r))
    return pl.pallas_call(
        body, name=f"attn_fwd_d{d}", grid=(d, nb),
        in_specs=[cur, prv, cur, prv, cur], out_specs=[cur, cur],
        out_shape=[jax.ShapeDtypeStruct((sd, d * GROUP_W), F32)] * 2,
        compiler_params=_cparams(2),
    )(q, k, k, v, v)


def _attn_out(os, ls, x, vec, wo4):
    S, D = x.shape
    cs = wo4.shape[2]
    tm = _tile(S, TM)
    ng = len(DILATIONS)

    def body(*refs):
        o_refs, l_refs = refs[0:2 * ng:2], refs[1:2 * ng:2]
        x_ref, vec_ref, wo_ref = refs[2 * ng:2 * ng + 3]
        outs = refs[2 * ng + 3:]
        mix_refs, lj_refs = outs[0:2 * ng:2], outs[1:2 * ng:2]
        y_ref, xo_ref, nat_ref = outs[2 * ng:]
        ov, lv = [], []
        for g in range(ng):
            _from_dilated(o_refs[g][...], nat_ref, 2 * g * GROUP_CH, DILATIONS[g])
            _from_dilated(l_refs[g][...], nat_ref, (2 * g + 1) * GROUP_CH, DILATIONS[g])
            ov.append(_group(nat_ref, 2 * g * GROUP_CH))
            lv.append(_group(nat_ref, (2 * g + 1) * GROUP_CH))
        mx = jnp.maximum(jnp.maximum(lv[0], lv[1]), lv[2])
        es = [jnp.exp(t - mx) for t in lv]
        den = es[0] + es[1] + es[2]
        mix = (es[0] * ov[0] + es[1] * ov[1] + es[2] * ov[2]) / den
        at_mix, at_lj = 2 * ng * GROUP_CH, (2 * ng + 1) * GROUP_CH
        _put_group(nat_ref, at_mix, mix)
        _put_group(nat_ref, at_lj, mx + jnp.log(den))
        for g in range(ng):
            _to_dilated(nat_ref, at_mix, mix_refs[g], DILATIONS[g])
            _to_dilated(nat_ref, at_lj, lj_refs[g], DILATIONS[g])
        mb = mix.astype(BF)
        y = jnp.concatenate([_dot(mb, wo_ref[q]) for q in range(4)], axis=1)
        y_ref[...] = y.astype(BF)
        xo_ref[...] = x_ref[...] + (1.0 + vec_ref[3:4, :]) * y

    tok = pl.BlockSpec((tm, D), lambda i: (i, 0))
    dil = [_dil_spec(tm, d) for d in DILATIONS for _ in range(2)]
    dil_shape = [jax.ShapeDtypeStruct((S // d, d * GROUP_W), F32) for d in DILATIONS for _ in range(2)]
    outs = pl.pallas_call(
        body, name="attn_out", grid=(S // tm,),
        in_specs=dil + [tok, pl.BlockSpec((8, D), lambda i: (0, 0)), pl.BlockSpec((4, GROUP_W, cs), lambda i: (0, 0, 0))],
        out_specs=dil + [tok, tok],
        out_shape=dil_shape + [jax.ShapeDtypeStruct((S, D), BF), jax.ShapeDtypeStruct((S, D), F32)],
        scratch_shapes=[pltpu.VMEM(((2 * ng + 2) * GROUP_CH, tm, LANES), F32)],
        compiler_params=_cparams(1),
    )(*[t for pair in zip(os, ls) for t in pair], x, vec, wo4)
    return list(outs[0:2 * ng:2]), list(outs[1:2 * ng:2]), outs[2 * ng], outs[2 * ng + 1]


def _attn_out_dgrad(dxo, vec, wo4):
    S, D = dxo.shape
    cs = wo4.shape[2]
    tm = _tile(S, TM)

    def body(dxo_ref, vec_ref, wo_ref, dy_ref, *rest):
        dm_refs, nat_ref = rest[:-1], rest[-1]
        dy = ((1.0 + vec_ref[3:4, :]) * dxo_ref[...]).astype(BF)
        dy_ref[...] = dy
        dm = _dot_nt(dy[:, 0:cs], wo_ref[0])
        for q in range(1, 4):
            dm = dm + _dot_nt(dy[:, q * cs:(q + 1) * cs], wo_ref[q])
        _put_group(nat_ref, 0, dm)
        for g, d in enumerate(DILATIONS):
            _to_dilated(nat_ref, 0, dm_refs[g], d)

    tok = pl.BlockSpec((tm, D), lambda i: (i, 0))
    dy, *dms = pl.pallas_call(
        body, name="attn_out_dgrad", grid=(S // tm,),
        in_specs=[tok, pl.BlockSpec((8, D), lambda i: (0, 0)), pl.BlockSpec((4, GROUP_W, cs), lambda i: (0, 0, 0))],
        out_specs=[tok] + [_dil_spec(tm, d) for d in DILATIONS],
        out_shape=[jax.ShapeDtypeStruct((S, D), BF)] + [jax.ShapeDtypeStruct((S // d, d * GROUP_W), F32) for d in DILATIONS],
        scratch_shapes=[pltpu.VMEM((GROUP_CH, tm, LANES), F32)],
        compiler_params=_cparams(1),
    )(dxo, vec, wo4)
    return dy, dms


def _attn_bwd(q, k, v, dmix, mix, lj, d):
    sd = q.shape[0]
    nb = sd // BAND

    def body(q_ref, kp_ref, kc_ref, vp_ref, vc_ref, do_ref, o_ref, l_ref, dq_ref, dk_ref, dv_ref, keep_k, keep_v):
        i = pl.program_id(1)

        @pl.when(i == 0)
        def _():
            keep_k[...] = jnp.zeros_like(keep_k)
            keep_v[...] = jnp.zeros_like(keep_v)

        @pl.when(i < nb)
        def _():
            same, prev = _band_masks()
            prev = jnp.logical_and(prev, i > 0)
            heads = range(HEADS_PER_GROUP)
            dob = [_head(do_ref, h).astype(BF) for h in heads]
            raw = [(_dot_nt(_head(q_ref, h), _head(kc_ref, h)), _dot_nt(_head(q_ref, h), _head(kp_ref, h)),
                    _dot_nt(dob[h], _head(vc_ref, h)), _dot_nt(dob[h], _head(vp_ref, h))) for h in heads]
            mid = []
            for h, (sc, sp, dpc, dpp) in zip(heads, raw):
                lrow = l_ref[:, h * HEAD_DIM:h * HEAD_DIM + 1]
                delta = jnp.sum(_head(do_ref, h) * _head(o_ref, h), axis=-1, keepdims=True)
                pc = jnp.exp(jnp.where(same, sc, NEG) - lrow)
                pp = jnp.exp(jnp.where(prev, sp, NEG) - lrow)
                mid.append((pc.astype(BF), pp.astype(BF), (pc * (dpc - delta)).astype(BF), (pp * (dpp - delta)).astype(BF)))
            dq = [(_dot(dsc, _head(kc_ref, h)) + _dot(dsp, _head(kp_ref, h))) * SM_SCALE for h, (_, _, dsc, dsp) in zip(heads, mid)]
            dk_prev = [_dot_tn(dsp, _head(q_ref, h)) for h, (_, _, _, dsp) in zip(heads, mid)]
            dv_prev = [_dot_tn(pp, dob[h]) for h, (_, pp, _, _) in zip(heads, mid)]
            dk_same = [_dot_tn(dsc, _head(q_ref, h)) for h, (_, _, dsc, _) in zip(heads, mid)]
            dv_same = [_dot_tn(pc, dob[h]) for h, (pc, _, _, _) in zip(heads, mid)]
            dq_ref[...] = jnp.concatenate(dq, axis=1)
            dk_ref[...] = keep_k[...] + jnp.concatenate(dk_prev, axis=1)
            dv_ref[...] = keep_v[...] + jnp.concatenate(dv_prev, axis=1)
            keep_k[...] = jnp.concatenate(dk_same, axis=1)
            keep_v[...] = jnp.concatenate(dv_same, axis=1)

        @pl.when(i == nb)
        def _():
            dk_ref[...] = keep_k[...]
            dv_ref[...] = keep_v[...]

    blk = (BAND, GROUP_W)
    cur = pl.BlockSpec(blk, lambda r, i: (jnp.minimum(i, nb - 1), r))
    prv = pl.BlockSpec(blk, lambda r, i: (jnp.clip(i - 1, 0, nb - 1), r))
    return pl.pallas_call(
        body, name=f"attn_bwd_d{d}", grid=(d, nb + 1),
        in_specs=[cur, prv, cur, prv, cur, cur, cur, cur], out_specs=[cur, prv, prv],
        out_shape=[jax.ShapeDtypeStruct((sd, d * GROUP_W), F32)] * 3,
        scratch_shapes=[pltpu.VMEM(blk, F32), pltpu.VMEM(blk, F32)],
        compiler_params=_cparams(2),
    )(q, k, k, v, v, dmix, mix, lj)


def _final_loss(x, gvec, tgt):
    S, D = x.shape
    tm = _tile(S, TM)

    def body(x_ref, g_ref, t_ref, dx_ref, red_ref):
        @pl.when(pl.program_id(0) == 0)
        def _():
            red_ref[...] = jnp.zeros_like(red_ref)

        xv = x_ref[...]
        g = g_ref[0:1, :]
        r = _rstd(xv)
        xh = xv * r
        err = xh * g - t_ref[...]
        dy = err * (1.0 / D)
        dxh = dy * g
        dx_ref[...] = r * (dxh - xh * jnp.mean(dxh * xh, axis=-1, keepdims=True))
        red_ref[0:1, :] += jnp.sum(dy * xh, axis=0, keepdims=True)
        red_ref[1:2, :] += jnp.sum(err * err, axis=0, keepdims=True)

    tok = pl.BlockSpec((tm, D), lambda i: (i, 0))
    small = pl.BlockSpec((8, D), lambda i: (0, 0))
    return pl.pallas_call(
        body, name="final_loss", grid=(S // tm,),
        in_specs=[tok, small, tok], out_specs=[tok, small],
        out_shape=[jax.ShapeDtypeStruct((S, D), F32), jax.ShapeDtypeStruct((8, D), F32)],
        compiler_params=_cparams(1),
    )(x, gvec, tgt)


def _adamw(w, g, m, v, copy_g=False):
    shape = w.shape
    C = shape[-1]
    R = w.size // C
    tr = 256 if R % 256 == 0 else R

    def body(w_ref, g_ref, m_ref, v_ref, d_ref, nm_ref, nv_ref, *g_out):
        gv = g_ref[...]
        nm = ADAM_B1 * m_ref[...] + (1.0 - ADAM_B1) * gv
        nv = ADAM_B2 * v_ref[...] + (1.0 - ADAM_B2) * (gv * gv)
        m_hat = nm / (1.0 - ADAM_B1 ** ADAM_STEP)
        v_hat = nv / (1.0 - ADAM_B2 ** ADAM_STEP)
        d_ref[...] = -ADAM_LR * (m_hat / (jnp.sqrt(v_hat) + ADAM_EPS) + ADAM_WD * w_ref[...])
        nm_ref[...] = nm
        nv_ref[...] = nv
        if copy_g:
            g_out[0][...] = gv

    spec = pl.BlockSpec((tr, C), lambda i: (i, 0))
    n_out = 4 if copy_g else 3
    outs = pl.pallas_call(
        body, name="adamw", grid=(R // tr,),
        in_specs=[spec] * 4, out_specs=[spec] * n_out,
        out_shape=[jax.ShapeDtypeStruct((R, C), F32)] * n_out,
        compiler_params=_cparams(1),
    )(*(t.reshape(R, C) for t in (w, g, m, v)))
    return tuple(o.reshape(shape) for o in (*outs[:3], outs[3] if copy_g else g))


def _ada_fwd(c_all, w, b):
    L, D, N = w.shape
    tn = 768 if N % 768 == 0 else _tile(N, 512)

    def body(c_ref, w_ref, b_ref, o_ref):
        cv = c_ref[...]
        cond = (cv * jax.nn.sigmoid(cv)).astype(BF)
        o_ref[...] = _dot(cond, w_ref[...].astype(BF)) + b_ref[...]

    return pl.pallas_call(
        body, name="ada_fwd", grid=(L, N // tn),
        in_specs=[pl.BlockSpec((8, D), lambda l, n: (0, 0)), pl.BlockSpec((None, D, tn), lambda l, n: (l, 0, n)),
                  pl.BlockSpec((None, 1, tn), lambda l, n: (l, 0, n))],
        out_specs=pl.BlockSpec((None, 8, tn), lambda l, n: (l, 0, n)),
        out_shape=jax.ShapeDtypeStruct((L, 8, N), F32),
        compiler_params=_cparams(2),
    )(c_all, w, b)


def _ada_wgrad(c_all_t, dm):
    D = c_all_t.shape[0]
    L, _, N = dm.shape
    tn = 256

    def body(c_ref, dm_ref, o_ref):
        cv = c_ref[...]
        cond = cv * jax.nn.sigmoid(cv)
        acc = cond[:, 0:1] * dm_ref[0:1, :]
        for b in range(1, 8):
            acc = acc + cond[:, b:b + 1] * dm_ref[b:b + 1, :]
        o_ref[...] = acc

    return pl.pallas_call(
        body, name="ada_wgrad", grid=(L, N // tn),
        in_specs=[pl.BlockSpec((D, 8), lambda l, n: (0, 0)), pl.BlockSpec((None, 8, tn), lambda l, n: (l, 0, n))],
        out_specs=pl.BlockSpec((None, D, tn), lambda l, n: (l, 0, n)),
        out_shape=jax.ShapeDtypeStruct((L, D, N), F32),
        compiler_params=_cparams(2),
    )(c_all_t, dm)


def _sum8(g):
    _, R, C = g.shape

    def body(g_ref, o_ref):
        acc = g_ref[0]
        for b in range(1, 8):
            acc = acc + g_ref[b]
        o_ref[...] = acc

    return pl.pallas_call(body, name="sum8", out_shape=jax.ShapeDtypeStruct((R, C), F32),
                          in_specs=[pl.BlockSpec(memory_space=pltpu.VMEM)],
                          out_specs=pl.BlockSpec(memory_space=pltpu.VMEM))(g)


def _pair_add(g4, recv, cidx):
    _, _, ha, B = g4.shape
    tr = _row_tile(ha)

    def body(c_ref, g_ref, r_ref, o_ref):
        o_ref[...] = (g_ref[...].astype(F32) + r_ref[...].astype(F32)).astype(BF)

    return pl.pallas_call(
        body, name="pair_add",
        grid_spec=pltpu.PrefetchScalarGridSpec(
            num_scalar_prefetch=1, grid=(4, ha // tr),
            in_specs=[pl.BlockSpec((None, None, tr, B), lambda s, i, c: (s, c[0], i, 0)),
                      pl.BlockSpec((None, tr, B), lambda s, i, c: (s, i, 0))],
            out_specs=pl.BlockSpec((None, tr, B), lambda s, i, c: (s, i, 0))),
        out_shape=jax.ShapeDtypeStruct((4, ha, B), BF),
        compiler_params=_cparams(2),
    )(cidx, g4, recv)


def _sum_partials(part, recv, gbuf, n_layers, l, place):
    _, ha, B = part.shape
    tr = _row_tile(ha, cap=256)

    def body(pc_ref, p_ref, r_ref, *rest):
        o_ref = rest[-1]
        o_ref[...] = ((p_ref[...].astype(F32) + r_ref[0].astype(F32)) + r_ref[1].astype(F32)) + r_ref[2].astype(F32)

    in_specs = [pl.BlockSpec((None, tr, B), lambda i, pc: (pc[0], i, 0)), pl.BlockSpec((3, tr, B), lambda i, pc: (0, i, 0))]
    ops = [part, recv]
    if gbuf is not None:
        in_specs.append(ANY)
        ops.append(gbuf)
    return pl.pallas_call(
        body, name="sum_partials",
        grid_spec=pltpu.PrefetchScalarGridSpec(
            num_scalar_prefetch=1, grid=(ha // tr,), in_specs=in_specs,
            out_specs=pl.BlockSpec((None, None, tr, B), lambda i, pc: (l, pc[1], i, 0))),
        out_shape=jax.ShapeDtypeStruct((n_layers, 2, ha, B), F32),
        input_output_aliases={} if gbuf is None else {3: 0},
        compiler_params=_cparams(1),
    )(place, *ops)


def _cast_shard(shard, l, place):
    _, _, ha, B = shard.shape
    tr = _row_tile(ha)

    def body(pc_ref, s_ref, o_ref):
        o_ref[...] = s_ref[...].astype(BF)

    return pl.pallas_call(
        body, name="cast_shard",
        grid_spec=pltpu.PrefetchScalarGridSpec(
            num_scalar_prefetch=1, grid=(2, ha // tr),
            in_specs=[pl.BlockSpec((None, None, tr, B), lambda h, i, pc: (l, h, i, 0))],
            out_specs=pl.BlockSpec((None, None, tr, B), lambda h, i, pc: (pc[0], h, i, 0))),
        out_shape=jax.ShapeDtypeStruct((4, 2, ha, B), BF),
        compiler_params=_cparams(2),
    )(place, shard)


def _gather8(v):
    R, C = v.shape

    def body(v_ref, o_ref, ssem, rsem):
        x, y, c, *_ = _place()
        me = 4 * x + 2 * y + c
        o_ref[me] = v_ref[...]
        cps = []
        for k in range(1, 8):
            fx, fy, fc = (k >> 2) & 1, (k >> 1) & 1, k & 1
            peer = (x ^ fx, y ^ fy, c ^ fc)
            cp = pltpu.make_async_remote_copy(src_ref=v_ref, dst_ref=o_ref.at[me], send_sem=ssem.at[k - 1],
                                              recv_sem=rsem.at[k - 1], device_id=peer, device_id_type=MESH)
            cp.start()
            cps.append((cp, 4 * peer[0] + 2 * peer[1] + peer[2]))
        for k, (cp, slot) in enumerate(cps):
            there = o_ref.at[slot]
            pltpu.make_async_remote_copy(src_ref=there, dst_ref=there, send_sem=ssem.at[k], recv_sem=rsem.at[k],
                                         device_id=(x, y, c), device_id_type=MESH).wait_recv()
        for cp, _ in cps:
            cp.wait_send()

    vm = pl.BlockSpec(memory_space=pltpu.VMEM)
    return pl.pallas_call(
        body, name="gather8", in_specs=[vm], out_specs=vm, out_shape=jax.ShapeDtypeStruct((8, R, C), F32),
        scratch_shapes=[pltpu.SemaphoreType.DMA((7,)), pltpu.SemaphoreType.DMA((7,))],
    )(v)


def _mods_to_owner(ms):
    _, R, C = ms.shape

    def body(m_ref, o_ref, ssem, rsem):
        x, y, c, p, chips, qs = _place()
        o_ref[p] = m_ref[2 * p + c]
        cps = []
        for j in range(3):
            cp = pltpu.make_async_remote_copy(src_ref=m_ref.at[2 * qs[j] + c], dst_ref=o_ref.at[p], send_sem=ssem.at[j],
                                              recv_sem=rsem.at[j], device_id=(*chips[j], c), device_id_type=MESH)
            cp.start()
            cps.append(cp)
        for j in range(3):
            there = o_ref.at[qs[j]]
            pltpu.make_async_remote_copy(src_ref=there, dst_ref=there, send_sem=ssem.at[j], recv_sem=rsem.at[j],
                                         device_id=(x, y, c), device_id_type=MESH).wait_recv()
        for cp in cps:
            cp.wait_send()

    vm = pl.BlockSpec(memory_space=pltpu.VMEM)
    return pl.pallas_call(
        body, name="mods_to_owner", in_specs=[vm], out_specs=vm, out_shape=jax.ShapeDtypeStruct((4, R, C), F32),
        scratch_shapes=[pltpu.SemaphoreType.DMA((3,)), pltpu.SemaphoreType.DMA((3,))],
    )(ms)


def _vec(*rows):
    D = rows[0].shape[-1]
    rows = [r.reshape(1, D) for r in rows]
    return jnp.concatenate(rows + [jnp.zeros((8 - len(rows), D), F32)], axis=0)


def _sets(n_layers, n_conv):
    sets = []
    for l in range(n_layers):
        j = l - n_conv
        mixer = [("conv_w_in", l), ("conv_w_out", l)] if l < n_conv else [("attn_w_q", j), ("attn_w_o", j)]
        sets.append([("ffn1_w_in", l), ("ffn1_w_out", l)] + ([("w_kv", 0)] if l == n_conv else []))
        sets.append(mixer + [("ffn2_w_in", l), ("ffn2_w_out", l)])
    return sets


def _as_weight(name, buf):
    _, _, ha, B = buf.shape
    return buf.reshape(8 * ha, B) if name.endswith("w_out") else buf.reshape(4, 2 * ha, B)


def _local_step(x, tgt, tabs, norm_g, conv_w, kv_norm_g, final_norm_g, mods, kvmods, cast, place):
    S, D = x.shape
    n_layers, n_conv = norm_g.shape[0], conv_w.shape[0]
    md = mods.reshape(n_layers, 9, D)

    def vec_of(l, k):
        return _vec(norm_g[l, k], md[l, 3 * k + 1], md[l, 3 * k], md[l, 3 * k + 2])

    W = {}

    def install(items, bufs):
        for (name, idx), b in zip(items, bufs):
            W[name, idx] = _as_weight(name, b)

    Q = _sets(n_layers, n_conv)
    assert n_conv >= 1, "the first layer's conv mixer carries a weight set"
    install(Q[0], _comm_call("gather_d2d", _comm_call("gather_ici", [cast[it] for it in Q[0]])))
    arrived = {}

    def ffn_fwd(n, *args):
        ici = [k for k in ([1] if n == 0 else [n + 2]) if k < len(Q)]
        d2d = [n + 1] if 2 <= n + 1 < len(Q) else []
        comm = [("gather_ici", [cast[it] for it in Q[k]]) for k in ici] + [("gather_d2d", arrived.pop(k)) for k in d2d]
        xo, h, ab, y, *moved = _ffn_fwd(*args, comm)
        for k in ici:
            arrived[k], moved = moved[:len(Q[k])], moved[len(Q[k]):]
        for k in d2d:
            install(Q[k], moved[:len(Q[k])])
            moved = moved[len(Q[k]):]
        if n == 0:
            install(Q[1], _comm_call("gather_d2d", arrived.pop(1)))
        return xo, h, ab, y

    vec_kv = _vec(kv_norm_g, kvmods[D:], kvmods[:D])
    ng = len(DILATIONS)
    n_rope = ng * GROUP_W // LANES
    saved = {}
    xc = x
    ks = vs = None
    for l in range(n_layers):
        if l == n_conv:
            h_kv, kv = _proj_fwd(xc, vec_kv, W["w_kv", 0], tabs, n_rope, DILATIONS * 2)
            ks, vs = kv[:ng], kv[ng:]
            saved["kv"] = (xc, h_kv)
        v0 = vec_of(l, 0)
        xo, h, ab, y = ffn_fwd(2 * l, xc, v0, W["ffn1_w_in", l], W["ffn1_w_out", l])
        saved[l, 0] = (xc, v0, h, ab, y)
        xc = xo
        v1 = vec_of(l, 1)
        if l < n_conv:
            cw = _vec(conv_w[l, 0], conv_w[l, 1], conv_w[l, 2])
            early = [("gather_ici", [cast[it] for it in Q[2]])] if l == 0 and len(Q) > 2 else None
            xo, h, s4, z, y, *moved = _conv_fwd(xc, v1, cw, W["conv_w_in", l], W["conv_w_out", l], early)
            if early:
                arrived[2] = moved
            saved[l, 1] = (xc, v1, h, y, cw, s4, z)
        else:
            j = l - n_conv
            h, qs = _proj_fwd(xc, v1, W["attn_w_q", j], tabs, n_rope, DILATIONS, SM_SCALE)
            og = [_attn_fwd(qs[g], ks[g], vs[g], DILATIONS[g]) for g in range(ng)]
            mixs, ljs, y, xo = _attn_out([o for o, _ in og], [s for _, s in og], xc, v1, W["attn_w_o", j])
            saved[l, 1] = (xc, v1, h, y, qs, mixs, ljs)
        xc = xo
        v2 = vec_of(l, 2)
        xo, h, ab, y = ffn_fwd(2 * l + 1, xc, v2, W["ffn2_w_in", l], W["ffn2_w_out", l])
        saved[l, 2] = (xc, v2, h, ab, y)
        xc = xo

    dx, red_f = _final_loss(xc, _vec(final_norm_g), tgt)
    loss_part = 0.5 * jnp.sum(red_f[1]) / D
    G, parts, others = {}, {}, {}
    dmods = [[None] * 9 for _ in range(n_layers)]
    dnorm = [[None] * 3 for _ in range(n_layers)]
    dconvw = [None] * n_conv
    dkv_parts = []

    def halves(items):
        return [G[it].reshape(4, 2, G[it].shape[1] // 2, G[it].shape[2]) for it in items]

    def pair_sums(items, recv):
        for it, g, r in zip(items, halves(items), recv):
            parts[it] = _pair_add(g, r, place[1:])
        return [parts[it] for it in items]

    R = Q[::-1]
    recvd = {}

    def ffn_bwd(dx, l, k, name):
        m = 2 * (n_layers - 1 - l) + (0 if k == 2 else 1)
        ex = [m - 1] if m >= 1 else []
        sc = [m - 2] if m >= 2 else []
        comm = [("exchange", halves(R[e])) for e in ex] + [("scatter", pair_sums(R[e], recvd.pop(e))) for e in sc]
        xin, v, h, ab, y = saved[l, k]
        w_in, w_out = W[name + "_w_in", l], W[name + "_w_out", l]
        dy, u, dab, dx, red, *moved = _ffn_dgrad(dx, v, ab, w_in, w_out, xin, y, comm)
        for e in ex:
            recvd[e], moved = moved[:len(R[e])], moved[len(R[e]):]
        for e in sc:
            others.update(zip(R[e], moved[:len(R[e])]))
            moved = moved[len(R[e]):]
        if m == len(R) - 1:
            G[name + "_w_in", l], got = _mm_tn(h, dab, w_in.shape[2], [("scatter", pair_sums(R[m - 1], recvd.pop(m - 1)))])
            others.update(zip(R[m - 1], got))
        else:
            G[name + "_w_in", l] = _mm_tn(h, dab, w_in.shape[2])
        G[name + "_w_out", l] = _mm_tn(u, dy[None], D).reshape(4, -1, D)
        return dx, red

    def note(l, k, red):
        dnorm[l][k] = red[0]
        dmods[l][3 * k], dmods[l][3 * k + 1], dmods[l][3 * k + 2] = red[2], red[1], red[3]

    for l in reversed(range(n_layers)):
        dx, red = ffn_bwd(dx, l, 2, "ffn2")
        note(l, 2, red)
        if l < n_conv:
            xin, v, h, y, cw, s4, z = saved[l, 1]
            w_in, w_out = W["conv_w_in", l], W["conv_w_out", l]
            dy, d3, dx, red, dcw = _conv_dgrad(dx, v, cw, s4, w_in, w_out, xin, y)
            G["conv_w_in", l] = _mm_tn(h, d3[None], w_in.shape[2])
            G["conv_w_out", l] = _mm_tn(z, dy[None], D).reshape(4, -1, D)
            dconvw[l] = dcw[0:3]
        else:
            j = l - n_conv
            xin, v, h, y, qs, mixs, ljs = saved[l, 1]
            wq, wo = W["attn_w_q", j], W["attn_w_o", j]
            dy, dmixs = _attn_out_dgrad(dx, v, wo)
            G["attn_w_o", j] = _mm_tn(mixs[0], dy[None], wo.shape[2])
            dqkv = [_attn_bwd(qs[g], ks[g], vs[g], dmixs[g], mixs[g], ljs[g], DILATIONS[g]) for g in range(ng)]
            dkv_parts.append([t[1:] for t in dqkv])
            dq, dx, red = _proj_dgrad([[t[0]] for t in dqkv], tabs, wq, ng, DILATIONS, dx, xin, y, v)
            G["attn_w_q", j] = _mm_tn(h, dq[None], wq.shape[2])
        note(l, 1, red)
        dx, red = ffn_bwd(dx, l, 0, "ffn1")
        note(l, 0, red)
        if l == n_conv:
            xin, h_kv = saved["kv"]
            kv_parts = [[lay[g][0] for lay in dkv_parts] for g in range(ng)] + [[lay[g][1] for lay in dkv_parts] for g in range(ng)]
            dkv, dx, red_kv = _proj_dgrad(kv_parts, tabs, W["w_kv", 0], ng, DILATIONS * 2, dx, xin, None, vec_kv)
            G["w_kv", 0] = _mm_tn(h_kv, dkv[None], W["w_kv", 0].shape[2])
    others.update(zip(R[-1], _comm_call("scatter", pair_sums(R[-1], _comm_call("exchange", halves(R[-1]))))))
    small = jnp.concatenate(
        [jnp.stack([jnp.stack(r) for r in dmods]).reshape(-1), red_kv[2], red_kv[1],
         jnp.stack([jnp.stack(r) for r in dnorm]).reshape(-1), jnp.stack(dconvw).reshape(-1), red_kv[0], red_f[0]])
    return loss_part, dx, parts, others, small


BIG = ("ffn1_w_in", "ffn1_w_out", "ffn2_w_in", "ffn2_w_out", "conv_w_in", "conv_w_out", "w_kv", "attn_w_q", "attn_w_o")


def _halved(t):
    if t.ndim == 2:
        t = t[None]
    L, A, B = t.shape
    return t.reshape(L, 2, A // 2, B)


def kernel(x, c, positions, norm_g, ada_w, ada_b, ffn1_w_in, ffn1_w_out, ffn2_w_in, ffn2_w_out, conv_w_in, conv_w, conv_w_out, kv_norm_g, kv_ada_w, kv_ada_b, w_kv, attn_w_q, attn_w_o, final_norm_g, loss_target, m_norm_g, m_ada_w, m_ada_b, m_ffn1_w_in, m_ffn1_w_out, m_ffn2_w_in, m_ffn2_w_out, m_conv_w_in, m_conv_w, m_conv_w_out, m_kv_norm_g, m_kv_ada_w, m_kv_ada_b, m_w_kv, m_attn_w_q, m_attn_w_o, m_final_norm_g, v_norm_g, v_ada_w, v_ada_b, v_ffn1_w_in, v_ffn1_w_out, v_ffn2_w_in, v_ffn2_w_out, v_conv_w_in, v_conv_w, v_conv_w_out, v_kv_norm_g, v_kv_ada_w, v_kv_ada_b, v_w_kv, v_attn_w_q, v_attn_w_o, v_final_norm_g):
    wts = dict(norm_g=norm_g, ada_w=ada_w, ada_b=ada_b, ffn1_w_in=ffn1_w_in, ffn1_w_out=ffn1_w_out, ffn2_w_in=ffn2_w_in,
               ffn2_w_out=ffn2_w_out, conv_w_in=conv_w_in, conv_w=conv_w, conv_w_out=conv_w_out, kv_norm_g=kv_norm_g,
               kv_ada_w=kv_ada_w, kv_ada_b=kv_ada_b, w_kv=w_kv, attn_w_q=attn_w_q, attn_w_o=attn_w_o,
               final_norm_g=final_norm_g)
    ms = dict(norm_g=m_norm_g, ada_w=m_ada_w, ada_b=m_ada_b, ffn1_w_in=m_ffn1_w_in, ffn1_w_out=m_ffn1_w_out,
              ffn2_w_in=m_ffn2_w_in, ffn2_w_out=m_ffn2_w_out, conv_w_in=m_conv_w_in, conv_w=m_conv_w,
              conv_w_out=m_conv_w_out, kv_norm_g=m_kv_norm_g, kv_ada_w=m_kv_ada_w, kv_ada_b=m_kv_ada_b, w_kv=m_w_kv,
              attn_w_q=m_attn_w_q, attn_w_o=m_attn_w_o, final_norm_g=m_final_norm_g)
    vs = dict(norm_g=v_norm_g, ada_w=v_ada_w, ada_b=v_ada_b, ffn1_w_in=v_ffn1_w_in, ffn1_w_out=v_ffn1_w_out,
              ffn2_w_in=v_ffn2_w_in, ffn2_w_out=v_ffn2_w_out, conv_w_in=v_conv_w_in, conv_w=v_conv_w,
              conv_w_out=v_conv_w_out, kv_norm_g=v_kv_norm_g, kv_ada_w=v_kv_ada_w, kv_ada_b=v_kv_ada_b, w_kv=v_w_kv,
              attn_w_q=v_attn_w_q, attn_w_o=v_attn_w_o, final_norm_g=v_final_norm_g)
    order = list(wts)
    S, D = x.shape[1], x.shape[2]
    n_layers, n_conv = norm_g.shape[0], conv_w.shape[0]
    nm, sw = ada_w.shape[2], norm_g.shape[2]
    ix, iy, ic = (lax.axis_index(a) for a in AXES)
    chip = 2 * ix + iy

    pack = jnp.concatenate([norm_g.reshape(-1), conv_w.reshape(-1), c.reshape(-1)])
    npad = (-pack.size) % (8 * LANES)
    allp = _gather8(jnp.pad(pack, (0, npad)).reshape(-1, LANES)).reshape(8, -1)
    n1, n2 = norm_g.size, norm_g.size + conv_w.size
    by_chip = allp[0::2]
    norm_full = jnp.moveaxis(by_chip[:, :n1].reshape(4, n_layers, 3, sw), 0, 2).reshape(n_layers, 3, 4 * sw)
    conv_full = jnp.moveaxis(by_chip[:, n1:n2].reshape(4, n_conv, 3, sw), 0, 2).reshape(n_conv, 3, 4 * sw)
    c_all = allp[:, n2:n2 + D]

    b_l = lax.dynamic_slice_in_dim(ada_b, chip * nm, nm, axis=1)
    mod_sh = _ada_fwd(c_all, ada_w, b_l[:, None, :])
    nkv = kv_ada_w.shape[1]
    bkv = lax.dynamic_slice_in_dim(kv_ada_b, chip * nkv, nkv, axis=0)
    kv_sh = _ada_fwd(c_all, kv_ada_w[None], bkv[None, None, :])
    rows = jnp.concatenate([jnp.moveaxis(mod_sh, 0, 1), jnp.pad(jnp.moveaxis(kv_sh, 0, 1), ((0, 0), (0, 0), (0, nm - nkv)))], axis=1)
    rpad = (-rows.shape[1]) % 8
    mine = _mods_to_owner(jnp.pad(rows, ((0, 0), (0, rpad), (0, 0))))
    mods = jnp.moveaxis(mine[:, :n_layers], 0, 1).reshape(n_layers, 4 * nm)
    kvmods = mine[:, n_layers, :nkv].reshape(4 * nkv)

    place = jnp.stack([chip, ic]).astype(jnp.int32)
    halved = {k: _halved(wts[k]) for k in BIG}
    cast = {(k, l): _cast_shard(halved[k], l, place) for k in BIG for l in range(halved[k].shape[0])}

    loss_part, dx, parts, others, small = _local_step(x[0], loss_target[0], _rope_tables(positions), norm_full, conv_full,
                                                      kv_norm_g, final_norm_g, mods, kvmods, cast, place)
    loss = lax.psum(loss_part, AXES)

    gbufs = []
    for k in BIG:
        gbuf, n_l = None, halved[k].shape[0]
        for l in range(n_l):
            gbuf = _sum_partials(parts[k, l], others[k, l], gbuf, n_l, l, place)
        gbufs.append(gbuf)
    grads = {}

    spad = (-small.size) % (8 * LANES)
    gath = _gather8(jnp.pad(small, (0, spad)).reshape(-1, LANES))
    tot = _sum8(gath).reshape(-1)
    gath = gath.reshape(8, -1)
    o = 0

    def take(n):
        nonlocal o
        o += n
        return tot[o - n:o]

    g_mods = take(n_layers * 9 * D).reshape(n_layers, 9 * D)
    g_kvmods = take(2 * D)
    g_norm = take(n_layers * 3 * D).reshape(n_layers, 3, D)
    g_convw = take(n_conv * 3 * D).reshape(n_conv, 3, D)
    g_kvn = take(D)
    g_fin = take(D)
    grads["ada_b"] = g_mods
    grads["kv_ada_b"] = g_kvmods
    grads["norm_g"] = lax.dynamic_slice_in_dim(g_norm, chip * sw, sw, axis=2)
    grads["conv_w"] = lax.dynamic_slice_in_dim(g_convw, chip * sw, sw, axis=2)
    grads["kv_norm_g"] = g_kvn
    grads["final_norm_g"] = g_fin

    dm_all = gath[:, :n_layers * 9 * D].reshape(8, n_layers, 9 * D)
    dm_mine = jnp.moveaxis(lax.dynamic_slice_in_dim(dm_all, chip * nm, nm, axis=2), 0, 1)
    dkv_all = gath[:, n_layers * 9 * D:n_layers * 9 * D + 2 * D]
    dkv_mine = lax.dynamic_slice_in_dim(dkv_all, chip * nkv, nkv, axis=1)[None]
    c_t = c_all.T
    grads["ada_w"] = _ada_wgrad(c_t, dm_mine)
    grads["kv_ada_w"] = _ada_wgrad(c_t, dkv_mine)[0]

    grads.update({k: s.reshape(wts[k].shape) for k, s in zip(BIG, _comm_call("share", gbufs))})
    deltas, new_m, new_v = {}, {}, {}
    for k in order:
        shp = wts[k].shape
        two_d = (lambda t: t.reshape(1, -1)) if len(shp) == 1 else (lambda t: t)
        dlt, nmk, nvk, gk = _adamw(two_d(wts[k]), two_d(grads[k]), two_d(ms[k]), two_d(vs[k]), k in BIG)
        deltas[k], new_m[k], new_v[k] = dlt.reshape(shp), nmk.reshape(shp), nvk.reshape(shp)
        grads[k] = gk.reshape(shp)
    return (loss, dx[None], *[grads[k] for k in order], *[deltas[k] for k in order], *[new_m[k] for k in order],
            *[new_v[k] for k in order])
```
